```python
import math
import jax
import jax.numpy as jnp
from jax import lax
import numpy as np


D_MODEL = 2048
BATCH = 2
SEQ = 4096
DEPTH = 1

A_HEADS = 8
A_DQK = 64
A_DV = 2 * A_DQK
B_HEADS = 16
B_GROUPS = 4
B_HPG = B_HEADS // B_GROUPS
B_DH = 64
CMP_LEN = 32
CMP_STRIDE = 16
CMP_HIDDEN = 256
SEL_BLOCK = 64
SEL_TOPN = 16
WINDOW = 512
N_NSA_BRANCH = 3
MEM_TOKENS = 256
M_HEADS = 4
M_DH = 256
REL_BUCKETS = 32
REL_MAX_DIST = 1024
PEER_HEADS = 8
PEER_NKEYS = 128
PEER_DKEY = 256
PEER_TOPK = 16
PEER_EXPERTS = PEER_NKEYS * PEER_NKEYS
N_BRANCH = 3
MIX_W = 1024
IN_WIDTHS = (A_HEADS * 2 * A_DQK, A_HEADS * 2 * A_DQK, A_HEADS * A_DV, B_HEADS * B_DH, N_NSA_BRANCH * 2 * B_GROUPS * B_DH, N_NSA_BRANCH * B_HEADS, M_HEADS * M_DH, N_BRANCH * D_MODEL)
C_IN = sum(IN_WIDTHS)
Q_BLOCK = 128
PEER_CHUNK = 128
LN_EPS = 1e-5
FORCE = 1e9
ALPHA = (2 * DEPTH) ** 0.25
BETA = (8 * DEPTH) ** -0.25

kernel_name = 'hybrid_diffattn_nsa_mem_peer_deepnorm'


def lambda_init(layer_idx):
    return 0.8 - 0.6 * math.exp(-0.3 * layer_idx)


def layer_norm(x, g, b):
    xf = x.astype(jnp.float32)
    mu = jnp.mean(xf, axis=-1, keepdims=True)
    var = jnp.mean(jnp.square(xf - mu), axis=-1, keepdims=True)
    return ((xf - mu) * lax.rsqrt(var + LN_EPS) * g + b).astype(x.dtype)


def rms_norm(x, g):
    xf = x.astype(jnp.float32)
    return (xf * lax.rsqrt(jnp.mean(jnp.square(xf), axis=-1, keepdims=True) + LN_EPS) * g).astype(x.dtype)


def masked_softmax(s, mask):
    s = jnp.where(mask, s.astype(jnp.float32), -jnp.inf)
    m = jnp.max(s, axis=-1, keepdims=True)
    m = jnp.where(jnp.isfinite(m), m, 0.0)
    e = jnp.where(mask, jnp.exp(s - m), 0.0)
    return e / jnp.maximum(jnp.sum(e, axis=-1, keepdims=True), 1e-30)


def rel_bucket(dist):
    n = jnp.maximum(dist, 0)
    max_exact = REL_BUCKETS // 2
    nf = jnp.maximum(n, 1).astype(jnp.float32)
    large = max_exact + (jnp.log(nf / max_exact) / math.log(REL_MAX_DIST / max_exact) * (REL_BUCKETS - max_exact)).astype(jnp.int32)
    large = jnp.minimum(large, REL_BUCKETS - 1)
    return jnp.where(n < max_exact, n, large)


def group_bias(tab_g, bkt):
    return jnp.moveaxis(tab_g[:, bkt], 0, 1)


def gather_blocks(blocks, idx):
    g = jax.vmap(jax.vmap(lambda kb, i: kb[i]))(blocks, idx)
    return g.reshape(idx.shape[0], idx.shape[1], idx.shape[2], -1, blocks.shape[-1])


def diff_attention(q, k, v, lam_params, subln_g, rel_tab, lam_init):
    B, T = q.shape[:2]
    nb = T // Q_BLOCK
    qh = q.transpose(0, 2, 3, 1, 4)
    kh = k.transpose(0, 2, 3, 1, 4)
    vh = v.transpose(0, 2, 1, 3)
    q_blocks = jnp.moveaxis(qh.reshape(B, A_HEADS, 2, nb, Q_BLOCK, A_DQK), 3, 0)
    lp = lam_params.astype(jnp.float32)
    lam = jnp.exp(jnp.sum(lp[0] * lp[1])) - jnp.exp(jnp.sum(lp[2] * lp[3])) + lam_init
    kpos = jnp.arange(T)
    scale = A_DQK ** -0.5

    def block(args):
        q_blk, i = args
        qpos = i * Q_BLOCK + jnp.arange(Q_BLOCK)
        dist = qpos[:, None] - kpos[None, :]
        bias = jnp.moveaxis(rel_tab[rel_bucket(dist)], -1, 0)
        s = jnp.einsum('bhmqd,bhmkd->bhmqk', q_blk, kh).astype(jnp.float32) * scale + bias[:, None]
        p = masked_softmax(s, dist >= 0)
        w = p[:, :, 0] - lam * p[:, :, 1]
        return jnp.einsum('bhqk,bhkd->bhqd', w.astype(vh.dtype), vh)

    o = lax.map(block, (q_blocks, jnp.arange(nb)))
    o = jnp.moveaxis(o, 0, 2).reshape(B, A_HEADS, T, A_DV).transpose(0, 2, 1, 3)
    o = rms_norm(o, subln_g) * (1.0 - lam_init)
    return o.reshape(B, T, A_HEADS * A_DV)


def nsa_attention(q, kv, gate_logits, cmp_pe, cmp_w1, cmp_w2, rel_tab):
    B, T = q.shape[:2]
    nb = T // Q_BLOCK
    n_sel = T // SEL_BLOCK
    n_top = min(SEL_TOPN, n_sel)
    n_c = (T - CMP_LEN) // CMP_STRIDE + 1
    scale = B_DH ** -0.5
    tpos = jnp.arange(T)
    qh = q.reshape(B, T, B_GROUPS, B_HPG, B_DH).transpose(0, 2, 3, 1, 4)
    kv = kv.transpose(2, 3, 0, 4, 1, 5)

    cidx = jnp.arange(n_c)[:, None] * CMP_STRIDE + jnp.arange(CMP_LEN)[None, :]

    def compress(raw, pe, w1, w2):
        blk = raw[:, :, cidx] + pe
        return jax.nn.gelu(blk.reshape(B, B_GROUPS, n_c, CMP_LEN * B_DH) @ w1) @ w2

    k_cmp = compress(kv[0, 0], cmp_pe[0], cmp_w1[0], cmp_w2[0])
    v_cmp = compress(kv[0, 1], cmp_pe[1], cmp_w1[1], cmp_w2[1])
    c_end = jnp.arange(n_c) * CMP_STRIDE + CMP_LEN - 1
    s_c = jnp.einsum('bghtd,bgcd->bghtc', qh, k_cmp).astype(jnp.float32) * scale
    p_c = masked_softmax(s_c, c_end[None, :] <= tpos[:, None])
    o_cmp = jnp.einsum('bghtc,bgcd->bghtd', p_c.astype(v_cmp.dtype), v_cmp)

    overlap = jax.nn.one_hot(cidx // SEL_BLOCK, n_sel, dtype=jnp.float32).mean(axis=1)
    imp = jnp.einsum('bghtc,cj->bgtj', p_c, overlap)
    blk_ids = jnp.arange(n_sel)[None, :]
    cur = (tpos // SEL_BLOCK)[:, None]
    forced = (blk_ids == 0) | (blk_ids == cur) | (blk_ids == cur - 1)
    future = blk_ids * SEL_BLOCK > tpos[:, None]
    imp = jnp.where(forced, FORCE, jnp.where(future, -FORCE, imp))
    _, sel_idx = lax.top_k(imp, n_top)

    ks_blocks = kv[1, 0].reshape(B, B_GROUPS, n_sel, SEL_BLOCK, B_DH)
    vs_blocks = kv[1, 1].reshape(B, B_GROUPS, n_sel, SEL_BLOCK, B_DH)
    pad = ((0, 0), (0, 0), (WINDOW, 0), (0, 0))
    k_wp = jnp.pad(kv[2, 0], pad)
    v_wp = jnp.pad(kv[2, 1], pad)
    tab_g = rel_tab.T.reshape(B_GROUPS, B_HPG, REL_BUCKETS)

    q_blocks = jnp.moveaxis(qh.reshape(B, B_GROUPS, B_HPG, nb, Q_BLOCK, B_DH), 3, 0)
    sel_blocks = jnp.moveaxis(sel_idx.reshape(B, B_GROUPS, nb, Q_BLOCK, n_top), 2, 0)

    def block(args):
        q_blk, sel_b, i = args
        q0 = i * Q_BLOCK
        qpos = q0 + jnp.arange(Q_BLOCK)
        kg = gather_blocks(ks_blocks, sel_b)
        vg = gather_blocks(vs_blocks, sel_b)
        kpos = (sel_b[..., None] * SEL_BLOCK + jnp.arange(SEL_BLOCK)).reshape(B, B_GROUPS, Q_BLOCK, -1)
        dist = qpos[:, None] - kpos
        bias = jax.vmap(group_bias, in_axes=(0, 1), out_axes=1)(tab_g, rel_bucket(dist))
        s = jnp.einsum('bghqd,bgqkd->bghqk', q_blk, kg).astype(jnp.float32) * scale + bias
        p = masked_softmax(s, (dist >= 0)[:, :, None])
        o_sel = jnp.einsum('bghqk,bgqkd->bghqd', p.astype(vg.dtype), vg)
        kw = lax.dynamic_slice_in_dim(k_wp, q0, WINDOW + Q_BLOCK, axis=2)
        vw = lax.dynamic_slice_in_dim(v_wp, q0, WINDOW + Q_BLOCK, axis=2)
        kpos_w = q0 - WINDOW + jnp.arange(WINDOW + Q_BLOCK)
        dist_w = qpos[:, None] - kpos_w[None, :]
        mask_w = (dist_w >= 0) & (dist_w < WINDOW) & (kpos_w >= 0)[None, :]
        bias_w = jnp.moveaxis(rel_tab[rel_bucket(dist_w)], -1, 0).reshape(B_GROUPS, B_HPG, Q_BLOCK, -1)
        s_w = jnp.einsum('bghqd,bgkd->bghqk', q_blk, kw).astype(jnp.float32) * scale + bias_w
        p_w = masked_softmax(s_w, mask_w)
        o_win = jnp.einsum('bghqk,bgkd->bghqd', p_w.astype(vw.dtype), vw)
        return o_sel, o_win

    o_sel, o_win = lax.map(block, (q_blocks, sel_blocks, jnp.arange(nb)))

    def unblock(o):
        return jnp.moveaxis(o, 0, 3).reshape(B, B_GROUPS, B_HPG, T, B_DH)

    g = jax.nn.sigmoid(gate_logits.astype(jnp.float32)).astype(q.dtype)
    g = g.reshape(B, T, N_NSA_BRANCH, B_GROUPS, B_HPG).transpose(2, 0, 3, 4, 1)[..., None]
    o = g[0] * o_cmp + g[1] * unblock(o_sel) + g[2] * unblock(o_win)
    return o.transpose(0, 3, 1, 2, 4).reshape(B, T, B_HEADS * B_DH)


def memory_attention(q, mem, w_kv):
    B, T = q.shape[:2]
    qh = q.reshape(B, T, M_HEADS, M_DH)
    kv = (mem @ w_kv).reshape(B, mem.shape[1], 2, M_HEADS, M_DH)
    s = jnp.einsum('bthd,bmhd->bhtm', qh, kv[:, :, 0]).astype(jnp.float32) * (M_DH ** -0.5)
    p = jax.nn.softmax(s, axis=-1)
    o = jnp.einsum('bhtm,bmhd->bthd', p.astype(kv.dtype), kv[:, :, 1])
    return o.reshape(B, T, M_HEADS * M_DH)


def peer_ffn(x, w_q, sub_keys, u_tab, v_tab):
    B, T, D = x.shape
    n_tok = B * T
    xf = x.reshape(n_tok, D)
    q = (xf @ w_q).reshape(n_tok, PEER_HEADS, 2, PEER_DKEY // 2)
    s = jnp.einsum('nhcd,hckd->nhck', q, sub_keys).astype(jnp.float32)
    s_top, i_top = lax.top_k(s, PEER_TOPK)
    comb = (s_top[:, :, 0, :, None] + s_top[:, :, 1, None, :]).reshape(n_tok, PEER_HEADS, -1)
    cand = (i_top[:, :, 0, :, None] * PEER_NKEYS + i_top[:, :, 1, None, :]).reshape(n_tok, PEER_HEADS, -1)
    s_fin, pos = lax.top_k(comb, PEER_TOPK)
    experts = jnp.take_along_axis(cand, pos, axis=-1)
    gate = jax.nn.softmax(s_fin, axis=-1)
    nc = n_tok // PEER_CHUNK

    def chunk(args):
        xc, ec, gc = args
        hid = jax.nn.gelu(jnp.einsum('cd,chkd->chk', xc, u_tab[ec]).astype(jnp.float32))
        return jnp.einsum('chk,chkd->cd', (gc * hid).astype(v_tab.dtype), v_tab[ec])

    out = lax.map(chunk, (xf.reshape(nc, PEER_CHUNK, D), experts.reshape(nc, PEER_CHUNK, PEER_HEADS, PEER_TOPK), gate.reshape(nc, PEER_CHUNK, PEER_HEADS, PEER_TOPK)))
    return out.reshape(B, T, D)


def setup_inputs(seed: int = 0) -> dict:
    key = jax.random.key(seed)
    ks = jax.random.split(key, 20)
    f32 = jnp.float32
    nrm = lambda k, shape, sc: jax.random.normal(k, shape, f32) * sc
    return {
        'x': nrm(ks[0], (BATCH, SEQ, D_MODEL), 1.0),
        'mem': nrm(ks[1], (BATCH, MEM_TOKENS, D_MODEL), 1.0),
        'w_in': nrm(ks[2], (DEPTH, D_MODEL, C_IN), D_MODEL ** -0.5),
        'diff_lambda': nrm(ks[3], (DEPTH, 4, A_DQK), 0.1),
        'diff_subln': 1.0 + nrm(ks[4], (DEPTH, A_DV), 0.02),
        'cmp_pe': nrm(ks[5], (DEPTH, 2, CMP_LEN, B_DH), 0.1),
        'cmp_w1': nrm(ks[6], (DEPTH, 2, CMP_LEN * B_DH, CMP_HIDDEN), (CMP_LEN * B_DH) ** -0.5),
        'cmp_w2': nrm(ks[7], (DEPTH, 2, CMP_HIDDEN, B_DH), CMP_HIDDEN ** -0.5),
        'w_mem_kv': nrm(ks[8], (DEPTH, D_MODEL, 2 * M_HEADS * M_DH), D_MODEL ** -0.5),
        'w_branch': nrm(ks[9], (DEPTH, N_BRANCH, MIX_W, D_MODEL), MIX_W ** -0.5),
        'w_out': nrm(ks[10], (DEPTH, D_MODEL, D_MODEL), BETA * D_MODEL ** -0.5),
        'ln1_g': 1.0 + nrm(ks[11], (DEPTH, D_MODEL), 0.02),
        'ln1_b': nrm(ks[12], (DEPTH, D_MODEL), 0.02),
        'peer_wq': nrm(ks[13], (DEPTH, D_MODEL, PEER_HEADS * PEER_DKEY), D_MODEL ** -0.5),
        'peer_keys': nrm(ks[14], (DEPTH, PEER_HEADS, 2, PEER_NKEYS, PEER_DKEY // 2), (PEER_DKEY // 2) ** -0.5),
        'peer_u': nrm(ks[15], (DEPTH, PEER_EXPERTS, D_MODEL), D_MODEL ** -0.5),
        'peer_v': nrm(ks[16], (DEPTH, PEER_EXPERTS, D_MODEL), BETA),
        'ln2_g': 1.0 + nrm(ks[17], (DEPTH, D_MODEL), 0.02),
        'ln2_b': nrm(ks[18], (DEPTH, D_MODEL), 0.02),
        'rel_bias': nrm(ks[19], (REL_BUCKETS, A_HEADS + B_HEADS), 0.3),
    }


def reference(x, mem, w_in, diff_lambda, diff_subln, cmp_pe, cmp_w1, cmp_w2, w_mem_kv, w_branch, w_out, ln1_g, ln1_b, peer_wq, peer_keys, peer_u, peer_v, ln2_g, ln2_b, rel_bias):
    split_at = np.cumsum(IN_WIDTHS)[:-1].tolist()
    for l in range(DEPTH):
        B, T, _ = x.shape
        h = x @ w_in[l]
        a_q, a_k, a_v, b_q, b_kv, b_g, m_q, merge = jnp.split(h, split_at, axis=-1)
        o_a = diff_attention(a_q.reshape(B, T, A_HEADS, 2, A_DQK), a_k.reshape(B, T, A_HEADS, 2, A_DQK), a_v.reshape(B, T, A_HEADS, A_DV), diff_lambda[l], diff_subln[l], rel_bias[:, :A_HEADS], lambda_init(l))
        o_b = nsa_attention(b_q.reshape(B, T, B_HEADS, B_DH), b_kv.reshape(B, T, N_NSA_BRANCH, 2, B_GROUPS, B_DH), b_g.reshape(B, T, N_NSA_BRANCH, B_HEADS), cmp_pe[l], cmp_w1[l], cmp_w2[l], rel_bias[:, A_HEADS:])
        o_m = memory_attention(m_q, mem, w_mem_kv[l])
        branches = jnp.stack([o_a, o_b, o_m], axis=2)
        up = jnp.einsum('btnc,ncd->btnd', branches, w_branch[l])
        gates = jax.nn.sigmoid(merge.astype(jnp.float32)).astype(x.dtype).reshape(B, T, N_BRANCH, D_MODEL)
        y = jnp.sum(gates * up, axis=2) @ w_out[l]
        x = layer_norm(ALPHA * x + y, ln1_g[l], ln1_b[l])
        x = layer_norm(ALPHA * x + peer_ffn(x, peer_wq[l], peer_keys[l], peer_u[l], peer_v[l]), ln2_g[l], ln2_b[l])
    return x
```

```python
import functools
import math

import numpy as np
import jax
import jax.numpy as jnp
from jax import lax
from jax.experimental import pallas as pl
from jax.experimental.pallas import tpu as pltpu

F32 = jnp.float32
BF16 = jnp.bfloat16

D_MODEL = 2048
A_HEADS, A_DQK, A_DV = 8, 64, 128
B_HEADS, B_GROUPS, B_HPG, B_DH = 16, 4, 4, 64
CMP_LEN, CMP_STRIDE, CMP_HIDDEN = 32, 16, 256
SEL_BLOCK, SEL_TOPN, WINDOW = 64, 16, 512
M_HEADS, M_DH = 4, 256
REL_BUCKETS, REL_MAX_DIST = 32, 1024
PEER_HEADS, PEER_NKEYS, PEER_DKEY, PEER_TOPK = 8, 128, 256, 16
MIX_W = 1024
LN_EPS = 1e-5
FORCE = 1e9
DEPTH = 1
ALPHA = (2 * DEPTH) ** 0.25
LAMBDA_INIT = 0.8 - 0.6 * math.exp(-0.3 * 0)

NEG = -1e30
ATT_TILE = 256
VMEM_LIMIT = 56 * 1024 * 1024


def _cparams(sem):
    return pltpu.CompilerParams(dimension_semantics=sem, vmem_limit_bytes=VMEM_LIMIT)


def _dot_nt(a, b):
    return lax.dot_general(a, b, (((1,), (1,)), ((), ())), preferred_element_type=F32)


def _mm_body(x_ref, w_ref, o_ref):
    o_ref[...] = jnp.dot(x_ref[...], w_ref[...], preferred_element_type=F32).astype(o_ref.dtype)


def _matmul(x, w, out_dtype, tm, tn, name):
    m, k = x.shape
    n = w.shape[1]
    return pl.pallas_call(
        _mm_body,
        grid=(m // tm, n // tn),
        in_specs=[pl.BlockSpec((tm, k), lambda i, j: (i, 0)),
                  pl.BlockSpec((k, tn), lambda i, j: (0, j))],
        out_specs=pl.BlockSpec((tm, tn), lambda i, j: (i, j)),
        out_shape=jax.ShapeDtypeStruct((m, n), out_dtype),
        compiler_params=_cparams(("parallel", "arbitrary")),
        name=name,
    )(x, w)


def _rel_bucket(dist):
    n = jnp.maximum(dist, 0)
    max_exact = REL_BUCKETS // 2
    nf = jnp.maximum(n, 1).astype(jnp.float32)
    large = max_exact + (jnp.log(nf / max_exact) / math.log(REL_MAX_DIST / max_exact)
                         * (REL_BUCKETS - max_exact)).astype(jnp.int32)
    large = jnp.minimum(large, REL_BUCKETS - 1)
    return jnp.where(n < max_exact, n, large)


def _bias_tiles(tab1d, t, n_tiles, max_dist):
    L = ATT_TILE
    m = np.arange(2 * L)
    off = np.where(m <= L, -m, 2 * L - m)
    d = np.arange(n_tiles)[:, None] * L + off[None, :]
    ok = (d >= 0) & (d < max_dist)
    rp = jnp.where(ok[None], tab1d[:, np.clip(d, 0, t - 1)], NEG)
    h = tab1d.shape[0]
    full = jnp.broadcast_to(rp[:, :, None, :], (h, n_tiles, L, 2 * L)).reshape(h, n_tiles, 2 * L * L)
    tiles = full[:, :, :L * (2 * L - 1)].reshape(h, n_tiles, L, 2 * L - 1)[:, :, :, :L]
    return jnp.transpose(tiles, (1, 0, 2, 3))


def _flash_update(s, v, m_ref, l_ref, acc_ref):
    m_prev = m_ref[...]
    m_new = jnp.maximum(m_prev, jnp.max(s, axis=1, keepdims=True))
    alpha = jnp.exp(m_prev - m_new)
    p = jnp.exp(s - m_new)
    l_ref[...] = alpha * l_ref[...] + jnp.sum(p, axis=1, keepdims=True)
    acc_ref[...] = alpha * acc_ref[...] + jnp.dot(p.astype(BF16), v, preferred_element_type=F32)
    m_ref[...] = m_new


def _diff_body(q_ref, k_ref, v_ref, bias_ref, lam_ref, g_ref, o_ref, m_sc, l_sc, acc_sc, *, n_bt):
    L = ATT_TILE
    qi = pl.program_id(2)
    m_sc[...] = jnp.full(m_sc.shape, NEG, F32)
    l_sc[...] = jnp.zeros(l_sc.shape, F32)
    acc_sc[...] = jnp.zeros(acc_sc.shape, F32)
    q = q_ref[...] * jnp.asarray(A_DQK ** -0.5, BF16)

    def body(kt, carry):
        off = pl.multiple_of(kt * L, L)
        k = k_ref[pl.ds(off, L), :]
        v = v_ref[pl.ds(off, L), :]
        bias = bias_ref[jnp.minimum(qi - kt, n_bt - 1)]
        for mp in range(2):
            s = _dot_nt(q[:, mp * A_DQK:(mp + 1) * A_DQK], k[:, mp * A_DQK:(mp + 1) * A_DQK]) + bias
            _flash_update(s, v, m_sc.at[mp], l_sc.at[mp], acc_sc.at[mp])
        return carry

    lax.fori_loop(0, qi + 1, body, 0)

    lp = lam_ref[...]
    lam = (jnp.exp(jnp.sum(lp[0:1] * lp[1:2], axis=1, keepdims=True))
           - jnp.exp(jnp.sum(lp[2:3] * lp[3:4], axis=1, keepdims=True)) + LAMBDA_INIT)
    o0 = acc_sc[0] / jnp.maximum(l_sc[0], 1e-30)
    o1 = acc_sc[1] / jnp.maximum(l_sc[1], 1e-30)
    o = o0 - lam * o1
    o = o * lax.rsqrt(jnp.mean(o * o, axis=1, keepdims=True) + LN_EPS) * g_ref[...]
    o_ref[...] = (o * (1.0 - LAMBDA_INIT)).astype(o_ref.dtype)


def _diff_attention(ha, bias, lam_params, subln, b, t):
    L = ATT_TILE
    n_bt = bias.shape[0]
    return pl.pallas_call(
        functools.partial(_diff_body, n_bt=n_bt),
        grid=(b, A_HEADS, t // L),
        in_specs=[
            pl.BlockSpec((None, L, 128), lambda bi, h, qi: (bi, qi, 8 + h)),
            pl.BlockSpec((None, t, 128), lambda bi, h, qi: (bi, 0, 16 + h)),
            pl.BlockSpec((None, t, 128), lambda bi, h, qi: (bi, 0, 24 + h)),
            pl.BlockSpec((n_bt, None, L, L), lambda bi, h, qi: (0, h, 0, 0)),
            pl.BlockSpec((4, A_DQK), lambda bi, h, qi: (0, 0)),
            pl.BlockSpec((1, A_DV), lambda bi, h, qi: (0, 0)),
        ],
        out_specs=pl.BlockSpec((None, L, 128), lambda bi, h, qi: (bi, qi, h)),
        out_shape=jax.ShapeDtypeStruct((b, t, MIX_W), F32),
        scratch_shapes=[pltpu.VMEM((2, L, 1), F32), pltpu.VMEM((2, L, 1), F32), pltpu.VMEM((2, L, A_DV), F32)],
        compiler_params=_cparams(("parallel", "parallel", "arbitrary")),
        name="diff_attention",
    )(ha, ha, ha, bias, lam_params, subln)


def _compress_body(r_ref, w1_ref, w2_ref, pe_ref, o_ref):
    outs = []
    half = CMP_STRIDE * B_DH
    for kv in range(2):
        r = r_ref[kv]
        w1 = w1_ref[kv]
        a = jnp.dot(r, w1[:half], preferred_element_type=F32)
        bb = jnp.dot(r, w1[half:], preferred_element_type=F32)
        bb = jnp.concatenate([bb[1:], bb[:1]], axis=0)
        pw = jnp.dot(pe_ref[kv], w1, preferred_element_type=F32)[0:1]
        hdn = jax.nn.gelu(a + bb + pw)
        outs.append(jnp.dot(hdn.astype(BF16), w2_ref[kv], preferred_element_type=F32))
    o_ref[...] = jnp.concatenate(outs, axis=1)


def _compress(raw, w1, w2, pe):
    b, g, _, nr, wdt = raw.shape
    return pl.pallas_call(
        _compress_body,
        grid=(b, g),
        in_specs=[
            pl.BlockSpec((None, None, 2, nr, wdt), lambda bi, gi: (bi, gi, 0, 0, 0)),
            pl.BlockSpec((2, 2 * wdt, CMP_HIDDEN), lambda bi, gi: (0, 0, 0)),
            pl.BlockSpec((2, CMP_HIDDEN, B_DH), lambda bi, gi: (0, 0, 0)),
            pl.BlockSpec((2, 8, 2 * wdt), lambda bi, gi: (0, 0, 0)),
        ],
        out_specs=pl.BlockSpec((None, None, nr, 2 * B_DH), lambda bi, gi: (bi, gi, 0, 0)),
        out_shape=jax.ShapeDtypeStruct((b, g, nr, 2 * B_DH), F32),
        compiler_params=_cparams(("parallel", "parallel")),
        name="nsa_compress",
    )(raw, w1, w2, pe)


def _stack_heads(q):
    return jnp.concatenate([q[:, i * B_DH:(i + 1) * B_DH] for i in range(B_HPG)], axis=0)


def _gated_unstack(o, gl_ref, L):
    gate = jax.nn.sigmoid(gl_ref[...])
    return jnp.concatenate([o[i * L:(i + 1) * L] * gate[:, i:i + 1] for i in range(B_HPG)], axis=1)


def _topk_mask_lanes(v, k):
    r, n = v.shape
    iota = lax.broadcasted_iota(jnp.int32, (r, n), 1).astype(F32)

    def body(_, c):
        v, sel = c
        mx = jnp.max(v, axis=1, keepdims=True)
        idx = jnp.min(jnp.where(v == mx, iota, float(n)), axis=1, keepdims=True)
        hit = iota == idx
        return jnp.where(hit, -jnp.inf, v), jnp.where(hit, 1.0, sel)

    _, sel = lax.fori_loop(0, k, body, (v, jnp.zeros((r, n), F32)))
    return sel


def _cmp_body(q_ref, kvc_ref, ov_ref, gl_ref, o_ref, mask_ref, *, n_c, n_top):
    L = ATT_TILE
    qi = pl.program_id(2)
    ncp = kvc_ref.shape[0]
    n_sel = ov_ref.shape[1]
    qs = _stack_heads(q_ref[...] * jnp.asarray(B_DH ** -0.5, BF16))
    kvc = kvc_ref[...]
    kc = kvc[:, :B_DH].astype(BF16)
    vc = kvc[:, B_DH:].astype(BF16)
    s = _dot_nt(qs, kc).reshape(B_HPG, L, ncp)
    tpos = qi * L + lax.broadcasted_iota(jnp.int32, (L, 1), 0)
    cidx = lax.broadcasted_iota(jnp.int32, (1, ncp), 1)
    valid = jnp.where(cidx < n_c, cidx * CMP_STRIDE + (CMP_LEN - 1), jnp.int32(2 ** 30)) <= tpos
    s = jnp.where(valid[None], s, NEG)
    mx = jnp.max(s, axis=2, keepdims=True)
    e = jnp.where(valid[None], jnp.exp(s - mx), 0.0)
    p = e / jnp.maximum(jnp.sum(e, axis=2, keepdims=True), 1e-30)
    o = jnp.dot(p.reshape(B_HPG * L, ncp).astype(BF16), vc, preferred_element_type=F32)
    o_ref[...] = _gated_unstack(o, gl_ref, L)

    psum = p[0] + p[1] + p[2] + p[3]
    ov = ov_ref[...]
    imp = jnp.zeros((L, n_sel), F32)
    rem = psum
    for _ in range(3):
        part = rem.astype(BF16)
        imp = imp + jnp.dot(part, ov, preferred_element_type=F32)
        rem = rem - part.astype(F32)
    blk = lax.broadcasted_iota(jnp.int32, (1, n_sel), 1)
    cur = jnp.right_shift(tpos, int(math.log2(SEL_BLOCK)))
    imp = jnp.where(blk * SEL_BLOCK > tpos, -FORCE, imp)
    imp = jnp.where(blk == 0, FORCE, imp)
    imp = jnp.where(blk == cur, FORCE, imp)
    imp = jnp.where(blk == cur - 1, FORCE, imp)
    mask_ref[...] = _topk_mask_lanes(imp, n_top).astype(mask_ref.dtype)


def _nsa_compressed(ha, kvc, overlap, glog, b, t, n_c, n_top):
    L = ATT_TILE
    ncp = kvc.shape[2]
    n_sel = overlap.shape[1]
    return pl.pallas_call(
        functools.partial(_cmp_body, n_c=n_c, n_top=n_top),
        grid=(b, B_GROUPS, t // L),
        in_specs=[
            pl.BlockSpec((None, L, 256), lambda bi, g, qi: (bi, qi, 16 + g)),
            pl.BlockSpec((None, None, ncp, 2 * B_DH), lambda bi, g, qi: (bi, g, 0, 0)),
            pl.BlockSpec((ncp, n_sel), lambda bi, g, qi: (0, 0)),
            pl.BlockSpec((None, None, L, B_HPG), lambda bi, g, qi: (bi, g, qi, 0)),
        ],
        out_specs=[
            pl.BlockSpec((None, L, 256), lambda bi, g, qi: (bi, qi, g)),
            pl.BlockSpec((None, None, L, n_sel), lambda bi, g, qi: (bi, g, qi, 0)),
        ],
        out_shape=[jax.ShapeDtypeStruct((b, t, MIX_W), F32),
                   jax.ShapeDtypeStruct((b, B_GROUPS, t, n_sel), BF16)],
        compiler_params=_cparams(("parallel", "parallel", "arbitrary")),
        name="nsa_compressed_select",
    )(ha, kvc, overlap, glog)


def _nsa_body(*refs, mode, n_bt):
    if mode == "sel":
        q_ref, kv_ref, bias_ref, gl_ref, mask_ref, o_ref, m_sc, l_sc, acc_sc = refs
    else:
        q_ref, kv_ref, bias_ref, gl_ref, o_ref, m_sc, l_sc, acc_sc = refs
    L = ATT_TILE
    qi = pl.program_id(2)
    m_sc[...] = jnp.full(m_sc.shape, NEG, F32)
    l_sc[...] = jnp.zeros(l_sc.shape, F32)
    acc_sc[...] = jnp.zeros(acc_sc.shape, F32)
    qs = _stack_heads(q_ref[...] * jnp.asarray(B_DH ** -0.5, BF16))

    def body(kt, carry):
        off = pl.multiple_of(kt * L, L)
        kv = kv_ref[pl.ds(off, L), :]
        k = kv[:, :B_DH]
        v = kv[:, B_DH:]
        bias = bias_ref[jnp.minimum(qi - kt, n_bt - 1)]
        s = _dot_nt(qs, k).reshape(B_HPG, L, L) + bias
        if mode == "sel":
            n_sel = mask_ref.shape[1]
            kblk = kt * (L // SEL_BLOCK) + jnp.right_shift(
                lax.broadcasted_iota(jnp.int32, (n_sel, L), 1), int(math.log2(SEL_BLOCK)))
            expand = jnp.where(lax.broadcasted_iota(jnp.int32, (n_sel, L), 0) == kblk, 1.0, 0.0).astype(BF16)
            chosen = jnp.dot(mask_ref[...], expand, preferred_element_type=F32)
            s = s + ((chosen - 1.0) * (-NEG))[None]
        _flash_update(s.reshape(B_HPG * L, L), v, m_sc, l_sc, acc_sc)
        return carry

    lo = 0 if mode == "sel" else jnp.maximum(qi - (n_bt - 1), 0)
    lax.fori_loop(lo, qi + 1, body, 0)
    o = acc_sc[...] / jnp.maximum(l_sc[...], 1e-30)
    o_ref[...] = _gated_unstack(o, gl_ref, L)


def _nsa_branch(ha, bias, glog, mask, b, t, branch, mode):
    L = ATT_TILE
    n_bt = bias.shape[0]
    in_specs = [
        pl.BlockSpec((None, L, 256), lambda bi, g, qi: (bi, qi, 16 + g)),
        pl.BlockSpec((None, t, 128), lambda bi, g, qi: (bi, 0, 40 + 4 * branch + g)),
        pl.BlockSpec((n_bt, None, B_HPG, L, L), lambda bi, g, qi: (0, g, 0, 0, 0)),
        pl.BlockSpec((None, None, L, B_HPG), lambda bi, g, qi: (bi, g, qi, 0)),
    ]
    args = [ha, ha, bias, glog]
    if mode == "sel":
        n_sel = mask.shape[-1]
        in_specs.append(pl.BlockSpec((None, None, L, n_sel), lambda bi, g, qi: (bi, g, qi, 0)))
        args.append(mask)
    return pl.pallas_call(
        functools.partial(_nsa_body, mode=mode, n_bt=n_bt),
        grid=(b, B_GROUPS, t // L),
        in_specs=in_specs,
        out_specs=pl.BlockSpec((None, L, 256), lambda bi, g, qi: (bi, qi, g)),
        out_shape=jax.ShapeDtypeStruct((b, t, MIX_W), F32),
        scratch_shapes=[pltpu.VMEM((B_HPG * L, 1), F32), pltpu.VMEM((B_HPG * L, 1), F32),
                        pltpu.VMEM((B_HPG * L, B_DH), F32)],
        compiler_params=_cparams(("parallel", "parallel", "arbitrary")),
        name="nsa_" + mode,
    )(*args)


def _mem_body(q_ref, k_ref, v_ref, o_ref):
    q = q_ref[...] * jnp.asarray(M_DH ** -0.5, BF16)
    outs = []
    for h in range(M_HEADS):
        sl = slice(h * M_DH, (h + 1) * M_DH)
        s = _dot_nt(q[:, sl], k_ref[:, sl])
        e = jnp.exp(s - jnp.max(s, axis=1, keepdims=True))
        p = e / jnp.sum(e, axis=1, keepdims=True)
        outs.append(jnp.dot(p.astype(BF16), v_ref[:, sl], preferred_element_type=F32))
    o_ref[...] = jnp.concatenate(outs, axis=1)


def _memory_attention(ha, memkv, b, t, n_mem):
    tq = 512
    w = M_HEADS * M_DH
    return pl.pallas_call(
        _mem_body,
        grid=(b, t // tq),
        in_specs=[
            pl.BlockSpec((None, tq, w), lambda bi, qi: (bi, qi, 0)),
            pl.BlockSpec((n_mem, w), lambda bi, qi: (bi, 0)),
            pl.BlockSpec((n_mem, w), lambda bi, qi: (bi, 1)),
        ],
        out_specs=pl.BlockSpec((None, tq, w), lambda bi, qi: (bi, qi, 0)),
        out_shape=jax.ShapeDtypeStruct((b, t, w), F32),
        compiler_params=_cparams(("parallel", "arbitrary")),
        name="memory_attention",
    )(ha, memkv, memkv)


def _merge_body(oa_ref, oc_ref, os_ref, ow_ref, om_ref, gl_ref, w_ref, o_ref):
    branches = (oa_ref[...], oc_ref[...] + os_ref[...] + ow_ref[...], om_ref[...])
    acc = jnp.zeros(o_ref.shape, F32)
    for n in range(3):
        up = jnp.dot(branches[n].astype(BF16), w_ref[n], preferred_element_type=F32)
        acc = acc + jax.nn.sigmoid(gl_ref[:, n * D_MODEL:(n + 1) * D_MODEL]) * up
    o_ref[...] = acc.astype(o_ref.dtype)


def _merge(o_a, o_c, o_s, o_w, o_m, hb, w_branch):
    n = o_a.shape[0]
    tm = 256
    row = lambda i: (i, 0)
    return pl.pallas_call(
        _merge_body,
        grid=(n // tm,),
        in_specs=[pl.BlockSpec((tm, MIX_W), row)] * 5 + [
            pl.BlockSpec((tm, 3 * D_MODEL), row),
            pl.BlockSpec((3, MIX_W, D_MODEL), lambda i: (0, 0, 0)),
        ],
        out_specs=pl.BlockSpec((tm, D_MODEL), row),
        out_shape=jax.ShapeDtypeStruct((n, D_MODEL), BF16),
        compiler_params=_cparams(("parallel",)),
        name="branch_merge",
    )(o_a, o_c, o_s, o_w, o_m, hb, w_branch)


def _layer_norm(z, g, b):
    mu = jnp.mean(z, axis=1, keepdims=True)
    zc = z - mu
    var = jnp.mean(zc * zc, axis=1, keepdims=True)
    return zc * lax.rsqrt(var + LN_EPS) * g + b


def _out_ln_body(y_ref, w_ref, x_ref, g_ref, b_ref, o_ref, ob_ref):
    y = jnp.dot(y_ref[...], w_ref[...], preferred_element_type=F32)
    o = _layer_norm(ALPHA * x_ref[...] + y, g_ref[...], b_ref[...])
    o_ref[...] = o
    ob_ref[...] = o.astype(BF16)


def _out_proj_ln(mixed, w_out, x, g, b):
    n = x.shape[0]
    tm = 512
    row = lambda i: (i, 0)
    const = lambda i: (0, 0)
    return pl.pallas_call(
        _out_ln_body,
        grid=(n // tm,),
        in_specs=[pl.BlockSpec((tm, D_MODEL), row), pl.BlockSpec((D_MODEL, D_MODEL), const),
                  pl.BlockSpec((tm, D_MODEL), row), pl.BlockSpec((1, D_MODEL), const),
                  pl.BlockSpec((1, D_MODEL), const)],
        out_specs=[pl.BlockSpec((tm, D_MODEL), row), pl.BlockSpec((tm, D_MODEL), row)],
        out_shape=[jax.ShapeDtypeStruct((n, D_MODEL), F32), jax.ShapeDtypeStruct((n, D_MODEL), BF16)],
        compiler_params=_cparams(("parallel",)),
        name="out_proj_ln1",
    )(mixed, w_out, x, g, b)


def _res_ln_body(x_ref, y_ref, g_ref, b_ref, o_ref):
    o_ref[...] = _layer_norm(ALPHA * x_ref[...] + y_ref[...], g_ref[...], b_ref[...])


def _residual_ln(x, y, g, b):
    n = x.shape[0]
    tm = 512
    row = lambda i: (i, 0)
    const = lambda i: (0, 0)
    return pl.pallas_call(
        _res_ln_body,
        grid=(n // tm,),
        in_specs=[pl.BlockSpec((tm, D_MODEL), row), pl.BlockSpec((tm, D_MODEL), row),
                  pl.BlockSpec((1, D_MODEL), const), pl.BlockSpec((1, D_MODEL), const)],
        out_specs=pl.BlockSpec((tm, D_MODEL), row),
        out_shape=jax.ShapeDtypeStruct((n, D_MODEL), F32),
        compiler_params=_cparams(("parallel",)),
        name="residual_ln2",
    )(x, y, g, b)


def _topk_axis0(v, k):
    r, n = v.shape
    iota = lax.broadcasted_iota(jnp.int32, (r, n), 0).astype(F32)
    slot = lax.broadcasted_iota(jnp.int32, (k, n), 0)

    def body(it, c):
        v, vals, idxs = c
        mx = jnp.max(v, axis=0, keepdims=True)
        idx = jnp.min(jnp.where(v == mx, iota, float(r)), axis=0, keepdims=True)
        v = jnp.where(iota == idx, -jnp.inf, v)
        return v, jnp.where(slot == it, mx, vals), jnp.where(slot == it, idx, idxs)

    _, vals, idxs = lax.fori_loop(0, k, body, (v, jnp.zeros((k, n), F32), jnp.zeros((k, n), F32)))
    return vals, idxs


def _pick_rows(table, pos, k):
    out = jnp.zeros(pos.shape, F32)
    for a in range(k):
        out = out + jnp.where(pos == float(a), table[a:a + 1], 0.0)
    return out


def _route_body(q_ref, keys_ref, ei_ref, ej_ref, g_ref):
    k = PEER_TOPK
    scores = _dot_nt(keys_ref[...], q_ref[...])
    v0, i0 = _topk_axis0(scores[:PEER_NKEYS], k)
    v1, i1 = _topk_axis0(scores[PEER_NKEYS:], k)
    comb = jnp.concatenate([v0[a:a + 1] + v1 for a in range(k)], axis=0)
    sf, pos = _topk_axis0(comb, k)
    pa = jnp.floor(pos * (1.0 / k))
    pb = pos - k * pa
    ei_ref[...] = _pick_rows(i0, pa, k)
    ej_ref[...] = _pick_rows(i1, pb, k)
    e = jnp.exp(sf - jnp.max(sf, axis=0, keepdims=True))
    g_ref[...] = e / jnp.sum(e, axis=0, keepdims=True)


def _peer_route(q, keys):
    n = q.shape[0]
    tn = 256
    out = jax.ShapeDtypeStruct((PEER_HEADS, PEER_TOPK, n), F32)
    ospec = pl.BlockSpec((None, PEER_TOPK, tn), lambda i, h: (h, 0, i))
    return pl.pallas_call(
        _route_body,
        grid=(n // tn, PEER_HEADS),
        in_specs=[pl.BlockSpec((tn, PEER_DKEY), lambda i, h: (i, h)),
                  pl.BlockSpec((None, 2 * PEER_NKEYS, PEER_DKEY), lambda i, h: (h, 0, 0))],
        out_specs=[ospec, ospec, ospec],
        out_shape=[out, out, out],
        compiler_params=_cparams(("parallel", "arbitrary")),
        name="peer_route",
    )(q, keys)


def _gate_body(ei_ref, ej_ref, g_ref, o_ref):
    tb = ei_ref.shape[0]
    nk = PEER_NKEYS
    iota = lax.broadcasted_iota(jnp.int32, (tb, nk, ei_ref.shape[2]), 1).astype(F32)
    rows = jnp.where(iota == ei_ref[...], 1.0, 0.0).astype(BF16)
    cols = jnp.where(iota == ej_ref[...], g_ref[...], 0.0).astype(BF16)
    gm = lax.dot_general(rows, cols, (((2,), (2,)), ((0,), (0,))), preferred_element_type=F32)
    o_ref[...] = gm.astype(o_ref.dtype)


def _gate_matrix(ei, ej, g):
    n, _, slots = ei.shape
    tb = 64
    spec = pl.BlockSpec((tb, 1, slots), lambda i: (i, 0, 0))
    return pl.pallas_call(
        _gate_body,
        grid=(n // tb,),
        in_specs=[spec, spec, spec],
        out_specs=pl.BlockSpec((tb, PEER_NKEYS, PEER_NKEYS), lambda i: (i, 0, 0)),
        out_shape=jax.ShapeDtypeStruct((n, PEER_NKEYS, PEER_NKEYS), BF16),
        compiler_params=_cparams(("parallel",)),
        name="peer_gate_matrix",
    )(ei, ej, g)


def _expert_body(x_ref, ut_ref, g_ref, v_ref, o_ref):
    @pl.when(pl.program_id(1) == 0)
    def _():
        o_ref[...] = jnp.zeros(o_ref.shape, F32)

    hid = jnp.dot(x_ref[...], ut_ref[...], preferred_element_type=F32)
    act = (jax.nn.gelu(hid) * g_ref[...].astype(F32)).astype(BF16)
    o_ref[...] += jnp.dot(act, v_ref[...], preferred_element_type=F32)


def _peer_experts(xb, ut, gm, v):
    n = xb.shape[0]
    ne = ut.shape[1]
    tn, te = 512, 1024
    return pl.pallas_call(
        _expert_body,
        grid=(n // tn, ne // te),
        in_specs=[pl.BlockSpec((tn, D_MODEL), lambda i, j: (i, 0)),
                  pl.BlockSpec((D_MODEL, te), lambda i, j: (0, j)),
                  pl.BlockSpec((tn, te), lambda i, j: (i, j)),
                  pl.BlockSpec((te, D_MODEL), lambda i, j: (j, 0))],
        out_specs=pl.BlockSpec((tn, D_MODEL), lambda i, j: (i, 0)),
        out_shape=jax.ShapeDtypeStruct((n, D_MODEL), F32),
        compiler_params=_cparams(("parallel", "arbitrary")),
        name="peer_experts",
    )(xb, ut, gm, v)


def _token_mixer(x, mem, w_in, diff_lambda, diff_subln, cmp_pe, cmp_w1, cmp_w2, w_mem_kv, w_branch, rel_bias):
    b, t, _ = x.shape
    n = b * t
    L = ATT_TILE
    xb = x.reshape(n, D_MODEL).astype(BF16)

    w_kv = w_in[:, 4096:5632].reshape(D_MODEL, 3, 2, B_GROUPS, B_DH).transpose(0, 1, 3, 2, 4).reshape(D_MODEL, 1536)
    w_a = jnp.concatenate([w_in[:, 5680:6704], w_in[:, :4096], w_kv], axis=1).astype(BF16)
    w_b = jnp.concatenate([w_in[:, 6704:], w_in[:, 5632:5680], jnp.zeros((D_MODEL, 464), F32)], axis=1).astype(BF16)
    ha = _matmul(xb, w_a, BF16, 1024, 512, "in_proj_a")
    hb = _matmul(xb, w_b, F32, 1024, 512, "in_proj_b")
    ha3 = ha.reshape(b, t, ha.shape[1])

    tab1d = rel_bias[_rel_bucket(jnp.arange(t))].T
    n_far = min(t // L, REL_MAX_DIST // L + 2)
    bias_a = _bias_tiles(tab1d[:A_HEADS], t, n_far, t)
    bias_b = tab1d[A_HEADS:]
    bias_sel = _bias_tiles(bias_b, t, n_far, t).reshape(n_far, B_GROUPS, B_HPG, L, L)
    n_win = min(t // L, WINDOW // L + 1)
    bias_win = _bias_tiles(bias_b, t, n_win, WINDOW).reshape(n_win, B_GROUPS, B_HPG, L, L)

    o_a = _diff_attention(ha3, bias_a, diff_lambda, diff_subln.reshape(1, A_DV), b, t)

    glog = hb[:, 6144:6192].reshape(b, t, 3, B_GROUPS, B_HPG).transpose(2, 0, 3, 1, 4)
    nr = t // CMP_STRIDE
    raw = ha3[:, :, 5120:5632].reshape(b, nr, CMP_STRIDE, B_GROUPS, 2, B_DH)
    raw = raw.transpose(0, 3, 4, 1, 2, 5).reshape(b, B_GROUPS, 2, nr, CMP_STRIDE * B_DH)
    pe = jnp.broadcast_to(cmp_pe.reshape(2, 1, CMP_LEN * B_DH), (2, 8, CMP_LEN * B_DH)).astype(BF16)
    kvc = _compress(raw, cmp_w1.astype(BF16), cmp_w2.astype(BF16), pe)
    n_c = (t - CMP_LEN) // CMP_STRIDE + 1
    n_sel = t // SEL_BLOCK
    cidx = np.arange(nr)[:, None] * CMP_STRIDE + np.arange(CMP_LEN)[None, :]
    overlap = (cidx[:, :, None] // SEL_BLOCK == np.arange(n_sel)[None, None, :]).astype(np.float32).mean(axis=1)
    overlap[n_c:] = 0.0
    o_c, sel_mask = _nsa_compressed(ha3, kvc, jnp.asarray(overlap, BF16), glog[0], b, t, n_c, min(SEL_TOPN, n_sel))
    o_s = _nsa_branch(ha3, bias_sel, glog[1], sel_mask, b, t, 1, "sel")
    o_w = _nsa_branch(ha3, bias_win, glog[2], None, b, t, 2, "win")

    n_mem = mem.shape[1]
    memkv = _matmul(mem.reshape(b * n_mem, D_MODEL).astype(BF16), w_mem_kv.astype(BF16), BF16,
                    b * n_mem, 1024, "mem_kv_proj")
    o_m = _memory_attention(ha3, memkv, b, t, n_mem)

    flat = lambda o: o.reshape(n, MIX_W)
    return flat(o_a), flat(o_c), flat(o_s), flat(o_w), flat(o_m), hb


def _peer(x1, x1b, peer_wq, peer_keys, peer_u, peer_v):
    n = x1.shape[0]
    q = _matmul(x1b, peer_wq.astype(BF16), BF16, 1024, 1024, "peer_query")
    zk = jnp.zeros((PEER_HEADS, PEER_NKEYS, PEER_DKEY // 2), F32)
    keys_bd = jnp.concatenate([jnp.concatenate([peer_keys[:, 0], zk], axis=2),
                               jnp.concatenate([zk, peer_keys[:, 1]], axis=2)], axis=1).astype(BF16)
    ei, ej, gate = _peer_route(q, keys_bd)
    slots = lambda a: a.reshape(PEER_HEADS * PEER_TOPK, n).T.reshape(n, 1, PEER_HEADS * PEER_TOPK)
    gm = _gate_matrix(slots(ei), slots(ej), slots(gate)).reshape(n, PEER_NKEYS * PEER_NKEYS)
    return _peer_experts(x1b, peer_u.astype(BF16).T, gm, peer_v.astype(BF16))


def kernel(x, mem, w_in, diff_lambda, diff_subln, cmp_pe, cmp_w1, cmp_w2, w_mem_kv, w_branch, w_out, ln1_g, ln1_b,
           peer_wq, peer_keys, peer_u, peer_v, ln2_g, ln2_b, rel_bias):
    b, t, _ = x.shape
    n = b * t
    for l in range(DEPTH):
        o_a, o_c, o_s, o_w, o_m, hb = _token_mixer(x, mem, w_in[l], diff_lambda[l], diff_subln[l], cmp_pe[l],
                                                   cmp_w1[l], cmp_w2[l], w_mem_kv[l], w_branch[l], rel_bias)
        mixed = _merge(o_a, o_c, o_s, o_w, o_m, hb, w_branch[l].astype(BF16))
        x1, x1b = _out_proj_ln(mixed, w_out[l].astype(BF16), x.reshape(n, D_MODEL),
                               ln1_g[l].reshape(1, D_MODEL), ln1_b[l].reshape(1, D_MODEL))
        y = _peer(x1, x1b, peer_wq[l], peer_keys[l], peer_u[l], peer_v[l])
        x = _residual_ln(x1, y, ln2_g[l].reshape(1, D_MODEL), ln2_b[l].reshape(1, D_MODEL)).reshape(b, t, D_MODEL)
    return x
```

```python
import functools
import math

import numpy as np
import jax
import jax.numpy as jnp
from jax import lax
from jax.experimental import pallas as pl
from jax.experimental.pallas import tpu as pltpu

F32 = jnp.float32
BF16 = jnp.bfloat16

D_MODEL = 2048
A_HEADS, A_DQK, A_DV = 8, 64, 128
B_HEADS, B_GROUPS, B_HPG, B_DH = 16, 4, 4, 64
CMP_LEN, CMP_STRIDE, CMP_HIDDEN = 32, 16, 256
SEL_BLOCK, SEL_TOPN, WINDOW = 64, 16, 512
M_HEADS, M_DH = 4, 256
REL_BUCKETS, REL_MAX_DIST = 32, 1024
PEER_HEADS, PEER_NKEYS, PEER_DKEY, PEER_TOPK = 8, 128, 256, 16
MIX_W = 1024
LN_EPS = 1e-5
FORCE = 1e9
DEPTH = 1
ALPHA = (2 * DEPTH) ** 0.25
LAMBDA_INIT = 0.8 - 0.6 * math.exp(-0.3 * 0)

NEG = -1e30
ATT_TILE = 256
LANES = 128
VMEM_LIMIT = 56 * 1024 * 1024


def _cparams(sem):
    return pltpu.CompilerParams(dimension_semantics=sem, vmem_limit_bytes=VMEM_LIMIT)


def _dot_nt(a, b):
    return lax.dot_general(a, b, (((1,), (1,)), ((), ())), preferred_element_type=F32)


def _mm_body(x_ref, w_ref, o_ref):
    o_ref[...] = jnp.dot(x_ref[...], w_ref[...], preferred_element_type=F32).astype(o_ref.dtype)


def _matmul(x, w, out_dtype, tm, tn, name):
    m, k = x.shape
    n = w.shape[1]
    return pl.pallas_call(
        _mm_body,
        grid=(m // tm, n // tn),
        in_specs=[pl.BlockSpec((tm, k), lambda i, j: (i, 0)),
                  pl.BlockSpec((k, tn), lambda i, j: (0, j))],
        out_specs=pl.BlockSpec((tm, tn), lambda i, j: (i, j)),
        out_shape=jax.ShapeDtypeStruct((m, n), out_dtype),
        compiler_params=_cparams(("parallel", "arbitrary")),
        name=name,
    )(x, w)


def _rel_bucket(dist):
    n = jnp.maximum(dist, 0)
    max_exact = REL_BUCKETS // 2
    nf = jnp.maximum(n, 1).astype(jnp.float32)
    large = max_exact + (jnp.log(nf / max_exact) / math.log(REL_MAX_DIST / max_exact)
                         * (REL_BUCKETS - max_exact)).astype(jnp.int32)
    large = jnp.minimum(large, REL_BUCKETS - 1)
    return jnp.where(n < max_exact, n, large)


def _bias_tiles(tab1d, t, first, n_tiles, max_dist):
    L = ATT_TILE
    m = np.arange(2 * L)
    off = np.where(m <= L, -m, 2 * L - m)
    d = (first + np.arange(n_tiles))[:, None] * L + off[None, :]
    ok = (d >= 0) & (d < max_dist)
    rp = jnp.where(ok[None], tab1d[:, np.clip(d, 0, t - 1)], NEG)
    h = tab1d.shape[0]
    full = jnp.broadcast_to(rp[:, :, None, :], (h, n_tiles, L, 2 * L)).reshape(h, n_tiles, 2 * L * L)
    tiles = full[:, :, :L * (2 * L - 1)].reshape(h, n_tiles, L, 2 * L - 1)[:, :, :, :L]
    return jnp.transpose(tiles, (1, 0, 2, 3))


def _lanes(x, w):
    if w <= LANES:
        return x[:, :w]
    return jnp.concatenate([x] * (w // LANES), axis=1)


def _flash_update(s, v_aug, m_ref, acc_ref):
    m_prev = m_ref[...]
    m_new = jnp.maximum(m_prev, jnp.max(s, axis=1, keepdims=True))
    alpha = jnp.exp(m_prev - m_new)
    p = jnp.exp(s - _lanes(m_new, s.shape[1]))
    acc_ref[...] = (_lanes(alpha, acc_ref.shape[1]) * acc_ref[...]
                    + jnp.dot(p.astype(BF16), v_aug, preferred_element_type=F32))
    m_ref[...] = m_new


def _diff_body(q_ref, k_ref, v_ref, bias_ref, lam_ref, g_ref, o_ref, m_sc, acc_sc, *, n_bt):
    L = ATT_TILE
    qi = pl.program_id(2)
    m_sc[...] = jnp.full(m_sc.shape, NEG, F32)
    acc_sc[...] = jnp.zeros(acc_sc.shape, F32)
    q = q_ref[...] * jnp.asarray(A_DQK ** -0.5, BF16)
    ones = jnp.ones((L, A_DV), BF16)

    def body(kt, carry):
        off = pl.multiple_of(kt * L, L)
        k = k_ref[pl.ds(off, L), :]
        v_aug = jnp.concatenate([v_ref[pl.ds(off, L), :], ones], axis=1)
        d0 = 2 * qi - kt
        bias = jnp.concatenate([bias_ref[jnp.minimum(d0 + 1, n_bt - 1)],
                                bias_ref[jnp.minimum(d0 + 2, n_bt - 1)]], axis=0)
        for mp in range(2):
            s = _dot_nt(q[:, mp * A_DQK:(mp + 1) * A_DQK], k[:, mp * A_DQK:(mp + 1) * A_DQK]) + bias
            _flash_update(s, v_aug, m_sc.at[mp], acc_sc.at[mp])
        return carry

    lax.fori_loop(0, 2 * qi + 2, body, 0)

    lp = lam_ref[...]
    lam = (jnp.exp(jnp.sum(lp[0:1] * lp[1:2], axis=1, keepdims=True))
           - jnp.exp(jnp.sum(lp[2:3] * lp[3:4], axis=1, keepdims=True)) + LAMBDA_INIT)
    o0 = acc_sc[0, :, :A_DV] / jnp.maximum(acc_sc[0, :, A_DV:], 1e-30)
    o1 = acc_sc[1, :, :A_DV] / jnp.maximum(acc_sc[1, :, A_DV:], 1e-30)
    o = o0 - lam * o1
    o = o * lax.rsqrt(jnp.mean(o * o, axis=1, keepdims=True) + LN_EPS) * g_ref[...]
    o_ref[...] = (o * (1.0 - LAMBDA_INIT)).astype(o_ref.dtype)


def _diff_attention(ha, bias, lam_params, subln, b, t):
    L = ATT_TILE
    tq = 2 * L
    n_bt = bias.shape[0]
    return pl.pallas_call(
        functools.partial(_diff_body, n_bt=n_bt),
        grid=(b, A_HEADS, t // tq),
        in_specs=[
            pl.BlockSpec((None, tq, 128), lambda bi, h, qi: (bi, qi, 8 + h)),
            pl.BlockSpec((None, t, 128), lambda bi, h, qi: (bi, 0, 16 + h)),
            pl.BlockSpec((None, t, 128), lambda bi, h, qi: (bi, 0, 24 + h)),
            pl.BlockSpec((n_bt, None, L, L), lambda bi, h, qi: (0, h, 0, 0)),
            pl.BlockSpec((4, A_DQK), lambda bi, h, qi: (0, 0)),
            pl.BlockSpec((1, A_DV), lambda bi, h, qi: (0, 0)),
        ],
        out_specs=pl.BlockSpec((None, tq, 128), lambda bi, h, qi: (bi, qi, h)),
        out_shape=jax.ShapeDtypeStruct((b, t, MIX_W), F32),
        scratch_shapes=[pltpu.VMEM((2, tq, LANES), F32), pltpu.VMEM((2, tq, 2 * A_DV), F32)],
        compiler_params=_cparams(("parallel", "parallel", "arbitrary")),
        name="diff_attention",
    )(ha, ha, ha, bias, lam_params, subln)


def _compress_body(r_ref, w1_ref, w2_ref, pe_ref, o_ref):
    outs = []
    half = CMP_STRIDE * B_DH
    for kv in range(2):
        r = r_ref[kv]
        w1 = w1_ref[kv]
        a = jnp.dot(r, w1[:half], preferred_element_type=F32)
        bb = jnp.dot(r, w1[half:], preferred_element_type=F32)
        bb = jnp.concatenate([bb[1:], bb[:1]], axis=0)
        pw = jnp.dot(pe_ref[kv], w1, preferred_element_type=F32)[0:1]
        hdn = jax.nn.gelu(a + bb + pw)
        outs.append(jnp.dot(hdn.astype(BF16), w2_ref[kv], preferred_element_type=F32))
    o_ref[...] = jnp.concatenate(outs, axis=1)


def _compress(raw, w1, w2, pe):
    b, g, _, nr, wdt = raw.shape
    return pl.pallas_call(
        _compress_body,
        grid=(b, g),
        in_specs=[
            pl.BlockSpec((None, None, 2, nr, wdt), lambda bi, gi: (bi, gi, 0, 0, 0)),
            pl.BlockSpec((2, 2 * wdt, CMP_HIDDEN), lambda bi, gi: (0, 0, 0)),
            pl.BlockSpec((2, CMP_HIDDEN, B_DH), lambda bi, gi: (0, 0, 0)),
            pl.BlockSpec((2, 8, 2 * wdt), lambda bi, gi: (0, 0, 0)),
        ],
        out_specs=pl.BlockSpec((None, None, nr, 2 * B_DH), lambda bi, gi: (bi, gi, 0, 0)),
        out_shape=jax.ShapeDtypeStruct((b, g, nr, 2 * B_DH), F32),
        compiler_params=_cparams(("parallel", "parallel")),
        name="nsa_compress",
    )(raw, w1, w2, pe)


def _stack_heads(q):
    return jnp.concatenate([q[:, i * B_DH:(i + 1) * B_DH] for i in range(B_HPG)], axis=0)


def _gated_unstack(o, gl_ref, L):
    gate = jax.nn.sigmoid(gl_ref[...])
    return jnp.concatenate([o[i * L:(i + 1) * L] * gate[:, i:i + 1] for i in range(B_HPG)], axis=1)


def _topk_mask_lanes(v, k):
    r, n = v.shape
    iota = lax.broadcasted_iota(jnp.int32, (r, n), 1).astype(F32)

    def body(_, c):
        v, sel = c
        mx = jnp.max(v, axis=1, keepdims=True)
        idx = jnp.min(jnp.where(v == mx, iota, float(n)), axis=1, keepdims=True)
        hit = iota == idx
        return jnp.where(hit, -jnp.inf, v), jnp.where(hit, 1.0, sel)

    _, sel = lax.fori_loop(0, k, body, (v, jnp.zeros((r, n), F32)))
    return sel


def _cmp_body(q_ref, kvc_ref, ov_ref, gl_ref, o_ref, mask_ref, *, n_c, n_top):
    L = ATT_TILE
    qi = pl.program_id(2)
    ncp = kvc_ref.shape[0]
    n_sel = ov_ref.shape[1]
    qs = _stack_heads(q_ref[...] * jnp.asarray(B_DH ** -0.5, BF16))
    kvc = kvc_ref[...]
    kc = kvc[:, :B_DH].astype(BF16)
    vc = kvc[:, B_DH:].astype(BF16)
    s = _dot_nt(qs, kc).reshape(B_HPG, L, ncp)
    tpos = qi * L + lax.broadcasted_iota(jnp.int32, (L, 1), 0)
    cidx = lax.broadcasted_iota(jnp.int32, (1, ncp), 1)
    valid = jnp.where(cidx < n_c, cidx * CMP_STRIDE + (CMP_LEN - 1), jnp.int32(2 ** 30)) <= tpos
    s = jnp.where(valid[None], s, NEG)
    mx = jnp.max(s, axis=2, keepdims=True)
    e = jnp.where(valid[None], jnp.exp(s - mx), 0.0)
    p = e / jnp.maximum(jnp.sum(e, axis=2, keepdims=True), 1e-30)
    o = jnp.dot(p.reshape(B_HPG * L, ncp).astype(BF16), vc, preferred_element_type=F32)
    o_ref[...] = _gated_unstack(o, gl_ref, L)

    psum = p[0] + p[1] + p[2] + p[3]
    ov = ov_ref[...]
    imp = jnp.zeros((L, n_sel), F32)
    rem = psum
    for _ in range(3):
        part = rem.astype(BF16)
        imp = imp + jnp.dot(part, ov, preferred_element_type=F32)
        rem = rem - part.astype(F32)
    blk = lax.broadcasted_iota(jnp.int32, (1, n_sel), 1)
    cur = jnp.right_shift(tpos, int(math.log2(SEL_BLOCK)))
    imp = jnp.where(blk * SEL_BLOCK > tpos, -FORCE, imp)
    imp = jnp.where(blk == 0, FORCE, imp)
    imp = jnp.where(blk == cur, FORCE, imp)
    imp = jnp.where(blk == cur - 1, FORCE, imp)
    mask_ref[...] = jnp.where(_topk_mask_lanes(imp, n_top) > 0.0, 0.0, NEG).astype(mask_ref.dtype)


def _nsa_compressed(ha, kvc, overlap, glog, b, t, n_c, n_top):
    L = ATT_TILE
    ncp = kvc.shape[2]
    n_sel = overlap.shape[1]
    return pl.pallas_call(
        functools.partial(_cmp_body, n_c=n_c, n_top=n_top),
        grid=(b, B_GROUPS, t // L),
        in_specs=[
            pl.BlockSpec((None, L, 256), lambda bi, g, qi: (bi, qi, 16 + g)),
            pl.BlockSpec((None, None, ncp, 2 * B_DH), lambda bi, g, qi: (bi, g, 0, 0)),
            pl.BlockSpec((ncp, n_sel), lambda bi, g, qi: (0, 0)),
            pl.BlockSpec((None, None, L, B_HPG), lambda bi, g, qi: (bi, g, qi, 0)),
        ],
        out_specs=[
            pl.BlockSpec((None, L, 256), lambda bi, g, qi: (bi, qi, g)),
            pl.BlockSpec((None, None, L, n_sel), lambda bi, g, qi: (bi, g, qi, 0)),
        ],
        out_shape=[jax.ShapeDtypeStruct((b, t, MIX_W), F32),
                   jax.ShapeDtypeStruct((b, B_GROUPS, t, n_sel), BF16)],
        compiler_params=_cparams(("parallel", "parallel", "arbitrary")),
        name="nsa_compressed_select",
    )(ha, kvc, overlap, glog)


def _nsa_body(*refs, mode, n_bt):
    if mode == "sel":
        q_ref, kv_ref, bias_ref, gl_ref, mask_ref, o_ref, m_sc, acc_sc = refs
    else:
        q_ref, kv_ref, bias_ref, gl_ref, o_ref, m_sc, acc_sc = refs
    L = ATT_TILE
    qi = pl.program_id(2)
    m_sc[...] = jnp.full(m_sc.shape, NEG, F32)
    acc_sc[...] = jnp.zeros(acc_sc.shape, F32)
    q = q_ref[...] * jnp.asarray(B_DH ** -0.5, BF16)
    if mode == "sel":
        n_sel = mask_ref.shape[1]
        qm = mask_ref[...]
        if n_sel < B_DH:
            qm = jnp.concatenate([qm, jnp.zeros((L, B_DH - n_sel), BF16)], axis=1)
    else:
        qm = jnp.zeros((L, B_DH), BF16)
    qs = jnp.concatenate([jnp.concatenate([q[:, i * B_DH:(i + 1) * B_DH], qm], axis=1) for i in range(B_HPG)],
                         axis=0)
    lane = lax.broadcasted_iota(jnp.int32, (L, LANES), 1)
    row_blk = jnp.right_shift(lax.broadcasted_iota(jnp.int32, (L, LANES), 0), int(math.log2(SEL_BLOCK)))

    def body(kt, carry):
        off = pl.multiple_of(kt * L, L)
        kv = kv_ref[pl.ds(off, L), :]
        if mode == "sel":
            hot = lane - B_DH == kt * (L // SEL_BLOCK) + row_blk
            k_aug = jnp.where(lane < B_DH, kv, jnp.where(hot, 1.0, 0.0).astype(BF16))
        else:
            k_aug = kv
        v_aug = jnp.where(lane < B_DH, jnp.ones_like(kv), kv)
        bias = bias_ref[jnp.minimum(qi - kt, n_bt - 1)]
        s = _dot_nt(qs, k_aug).reshape(B_HPG, L, L) + bias
        _flash_update(s.reshape(B_HPG * L, L), v_aug, m_sc, acc_sc)
        return carry

    lo = 0 if mode == "sel" else jnp.maximum(qi - (n_bt - 1), 0)
    lax.fori_loop(lo, qi + 1, body, 0)
    o = acc_sc[:, B_DH:] / jnp.maximum(acc_sc[:, :B_DH], 1e-30)
    o_ref[...] = _gated_unstack(o, gl_ref, L)


def _nsa_branch(ha, bias, glog, mask, b, t, branch, mode):
    L = ATT_TILE
    n_bt = bias.shape[0]
    in_specs = [
        pl.BlockSpec((None, L, 256), lambda bi, g, qi: (bi, qi, 16 + g)),
        pl.BlockSpec((None, t, 128), lambda bi, g, qi: (bi, 0, 40 + 4 * branch + g)),
        pl.BlockSpec((n_bt, None, B_HPG, L, L), lambda bi, g, qi: (0, g, 0, 0, 0)),
        pl.BlockSpec((None, None, L, B_HPG), lambda bi, g, qi: (bi, g, qi, 0)),
    ]
    args = [ha, ha, bias, glog]
    if mode == "sel":
        n_sel = mask.shape[-1]
        in_specs.append(pl.BlockSpec((None, None, L, n_sel), lambda bi, g, qi: (bi, g, qi, 0)))
        args.append(mask)
    return pl.pallas_call(
        functools.partial(_nsa_body, mode=mode, n_bt=n_bt),
        grid=(b, B_GROUPS, t // L),
        in_specs=in_specs,
        out_specs=pl.BlockSpec((None, L, 256), lambda bi, g, qi: (bi, qi, g)),
        out_shape=jax.ShapeDtypeStruct((b, t, MIX_W), F32),
        scratch_shapes=[pltpu.VMEM((B_HPG * L, LANES), F32), pltpu.VMEM((B_HPG * L, 2 * B_DH), F32)],
        compiler_params=_cparams(("parallel", "parallel", "arbitrary")),
        name="nsa_" + mode,
    )(*args)


def _mem_body(q_ref, k_ref, v_ref, o_ref):
    q = q_ref[...] * jnp.asarray(M_DH ** -0.5, BF16)
    outs = []
    for h in range(M_HEADS):
        sl = slice(h * M_DH, (h + 1) * M_DH)
        s = _dot_nt(q[:, sl], k_ref[:, sl])
        e = jnp.exp(s - jnp.max(s, axis=1, keepdims=True))
        p = e / jnp.sum(e, axis=1, keepdims=True)
        outs.append(jnp.dot(p.astype(BF16), v_ref[:, sl], preferred_element_type=F32))
    o_ref[...] = jnp.concatenate(outs, axis=1)


def _memory_attention(ha, memkv, b, t, n_mem):
    tq = 512
    w = M_HEADS * M_DH
    return pl.pallas_call(
        _mem_body,
        grid=(b, t // tq),
        in_specs=[
            pl.BlockSpec((None, tq, w), lambda bi, qi: (bi, qi, 0)),
            pl.BlockSpec((n_mem, w), lambda bi, qi: (bi, 0)),
            pl.BlockSpec((n_mem, w), lambda bi, qi: (bi, 1)),
        ],
        out_specs=pl.BlockSpec((None, tq, w), lambda bi, qi: (bi, qi, 0)),
        out_shape=jax.ShapeDtypeStruct((b, t, w), F32),
        compiler_params=_cparams(("parallel", "arbitrary")),
        name="memory_attention",
    )(ha, memkv, memkv)


def _merge_body(oa_ref, oc_ref, os_ref, ow_ref, om_ref, gl_ref, w_ref, o_ref):
    branches = (oa_ref[...], oc_ref[...] + os_ref[...] + ow_ref[...], om_ref[...])
    acc = jnp.zeros(o_ref.shape, F32)
    for n in range(3):
        up = jnp.dot(branches[n].astype(BF16), w_ref[n], preferred_element_type=F32)
        acc = acc + jax.nn.sigmoid(gl_ref[:, n * D_MODEL:(n + 1) * D_MODEL]) * up
    o_ref[...] = acc.astype(o_ref.dtype)


def _merge(o_a, o_c, o_s, o_w, o_m, hb, w_branch):
    n = o_a.shape[0]
    tm = 256
    row = lambda i: (i, 0)
    return pl.pallas_call(
        _merge_body,
        grid=(n // tm,),
        in_specs=[pl.BlockSpec((tm, MIX_W), row)] * 5 + [
            pl.BlockSpec((tm, 3 * D_MODEL), row),
            pl.BlockSpec((3, MIX_W, D_MODEL), lambda i: (0, 0, 0)),
        ],
        out_specs=pl.BlockSpec((tm, D_MODEL), row),
        out_shape=jax.ShapeDtypeStruct((n, D_MODEL), BF16),
        compiler_params=_cparams(("parallel",)),
        name="branch_merge",
    )(o_a, o_c, o_s, o_w, o_m, hb, w_branch)


def _layer_norm(z, g, b):
    mu = jnp.mean(z, axis=1, keepdims=True)
    zc = z - mu
    var = jnp.mean(zc * zc, axis=1, keepdims=True)
    return zc * lax.rsqrt(var + LN_EPS) * g + b


def _out_ln_body(y_ref, w_ref, x_ref, g_ref, b_ref, o_ref, ob_ref):
    y = jnp.dot(y_ref[...], w_ref[...], preferred_element_type=F32)
    o = _layer_norm(ALPHA * x_ref[...] + y, g_ref[...], b_ref[...])
    o_ref[...] = o
    ob_ref[...] = o.astype(BF16)


def _out_proj_ln(mixed, w_out, x, g, b):
    n = x.shape[0]
    tm = 512
    row = lambda i: (i, 0)
    const = lambda i: (0, 0)
    return pl.pallas_call(
        _out_ln_body,
        grid=(n // tm,),
        in_specs=[pl.BlockSpec((tm, D_MODEL), row), pl.BlockSpec((D_MODEL, D_MODEL), const),
                  pl.BlockSpec((tm, D_MODEL), row), pl.BlockSpec((1, D_MODEL), const),
                  pl.BlockSpec((1, D_MODEL), const)],
        out_specs=[pl.BlockSpec((tm, D_MODEL), row), pl.BlockSpec((tm, D_MODEL), row)],
        out_shape=[jax.ShapeDtypeStruct((n, D_MODEL), F32), jax.ShapeDtypeStruct((n, D_MODEL), BF16)],
        compiler_params=_cparams(("parallel",)),
        name="out_proj_ln1",
    )(mixed, w_out, x, g, b)


def _res_ln_body(x_ref, y_ref, g_ref, b_ref, o_ref):
    o_ref[...] = _layer_norm(ALPHA * x_ref[...] + y_ref[...], g_ref[...], b_ref[...])


def _residual_ln(x, y, g, b):
    n = x.shape[0]
    tm = 512
    row = lambda i: (i, 0)
    const = lambda i: (0, 0)
    return pl.pallas_call(
        _res_ln_body,
        grid=(n // tm,),
        in_specs=[pl.BlockSpec((tm, D_MODEL), row), pl.BlockSpec((tm, D_MODEL), row),
                  pl.BlockSpec((1, D_MODEL), const), pl.BlockSpec((1, D_MODEL), const)],
        out_specs=pl.BlockSpec((tm, D_MODEL), row),
        out_shape=jax.ShapeDtypeStruct((n, D_MODEL), F32),
        compiler_params=_cparams(("parallel",)),
        name="residual_ln2",
    )(x, y, g, b)


def _topk_axis0(v, k):
    r, n = v.shape
    iota = lax.broadcasted_iota(jnp.int32, (r, n), 0).astype(F32)
    slot = lax.broadcasted_iota(jnp.int32, (k, n), 0)

    def body(it, c):
        v, vals, idxs = c
        mx = jnp.max(v, axis=0, keepdims=True)
        idx = jnp.min(jnp.where(v == mx, iota, float(r)), axis=0, keepdims=True)
        v = jnp.where(iota == idx, -jnp.inf, v)
        return v, jnp.where(slot == it, mx, vals), jnp.where(slot == it, idx, idxs)

    _, vals, idxs = lax.fori_loop(0, k, body, (v, jnp.zeros((k, n), F32), jnp.zeros((k, n), F32)))
    return vals, idxs


def _pick_rows(table, pos, k):
    out = jnp.zeros(pos.shape, F32)
    for a in range(k):
        out = out + jnp.where(pos == float(a), table[a:a + 1], 0.0)
    return out


def _route_body(q_ref, keys_ref, ei_ref, ej_ref, g_ref):
    k = PEER_TOPK
    scores = _dot_nt(keys_ref[...], q_ref[...])
    v0, i0 = _topk_axis0(scores[:PEER_NKEYS], k)
    v1, i1 = _topk_axis0(scores[PEER_NKEYS:], k)
    comb = jnp.concatenate([v0[a:a + 1] + v1 for a in range(k)], axis=0)
    sf, pos = _topk_axis0(comb, k)
    pa = jnp.floor(pos * (1.0 / k))
    pb = pos - k * pa
    ei_ref[...] = _pick_rows(i0, pa, k)
    ej_ref[...] = _pick_rows(i1, pb, k)
    e = jnp.exp(sf - jnp.max(sf, axis=0, keepdims=True))
    g_ref[...] = e / jnp.sum(e, axis=0, keepdims=True)


def _peer_route(q, keys):
    n = q.shape[0]
    tn = 256
    out = jax.ShapeDtypeStruct((PEER_HEADS, PEER_TOPK, n), F32)
    ospec = pl.BlockSpec((None, PEER_TOPK, tn), lambda i, h: (h, 0, i))
    return pl.pallas_call(
        _route_body,
        grid=(n // tn, PEER_HEADS),
        in_specs=[pl.BlockSpec((tn, PEER_DKEY), lambda i, h: (i, h)),
                  pl.BlockSpec((None, 2 * PEER_NKEYS, PEER_DKEY), lambda i, h: (h, 0, 0))],
        out_specs=[ospec, ospec, ospec],
        out_shape=[out, out, out],
        compiler_params=_cparams(("parallel", "arbitrary")),
        name="peer_route",
    )(q, keys)


def _gate_body(ei_ref, ej_ref, g_ref, o_ref):
    tb = ei_ref.shape[0]
    nk = PEER_NKEYS
    iota = lax.broadcasted_iota(jnp.int32, (tb, nk, ei_ref.shape[2]), 1).astype(F32)
    rows = jnp.where(iota == ei_ref[...], 1.0, 0.0).astype(BF16)
    cols = jnp.where(iota == ej_ref[...], g_ref[...], 0.0).astype(BF16)
    gm = lax.dot_general(rows, cols, (((2,), (2,)), ((0,), (0,))), preferred_element_type=F32)
    o_ref[...] = gm.astype(o_ref.dtype)


def _gate_matrix(ei, ej, g):
    n, _, slots = ei.shape
    tb = 64
    spec = pl.BlockSpec((tb, 1, slots), lambda i: (i, 0, 0))
    return pl.pallas_call(
        _gate_body,
        grid=(n // tb,),
        in_specs=[spec, spec, spec],
        out_specs=pl.BlockSpec((tb, PEER_NKEYS, PEER_NKEYS), lambda i: (i, 0, 0)),
        out_shape=jax.ShapeDtypeStruct((n, PEER_NKEYS, PEER_NKEYS), BF16),
        compiler_params=_cparams(("parallel",)),
        name="peer_gate_matrix",
    )(ei, ej, g)


def _expert_body(x_ref, ut_ref, g_ref, v_ref, o_ref):
    @pl.when(pl.program_id(1) == 0)
    def _():
        o_ref[...] = jnp.zeros(o_ref.shape, F32)

    hid = jnp.dot(x_ref[...], ut_ref[...], preferred_element_type=F32)
    act = (jax.nn.gelu(hid) * g_ref[...].astype(F32)).astype(BF16)
    o_ref[...] += jnp.dot(act, v_ref[...], preferred_element_type=F32)


def _peer_experts(xb, ut, gm, v):
    n = xb.shape[0]
    ne = ut.shape[1]
    tn, te = 512, 1024
    return pl.pallas_call(
        _expert_body,
        grid=(n // tn, ne // te),
        in_specs=[pl.BlockSpec((tn, D_MODEL), lambda i, j: (i, 0)),
                  pl.BlockSpec((D_MODEL, te), lambda i, j: (0, j)),
                  pl.BlockSpec((tn, te), lambda i, j: (i, j)),
                  pl.BlockSpec((te, D_MODEL), lambda i, j: (j, 0))],
        out_specs=pl.BlockSpec((tn, D_MODEL), lambda i, j: (i, 0)),
        out_shape=jax.ShapeDtypeStruct((n, D_MODEL), F32),
        compiler_params=_cparams(("parallel", "arbitrary")),
        name="peer_experts",
    )(xb, ut, gm, v)


def _token_mixer(x, mem, w_in, diff_lambda, diff_subln, cmp_pe, cmp_w1, cmp_w2, w_mem_kv, w_branch, rel_bias):
    b, t, _ = x.shape
    n = b * t
    L = ATT_TILE
    xb = x.reshape(n, D_MODEL).astype(BF16)

    w_kv = w_in[:, 4096:5632].reshape(D_MODEL, 3, 2, B_GROUPS, B_DH).transpose(0, 1, 3, 2, 4).reshape(D_MODEL, 1536)
    w_a = jnp.concatenate([w_in[:, 5680:6704], w_in[:, :4096], w_kv], axis=1).astype(BF16)
    w_b = jnp.concatenate([w_in[:, 6704:], w_in[:, 5632:5680], jnp.zeros((D_MODEL, 464), F32)], axis=1).astype(BF16)
    ha = _matmul(xb, w_a, BF16, 1024, 512, "in_proj_a")
    hb = _matmul(xb, w_b, F32, 1024, 512, "in_proj_b")
    ha3 = ha.reshape(b, t, ha.shape[1])

    tab1d = rel_bias[_rel_bucket(jnp.arange(t))].T
    n_far = min(t // L, REL_MAX_DIST // L + 2)
    bias_a = _bias_tiles(tab1d[:A_HEADS], t, -1, n_far + 1, t)
    bias_b = tab1d[A_HEADS:]
    bias_sel = _bias_tiles(bias_b, t, 0, n_far, t).reshape(n_far, B_GROUPS, B_HPG, L, L)
    n_win = min(t // L, WINDOW // L + 1)
    bias_win = _bias_tiles(bias_b, t, 0, n_win, WINDOW).reshape(n_win, B_GROUPS, B_HPG, L, L)

    o_a = _diff_attention(ha3, bias_a, diff_lambda, diff_subln.reshape(1, A_DV), b, t)

    glog = hb[:, 6144:6192].reshape(b, t, 3, B_GROUPS, B_HPG).transpose(2, 0, 3, 1, 4)
    nr = t // CMP_STRIDE
    raw = ha3[:, :, 5120:5632].reshape(b, nr, CMP_STRIDE, B_GROUPS, 2, B_DH)
    raw = raw.transpose(0, 3, 4, 1, 2, 5).reshape(b, B_GROUPS, 2, nr, CMP_STRIDE * B_DH)
    pe = jnp.broadcast_to(cmp_pe.reshape(2, 1, CMP_LEN * B_DH), (2, 8, CMP_LEN * B_DH)).astype(BF16)
    kvc = _compress(raw, cmp_w1.astype(BF16), cmp_w2.astype(BF16), pe)
    n_c = (t - CMP_LEN) // CMP_STRIDE + 1
    n_sel = t // SEL_BLOCK
    cidx = np.arange(nr)[:, None] * CMP_STRIDE + np.arange(CMP_LEN)[None, :]
    overlap = (cidx[:, :, None] // SEL_BLOCK == np.arange(n_sel)[None, None, :]).astype(np.float32).mean(axis=1)
    overlap[n_c:] = 0.0
    o_c, sel_mask = _nsa_compressed(ha3, kvc, jnp.asarray(overlap, BF16), glog[0], b, t, n_c, min(SEL_TOPN, n_sel))
    o_s = _nsa_branch(ha3, bias_sel, glog[1], sel_mask, b, t, 1, "sel")
    o_w = _nsa_branch(ha3, bias_win, glog[2], None, b, t, 2, "win")

    n_mem = mem.shape[1]
    memkv = _matmul(mem.reshape(b * n_mem, D_MODEL).astype(BF16), w_mem_kv.astype(BF16), BF16,
                    b * n_mem, 1024, "mem_kv_proj")
    o_m = _memory_attention(ha3, memkv, b, t, n_mem)

    flat = lambda o: o.reshape(n, MIX_W)
    return flat(o_a), flat(o_c), flat(o_s), flat(o_w), flat(o_m), hb


def _peer(x1, x1b, peer_wq, peer_keys, peer_u, peer_v):
    n = x1.shape[0]
    q = _matmul(x1b, peer_wq.astype(BF16), BF16, 1024, 1024, "peer_query")
    zk = jnp.zeros((PEER_HEADS, PEER_NKEYS, PEER_DKEY // 2), F32)
    keys_bd = jnp.concatenate([jnp.concatenate([peer_keys[:, 0], zk], axis=2),
                               jnp.concatenate([zk, peer_keys[:, 1]], axis=2)], axis=1).astype(BF16)
    ei, ej, gate = _peer_route(q, keys_bd)
    slots = lambda a: a.reshape(PEER_HEADS * PEER_TOPK, n).T.reshape(n, 1, PEER_HEADS * PEER_TOPK)
    gm = _gate_matrix(slots(ei), slots(ej), slots(gate)).reshape(n, PEER_NKEYS * PEER_NKEYS)
    return _peer_experts(x1b, peer_u.astype(BF16).T, gm, peer_v.astype(BF16))


def kernel(x, mem, w_in, diff_lambda, diff_subln, cmp_pe, cmp_w1, cmp_w2, w_mem_kv, w_branch, w_out, ln1_g, ln1_b,
           peer_wq, peer_keys, peer_u, peer_v, ln2_g, ln2_b, rel_bias):
    b, t, _ = x.shape
    n = b * t
    for l in range(DEPTH):
        o_a, o_c, o_s, o_w, o_m, hb = _token_mixer(x, mem, w_in[l], diff_lambda[l], diff_subln[l], cmp_pe[l],
                                                   cmp_w1[l], cmp_w2[l], w_mem_kv[l], w_branch[l], rel_bias)
        mixed = _merge(o_a, o_c, o_s, o_w, o_m, hb, w_branch[l].astype(BF16))
        x1, x1b = _out_proj_ln(mixed, w_out[l].astype(BF16), x.reshape(n, D_MODEL),
                               ln1_g[l].reshape(1, D_MODEL), ln1_b[l].reshape(1, D_MODEL))
        y = _peer(x1, x1b, peer_wq[l], peer_keys[l], peer_u[l], peer_v[l])
        x = _residual_ln(x1, y, ln2_g[l].reshape(1, D_MODEL), ln2_b[l].reshape(1, D_MODEL)).reshape(b, t, D_MODEL)
    return x
```

```python
import functools
import math

import numpy as np
import jax
import jax.numpy as jnp
from jax import lax
from jax.experimental import pallas as pl
from jax.experimental.pallas import tpu as pltpu

F32 = jnp.float32
BF16 = jnp.bfloat16

D_MODEL = 2048
A_HEADS, A_DQK, A_DV = 8, 64, 128
B_HEADS, B_GROUPS, B_HPG, B_DH = 16, 4, 4, 64
CMP_LEN, CMP_STRIDE, CMP_HIDDEN = 32, 16, 256
SEL_BLOCK, SEL_TOPN, WINDOW = 64, 16, 512
M_HEADS, M_DH = 4, 256
REL_BUCKETS, REL_MAX_DIST = 32, 1024
PEER_HEADS, PEER_NKEYS, PEER_DKEY, PEER_TOPK = 8, 128, 256, 16
MIX_W = 1024
LN_EPS = 1e-5
FORCE = 1e9
DEPTH = 1
ALPHA = (2 * DEPTH) ** 0.25
LAMBDA_INIT = 0.8 - 0.6 * math.exp(-0.3 * 0)

NEG = -1e30
ATT_TILE = 256
LANES = 128
VMEM_LIMIT = 56 * 1024 * 1024


def _cparams(sem):
    return pltpu.CompilerParams(dimension_semantics=sem, vmem_limit_bytes=VMEM_LIMIT)


def _dot_nt(a, b):
    return lax.dot_general(a, b, (((1,), (1,)), ((), ())), preferred_element_type=F32)


def _mm_body(x_ref, w_ref, o_ref):
    o_ref[...] = jnp.dot(x_ref[...], w_ref[...], preferred_element_type=F32).astype(o_ref.dtype)


def _matmul(x, w, out_dtype, tm, tn, name):
    m, k = x.shape
    n = w.shape[1]
    return pl.pallas_call(
        _mm_body,
        grid=(m // tm, n // tn),
        in_specs=[pl.BlockSpec((tm, k), lambda i, j: (i, 0)),
                  pl.BlockSpec((k, tn), lambda i, j: (0, j))],
        out_specs=pl.BlockSpec((tm, tn), lambda i, j: (i, j)),
        out_shape=jax.ShapeDtypeStruct((m, n), out_dtype),
        compiler_params=_cparams(("parallel", "arbitrary")),
        name=name,
    )(x, w)


def _rel_bucket(dist):
    n = jnp.maximum(dist, 0)
    max_exact = REL_BUCKETS // 2
    nf = jnp.maximum(n, 1).astype(jnp.float32)
    large = max_exact + (jnp.log(nf / max_exact) / math.log(REL_MAX_DIST / max_exact)
                         * (REL_BUCKETS - max_exact)).astype(jnp.int32)
    large = jnp.minimum(large, REL_BUCKETS - 1)
    return jnp.where(n < max_exact, n, large)


def _bias_tiles(tab1d, t, first, n_tiles, max_dist):
    L = ATT_TILE
    m = np.arange(2 * L)
    off = np.where(m <= L, -m, 2 * L - m)
    d = (first + np.arange(n_tiles))[:, None] * L + off[None, :]
    ok = (d >= 0) & (d < max_dist)
    h = tab1d.shape[0]
    rp = jnp.where(ok[:, None], tab1d.T[np.clip(d, 0, t - 1)].transpose(0, 2, 1), NEG)
    tiles = pl.pallas_call(
        _toeplitz_body,
        grid=(n_tiles * h,),
        in_specs=[pl.BlockSpec((None, 1, 2 * L), lambda i: (i, 0, 0))],
        out_specs=pl.BlockSpec((None, L, L), lambda i: (i, 0, 0)),
        out_shape=jax.ShapeDtypeStruct((n_tiles * h, L, L), F32),
        compiler_params=_cparams(("parallel",)),
        name="bias_tiles",
    )(rp.reshape(n_tiles * h, 1, 2 * L))
    return tiles.reshape(n_tiles, h, L, L)


def _toeplitz_body(rp_ref, o_ref):
    L = o_ref.shape[0]
    rows = jnp.broadcast_to(rp_ref[...], (L, 2 * L))
    o_ref[...] = pltpu.roll(rows, 0, 1, stride=1, stride_axis=0)[:, :L]


def _lanes(x, w):
    if w <= LANES:
        return x[:, :w]
    return jnp.concatenate([x] * (w // LANES), axis=1)


def _flash_update(s, v_aug, m_ref, acc_ref):
    m_prev = m_ref[...]
    m_new = jnp.maximum(m_prev, jnp.max(s, axis=1, keepdims=True))
    alpha = jnp.exp(m_prev - m_new)
    p = jnp.exp(s - _lanes(m_new, s.shape[1]))
    acc_ref[...] = (_lanes(alpha, acc_ref.shape[1]) * acc_ref[...]
                    + jnp.dot(p.astype(BF16), v_aug, preferred_element_type=F32))
    m_ref[...] = m_new


def _diff_body(q_ref, k_ref, v_ref, bias_ref, lam_ref, g_ref, o_ref, m_sc, acc_sc, *, n_bt):
    L = ATT_TILE
    qi = pl.program_id(2)
    m_sc[...] = jnp.full(m_sc.shape, NEG, F32)
    acc_sc[...] = jnp.zeros(acc_sc.shape, F32)
    q = q_ref[...] * jnp.asarray(A_DQK ** -0.5, BF16)
    ones = jnp.ones((L, A_DV), BF16)

    def body(kt, carry):
        off = pl.multiple_of(kt * L, L)
        k = k_ref[pl.ds(off, L), :]
        v_aug = jnp.concatenate([v_ref[pl.ds(off, L), :], ones], axis=1)
        d0 = 2 * qi - kt
        bias = jnp.concatenate([bias_ref[jnp.minimum(d0 + 1, n_bt - 1)],
                                bias_ref[jnp.minimum(d0 + 2, n_bt - 1)]], axis=0)
        for mp in range(2):
            s = _dot_nt(q[:, mp * A_DQK:(mp + 1) * A_DQK], k[:, mp * A_DQK:(mp + 1) * A_DQK]) + bias
            _flash_update(s, v_aug, m_sc.at[mp], acc_sc.at[mp])
        return carry

    lax.fori_loop(0, 2 * qi + 2, body, 0)

    lp = lam_ref[...]
    lam = (jnp.exp(jnp.sum(lp[0:1] * lp[1:2], axis=1, keepdims=True))
           - jnp.exp(jnp.sum(lp[2:3] * lp[3:4], axis=1, keepdims=True)) + LAMBDA_INIT)
    o0 = acc_sc[0, :, :A_DV] / jnp.maximum(acc_sc[0, :, A_DV:], 1e-30)
    o1 = acc_sc[1, :, :A_DV] / jnp.maximum(acc_sc[1, :, A_DV:], 1e-30)
    o = o0 - lam * o1
    o = o * lax.rsqrt(jnp.mean(o * o, axis=1, keepdims=True) + LN_EPS) * g_ref[...]
    o_ref[...] = (o * (1.0 - LAMBDA_INIT)).astype(o_ref.dtype)


def _diff_attention(ha, bias, lam_params, subln, b, t):
    L = ATT_TILE
    tq = 2 * L
    n_bt = bias.shape[0]
    return pl.pallas_call(
        functools.partial(_diff_body, n_bt=n_bt),
        grid=(b, A_HEADS, t // tq),
        in_specs=[
            pl.BlockSpec((None, tq, 128), lambda bi, h, qi: (bi, qi, 8 + h)),
            pl.BlockSpec((None, t, 128), lambda bi, h, qi: (bi, 0, 16 + h)),
            pl.BlockSpec((None, t, 128), lambda bi, h, qi: (bi, 0, 24 + h)),
            pl.BlockSpec((n_bt, None, L, L), lambda bi, h, qi: (0, h, 0, 0)),
            pl.BlockSpec((4, A_DQK), lambda bi, h, qi: (0, 0)),
            pl.BlockSpec((1, A_DV), lambda bi, h, qi: (0, 0)),
        ],
        out_specs=pl.BlockSpec((None, tq, 128), lambda bi, h, qi: (bi, qi, h)),
        out_shape=jax.ShapeDtypeStruct((b, t, MIX_W), F32),
        scratch_shapes=[pltpu.VMEM((2, tq, LANES), F32), pltpu.VMEM((2, tq, 2 * A_DV), F32)],
        compiler_params=_cparams(("parallel", "parallel", "arbitrary")),
        name="diff_attention",
    )(ha, ha, ha, bias, lam_params, subln)


def _compress_body(r_ref, w1_ref, w2_ref, pe_ref, o_ref):
    outs = []
    half = CMP_STRIDE * B_DH
    for kv in range(2):
        r = r_ref[kv]
        w1 = w1_ref[kv]
        a = jnp.dot(r, w1[:half], preferred_element_type=F32)
        bb = jnp.dot(r, w1[half:], preferred_element_type=F32)
        bb = jnp.concatenate([bb[1:], bb[:1]], axis=0)
        pw = jnp.dot(pe_ref[kv], w1, preferred_element_type=F32)[0:1]
        hdn = jax.nn.gelu(a + bb + pw)
        outs.append(jnp.dot(hdn.astype(BF16), w2_ref[kv], preferred_element_type=F32))
    o_ref[...] = jnp.concatenate(outs, axis=1)


def _compress(raw, w1, w2, pe):
    b, g, _, nr, wdt = raw.shape
    return pl.pallas_call(
        _compress_body,
        grid=(b, g),
        in_specs=[
            pl.BlockSpec((None, None, 2, nr, wdt), lambda bi, gi: (bi, gi, 0, 0, 0)),
            pl.BlockSpec((2, 2 * wdt, CMP_HIDDEN), lambda bi, gi: (0, 0, 0)),
            pl.BlockSpec((2, CMP_HIDDEN, B_DH), lambda bi, gi: (0, 0, 0)),
            pl.BlockSpec((2, 8, 2 * wdt), lambda bi, gi: (0, 0, 0)),
        ],
        out_specs=pl.BlockSpec((None, None, nr, 2 * B_DH), lambda bi, gi: (bi, gi, 0, 0)),
        out_shape=jax.ShapeDtypeStruct((b, g, nr, 2 * B_DH), F32),
        compiler_params=_cparams(("parallel", "parallel")),
        name="nsa_compress",
    )(raw, w1, w2, pe)


def _stack_heads(q):
    return jnp.concatenate([q[:, i * B_DH:(i + 1) * B_DH] for i in range(B_HPG)], axis=0)


def _gated_unstack(o, gl_ref, L):
    gate = jax.nn.sigmoid(gl_ref[...])
    return jnp.concatenate([o[i * L:(i + 1) * L] * gate[:, i:i + 1] for i in range(B_HPG)], axis=1)


def _topk_mask_axis0(v, k):
    r, n = v.shape
    iota = lax.broadcasted_iota(jnp.int32, (r, n), 0).astype(F32)

    def body(_, c):
        v, sel = c
        mx = jnp.max(v, axis=0, keepdims=True)
        idx = jnp.min(jnp.where(v == mx, iota, float(r)), axis=0, keepdims=True)
        hit = iota == idx
        return jnp.where(hit, -jnp.inf, v), jnp.where(hit, 1.0, sel)

    _, sel = lax.fori_loop(0, k, body, (v, jnp.zeros((r, n), F32)))
    return sel


def _cmp_body(q_ref, kvc_ref, ov_ref, gl_ref, o_ref, mask_ref, *, n_c, n_top):
    L = ATT_TILE
    qi = pl.program_id(2)
    ncp = kvc_ref.shape[0]
    n_sel = mask_ref.shape[1]
    qs = _stack_heads(q_ref[...] * jnp.asarray(B_DH ** -0.5, BF16))
    kvc = kvc_ref[...]
    kc = kvc[:, :B_DH].astype(BF16)
    vc = kvc[:, B_DH:].astype(BF16)
    s = _dot_nt(qs, kc).reshape(B_HPG, L, ncp)
    tpos = qi * L + lax.broadcasted_iota(jnp.int32, (L, 1), 0)
    cidx = lax.broadcasted_iota(jnp.int32, (1, ncp), 1)
    valid = jnp.where(cidx < n_c, cidx * CMP_STRIDE + (CMP_LEN - 1), jnp.int32(2 ** 30)) <= tpos
    s = jnp.where(valid[None], s, NEG)
    mx = jnp.max(s, axis=2, keepdims=True)
    e = jnp.where(valid[None], jnp.exp(s - mx), 0.0)
    p = e / jnp.maximum(jnp.sum(e, axis=2, keepdims=True), 1e-30)
    o = jnp.dot(p.reshape(B_HPG * L, ncp).astype(BF16), vc, preferred_element_type=F32)
    o_ref[...] = _gated_unstack(o, gl_ref, L)

    psum = p[0] + p[1] + p[2] + p[3]
    ov_t = ov_ref[...]
    imp = jnp.zeros((ov_t.shape[0], L), F32)
    rem = psum
    for _ in range(3):
        part = rem.astype(BF16)
        imp = imp + _dot_nt(ov_t, part)
        rem = rem - part.astype(F32)
    imp = imp[:n_sel]
    blk = lax.broadcasted_iota(jnp.int32, (n_sel, 1), 0)
    tcol = qi * L + lax.broadcasted_iota(jnp.int32, (1, L), 1)
    cur = jnp.right_shift(tcol, int(math.log2(SEL_BLOCK)))
    imp = jnp.where(blk * SEL_BLOCK > tcol, -FORCE, imp)
    imp = jnp.where(blk == 0, FORCE, imp)
    imp = jnp.where(blk == cur, FORCE, imp)
    imp = jnp.where(blk == cur - 1, FORCE, imp)
    add = jnp.where(_topk_mask_axis0(imp, n_top) > 0.0, 0.0, NEG)
    add = jnp.concatenate([add, jnp.zeros((ov_t.shape[0] - n_sel, L), F32)], axis=0)
    mask_ref[...] = add.T[:, :n_sel].astype(mask_ref.dtype)


def _nsa_compressed(ha, kvc, overlap_t, glog, b, t, n_c, n_top):
    L = ATT_TILE
    ncp = kvc.shape[2]
    n_sel = t // SEL_BLOCK
    return pl.pallas_call(
        functools.partial(_cmp_body, n_c=n_c, n_top=n_top),
        grid=(b, B_GROUPS, t // L),
        in_specs=[
            pl.BlockSpec((None, L, 256), lambda bi, g, qi: (bi, qi, 16 + g)),
            pl.BlockSpec((None, None, ncp, 2 * B_DH), lambda bi, g, qi: (bi, g, 0, 0)),
            pl.BlockSpec(overlap_t.shape, lambda bi, g, qi: (0, 0)),
            pl.BlockSpec((None, None, L, B_HPG), lambda bi, g, qi: (bi, g, qi, 0)),
        ],
        out_specs=[
            pl.BlockSpec((None, L, 256), lambda bi, g, qi: (bi, qi, g)),
            pl.BlockSpec((None, None, L, n_sel), lambda bi, g, qi: (bi, g, qi, 0)),
        ],
        out_shape=[jax.ShapeDtypeStruct((b, t, MIX_W), F32),
                   jax.ShapeDtypeStruct((b, B_GROUPS, t, n_sel), BF16)],
        compiler_params=_cparams(("parallel", "parallel", "arbitrary")),
        name="nsa_compressed_select",
    )(ha, kvc, overlap_t, glog)


def _nsa_body(*refs, mode, n_bt):
    if mode == "sel":
        q_ref, kv_ref, bias_ref, gl_ref, mask_ref, o_ref, m_sc, acc_sc = refs
    else:
        q_ref, kv_ref, bias_ref, gl_ref, o_ref, m_sc, acc_sc = refs
    L = ATT_TILE
    qi = pl.program_id(2)
    m_sc[...] = jnp.full(m_sc.shape, NEG, F32)
    acc_sc[...] = jnp.zeros(acc_sc.shape, F32)
    q = q_ref[...] * jnp.asarray(B_DH ** -0.5, BF16)
    if mode == "sel":
        n_sel = mask_ref.shape[1]
        qm = mask_ref[...]
        if n_sel < B_DH:
            qm = jnp.concatenate([qm, jnp.zeros((L, B_DH - n_sel), BF16)], axis=1)
    else:
        qm = jnp.zeros((L, B_DH), BF16)
    qs = jnp.concatenate([jnp.concatenate([q[:, i * B_DH:(i + 1) * B_DH], qm], axis=1) for i in range(B_HPG)],
                         axis=0)
    lane = lax.broadcasted_iota(jnp.int32, (L, LANES), 1)
    row_blk = jnp.right_shift(lax.broadcasted_iota(jnp.int32, (L, LANES), 0), int(math.log2(SEL_BLOCK)))

    def body(kt, carry):
        off = pl.multiple_of(kt * L, L)
        kv = kv_ref[pl.ds(off, L), :]
        if mode == "sel":
            hot = lane - B_DH == kt * (L // SEL_BLOCK) + row_blk
            k_aug = jnp.where(lane < B_DH, kv, jnp.where(hot, 1.0, 0.0).astype(BF16))
        else:
            k_aug = kv
        v_aug = jnp.where(lane < B_DH, jnp.ones_like(kv), kv)
        bias = bias_ref[jnp.minimum(qi - kt, n_bt - 1)]
        s = _dot_nt(qs, k_aug).reshape(B_HPG, L, L) + bias
        _flash_update(s.reshape(B_HPG * L, L), v_aug, m_sc, acc_sc)
        return carry

    lo = 0 if mode == "sel" else jnp.maximum(qi - (n_bt - 1), 0)
    lax.fori_loop(lo, qi + 1, body, 0)
    o = acc_sc[:, B_DH:] / jnp.maximum(acc_sc[:, :B_DH], 1e-30)
    o_ref[...] = _gated_unstack(o, gl_ref, L)


def _nsa_branch(ha, bias, glog, mask, b, t, branch, mode):
    L = ATT_TILE
    n_bt = bias.shape[0]
    in_specs = [
        pl.BlockSpec((None, L, 256), lambda bi, g, qi: (bi, qi, 16 + g)),
        pl.BlockSpec((None, t, 128), lambda bi, g, qi: (bi, 0, 40 + 4 * branch + g)),
        pl.BlockSpec((n_bt, None, B_HPG, L, L), lambda bi, g, qi: (0, g, 0, 0, 0)),
        pl.BlockSpec((None, None, L, B_HPG), lambda bi, g, qi: (bi, g, qi, 0)),
    ]
    args = [ha, ha, bias, glog]
    if mode == "sel":
        n_sel = mask.shape[-1]
        in_specs.append(pl.BlockSpec((None, None, L, n_sel), lambda bi, g, qi: (bi, g, qi, 0)))
        args.append(mask)
    return pl.pallas_call(
        functools.partial(_nsa_body, mode=mode, n_bt=n_bt),
        grid=(b, B_GROUPS, t // L),
        in_specs=in_specs,
        out_specs=pl.BlockSpec((None, L, 256), lambda bi, g, qi: (bi, qi, g)),
        out_shape=jax.ShapeDtypeStruct((b, t, MIX_W), F32),
        scratch_shapes=[pltpu.VMEM((B_HPG * L, LANES), F32), pltpu.VMEM((B_HPG * L, 2 * B_DH), F32)],
        compiler_params=_cparams(("parallel", "parallel", "arbitrary")),
        name="nsa_" + mode,
    )(*args)


def _mem_body(q_ref, k_ref, v_ref, o_ref):
    q = q_ref[...] * jnp.asarray(M_DH ** -0.5, BF16)
    outs = []
    for h in range(M_HEADS):
        sl = slice(h * M_DH, (h + 1) * M_DH)
        s = _dot_nt(q[:, sl], k_ref[:, sl])
        e = jnp.exp(s - jnp.max(s, axis=1, keepdims=True))
        p = e / jnp.sum(e, axis=1, keepdims=True)
        outs.append(jnp.dot(p.astype(BF16), v_ref[:, sl], preferred_element_type=F32))
    o_ref[...] = jnp.concatenate(outs, axis=1)


def _memory_attention(ha, memkv, b, t, n_mem):
    tq = 512
    w = M_HEADS * M_DH
    return pl.pallas_call(
        _mem_body,
        grid=(b, t // tq),
        in_specs=[
            pl.BlockSpec((None, tq, w), lambda bi, qi: (bi, qi, 0)),
            pl.BlockSpec((n_mem, w), lambda bi, qi: (bi, 0)),
            pl.BlockSpec((n_mem, w), lambda bi, qi: (bi, 1)),
        ],
        out_specs=pl.BlockSpec((None, tq, w), lambda bi, qi: (bi, qi, 0)),
        out_shape=jax.ShapeDtypeStruct((b, t, w), F32),
        compiler_params=_cparams(("parallel", "arbitrary")),
        name="memory_attention",
    )(ha, memkv, memkv)


def _merge_body(oa_ref, oc_ref, os_ref, ow_ref, om_ref, gl_ref, w_ref, o_ref):
    branches = (oa_ref[...], oc_ref[...] + os_ref[...] + ow_ref[...], om_ref[...])
    acc = jnp.zeros(o_ref.shape, F32)
    for n in range(3):
        up = jnp.dot(branches[n].astype(BF16), w_ref[n], preferred_element_type=F32)
        acc = acc + jax.nn.sigmoid(gl_ref[:, n * D_MODEL:(n + 1) * D_MODEL]) * up
    o_ref[...] = acc.astype(o_ref.dtype)


def _merge(o_a, o_c, o_s, o_w, o_m, hb, w_branch):
    n = o_a.shape[0]
    tm = 256
    row = lambda i: (i, 0)
    return pl.pallas_call(
        _merge_body,
        grid=(n // tm,),
        in_specs=[pl.BlockSpec((tm, MIX_W), row)] * 5 + [
            pl.BlockSpec((tm, 3 * D_MODEL), row),
            pl.BlockSpec((3, MIX_W, D_MODEL), lambda i: (0, 0, 0)),
        ],
        out_specs=pl.BlockSpec((tm, D_MODEL), row),
        out_shape=jax.ShapeDtypeStruct((n, D_MODEL), BF16),
        compiler_params=_cparams(("parallel",)),
        name="branch_merge",
    )(o_a, o_c, o_s, o_w, o_m, hb, w_branch)


def _layer_norm(z, g, b):
    mu = jnp.mean(z, axis=1, keepdims=True)
    zc = z - mu
    var = jnp.mean(zc * zc, axis=1, keepdims=True)
    return zc * lax.rsqrt(var + LN_EPS) * g + b


def _out_ln_body(y_ref, w_ref, x_ref, g_ref, b_ref, o_ref, ob_ref):
    y = jnp.dot(y_ref[...], w_ref[...], preferred_element_type=F32)
    o = _layer_norm(ALPHA * x_ref[...] + y, g_ref[...], b_ref[...])
    o_ref[...] = o
    ob_ref[...] = o.astype(BF16)


def _out_proj_ln(mixed, w_out, x, g, b):
    n = x.shape[0]
    tm = 512
    row = lambda i: (i, 0)
    const = lambda i: (0, 0)
    return pl.pallas_call(
        _out_ln_body,
        grid=(n // tm,),
        in_specs=[pl.BlockSpec((tm, D_MODEL), row), pl.BlockSpec((D_MODEL, D_MODEL), const),
                  pl.BlockSpec((tm, D_MODEL), row), pl.BlockSpec((1, D_MODEL), const),
                  pl.BlockSpec((1, D_MODEL), const)],
        out_specs=[pl.BlockSpec((tm, D_MODEL), row), pl.BlockSpec((tm, D_MODEL), row)],
        out_shape=[jax.ShapeDtypeStruct((n, D_MODEL), F32), jax.ShapeDtypeStruct((n, D_MODEL), BF16)],
        compiler_params=_cparams(("parallel",)),
        name="out_proj_ln1",
    )(mixed, w_out, x, g, b)


def _res_ln_body(x_ref, y_ref, g_ref, b_ref, o_ref):
    o_ref[...] = _layer_norm(ALPHA * x_ref[...] + y_ref[...], g_ref[...], b_ref[...])


def _residual_ln(x, y, g, b):
    n = x.shape[0]
    tm = 512
    row = lambda i: (i, 0)
    const = lambda i: (0, 0)
    return pl.pallas_call(
        _res_ln_body,
        grid=(n // tm,),
        in_specs=[pl.BlockSpec((tm, D_MODEL), row), pl.BlockSpec((tm, D_MODEL), row),
                  pl.BlockSpec((1, D_MODEL), const), pl.BlockSpec((1, D_MODEL), const)],
        out_specs=pl.BlockSpec((tm, D_MODEL), row),
        out_shape=jax.ShapeDtypeStruct((n, D_MODEL), F32),
        compiler_params=_cparams(("parallel",)),
        name="residual_ln2",
    )(x, y, g, b)


def _topk_axis0(v, k):
    r, n = v.shape
    iota = lax.broadcasted_iota(jnp.int32, (r, n), 0).astype(F32)
    slot = lax.broadcasted_iota(jnp.int32, (k, n), 0)

    def body(it, c):
        v, vals, idxs = c
        mx = jnp.max(v, axis=0, keepdims=True)
        idx = jnp.min(jnp.where(v == mx, iota, float(r)), axis=0, keepdims=True)
        v = jnp.where(iota == idx, -jnp.inf, v)
        return v, jnp.where(slot == it, mx, vals), jnp.where(slot == it, idx, idxs)

    _, vals, idxs = lax.fori_loop(0, k, body, (v, jnp.zeros((k, n), F32), jnp.zeros((k, n), F32)))
    return vals, idxs


def _pick_rows(table, pos, k):
    out = jnp.zeros(pos.shape, F32)
    for a in range(k):
        out = out + jnp.where(pos == float(a), table[a:a + 1], 0.0)
    return out


def _route_body(q_ref, keys_ref, ei_ref, ej_ref, g_ref):
    k = PEER_TOPK
    scores = _dot_nt(keys_ref[...], q_ref[...])
    v0, i0 = _topk_axis0(scores[:PEER_NKEYS], k)
    v1, i1 = _topk_axis0(scores[PEER_NKEYS:], k)
    counts = [k // (a + 1) for a in range(k)]
    starts = np.cumsum([0] + counts[:-1])
    pad = (-sum(counts)) % 8
    comb = jnp.concatenate([v0[a:a + 1] + v1[:counts[a]] for a in range(k)]
                           + [jnp.full((pad, v0.shape[1]), -jnp.inf, F32)], axis=0)
    sf, pos = _topk_axis0(comb, k)
    pa = jnp.zeros(pos.shape, F32)
    pb = pos
    for a in range(1, k):
        later = pos >= float(starts[a])
        pa = pa + jnp.where(later, 1.0, 0.0)
        pb = pb - jnp.where(later, float(counts[a - 1]), 0.0)
    ei_ref[...] = _pick_rows(i0, pa, k)
    ej_ref[...] = _pick_rows(i1, pb, k)
    e = jnp.exp(sf - jnp.max(sf, axis=0, keepdims=True))
    g_ref[...] = e / jnp.sum(e, axis=0, keepdims=True)


def _peer_route(q, keys):
    n = q.shape[0]
    tn = 256
    out = jax.ShapeDtypeStruct((PEER_HEADS, PEER_TOPK, n), F32)
    ospec = pl.BlockSpec((None, PEER_TOPK, tn), lambda i, h: (h, 0, i))
    return pl.pallas_call(
        _route_body,
        grid=(n // tn, PEER_HEADS),
        in_specs=[pl.BlockSpec((tn, PEER_DKEY), lambda i, h: (i, h)),
                  pl.BlockSpec((None, 2 * PEER_NKEYS, PEER_DKEY), lambda i, h: (h, 0, 0))],
        out_specs=[ospec, ospec, ospec],
        out_shape=[out, out, out],
        compiler_params=_cparams(("parallel", "arbitrary")),
        name="peer_route",
    )(q, keys)


def _gate_body(ei_ref, ej_ref, g_ref, o_ref):
    tb = ei_ref.shape[0]
    nk = PEER_NKEYS
    iota = lax.broadcasted_iota(jnp.int32, (tb, nk, ei_ref.shape[2]), 1).astype(F32)
    rows = jnp.where(iota == ei_ref[...], 1.0, 0.0).astype(BF16)
    cols = jnp.where(iota == ej_ref[...], g_ref[...], 0.0).astype(BF16)
    gm = lax.dot_general(rows, cols, (((2,), (2,)), ((0,), (0,))), preferred_element_type=F32)
    o_ref[...] = jnp.swapaxes(gm, 0, 1).astype(o_ref.dtype)


def _gate_matrix(ei, ej, g):
    n, _, slots = ei.shape
    tb = 64
    spec = pl.BlockSpec((tb, 1, slots), lambda i: (i, 0, 0))
    return pl.pallas_call(
        _gate_body,
        grid=(n // tb,),
        in_specs=[spec, spec, spec],
        out_specs=pl.BlockSpec((PEER_NKEYS, tb, PEER_NKEYS), lambda i: (0, i, 0)),
        out_shape=jax.ShapeDtypeStruct((PEER_NKEYS, n, PEER_NKEYS), BF16),
        compiler_params=_cparams(("parallel",)),
        name="peer_gate_matrix",
    )(ei, ej, g)


def _expert_body(x_ref, u_ref, g_ref, v_ref, o_ref):
    @pl.when(pl.program_id(1) == 0)
    def _():
        o_ref[...] = jnp.zeros(o_ref.shape, F32)

    hid = _dot_nt(x_ref[...], u_ref[...])
    gate = jnp.concatenate([g_ref[i] for i in range(g_ref.shape[0])], axis=1)
    act = (jax.nn.gelu(hid) * gate.astype(F32)).astype(BF16)
    o_ref[...] += jnp.dot(act, v_ref[...], preferred_element_type=F32)


def _peer_experts(xb, u, gm, v):
    n = xb.shape[0]
    ne = u.shape[0]
    tn, te = 512, 1024
    return pl.pallas_call(
        _expert_body,
        grid=(n // tn, ne // te),
        in_specs=[pl.BlockSpec((tn, D_MODEL), lambda i, j: (i, 0)),
                  pl.BlockSpec((te, D_MODEL), lambda i, j: (j, 0)),
                  pl.BlockSpec((te // PEER_NKEYS, tn, PEER_NKEYS), lambda i, j: (j, i, 0)),
                  pl.BlockSpec((te, D_MODEL), lambda i, j: (j, 0))],
        out_specs=pl.BlockSpec((tn, D_MODEL), lambda i, j: (i, 0)),
        out_shape=jax.ShapeDtypeStruct((n, D_MODEL), F32),
        compiler_params=_cparams(("parallel", "arbitrary")),
        name="peer_experts",
    )(xb, u, gm, v)


def _token_mixer(x, mem, w_in, diff_lambda, diff_subln, cmp_pe, cmp_w1, cmp_w2, w_mem_kv, w_branch, rel_bias):
    b, t, _ = x.shape
    n = b * t
    L = ATT_TILE
    xb = x.reshape(n, D_MODEL).astype(BF16)

    w_kv = w_in[:, 4096:5632].reshape(D_MODEL, 3, 2, B_GROUPS, B_DH).transpose(0, 1, 3, 2, 4).reshape(D_MODEL, 1536)
    w_a = jnp.concatenate([w_in[:, 5680:6704], w_in[:, :4096], w_kv], axis=1).astype(BF16)
    w_b = jnp.concatenate([w_in[:, 6704:], w_in[:, 5632:5680], jnp.zeros((D_MODEL, 464), F32)], axis=1).astype(BF16)
    ha = _matmul(xb, w_a, BF16, 1024, 512, "in_proj_a")
    hb = _matmul(xb, w_b, F32, 1024, 512, "in_proj_b")
    ha3 = ha.reshape(b, t, ha.shape[1])

    tab1d = rel_bias[_rel_bucket(jnp.arange(t))].T
    n_far = min(t // L, REL_MAX_DIST // L + 2)
    bias_a = _bias_tiles(tab1d[:A_HEADS], t, -1, n_far + 1, t)
    bias_b = tab1d[A_HEADS:]
    bias_sel = _bias_tiles(bias_b, t, 0, n_far, t).reshape(n_far, B_GROUPS, B_HPG, L, L)
    n_win = min(t // L, WINDOW // L + 1)
    bias_win = _bias_tiles(bias_b, t, 0, n_win, WINDOW).reshape(n_win, B_GROUPS, B_HPG, L, L)

    o_a = _diff_attention(ha3, bias_a, diff_lambda, diff_subln.reshape(1, A_DV), b, t)

    glog = hb[:, 6144:6192].reshape(b, t, 3, B_GROUPS, B_HPG).transpose(2, 0, 3, 1, 4)
    nr = t // CMP_STRIDE
    raw = ha3[:, :, 5120:5632].reshape(b, nr, CMP_STRIDE, B_GROUPS, 2, B_DH)
    raw = raw.transpose(0, 3, 4, 1, 2, 5).reshape(b, B_GROUPS, 2, nr, CMP_STRIDE * B_DH)
    pe = jnp.broadcast_to(cmp_pe.reshape(2, 1, CMP_LEN * B_DH), (2, 8, CMP_LEN * B_DH)).astype(BF16)
    kvc = _compress(raw, cmp_w1.astype(BF16), cmp_w2.astype(BF16), pe)
    n_c = (t - CMP_LEN) // CMP_STRIDE + 1
    n_sel = t // SEL_BLOCK
    cidx = np.arange(nr)[:, None] * CMP_STRIDE + np.arange(CMP_LEN)[None, :]
    overlap = (cidx[:, :, None] // SEL_BLOCK == np.arange(n_sel)[None, None, :]).astype(np.float32).mean(axis=1)
    overlap[n_c:] = 0.0
    overlap_t = np.zeros((LANES, nr), np.float32)
    overlap_t[:n_sel] = overlap.T
    o_c, sel_mask = _nsa_compressed(ha3, kvc, jnp.asarray(overlap_t, BF16), glog[0], b, t, n_c,
                                    min(SEL_TOPN, n_sel))
    o_s = _nsa_branch(ha3, bias_sel, glog[1], sel_mask, b, t, 1, "sel")
    o_w = _nsa_branch(ha3, bias_win, glog[2], None, b, t, 2, "win")

    n_mem = mem.shape[1]
    memkv = _matmul(mem.reshape(b * n_mem, D_MODEL).astype(BF16), w_mem_kv.astype(BF16), BF16,
                    b * n_mem, 1024, "mem_kv_proj")
    o_m = _memory_attention(ha3, memkv, b, t, n_mem)

    flat = lambda o: o.reshape(n, MIX_W)
    return flat(o_a), flat(o_c), flat(o_s), flat(o_w), flat(o_m), hb


def _peer(x1, x1b, peer_wq, peer_keys, peer_u, peer_v):
    n = x1.shape[0]
    q = _matmul(x1b, peer_wq.astype(BF16), BF16, 1024, 1024, "peer_query")
    zk = jnp.zeros((PEER_HEADS, PEER_NKEYS, PEER_DKEY // 2), F32)
    keys_bd = jnp.concatenate([jnp.concatenate([peer_keys[:, 0], zk], axis=2),
                               jnp.concatenate([zk, peer_keys[:, 1]], axis=2)], axis=1).astype(BF16)
    ei, ej, gate = _peer_route(q, keys_bd)
    slots = lambda a: a.reshape(PEER_HEADS * PEER_TOPK, n).T.reshape(n, 1, PEER_HEADS * PEER_TOPK)
    gm = _gate_matrix(slots(ei), slots(ej), slots(gate))
    return _peer_experts(x1b, peer_u.astype(BF16), gm, peer_v.astype(BF16))


def kernel(x, mem, w_in, diff_lambda, diff_subln, cmp_pe, cmp_w1, cmp_w2, w_mem_kv, w_branch, w_out, ln1_g, ln1_b,
           peer_wq, peer_keys, peer_u, peer_v, ln2_g, ln2_b, rel_bias):
    b, t, _ = x.shape
    n = b * t
    for l in range(DEPTH):
        o_a, o_c, o_s, o_w, o_m, hb = _token_mixer(x, mem, w_in[l], diff_lambda[l], diff_subln[l], cmp_pe[l],
                                                   cmp_w1[l], cmp_w2[l], w_mem_kv[l], w_branch[l], rel_bias)
        mixed = _merge(o_a, o_c, o_s, o_w, o_m, hb, w_branch[l].astype(BF16))
        x1, x1b = _out_proj_ln(mixed, w_out[l].astype(BF16), x.reshape(n, D_MODEL),
                               ln1_g[l].reshape(1, D_MODEL), ln1_b[l].reshape(1, D_MODEL))
        y = _peer(x1, x1b, peer_wq[l], peer_keys[l], peer_u[l], peer_v[l])
        x = _residual_ln(x1, y, ln2_g[l].reshape(1, D_MODEL), ln2_b[l].reshape(1, D_MODEL)).reshape(b, t, D_MODEL)
    return x
```

```python
import functools
import math

import numpy as np
import jax
import jax.numpy as jnp
from jax import lax
from jax.experimental import pallas as pl
from jax.experimental.pallas import tpu as pltpu

F32 = jnp.float32
BF16 = jnp.bfloat16

D_MODEL = 2048
A_HEADS, A_DQK, A_DV = 8, 64, 128
B_HEADS, B_GROUPS, B_HPG, B_DH = 16, 4, 4, 64
CMP_LEN, CMP_STRIDE, CMP_HIDDEN = 32, 16, 256
SEL_BLOCK, SEL_TOPN, WINDOW = 64, 16, 512
M_HEADS, M_DH = 4, 256
REL_BUCKETS, REL_MAX_DIST = 32, 1024
PEER_HEADS, PEER_NKEYS, PEER_DKEY, PEER_TOPK = 8, 128, 256, 16
MIX_W = 1024
LN_EPS = 1e-5
FORCE = 1e9
DEPTH = 1
ALPHA = (2 * DEPTH) ** 0.25
LAMBDA_INIT = 0.8 - 0.6 * math.exp(-0.3 * 0)
LOG2E = math.log2(math.e)

NEG = -1e30
ATT_TILE = 256
LANES = 128
ONES_ROWS = 16
VMEM_LIMIT = 56 * 1024 * 1024


def _cparams(sem):
    return pltpu.CompilerParams(dimension_semantics=sem, vmem_limit_bytes=VMEM_LIMIT)


def _dot_nt(a, b):
    return lax.dot_general(a, b, (((1,), (1,)), ((), ())), preferred_element_type=F32)


def _mm_body(x_ref, w_ref, o_ref):
    o_ref[...] = jnp.dot(x_ref[...], w_ref[...], preferred_element_type=F32).astype(o_ref.dtype)


def _matmul(x, w, out_dtype, tm, tn, name):
    m, k = x.shape
    n = w.shape[1]
    return pl.pallas_call(
        _mm_body,
        grid=(m // tm, n // tn),
        in_specs=[pl.BlockSpec((tm, k), lambda i, j: (i, 0)),
                  pl.BlockSpec((k, tn), lambda i, j: (0, j))],
        out_specs=pl.BlockSpec((tm, tn), lambda i, j: (i, j)),
        out_shape=jax.ShapeDtypeStruct((m, n), out_dtype),
        compiler_params=_cparams(("parallel", "arbitrary")),
        name=name,
    )(x, w)


def _rel_bucket(dist):
    n = jnp.maximum(dist, 0)
    max_exact = REL_BUCKETS // 2
    nf = jnp.maximum(n, 1).astype(jnp.float32)
    large = max_exact + (jnp.log(nf / max_exact) / math.log(REL_MAX_DIST / max_exact)
                         * (REL_BUCKETS - max_exact)).astype(jnp.int32)
    large = jnp.minimum(large, REL_BUCKETS - 1)
    return jnp.where(n < max_exact, n, large)


def _bias_tiles(tab1d, t, first, n_tiles, max_dist, hpr):
    L = ATT_TILE
    m = np.arange(2 * L)
    off = np.where(m <= L, m, m - 2 * L)
    d = (first + np.arange(n_tiles))[:, None] * L + off[None, :]
    ok = (d >= 0) & (d < max_dist)
    h = tab1d.shape[0]
    rp = jnp.where(ok[:, None], tab1d.T[np.clip(d, 0, t - 1)].transpose(0, 2, 1), NEG)
    return pl.pallas_call(
        _toeplitz_body,
        grid=(n_tiles, h),
        in_specs=[pl.BlockSpec((None, None, 1, 2 * L), lambda c, hd: (c, hd, 0, 0))],
        out_specs=pl.BlockSpec((None, None, L, L), lambda c, hd: (c, hd // hpr, 0, hd % hpr)),
        out_shape=jax.ShapeDtypeStruct((n_tiles, h // hpr, L, hpr * L), F32),
        compiler_params=_cparams(("parallel", "parallel")),
        name="bias_tiles",
    )(rp.reshape(n_tiles, h, 1, 2 * L))


def _toeplitz_body(rp_ref, o_ref):
    L = o_ref.shape[0]
    rows = jnp.broadcast_to(rp_ref[...], (L, 2 * L))
    o_ref[...] = pltpu.roll(rows, 0, 1, stride=1, stride_axis=0)[:, :L]


def _flash_loop(lo, hi, scores, values, m_ref, acc_ref):
    def ahead(kt):
        s = scores(kt)
        return s, jnp.max(s, axis=0, keepdims=True)

    def body(kt, carry):
        s, mx = carry
        nxt = ahead(jnp.minimum(kt + 1, hi - 1))
        m_prev = m_ref[...]
        m_new = jnp.maximum(m_prev, mx)
        alpha = jnp.exp2(m_prev - m_new)
        p = jnp.exp2(s - m_new[0:1])
        acc_ref[...] = alpha[0:1] * acc_ref[...] + jnp.dot(values(kt), p.astype(BF16), preferred_element_type=F32)
        m_ref[...] = m_new
        return nxt

    lax.fori_loop(lo, hi, body, ahead(lo))


def _diff_body(q_ref, k_ref, vt_ref, bias_ref, lam_ref, g_ref, o_ref, m_sc, acc_sc, *, n_bt):
    L = ATT_TILE
    tq = 2 * L
    qi = pl.program_id(2)
    m_sc[...] = jnp.full(m_sc.shape, NEG, F32)
    acc_sc[...] = jnp.zeros(acc_sc.shape, F32)
    qt = q_ref[...].astype(F32).T.astype(BF16)
    zero = jnp.zeros((A_DQK, tq), BF16)
    q_cat = jnp.concatenate([jnp.concatenate([qt[:A_DQK], zero], axis=0),
                             jnp.concatenate([zero, qt[A_DQK:]], axis=0)], axis=1)

    def scores(kt):
        off = pl.multiple_of(kt * L, L)
        k = k_ref[pl.ds(off, L), :]
        d0 = 2 * qi - kt
        bias = jnp.concatenate([bias_ref[jnp.minimum(d0 + 1, n_bt - 1)],
                                bias_ref[jnp.minimum(d0 + 2, n_bt - 1)]], axis=1)
        s = jnp.dot(k, q_cat, preferred_element_type=F32)
        return jnp.concatenate([s[:, :tq] + bias, s[:, tq:] + bias], axis=1)

    _flash_loop(0, 2 * qi + 2, scores, lambda kt: vt_ref[kt], m_sc, acc_sc)

    lp = lam_ref[...]
    lam = (jnp.exp(jnp.sum(lp[0:1] * lp[1:2], axis=1, keepdims=True))
           - jnp.exp(jnp.sum(lp[2:3] * lp[3:4], axis=1, keepdims=True)) + LAMBDA_INIT)
    acc = acc_sc[...]
    o0 = acc[:A_DV, :tq] / jnp.maximum(acc[A_DV:A_DV + 1, :tq], 1e-30)
    o1 = acc[:A_DV, tq:] / jnp.maximum(acc[A_DV:A_DV + 1, tq:], 1e-30)
    o = o0 - lam * o1
    g = jnp.concatenate([g_ref[...]] * (tq // LANES), axis=1)
    o = o * lax.rsqrt(jnp.mean(o * o, axis=0, keepdims=True) + LN_EPS) * g
    o_ref[...] = (o * (1.0 - LAMBDA_INIT)).T.astype(o_ref.dtype)


def _diff_attention(ha, v_t, bias, lam_params, subln, b, t):
    L = ATT_TILE
    tq = 2 * L
    n_bt = bias.shape[0]
    return pl.pallas_call(
        functools.partial(_diff_body, n_bt=n_bt),
        grid=(b, A_HEADS, t // tq),
        in_specs=[
            pl.BlockSpec((None, tq, 128), lambda bi, h, qi: (bi, qi, 8 + h)),
            pl.BlockSpec((None, t, 128), lambda bi, h, qi: (bi, 0, 16 + h)),
            pl.BlockSpec((None, None, t // L, A_DV + ONES_ROWS, L), lambda bi, h, qi: (bi, h, 0, 0, 0)),
            pl.BlockSpec((n_bt, None, L, L), lambda bi, h, qi: (0, h, 0, 0)),
            pl.BlockSpec((4, A_DQK), lambda bi, h, qi: (0, 0)),
            pl.BlockSpec((A_DV, LANES), lambda bi, h, qi: (0, 0)),
        ],
        out_specs=pl.BlockSpec((None, tq, 128), lambda bi, h, qi: (bi, qi, h)),
        out_shape=jax.ShapeDtypeStruct((b, t, MIX_W), F32),
        scratch_shapes=[pltpu.VMEM((8, 2 * tq), F32), pltpu.VMEM((A_DV + ONES_ROWS, 2 * tq), F32)],
        compiler_params=_cparams(("parallel", "parallel", "arbitrary")),
        name="diff_attention",
    )(ha, ha, v_t, bias, lam_params, subln)


def _compress_body(r_ref, w1_ref, w2_ref, pe_ref, o_ref):
    outs = []
    half = CMP_STRIDE * B_DH
    for kv in range(2):
        r = r_ref[kv]
        w1 = w1_ref[kv]
        a = jnp.dot(r, w1[:half], preferred_element_type=F32)
        bb = jnp.dot(r, w1[half:], preferred_element_type=F32)
        bb = jnp.concatenate([bb[1:], bb[:1]], axis=0)
        pw = jnp.dot(pe_ref[kv], w1, preferred_element_type=F32)[0:1]
        hdn = jax.nn.gelu(a + bb + pw)
        outs.append(jnp.dot(hdn.astype(BF16), w2_ref[kv], preferred_element_type=F32))
    o_ref[...] = jnp.concatenate(outs, axis=1)


def _compress(raw, w1, w2, pe):
    b, g, _, nr, wdt = raw.shape
    return pl.pallas_call(
        _compress_body,
        grid=(b, g),
        in_specs=[
            pl.BlockSpec((None, None, 2, nr, wdt), lambda bi, gi: (bi, gi, 0, 0, 0)),
            pl.BlockSpec((2, 2 * wdt, CMP_HIDDEN), lambda bi, gi: (0, 0, 0)),
            pl.BlockSpec((2, CMP_HIDDEN, B_DH), lambda bi, gi: (0, 0, 0)),
            pl.BlockSpec((2, 8, 2 * wdt), lambda bi, gi: (0, 0, 0)),
        ],
        out_specs=pl.BlockSpec((None, None, nr, 2 * B_DH), lambda bi, gi: (bi, gi, 0, 0)),
        out_shape=jax.ShapeDtypeStruct((b, g, nr, 2 * B_DH), F32),
        compiler_params=_cparams(("parallel", "parallel")),
        name="nsa_compress",
    )(raw, w1, w2, pe)


def _stack_heads(q):
    return jnp.concatenate([q[:, i * B_DH:(i + 1) * B_DH] for i in range(B_HPG)], axis=0)


def _gated_unstack(o, gl_ref, L):
    gate = jax.nn.sigmoid(gl_ref[...])
    return jnp.concatenate([o[i * L:(i + 1) * L] * gate[:, i:i + 1] for i in range(B_HPG)], axis=1)


def _topk_mask_axis0(v, k):
    r, n = v.shape
    iota = lax.broadcasted_iota(jnp.int32, (r, n), 0).astype(F32)

    def body(_, c):
        v, sel = c
        mx = jnp.max(v, axis=0, keepdims=True)
        idx = jnp.min(jnp.where(v == mx, iota, float(r)), axis=0, keepdims=True)
        hit = iota == idx
        return jnp.where(hit, -jnp.inf, v), jnp.where(hit, 1.0, sel)

    _, sel = lax.fori_loop(0, k, body, (v, jnp.zeros((r, n), F32)))
    return sel


def _cmp_body(q_ref, kvc_ref, ov_ref, gl_ref, o_ref, mask_ref, *, n_c, n_top):
    L = ATT_TILE
    qi = pl.program_id(2)
    ncp = kvc_ref.shape[0]
    n_sel = mask_ref.shape[0]
    qs = _stack_heads(q_ref[...])
    kvc = kvc_ref[...]
    kc = kvc[:, :B_DH].astype(BF16)
    vc = kvc[:, B_DH:].astype(BF16)
    s = _dot_nt(qs, kc).reshape(B_HPG, L, ncp)
    tpos = qi * L + lax.broadcasted_iota(jnp.int32, (L, 1), 0)
    cidx = lax.broadcasted_iota(jnp.int32, (1, ncp), 1)
    valid = jnp.where(cidx < n_c, cidx * CMP_STRIDE + (CMP_LEN - 1), jnp.int32(2 ** 30)) <= tpos
    s = jnp.where(valid[None], s, NEG)
    mx = jnp.max(s, axis=2, keepdims=True)
    e = jnp.where(valid[None], jnp.exp2(s - mx), 0.0)
    p = e / jnp.maximum(jnp.sum(e, axis=2, keepdims=True), 1e-30)
    o = jnp.dot(p.reshape(B_HPG * L, ncp).astype(BF16), vc, preferred_element_type=F32)
    o_ref[...] = _gated_unstack(o, gl_ref, L)

    psum = p[0] + p[1] + p[2] + p[3]
    ov_t = ov_ref[...]
    imp = jnp.zeros((ov_t.shape[0], L), F32)
    rem = psum
    for _ in range(3):
        part = rem.astype(BF16)
        imp = imp + _dot_nt(ov_t, part)
        rem = rem - part.astype(F32)
    imp = imp[:n_sel]
    blk = lax.broadcasted_iota(jnp.int32, (n_sel, 1), 0)
    tcol = qi * L + lax.broadcasted_iota(jnp.int32, (1, L), 1)
    cur = jnp.right_shift(tcol, int(math.log2(SEL_BLOCK)))
    imp = jnp.where(blk * SEL_BLOCK > tcol, -FORCE, imp)
    imp = jnp.where(blk == 0, FORCE, imp)
    imp = jnp.where(blk == cur, FORCE, imp)
    imp = jnp.where(blk == cur - 1, FORCE, imp)
    mask_ref[...] = jnp.where(_topk_mask_axis0(imp, n_top) > 0.0, 0.0, NEG).astype(mask_ref.dtype)


def _nsa_compressed(ha, kvc, overlap_t, glog, b, t, n_c, n_top):
    L = ATT_TILE
    ncp = kvc.shape[2]
    n_sel = t // SEL_BLOCK
    return pl.pallas_call(
        functools.partial(_cmp_body, n_c=n_c, n_top=n_top),
        grid=(b, B_GROUPS, t // L),
        in_specs=[
            pl.BlockSpec((None, L, 256), lambda bi, g, qi: (bi, qi, 16 + g)),
            pl.BlockSpec((None, None, ncp, 2 * B_DH), lambda bi, g, qi: (bi, g, 0, 0)),
            pl.BlockSpec(overlap_t.shape, lambda bi, g, qi: (0, 0)),
            pl.BlockSpec((None, None, L, B_HPG), lambda bi, g, qi: (bi, g, qi, 0)),
        ],
        out_specs=[
            pl.BlockSpec((None, L, 256), lambda bi, g, qi: (bi, qi, g)),
            pl.BlockSpec((None, None, n_sel, L), lambda bi, g, qi: (bi, g, 0, qi)),
        ],
        out_shape=[jax.ShapeDtypeStruct((b, t, MIX_W), F32),
                   jax.ShapeDtypeStruct((b, B_GROUPS, n_sel, t), BF16)],
        compiler_params=_cparams(("parallel", "parallel", "arbitrary")),
        name="nsa_compressed_select",
    )(ha, kvc, overlap_t, glog)


def _nsa_body(*refs, mode, n_bt):
    if mode == "sel":
        q_ref, ka_ref, vt_ref, bias_ref, gl_ref, mask_ref, o_ref, m_sc, acc_sc = refs
    else:
        q_ref, ka_ref, vt_ref, bias_ref, gl_ref, o_ref, m_sc, acc_sc = refs
    L = ATT_TILE
    qi = pl.program_id(2)
    m_sc[...] = jnp.full(m_sc.shape, NEG, F32)
    acc_sc[...] = jnp.zeros(acc_sc.shape, F32)
    qt = q_ref[...].astype(F32).T.astype(BF16)
    if mode == "sel":
        n_sel = mask_ref.shape[0]
        qm = mask_ref[...]
        if n_sel < B_DH:
            qm = jnp.concatenate([qm, jnp.zeros((B_DH - n_sel, L), BF16)], axis=0)
    else:
        qm = jnp.zeros((B_DH, L), BF16)
    q_aug_t = jnp.concatenate([jnp.concatenate([qt[i * B_DH:(i + 1) * B_DH], qm], axis=0) for i in range(B_HPG)],
                              axis=1)

    def scores(kt):
        off = pl.multiple_of(kt * L, L)
        bias = bias_ref[jnp.minimum(qi - kt, n_bt - 1)]
        return jnp.dot(ka_ref[pl.ds(off, L), :], q_aug_t, preferred_element_type=F32) + bias

    lo = 0 if mode == "sel" else jnp.maximum(qi - (n_bt - 1), 0)
    _flash_loop(lo, qi + 1, scores, lambda kt: vt_ref[kt], m_sc, acc_sc)
    acc = acc_sc[...]
    o = acc[:B_DH] / jnp.maximum(acc[B_DH:B_DH + 1], 1e-30)
    gate = jax.nn.sigmoid(gl_ref[...])
    o = jnp.concatenate([o[:, i * L:(i + 1) * L] * gate[i:i + 1] for i in range(B_HPG)], axis=0)
    o_ref[...] = o.T


def _nsa_branch(ha, k_aug, v_t, bias, glog_t, mask_t, b, t, mode):
    L = ATT_TILE
    n_bt = bias.shape[0]
    in_specs = [
        pl.BlockSpec((None, L, 256), lambda bi, g, qi: (bi, qi, 16 + g)),
        pl.BlockSpec((None, t, LANES), lambda bi, g, qi: (bi, 0, g)),
        pl.BlockSpec((None, None, t // L, B_DH + ONES_ROWS, L), lambda bi, g, qi: (bi, g, 0, 0, 0)),
        pl.BlockSpec((n_bt, None, L, B_HPG * L), lambda bi, g, qi: (0, g, 0, 0)),
        pl.BlockSpec((None, None, B_HPG, L), lambda bi, g, qi: (bi, g, 0, qi)),
    ]
    args = [ha, k_aug, v_t, bias, glog_t]
    if mode == "sel":
        n_sel = mask_t.shape[2]
        in_specs.append(pl.BlockSpec((None, None, n_sel, L), lambda bi, g, qi: (bi, g, 0, qi)))
        args.append(mask_t)
    return pl.pallas_call(
        functools.partial(_nsa_body, mode=mode, n_bt=n_bt),
        grid=(b, B_GROUPS, t // L),
        in_specs=in_specs,
        out_specs=pl.BlockSpec((None, L, 256), lambda bi, g, qi: (bi, qi, g)),
        out_shape=jax.ShapeDtypeStruct((b, t, MIX_W), F32),
        scratch_shapes=[pltpu.VMEM((8, B_HPG * L), F32), pltpu.VMEM((B_DH + ONES_ROWS, B_HPG * L), F32)],
        compiler_params=_cparams(("parallel", "parallel", "arbitrary")),
        name="nsa_" + mode,
    )(*args)


def _mem_body(q_ref, k_ref, v_ref, o_ref):
    q = q_ref[...]
    outs = []
    for h in range(M_HEADS):
        sl = slice(h * M_DH, (h + 1) * M_DH)
        s = _dot_nt(q[:, sl], k_ref[:, sl])
        e = jnp.exp2(s - jnp.max(s, axis=1, keepdims=True))
        p = e / jnp.sum(e, axis=1, keepdims=True)
        outs.append(jnp.dot(p.astype(BF16), v_ref[:, sl], preferred_element_type=F32))
    o_ref[...] = jnp.concatenate(outs, axis=1)


def _memory_attention(ha, memkv, b, t, n_mem):
    tq = 512
    w = M_HEADS * M_DH
    return pl.pallas_call(
        _mem_body,
        grid=(b, t // tq),
        in_specs=[
            pl.BlockSpec((None, tq, w), lambda bi, qi: (bi, qi, 0)),
            pl.BlockSpec((n_mem, w), lambda bi, qi: (bi, 0)),
            pl.BlockSpec((n_mem, w), lambda bi, qi: (bi, 1)),
        ],
        out_specs=pl.BlockSpec((None, tq, w), lambda bi, qi: (bi, qi, 0)),
        out_shape=jax.ShapeDtypeStruct((b, t, w), F32),
        compiler_params=_cparams(("parallel", "arbitrary")),
        name="memory_attention",
    )(ha, memkv, memkv)


def _merge_body(oa_ref, oc_ref, os_ref, ow_ref, om_ref, gl_ref, w_ref, o_ref):
    branches = (oa_ref[...], oc_ref[...] + os_ref[...] + ow_ref[...], om_ref[...])
    acc = jnp.zeros(o_ref.shape, F32)
    for n in range(3):
        up = jnp.dot(branches[n].astype(BF16), w_ref[n], preferred_element_type=F32)
        acc = acc + jax.nn.sigmoid(gl_ref[:, n * D_MODEL:(n + 1) * D_MODEL]) * up
    o_ref[...] = acc.astype(o_ref.dtype)


def _merge(o_a, o_c, o_s, o_w, o_m, hb, w_branch):
    n = o_a.shape[0]
    tm = 256
    row = lambda i: (i, 0)
    return pl.pallas_call(
        _merge_body,
        grid=(n // tm,),
        in_specs=[pl.BlockSpec((tm, MIX_W), row)] * 5 + [
            pl.BlockSpec((tm, 3 * D_MODEL), row),
            pl.BlockSpec((3, MIX_W, D_MODEL), lambda i: (0, 0, 0)),
        ],
        out_specs=pl.BlockSpec((tm, D_MODEL), row),
        out_shape=jax.ShapeDtypeStruct((n, D_MODEL), BF16),
        compiler_params=_cparams(("parallel",)),
        name="branch_merge",
    )(o_a, o_c, o_s, o_w, o_m, hb, w_branch)


def _layer_norm(z, g, b):
    mu = jnp.mean(z, axis=1, keepdims=True)
    zc = z - mu
    var = jnp.mean(zc * zc, axis=1, keepdims=True)
    return zc * lax.rsqrt(var + LN_EPS) * g + b


def _out_ln_body(y_ref, w_ref, x_ref, g_ref, b_ref, o_ref, ob_ref):
    y = jnp.dot(y_ref[...], w_ref[...], preferred_element_type=F32)
    o = _layer_norm(ALPHA * x_ref[...] + y, g_ref[...], b_ref[...])
    o_ref[...] = o
    ob_ref[...] = o.astype(BF16)


def _out_proj_ln(mixed, w_out, x, g, b):
    n = x.shape[0]
    tm = 512
    row = lambda i: (i, 0)
    const = lambda i: (0, 0)
    return pl.pallas_call(
        _out_ln_body,
        grid=(n // tm,),
        in_specs=[pl.BlockSpec((tm, D_MODEL), row), pl.BlockSpec((D_MODEL, D_MODEL), const),
                  pl.BlockSpec((tm, D_MODEL), row), pl.BlockSpec((1, D_MODEL), const),
                  pl.BlockSpec((1, D_MODEL), const)],
        out_specs=[pl.BlockSpec((tm, D_MODEL), row), pl.BlockSpec((tm, D_MODEL), row)],
        out_shape=[jax.ShapeDtypeStruct((n, D_MODEL), F32), jax.ShapeDtypeStruct((n, D_MODEL), BF16)],
        compiler_params=_cparams(("parallel",)),
        name="out_proj_ln1",
    )(mixed, w_out, x, g, b)


def _res_ln_body(x_ref, y_ref, g_ref, b_ref, o_ref):
    o_ref[...] = _layer_norm(ALPHA * x_ref[...] + y_ref[...], g_ref[...], b_ref[...])


def _residual_ln(x, y, g, b):
    n = x.shape[0]
    tm = 512
    row = lambda i: (i, 0)
    const = lambda i: (0, 0)
    return pl.pallas_call(
        _res_ln_body,
        grid=(n // tm,),
        in_specs=[pl.BlockSpec((tm, D_MODEL), row), pl.BlockSpec((tm, D_MODEL), row),
                  pl.BlockSpec((1, D_MODEL), const), pl.BlockSpec((1, D_MODEL), const)],
        out_specs=pl.BlockSpec((tm, D_MODEL), row),
        out_shape=jax.ShapeDtypeStruct((n, D_MODEL), F32),
        compiler_params=_cparams(("parallel",)),
        name="residual_ln2",
    )(x, y, g, b)


def _topk_axis0(v, k):
    r, n = v.shape
    iota = lax.broadcasted_iota(jnp.int32, (r, n), 0).astype(F32)
    slot = lax.broadcasted_iota(jnp.int32, (k, n), 0)

    def body(it, c):
        v, vals, idxs = c
        mx = jnp.max(v, axis=0, keepdims=True)
        idx = jnp.min(jnp.where(v == mx, iota, float(r)), axis=0, keepdims=True)
        v = jnp.where(iota == idx, -jnp.inf, v)
        return v, jnp.where(slot == it, mx, vals), jnp.where(slot == it, idx, idxs)

    _, vals, idxs = lax.fori_loop(0, k, body, (v, jnp.zeros((k, n), F32), jnp.zeros((k, n), F32)))
    return vals, idxs


def _pick_rows(table, pos, k):
    out = jnp.zeros(pos.shape, F32)
    for a in range(k):
        out = out + jnp.where(pos == float(a), table[a:a + 1], 0.0)
    return out


def _route_body(q_ref, keys_ref, ei_ref, ej_ref, g_ref):
    k = PEER_TOPK
    scores = _dot_nt(keys_ref[...], q_ref[...])
    v0, i0 = _topk_axis0(scores[:PEER_NKEYS], k)
    v1, i1 = _topk_axis0(scores[PEER_NKEYS:], k)
    counts = [k // (a + 1) for a in range(k)]
    starts = np.cumsum([0] + counts[:-1])
    pad = (-sum(counts)) % 8
    comb = jnp.concatenate([v0[a:a + 1] + v1[:counts[a]] for a in range(k)]
                           + [jnp.full((pad, v0.shape[1]), -jnp.inf, F32)], axis=0)
    sf, pos = _topk_axis0(comb, k)
    pa = jnp.zeros(pos.shape, F32)
    pb = pos
    for a in range(1, k):
        later = pos >= float(starts[a])
        pa = pa + jnp.where(later, 1.0, 0.0)
        pb = pb - jnp.where(later, float(counts[a - 1]), 0.0)
    ei_ref[...] = _pick_rows(i0, pa, k)
    ej_ref[...] = _pick_rows(i1, pb, k)
    e = jnp.exp(sf - jnp.max(sf, axis=0, keepdims=True))
    g_ref[...] = e / jnp.sum(e, axis=0, keepdims=True)


def _peer_route(q, keys):
    n = q.shape[0]
    tn = 256
    out = jax.ShapeDtypeStruct((PEER_HEADS, PEER_TOPK, n), F32)
    ospec = pl.BlockSpec((None, PEER_TOPK, tn), lambda i, h: (h, 0, i))
    return pl.pallas_call(
        _route_body,
        grid=(n // tn, PEER_HEADS),
        in_specs=[pl.BlockSpec((tn, PEER_DKEY), lambda i, h: (i, h)),
                  pl.BlockSpec((None, 2 * PEER_NKEYS, PEER_DKEY), lambda i, h: (h, 0, 0))],
        out_specs=[ospec, ospec, ospec],
        out_shape=[out, out, out],
        compiler_params=_cparams(("parallel", "arbitrary")),
        name="peer_route",
    )(q, keys)


def _gate_body(ei_ref, ej_ref, g_ref, o_ref):
    tb = ei_ref.shape[0]
    nk = PEER_NKEYS
    iota = lax.broadcasted_iota(jnp.int32, (tb, nk, ei_ref.shape[2]), 1).astype(F32)
    rows = jnp.where(iota == ei_ref[...], 1.0, 0.0).astype(BF16)
    cols = jnp.where(iota == ej_ref[...], g_ref[...], 0.0).astype(BF16)
    gm = lax.dot_general(rows, cols, (((2,), (2,)), ((0,), (0,))), preferred_element_type=F32)
    o_ref[...] = jnp.swapaxes(gm, 0, 1).astype(o_ref.dtype)


def _gate_matrix(ei, ej, g):
    n, _, slots = ei.shape
    tb = 64
    spec = pl.BlockSpec((tb, 1, slots), lambda i: (i, 0, 0))
    return pl.pallas_call(
        _gate_body,
        grid=(n // tb,),
        in_specs=[spec, spec, spec],
        out_specs=pl.BlockSpec((PEER_NKEYS, tb, PEER_NKEYS), lambda i: (0, i, 0)),
        out_shape=jax.ShapeDtypeStruct((PEER_NKEYS, n, PEER_NKEYS), BF16),
        compiler_params=_cparams(("parallel",)),
        name="peer_gate_matrix",
    )(ei, ej, g)


def _expert_body(x_ref, u_ref, g_ref, v_ref, o_ref):
    @pl.when(pl.program_id(1) == 0)
    def _():
        o_ref[...] = jnp.zeros(o_ref.shape, F32)

    hid = _dot_nt(x_ref[...], u_ref[...])
    gate = jnp.concatenate([g_ref[i] for i in range(g_ref.shape[0])], axis=1)
    act = (jax.nn.gelu(hid) * gate.astype(F32)).astype(BF16)
    o_ref[...] += jnp.dot(act, v_ref[...], preferred_element_type=F32)


def _peer_experts(xb, u, gm, v):
    n = xb.shape[0]
    ne = u.shape[0]
    tn, te = 512, 1024
    return pl.pallas_call(
        _expert_body,
        grid=(n // tn, ne // te),
        in_specs=[pl.BlockSpec((tn, D_MODEL), lambda i, j: (i, 0)),
                  pl.BlockSpec((te, D_MODEL), lambda i, j: (j, 0)),
                  pl.BlockSpec((te // PEER_NKEYS, tn, PEER_NKEYS), lambda i, j: (j, i, 0)),
                  pl.BlockSpec((te, D_MODEL), lambda i, j: (j, 0))],
        out_specs=pl.BlockSpec((tn, D_MODEL), lambda i, j: (i, 0)),
        out_shape=jax.ShapeDtypeStruct((n, D_MODEL), F32),
        compiler_params=_cparams(("parallel", "arbitrary")),
        name="peer_experts",
    )(xb, u, gm, v)


def _token_mixer(x, mem, w_in, diff_lambda, diff_subln, cmp_pe, cmp_w1, cmp_w2, w_mem_kv, w_branch, rel_bias):
    b, t, _ = x.shape
    n = b * t
    L = ATT_TILE
    xb = x.reshape(n, D_MODEL).astype(BF16)

    w_kv = w_in[:, 4096:5632].reshape(D_MODEL, 3, 2, B_GROUPS, B_DH).transpose(0, 1, 3, 2, 4).reshape(D_MODEL, 1536)
    qscale = np.ones((6656,), np.float32)
    qscale[0:1024] = M_DH ** -0.5 * LOG2E
    qscale[1024:2048] = A_DQK ** -0.5 * LOG2E
    qscale[4096:5120] = B_DH ** -0.5 * LOG2E
    w_a = (jnp.concatenate([w_in[:, 5680:6704], w_in[:, :4096], w_kv], axis=1) * qscale).astype(BF16)
    w_b = jnp.concatenate([w_in[:, 6704:], w_in[:, 5632:5680], jnp.zeros((D_MODEL, 464), F32)], axis=1).astype(BF16)
    ha = _matmul(xb, w_a, BF16, 1024, 1664, "in_proj_a")
    hb = _matmul(xb, w_b, F32, 1024, 1664, "in_proj_b")
    ha3 = ha.reshape(b, t, ha.shape[1])

    tab1d = rel_bias[_rel_bucket(jnp.arange(t))].T * LOG2E
    n_far = min(t // L, REL_MAX_DIST // L + 2)
    bias_a = _bias_tiles(tab1d[:A_HEADS], t, -1, n_far + 1, t, 1)
    bias_b = tab1d[A_HEADS:]
    bias_sel = _bias_tiles(bias_b, t, 0, n_far, t, B_HPG)
    n_win = min(t // L, WINDOW // L + 1)
    bias_win = _bias_tiles(bias_b, t, 0, n_win, WINDOW, B_HPG)

    def values_t(v):
        heads, dv = v.shape[2:]
        vt = v.reshape(b, t // L, L, heads, dv).transpose(0, 3, 1, 4, 2)
        return jnp.concatenate([vt, jnp.ones((b, heads, t // L, ONES_ROWS, L), BF16)], axis=3)

    o_a = _diff_attention(ha3, values_t(ha3[:, :, 3072:4096].reshape(b, t, A_HEADS, A_DV)), bias_a, diff_lambda,
                          jnp.broadcast_to(diff_subln[:, None], (A_DV, LANES)), b, t)

    glog = hb[:, 6144:6192].reshape(b, t, 3, B_GROUPS, B_HPG).transpose(2, 0, 3, 1, 4)
    glog_t = glog.transpose(0, 1, 2, 4, 3)
    nr = t // CMP_STRIDE
    raw = ha3[:, :, 5120:5632].reshape(b, nr, CMP_STRIDE, B_GROUPS, 2, B_DH)
    raw = raw.transpose(0, 3, 4, 1, 2, 5).reshape(b, B_GROUPS, 2, nr, CMP_STRIDE * B_DH)
    pe = jnp.broadcast_to(cmp_pe.reshape(2, 1, CMP_LEN * B_DH), (2, 8, CMP_LEN * B_DH)).astype(BF16)
    kvc = _compress(raw, cmp_w1.astype(BF16), cmp_w2.astype(BF16), pe)
    n_c = (t - CMP_LEN) // CMP_STRIDE + 1
    n_sel = t // SEL_BLOCK
    cidx = np.arange(nr)[:, None] * CMP_STRIDE + np.arange(CMP_LEN)[None, :]
    overlap = (cidx[:, :, None] // SEL_BLOCK == np.arange(n_sel)[None, None, :]).astype(np.float32).mean(axis=1)
    overlap[n_c:] = 0.0
    overlap_t = np.zeros((LANES, nr), np.float32)
    overlap_t[:n_sel] = overlap.T
    o_c, sel_mask = _nsa_compressed(ha3, kvc, jnp.asarray(overlap_t, BF16), glog[0], b, t, n_c,
                                    min(SEL_TOPN, n_sel))
    kv_sel = ha3[:, :, 5632:6144].reshape(b, t, B_GROUPS, 2, B_DH)
    kv_win = ha3[:, :, 6144:6656].reshape(b, t, B_GROUPS, 2, B_DH)
    assert n_sel <= B_DH
    hot = (np.arange(t)[:, None] // SEL_BLOCK == np.arange(B_DH)[None, :]).astype(np.float32)
    hot = jnp.broadcast_to(jnp.asarray(hot, BF16)[None, :, None, :], (b, t, B_GROUPS, B_DH))
    ka_sel = jnp.concatenate([kv_sel[:, :, :, 0], hot], axis=3).reshape(b, t, B_GROUPS * LANES)
    ka_win = jnp.concatenate([kv_win[:, :, :, 0], jnp.zeros_like(hot)], axis=3).reshape(b, t, B_GROUPS * LANES)
    o_s = _nsa_branch(ha3, ka_sel, values_t(kv_sel[:, :, :, 1]), bias_sel, glog_t[1], sel_mask, b, t, "sel")
    o_w = _nsa_branch(ha3, ka_win, values_t(kv_win[:, :, :, 1]), bias_win, glog_t[2], None, b, t, "win")

    n_mem = mem.shape[1]
    memkv = _matmul(mem.reshape(b * n_mem, D_MODEL).astype(BF16), w_mem_kv.astype(BF16), BF16,
                    b * n_mem, 1024, "mem_kv_proj")
    o_m = _memory_attention(ha3, memkv, b, t, n_mem)

    flat = lambda o: o.reshape(n, MIX_W)
    return flat(o_a), flat(o_c), flat(o_s), flat(o_w), flat(o_m), hb


def _peer(x1, x1b, peer_wq, peer_keys, peer_u, peer_v):
    n = x1.shape[0]
    q = _matmul(x1b, peer_wq.astype(BF16), BF16, 1024, 1024, "peer_query")
    zk = jnp.zeros((PEER_HEADS, PEER_NKEYS, PEER_DKEY // 2), F32)
    keys_bd = jnp.concatenate([jnp.concatenate([peer_keys[:, 0], zk], axis=2),
                               jnp.concatenate([zk, peer_keys[:, 1]], axis=2)], axis=1).astype(BF16)
    ei, ej, gate = _peer_route(q, keys_bd)
    slots = lambda a: a.reshape(PEER_HEADS * PEER_TOPK, n).T.reshape(n, 1, PEER_HEADS * PEER_TOPK)
    gm = _gate_matrix(slots(ei), slots(ej), slots(gate))
    return _peer_experts(x1b, peer_u.astype(BF16), gm, peer_v.astype(BF16))


def kernel(x, mem, w_in, diff_lambda, diff_subln, cmp_pe, cmp_w1, cmp_w2, w_mem_kv, w_branch, w_out, ln1_g, ln1_b,
           peer_wq, peer_keys, peer_u, peer_v, ln2_g, ln2_b, rel_bias):
    b, t, _ = x.shape
    n = b * t
    for l in range(DEPTH):
        o_a, o_c, o_s, o_w, o_m, hb = _token_mixer(x, mem, w_in[l], diff_lambda[l], diff_subln[l], cmp_pe[l],
                                                   cmp_w1[l], cmp_w2[l], w_mem_kv[l], w_branch[l], rel_bias)
        mixed = _merge(o_a, o_c, o_s, o_w, o_m, hb, w_branch[l].astype(BF16))
        x1, x1b = _out_proj_ln(mixed, w_out[l].astype(BF16), x.reshape(n, D_MODEL),
                               ln1_g[l].reshape(1, D_MODEL), ln1_b[l].reshape(1, D_MODEL))
        y = _peer(x1, x1b, peer_wq[l], peer_keys[l], peer_u[l], peer_v[l])
        x = _residual_ln(x1, y, ln2_g[l].reshape(1, D_MODEL), ln2_b[l].reshape(1, D_MODEL)).reshape(b, t, D_MODEL)
    return x
```

```python
import functools
import math

import numpy as np
import jax
import jax.numpy as jnp
from jax import lax
from jax.experimental import pallas as pl
from jax.experimental.pallas import tpu as pltpu

F32 = jnp.float32
BF16 = jnp.bfloat16

D_MODEL = 2048
A_HEADS, A_DQK, A_DV = 8, 64, 128
B_HEADS, B_GROUPS, B_HPG, B_DH = 16, 4, 4, 64
CMP_LEN, CMP_STRIDE, CMP_HIDDEN = 32, 16, 256
SEL_BLOCK, SEL_TOPN, WINDOW = 64, 16, 512
M_HEADS, M_DH = 4, 256
REL_BUCKETS, REL_MAX_DIST = 32, 1024
PEER_HEADS, PEER_NKEYS, PEER_DKEY, PEER_TOPK = 8, 128, 256, 16
MIX_W = 1024
LN_EPS = 1e-5
FORCE = 1e9
DEPTH = 1
ALPHA = (2 * DEPTH) ** 0.25
LAMBDA_INIT = 0.8 - 0.6 * math.exp(-0.3 * 0)
LOG2E = math.log2(math.e)

NEG = -1e30
ATT_TILE = 256
LANES = 128
ONES_ROWS = 16
VMEM_LIMIT = 56 * 1024 * 1024


def _cparams(sem):
    return pltpu.CompilerParams(dimension_semantics=sem, vmem_limit_bytes=VMEM_LIMIT)


def _dot_nt(a, b):
    return lax.dot_general(a, b, (((1,), (1,)), ((), ())), preferred_element_type=F32)


def _mm_body(x_ref, w_ref, o_ref):
    o_ref[...] = jnp.dot(x_ref[...], w_ref[...], preferred_element_type=F32).astype(o_ref.dtype)


def _matmul(x, w, out_dtype, tm, tn, name):
    m, k = x.shape
    n = w.shape[1]
    return pl.pallas_call(
        _mm_body,
        grid=(m // tm, n // tn),
        in_specs=[pl.BlockSpec((tm, k), lambda i, j: (i, 0)),
                  pl.BlockSpec((k, tn), lambda i, j: (0, j))],
        out_specs=pl.BlockSpec((tm, tn), lambda i, j: (i, j)),
        out_shape=jax.ShapeDtypeStruct((m, n), out_dtype),
        compiler_params=_cparams(("parallel", "arbitrary")),
        name=name,
    )(x, w)


W_A_COLS = 6656
W_B_COLS = 6656


def _w_prep_body(w_ref, wa_ref, wb_ref):
    def put(dst_ref, d0, pieces):
        vals = [w_ref[:, s0:s0 + n] * c if c != 1.0 else w_ref[:, s0:s0 + n] for s0, n, c in pieces]
        val = vals[0] if len(vals) == 1 else jnp.concatenate(vals, axis=1)
        dst_ref[:, d0:d0 + val.shape[1]] = val.astype(dst_ref.dtype)

    put(wa_ref, 0, [(5680, 1024, M_DH ** -0.5 * LOG2E)])
    put(wa_ref, 1024, [(0, 1024, A_DQK ** -0.5 * LOG2E)])
    put(wa_ref, 2048, [(1024, 2048, 1.0)])
    put(wa_ref, 4096, [(3072, 1024, B_DH ** -0.5 * LOG2E)])
    for br in range(3):
        for g in range(B_GROUPS):
            src = 4096 + br * 2 * B_GROUPS * B_DH + g * B_DH
            put(wa_ref, 5120 + (br * B_GROUPS + g) * 2 * B_DH, [(src, B_DH, 1.0), (src + B_GROUPS * B_DH, B_DH, 1.0)])
    put(wb_ref, 0, [(6704, 6144, 1.0)])
    pad = W_B_COLS - 6144 - 48
    wb_ref[:, 6144:] = jnp.concatenate([w_ref[:, 5632:5680], jnp.zeros((w_ref.shape[0], pad), F32)],
                                       axis=1).astype(wb_ref.dtype)


def _w_prep(w_in):
    k, c = w_in.shape
    tr = 128
    return pl.pallas_call(
        _w_prep_body,
        grid=(k // tr,),
        in_specs=[pl.BlockSpec((tr, c), lambda i: (i, 0))],
        out_specs=[pl.BlockSpec((tr, W_A_COLS), lambda i: (i, 0)), pl.BlockSpec((tr, W_B_COLS), lambda i: (i, 0))],
        out_shape=[jax.ShapeDtypeStruct((k, W_A_COLS), BF16), jax.ShapeDtypeStruct((k, W_B_COLS), BF16)],
        compiler_params=_cparams(("parallel",)),
        name="w_in_regroup",
    )(w_in)


def _rel_bucket(dist):
    n = jnp.maximum(dist, 0)
    max_exact = REL_BUCKETS // 2
    nf = jnp.maximum(n, 1).astype(jnp.float32)
    large = max_exact + (jnp.log(nf / max_exact) / math.log(REL_MAX_DIST / max_exact)
                         * (REL_BUCKETS - max_exact)).astype(jnp.int32)
    large = jnp.minimum(large, REL_BUCKETS - 1)
    return jnp.where(n < max_exact, n, large)


def _bias_tiles(tab1d, t, first, n_tiles, max_dist, hpr):
    L = ATT_TILE
    m = np.arange(2 * L)
    off = np.where(m <= L, m, m - 2 * L)
    d = (first + np.arange(n_tiles))[:, None] * L + off[None, :]
    ok = (d >= 0) & (d < max_dist)
    h = tab1d.shape[0]
    rp = jnp.where(ok[:, None], tab1d.T[np.clip(d, 0, t - 1)].transpose(0, 2, 1), NEG)
    return pl.pallas_call(
        functools.partial(_toeplitz_body, hpr=hpr),
        grid=(n_tiles,),
        in_specs=[pl.BlockSpec((None, h, 1, 2 * L), lambda c: (c, 0, 0, 0))],
        out_specs=pl.BlockSpec((None, h // hpr, L, hpr * L), lambda c: (c, 0, 0, 0)),
        out_shape=jax.ShapeDtypeStruct((n_tiles, h // hpr, L, hpr * L), F32),
        compiler_params=_cparams(("parallel",)),
        name="bias_tiles",
    )(rp.reshape(n_tiles, h, 1, 2 * L))


def _toeplitz_body(rp_ref, o_ref, *, hpr):
    L = o_ref.shape[1]
    for hd in range(rp_ref.shape[0]):
        rows = jnp.broadcast_to(rp_ref[hd], (L, 2 * L))
        tile = pltpu.roll(rows, 0, 1, stride=1, stride_axis=0)[:, :L]
        o_ref[hd // hpr, :, (hd % hpr) * L:(hd % hpr + 1) * L] = tile


def _flash_loop(lo, hi, scores, values, m_ref, acc_ref):
    def ahead(kt):
        s = scores(kt)
        return s, jnp.max(s, axis=0, keepdims=True)

    def body(kt, carry):
        s, mx = carry
        nxt = ahead(jnp.minimum(kt + 1, hi - 1))
        m_prev = m_ref[...]
        m_new = jnp.maximum(m_prev, mx)
        alpha = jnp.exp2(m_prev - m_new)
        p = jnp.exp2(s - m_new[0:1])
        acc_ref[...] = alpha[0:1] * acc_ref[...] + jnp.dot(values(kt), p.astype(BF16), preferred_element_type=F32)
        m_ref[...] = m_new
        return nxt

    lax.fori_loop(lo, hi, body, ahead(lo))


def _diff_body(q_ref, k_ref, vt_ref, bias_ref, lam_ref, g_ref, o_ref, m_sc, acc_sc, *, n_bt):
    L = ATT_TILE
    tq = 2 * L
    qi = pl.program_id(2)
    m_sc[...] = jnp.full(m_sc.shape, NEG, F32)
    acc_sc[...] = jnp.zeros(acc_sc.shape, F32)
    qt = q_ref[...].astype(F32).T.astype(BF16)
    zero = jnp.zeros((A_DQK, tq), BF16)
    q_cat = jnp.concatenate([jnp.concatenate([qt[:A_DQK], zero], axis=0),
                             jnp.concatenate([zero, qt[A_DQK:]], axis=0)], axis=1)

    def scores(kt):
        off = pl.multiple_of(kt * L, L)
        k = k_ref[pl.ds(off, L), :]
        d0 = 2 * qi - kt
        bias = jnp.concatenate([bias_ref[jnp.minimum(d0 + 1, n_bt - 1)],
                                bias_ref[jnp.minimum(d0 + 2, n_bt - 1)]], axis=1)
        s = jnp.dot(k, q_cat, preferred_element_type=F32)
        return jnp.concatenate([s[:, :tq] + bias, s[:, tq:] + bias], axis=1)

    _flash_loop(0, 2 * qi + 2, scores, lambda kt: vt_ref[kt], m_sc, acc_sc)

    lp = lam_ref[...]
    lam = (jnp.exp(jnp.sum(lp[0:1] * lp[1:2], axis=1, keepdims=True))
           - jnp.exp(jnp.sum(lp[2:3] * lp[3:4], axis=1, keepdims=True)) + LAMBDA_INIT)
    acc = acc_sc[...]
    o0 = acc[:A_DV, :tq] / jnp.maximum(acc[A_DV:A_DV + 1, :tq], 1e-30)
    o1 = acc[:A_DV, tq:] / jnp.maximum(acc[A_DV:A_DV + 1, tq:], 1e-30)
    o = o0 - lam * o1
    g = jnp.concatenate([g_ref[...]] * (tq // LANES), axis=1)
    o = o * lax.rsqrt(jnp.mean(o * o, axis=0, keepdims=True) + LN_EPS) * g
    o_ref[...] = (o * (1.0 - LAMBDA_INIT)).T.astype(o_ref.dtype)


def _diff_attention(ha, v_t, bias, lam_params, subln, b, t):
    L = ATT_TILE
    tq = 2 * L
    n_bt = bias.shape[0]
    return pl.pallas_call(
        functools.partial(_diff_body, n_bt=n_bt),
        grid=(b, A_HEADS, t // tq),
        in_specs=[
            pl.BlockSpec((None, tq, 128), lambda bi, h, qi: (bi, qi, 8 + h)),
            pl.BlockSpec((None, t, 128), lambda bi, h, qi: (bi, 0, 16 + h)),
            pl.BlockSpec((None, None, t // L, A_DV + ONES_ROWS, L), lambda bi, h, qi: (bi, h, 0, 0, 0)),
            pl.BlockSpec((n_bt, None, L, L), lambda bi, h, qi: (0, h, 0, 0)),
            pl.BlockSpec((4, A_DQK), lambda bi, h, qi: (0, 0)),
            pl.BlockSpec((A_DV, LANES), lambda bi, h, qi: (0, 0)),
        ],
        out_specs=pl.BlockSpec((None, tq, 128), lambda bi, h, qi: (bi, qi, h)),
        out_shape=jax.ShapeDtypeStruct((b, t, MIX_W), BF16),
        scratch_shapes=[pltpu.VMEM((8, 2 * tq), F32), pltpu.VMEM((A_DV + ONES_ROWS, 2 * tq), F32)],
        compiler_params=_cparams(("parallel", "parallel", "arbitrary")),
        name="diff_attention",
    )(ha, ha, v_t, bias, lam_params, subln)


def _compress_body(kv_ref, w1_ref, w2_ref, pe_ref, o_ref):
    nr = o_ref.shape[0]
    first = jnp.zeros((nr, 2 * CMP_HIDDEN), F32)
    second = jnp.zeros((nr, 2 * CMP_HIDDEN), F32)
    for l in range(CMP_STRIDE):
        rows = kv_ref[pl.ds(l, nr, stride=CMP_STRIDE), :].astype(BF16)
        first = first + jnp.dot(rows, w1_ref[l], preferred_element_type=F32)
        second = second + jnp.dot(rows, w1_ref[CMP_STRIDE + l], preferred_element_type=F32)
    pw = jnp.zeros((8, 2 * CMP_HIDDEN), F32)
    for l in range(CMP_LEN):
        pw = pw + jnp.dot(pe_ref[l], w1_ref[l], preferred_element_type=F32)
    second = jnp.concatenate([second[1:], second[:1]], axis=0)
    hdn = jax.nn.gelu(first + second + pw[0:1])
    o_ref[...] = jnp.dot(hdn.astype(BF16), w2_ref[...], preferred_element_type=F32)


def _compress(kv, w1_bd, w2_bd, pe_bd):
    b, t, _ = kv.shape
    nr = t // CMP_STRIDE
    return pl.pallas_call(
        _compress_body,
        grid=(b, B_GROUPS),
        in_specs=[
            pl.BlockSpec((None, t, LANES), lambda bi, gi: (bi, 0, gi)),
            pl.BlockSpec(w1_bd.shape, lambda bi, gi: (0, 0, 0)),
            pl.BlockSpec(w2_bd.shape, lambda bi, gi: (0, 0)),
            pl.BlockSpec(pe_bd.shape, lambda bi, gi: (0, 0, 0)),
        ],
        out_specs=pl.BlockSpec((None, None, nr, 2 * B_DH), lambda bi, gi: (bi, gi, 0, 0)),
        out_shape=jax.ShapeDtypeStruct((b, B_GROUPS, nr, 2 * B_DH), F32),
        compiler_params=_cparams(("parallel", "parallel")),
        name="nsa_compress",
    )(kv, w1_bd, w2_bd, pe_bd)


def _stack_heads(q):
    return jnp.concatenate([q[:, i * B_DH:(i + 1) * B_DH] for i in range(B_HPG)], axis=0)


def _gated_unstack(o, gl_ref, L):
    gate = jax.nn.sigmoid(gl_ref[...])
    return jnp.concatenate([o[i * L:(i + 1) * L] * gate[:, i:i + 1] for i in range(B_HPG)], axis=1)


def _topk_mask_axis0(v, k):
    r, n = v.shape
    iota = lax.broadcasted_iota(jnp.int32, (r, n), 0).astype(F32)

    def body(_, c):
        v, sel = c
        mx = jnp.max(v, axis=0, keepdims=True)
        idx = jnp.min(jnp.where(v == mx, iota, float(r)), axis=0, keepdims=True)
        hit = iota == idx
        return jnp.where(hit, -jnp.inf, v), jnp.where(hit, 1.0, sel)

    _, sel = lax.fori_loop(0, k, body, (v, jnp.zeros((r, n), F32)))
    return sel


def _cmp_body(q_ref, kvc_ref, ov_ref, gl_ref, o_ref, mask_ref, *, n_c, n_top):
    L = ATT_TILE
    qi = pl.program_id(2)
    ncp = kvc_ref.shape[0]
    n_sel = mask_ref.shape[0]
    qs = _stack_heads(q_ref[...])
    kvc = kvc_ref[...]
    kc = kvc[:, :B_DH].astype(BF16)
    vc = kvc[:, B_DH:].astype(BF16)
    s = _dot_nt(qs, kc).reshape(B_HPG, L, ncp)
    tpos = qi * L + lax.broadcasted_iota(jnp.int32, (L, 1), 0)
    cidx = lax.broadcasted_iota(jnp.int32, (1, ncp), 1)
    valid = jnp.where(cidx < n_c, cidx * CMP_STRIDE + (CMP_LEN - 1), jnp.int32(2 ** 30)) <= tpos
    s = jnp.where(valid[None], s, NEG)
    mx = jnp.max(s, axis=2, keepdims=True)
    e = jnp.where(valid[None], jnp.exp2(s - mx), 0.0)
    p = e / jnp.maximum(jnp.sum(e, axis=2, keepdims=True), 1e-30)
    o = jnp.dot(p.reshape(B_HPG * L, ncp).astype(BF16), vc, preferred_element_type=F32)
    o_ref[...] = _gated_unstack(o, gl_ref, L)

    psum = p[0] + p[1] + p[2] + p[3]
    ov_t = ov_ref[...]
    imp = jnp.zeros((ov_t.shape[0], L), F32)
    rem = psum
    for _ in range(3):
        part = rem.astype(BF16)
        imp = imp + _dot_nt(ov_t, part)
        rem = rem - part.astype(F32)
    imp = imp[:n_sel]
    blk = lax.broadcasted_iota(jnp.int32, (n_sel, 1), 0)
    tcol = qi * L + lax.broadcasted_iota(jnp.int32, (1, L), 1)
    cur = jnp.right_shift(tcol, int(math.log2(SEL_BLOCK)))
    imp = jnp.where(blk * SEL_BLOCK > tcol, -FORCE, imp)
    imp = jnp.where(blk == 0, FORCE, imp)
    imp = jnp.where(blk == cur, FORCE, imp)
    imp = jnp.where(blk == cur - 1, FORCE, imp)
    mask_ref[...] = jnp.where(_topk_mask_axis0(imp, n_top) > 0.0, 0.0, NEG).astype(mask_ref.dtype)


def _nsa_compressed(ha, kvc, overlap_t, glog, b, t, n_c, n_top):
    L = ATT_TILE
    ncp = kvc.shape[2]
    n_sel = t // SEL_BLOCK
    return pl.pallas_call(
        functools.partial(_cmp_body, n_c=n_c, n_top=n_top),
        grid=(b, B_GROUPS, t // L),
        in_specs=[
            pl.BlockSpec((None, L, 256), lambda bi, g, qi: (bi, qi, 16 + g)),
            pl.BlockSpec((None, None, ncp, 2 * B_DH), lambda bi, g, qi: (bi, g, 0, 0)),
            pl.BlockSpec(overlap_t.shape, lambda bi, g, qi: (0, 0)),
            pl.BlockSpec((None, None, L, B_HPG), lambda bi, g, qi: (bi, g, qi, 0)),
        ],
        out_specs=[
            pl.BlockSpec((None, L, 256), lambda bi, g, qi: (bi, qi, g)),
            pl.BlockSpec((None, None, n_sel, L), lambda bi, g, qi: (bi, g, 0, qi)),
        ],
        out_shape=[jax.ShapeDtypeStruct((b, t, MIX_W), F32),
                   jax.ShapeDtypeStruct((b, B_GROUPS, n_sel, t), BF16)],
        compiler_params=_cparams(("parallel", "parallel", "arbitrary")),
        name="nsa_compressed_select",
    )(ha, kvc, overlap_t, glog)


def _nsa_body(*refs, mode, n_bt):
    if mode == "sel":
        q_ref, ka_ref, vt_ref, bias_ref, gl_ref, mask_ref, o_ref, m_sc, acc_sc = refs
    else:
        q_ref, ka_ref, vt_ref, bias_ref, gl_ref, o_ref, m_sc, acc_sc = refs
    L = ATT_TILE
    qi = pl.program_id(2)
    m_sc[...] = jnp.full(m_sc.shape, NEG, F32)
    acc_sc[...] = jnp.zeros(acc_sc.shape, F32)
    qt = q_ref[...].astype(F32).T.astype(BF16)
    if mode == "sel":
        n_sel = mask_ref.shape[0]
        qm = mask_ref[...]
        if n_sel < B_DH:
            qm = jnp.concatenate([qm, jnp.zeros((B_DH - n_sel, L), BF16)], axis=0)
    else:
        qm = jnp.zeros((B_DH, L), BF16)
    q_aug_t = jnp.concatenate([jnp.concatenate([qt[i * B_DH:(i + 1) * B_DH], qm], axis=0) for i in range(B_HPG)],
                              axis=1)

    def scores(kt):
        off = pl.multiple_of(kt * L, L)
        bias = bias_ref[jnp.minimum(qi - kt, n_bt - 1)]
        return jnp.dot(ka_ref[pl.ds(off, L), :], q_aug_t, preferred_element_type=F32) + bias

    lo = 0 if mode == "sel" else jnp.maximum(qi - (n_bt - 1), 0)
    _flash_loop(lo, qi + 1, scores, lambda kt: vt_ref[kt], m_sc, acc_sc)
    acc = acc_sc[...]
    o = acc[:B_DH] / jnp.maximum(acc[B_DH:B_DH + 1], 1e-30)
    gate = jax.nn.sigmoid(gl_ref[...])
    o = jnp.concatenate([o[:, i * L:(i + 1) * L] * gate[i:i + 1] for i in range(B_HPG)], axis=0)
    o_ref[...] = o.T


def _nsa_branch(ha, k_aug, v_t, bias, glog_t, mask_t, b, t, mode):
    L = ATT_TILE
    n_bt = bias.shape[0]
    in_specs = [
        pl.BlockSpec((None, L, 256), lambda bi, g, qi: (bi, qi, 16 + g)),
        pl.BlockSpec((None, t, LANES), lambda bi, g, qi: (bi, 0, g)),
        pl.BlockSpec((None, None, t // L, B_DH + ONES_ROWS, L), lambda bi, g, qi: (bi, g, 0, 0, 0)),
        pl.BlockSpec((n_bt, None, L, B_HPG * L), lambda bi, g, qi: (0, g, 0, 0)),
        pl.BlockSpec((None, None, B_HPG, L), lambda bi, g, qi: (bi, g, 0, qi)),
    ]
    args = [ha, k_aug, v_t, bias, glog_t]
    if mode == "sel":
        n_sel = mask_t.shape[2]
        in_specs.append(pl.BlockSpec((None, None, n_sel, L), lambda bi, g, qi: (bi, g, 0, qi)))
        args.append(mask_t)
    return pl.pallas_call(
        functools.partial(_nsa_body, mode=mode, n_bt=n_bt),
        grid=(b, B_GROUPS, t // L),
        in_specs=in_specs,
        out_specs=pl.BlockSpec((None, L, 256), lambda bi, g, qi: (bi, qi, g)),
        out_shape=jax.ShapeDtypeStruct((b, t, MIX_W), F32),
        scratch_shapes=[pltpu.VMEM((8, B_HPG * L), F32), pltpu.VMEM((B_DH + ONES_ROWS, B_HPG * L), F32)],
        compiler_params=_cparams(("parallel", "parallel", "arbitrary")),
        name="nsa_" + mode,
    )(*args)


def _mem_body(q_ref, k_ref, v_ref, o_ref):
    q = q_ref[...]
    outs = []
    for h in range(M_HEADS):
        sl = slice(h * M_DH, (h + 1) * M_DH)
        s = _dot_nt(q[:, sl], k_ref[:, sl])
        e = jnp.exp2(s - jnp.max(s, axis=1, keepdims=True))
        p = e / jnp.sum(e, axis=1, keepdims=True)
        outs.append(jnp.dot(p.astype(BF16), v_ref[:, sl], preferred_element_type=F32))
    o_ref[...] = jnp.concatenate(outs, axis=1).astype(o_ref.dtype)


def _memory_attention(ha, memkv, b, t, n_mem):
    tq = 512
    w = M_HEADS * M_DH
    return pl.pallas_call(
        _mem_body,
        grid=(b, t // tq),
        in_specs=[
            pl.BlockSpec((None, tq, w), lambda bi, qi: (bi, qi, 0)),
            pl.BlockSpec((n_mem, w), lambda bi, qi: (bi, 0)),
            pl.BlockSpec((n_mem, w), lambda bi, qi: (bi, 1)),
        ],
        out_specs=pl.BlockSpec((None, tq, w), lambda bi, qi: (bi, qi, 0)),
        out_shape=jax.ShapeDtypeStruct((b, t, w), BF16),
        compiler_params=_cparams(("parallel", "arbitrary")),
        name="memory_attention",
    )(ha, memkv, memkv)


def _merge_body(oa_ref, oc_ref, os_ref, ow_ref, om_ref, gl_ref, w_ref, o_ref):
    branches = (oa_ref[...], oc_ref[...] + os_ref[...] + ow_ref[...], om_ref[...])
    acc = jnp.zeros(o_ref.shape, F32)
    for n in range(3):
        up = jnp.dot(branches[n].astype(BF16), w_ref[n], preferred_element_type=F32)
        acc = acc + jax.nn.sigmoid(gl_ref[:, n * D_MODEL:(n + 1) * D_MODEL]) * up
    o_ref[...] = acc.astype(o_ref.dtype)


def _merge(o_a, o_c, o_s, o_w, o_m, hb, w_branch):
    n = o_a.shape[0]
    tm = 256
    row = lambda i: (i, 0)
    return pl.pallas_call(
        _merge_body,
        grid=(n // tm,),
        in_specs=[pl.BlockSpec((tm, MIX_W), row)] * 5 + [
            pl.BlockSpec((tm, 3 * D_MODEL), row),
            pl.BlockSpec((3, MIX_W, D_MODEL), lambda i: (0, 0, 0)),
        ],
        out_specs=pl.BlockSpec((tm, D_MODEL), row),
        out_shape=jax.ShapeDtypeStruct((n, D_MODEL), BF16),
        compiler_params=_cparams(("parallel",)),
        name="branch_merge",
    )(o_a, o_c, o_s, o_w, o_m, hb, w_branch)


def _layer_norm(z, g, b):
    mu = jnp.mean(z, axis=1, keepdims=True)
    zc = z - mu
    var = jnp.mean(zc * zc, axis=1, keepdims=True)
    return zc * lax.rsqrt(var + LN_EPS) * g + b


def _out_ln_body(y_ref, w_ref, x_ref, g_ref, b_ref, o_ref, ob_ref):
    y = jnp.dot(y_ref[...], w_ref[...], preferred_element_type=F32)
    o = _layer_norm(ALPHA * x_ref[...] + y, g_ref[...], b_ref[...])
    o_ref[...] = o
    ob_ref[...] = o.astype(BF16)


def _out_proj_ln(mixed, w_out, x, g, b):
    n = x.shape[0]
    tm = 512
    row = lambda i: (i, 0)
    const = lambda i: (0, 0)
    return pl.pallas_call(
        _out_ln_body,
        grid=(n // tm,),
        in_specs=[pl.BlockSpec((tm, D_MODEL), row), pl.BlockSpec((D_MODEL, D_MODEL), const),
                  pl.BlockSpec((tm, D_MODEL), row), pl.BlockSpec((1, D_MODEL), const),
                  pl.BlockSpec((1, D_MODEL), const)],
        out_specs=[pl.BlockSpec((tm, D_MODEL), row), pl.BlockSpec((tm, D_MODEL), row)],
        out_shape=[jax.ShapeDtypeStruct((n, D_MODEL), F32), jax.ShapeDtypeStruct((n, D_MODEL), BF16)],
        compiler_params=_cparams(("parallel",)),
        name="out_proj_ln1",
    )(mixed, w_out, x, g, b)


def _res_ln_body(x_ref, y_ref, g_ref, b_ref, o_ref):
    o_ref[...] = _layer_norm(ALPHA * x_ref[...] + y_ref[...], g_ref[...], b_ref[...])


def _residual_ln(x, y, g, b):
    n = x.shape[0]
    tm = 512
    row = lambda i: (i, 0)
    const = lambda i: (0, 0)
    return pl.pallas_call(
        _res_ln_body,
        grid=(n // tm,),
        in_specs=[pl.BlockSpec((tm, D_MODEL), row), pl.BlockSpec((tm, D_MODEL), row),
                  pl.BlockSpec((1, D_MODEL), const), pl.BlockSpec((1, D_MODEL), const)],
        out_specs=pl.BlockSpec((tm, D_MODEL), row),
        out_shape=jax.ShapeDtypeStruct((n, D_MODEL), F32),
        compiler_params=_cparams(("parallel",)),
        name="residual_ln2",
    )(x, y, g, b)


def _topk_axis0(v, k):
    r, n = v.shape
    iota = lax.broadcasted_iota(jnp.int32, (r, n), 0).astype(F32)
    slot = lax.broadcasted_iota(jnp.int32, (k, n), 0)

    def body(it, c):
        v, vals, idxs = c
        mx = jnp.max(v, axis=0, keepdims=True)
        idx = jnp.min(jnp.where(v == mx, iota, float(r)), axis=0, keepdims=True)
        v = jnp.where(iota == idx, -jnp.inf, v)
        return v, jnp.where(slot == it, mx, vals), jnp.where(slot == it, idx, idxs)

    _, vals, idxs = lax.fori_loop(0, k, body, (v, jnp.zeros((k, n), F32), jnp.zeros((k, n), F32)))
    return vals, idxs


def _pick_rows(table, pos, k):
    out = jnp.zeros(pos.shape, F32)
    for a in range(k):
        out = out + jnp.where(pos == float(a), table[a:a + 1], 0.0)
    return out


def _route_body(q_ref, keys_ref, ei_ref, ej_ref, g_ref):
    k = PEER_TOPK
    scores = _dot_nt(keys_ref[...], q_ref[...])
    v0, i0 = _topk_axis0(scores[:PEER_NKEYS], k)
    v1, i1 = _topk_axis0(scores[PEER_NKEYS:], k)
    counts = [k // (a + 1) for a in range(k)]
    starts = np.cumsum([0] + counts[:-1])
    pad = (-sum(counts)) % 8
    comb = jnp.concatenate([v0[a:a + 1] + v1[:counts[a]] for a in range(k)]
                           + [jnp.full((pad, v0.shape[1]), -jnp.inf, F32)], axis=0)
    sf, pos = _topk_axis0(comb, k)
    pa = jnp.zeros(pos.shape, F32)
    pb = pos
    for a in range(1, k):
        later = pos >= float(starts[a])
        pa = pa + jnp.where(later, 1.0, 0.0)
        pb = pb - jnp.where(later, float(counts[a - 1]), 0.0)
    ei_ref[...] = _pick_rows(i0, pa, k)
    ej_ref[...] = _pick_rows(i1, pb, k)
    e = jnp.exp(sf - jnp.max(sf, axis=0, keepdims=True))
    g_ref[...] = e / jnp.sum(e, axis=0, keepdims=True)


def _peer_route(q, keys):
    n = q.shape[0]
    tn = 256
    out = jax.ShapeDtypeStruct((PEER_HEADS, PEER_TOPK, n), F32)
    ospec = pl.BlockSpec((None, PEER_TOPK, tn), lambda i, h: (h, 0, i))
    return pl.pallas_call(
        _route_body,
        grid=(n // tn, PEER_HEADS),
        in_specs=[pl.BlockSpec((tn, PEER_DKEY), lambda i, h: (i, h)),
                  pl.BlockSpec((None, 2 * PEER_NKEYS, PEER_DKEY), lambda i, h: (h, 0, 0))],
        out_specs=[ospec, ospec, ospec],
        out_shape=[out, out, out],
        compiler_params=_cparams(("parallel", "arbitrary")),
        name="peer_route",
    )(q, keys)


def _gate_body(ei_ref, ej_ref, g_ref, o_ref):
    tb = ei_ref.shape[0]
    nk = PEER_NKEYS
    iota = lax.broadcasted_iota(jnp.int32, (tb, nk, ei_ref.shape[2]), 1).astype(F32)
    rows = jnp.where(iota == ei_ref[...], 1.0, 0.0).astype(BF16)
    cols = jnp.where(iota == ej_ref[...], g_ref[...], 0.0).astype(BF16)
    gm = lax.dot_general(rows, cols, (((2,), (2,)), ((0,), (0,))), preferred_element_type=F32)
    o_ref[...] = jnp.swapaxes(gm, 0, 1).astype(o_ref.dtype)


def _gate_matrix(ei, ej, g):
    n, _, slots = ei.shape
    tb = 64
    spec = pl.BlockSpec((tb, 1, slots), lambda i: (i, 0, 0))
    return pl.pallas_call(
        _gate_body,
        grid=(n // tb,),
        in_specs=[spec, spec, spec],
        out_specs=pl.BlockSpec((PEER_NKEYS, tb, PEER_NKEYS), lambda i: (0, i, 0)),
        out_shape=jax.ShapeDtypeStruct((PEER_NKEYS, n, PEER_NKEYS), BF16),
        compiler_params=_cparams(("parallel",)),
        name="peer_gate_matrix",
    )(ei, ej, g)


def _expert_body(x_ref, u_ref, g_ref, v_ref, o_ref):
    @pl.when(pl.program_id(1) == 0)
    def _():
        o_ref[...] = jnp.zeros(o_ref.shape, F32)

    hid = _dot_nt(x_ref[...], u_ref[...].astype(BF16))
    gate = jnp.concatenate([g_ref[i] for i in range(g_ref.shape[0])], axis=1)
    act = (jax.nn.gelu(hid) * gate.astype(F32)).astype(BF16)
    o_ref[...] += jnp.dot(act, v_ref[...].astype(BF16), preferred_element_type=F32)


def _peer_experts(xb, u, gm, v):
    n = xb.shape[0]
    ne = u.shape[0]
    tn, te = 1024, 512
    return pl.pallas_call(
        _expert_body,
        grid=(n // tn, ne // te),
        in_specs=[pl.BlockSpec((tn, D_MODEL), lambda i, j: (i, 0)),
                  pl.BlockSpec((te, D_MODEL), lambda i, j: (j, 0)),
                  pl.BlockSpec((te // PEER_NKEYS, tn, PEER_NKEYS), lambda i, j: (j, i, 0)),
                  pl.BlockSpec((te, D_MODEL), lambda i, j: (j, 0))],
        out_specs=pl.BlockSpec((tn, D_MODEL), lambda i, j: (i, 0)),
        out_shape=jax.ShapeDtypeStruct((n, D_MODEL), F32),
        compiler_params=_cparams(("parallel", "arbitrary")),
        name="peer_experts",
    )(xb, u, gm, v)


def _token_mixer(x, mem, w_in, diff_lambda, diff_subln, cmp_pe, cmp_w1, cmp_w2, w_mem_kv, w_branch, rel_bias):
    b, t, _ = x.shape
    n = b * t
    L = ATT_TILE
    xb = x.reshape(n, D_MODEL).astype(BF16)

    w_a, w_b = _w_prep(w_in)
    ha = _matmul(xb, w_a, BF16, 1024, 1664, "in_proj_a")
    hb = _matmul(xb, w_b, F32, 1024, 1664, "in_proj_b")
    ha3 = ha.reshape(b, t, ha.shape[1])

    tab1d = rel_bias[_rel_bucket(jnp.arange(t))].T * LOG2E
    n_far = min(t // L, REL_MAX_DIST // L + 2)
    bias_a = _bias_tiles(tab1d[:A_HEADS], t, -1, n_far + 1, t, 1)
    bias_b = tab1d[A_HEADS:]
    bias_sel = _bias_tiles(bias_b, t, 0, n_far, t, B_HPG)
    n_win = min(t // L, WINDOW // L + 1)
    bias_win = _bias_tiles(bias_b, t, 0, n_win, WINDOW, B_HPG)

    def values_t(v):
        heads, dv = v.shape[2:]
        vt = v.reshape(b, t // L, L, heads, dv).transpose(0, 3, 1, 4, 2)
        return jnp.concatenate([vt, jnp.ones((b, heads, t // L, ONES_ROWS, L), BF16)], axis=3)

    o_a = _diff_attention(ha3, values_t(ha3[:, :, 3072:4096].reshape(b, t, A_HEADS, A_DV)), bias_a, diff_lambda,
                          jnp.broadcast_to(diff_subln[:, None], (A_DV, LANES)), b, t)

    glog = hb[:, 6144:6192].reshape(b, t, 3, B_GROUPS, B_HPG).transpose(2, 0, 3, 1, 4)
    glog_t = glog.transpose(0, 1, 2, 4, 3)
    nr = t // CMP_STRIDE
    w1 = cmp_w1.reshape(2, CMP_LEN, B_DH, CMP_HIDDEN)
    z1 = jnp.zeros_like(w1[0])
    w1_bd = jnp.concatenate([jnp.concatenate([w1[0], z1], axis=2), jnp.concatenate([z1, w1[1]], axis=2)],
                            axis=1).astype(BF16)
    z2 = jnp.zeros_like(cmp_w2[0])
    w2_bd = jnp.concatenate([jnp.concatenate([cmp_w2[0], z2], axis=1), jnp.concatenate([z2, cmp_w2[1]], axis=1)],
                            axis=0).astype(BF16)
    pe_bd = jnp.broadcast_to(jnp.concatenate([cmp_pe[0], cmp_pe[1]], axis=1)[:, None, :],
                             (CMP_LEN, 8, 2 * B_DH)).astype(BF16)
    kvc = _compress(ha3[:, :, 5120:5632].astype(F32), w1_bd, w2_bd, pe_bd)
    n_c = (t - CMP_LEN) // CMP_STRIDE + 1
    n_sel = t // SEL_BLOCK
    cidx = np.arange(nr)[:, None] * CMP_STRIDE + np.arange(CMP_LEN)[None, :]
    overlap = (cidx[:, :, None] // SEL_BLOCK == np.arange(n_sel)[None, None, :]).astype(np.float32).mean(axis=1)
    overlap[n_c:] = 0.0
    overlap_t = np.zeros((LANES, nr), np.float32)
    overlap_t[:n_sel] = overlap.T
    o_c, sel_mask = _nsa_compressed(ha3, kvc, jnp.asarray(overlap_t, BF16), glog[0], b, t, n_c,
                                    min(SEL_TOPN, n_sel))
    kv_sel = ha3[:, :, 5632:6144].reshape(b, t, B_GROUPS, 2, B_DH)
    kv_win = ha3[:, :, 6144:6656].reshape(b, t, B_GROUPS, 2, B_DH)
    assert n_sel <= B_DH
    hot = (np.arange(t)[:, None] // SEL_BLOCK == np.arange(B_DH)[None, :]).astype(np.float32)
    hot = jnp.broadcast_to(jnp.asarray(hot, BF16)[None, :, None, :], (b, t, B_GROUPS, B_DH))
    ka_sel = jnp.concatenate([kv_sel[:, :, :, 0], hot], axis=3).reshape(b, t, B_GROUPS * LANES)
    ka_win = jnp.concatenate([kv_win[:, :, :, 0], jnp.zeros_like(hot)], axis=3).reshape(b, t, B_GROUPS * LANES)
    o_s = _nsa_branch(ha3, ka_sel, values_t(kv_sel[:, :, :, 1]), bias_sel, glog_t[1], sel_mask, b, t, "sel")
    o_w = _nsa_branch(ha3, ka_win, values_t(kv_win[:, :, :, 1]), bias_win, glog_t[2], None, b, t, "win")

    n_mem = mem.shape[1]
    memkv = _matmul(mem.reshape(b * n_mem, D_MODEL).astype(BF16), w_mem_kv.astype(BF16), BF16,
                    b * n_mem, 1024, "mem_kv_proj")
    o_m = _memory_attention(ha3, memkv, b, t, n_mem)

    flat = lambda o: o.reshape(n, MIX_W)
    return flat(o_a), flat(o_c), flat(o_s), flat(o_w), flat(o_m), hb


def _peer(x1, x1b, peer_wq, peer_keys, peer_u, peer_v):
    n = x1.shape[0]
    q = _matmul(x1b, peer_wq.astype(BF16), BF16, 1024, 1024, "peer_query")
    zk = jnp.zeros((PEER_HEADS, PEER_NKEYS, PEER_DKEY // 2), F32)
    keys_bd = jnp.concatenate([jnp.concatenate([peer_keys[:, 0], zk], axis=2),
                               jnp.concatenate([zk, peer_keys[:, 1]], axis=2)], axis=1).astype(BF16)
    ei, ej, gate = _peer_route(q, keys_bd)
    slots = lambda a: a.reshape(PEER_HEADS * PEER_TOPK, n).T.reshape(n, 1, PEER_HEADS * PEER_TOPK)
    gm = _gate_matrix(slots(ei), slots(ej), slots(gate))
    return _peer_experts(x1b, peer_u, gm, peer_v)


def kernel(x, mem, w_in, diff_lambda, diff_subln, cmp_pe, cmp_w1, cmp_w2, w_mem_kv, w_branch, w_out, ln1_g, ln1_b,
           peer_wq, peer_keys, peer_u, peer_v, ln2_g, ln2_b, rel_bias):
    b, t, _ = x.shape
    n = b * t
    for l in range(DEPTH):
        o_a, o_c, o_s, o_w, o_m, hb = _token_mixer(x, mem, w_in[l], diff_lambda[l], diff_subln[l], cmp_pe[l],
                                                   cmp_w1[l], cmp_w2[l], w_mem_kv[l], w_branch[l], rel_bias)
        mixed = _merge(o_a, o_c, o_s, o_w, o_m, hb, w_branch[l].astype(BF16))
        x1, x1b = _out_proj_ln(mixed, w_out[l].astype(BF16), x.reshape(n, D_MODEL),
                               ln1_g[l].reshape(1, D_MODEL), ln1_b[l].reshape(1, D_MODEL))
        y = _peer(x1, x1b, peer_wq[l], peer_keys[l], peer_u[l], peer_v[l])
        x = _residual_ln(x1, y, ln2_g[l].reshape(1, D_MODEL), ln2_b[l].reshape(1, D_MODEL)).reshape(b, t, D_MODEL)
    return x
```

```python
import functools
import math

import numpy as np
import jax
import jax.numpy as jnp
from jax import lax
from jax.experimental import pallas as pl
from jax.experimental.pallas import tpu as pltpu

F32 = jnp.float32
BF16 = jnp.bfloat16

D_MODEL = 2048
A_HEADS, A_DQK, A_DV = 8, 64, 128
B_HEADS, B_GROUPS, B_HPG, B_DH = 16, 4, 4, 64
CMP_LEN, CMP_STRIDE, CMP_HIDDEN = 32, 16, 256
SEL_BLOCK, SEL_TOPN, WINDOW = 64, 16, 512
M_HEADS, M_DH = 4, 256
REL_BUCKETS, REL_MAX_DIST = 32, 1024
PEER_HEADS, PEER_NKEYS, PEER_DKEY, PEER_TOPK = 8, 128, 256, 16
MIX_W = 1024
LN_EPS = 1e-5
FORCE = 1e9
DEPTH = 1
ALPHA = (2 * DEPTH) ** 0.25
LAMBDA_INIT = 0.8 - 0.6 * math.exp(-0.3 * 0)
LOG2E = math.log2(math.e)

NEG = -1e30
ATT_TILE = 256
LANES = 128
ONES_ROWS = 16
SEL_NQ = 1
VMEM_LIMIT = 56 * 1024 * 1024


def _cparams(sem):
    return pltpu.CompilerParams(dimension_semantics=sem, vmem_limit_bytes=VMEM_LIMIT)


def _dot_nt(a, b):
    return lax.dot_general(a, b, (((1,), (1,)), ((), ())), preferred_element_type=F32)


def _mm_body(x_ref, w_ref, o_ref):
    o_ref[...] = jnp.dot(x_ref[...], w_ref[...], preferred_element_type=F32).astype(o_ref.dtype)


def _mm_nt_body(x_ref, w_ref, o_ref):
    o_ref[...] = _dot_nt(x_ref[...], w_ref[...]).astype(o_ref.dtype)


def _matmul(x, w, out_dtype, tm, tn, name, w_transposed=False):
    m, k = x.shape
    n = w.shape[0] if w_transposed else w.shape[1]
    w_spec = pl.BlockSpec((tn, k), lambda i, j: (j, 0)) if w_transposed else pl.BlockSpec((k, tn), lambda i, j: (0, j))
    return pl.pallas_call(
        _mm_nt_body if w_transposed else _mm_body,
        grid=(m // tm, n // tn),
        in_specs=[pl.BlockSpec((tm, k), lambda i, j: (i, 0)), w_spec],
        out_specs=pl.BlockSpec((tm, tn), lambda i, j: (i, j)),
        out_shape=jax.ShapeDtypeStruct((m, n), out_dtype),
        compiler_params=_cparams(("parallel", "arbitrary")),
        name=name,
    )(x, w)


W_A_COLS = 6656
W_B_COLS = 6656


def _w_prep_body(w_ref, wa_ref, wb_ref):
    def put(dst_ref, d0, s0, n, c=1.0):
        val = w_ref[s0:s0 + n, :]
        dst_ref[d0:d0 + n, :] = (val * c if c != 1.0 else val).astype(dst_ref.dtype)

    put(wa_ref, 0, 5680, 1024, M_DH ** -0.5 * LOG2E)
    put(wa_ref, 1024, 0, 1024, A_DQK ** -0.5 * LOG2E)
    put(wa_ref, 2048, 1024, 2048)
    put(wa_ref, 4096, 3072, 1024, B_DH ** -0.5 * LOG2E)
    for br in range(3):
        for g in range(B_GROUPS):
            src = 4096 + br * 2 * B_GROUPS * B_DH + g * B_DH
            dst = 5120 + (br * B_GROUPS + g) * 2 * B_DH
            put(wa_ref, dst, src, B_DH)
            put(wa_ref, dst + B_DH, src + B_GROUPS * B_DH, B_DH)
    put(wb_ref, 0, 6704, 6144)
    put(wb_ref, 6144, 5632, 48)
    wb_ref[6144 + 48:, :] = jnp.zeros((W_B_COLS - 6144 - 48, wb_ref.shape[1]), wb_ref.dtype)


def _w_prep(w_in_t):
    c, k = w_in_t.shape
    tk = 256
    return pl.pallas_call(
        _w_prep_body,
        grid=(k // tk,),
        in_specs=[pl.BlockSpec((c, tk), lambda i: (0, i))],
        out_specs=[pl.BlockSpec((W_A_COLS, tk), lambda i: (0, i)), pl.BlockSpec((W_B_COLS, tk), lambda i: (0, i))],
        out_shape=[jax.ShapeDtypeStruct((W_A_COLS, k), BF16), jax.ShapeDtypeStruct((W_B_COLS, k), BF16)],
        compiler_params=_cparams(("parallel",)),
        name="w_in_regroup",
    )(w_in_t)


def _rel_bucket(dist):
    n = jnp.maximum(dist, 0)
    max_exact = REL_BUCKETS // 2
    nf = jnp.maximum(n, 1).astype(jnp.float32)
    large = max_exact + (jnp.log(nf / max_exact) / math.log(REL_MAX_DIST / max_exact)
                         * (REL_BUCKETS - max_exact)).astype(jnp.int32)
    large = jnp.minimum(large, REL_BUCKETS - 1)
    return jnp.where(n < max_exact, n, large)


def _bias_tiles(tab1d, t, first, n_tiles, max_dist, hpr):
    L = ATT_TILE
    m = np.arange(2 * L)
    off = np.where(m <= L, m, m - 2 * L)
    d = (first + np.arange(n_tiles))[:, None] * L + off[None, :]
    ok = (d >= 0) & (d < max_dist)
    h = tab1d.shape[0]
    rp = jnp.where(ok[:, None], tab1d.T[np.clip(d, 0, t - 1)].transpose(0, 2, 1), NEG)
    return pl.pallas_call(
        functools.partial(_toeplitz_body, hpr=hpr),
        grid=(n_tiles,),
        in_specs=[pl.BlockSpec((None, h, 1, 2 * L), lambda c: (c, 0, 0, 0))],
        out_specs=pl.BlockSpec((None, h // hpr, L, hpr * L), lambda c: (c, 0, 0, 0)),
        out_shape=jax.ShapeDtypeStruct((n_tiles, h // hpr, L, hpr * L), F32),
        compiler_params=_cparams(("parallel",)),
        name="bias_tiles",
    )(rp.reshape(n_tiles, h, 1, 2 * L))


def _toeplitz_body(rp_ref, o_ref, *, hpr):
    L = o_ref.shape[1]
    for hd in range(rp_ref.shape[0]):
        rows = jnp.broadcast_to(rp_ref[hd], (L, 2 * L))
        tile = pltpu.roll(rows, 0, 1, stride=1, stride_axis=0)[:, :L]
        o_ref[hd // hpr, :, (hd % hpr) * L:(hd % hpr + 1) * L] = tile


def _flash_loop(lo, hi, scores, values, m_ref, acc_ref):
    def ahead(kt):
        s = scores(kt)
        return s, jnp.max(s, axis=0, keepdims=True)

    def body(kt, carry):
        s, mx = carry
        nxt = ahead(jnp.minimum(kt + 1, hi - 1))
        m_prev = m_ref[...]
        m_new = jnp.maximum(m_prev, mx)
        alpha = jnp.exp2(m_prev - m_new)
        p = jnp.exp2(s - m_new[0:1])
        acc_ref[...] = alpha[0:1] * acc_ref[...] + jnp.dot(values(kt), p.astype(BF16), preferred_element_type=F32)
        m_ref[...] = m_new
        return nxt

    lax.fori_loop(lo, hi, body, ahead(lo))


def _diff_body(q_ref, k_ref, vt_ref, bias_ref, lam_ref, g_ref, o_ref, m_sc, acc_sc, *, n_bt):
    L = ATT_TILE
    tq = 2 * L
    qi = pl.program_id(2)
    m_sc[...] = jnp.full(m_sc.shape, NEG, F32)
    acc_sc[...] = jnp.zeros(acc_sc.shape, F32)
    qt = q_ref[...].astype(F32).T.astype(BF16)
    zero = jnp.zeros((A_DQK, tq), BF16)
    q_cat = jnp.concatenate([jnp.concatenate([qt[:A_DQK], zero], axis=0),
                             jnp.concatenate([zero, qt[A_DQK:]], axis=0)], axis=1)

    def scores(kt):
        off = pl.multiple_of(kt * L, L)
        k = k_ref[pl.ds(off, L), :]
        d0 = 2 * qi - kt
        bias = jnp.concatenate([bias_ref[jnp.minimum(d0 + 1, n_bt - 1)],
                                bias_ref[jnp.minimum(d0 + 2, n_bt - 1)]], axis=1)
        s = jnp.dot(k, q_cat, preferred_element_type=F32)
        return jnp.concatenate([s[:, :tq] + bias, s[:, tq:] + bias], axis=1)

    _flash_loop(0, 2 * qi + 2, scores, lambda kt: vt_ref[kt], m_sc, acc_sc)

    lp = lam_ref[...]
    lam = (jnp.exp(jnp.sum(lp[0:1] * lp[1:2], axis=1, keepdims=True))
           - jnp.exp(jnp.sum(lp[2:3] * lp[3:4], axis=1, keepdims=True)) + LAMBDA_INIT)
    acc = acc_sc[...]
    o0 = acc[:A_DV, :tq] / jnp.maximum(acc[A_DV:A_DV + 1, :tq], 1e-30)
    o1 = acc[:A_DV, tq:] / jnp.maximum(acc[A_DV:A_DV + 1, tq:], 1e-30)
    o = o0 - lam * o1
    g = jnp.concatenate([g_ref[...]] * (tq // LANES), axis=1)
    o = o * lax.rsqrt(jnp.mean(o * o, axis=0, keepdims=True) + LN_EPS) * g
    o_ref[...] = (o * (1.0 - LAMBDA_INIT)).T.astype(o_ref.dtype)


def _diff_attention(ha, v_t, bias, lam_params, subln, b, t):
    L = ATT_TILE
    tq = 2 * L
    n_bt = bias.shape[0]
    return pl.pallas_call(
        functools.partial(_diff_body, n_bt=n_bt),
        grid=(b, A_HEADS, t // tq),
        in_specs=[
            pl.BlockSpec((None, tq, 128), lambda bi, h, qi: (bi, qi, 8 + h)),
            pl.BlockSpec((None, t, 128), lambda bi, h, qi: (bi, 0, 16 + h)),
            pl.BlockSpec((None, None, t // L, A_DV + ONES_ROWS, L), lambda bi, h, qi: (bi, h, 0, 0, 0)),
            pl.BlockSpec((n_bt, None, L, L), lambda bi, h, qi: (0, h, 0, 0)),
            pl.BlockSpec((4, A_DQK), lambda bi, h, qi: (0, 0)),
            pl.BlockSpec((A_DV, LANES), lambda bi, h, qi: (0, 0)),
        ],
        out_specs=pl.BlockSpec((None, tq, 128), lambda bi, h, qi: (bi, qi, h)),
        out_shape=jax.ShapeDtypeStruct((b, t, MIX_W), BF16),
        scratch_shapes=[pltpu.VMEM((8, 2 * tq), F32), pltpu.VMEM((A_DV + ONES_ROWS, 2 * tq), F32)],
        compiler_params=_cparams(("parallel", "parallel", "arbitrary")),
        name="diff_attention",
    )(ha, ha, v_t, bias, lam_params, subln)


def _compress_body(kv_ref, w1_ref, w2_ref, pe_ref, o_ref):
    nr = o_ref.shape[0]
    first = jnp.zeros((nr, 2 * CMP_HIDDEN), F32)
    second = jnp.zeros((nr, 2 * CMP_HIDDEN), F32)
    for l in range(CMP_STRIDE):
        rows = kv_ref[pl.ds(l, nr, stride=CMP_STRIDE), :].astype(BF16)
        first = first + jnp.dot(rows, w1_ref[l], preferred_element_type=F32)
        second = second + jnp.dot(rows, w1_ref[CMP_STRIDE + l], preferred_element_type=F32)
    pw = jnp.zeros((8, 2 * CMP_HIDDEN), F32)
    for l in range(CMP_LEN):
        pw = pw + jnp.dot(pe_ref[l], w1_ref[l], preferred_element_type=F32)
    second = jnp.concatenate([second[1:], second[:1]], axis=0)
    hdn = jax.nn.gelu(first + second + pw[0:1])
    o_ref[...] = jnp.dot(hdn.astype(BF16), w2_ref[...], preferred_element_type=F32)


def _compress(kv, w1_bd, w2_bd, pe_bd):
    b, t, _ = kv.shape
    nr = t // CMP_STRIDE
    return pl.pallas_call(
        _compress_body,
        grid=(b, B_GROUPS),
        in_specs=[
            pl.BlockSpec((None, t, LANES), lambda bi, gi: (bi, 0, gi)),
            pl.BlockSpec(w1_bd.shape, lambda bi, gi: (0, 0, 0)),
            pl.BlockSpec(w2_bd.shape, lambda bi, gi: (0, 0)),
            pl.BlockSpec(pe_bd.shape, lambda bi, gi: (0, 0, 0)),
        ],
        out_specs=pl.BlockSpec((None, None, nr, 2 * B_DH), lambda bi, gi: (bi, gi, 0, 0)),
        out_shape=jax.ShapeDtypeStruct((b, B_GROUPS, nr, 2 * B_DH), F32),
        compiler_params=_cparams(("parallel", "parallel")),
        name="nsa_compress",
    )(kv, w1_bd, w2_bd, pe_bd)


def _topk_mask_axis0(v, k):
    r, n = v.shape
    iota = lax.broadcasted_iota(jnp.int32, (r, n), 0).astype(F32)

    def body(_, c):
        v, sel = c
        mx = jnp.max(v, axis=0, keepdims=True)
        idx = jnp.min(jnp.where(v == mx, iota, float(r)), axis=0, keepdims=True)
        hit = iota == idx
        return jnp.where(hit, -jnp.inf, v), jnp.where(hit, 1.0, sel)

    _, sel = lax.fori_loop(0, k, body, (v, jnp.zeros((r, n), F32)))
    return sel


def _cmp_body(q_ref, kvc_ref, ov_ref, gl_ref, o_ref, mask_ref, *, n_c, n_top):
    L = ATT_TILE
    qi = pl.program_id(2)
    ncp = kvc_ref.shape[0]
    n_sel = mask_ref.shape[0]
    qt = q_ref[...].astype(F32).T.astype(BF16)
    qh = jnp.concatenate([qt[h * B_DH:(h + 1) * B_DH] for h in range(B_HPG)], axis=1)
    kvc = kvc_ref[...]
    s = jnp.dot(kvc[:, :B_DH].astype(BF16), qh, preferred_element_type=F32)
    tcol = qi * L + lax.broadcasted_iota(jnp.int32, (1, L), 1)
    crow = lax.broadcasted_iota(jnp.int32, (ncp, 1), 0)
    seen = jnp.where(crow < n_c, crow * CMP_STRIDE + (CMP_LEN - 1), jnp.int32(2 ** 30)) <= tcol
    valid = jnp.concatenate([seen] * B_HPG, axis=1)
    s = jnp.where(valid, s, NEG)
    mx = jnp.max(s, axis=0, keepdims=True)
    e = jnp.where(valid, jnp.exp2(s - mx), 0.0)
    p = e / jnp.maximum(jnp.sum(e, axis=0, keepdims=True), 1e-30)
    o = jnp.dot(kvc.T[B_DH:].astype(BF16), p.astype(BF16), preferred_element_type=F32)
    gate = jax.nn.sigmoid(gl_ref[...])
    o = jnp.concatenate([o[:, h * L:(h + 1) * L] * gate[h:h + 1] for h in range(B_HPG)], axis=0)
    o_ref[...] = o.T

    psum = p[:, :L] + p[:, L:2 * L] + p[:, 2 * L:3 * L] + p[:, 3 * L:]
    ov_t = ov_ref[...]
    imp = jnp.zeros((ov_t.shape[0], L), F32)
    rem = psum
    for _ in range(3):
        part = rem.astype(BF16)
        imp = imp + jnp.dot(ov_t, part, preferred_element_type=F32)
        rem = rem - part.astype(F32)
    imp = imp[:n_sel]
    blk = lax.broadcasted_iota(jnp.int32, (n_sel, 1), 0)
    cur = jnp.right_shift(tcol, int(math.log2(SEL_BLOCK)))
    imp = jnp.where(blk * SEL_BLOCK > tcol, -FORCE, imp)
    imp = jnp.where(blk == 0, FORCE, imp)
    imp = jnp.where(blk == cur, FORCE, imp)
    imp = jnp.where(blk == cur - 1, FORCE, imp)
    mask_ref[...] = jnp.where(_topk_mask_axis0(imp, n_top) > 0.0, 0.0, NEG).astype(mask_ref.dtype)


def _nsa_compressed(ha, kvc, overlap_t, glog_t, b, t, n_c, n_top):
    L = ATT_TILE
    ncp = kvc.shape[2]
    n_sel = t // SEL_BLOCK
    return pl.pallas_call(
        functools.partial(_cmp_body, n_c=n_c, n_top=n_top),
        grid=(b, B_GROUPS, t // L),
        in_specs=[
            pl.BlockSpec((None, L, 256), lambda bi, g, qi: (bi, qi, 16 + g)),
            pl.BlockSpec((None, None, ncp, 2 * B_DH), lambda bi, g, qi: (bi, g, 0, 0)),
            pl.BlockSpec(overlap_t.shape, lambda bi, g, qi: (0, 0)),
            pl.BlockSpec((None, None, B_HPG, L), lambda bi, g, qi: (bi, g, 0, qi)),
        ],
        out_specs=[
            pl.BlockSpec((None, L, 256), lambda bi, g, qi: (bi, qi, g)),
            pl.BlockSpec((None, None, n_sel, L), lambda bi, g, qi: (bi, g, 0, qi)),
        ],
        out_shape=[jax.ShapeDtypeStruct((b, t, MIX_W), F32),
                   jax.ShapeDtypeStruct((b, B_GROUPS, n_sel, t), BF16)],
        compiler_params=_cparams(("parallel", "parallel", "arbitrary")),
        name="nsa_compressed_select",
    )(ha, kvc, overlap_t, glog_t)


def _nsa_body(*refs, mode, n_bt):
    if mode == "sel":
        q_ref, ka_ref, vt_ref, bias_ref, gl_ref, mask_ref, o_ref, m_sc, acc_sc = refs
    else:
        q_ref, ka_ref, vt_ref, bias_ref, gl_ref, o_ref, m_sc, acc_sc = refs
    L = ATT_TILE
    nq = q_ref.shape[0] // L
    qi = pl.program_id(2)
    m_sc[...] = jnp.full(m_sc.shape, NEG, F32)
    acc_sc[...] = jnp.zeros(acc_sc.shape, F32)
    qt = q_ref[...].astype(F32).T.astype(BF16)
    if mode == "sel":
        n_sel = mask_ref.shape[0]
        qm = mask_ref[...]
        if n_sel < B_DH:
            qm = jnp.concatenate([qm, jnp.zeros((B_DH - n_sel, nq * L), BF16)], axis=0)
    else:
        qm = jnp.zeros((B_DH, nq * L), BF16)
    q_aug_t = jnp.concatenate(
        [jnp.concatenate([qt[h * B_DH:(h + 1) * B_DH, s * L:(s + 1) * L], qm[:, s * L:(s + 1) * L]], axis=0)
         for s in range(nq) for h in range(B_HPG)], axis=1)

    def scores(kt):
        off = pl.multiple_of(kt * L, L)
        idx = nq * qi - kt + (nq - 1)
        bias = [bias_ref[jnp.minimum(idx + s, n_bt - 1)] for s in range(nq)]
        bias = bias[0] if nq == 1 else jnp.concatenate(bias, axis=1)
        return jnp.dot(ka_ref[pl.ds(off, L), :], q_aug_t, preferred_element_type=F32) + bias

    lo = 0 if mode == "sel" else jnp.maximum(nq * qi - (n_bt - nq), 0)
    _flash_loop(lo, nq * qi + nq, scores, lambda kt: vt_ref[kt], m_sc, acc_sc)
    acc = acc_sc[...]
    o = acc[:B_DH] / jnp.maximum(acc[B_DH:B_DH + 1], 1e-30)
    gate = jax.nn.sigmoid(gl_ref[...])
    heads = []
    for h in range(B_HPG):
        cols = [o[:, (s * B_HPG + h) * L:(s * B_HPG + h + 1) * L] for s in range(nq)]
        heads.append((cols[0] if nq == 1 else jnp.concatenate(cols, axis=1)) * gate[h:h + 1])
    o_ref[...] = jnp.concatenate(heads, axis=0).T


def _nsa_branch(ha, k_aug, v_t, bias, glog_t, mask_t, b, t, mode, nq):
    L = ATT_TILE
    tq = nq * L
    n_bt = bias.shape[0]
    in_specs = [
        pl.BlockSpec((None, tq, 256), lambda bi, g, qi: (bi, qi, 16 + g)),
        pl.BlockSpec((None, t, LANES), lambda bi, g, qi: (bi, 0, g)),
        pl.BlockSpec((None, None, t // L, B_DH + ONES_ROWS, L), lambda bi, g, qi: (bi, g, 0, 0, 0)),
        pl.BlockSpec((n_bt, None, L, B_HPG * L), lambda bi, g, qi: (0, g, 0, 0)),
        pl.BlockSpec((None, None, B_HPG, tq), lambda bi, g, qi: (bi, g, 0, qi)),
    ]
    args = [ha, k_aug, v_t, bias, glog_t]
    if mode == "sel":
        n_sel = mask_t.shape[2]
        in_specs.append(pl.BlockSpec((None, None, n_sel, tq), lambda bi, g, qi: (bi, g, 0, qi)))
        args.append(mask_t)
    return pl.pallas_call(
        functools.partial(_nsa_body, mode=mode, n_bt=n_bt),
        grid=(b, B_GROUPS, t // tq),
        in_specs=in_specs,
        out_specs=pl.BlockSpec((None, tq, 256), lambda bi, g, qi: (bi, qi, g)),
        out_shape=jax.ShapeDtypeStruct((b, t, MIX_W), F32),
        scratch_shapes=[pltpu.VMEM((8, B_HPG * tq), F32), pltpu.VMEM((B_DH + ONES_ROWS, B_HPG * tq), F32)],
        compiler_params=_cparams(("parallel", "parallel", "arbitrary")),
        name="nsa_" + mode,
    )(*args)


def _mem_body(q_ref, k_ref, v_ref, o_ref):
    q = q_ref[...]
    outs = []
    for h in range(M_HEADS):
        sl = slice(h * M_DH, (h + 1) * M_DH)
        s = _dot_nt(q[:, sl], k_ref[:, sl])
        e = jnp.exp2(s - jnp.max(s, axis=1, keepdims=True))
        p = e / jnp.sum(e, axis=1, keepdims=True)
        outs.append(jnp.dot(p.astype(BF16), v_ref[:, sl], preferred_element_type=F32))
    o_ref[...] = jnp.concatenate(outs, axis=1).astype(o_ref.dtype)


def _memory_attention(ha, memkv, b, t, n_mem):
    tq = 512
    w = M_HEADS * M_DH
    return pl.pallas_call(
        _mem_body,
        grid=(b, t // tq),
        in_specs=[
            pl.BlockSpec((None, tq, w), lambda bi, qi: (bi, qi, 0)),
            pl.BlockSpec((n_mem, w), lambda bi, qi: (bi, 0)),
            pl.BlockSpec((n_mem, w), lambda bi, qi: (bi, 1)),
        ],
        out_specs=pl.BlockSpec((None, tq, w), lambda bi, qi: (bi, qi, 0)),
        out_shape=jax.ShapeDtypeStruct((b, t, w), BF16),
        compiler_params=_cparams(("parallel", "arbitrary")),
        name="memory_attention",
    )(ha, memkv, memkv)


def _merge_body(oa_ref, oc_ref, os_ref, ow_ref, om_ref, gl_ref, w_ref, o_ref):
    branches = (oa_ref[...], oc_ref[...] + os_ref[...] + ow_ref[...], om_ref[...])
    acc = jnp.zeros(o_ref.shape, F32)
    for n in range(3):
        up = jnp.dot(branches[n].astype(BF16), w_ref[n], preferred_element_type=F32)
        acc = acc + jax.nn.sigmoid(gl_ref[:, n * D_MODEL:(n + 1) * D_MODEL]) * up
    o_ref[...] = acc.astype(o_ref.dtype)


def _merge(o_a, o_c, o_s, o_w, o_m, hb, w_branch):
    n = o_a.shape[0]
    tm = 256
    row = lambda i: (i, 0)
    return pl.pallas_call(
        _merge_body,
        grid=(n // tm,),
        in_specs=[pl.BlockSpec((tm, MIX_W), row)] * 5 + [
            pl.BlockSpec((tm, 3 * D_MODEL), row),
            pl.BlockSpec((3, MIX_W, D_MODEL), lambda i: (0, 0, 0)),
        ],
        out_specs=pl.BlockSpec((tm, D_MODEL), row),
        out_shape=jax.ShapeDtypeStruct((n, D_MODEL), BF16),
        compiler_params=_cparams(("parallel",)),
        name="branch_merge",
    )(o_a, o_c, o_s, o_w, o_m, hb, w_branch)


def _layer_norm(z, g, b):
    mu = jnp.mean(z, axis=1, keepdims=True)
    zc = z - mu
    var = jnp.mean(zc * zc, axis=1, keepdims=True)
    return zc * lax.rsqrt(var + LN_EPS) * g + b


def _out_ln_body(y_ref, w_ref, x_ref, g_ref, b_ref, o_ref, ob_ref):
    y = jnp.dot(y_ref[...], w_ref[...], preferred_element_type=F32)
    o = _layer_norm(ALPHA * x_ref[...] + y, g_ref[...], b_ref[...])
    o_ref[...] = o
    ob_ref[...] = o.astype(BF16)


def _out_proj_ln(mixed, w_out, x, g, b):
    n = x.shape[0]
    tm = 512
    row = lambda i: (i, 0)
    const = lambda i: (0, 0)
    return pl.pallas_call(
        _out_ln_body,
        grid=(n // tm,),
        in_specs=[pl.BlockSpec((tm, D_MODEL), row), pl.BlockSpec((D_MODEL, D_MODEL), const),
                  pl.BlockSpec((tm, D_MODEL), row), pl.BlockSpec((1, D_MODEL), const),
                  pl.BlockSpec((1, D_MODEL), const)],
        out_specs=[pl.BlockSpec((tm, D_MODEL), row), pl.BlockSpec((tm, D_MODEL), row)],
        out_shape=[jax.ShapeDtypeStruct((n, D_MODEL), F32), jax.ShapeDtypeStruct((n, D_MODEL), BF16)],
        compiler_params=_cparams(("parallel",)),
        name="out_proj_ln1",
    )(mixed, w_out, x, g, b)


def _res_ln_body(x_ref, y_ref, g_ref, b_ref, o_ref):
    o_ref[...] = _layer_norm(ALPHA * x_ref[...] + y_ref[...], g_ref[...], b_ref[...])


def _residual_ln(x, y, g, b):
    n = x.shape[0]
    tm = 512
    row = lambda i: (i, 0)
    const = lambda i: (0, 0)
    return pl.pallas_call(
        _res_ln_body,
        grid=(n // tm,),
        in_specs=[pl.BlockSpec((tm, D_MODEL), row), pl.BlockSpec((tm, D_MODEL), row),
                  pl.BlockSpec((1, D_MODEL), const), pl.BlockSpec((1, D_MODEL), const)],
        out_specs=pl.BlockSpec((tm, D_MODEL), row),
        out_shape=jax.ShapeDtypeStruct((n, D_MODEL), F32),
        compiler_params=_cparams(("parallel",)),
        name="residual_ln2",
    )(x, y, g, b)


def _topk_axis0(v, k):
    r, n = v.shape
    iota = lax.broadcasted_iota(jnp.int32, (r, n), 0).astype(F32)
    slot = lax.broadcasted_iota(jnp.int32, (k, n), 0)

    def body(it, c):
        v, vals, idxs = c
        mx = jnp.max(v, axis=0, keepdims=True)
        idx = jnp.min(jnp.where(v == mx, iota, float(r)), axis=0, keepdims=True)
        v = jnp.where(iota == idx, -jnp.inf, v)
        return v, jnp.where(slot == it, mx, vals), jnp.where(slot == it, idx, idxs)

    _, vals, idxs = lax.fori_loop(0, k, body, (v, jnp.zeros((k, n), F32), jnp.zeros((k, n), F32)))
    return vals, idxs


def _pick_rows(table, pos, k):
    out = jnp.zeros(pos.shape, F32)
    for a in range(k):
        out = out + jnp.where(pos == float(a), table[a:a + 1], 0.0)
    return out


def _route_body(q_ref, keys_ref, ei_ref, ej_ref, g_ref):
    k = PEER_TOPK
    scores = _dot_nt(keys_ref[...], q_ref[...])
    v0, i0 = _topk_axis0(scores[:PEER_NKEYS], k)
    v1, i1 = _topk_axis0(scores[PEER_NKEYS:], k)
    counts = [k // (a + 1) for a in range(k)]
    starts = np.cumsum([0] + counts[:-1])
    pad = (-sum(counts)) % 8
    comb = jnp.concatenate([v0[a:a + 1] + v1[:counts[a]] for a in range(k)]
                           + [jnp.full((pad, v0.shape[1]), -jnp.inf, F32)], axis=0)
    sf, pos = _topk_axis0(comb, k)
    pa = jnp.zeros(pos.shape, F32)
    pb = pos
    for a in range(1, k):
        later = pos >= float(starts[a])
        pa = pa + jnp.where(later, 1.0, 0.0)
        pb = pb - jnp.where(later, float(counts[a - 1]), 0.0)
    ei_ref[...] = _pick_rows(i0, pa, k)
    ej_ref[...] = _pick_rows(i1, pb, k)
    e = jnp.exp(sf - jnp.max(sf, axis=0, keepdims=True))
    g_ref[...] = e / jnp.sum(e, axis=0, keepdims=True)


def _peer_route(q, keys):
    n = q.shape[0]
    tn = 256
    out = jax.ShapeDtypeStruct((PEER_HEADS, PEER_TOPK, n), F32)
    ospec = pl.BlockSpec((None, PEER_TOPK, tn), lambda i, h: (h, 0, i))
    return pl.pallas_call(
        _route_body,
        grid=(n // tn, PEER_HEADS),
        in_specs=[pl.BlockSpec((tn, PEER_DKEY), lambda i, h: (i, h)),
                  pl.BlockSpec((None, 2 * PEER_NKEYS, PEER_DKEY), lambda i, h: (h, 0, 0))],
        out_specs=[ospec, ospec, ospec],
        out_shape=[out, out, out],
        compiler_params=_cparams(("parallel", "arbitrary")),
        name="peer_route",
    )(q, keys)


def _gate_body(ei_ref, ej_ref, g_ref, o_ref):
    tb = ei_ref.shape[0]
    nk = PEER_NKEYS
    iota = lax.broadcasted_iota(jnp.int32, (tb, nk, ei_ref.shape[2]), 1).astype(F32)
    rows = jnp.where(iota == ei_ref[...], 1.0, 0.0).astype(BF16)
    cols = jnp.where(iota == ej_ref[...], g_ref[...], 0.0).astype(BF16)
    gm = lax.dot_general(rows, cols, (((2,), (2,)), ((0,), (0,))), preferred_element_type=F32)
    o_ref[...] = jnp.swapaxes(gm, 0, 1).astype(o_ref.dtype)


def _gate_matrix(ei, ej, g):
    n, _, slots = ei.shape
    tb = 64
    spec = pl.BlockSpec((tb, 1, slots), lambda i: (i, 0, 0))
    return pl.pallas_call(
        _gate_body,
        grid=(n // tb,),
        in_specs=[spec, spec, spec],
        out_specs=pl.BlockSpec((PEER_NKEYS, tb, PEER_NKEYS), lambda i: (0, i, 0)),
        out_shape=jax.ShapeDtypeStruct((PEER_NKEYS, n, PEER_NKEYS), BF16),
        compiler_params=_cparams(("parallel",)),
        name="peer_gate_matrix",
    )(ei, ej, g)


def _expert_body(x_ref, u_ref, g_ref, v_ref, o_ref):
    @pl.when(pl.program_id(1) == 0)
    def _():
        o_ref[...] = jnp.zeros(o_ref.shape, F32)

    hid = _dot_nt(x_ref[...], u_ref[...].astype(BF16))
    gate = jnp.concatenate([g_ref[i] for i in range(g_ref.shape[0])], axis=1)
    act = (jax.nn.gelu(hid) * gate.astype(F32)).astype(BF16)
    o_ref[...] += jnp.dot(act, v_ref[...].astype(BF16), preferred_element_type=F32)


def _peer_experts(xb, u, gm, v):
    n = xb.shape[0]
    ne = u.shape[0]
    tn, te = 1024, 512
    return pl.pallas_call(
        _expert_body,
        grid=(n // tn, ne // te),
        in_specs=[pl.BlockSpec((tn, D_MODEL), lambda i, j: (i, 0)),
                  pl.BlockSpec((te, D_MODEL), lambda i, j: (j, 0)),
                  pl.BlockSpec((te // PEER_NKEYS, tn, PEER_NKEYS), lambda i, j: (j, i, 0)),
                  pl.BlockSpec((te, D_MODEL), lambda i, j: (j, 0))],
        out_specs=pl.BlockSpec((tn, D_MODEL), lambda i, j: (i, 0)),
        out_shape=jax.ShapeDtypeStruct((n, D_MODEL), F32),
        compiler_params=_cparams(("parallel", "arbitrary")),
        name="peer_experts",
    )(xb, u, gm, v)


def _token_mixer(x, mem, w_in, diff_lambda, diff_subln, cmp_pe, cmp_w1, cmp_w2, w_mem_kv, w_branch, rel_bias):
    b, t, _ = x.shape
    n = b * t
    L = ATT_TILE
    xb = x.reshape(n, D_MODEL).astype(BF16)

    w_a, w_b = _w_prep(w_in.T)
    ha = _matmul(xb, w_a, BF16, 1024, 1664, "in_proj_a", w_transposed=True)
    hb = _matmul(xb, w_b, F32, 1024, 1664, "in_proj_b", w_transposed=True)
    ha3 = ha.reshape(b, t, ha.shape[1])

    tab1d = rel_bias[_rel_bucket(jnp.arange(t))].T * LOG2E
    n_far = min(t // L, REL_MAX_DIST // L + 2)
    bias_a = _bias_tiles(tab1d[:A_HEADS], t, -1, n_far + 1, t, 1)
    bias_b = tab1d[A_HEADS:]
    bias_sel = _bias_tiles(bias_b, t, 1 - SEL_NQ, n_far + SEL_NQ - 1, t, B_HPG)
    n_win = min(t // L, WINDOW // L + 1)
    bias_win = _bias_tiles(bias_b, t, 0, n_win, WINDOW, B_HPG)

    def values_t(v):
        heads, dv = v.shape[2:]
        vt = v.reshape(b, t // L, L, heads, dv).transpose(0, 3, 1, 4, 2)
        return jnp.concatenate([vt, jnp.ones((b, heads, t // L, ONES_ROWS, L), BF16)], axis=3)

    o_a = _diff_attention(ha3, values_t(ha3[:, :, 3072:4096].reshape(b, t, A_HEADS, A_DV)), bias_a, diff_lambda,
                          jnp.broadcast_to(diff_subln[:, None], (A_DV, LANES)), b, t)

    glog_t = hb[:, 6144:6192].reshape(b, t, 3, B_GROUPS, B_HPG).transpose(2, 0, 3, 4, 1)
    nr = t // CMP_STRIDE
    w1 = cmp_w1.reshape(2, CMP_LEN, B_DH, CMP_HIDDEN)
    z1 = jnp.zeros_like(w1[0])
    w1_bd = jnp.concatenate([jnp.concatenate([w1[0], z1], axis=2), jnp.concatenate([z1, w1[1]], axis=2)],
                            axis=1).astype(BF16)
    z2 = jnp.zeros_like(cmp_w2[0])
    w2_bd = jnp.concatenate([jnp.concatenate([cmp_w2[0], z2], axis=1), jnp.concatenate([z2, cmp_w2[1]], axis=1)],
                            axis=0).astype(BF16)
    pe_bd = jnp.broadcast_to(jnp.concatenate([cmp_pe[0], cmp_pe[1]], axis=1)[:, None, :],
                             (CMP_LEN, 8, 2 * B_DH)).astype(BF16)
    kvc = _compress(ha3[:, :, 5120:5632].astype(F32), w1_bd, w2_bd, pe_bd)
    n_c = (t - CMP_LEN) // CMP_STRIDE + 1
    n_sel = t // SEL_BLOCK
    cidx = np.arange(nr)[:, None] * CMP_STRIDE + np.arange(CMP_LEN)[None, :]
    overlap = (cidx[:, :, None] // SEL_BLOCK == np.arange(n_sel)[None, None, :]).astype(np.float32).mean(axis=1)
    overlap[n_c:] = 0.0
    overlap_t = np.zeros((LANES, nr), np.float32)
    overlap_t[:n_sel] = overlap.T
    o_c, sel_mask = _nsa_compressed(ha3, kvc, jnp.asarray(overlap_t, BF16), glog_t[0], b, t, n_c,
                                    min(SEL_TOPN, n_sel))
    kv_sel = ha3[:, :, 5632:6144].reshape(b, t, B_GROUPS, 2, B_DH)
    kv_win = ha3[:, :, 6144:6656].reshape(b, t, B_GROUPS, 2, B_DH)
    assert n_sel <= B_DH
    hot = (np.arange(t)[:, None] // SEL_BLOCK == np.arange(B_DH)[None, :]).astype(np.float32)
    hot = jnp.broadcast_to(jnp.asarray(hot, BF16)[None, :, None, :], (b, t, B_GROUPS, B_DH))
    ka_sel = jnp.concatenate([kv_sel[:, :, :, 0], hot], axis=3).reshape(b, t, B_GROUPS * LANES)
    ka_win = jnp.concatenate([kv_win[:, :, :, 0], jnp.zeros_like(hot)], axis=3).reshape(b, t, B_GROUPS * LANES)
    o_s = _nsa_branch(ha3, ka_sel, values_t(kv_sel[:, :, :, 1]), bias_sel, glog_t[1], sel_mask, b, t, "sel", SEL_NQ)
    o_w = _nsa_branch(ha3, ka_win, values_t(kv_win[:, :, :, 1]), bias_win, glog_t[2], None, b, t, "win", 1)

    n_mem = mem.shape[1]
    memkv = _matmul(mem.reshape(b * n_mem, D_MODEL).astype(BF16), w_mem_kv.astype(BF16), BF16,
                    b * n_mem, 1024, "mem_kv_proj")
    o_m = _memory_attention(ha3, memkv, b, t, n_mem)

    flat = lambda o: o.reshape(n, MIX_W)
    return flat(o_a), flat(o_c), flat(o_s), flat(o_w), flat(o_m), hb


def _peer(x1, x1b, peer_wq, peer_keys, peer_u, peer_v):
    n = x1.shape[0]
    q = _matmul(x1b, peer_wq.astype(BF16), BF16, 1024, 1024, "peer_query")
    zk = jnp.zeros((PEER_HEADS, PEER_NKEYS, PEER_DKEY // 2), F32)
    keys_bd = jnp.concatenate([jnp.concatenate([peer_keys[:, 0], zk], axis=2),
                               jnp.concatenate([zk, peer_keys[:, 1]], axis=2)], axis=1).astype(BF16)
    ei, ej, gate = _peer_route(q, keys_bd)
    slots = lambda a: a.reshape(PEER_HEADS * PEER_TOPK, n).T.reshape(n, 1, PEER_HEADS * PEER_TOPK)
    gm = _gate_matrix(slots(ei), slots(ej), slots(gate))
    return _peer_experts(x1b, peer_u, gm, peer_v)


def kernel(x, mem, w_in, diff_lambda, diff_subln, cmp_pe, cmp_w1, cmp_w2, w_mem_kv, w_branch, w_out, ln1_g, ln1_b,
           peer_wq, peer_keys, peer_u, peer_v, ln2_g, ln2_b, rel_bias):
    b, t, _ = x.shape
    n = b * t
    for l in range(DEPTH):
        o_a, o_c, o_s, o_w, o_m, hb = _token_mixer(x, mem, w_in[l], diff_lambda[l], diff_subln[l], cmp_pe[l],
                                                   cmp_w1[l], cmp_w2[l], w_mem_kv[l], w_branch[l], rel_bias)
        mixed = _merge(o_a, o_c, o_s, o_w, o_m, hb, w_branch[l].astype(BF16))
        x1, x1b = _out_proj_ln(mixed, w_out[l].astype(BF16), x.reshape(n, D_MODEL),
                               ln1_g[l].reshape(1, D_MODEL), ln1_b[l].reshape(1, D_MODEL))
        y = _peer(x1, x1b, peer_wq[l], peer_keys[l], peer_u[l], peer_v[l])
        x = _residual_ln(x1, y, ln2_g[l].reshape(1, D_MODEL), ln2_b[l].reshape(1, D_MODEL)).reshape(b, t, D_MODEL)
    return x
```

```python
import functools
import math

import numpy as np
import jax
import jax.numpy as jnp
from jax import lax
from jax.experimental import pallas as pl
from jax.experimental.pallas import tpu as pltpu

F32 = jnp.float32
BF16 = jnp.bfloat16

D_MODEL = 2048
A_HEADS, A_DQK, A_DV = 8, 64, 128
B_HEADS, B_GROUPS, B_HPG, B_DH = 16, 4, 4, 64
CMP_LEN, CMP_STRIDE, CMP_HIDDEN = 32, 16, 256
SEL_BLOCK, SEL_TOPN, WINDOW = 64, 16, 512
M_HEADS, M_DH = 4, 256
REL_BUCKETS, REL_MAX_DIST = 32, 1024
PEER_HEADS, PEER_NKEYS, PEER_DKEY, PEER_TOPK = 8, 128, 256, 16
MIX_W = 1024
LN_EPS = 1e-5
FORCE = 1e9
DEPTH = 1
ALPHA = (2 * DEPTH) ** 0.25
LAMBDA_INIT = 0.8 - 0.6 * math.exp(-0.3 * 0)
LOG2E = math.log2(math.e)

NEG = -1e30
ATT_TILE = 256
LANES = 128
ONES_ROWS = 16
SEL_NQ = 1
VMEM_LIMIT = 56 * 1024 * 1024


def _cparams(sem):
    return pltpu.CompilerParams(dimension_semantics=sem, vmem_limit_bytes=VMEM_LIMIT)


def _dot_nt(a, b):
    return lax.dot_general(a, b, (((1,), (1,)), ((), ())), preferred_element_type=F32)


def _mm_body(x_ref, w_ref, o_ref):
    o_ref[...] = jnp.dot(x_ref[...], w_ref[...], preferred_element_type=F32).astype(o_ref.dtype)


def _mm_nt_body(x_ref, w_ref, o_ref):
    o_ref[...] = _dot_nt(x_ref[...], w_ref[...]).astype(o_ref.dtype)


def _matmul(x, w, out_dtype, tm, tn, name, w_transposed=False):
    m, k = x.shape
    n = w.shape[0] if w_transposed else w.shape[1]
    w_spec = pl.BlockSpec((tn, k), lambda i, j: (j, 0)) if w_transposed else pl.BlockSpec((k, tn), lambda i, j: (0, j))
    return pl.pallas_call(
        _mm_nt_body if w_transposed else _mm_body,
        grid=(m // tm, n // tn),
        in_specs=[pl.BlockSpec((tm, k), lambda i, j: (i, 0)), w_spec],
        out_specs=pl.BlockSpec((tm, tn), lambda i, j: (i, j)),
        out_shape=jax.ShapeDtypeStruct((m, n), out_dtype),
        compiler_params=_cparams(("parallel", "arbitrary")),
        name=name,
    )(x, w)


W_A_COLS = 6656
W_B_COLS = 6656


def _w_prep_body(w_ref, wa_ref, wb_ref):
    def put(dst_ref, d0, s0, n, c=1.0):
        val = w_ref[s0:s0 + n, :]
        dst_ref[d0:d0 + n, :] = (val * c if c != 1.0 else val).astype(dst_ref.dtype)

    put(wa_ref, 0, 5680, 1024, M_DH ** -0.5 * LOG2E)
    put(wa_ref, 1024, 0, 1024, A_DQK ** -0.5 * LOG2E)
    put(wa_ref, 2048, 1024, 2048)
    put(wa_ref, 4096, 3072, 1024, B_DH ** -0.5 * LOG2E)
    for br in range(3):
        for g in range(B_GROUPS):
            src = 4096 + br * 2 * B_GROUPS * B_DH + g * B_DH
            dst = 5120 + (br * B_GROUPS + g) * 2 * B_DH
            put(wa_ref, dst, src, B_DH)
            put(wa_ref, dst + B_DH, src + B_GROUPS * B_DH, B_DH)
    put(wb_ref, 0, 6704, 6144)
    put(wb_ref, 6144, 5632, 48)
    wb_ref[6144 + 48:, :] = jnp.zeros((W_B_COLS - 6144 - 48, wb_ref.shape[1]), wb_ref.dtype)


def _w_prep(w_in_t):
    c, k = w_in_t.shape
    tk = 256
    return pl.pallas_call(
        _w_prep_body,
        grid=(k // tk,),
        in_specs=[pl.BlockSpec((c, tk), lambda i: (0, i))],
        out_specs=[pl.BlockSpec((W_A_COLS, tk), lambda i: (0, i)), pl.BlockSpec((W_B_COLS, tk), lambda i: (0, i))],
        out_shape=[jax.ShapeDtypeStruct((W_A_COLS, k), BF16), jax.ShapeDtypeStruct((W_B_COLS, k), BF16)],
        compiler_params=_cparams(("parallel",)),
        name="w_in_regroup",
    )(w_in_t)


def _rel_bucket(dist):
    n = jnp.maximum(dist, 0)
    max_exact = REL_BUCKETS // 2
    nf = jnp.maximum(n, 1).astype(jnp.float32)
    large = max_exact + (jnp.log(nf / max_exact) / math.log(REL_MAX_DIST / max_exact)
                         * (REL_BUCKETS - max_exact)).astype(jnp.int32)
    large = jnp.minimum(large, REL_BUCKETS - 1)
    return jnp.where(n < max_exact, n, large)


def _bias_tiles(tab1d, t, first, n_tiles, max_dist, hpr):
    L = ATT_TILE
    m = np.arange(2 * L)
    off = np.where(m <= L, m, m - 2 * L)
    d = (first + np.arange(n_tiles))[:, None] * L + off[None, :]
    ok = (d >= 0) & (d < max_dist)
    h = tab1d.shape[0]
    rp = jnp.where(ok[:, None], tab1d.T[np.clip(d, 0, t - 1)].transpose(0, 2, 1), NEG)
    return pl.pallas_call(
        functools.partial(_toeplitz_body, hpr=hpr),
        grid=(n_tiles,),
        in_specs=[pl.BlockSpec((None, h, 1, 2 * L), lambda c: (c, 0, 0, 0))],
        out_specs=pl.BlockSpec((None, h // hpr, L, hpr * L), lambda c: (c, 0, 0, 0)),
        out_shape=jax.ShapeDtypeStruct((n_tiles, h // hpr, L, hpr * L), F32),
        compiler_params=_cparams(("parallel",)),
        name="bias_tiles",
    )(rp.reshape(n_tiles, h, 1, 2 * L))


def _toeplitz_body(rp_ref, o_ref, *, hpr):
    L = o_ref.shape[1]
    for hd in range(rp_ref.shape[0]):
        rows = jnp.broadcast_to(rp_ref[hd], (L, 2 * L))
        tile = pltpu.roll(rows, 0, 1, stride=1, stride_axis=0)[:, :L]
        o_ref[hd // hpr, :, (hd % hpr) * L:(hd % hpr + 1) * L] = tile


def _flash_loop(lo, hi, scores, values, m_ref, acc_ref, s_ref, mx_ref):
    def ahead(kt, slot):
        s = scores(kt)
        s_ref[slot] = s
        mx_ref[slot] = jnp.broadcast_to(jnp.max(s, axis=0, keepdims=True), mx_ref.shape[1:])

    def finish(kt, slot):
        m_prev = m_ref[...]
        m_new = jnp.maximum(m_prev, mx_ref[slot])
        alpha = jnp.exp2(m_prev - m_new)
        p = jnp.exp2(s_ref[slot] - m_new[0:1])
        acc_ref[...] = alpha[0:1] * acc_ref[...] + jnp.dot(values(kt), p.astype(BF16), preferred_element_type=F32)
        m_ref[...] = m_new

    def body(i, carry):
        kt = lo + 2 * i
        ahead(kt + 1, 1)
        finish(kt, 0)
        ahead(kt + 2, 0)
        finish(kt + 1, 1)
        return carry

    ahead(lo, 0)
    lax.fori_loop(0, (hi - lo + 1) // 2, body, 0)


def _diff_body(q_ref, k_ref, vt_ref, bias_ref, lam_ref, g_ref, o_ref, m_sc, acc_sc, s_sc, mx_sc, *, n_bt):
    L = ATT_TILE
    tq = 2 * L
    last = vt_ref.shape[0] - 1
    qi = pl.program_id(2)
    m_sc[...] = jnp.full(m_sc.shape, NEG, F32)
    acc_sc[...] = jnp.zeros(acc_sc.shape, F32)
    qt = q_ref[...].astype(F32).T.astype(BF16)
    zero = jnp.zeros((A_DQK, tq), BF16)
    q_cat = jnp.concatenate([jnp.concatenate([qt[:A_DQK], zero], axis=0),
                             jnp.concatenate([zero, qt[A_DQK:]], axis=0)], axis=1)

    def scores(kt):
        off = pl.multiple_of(jnp.minimum(kt, last) * L, L)
        k = k_ref[pl.ds(off, L), :]
        d0 = 2 * qi - kt
        bias = jnp.concatenate([bias_ref[jnp.clip(d0 + 1, 0, n_bt - 1)],
                                bias_ref[jnp.clip(d0 + 2, 0, n_bt - 1)]], axis=1)
        s = jnp.dot(k, q_cat, preferred_element_type=F32)
        return jnp.concatenate([s[:, :tq] + bias, s[:, tq:] + bias], axis=1)

    _flash_loop(0, 2 * qi + 2, scores, lambda kt: vt_ref[jnp.minimum(kt, last)], m_sc, acc_sc, s_sc, mx_sc)

    lp = lam_ref[...]
    lam = (jnp.exp(jnp.sum(lp[0:1] * lp[1:2], axis=1, keepdims=True))
           - jnp.exp(jnp.sum(lp[2:3] * lp[3:4], axis=1, keepdims=True)) + LAMBDA_INIT)
    acc = acc_sc[...]
    o0 = acc[:A_DV, :tq] / jnp.maximum(acc[A_DV:A_DV + 1, :tq], 1e-30)
    o1 = acc[:A_DV, tq:] / jnp.maximum(acc[A_DV:A_DV + 1, tq:], 1e-30)
    o = o0 - lam * o1
    g = jnp.concatenate([g_ref[...]] * (tq // LANES), axis=1)
    o = o * lax.rsqrt(jnp.mean(o * o, axis=0, keepdims=True) + LN_EPS) * g
    o_ref[...] = (o * (1.0 - LAMBDA_INIT)).T.astype(o_ref.dtype)


def _diff_attention(ha, v_t, bias, lam_params, subln, b, t):
    L = ATT_TILE
    tq = 2 * L
    n_bt = bias.shape[0]
    return pl.pallas_call(
        functools.partial(_diff_body, n_bt=n_bt),
        grid=(b, A_HEADS, t // tq),
        in_specs=[
            pl.BlockSpec((None, tq, 128), lambda bi, h, qi: (bi, qi, 8 + h)),
            pl.BlockSpec((None, t, 128), lambda bi, h, qi: (bi, 0, 16 + h)),
            pl.BlockSpec((None, None, t // L, A_DV + ONES_ROWS, L), lambda bi, h, qi: (bi, h, 0, 0, 0)),
            pl.BlockSpec((n_bt, None, L, L), lambda bi, h, qi: (0, h, 0, 0)),
            pl.BlockSpec((4, A_DQK), lambda bi, h, qi: (0, 0)),
            pl.BlockSpec((A_DV, LANES), lambda bi, h, qi: (0, 0)),
        ],
        out_specs=pl.BlockSpec((None, tq, 128), lambda bi, h, qi: (bi, qi, h)),
        out_shape=jax.ShapeDtypeStruct((b, t, MIX_W), BF16),
        scratch_shapes=[pltpu.VMEM((8, 2 * tq), F32), pltpu.VMEM((A_DV + ONES_ROWS, 2 * tq), F32),
                        pltpu.VMEM((2, L, 2 * tq), F32), pltpu.VMEM((2, 8, 2 * tq), F32)],
        compiler_params=_cparams(("parallel", "parallel", "arbitrary")),
        name="diff_attention",
    )(ha, ha, v_t, bias, lam_params, subln)


def _compress_body(kv_ref, w1_ref, w2_ref, pe_ref, o_ref):
    nr = o_ref.shape[0]
    first = jnp.zeros((nr, 2 * CMP_HIDDEN), F32)
    second = jnp.zeros((nr, 2 * CMP_HIDDEN), F32)
    for l in range(CMP_STRIDE):
        rows = kv_ref[pl.ds(l, nr, stride=CMP_STRIDE), :].astype(BF16)
        first = first + jnp.dot(rows, w1_ref[l], preferred_element_type=F32)
        second = second + jnp.dot(rows, w1_ref[CMP_STRIDE + l], preferred_element_type=F32)
    pw = jnp.zeros((8, 2 * CMP_HIDDEN), F32)
    for l in range(CMP_LEN):
        pw = pw + jnp.dot(pe_ref[l], w1_ref[l], preferred_element_type=F32)
    second = jnp.concatenate([second[1:], second[:1]], axis=0)
    hdn = jax.nn.gelu(first + second + pw[0:1])
    o_ref[...] = jnp.dot(hdn.astype(BF16), w2_ref[...], preferred_element_type=F32)


def _compress(kv, w1_bd, w2_bd, pe_bd):
    b, t, _ = kv.shape
    nr = t // CMP_STRIDE
    return pl.pallas_call(
        _compress_body,
        grid=(b, B_GROUPS),
        in_specs=[
            pl.BlockSpec((None, t, LANES), lambda bi, gi: (bi, 0, gi)),
            pl.BlockSpec(w1_bd.shape, lambda bi, gi: (0, 0, 0)),
            pl.BlockSpec(w2_bd.shape, lambda bi, gi: (0, 0)),
            pl.BlockSpec(pe_bd.shape, lambda bi, gi: (0, 0, 0)),
        ],
        out_specs=pl.BlockSpec((None, None, nr, 2 * B_DH), lambda bi, gi: (bi, gi, 0, 0)),
        out_shape=jax.ShapeDtypeStruct((b, B_GROUPS, nr, 2 * B_DH), F32),
        compiler_params=_cparams(("parallel", "parallel")),
        name="nsa_compress",
    )(kv, w1_bd, w2_bd, pe_bd)


def _topk_mask_axis0(v, k):
    r, n = v.shape
    iota = lax.broadcasted_iota(jnp.int32, (r, n), 0).astype(F32)

    def body(_, c):
        v, sel = c
        mx = jnp.max(v, axis=0, keepdims=True)
        idx = jnp.min(jnp.where(v == mx, iota, float(r)), axis=0, keepdims=True)
        hit = iota == idx
        return jnp.where(hit, -jnp.inf, v), jnp.where(hit, 1.0, sel)

    _, sel = lax.fori_loop(0, k, body, (v, jnp.zeros((r, n), F32)))
    return sel


def _cmp_body(q_ref, kvc_ref, ov_ref, gl_ref, o_ref, mask_ref, *, n_c, n_top):
    L = ATT_TILE
    qi = pl.program_id(2)
    ncp = kvc_ref.shape[0]
    n_sel = mask_ref.shape[0]
    qt = q_ref[...].astype(F32).T.astype(BF16)
    qh = jnp.concatenate([qt[h * B_DH:(h + 1) * B_DH] for h in range(B_HPG)], axis=1)
    kvc = kvc_ref[...]
    s = jnp.dot(kvc[:, :B_DH].astype(BF16), qh, preferred_element_type=F32)
    tcol = qi * L + lax.broadcasted_iota(jnp.int32, (1, L), 1)
    crow = lax.broadcasted_iota(jnp.int32, (ncp, 1), 0)
    seen = jnp.where(crow < n_c, crow * CMP_STRIDE + (CMP_LEN - 1), jnp.int32(2 ** 30)) <= tcol
    valid = jnp.concatenate([seen] * B_HPG, axis=1)
    s = jnp.where(valid, s, NEG)
    mx = jnp.max(s, axis=0, keepdims=True)
    e = jnp.where(valid, jnp.exp2(s - mx), 0.0)
    p = e / jnp.maximum(jnp.sum(e, axis=0, keepdims=True), 1e-30)
    o = jnp.dot(kvc.T[B_DH:].astype(BF16), p.astype(BF16), preferred_element_type=F32)
    gate = jax.nn.sigmoid(gl_ref[...])
    o = jnp.concatenate([o[:, h * L:(h + 1) * L] * gate[h:h + 1] for h in range(B_HPG)], axis=0)
    o_ref[...] = o.T

    psum = p[:, :L] + p[:, L:2 * L] + p[:, 2 * L:3 * L] + p[:, 3 * L:]
    ov_t = ov_ref[...]
    imp = jnp.zeros((ov_t.shape[0], L), F32)
    rem = psum
    for _ in range(3):
        part = rem.astype(BF16)
        imp = imp + jnp.dot(ov_t, part, preferred_element_type=F32)
        rem = rem - part.astype(F32)
    imp = imp[:n_sel]
    blk = lax.broadcasted_iota(jnp.int32, (n_sel, 1), 0)
    cur = jnp.right_shift(tcol, int(math.log2(SEL_BLOCK)))
    imp = jnp.where(blk * SEL_BLOCK > tcol, -FORCE, imp)
    imp = jnp.where(blk == 0, FORCE, imp)
    imp = jnp.where(blk == cur, FORCE, imp)
    imp = jnp.where(blk == cur - 1, FORCE, imp)
    mask_ref[...] = jnp.where(_topk_mask_axis0(imp, n_top) > 0.0, 0.0, NEG).astype(mask_ref.dtype)


def _nsa_compressed(ha, kvc, overlap_t, glog_t, b, t, n_c, n_top):
    L = ATT_TILE
    ncp = kvc.shape[2]
    n_sel = t // SEL_BLOCK
    return pl.pallas_call(
        functools.partial(_cmp_body, n_c=n_c, n_top=n_top),
        grid=(b, B_GROUPS, t // L),
        in_specs=[
            pl.BlockSpec((None, L, 256), lambda bi, g, qi: (bi, qi, 16 + g)),
            pl.BlockSpec((None, None, ncp, 2 * B_DH), lambda bi, g, qi: (bi, g, 0, 0)),
            pl.BlockSpec(overlap_t.shape, lambda bi, g, qi: (0, 0)),
            pl.BlockSpec((None, None, B_HPG, L), lambda bi, g, qi: (bi, g, 0, qi)),
        ],
        out_specs=[
            pl.BlockSpec((None, L, 256), lambda bi, g, qi: (bi, qi, g)),
            pl.BlockSpec((None, None, n_sel, L), lambda bi, g, qi: (bi, g, 0, qi)),
        ],
        out_shape=[jax.ShapeDtypeStruct((b, t, MIX_W), F32),
                   jax.ShapeDtypeStruct((b, B_GROUPS, n_sel, t), BF16)],
        compiler_params=_cparams(("parallel", "parallel", "arbitrary")),
        name="nsa_compressed_select",
    )(ha, kvc, overlap_t, glog_t)


def _nsa_body(*refs, mode, n_bt):
    if mode == "sel":
        q_ref, ka_ref, vt_ref, bias_ref, gl_ref, mask_ref, o_ref, m_sc, acc_sc, s_sc, mx_sc = refs
    else:
        q_ref, ka_ref, vt_ref, bias_ref, gl_ref, o_ref, m_sc, acc_sc, s_sc, mx_sc = refs
    L = ATT_TILE
    nq = q_ref.shape[0] // L
    qi = pl.program_id(2)
    m_sc[...] = jnp.full(m_sc.shape, NEG, F32)
    acc_sc[...] = jnp.zeros(acc_sc.shape, F32)
    qt = q_ref[...].astype(F32).T.astype(BF16)
    if mode == "sel":
        n_sel = mask_ref.shape[0]
        qm = mask_ref[...]
        if n_sel < B_DH:
            qm = jnp.concatenate([qm, jnp.zeros((B_DH - n_sel, nq * L), BF16)], axis=0)
    else:
        qm = jnp.zeros((B_DH, nq * L), BF16)
    q_aug_t = jnp.concatenate(
        [jnp.concatenate([qt[h * B_DH:(h + 1) * B_DH, s * L:(s + 1) * L], qm[:, s * L:(s + 1) * L]], axis=0)
         for s in range(nq) for h in range(B_HPG)], axis=1)

    last = vt_ref.shape[0] - 1

    def scores(kt):
        off = pl.multiple_of(jnp.minimum(kt, last) * L, L)
        idx = nq * qi - kt + nq
        bias = [bias_ref[jnp.clip(idx + s, 0, n_bt - 1)] for s in range(nq)]
        bias = bias[0] if nq == 1 else jnp.concatenate(bias, axis=1)
        return jnp.dot(ka_ref[pl.ds(off, L), :], q_aug_t, preferred_element_type=F32) + bias

    lo = 0 if mode == "sel" else jnp.maximum(nq * qi - (n_bt - nq - 1), 0)
    _flash_loop(lo, nq * qi + nq, scores, lambda kt: vt_ref[jnp.minimum(kt, last)], m_sc, acc_sc, s_sc, mx_sc)
    acc = acc_sc[...]
    o = acc[:B_DH] / jnp.maximum(acc[B_DH:B_DH + 1], 1e-30)
    gate = jax.nn.sigmoid(gl_ref[...])
    heads = []
    for h in range(B_HPG):
        cols = [o[:, (s * B_HPG + h) * L:(s * B_HPG + h + 1) * L] for s in range(nq)]
        heads.append((cols[0] if nq == 1 else jnp.concatenate(cols, axis=1)) * gate[h:h + 1])
    o_ref[...] = jnp.concatenate(heads, axis=0).T


def _nsa_branch(ha, k_aug, v_t, bias, glog_t, mask_t, b, t, mode, nq):
    L = ATT_TILE
    tq = nq * L
    n_bt = bias.shape[0]
    in_specs = [
        pl.BlockSpec((None, tq, 256), lambda bi, g, qi: (bi, qi, 16 + g)),
        pl.BlockSpec((None, t, LANES), lambda bi, g, qi: (bi, 0, g)),
        pl.BlockSpec((None, None, t // L, B_DH + ONES_ROWS, L), lambda bi, g, qi: (bi, g, 0, 0, 0)),
        pl.BlockSpec((n_bt, None, L, B_HPG * L), lambda bi, g, qi: (0, g, 0, 0)),
        pl.BlockSpec((None, None, B_HPG, tq), lambda bi, g, qi: (bi, g, 0, qi)),
    ]
    args = [ha, k_aug, v_t, bias, glog_t]
    if mode == "sel":
        n_sel = mask_t.shape[2]
        in_specs.append(pl.BlockSpec((None, None, n_sel, tq), lambda bi, g, qi: (bi, g, 0, qi)))
        args.append(mask_t)
    return pl.pallas_call(
        functools.partial(_nsa_body, mode=mode, n_bt=n_bt),
        grid=(b, B_GROUPS, t // tq),
        in_specs=in_specs,
        out_specs=pl.BlockSpec((None, tq, 256), lambda bi, g, qi: (bi, qi, g)),
        out_shape=jax.ShapeDtypeStruct((b, t, MIX_W), F32),
        scratch_shapes=[pltpu.VMEM((8, B_HPG * tq), F32), pltpu.VMEM((B_DH + ONES_ROWS, B_HPG * tq), F32),
                        pltpu.VMEM((2, L, B_HPG * tq), F32), pltpu.VMEM((2, 8, B_HPG * tq), F32)],
        compiler_params=_cparams(("parallel", "parallel", "arbitrary")),
        name="nsa_" + mode,
    )(*args)


def _mem_body(q_ref, k_ref, v_ref, o_ref):
    q = q_ref[...]
    outs = []
    for h in range(M_HEADS):
        sl = slice(h * M_DH, (h + 1) * M_DH)
        s = _dot_nt(q[:, sl], k_ref[:, sl])
        e = jnp.exp2(s - jnp.max(s, axis=1, keepdims=True))
        p = e / jnp.sum(e, axis=1, keepdims=True)
        outs.append(jnp.dot(p.astype(BF16), v_ref[:, sl], preferred_element_type=F32))
    o_ref[...] = jnp.concatenate(outs, axis=1).astype(o_ref.dtype)


def _memory_attention(ha, memkv, b, t, n_mem):
    tq = 512
    w = M_HEADS * M_DH
    return pl.pallas_call(
        _mem_body,
        grid=(b, t // tq),
        in_specs=[
            pl.BlockSpec((None, tq, w), lambda bi, qi: (bi, qi, 0)),
            pl.BlockSpec((n_mem, w), lambda bi, qi: (bi, 0)),
            pl.BlockSpec((n_mem, w), lambda bi, qi: (bi, 1)),
        ],
        out_specs=pl.BlockSpec((None, tq, w), lambda bi, qi: (bi, qi, 0)),
        out_shape=jax.ShapeDtypeStruct((b, t, w), BF16),
        compiler_params=_cparams(("parallel", "arbitrary")),
        name="memory_attention",
    )(ha, memkv, memkv)


def _merge_body(oa_ref, oc_ref, os_ref, ow_ref, om_ref, gl_ref, w_ref, o_ref):
    branches = (oa_ref[...], oc_ref[...] + os_ref[...] + ow_ref[...], om_ref[...])
    acc = jnp.zeros(o_ref.shape, F32)
    for n in range(3):
        up = jnp.dot(branches[n].astype(BF16), w_ref[n], preferred_element_type=F32)
        acc = acc + jax.nn.sigmoid(gl_ref[:, n * D_MODEL:(n + 1) * D_MODEL]) * up
    o_ref[...] = acc.astype(o_ref.dtype)


def _merge(o_a, o_c, o_s, o_w, o_m, hb, w_branch):
    n = o_a.shape[0]
    tm = 256
    row = lambda i: (i, 0)
    return pl.pallas_call(
        _merge_body,
        grid=(n // tm,),
        in_specs=[pl.BlockSpec((tm, MIX_W), row)] * 5 + [
            pl.BlockSpec((tm, 3 * D_MODEL), row),
            pl.BlockSpec((3, MIX_W, D_MODEL), lambda i: (0, 0, 0)),
        ],
        out_specs=pl.BlockSpec((tm, D_MODEL), row),
        out_shape=jax.ShapeDtypeStruct((n, D_MODEL), BF16),
        compiler_params=_cparams(("parallel",)),
        name="branch_merge",
    )(o_a, o_c, o_s, o_w, o_m, hb, w_branch)


def _layer_norm(z, g, b):
    mu = jnp.mean(z, axis=1, keepdims=True)
    zc = z - mu
    var = jnp.mean(zc * zc, axis=1, keepdims=True)
    return zc * lax.rsqrt(var + LN_EPS) * g + b


def _out_ln_body(y_ref, w_ref, x_ref, g_ref, b_ref, o_ref, ob_ref):
    y = jnp.dot(y_ref[...], w_ref[...], preferred_element_type=F32)
    o = _layer_norm(ALPHA * x_ref[...] + y, g_ref[...], b_ref[...])
    o_ref[...] = o
    ob_ref[...] = o.astype(BF16)


def _out_proj_ln(mixed, w_out, x, g, b):
    n = x.shape[0]
    tm = 512
    row = lambda i: (i, 0)
    const = lambda i: (0, 0)
    return pl.pallas_call(
        _out_ln_body,
        grid=(n // tm,),
        in_specs=[pl.BlockSpec((tm, D_MODEL), row), pl.BlockSpec((D_MODEL, D_MODEL), const),
                  pl.BlockSpec((tm, D_MODEL), row), pl.BlockSpec((1, D_MODEL), const),
                  pl.BlockSpec((1, D_MODEL), const)],
        out_specs=[pl.BlockSpec((tm, D_MODEL), row), pl.BlockSpec((tm, D_MODEL), row)],
        out_shape=[jax.ShapeDtypeStruct((n, D_MODEL), F32), jax.ShapeDtypeStruct((n, D_MODEL), BF16)],
        compiler_params=_cparams(("parallel",)),
        name="out_proj_ln1",
    )(mixed, w_out, x, g, b)


def _res_ln_body(x_ref, y_ref, g_ref, b_ref, o_ref):
    o_ref[...] = _layer_norm(ALPHA * x_ref[...] + y_ref[...], g_ref[...], b_ref[...])


def _residual_ln(x, y, g, b):
    n = x.shape[0]
    tm = 512
    row = lambda i: (i, 0)
    const = lambda i: (0, 0)
    return pl.pallas_call(
        _res_ln_body,
        grid=(n // tm,),
        in_specs=[pl.BlockSpec((tm, D_MODEL), row), pl.BlockSpec((tm, D_MODEL), row),
                  pl.BlockSpec((1, D_MODEL), const), pl.BlockSpec((1, D_MODEL), const)],
        out_specs=pl.BlockSpec((tm, D_MODEL), row),
        out_shape=jax.ShapeDtypeStruct((n, D_MODEL), F32),
        compiler_params=_cparams(("parallel",)),
        name="residual_ln2",
    )(x, y, g, b)


def _topk_axis0(v, k):
    r, n = v.shape
    iota = lax.broadcasted_iota(jnp.int32, (r, n), 0).astype(F32)
    slot = lax.broadcasted_iota(jnp.int32, (k, n), 0)

    def body(it, c):
        v, vals, idxs = c
        mx = jnp.max(v, axis=0, keepdims=True)
        idx = jnp.min(jnp.where(v == mx, iota, float(r)), axis=0, keepdims=True)
        v = jnp.where(iota == idx, -jnp.inf, v)
        return v, jnp.where(slot == it, mx, vals), jnp.where(slot == it, idx, idxs)

    _, vals, idxs = lax.fori_loop(0, k, body, (v, jnp.zeros((k, n), F32), jnp.zeros((k, n), F32)))
    return vals, idxs


def _pick_rows(table, pos, k):
    out = jnp.zeros(pos.shape, F32)
    for a in range(k):
        out = out + jnp.where(pos == float(a), table[a:a + 1], 0.0)
    return out


def _route_body(q_ref, keys_ref, ei_ref, ej_ref, g_ref):
    k = PEER_TOPK
    scores = _dot_nt(keys_ref[...], q_ref[...])
    v0, i0 = _topk_axis0(scores[:PEER_NKEYS], k)
    v1, i1 = _topk_axis0(scores[PEER_NKEYS:], k)
    counts = [k // (a + 1) for a in range(k)]
    starts = np.cumsum([0] + counts[:-1])
    pad = (-sum(counts)) % 8
    comb = jnp.concatenate([v0[a:a + 1] + v1[:counts[a]] for a in range(k)]
                           + [jnp.full((pad, v0.shape[1]), -jnp.inf, F32)], axis=0)
    sf, pos = _topk_axis0(comb, k)
    pa = jnp.zeros(pos.shape, F32)
    pb = pos
    for a in range(1, k):
        later = pos >= float(starts[a])
        pa = pa + jnp.where(later, 1.0, 0.0)
        pb = pb - jnp.where(later, float(counts[a - 1]), 0.0)
    ei_ref[...] = _pick_rows(i0, pa, k)
    ej_ref[...] = _pick_rows(i1, pb, k)
    e = jnp.exp(sf - jnp.max(sf, axis=0, keepdims=True))
    g_ref[...] = e / jnp.sum(e, axis=0, keepdims=True)


def _peer_route(q, keys):
    n = q.shape[0]
    tn = 256
    out = jax.ShapeDtypeStruct((PEER_HEADS, PEER_TOPK, n), F32)
    ospec = pl.BlockSpec((None, PEER_TOPK, tn), lambda i, h: (h, 0, i))
    return pl.pallas_call(
        _route_body,
        grid=(n // tn, PEER_HEADS),
        in_specs=[pl.BlockSpec((tn, PEER_DKEY), lambda i, h: (i, h)),
                  pl.BlockSpec((None, 2 * PEER_NKEYS, PEER_DKEY), lambda i, h: (h, 0, 0))],
        out_specs=[ospec, ospec, ospec],
        out_shape=[out, out, out],
        compiler_params=_cparams(("parallel", "arbitrary")),
        name="peer_route",
    )(q, keys)


def _gate_body(ei_ref, ej_ref, g_ref, o_ref):
    tb = ei_ref.shape[0]
    nk = PEER_NKEYS
    iota = lax.broadcasted_iota(jnp.int32, (tb, nk, ei_ref.shape[2]), 1).astype(F32)
    rows = jnp.where(iota == ei_ref[...], 1.0, 0.0).astype(BF16)
    cols = jnp.where(iota == ej_ref[...], g_ref[...], 0.0).astype(BF16)
    gm = lax.dot_general(rows, cols, (((2,), (2,)), ((0,), (0,))), preferred_element_type=F32)
    o_ref[...] = jnp.swapaxes(gm, 0, 1).astype(o_ref.dtype)


def _gate_matrix(ei, ej, g):
    n, _, slots = ei.shape
    tb = 64
    spec = pl.BlockSpec((tb, 1, slots), lambda i: (i, 0, 0))
    return pl.pallas_call(
        _gate_body,
        grid=(n // tb,),
        in_specs=[spec, spec, spec],
        out_specs=pl.BlockSpec((PEER_NKEYS, tb, PEER_NKEYS), lambda i: (0, i, 0)),
        out_shape=jax.ShapeDtypeStruct((PEER_NKEYS, n, PEER_NKEYS), BF16),
        compiler_params=_cparams(("parallel",)),
        name="peer_gate_matrix",
    )(ei, ej, g)


def _expert_body(x_ref, u_ref, g_ref, v_ref, o_ref):
    @pl.when(pl.program_id(1) == 0)
    def _():
        o_ref[...] = jnp.zeros(o_ref.shape, F32)

    hid = _dot_nt(x_ref[...], u_ref[...].astype(BF16))
    gate = jnp.concatenate([g_ref[i] for i in range(g_ref.shape[0])], axis=1)
    act = (jax.nn.gelu(hid) * gate.astype(F32)).astype(BF16)
    o_ref[...] += jnp.dot(act, v_ref[...].astype(BF16), preferred_element_type=F32)


def _peer_experts(xb, u, gm, v):
    n = xb.shape[0]
    ne = u.shape[0]
    tn, te = 1024, 512
    return pl.pallas_call(
        _expert_body,
        grid=(n // tn, ne // te),
        in_specs=[pl.BlockSpec((tn, D_MODEL), lambda i, j: (i, 0)),
                  pl.BlockSpec((te, D_MODEL), lambda i, j: (j, 0)),
                  pl.BlockSpec((te // PEER_NKEYS, tn, PEER_NKEYS), lambda i, j: (j, i, 0)),
                  pl.BlockSpec((te, D_MODEL), lambda i, j: (j, 0))],
        out_specs=pl.BlockSpec((tn, D_MODEL), lambda i, j: (i, 0)),
        out_shape=jax.ShapeDtypeStruct((n, D_MODEL), F32),
        compiler_params=_cparams(("parallel", "arbitrary")),
        name="peer_experts",
    )(xb, u, gm, v)


def _token_mixer(x, mem, w_in, diff_lambda, diff_subln, cmp_pe, cmp_w1, cmp_w2, w_mem_kv, w_branch, rel_bias):
    b, t, _ = x.shape
    n = b * t
    L = ATT_TILE
    xb = x.reshape(n, D_MODEL).astype(BF16)

    w_a, w_b = _w_prep(w_in.T)
    ha = _matmul(xb, w_a, BF16, 1024, 1664, "in_proj_a", w_transposed=True)
    hb = _matmul(xb, w_b, F32, 1024, 1664, "in_proj_b", w_transposed=True)
    ha3 = ha.reshape(b, t, ha.shape[1])

    tab1d = rel_bias[_rel_bucket(jnp.arange(t))].T * LOG2E
    n_far = min(t // L, REL_MAX_DIST // L + 2)
    bias_a = _bias_tiles(tab1d[:A_HEADS], t, -1, n_far + 1, t, 1)
    bias_b = tab1d[A_HEADS:]
    bias_sel = _bias_tiles(bias_b, t, -SEL_NQ, n_far + SEL_NQ, t, B_HPG)
    n_win = min(t // L, WINDOW // L + 1)
    bias_win = _bias_tiles(bias_b, t, -1, n_win + 1, WINDOW, B_HPG)

    def values_t(v):
        heads, dv = v.shape[2:]
        vt = v.reshape(b, t // L, L, heads, dv).transpose(0, 3, 1, 4, 2)
        return jnp.concatenate([vt, jnp.ones((b, heads, t // L, ONES_ROWS, L), BF16)], axis=3)

    o_a = _diff_attention(ha3, values_t(ha3[:, :, 3072:4096].reshape(b, t, A_HEADS, A_DV)), bias_a, diff_lambda,
                          jnp.broadcast_to(diff_subln[:, None], (A_DV, LANES)), b, t)

    glog_t = hb[:, 6144:6192].reshape(b, t, 3, B_GROUPS, B_HPG).transpose(2, 0, 3, 4, 1)
    nr = t // CMP_STRIDE
    w1 = cmp_w1.reshape(2, CMP_LEN, B_DH, CMP_HIDDEN)
    z1 = jnp.zeros_like(w1[0])
    w1_bd = jnp.concatenate([jnp.concatenate([w1[0], z1], axis=2), jnp.concatenate([z1, w1[1]], axis=2)],
                            axis=1).astype(BF16)
    z2 = jnp.zeros_like(cmp_w2[0])
    w2_bd = jnp.concatenate([jnp.concatenate([cmp_w2[0], z2], axis=1), jnp.concatenate([z2, cmp_w2[1]], axis=1)],
                            axis=0).astype(BF16)
    pe_bd = jnp.broadcast_to(jnp.concatenate([cmp_pe[0], cmp_pe[1]], axis=1)[:, None, :],
                             (CMP_LEN, 8, 2 * B_DH)).astype(BF16)
    kvc = _compress(ha3[:, :, 5120:5632].astype(F32), w1_bd, w2_bd, pe_bd)
    n_c = (t - CMP_LEN) // CMP_STRIDE + 1
    n_sel = t // SEL_BLOCK
    cidx = np.arange(nr)[:, None] * CMP_STRIDE + np.arange(CMP_LEN)[None, :]
    overlap = (cidx[:, :, None] // SEL_BLOCK == np.arange(n_sel)[None, None, :]).astype(np.float32).mean(axis=1)
    overlap[n_c:] = 0.0
    overlap_t = np.zeros((LANES, nr), np.float32)
    overlap_t[:n_sel] = overlap.T
    o_c, sel_mask = _nsa_compressed(ha3, kvc, jnp.asarray(overlap_t, BF16), glog_t[0], b, t, n_c,
                                    min(SEL_TOPN, n_sel))
    kv_sel = ha3[:, :, 5632:6144].reshape(b, t, B_GROUPS, 2, B_DH)
    kv_win = ha3[:, :, 6144:6656].reshape(b, t, B_GROUPS, 2, B_DH)
    assert n_sel <= B_DH
    hot = (np.arange(t)[:, None] // SEL_BLOCK == np.arange(B_DH)[None, :]).astype(np.float32)
    hot = jnp.broadcast_to(jnp.asarray(hot, BF16)[None, :, None, :], (b, t, B_GROUPS, B_DH))
    ka_sel = jnp.concatenate([kv_sel[:, :, :, 0], hot], axis=3).reshape(b, t, B_GROUPS * LANES)
    ka_win = jnp.concatenate([kv_win[:, :, :, 0], jnp.zeros_like(hot)], axis=3).reshape(b, t, B_GROUPS * LANES)
    o_s = _nsa_branch(ha3, ka_sel, values_t(kv_sel[:, :, :, 1]), bias_sel, glog_t[1], sel_mask, b, t, "sel", SEL_NQ)
    o_w = _nsa_branch(ha3, ka_win, values_t(kv_win[:, :, :, 1]), bias_win, glog_t[2], None, b, t, "win", 1)

    n_mem = mem.shape[1]
    memkv = _matmul(mem.reshape(b * n_mem, D_MODEL).astype(BF16), w_mem_kv.astype(BF16), BF16,
                    b * n_mem, 1024, "mem_kv_proj")
    o_m = _memory_attention(ha3, memkv, b, t, n_mem)

    flat = lambda o: o.reshape(n, MIX_W)
    return flat(o_a), flat(o_c), flat(o_s), flat(o_w), flat(o_m), hb


def _peer(x1, x1b, peer_wq, peer_keys, peer_u, peer_v):
    n = x1.shape[0]
    q = _matmul(x1b, peer_wq.astype(BF16), BF16, 1024, 1024, "peer_query")
    zk = jnp.zeros((PEER_HEADS, PEER_NKEYS, PEER_DKEY // 2), F32)
    keys_bd = jnp.concatenate([jnp.concatenate([peer_keys[:, 0], zk], axis=2),
                               jnp.concatenate([zk, peer_keys[:, 1]], axis=2)], axis=1).astype(BF16)
    ei, ej, gate = _peer_route(q, keys_bd)
    slots = lambda a: a.reshape(PEER_HEADS * PEER_TOPK, n).T.reshape(n, 1, PEER_HEADS * PEER_TOPK)
    gm = _gate_matrix(slots(ei), slots(ej), slots(gate))
    return _peer_experts(x1b, peer_u, gm, peer_v)


def kernel(x, mem, w_in, diff_lambda, diff_subln, cmp_pe, cmp_w1, cmp_w2, w_mem_kv, w_branch, w_out, ln1_g, ln1_b,
           peer_wq, peer_keys, peer_u, peer_v, ln2_g, ln2_b, rel_bias):
    b, t, _ = x.shape
    n = b * t
    for l in range(DEPTH):
        o_a, o_c, o_s, o_w, o_m, hb = _token_mixer(x, mem, w_in[l], diff_lambda[l], diff_subln[l], cmp_pe[l],
                                                   cmp_w1[l], cmp_w2[l], w_mem_kv[l], w_branch[l], rel_bias)
        mixed = _merge(o_a, o_c, o_s, o_w, o_m, hb, w_branch[l].astype(BF16))
        x1, x1b = _out_proj_ln(mixed, w_out[l].astype(BF16), x.reshape(n, D_MODEL),
                               ln1_g[l].reshape(1, D_MODEL), ln1_b[l].reshape(1, D_MODEL))
        y = _peer(x1, x1b, peer_wq[l], peer_keys[l], peer_u[l], peer_v[l])
        x = _residual_ln(x1, y, ln2_g[l].reshape(1, D_MODEL), ln2_b[l].reshape(1, D_MODEL)).reshape(b, t, D_MODEL)
    return x
```

```python
import functools
import math

import numpy as np
import jax
import jax.numpy as jnp
from jax import lax
from jax.experimental import pallas as pl
from jax.experimental.pallas import tpu as pltpu

F32 = jnp.float32
BF16 = jnp.bfloat16

D_MODEL = 2048
A_HEADS, A_DQK, A_DV = 8, 64, 128
B_HEADS, B_GROUPS, B_HPG, B_DH = 16, 4, 4, 64
CMP_LEN, CMP_STRIDE, CMP_HIDDEN = 32, 16, 256
SEL_BLOCK, SEL_TOPN, WINDOW = 64, 16, 512
M_HEADS, M_DH = 4, 256
REL_BUCKETS, REL_MAX_DIST = 32, 1024
PEER_HEADS, PEER_NKEYS, PEER_DKEY, PEER_TOPK = 8, 128, 256, 16
MIX_W = 1024
LN_EPS = 1e-5
FORCE = 1e9
DEPTH = 1
ALPHA = (2 * DEPTH) ** 0.25
LAMBDA_INIT = 0.8 - 0.6 * math.exp(-0.3 * 0)
LOG2E = math.log2(math.e)

NEG = -1e30
ATT_TILE = 256
LANES = 128
ONES_ROWS = 16
SEL_NQ = 1
VMEM_LIMIT = 56 * 1024 * 1024


def _cparams(sem):
    return pltpu.CompilerParams(dimension_semantics=sem, vmem_limit_bytes=VMEM_LIMIT)


def _dot_nt(a, b):
    return lax.dot_general(a, b, (((1,), (1,)), ((), ())), preferred_element_type=F32)


def _mm_body(x_ref, w_ref, o_ref):
    o_ref[...] = jnp.dot(x_ref[...], w_ref[...], preferred_element_type=F32).astype(o_ref.dtype)


def _mm_nt_body(x_ref, w_ref, o_ref):
    o_ref[...] = _dot_nt(x_ref[...], w_ref[...]).astype(o_ref.dtype)


def _matmul(x, w, out_dtype, tm, tn, name, w_transposed=False):
    m, k = x.shape
    n = w.shape[0] if w_transposed else w.shape[1]
    w_spec = pl.BlockSpec((tn, k), lambda i, j: (j, 0)) if w_transposed else pl.BlockSpec((k, tn), lambda i, j: (0, j))
    return pl.pallas_call(
        _mm_nt_body if w_transposed else _mm_body,
        grid=(m // tm, n // tn),
        in_specs=[pl.BlockSpec((tm, k), lambda i, j: (i, 0)), w_spec],
        out_specs=pl.BlockSpec((tm, tn), lambda i, j: (i, j)),
        out_shape=jax.ShapeDtypeStruct((m, n), out_dtype),
        compiler_params=_cparams(("parallel", "arbitrary")),
        name=name,
    )(x, w)


W_A_COLS = 6656
W_B_COLS = 6656


def _w_prep_body(w_ref, wa_ref, wb_ref):
    def put(dst_ref, d0, s0, n, c=1.0):
        val = w_ref[s0:s0 + n, :]
        dst_ref[d0:d0 + n, :] = (val * c if c != 1.0 else val).astype(dst_ref.dtype)

    put(wa_ref, 0, 5680, 1024, M_DH ** -0.5 * LOG2E)
    put(wa_ref, 1024, 0, 1024, A_DQK ** -0.5 * LOG2E)
    put(wa_ref, 2048, 1024, 2048)
    put(wa_ref, 4096, 3072, 1024, B_DH ** -0.5 * LOG2E)
    for br in range(3):
        for g in range(B_GROUPS):
            src = 4096 + br * 2 * B_GROUPS * B_DH + g * B_DH
            dst = 5120 + (br * B_GROUPS + g) * 2 * B_DH
            put(wa_ref, dst, src, B_DH)
            put(wa_ref, dst + B_DH, src + B_GROUPS * B_DH, B_DH)
    put(wb_ref, 0, 6704, 6144)
    put(wb_ref, 6144, 5632, 48)
    wb_ref[6144 + 48:, :] = jnp.zeros((W_B_COLS - 6144 - 48, wb_ref.shape[1]), wb_ref.dtype)


def _w_prep(w_in_t):
    c, k = w_in_t.shape
    tk = 256
    return pl.pallas_call(
        _w_prep_body,
        grid=(k // tk,),
        in_specs=[pl.BlockSpec((c, tk), lambda i: (0, i))],
        out_specs=[pl.BlockSpec((W_A_COLS, tk), lambda i: (0, i)), pl.BlockSpec((W_B_COLS, tk), lambda i: (0, i))],
        out_shape=[jax.ShapeDtypeStruct((W_A_COLS, k), BF16), jax.ShapeDtypeStruct((W_B_COLS, k), BF16)],
        compiler_params=_cparams(("parallel",)),
        name="w_in_regroup",
    )(w_in_t)


def _rel_bucket(dist):
    n = jnp.maximum(dist, 0)
    max_exact = REL_BUCKETS // 2
    nf = jnp.maximum(n, 1).astype(jnp.float32)
    large = max_exact + (jnp.log(nf / max_exact) / math.log(REL_MAX_DIST / max_exact)
                         * (REL_BUCKETS - max_exact)).astype(jnp.int32)
    large = jnp.minimum(large, REL_BUCKETS - 1)
    return jnp.where(n < max_exact, n, large)


def _bias_tiles(tab1d, t, first, n_tiles, max_dist, hpr):
    L = ATT_TILE
    m = np.arange(2 * L)
    off = np.where(m <= L, m, m - 2 * L)
    d = (first + np.arange(n_tiles))[:, None] * L + off[None, :]
    ok = (d >= 0) & (d < max_dist)
    h = tab1d.shape[0]
    rp = jnp.where(ok[:, None], tab1d.T[np.clip(d, 0, t - 1)].transpose(0, 2, 1), NEG)
    return pl.pallas_call(
        functools.partial(_toeplitz_body, hpr=hpr),
        grid=(n_tiles,),
        in_specs=[pl.BlockSpec((None, h, 1, 2 * L), lambda c: (c, 0, 0, 0))],
        out_specs=pl.BlockSpec((None, h // hpr, L, hpr * L), lambda c: (c, 0, 0, 0)),
        out_shape=jax.ShapeDtypeStruct((n_tiles, h // hpr, L, hpr * L), F32),
        compiler_params=_cparams(("parallel",)),
        name="bias_tiles",
    )(rp.reshape(n_tiles, h, 1, 2 * L))


def _toeplitz_body(rp_ref, o_ref, *, hpr):
    L = o_ref.shape[1]
    for hd in range(rp_ref.shape[0]):
        rows = jnp.broadcast_to(rp_ref[hd], (L, 2 * L))
        tile = pltpu.roll(rows, 0, 1, stride=1, stride_axis=0)[:, :L]
        o_ref[hd // hpr, :, (hd % hpr) * L:(hd % hpr + 1) * L] = tile


def _flash_loop(lo, hi, scores, values, m_ref, acc_ref, s_ref, mx_ref):
    def ahead(kt, slot):
        s = scores(kt)
        s_ref[slot] = s
        mx_ref[slot] = jnp.broadcast_to(jnp.max(s, axis=0, keepdims=True), mx_ref.shape[1:])

    def finish(kt, slot):
        m_prev = m_ref[...]
        m_new = jnp.maximum(m_prev, mx_ref[slot])
        alpha = jnp.exp2(m_prev - m_new)
        p = jnp.exp2(s_ref[slot] - m_new[0:1])
        acc_ref[...] = alpha[0:1] * acc_ref[...] + jnp.dot(values(kt), p.astype(BF16), preferred_element_type=F32)
        m_ref[...] = m_new

    def body(i, carry):
        kt = lo + 2 * i
        ahead(kt + 1, 1)
        finish(kt, 0)
        ahead(kt + 2, 0)
        finish(kt + 1, 1)
        return carry

    ahead(lo, 0)
    lax.fori_loop(0, (hi - lo + 1) // 2, body, 0)


def _diff_body(q_ref, k_ref, vt_ref, bias_ref, lam_ref, g_ref, o_ref, m_sc, acc_sc, s_sc, mx_sc, *, n_bt):
    L = ATT_TILE
    tq = 2 * L
    last = vt_ref.shape[0] - 1
    qi = pl.program_id(2)
    m_sc[...] = jnp.full(m_sc.shape, NEG, F32)
    acc_sc[...] = jnp.zeros(acc_sc.shape, F32)
    qt = q_ref[...].astype(F32).T.astype(BF16)
    zero = jnp.zeros((A_DQK, tq), BF16)
    q_cat = jnp.concatenate([jnp.concatenate([qt[:A_DQK], zero], axis=0),
                             jnp.concatenate([zero, qt[A_DQK:]], axis=0)], axis=1)

    def scores(kt):
        off = pl.multiple_of(jnp.minimum(kt, last) * L, L)
        k = k_ref[pl.ds(off, L), :]
        d0 = 2 * qi - kt
        bias = jnp.concatenate([bias_ref[jnp.clip(d0 + 1, 0, n_bt - 1)],
                                bias_ref[jnp.clip(d0 + 2, 0, n_bt - 1)]], axis=1)
        s = jnp.dot(k, q_cat, preferred_element_type=F32)
        return jnp.concatenate([s[:, :tq] + bias, s[:, tq:] + bias], axis=1)

    ones = jnp.ones((ONES_ROWS, L), BF16)

    def values(kt):
        return jnp.concatenate([vt_ref[jnp.minimum(kt, last)], ones], axis=0)

    _flash_loop(0, 2 * qi + 2, scores, values, m_sc, acc_sc, s_sc, mx_sc)

    lp = lam_ref[...]
    lam = (jnp.exp(jnp.sum(lp[0:1] * lp[1:2], axis=1, keepdims=True))
           - jnp.exp(jnp.sum(lp[2:3] * lp[3:4], axis=1, keepdims=True)) + LAMBDA_INIT)
    acc = acc_sc[...]
    o0 = acc[:A_DV, :tq] / jnp.maximum(acc[A_DV:A_DV + 1, :tq], 1e-30)
    o1 = acc[:A_DV, tq:] / jnp.maximum(acc[A_DV:A_DV + 1, tq:], 1e-30)
    o = o0 - lam * o1
    g = jnp.concatenate([g_ref[...]] * (tq // LANES), axis=1)
    o = o * lax.rsqrt(jnp.mean(o * o, axis=0, keepdims=True) + LN_EPS) * g
    o_ref[...] = (o * (1.0 - LAMBDA_INIT)).T.astype(o_ref.dtype)


def _diff_attention(ha, v_t, bias, lam_params, subln, b, t):
    L = ATT_TILE
    tq = 2 * L
    n_bt = bias.shape[0]
    return pl.pallas_call(
        functools.partial(_diff_body, n_bt=n_bt),
        grid=(b, A_HEADS, t // tq),
        in_specs=[
            pl.BlockSpec((None, tq, 128), lambda bi, h, qi: (bi, qi, 8 + h)),
            pl.BlockSpec((None, t, 128), lambda bi, h, qi: (bi, 0, 16 + h)),
            pl.BlockSpec((None, None, t // L, A_DV, L), lambda bi, h, qi: (bi, h, 0, 0, 0)),
            pl.BlockSpec((n_bt, None, L, L), lambda bi, h, qi: (0, h, 0, 0)),
            pl.BlockSpec((4, A_DQK), lambda bi, h, qi: (0, 0)),
            pl.BlockSpec((A_DV, LANES), lambda bi, h, qi: (0, 0)),
        ],
        out_specs=pl.BlockSpec((None, tq, 128), lambda bi, h, qi: (bi, qi, h)),
        out_shape=jax.ShapeDtypeStruct((b, t, MIX_W), BF16),
        scratch_shapes=[pltpu.VMEM((8, 2 * tq), F32), pltpu.VMEM((A_DV + ONES_ROWS, 2 * tq), F32),
                        pltpu.VMEM((2, L, 2 * tq), F32), pltpu.VMEM((2, 8, 2 * tq), F32)],
        compiler_params=_cparams(("parallel", "parallel", "arbitrary")),
        name="diff_attention",
    )(ha, ha, v_t, bias, lam_params, subln)


def _compress_body(kv_ref, w1_ref, w2_ref, pe_ref, o_ref):
    nr = o_ref.shape[0]
    first = jnp.zeros((nr, 2 * CMP_HIDDEN), F32)
    second = jnp.zeros((nr, 2 * CMP_HIDDEN), F32)
    for l in range(CMP_STRIDE):
        rows = kv_ref[pl.ds(l, nr, stride=CMP_STRIDE), :].astype(BF16)
        first = first + jnp.dot(rows, w1_ref[l], preferred_element_type=F32)
        second = second + jnp.dot(rows, w1_ref[CMP_STRIDE + l], preferred_element_type=F32)
    pw = jnp.zeros((8, 2 * CMP_HIDDEN), F32)
    for l in range(CMP_LEN):
        pw = pw + jnp.dot(pe_ref[l], w1_ref[l], preferred_element_type=F32)
    second = jnp.concatenate([second[1:], second[:1]], axis=0)
    hdn = jax.nn.gelu(first + second + pw[0:1])
    o_ref[...] = jnp.dot(hdn.astype(BF16), w2_ref[...], preferred_element_type=F32)


def _compress(kv, w1_bd, w2_bd, pe_bd):
    b, t, _ = kv.shape
    nr = t // CMP_STRIDE
    return pl.pallas_call(
        _compress_body,
        grid=(b, B_GROUPS),
        in_specs=[
            pl.BlockSpec((None, t, LANES), lambda bi, gi: (bi, 0, gi)),
            pl.BlockSpec(w1_bd.shape, lambda bi, gi: (0, 0, 0)),
            pl.BlockSpec(w2_bd.shape, lambda bi, gi: (0, 0)),
            pl.BlockSpec(pe_bd.shape, lambda bi, gi: (0, 0, 0)),
        ],
        out_specs=pl.BlockSpec((None, None, nr, 2 * B_DH), lambda bi, gi: (bi, gi, 0, 0)),
        out_shape=jax.ShapeDtypeStruct((b, B_GROUPS, nr, 2 * B_DH), F32),
        compiler_params=_cparams(("parallel", "parallel")),
        name="nsa_compress",
    )(kv, w1_bd, w2_bd, pe_bd)


def _topk_mask_axis0(v, k):
    r, n = v.shape
    iota = lax.broadcasted_iota(jnp.int32, (r, n), 0).astype(F32)

    def body(_, c):
        v, sel = c
        mx = jnp.max(v, axis=0, keepdims=True)
        idx = jnp.min(jnp.where(v == mx, iota, float(r)), axis=0, keepdims=True)
        hit = iota == idx
        return jnp.where(hit, -jnp.inf, v), jnp.where(hit, 1.0, sel)

    _, sel = lax.fori_loop(0, k, body, (v, jnp.zeros((r, n), F32)))
    return sel


def _cmp_body(q_ref, kvc_ref, ov_ref, gl_ref, o_ref, mask_ref, *, n_c, n_top):
    L = ATT_TILE
    qi = pl.program_id(2)
    ncp = kvc_ref.shape[0]
    n_sel = mask_ref.shape[0]
    qt = q_ref[...].astype(F32).T.astype(BF16)
    qh = jnp.concatenate([qt[h * B_DH:(h + 1) * B_DH] for h in range(B_HPG)], axis=1)
    kvc = kvc_ref[...]
    s = jnp.dot(kvc[:, :B_DH].astype(BF16), qh, preferred_element_type=F32)
    tcol = qi * L + lax.broadcasted_iota(jnp.int32, (1, L), 1)
    crow = lax.broadcasted_iota(jnp.int32, (ncp, 1), 0)
    seen = jnp.where(crow < n_c, crow * CMP_STRIDE + (CMP_LEN - 1), jnp.int32(2 ** 30)) <= tcol
    valid = jnp.concatenate([seen] * B_HPG, axis=1)
    s = jnp.where(valid, s, NEG)
    mx = jnp.max(s, axis=0, keepdims=True)
    e = jnp.where(valid, jnp.exp2(s - mx), 0.0)
    p = e / jnp.maximum(jnp.sum(e, axis=0, keepdims=True), 1e-30)
    o = jnp.dot(kvc.T[B_DH:].astype(BF16), p.astype(BF16), preferred_element_type=F32)
    gate = jax.nn.sigmoid(gl_ref[...])
    o = jnp.concatenate([o[:, h * L:(h + 1) * L] * gate[h:h + 1] for h in range(B_HPG)], axis=0)
    o_ref[...] = o.T

    psum = p[:, :L] + p[:, L:2 * L] + p[:, 2 * L:3 * L] + p[:, 3 * L:]
    ov_t = ov_ref[...]
    imp = jnp.zeros((ov_t.shape[0], L), F32)
    rem = psum
    for _ in range(3):
        part = rem.astype(BF16)
        imp = imp + jnp.dot(ov_t, part, preferred_element_type=F32)
        rem = rem - part.astype(F32)
    imp = imp[:n_sel]
    blk = lax.broadcasted_iota(jnp.int32, (n_sel, 1), 0)
    cur = jnp.right_shift(tcol, int(math.log2(SEL_BLOCK)))
    imp = jnp.where(blk * SEL_BLOCK > tcol, -FORCE, imp)
    imp = jnp.where(blk == 0, FORCE, imp)
    imp = jnp.where(blk == cur, FORCE, imp)
    imp = jnp.where(blk == cur - 1, FORCE, imp)
    mask_ref[...] = jnp.where(_topk_mask_axis0(imp, n_top) > 0.0, 0.0, NEG).astype(mask_ref.dtype)


def _nsa_compressed(ha, kvc, overlap_t, glog_t, b, t, n_c, n_top):
    L = ATT_TILE
    ncp = kvc.shape[2]
    n_sel = t // SEL_BLOCK
    return pl.pallas_call(
        functools.partial(_cmp_body, n_c=n_c, n_top=n_top),
        grid=(b, B_GROUPS, t // L),
        in_specs=[
            pl.BlockSpec((None, L, 256), lambda bi, g, qi: (bi, qi, 16 + g)),
            pl.BlockSpec((None, None, ncp, 2 * B_DH), lambda bi, g, qi: (bi, g, 0, 0)),
            pl.BlockSpec(overlap_t.shape, lambda bi, g, qi: (0, 0)),
            pl.BlockSpec((None, None, B_HPG, L), lambda bi, g, qi: (bi, g, 0, qi)),
        ],
        out_specs=[
            pl.BlockSpec((None, L, 256), lambda bi, g, qi: (bi, qi, g)),
            pl.BlockSpec((None, None, n_sel, L), lambda bi, g, qi: (bi, g, 0, qi)),
        ],
        out_shape=[jax.ShapeDtypeStruct((b, t, MIX_W), F32),
                   jax.ShapeDtypeStruct((b, B_GROUPS, n_sel, t), BF16)],
        compiler_params=_cparams(("parallel", "parallel", "arbitrary")),
        name="nsa_compressed_select",
    )(ha, kvc, overlap_t, glog_t)


def _nsa_body(*refs, mode, n_bt):
    if mode == "sel":
        q_ref, kv_ref, vt_ref, bias_ref, gl_ref, mask_ref, hot_ref, o_ref, m_sc, acc_sc, s_sc, mx_sc = refs
    else:
        q_ref, kv_ref, vt_ref, bias_ref, gl_ref, o_ref, m_sc, acc_sc, s_sc, mx_sc = refs
    L = ATT_TILE
    nq = q_ref.shape[0] // L
    qi = pl.program_id(2)
    m_sc[...] = jnp.full(m_sc.shape, NEG, F32)
    acc_sc[...] = jnp.zeros(acc_sc.shape, F32)
    qt = q_ref[...].astype(F32).T.astype(BF16)
    if mode == "sel":
        n_sel = mask_ref.shape[0]
        qm = mask_ref[...]
        if n_sel < B_DH:
            qm = jnp.concatenate([qm, jnp.zeros((B_DH - n_sel, nq * L), BF16)], axis=0)
    else:
        qm = jnp.zeros((B_DH, nq * L), BF16)
    q_aug_t = jnp.concatenate(
        [jnp.concatenate([qt[h * B_DH:(h + 1) * B_DH, s * L:(s + 1) * L], qm[:, s * L:(s + 1) * L]], axis=0)
         for s in range(nq) for h in range(B_HPG)], axis=1)

    last = vt_ref.shape[0] - 1
    ones = jnp.ones((ONES_ROWS, L), BF16)
    k_lanes = lax.broadcasted_iota(jnp.int32, (L, LANES), 1) < B_DH

    def scores(kt):
        off = pl.multiple_of(jnp.minimum(kt, last) * L, L)
        idx = nq * qi - kt + nq
        bias = [bias_ref[jnp.clip(idx + s, 0, n_bt - 1)] for s in range(nq)]
        bias = bias[0] if nq == 1 else jnp.concatenate(bias, axis=1)
        k_aug = kv_ref[pl.ds(off, L), :]
        if mode == "sel":
            k_aug = jnp.where(k_lanes, k_aug, hot_ref[pl.ds(off, L), :])
        return jnp.dot(k_aug, q_aug_t, preferred_element_type=F32) + bias

    def values(kt):
        return jnp.concatenate([vt_ref[jnp.minimum(kt, last)], ones], axis=0)

    lo = 0 if mode == "sel" else jnp.maximum(nq * qi - (n_bt - nq - 1), 0)
    _flash_loop(lo, nq * qi + nq, scores, values, m_sc, acc_sc, s_sc, mx_sc)
    acc = acc_sc[...]
    o = acc[:B_DH] / jnp.maximum(acc[B_DH:B_DH + 1], 1e-30)
    gate = jax.nn.sigmoid(gl_ref[...])
    heads = []
    for h in range(B_HPG):
        cols = [o[:, (s * B_HPG + h) * L:(s * B_HPG + h + 1) * L] for s in range(nq)]
        heads.append((cols[0] if nq == 1 else jnp.concatenate(cols, axis=1)) * gate[h:h + 1])
    o_ref[...] = jnp.concatenate(heads, axis=0).T


def _nsa_branch(ha, v_t, bias, glog_t, mask_t, hot, b, t, mode, nq):
    L = ATT_TILE
    tq = nq * L
    n_bt = bias.shape[0]
    branch = 1 if mode == "sel" else 2
    in_specs = [
        pl.BlockSpec((None, tq, 256), lambda bi, g, qi: (bi, qi, 16 + g)),
        pl.BlockSpec((None, t, LANES), lambda bi, g, qi: (bi, 0, 40 + 4 * branch + g)),
        pl.BlockSpec((None, None, t // L, B_DH, L), lambda bi, g, qi: (bi, g, 0, 0, 0)),
        pl.BlockSpec((n_bt, None, L, B_HPG * L), lambda bi, g, qi: (0, g, 0, 0)),
        pl.BlockSpec((None, None, B_HPG, tq), lambda bi, g, qi: (bi, g, 0, qi)),
    ]
    args = [ha, ha, v_t, bias, glog_t]
    if mode == "sel":
        n_sel = mask_t.shape[2]
        in_specs.append(pl.BlockSpec((None, None, n_sel, tq), lambda bi, g, qi: (bi, g, 0, qi)))
        in_specs.append(pl.BlockSpec((t, LANES), lambda bi, g, qi: (0, 0)))
        args += [mask_t, hot]
    return pl.pallas_call(
        functools.partial(_nsa_body, mode=mode, n_bt=n_bt),
        grid=(b, B_GROUPS, t // tq),
        in_specs=in_specs,
        out_specs=pl.BlockSpec((None, tq, 256), lambda bi, g, qi: (bi, qi, g)),
        out_shape=jax.ShapeDtypeStruct((b, t, MIX_W), F32),
        scratch_shapes=[pltpu.VMEM((8, B_HPG * tq), F32), pltpu.VMEM((B_DH + ONES_ROWS, B_HPG * tq), F32),
                        pltpu.VMEM((2, L, B_HPG * tq), F32), pltpu.VMEM((2, 8, B_HPG * tq), F32)],
        compiler_params=_cparams(("parallel", "parallel", "arbitrary")),
        name="nsa_" + mode,
    )(*args)


def _mem_body(q_ref, k_ref, v_ref, o_ref):
    q = q_ref[...]
    outs = []
    for h in range(M_HEADS):
        sl = slice(h * M_DH, (h + 1) * M_DH)
        s = _dot_nt(q[:, sl], k_ref[:, sl])
        e = jnp.exp2(s - jnp.max(s, axis=1, keepdims=True))
        p = e / jnp.sum(e, axis=1, keepdims=True)
        outs.append(jnp.dot(p.astype(BF16), v_ref[:, sl], preferred_element_type=F32))
    o_ref[...] = jnp.concatenate(outs, axis=1).astype(o_ref.dtype)


def _memory_attention(ha, memkv, b, t, n_mem):
    tq = 512
    w = M_HEADS * M_DH
    return pl.pallas_call(
        _mem_body,
        grid=(b, t // tq),
        in_specs=[
            pl.BlockSpec((None, tq, w), lambda bi, qi: (bi, qi, 0)),
            pl.BlockSpec((n_mem, w), lambda bi, qi: (bi, 0)),
            pl.BlockSpec((n_mem, w), lambda bi, qi: (bi, 1)),
        ],
        out_specs=pl.BlockSpec((None, tq, w), lambda bi, qi: (bi, qi, 0)),
        out_shape=jax.ShapeDtypeStruct((b, t, w), BF16),
        compiler_params=_cparams(("parallel", "arbitrary")),
        name="memory_attention",
    )(ha, memkv, memkv)


def _merge_body(oa_ref, oc_ref, os_ref, ow_ref, om_ref, gl_ref, w_ref, o_ref):
    branches = (oa_ref[...], oc_ref[...] + os_ref[...] + ow_ref[...], om_ref[...])
    acc = jnp.zeros(o_ref.shape, F32)
    for n in range(3):
        up = jnp.dot(branches[n].astype(BF16), w_ref[n], preferred_element_type=F32)
        acc = acc + jax.nn.sigmoid(gl_ref[:, n * D_MODEL:(n + 1) * D_MODEL]) * up
    o_ref[...] = acc.astype(o_ref.dtype)


def _merge(o_a, o_c, o_s, o_w, o_m, hb, w_branch):
    n = o_a.shape[0]
    tm = 256
    row = lambda i: (i, 0)
    return pl.pallas_call(
        _merge_body,
        grid=(n // tm,),
        in_specs=[pl.BlockSpec((tm, MIX_W), row)] * 5 + [
            pl.BlockSpec((tm, 3 * D_MODEL), row),
            pl.BlockSpec((3, MIX_W, D_MODEL), lambda i: (0, 0, 0)),
        ],
        out_specs=pl.BlockSpec((tm, D_MODEL), row),
        out_shape=jax.ShapeDtypeStruct((n, D_MODEL), BF16),
        compiler_params=_cparams(("parallel",)),
        name="branch_merge",
    )(o_a, o_c, o_s, o_w, o_m, hb, w_branch)


def _layer_norm(z, g, b):
    mu = jnp.mean(z, axis=1, keepdims=True)
    zc = z - mu
    var = jnp.mean(zc * zc, axis=1, keepdims=True)
    return zc * lax.rsqrt(var + LN_EPS) * g + b


def _out_ln_body(y_ref, w_ref, x_ref, g_ref, b_ref, o_ref, ob_ref):
    y = jnp.dot(y_ref[...], w_ref[...], preferred_element_type=F32)
    o = _layer_norm(ALPHA * x_ref[...] + y, g_ref[...], b_ref[...])
    o_ref[...] = o
    ob_ref[...] = o.astype(BF16)


def _out_proj_ln(mixed, w_out, x, g, b):
    n = x.shape[0]
    tm = 512
    row = lambda i: (i, 0)
    const = lambda i: (0, 0)
    return pl.pallas_call(
        _out_ln_body,
        grid=(n // tm,),
        in_specs=[pl.BlockSpec((tm, D_MODEL), row), pl.BlockSpec((D_MODEL, D_MODEL), const),
                  pl.BlockSpec((tm, D_MODEL), row), pl.BlockSpec((1, D_MODEL), const),
                  pl.BlockSpec((1, D_MODEL), const)],
        out_specs=[pl.BlockSpec((tm, D_MODEL), row), pl.BlockSpec((tm, D_MODEL), row)],
        out_shape=[jax.ShapeDtypeStruct((n, D_MODEL), F32), jax.ShapeDtypeStruct((n, D_MODEL), BF16)],
        compiler_params=_cparams(("parallel",)),
        name="out_proj_ln1",
    )(mixed, w_out, x, g, b)


def _res_ln_body(x_ref, y_ref, g_ref, b_ref, o_ref):
    o_ref[...] = _layer_norm(ALPHA * x_ref[...] + y_ref[...], g_ref[...], b_ref[...])


def _residual_ln(x, y, g, b):
    n = x.shape[0]
    tm = 512
    row = lambda i: (i, 0)
    const = lambda i: (0, 0)
    return pl.pallas_call(
        _res_ln_body,
        grid=(n // tm,),
        in_specs=[pl.BlockSpec((tm, D_MODEL), row), pl.BlockSpec((tm, D_MODEL), row),
                  pl.BlockSpec((1, D_MODEL), const), pl.BlockSpec((1, D_MODEL), const)],
        out_specs=pl.BlockSpec((tm, D_MODEL), row),
        out_shape=jax.ShapeDtypeStruct((n, D_MODEL), F32),
        compiler_params=_cparams(("parallel",)),
        name="residual_ln2",
    )(x, y, g, b)


def _topk_axis0(v, k):
    r, n = v.shape
    iota = lax.broadcasted_iota(jnp.int32, (r, n), 0).astype(F32)
    slot = lax.broadcasted_iota(jnp.int32, (k, n), 0)

    def body(it, c):
        v, vals, idxs = c
        mx = jnp.max(v, axis=0, keepdims=True)
        idx = jnp.min(jnp.where(v == mx, iota, float(r)), axis=0, keepdims=True)
        v = jnp.where(iota == idx, -jnp.inf, v)
        return v, jnp.where(slot == it, mx, vals), jnp.where(slot == it, idx, idxs)

    _, vals, idxs = lax.fori_loop(0, k, body, (v, jnp.zeros((k, n), F32), jnp.zeros((k, n), F32)))
    return vals, idxs


def _pick_rows(table, pos, k):
    out = jnp.zeros(pos.shape, F32)
    for a in range(k):
        out = out + jnp.where(pos == float(a), table[a:a + 1], 0.0)
    return out


def _route_body(q_ref, keys_ref, ei_ref, ej_ref, g_ref):
    k = PEER_TOPK
    scores = _dot_nt(keys_ref[...], q_ref[...])
    v0, i0 = _topk_axis0(scores[:PEER_NKEYS], k)
    v1, i1 = _topk_axis0(scores[PEER_NKEYS:], k)
    counts = [k // (a + 1) for a in range(k)]
    starts = np.cumsum([0] + counts[:-1])
    pad = (-sum(counts)) % 8
    comb = jnp.concatenate([v0[a:a + 1] + v1[:counts[a]] for a in range(k)]
                           + [jnp.full((pad, v0.shape[1]), -jnp.inf, F32)], axis=0)
    sf, pos = _topk_axis0(comb, k)
    pa = jnp.zeros(pos.shape, F32)
    pb = pos
    for a in range(1, k):
        later = pos >= float(starts[a])
        pa = pa + jnp.where(later, 1.0, 0.0)
        pb = pb - jnp.where(later, float(counts[a - 1]), 0.0)
    ei_ref[...] = _pick_rows(i0, pa, k)
    ej_ref[...] = _pick_rows(i1, pb, k)
    e = jnp.exp(sf - jnp.max(sf, axis=0, keepdims=True))
    g_ref[...] = e / jnp.sum(e, axis=0, keepdims=True)


def _peer_route(q, keys):
    n = q.shape[0]
    tn = 512
    out = jax.ShapeDtypeStruct((PEER_HEADS, PEER_TOPK, n), F32)
    ospec = pl.BlockSpec((None, PEER_TOPK, tn), lambda i, h: (h, 0, i))
    return pl.pallas_call(
        _route_body,
        grid=(n // tn, PEER_HEADS),
        in_specs=[pl.BlockSpec((tn, PEER_DKEY), lambda i, h: (i, h)),
                  pl.BlockSpec((None, 2 * PEER_NKEYS, PEER_DKEY), lambda i, h: (h, 0, 0))],
        out_specs=[ospec, ospec, ospec],
        out_shape=[out, out, out],
        compiler_params=_cparams(("parallel", "arbitrary")),
        name="peer_route",
    )(q, keys)


def _gate_body(ei_ref, ej_ref, g_ref, o_ref):
    tb = ei_ref.shape[0]
    nk = PEER_NKEYS
    iota = lax.broadcasted_iota(jnp.int32, (tb, nk, ei_ref.shape[2]), 1).astype(F32)
    rows = jnp.where(iota == ei_ref[...], 1.0, 0.0).astype(BF16)
    cols = jnp.where(iota == ej_ref[...], g_ref[...], 0.0).astype(BF16)
    gm = lax.dot_general(rows, cols, (((2,), (2,)), ((0,), (0,))), preferred_element_type=F32)
    o_ref[...] = jnp.swapaxes(gm, 0, 1).astype(o_ref.dtype)


def _gate_matrix(ei, ej, g):
    n, _, slots = ei.shape
    tb = 128
    spec = pl.BlockSpec((tb, 1, slots), lambda i: (i, 0, 0))
    return pl.pallas_call(
        _gate_body,
        grid=(n // tb,),
        in_specs=[spec, spec, spec],
        out_specs=pl.BlockSpec((PEER_NKEYS, tb, PEER_NKEYS), lambda i: (0, i, 0)),
        out_shape=jax.ShapeDtypeStruct((PEER_NKEYS, n, PEER_NKEYS), BF16),
        compiler_params=_cparams(("parallel",)),
        name="peer_gate_matrix",
    )(ei, ej, g)


def _expert_body(x_ref, u_ref, g_ref, v_ref, o_ref):
    @pl.when(pl.program_id(1) == 0)
    def _():
        o_ref[...] = jnp.zeros(o_ref.shape, F32)

    hid = _dot_nt(x_ref[...], u_ref[...].astype(BF16))
    gate = jnp.concatenate([g_ref[i] for i in range(g_ref.shape[0])], axis=1)
    act = (jax.nn.gelu(hid) * gate.astype(F32)).astype(BF16)
    o_ref[...] += jnp.dot(act, v_ref[...].astype(BF16), preferred_element_type=F32)


def _peer_experts(xb, u, gm, v):
    n = xb.shape[0]
    ne = u.shape[0]
    tn, te = 1024, 512
    return pl.pallas_call(
        _expert_body,
        grid=(n // tn, ne // te),
        in_specs=[pl.BlockSpec((tn, D_MODEL), lambda i, j: (i, 0)),
                  pl.BlockSpec((te, D_MODEL), lambda i, j: (j, 0)),
                  pl.BlockSpec((te // PEER_NKEYS, tn, PEER_NKEYS), lambda i, j: (j, i, 0)),
                  pl.BlockSpec((te, D_MODEL), lambda i, j: (j, 0))],
        out_specs=pl.BlockSpec((tn, D_MODEL), lambda i, j: (i, 0)),
        out_shape=jax.ShapeDtypeStruct((n, D_MODEL), F32),
        compiler_params=_cparams(("parallel", "arbitrary")),
        name="peer_experts",
    )(xb, u, gm, v)


def _token_mixer(x, mem, w_in, diff_lambda, diff_subln, cmp_pe, cmp_w1, cmp_w2, w_mem_kv, w_branch, rel_bias):
    b, t, _ = x.shape
    n = b * t
    L = ATT_TILE
    xb = x.reshape(n, D_MODEL).astype(BF16)

    w_a, w_b = _w_prep(w_in.T)
    ha = _matmul(xb, w_a, BF16, 1024, 1664, "in_proj_a", w_transposed=True)
    hb = _matmul(xb, w_b, F32, 1024, 1664, "in_proj_b", w_transposed=True)
    ha3 = ha.reshape(b, t, ha.shape[1])

    tab1d = rel_bias[_rel_bucket(jnp.arange(t))].T * LOG2E
    n_far = min(t // L, REL_MAX_DIST // L + 2)
    bias_a = _bias_tiles(tab1d[:A_HEADS], t, -1, n_far + 1, t, 1)
    bias_b = tab1d[A_HEADS:]
    bias_sel = _bias_tiles(bias_b, t, -SEL_NQ, n_far + SEL_NQ, t, B_HPG)
    n_win = min(t // L, WINDOW // L + 1)
    bias_win = _bias_tiles(bias_b, t, -1, n_win + 1, WINDOW, B_HPG)

    def values_t(v):
        heads, dv = v.shape[2:]
        return v.reshape(b, t // L, L, heads, dv).transpose(0, 3, 1, 4, 2)

    o_a = _diff_attention(ha3, values_t(ha3[:, :, 3072:4096].reshape(b, t, A_HEADS, A_DV)), bias_a, diff_lambda,
                          jnp.broadcast_to(diff_subln[:, None], (A_DV, LANES)), b, t)

    glog_t = hb[:, 6144:6192].reshape(b, t, 3, B_GROUPS, B_HPG).transpose(2, 0, 3, 4, 1)
    nr = t // CMP_STRIDE
    w1 = cmp_w1.reshape(2, CMP_LEN, B_DH, CMP_HIDDEN)
    z1 = jnp.zeros_like(w1[0])
    w1_bd = jnp.concatenate([jnp.concatenate([w1[0], z1], axis=2), jnp.concatenate([z1, w1[1]], axis=2)],
                            axis=1).astype(BF16)
    z2 = jnp.zeros_like(cmp_w2[0])
    w2_bd = jnp.concatenate([jnp.concatenate([cmp_w2[0], z2], axis=1), jnp.concatenate([z2, cmp_w2[1]], axis=1)],
                            axis=0).astype(BF16)
    pe_bd = jnp.broadcast_to(jnp.concatenate([cmp_pe[0], cmp_pe[1]], axis=1)[:, None, :],
                             (CMP_LEN, 8, 2 * B_DH)).astype(BF16)
    kvc = _compress(ha3[:, :, 5120:5632].astype(F32), w1_bd, w2_bd, pe_bd)
    n_c = (t - CMP_LEN) // CMP_STRIDE + 1
    n_sel = t // SEL_BLOCK
    cidx = np.arange(nr)[:, None] * CMP_STRIDE + np.arange(CMP_LEN)[None, :]
    overlap = (cidx[:, :, None] // SEL_BLOCK == np.arange(n_sel)[None, None, :]).astype(np.float32).mean(axis=1)
    overlap[n_c:] = 0.0
    overlap_t = np.zeros((LANES, nr), np.float32)
    overlap_t[:n_sel] = overlap.T
    o_c, sel_mask = _nsa_compressed(ha3, kvc, jnp.asarray(overlap_t, BF16), glog_t[0], b, t, n_c,
                                    min(SEL_TOPN, n_sel))
    v_sel = ha3[:, :, 5632:6144].reshape(b, t, B_GROUPS, 2, B_DH)[:, :, :, 1]
    v_win = ha3[:, :, 6144:6656].reshape(b, t, B_GROUPS, 2, B_DH)[:, :, :, 1]
    assert n_sel <= B_DH
    hot = np.zeros((t, LANES), np.float32)
    hot[np.arange(t), B_DH + np.arange(t) // SEL_BLOCK] = 1.0
    o_s = _nsa_branch(ha3, values_t(v_sel), bias_sel, glog_t[1], sel_mask, jnp.asarray(hot, BF16), b, t, "sel",
                      SEL_NQ)
    o_w = _nsa_branch(ha3, values_t(v_win), bias_win, glog_t[2], None, None, b, t, "win", 1)

    n_mem = mem.shape[1]
    memkv = _matmul(mem.reshape(b * n_mem, D_MODEL).astype(BF16), w_mem_kv.astype(BF16), BF16,
                    b * n_mem, 1024, "mem_kv_proj")
    o_m = _memory_attention(ha3, memkv, b, t, n_mem)

    flat = lambda o: o.reshape(n, MIX_W)
    return flat(o_a), flat(o_c), flat(o_s), flat(o_w), flat(o_m), hb


def _peer(x1, x1b, peer_wq, peer_keys, peer_u, peer_v):
    n = x1.shape[0]
    q = _matmul(x1b, peer_wq.astype(BF16), BF16, 1024, 1024, "peer_query")
    zk = jnp.zeros((PEER_HEADS, PEER_NKEYS, PEER_DKEY // 2), F32)
    keys_bd = jnp.concatenate([jnp.concatenate([peer_keys[:, 0], zk], axis=2),
                               jnp.concatenate([zk, peer_keys[:, 1]], axis=2)], axis=1).astype(BF16)
    ei, ej, gate = _peer_route(q, keys_bd)
    slots = lambda a: a.reshape(PEER_HEADS * PEER_TOPK, n).T.reshape(n, 1, PEER_HEADS * PEER_TOPK)
    gm = _gate_matrix(slots(ei), slots(ej), slots(gate))
    return _peer_experts(x1b, peer_u, gm, peer_v)


def kernel(x, mem, w_in, diff_lambda, diff_subln, cmp_pe, cmp_w1, cmp_w2, w_mem_kv, w_branch, w_out, ln1_g, ln1_b,
           peer_wq, peer_keys, peer_u, peer_v, ln2_g, ln2_b, rel_bias):
    b, t, _ = x.shape
    n = b * t
    for l in range(DEPTH):
        o_a, o_c, o_s, o_w, o_m, hb = _token_mixer(x, mem, w_in[l], diff_lambda[l], diff_subln[l], cmp_pe[l],
                                                   cmp_w1[l], cmp_w2[l], w_mem_kv[l], w_branch[l], rel_bias)
        mixed = _merge(o_a, o_c, o_s, o_w, o_m, hb, w_branch[l].astype(BF16))
        x1, x1b = _out_proj_ln(mixed, w_out[l].astype(BF16), x.reshape(n, D_MODEL),
                               ln1_g[l].reshape(1, D_MODEL), ln1_b[l].reshape(1, D_MODEL))
        y = _peer(x1, x1b, peer_wq[l], peer_keys[l], peer_u[l], peer_v[l])
        x = _residual_ln(x1, y, ln2_g[l].reshape(1, D_MODEL), ln2_b[l].reshape(1, D_MODEL)).reshape(b, t, D_MODEL)
    return x
```

```python
import functools
import math

import numpy as np
import jax
import jax.numpy as jnp
from jax import lax
from jax.experimental import pallas as pl
from jax.experimental.pallas import tpu as pltpu

F32 = jnp.float32
BF16 = jnp.bfloat16

D_MODEL = 2048
A_HEADS, A_DQK, A_DV = 8, 64, 128
B_HEADS, B_GROUPS, B_HPG, B_DH = 16, 4, 4, 64
CMP_LEN, CMP_STRIDE, CMP_HIDDEN = 32, 16, 256
SEL_BLOCK, SEL_TOPN, WINDOW = 64, 16, 512
M_HEADS, M_DH = 4, 256
REL_BUCKETS, REL_MAX_DIST = 32, 1024
PEER_HEADS, PEER_NKEYS, PEER_DKEY, PEER_TOPK = 8, 128, 256, 16
MIX_W = 1024
LN_EPS = 1e-5
FORCE = 1e9
DEPTH = 1
ALPHA = (2 * DEPTH) ** 0.25
LAMBDA_INIT = 0.8 - 0.6 * math.exp(-0.3 * 0)
LOG2E = math.log2(math.e)

NEG = -1e30
ATT_TILE = 256
LANES = 128
ONES_ROWS = 16
N_MERGE = 3
SEL_NQ = 1
VMEM_LIMIT = 56 * 1024 * 1024


def _cparams(sem):
    return pltpu.CompilerParams(dimension_semantics=sem, vmem_limit_bytes=VMEM_LIMIT)


def _dot_nt(a, b):
    return lax.dot_general(a, b, (((1,), (1,)), ((), ())), preferred_element_type=F32)


def _mm_body(x_ref, w_ref, o_ref):
    o_ref[...] = jnp.dot(x_ref[...], w_ref[...], preferred_element_type=F32).astype(o_ref.dtype)


def _mm_nt_body(x_ref, w_ref, o_ref):
    o_ref[...] = _dot_nt(x_ref[...], w_ref[...]).astype(o_ref.dtype)


def _matmul(x, w, out_dtype, tm, tn, name, w_transposed=False):
    m, k = x.shape
    n = w.shape[0] if w_transposed else w.shape[1]
    w_spec = pl.BlockSpec((tn, k), lambda i, j: (j, 0)) if w_transposed else pl.BlockSpec((k, tn), lambda i, j: (0, j))
    return pl.pallas_call(
        _mm_nt_body if w_transposed else _mm_body,
        grid=(m // tm, n // tn),
        in_specs=[pl.BlockSpec((tm, k), lambda i, j: (i, 0)), w_spec],
        out_specs=pl.BlockSpec((tm, tn), lambda i, j: (i, j)),
        out_shape=jax.ShapeDtypeStruct((m, n), out_dtype),
        compiler_params=_cparams(("parallel", "arbitrary")),
        name=name,
    )(x, w)


W_A_COLS = 6656
W_G_COLS = 128


def _w_prep_body(w_ref, wa_ref, wm_ref, wg_ref):
    def put(dst_ref, d0, s0, n, c=1.0):
        val = w_ref[s0:s0 + n, :]
        dst_ref[d0:d0 + n, :] = (val * c if c != 1.0 else val).astype(dst_ref.dtype)

    put(wa_ref, 0, 5680, 1024, M_DH ** -0.5 * LOG2E)
    put(wa_ref, 1024, 0, 1024, A_DQK ** -0.5 * LOG2E)
    put(wa_ref, 2048, 1024, 2048)
    put(wa_ref, 4096, 3072, 1024, B_DH ** -0.5 * LOG2E)
    for br in range(3):
        for g in range(B_GROUPS):
            src = 4096 + br * 2 * B_GROUPS * B_DH + g * B_DH
            dst = 5120 + (br * B_GROUPS + g) * 2 * B_DH
            put(wa_ref, dst, src, B_DH)
            put(wa_ref, dst + B_DH, src + B_GROUPS * B_DH, B_DH)
    put(wm_ref, 0, 6704, N_MERGE * D_MODEL)
    put(wg_ref, 0, 5632, 48)
    wg_ref[48:, :] = jnp.zeros((W_G_COLS - 48, wg_ref.shape[1]), wg_ref.dtype)


def _w_prep(w_in_t):
    c, k = w_in_t.shape
    tk = 256
    col = lambda i: (0, i)
    return pl.pallas_call(
        _w_prep_body,
        grid=(k // tk,),
        in_specs=[pl.BlockSpec((c, tk), col)],
        out_specs=[pl.BlockSpec((W_A_COLS, tk), col), pl.BlockSpec((N_MERGE * D_MODEL, tk), col),
                   pl.BlockSpec((W_G_COLS, tk), col)],
        out_shape=[jax.ShapeDtypeStruct((W_A_COLS, k), BF16), jax.ShapeDtypeStruct((N_MERGE * D_MODEL, k), BF16),
                   jax.ShapeDtypeStruct((W_G_COLS, k), BF16)],
        compiler_params=_cparams(("parallel",)),
        name="w_in_regroup",
    )(w_in_t)


def _rel_bucket(dist):
    n = jnp.maximum(dist, 0)
    max_exact = REL_BUCKETS // 2
    nf = jnp.maximum(n, 1).astype(jnp.float32)
    large = max_exact + (jnp.log(nf / max_exact) / math.log(REL_MAX_DIST / max_exact)
                         * (REL_BUCKETS - max_exact)).astype(jnp.int32)
    large = jnp.minimum(large, REL_BUCKETS - 1)
    return jnp.where(n < max_exact, n, large)


def _bias_tiles(tab1d, t, first, n_tiles, max_dist, hpr):
    L = ATT_TILE
    m = np.arange(2 * L)
    off = np.where(m <= L, m, m - 2 * L)
    d = (first + np.arange(n_tiles))[:, None] * L + off[None, :]
    ok = (d >= 0) & (d < max_dist)
    h = tab1d.shape[0]
    rp = jnp.where(ok[:, None], tab1d.T[np.clip(d, 0, t - 1)].transpose(0, 2, 1), NEG)
    return pl.pallas_call(
        functools.partial(_toeplitz_body, hpr=hpr),
        grid=(n_tiles,),
        in_specs=[pl.BlockSpec((None, h, 1, 2 * L), lambda c: (c, 0, 0, 0))],
        out_specs=pl.BlockSpec((None, h // hpr, L, hpr * L), lambda c: (c, 0, 0, 0)),
        out_shape=jax.ShapeDtypeStruct((n_tiles, h // hpr, L, hpr * L), F32),
        compiler_params=_cparams(("parallel",)),
        name="bias_tiles",
    )(rp.reshape(n_tiles, h, 1, 2 * L))


def _toeplitz_body(rp_ref, o_ref, *, hpr):
    L = o_ref.shape[1]
    for hd in range(rp_ref.shape[0]):
        rows = jnp.broadcast_to(rp_ref[hd], (L, 2 * L))
        tile = pltpu.roll(rows, 0, 1, stride=1, stride_axis=0)[:, :L]
        o_ref[hd // hpr, :, (hd % hpr) * L:(hd % hpr + 1) * L] = tile


def _flash_loop(lo, hi, scores, values, m_ref, acc_ref, s_ref, mx_ref):
    def ahead(kt, slot):
        s = scores(kt)
        s_ref[slot] = s
        mx_ref[slot] = jnp.broadcast_to(jnp.max(s, axis=0, keepdims=True), mx_ref.shape[1:])

    def finish(kt, slot):
        m_prev = m_ref[...]
        m_new = jnp.maximum(m_prev, mx_ref[slot])
        alpha = jnp.exp2(m_prev - m_new)
        p = jnp.exp2(s_ref[slot] - m_new[0:1])
        acc_ref[...] = alpha[0:1] * acc_ref[...] + jnp.dot(values(kt), p.astype(BF16), preferred_element_type=F32)
        m_ref[...] = m_new

    def body(i, carry):
        kt = lo + 2 * i
        ahead(kt + 1, 1)
        finish(kt, 0)
        ahead(kt + 2, 0)
        finish(kt + 1, 1)
        return carry

    ahead(lo, 0)
    lax.fori_loop(0, (hi - lo + 1) // 2, body, 0)


def _diff_body(q_ref, k_ref, vt_ref, bias_ref, lam_ref, g_ref, o_ref, m_sc, acc_sc, s_sc, mx_sc, *, n_bt):
    L = ATT_TILE
    tq = 2 * L
    last = vt_ref.shape[0] - 1
    qi = pl.program_id(2)
    m_sc[...] = jnp.full(m_sc.shape, NEG, F32)
    acc_sc[...] = jnp.zeros(acc_sc.shape, F32)
    qt = q_ref[...].astype(F32).T.astype(BF16)
    zero = jnp.zeros((A_DQK, tq), BF16)
    q_cat = jnp.concatenate([jnp.concatenate([qt[:A_DQK], zero], axis=0),
                             jnp.concatenate([zero, qt[A_DQK:]], axis=0)], axis=1)

    def scores(kt):
        off = pl.multiple_of(jnp.minimum(kt, last) * L, L)
        k = k_ref[pl.ds(off, L), :]
        d0 = 2 * qi - kt
        bias = jnp.concatenate([bias_ref[jnp.clip(d0 + 1, 0, n_bt - 1)],
                                bias_ref[jnp.clip(d0 + 2, 0, n_bt - 1)]], axis=1)
        s = jnp.dot(k, q_cat, preferred_element_type=F32)
        return jnp.concatenate([s[:, :tq] + bias, s[:, tq:] + bias], axis=1)

    ones = jnp.ones((ONES_ROWS, L), BF16)

    def values(kt):
        return jnp.concatenate([vt_ref[jnp.minimum(kt, last)], ones], axis=0)

    _flash_loop(0, 2 * qi + 2, scores, values, m_sc, acc_sc, s_sc, mx_sc)

    lp = lam_ref[...]
    lam = (jnp.exp(jnp.sum(lp[0:1] * lp[1:2], axis=1, keepdims=True))
           - jnp.exp(jnp.sum(lp[2:3] * lp[3:4], axis=1, keepdims=True)) + LAMBDA_INIT)
    acc = acc_sc[...]
    o0 = acc[:A_DV, :tq] / jnp.maximum(acc[A_DV:A_DV + 1, :tq], 1e-30)
    o1 = acc[:A_DV, tq:] / jnp.maximum(acc[A_DV:A_DV + 1, tq:], 1e-30)
    o = o0 - lam * o1
    g = jnp.concatenate([g_ref[...]] * (tq // LANES), axis=1)
    o = o * lax.rsqrt(jnp.mean(o * o, axis=0, keepdims=True) + LN_EPS) * g
    o_ref[...] = (o * (1.0 - LAMBDA_INIT)).T.astype(o_ref.dtype)


def _diff_attention(ha, v_t, bias, lam_params, subln, b, t):
    L = ATT_TILE
    tq = 2 * L
    n_bt = bias.shape[0]
    return pl.pallas_call(
        functools.partial(_diff_body, n_bt=n_bt),
        grid=(b, A_HEADS, t // tq),
        in_specs=[
            pl.BlockSpec((None, tq, 128), lambda bi, h, qi: (bi, qi, 8 + h)),
            pl.BlockSpec((None, t, 128), lambda bi, h, qi: (bi, 0, 16 + h)),
            pl.BlockSpec((None, None, t // L, A_DV, L), lambda bi, h, qi: (bi, h, 0, 0, 0)),
            pl.BlockSpec((n_bt, None, L, L), lambda bi, h, qi: (0, h, 0, 0)),
            pl.BlockSpec((4, A_DQK), lambda bi, h, qi: (0, 0)),
            pl.BlockSpec((A_DV, LANES), lambda bi, h, qi: (0, 0)),
        ],
        out_specs=pl.BlockSpec((None, tq, 128), lambda bi, h, qi: (bi, qi, h)),
        out_shape=jax.ShapeDtypeStruct((b, t, MIX_W), BF16),
        scratch_shapes=[pltpu.VMEM((8, 2 * tq), F32), pltpu.VMEM((A_DV + ONES_ROWS, 2 * tq), F32),
                        pltpu.VMEM((2, L, 2 * tq), F32), pltpu.VMEM((2, 8, 2 * tq), F32)],
        compiler_params=_cparams(("parallel", "parallel", "arbitrary")),
        name="diff_attention",
    )(ha, ha, v_t, bias, lam_params, subln)


def _compress_body(kv_ref, w1_ref, w2_ref, pe_ref, o_ref):
    nr = o_ref.shape[0]
    first = jnp.zeros((nr, 2 * CMP_HIDDEN), F32)
    second = jnp.zeros((nr, 2 * CMP_HIDDEN), F32)
    for l in range(CMP_STRIDE):
        rows = kv_ref[pl.ds(l, nr, stride=CMP_STRIDE), :].astype(BF16)
        first = first + jnp.dot(rows, w1_ref[l], preferred_element_type=F32)
        second = second + jnp.dot(rows, w1_ref[CMP_STRIDE + l], preferred_element_type=F32)
    pw = jnp.zeros((8, 2 * CMP_HIDDEN), F32)
    for l in range(CMP_LEN):
        pw = pw + jnp.dot(pe_ref[l], w1_ref[l], preferred_element_type=F32)
    second = jnp.concatenate([second[1:], second[:1]], axis=0)
    hdn = jax.nn.gelu(first + second + pw[0:1])
    o_ref[...] = jnp.dot(hdn.astype(BF16), w2_ref[...], preferred_element_type=F32)


def _compress(kv, w1_bd, w2_bd, pe_bd):
    b, t, _ = kv.shape
    nr = t // CMP_STRIDE
    return pl.pallas_call(
        _compress_body,
        grid=(b, B_GROUPS),
        in_specs=[
            pl.BlockSpec((None, t, LANES), lambda bi, gi: (bi, 0, gi)),
            pl.BlockSpec(w1_bd.shape, lambda bi, gi: (0, 0, 0)),
            pl.BlockSpec(w2_bd.shape, lambda bi, gi: (0, 0)),
            pl.BlockSpec(pe_bd.shape, lambda bi, gi: (0, 0, 0)),
        ],
        out_specs=pl.BlockSpec((None, None, nr, 2 * B_DH), lambda bi, gi: (bi, gi, 0, 0)),
        out_shape=jax.ShapeDtypeStruct((b, B_GROUPS, nr, 2 * B_DH), F32),
        compiler_params=_cparams(("parallel", "parallel")),
        name="nsa_compress",
    )(kv, w1_bd, w2_bd, pe_bd)


def _topk_mask_axis0(v, k):
    r, n = v.shape
    iota = lax.broadcasted_iota(jnp.int32, (r, n), 0).astype(F32)

    def body(_, c):
        v, sel = c
        mx = jnp.max(v, axis=0, keepdims=True)
        idx = jnp.min(jnp.where(v == mx, iota, float(r)), axis=0, keepdims=True)
        hit = iota == idx
        return jnp.where(hit, -jnp.inf, v), jnp.where(hit, 1.0, sel)

    _, sel = lax.fori_loop(0, k, body, (v, jnp.zeros((r, n), F32)))
    return sel


def _cmp_body(q_ref, kvc_ref, ov_ref, gl_ref, o_ref, mask_ref, *, n_c, n_top):
    L = ATT_TILE
    qi = pl.program_id(2)
    ncp = kvc_ref.shape[0]
    n_sel = mask_ref.shape[0]
    qt = q_ref[...].astype(F32).T.astype(BF16)
    qh = jnp.concatenate([qt[h * B_DH:(h + 1) * B_DH] for h in range(B_HPG)], axis=1)
    kvc = kvc_ref[...]
    s = jnp.dot(kvc[:, :B_DH].astype(BF16), qh, preferred_element_type=F32)
    tcol = qi * L + lax.broadcasted_iota(jnp.int32, (1, L), 1)
    crow = lax.broadcasted_iota(jnp.int32, (ncp, 1), 0)
    seen = jnp.where(crow < n_c, crow * CMP_STRIDE + (CMP_LEN - 1), jnp.int32(2 ** 30)) <= tcol
    valid = jnp.concatenate([seen] * B_HPG, axis=1)
    s = jnp.where(valid, s, NEG)
    mx = jnp.max(s, axis=0, keepdims=True)
    e = jnp.where(valid, jnp.exp2(s - mx), 0.0)
    p = e / jnp.maximum(jnp.sum(e, axis=0, keepdims=True), 1e-30)
    o = jnp.dot(kvc.T[B_DH:].astype(BF16), p.astype(BF16), preferred_element_type=F32)
    gate = jax.nn.sigmoid(gl_ref[...])
    o = jnp.concatenate([o[:, h * L:(h + 1) * L] * gate[h:h + 1] for h in range(B_HPG)], axis=0)
    o_ref[...] = o.T

    psum = p[:, :L] + p[:, L:2 * L] + p[:, 2 * L:3 * L] + p[:, 3 * L:]
    ov_t = ov_ref[...]
    imp = jnp.zeros((ov_t.shape[0], L), F32)
    rem = psum
    for _ in range(3):
        part = rem.astype(BF16)
        imp = imp + jnp.dot(ov_t, part, preferred_element_type=F32)
        rem = rem - part.astype(F32)
    imp = imp[:n_sel]
    blk = lax.broadcasted_iota(jnp.int32, (n_sel, 1), 0)
    cur = jnp.right_shift(tcol, int(math.log2(SEL_BLOCK)))
    imp = jnp.where(blk * SEL_BLOCK > tcol, -FORCE, imp)
    imp = jnp.where(blk == 0, FORCE, imp)
    imp = jnp.where(blk == cur, FORCE, imp)
    imp = jnp.where(blk == cur - 1, FORCE, imp)
    mask_ref[...] = jnp.where(_topk_mask_axis0(imp, n_top) > 0.0, 0.0, NEG).astype(mask_ref.dtype)


def _nsa_compressed(ha, kvc, overlap_t, glog_t, b, t, n_c, n_top):
    L = ATT_TILE
    ncp = kvc.shape[2]
    n_sel = t // SEL_BLOCK
    return pl.pallas_call(
        functools.partial(_cmp_body, n_c=n_c, n_top=n_top),
        grid=(b, B_GROUPS, t // L),
        in_specs=[
            pl.BlockSpec((None, L, 256), lambda bi, g, qi: (bi, qi, 16 + g)),
            pl.BlockSpec((None, None, ncp, 2 * B_DH), lambda bi, g, qi: (bi, g, 0, 0)),
            pl.BlockSpec(overlap_t.shape, lambda bi, g, qi: (0, 0)),
            pl.BlockSpec((None, None, B_HPG, L), lambda bi, g, qi: (bi, g, 0, qi)),
        ],
        out_specs=[
            pl.BlockSpec((None, L, 256), lambda bi, g, qi: (bi, qi, g)),
            pl.BlockSpec((None, None, n_sel, L), lambda bi, g, qi: (bi, g, 0, qi)),
        ],
        out_shape=[jax.ShapeDtypeStruct((b, t, MIX_W), F32),
                   jax.ShapeDtypeStruct((b, B_GROUPS, n_sel, t), BF16)],
        compiler_params=_cparams(("parallel", "parallel", "arbitrary")),
        name="nsa_compressed_select",
    )(ha, kvc, overlap_t, glog_t)


def _nsa_body(*refs, mode, n_bt):
    if mode == "sel":
        q_ref, kv_ref, vt_ref, bias_ref, gl_ref, prev_ref, mask_ref, hot_ref, o_ref, m_sc, acc_sc, s_sc, mx_sc = refs
    else:
        q_ref, kv_ref, vt_ref, bias_ref, gl_ref, prev_ref, o_ref, m_sc, acc_sc, s_sc, mx_sc = refs
    L = ATT_TILE
    nq = q_ref.shape[0] // L
    qi = pl.program_id(2)
    m_sc[...] = jnp.full(m_sc.shape, NEG, F32)
    acc_sc[...] = jnp.zeros(acc_sc.shape, F32)
    qt = q_ref[...].astype(F32).T.astype(BF16)
    if mode == "sel":
        n_sel = mask_ref.shape[0]
        qm = mask_ref[...]
        if n_sel < B_DH:
            qm = jnp.concatenate([qm, jnp.zeros((B_DH - n_sel, nq * L), BF16)], axis=0)
    else:
        qm = jnp.zeros((B_DH, nq * L), BF16)
    q_aug_t = jnp.concatenate(
        [jnp.concatenate([qt[h * B_DH:(h + 1) * B_DH, s * L:(s + 1) * L], qm[:, s * L:(s + 1) * L]], axis=0)
         for s in range(nq) for h in range(B_HPG)], axis=1)

    last = vt_ref.shape[0] - 1
    ones = jnp.ones((ONES_ROWS, L), BF16)
    k_lanes = lax.broadcasted_iota(jnp.int32, (L, LANES), 1) < B_DH

    def scores(kt):
        off = pl.multiple_of(jnp.minimum(kt, last) * L, L)
        idx = nq * qi - kt + nq
        bias = [bias_ref[jnp.clip(idx + s, 0, n_bt - 1)] for s in range(nq)]
        bias = bias[0] if nq == 1 else jnp.concatenate(bias, axis=1)
        k_aug = kv_ref[pl.ds(off, L), :]
        if mode == "sel":
            k_aug = jnp.where(k_lanes, k_aug, hot_ref[pl.ds(off, L), :])
        return jnp.dot(k_aug, q_aug_t, preferred_element_type=F32) + bias

    def values(kt):
        return jnp.concatenate([vt_ref[jnp.minimum(kt, last)], ones], axis=0)

    lo = 0 if mode == "sel" else jnp.maximum(nq * qi - (n_bt - nq - 1), 0)
    _flash_loop(lo, nq * qi + nq, scores, values, m_sc, acc_sc, s_sc, mx_sc)
    acc = acc_sc[...]
    o = acc[:B_DH] / jnp.maximum(acc[B_DH:B_DH + 1], 1e-30)
    gate = jax.nn.sigmoid(gl_ref[...])
    heads = []
    for h in range(B_HPG):
        cols = [o[:, (s * B_HPG + h) * L:(s * B_HPG + h + 1) * L] for s in range(nq)]
        heads.append((cols[0] if nq == 1 else jnp.concatenate(cols, axis=1)) * gate[h:h + 1])
    o_ref[...] = (prev_ref[...] + jnp.concatenate(heads, axis=0).T).astype(o_ref.dtype)


def _nsa_branch(ha, v_t, bias, glog_t, prev, mask_t, hot, b, t, mode, nq, out_dtype):
    L = ATT_TILE
    tq = nq * L
    n_bt = bias.shape[0]
    branch = 1 if mode == "sel" else 2
    in_specs = [
        pl.BlockSpec((None, tq, 256), lambda bi, g, qi: (bi, qi, 16 + g)),
        pl.BlockSpec((None, t, LANES), lambda bi, g, qi: (bi, 0, 40 + 4 * branch + g)),
        pl.BlockSpec((None, None, t // L, B_DH, L), lambda bi, g, qi: (bi, g, 0, 0, 0)),
        pl.BlockSpec((n_bt, None, L, B_HPG * L), lambda bi, g, qi: (0, g, 0, 0)),
        pl.BlockSpec((None, None, B_HPG, tq), lambda bi, g, qi: (bi, g, 0, qi)),
        pl.BlockSpec((None, tq, 256), lambda bi, g, qi: (bi, qi, g)),
    ]
    args = [ha, ha, v_t, bias, glog_t, prev]
    if mode == "sel":
        n_sel = mask_t.shape[2]
        in_specs.append(pl.BlockSpec((None, None, n_sel, tq), lambda bi, g, qi: (bi, g, 0, qi)))
        in_specs.append(pl.BlockSpec((t, LANES), lambda bi, g, qi: (0, 0)))
        args += [mask_t, hot]
    return pl.pallas_call(
        functools.partial(_nsa_body, mode=mode, n_bt=n_bt),
        grid=(b, B_GROUPS, t // tq),
        in_specs=in_specs,
        out_specs=pl.BlockSpec((None, tq, 256), lambda bi, g, qi: (bi, qi, g)),
        out_shape=jax.ShapeDtypeStruct((b, t, MIX_W), out_dtype),
        scratch_shapes=[pltpu.VMEM((8, B_HPG * tq), F32), pltpu.VMEM((B_DH + ONES_ROWS, B_HPG * tq), F32),
                        pltpu.VMEM((2, L, B_HPG * tq), F32), pltpu.VMEM((2, 8, B_HPG * tq), F32)],
        compiler_params=_cparams(("parallel", "parallel", "arbitrary")),
        name="nsa_" + mode,
    )(*args)


def _mem_body(q_ref, k_ref, v_ref, o_ref):
    q = q_ref[...]
    outs = []
    for h in range(M_HEADS):
        sl = slice(h * M_DH, (h + 1) * M_DH)
        s = _dot_nt(q[:, sl], k_ref[:, sl])
        e = jnp.exp2(s - jnp.max(s, axis=1, keepdims=True))
        p = e / jnp.sum(e, axis=1, keepdims=True)
        outs.append(jnp.dot(p.astype(BF16), v_ref[:, sl], preferred_element_type=F32))
    o_ref[...] = jnp.concatenate(outs, axis=1).astype(o_ref.dtype)


def _memory_attention(ha, memkv, b, t, n_mem):
    tq = 512
    w = M_HEADS * M_DH
    return pl.pallas_call(
        _mem_body,
        grid=(b, t // tq),
        in_specs=[
            pl.BlockSpec((None, tq, w), lambda bi, qi: (bi, qi, 0)),
            pl.BlockSpec((n_mem, w), lambda bi, qi: (bi, 0)),
            pl.BlockSpec((n_mem, w), lambda bi, qi: (bi, 1)),
        ],
        out_specs=pl.BlockSpec((None, tq, w), lambda bi, qi: (bi, qi, 0)),
        out_shape=jax.ShapeDtypeStruct((b, t, w), BF16),
        compiler_params=_cparams(("parallel", "arbitrary")),
        name="memory_attention",
    )(ha, memkv, memkv)


def _merge_body(x_ref, wg_ref, oa_ref, ob_ref, om_ref, wb_ref, o_ref, acc_sc):
    n = pl.program_id(1)

    @pl.when(n == 0)
    def _():
        acc_sc[...] = jnp.zeros(acc_sc.shape, F32)

    gate = jax.nn.sigmoid(_dot_nt(x_ref[...], wg_ref[...]))
    for idx, ref in enumerate((oa_ref, ob_ref, om_ref)):
        @pl.when(n == idx)
        def _(ref=ref):
            acc_sc[...] += gate * jnp.dot(ref[...], wb_ref[...], preferred_element_type=F32)

    @pl.when(n == N_MERGE - 1)
    def _():
        o_ref[...] = acc_sc[...].astype(o_ref.dtype)


def _merge(xb, w_gate_t, o_a, o_b, o_m, w_branch):
    n = xb.shape[0]
    tm = 512
    row = lambda i, j: (i, 0)
    return pl.pallas_call(
        _merge_body,
        grid=(n // tm, N_MERGE),
        in_specs=[pl.BlockSpec((tm, D_MODEL), row),
                  pl.BlockSpec((D_MODEL, D_MODEL), lambda i, j: (j, 0))]
                 + [pl.BlockSpec((tm, MIX_W), row)] * 3
                 + [pl.BlockSpec((None, MIX_W, D_MODEL), lambda i, j: (j, 0, 0))],
        out_specs=pl.BlockSpec((tm, D_MODEL), row),
        out_shape=jax.ShapeDtypeStruct((n, D_MODEL), BF16),
        scratch_shapes=[pltpu.VMEM((tm, D_MODEL), F32)],
        compiler_params=_cparams(("parallel", "arbitrary")),
        name="branch_merge",
    )(xb, w_gate_t, o_a, o_b, o_m, w_branch)


def _layer_norm(z, g, b):
    mu = jnp.mean(z, axis=1, keepdims=True)
    zc = z - mu
    var = jnp.mean(zc * zc, axis=1, keepdims=True)
    return zc * lax.rsqrt(var + LN_EPS) * g + b


def _out_ln_body(y_ref, w_ref, x_ref, g_ref, b_ref, o_ref, ob_ref):
    y = jnp.dot(y_ref[...], w_ref[...], preferred_element_type=F32)
    o = _layer_norm(ALPHA * x_ref[...] + y, g_ref[...], b_ref[...])
    o_ref[...] = o
    ob_ref[...] = o.astype(BF16)


def _out_proj_ln(mixed, w_out, x, g, b):
    n = x.shape[0]
    tm = 512
    row = lambda i: (i, 0)
    const = lambda i: (0, 0)
    return pl.pallas_call(
        _out_ln_body,
        grid=(n // tm,),
        in_specs=[pl.BlockSpec((tm, D_MODEL), row), pl.BlockSpec((D_MODEL, D_MODEL), const),
                  pl.BlockSpec((tm, D_MODEL), row), pl.BlockSpec((1, D_MODEL), const),
                  pl.BlockSpec((1, D_MODEL), const)],
        out_specs=[pl.BlockSpec((tm, D_MODEL), row), pl.BlockSpec((tm, D_MODEL), row)],
        out_shape=[jax.ShapeDtypeStruct((n, D_MODEL), F32), jax.ShapeDtypeStruct((n, D_MODEL), BF16)],
        compiler_params=_cparams(("parallel",)),
        name="out_proj_ln1",
    )(mixed, w_out, x, g, b)


def _res_ln_body(x_ref, y_ref, g_ref, b_ref, o_ref):
    o_ref[...] = _layer_norm(ALPHA * x_ref[...] + y_ref[...], g_ref[...], b_ref[...])


def _residual_ln(x, y, g, b):
    n = x.shape[0]
    tm = 512
    row = lambda i: (i, 0)
    const = lambda i: (0, 0)
    return pl.pallas_call(
        _res_ln_body,
        grid=(n // tm,),
        in_specs=[pl.BlockSpec((tm, D_MODEL), row), pl.BlockSpec((tm, D_MODEL), row),
                  pl.BlockSpec((1, D_MODEL), const), pl.BlockSpec((1, D_MODEL), const)],
        out_specs=pl.BlockSpec((tm, D_MODEL), row),
        out_shape=jax.ShapeDtypeStruct((n, D_MODEL), F32),
        compiler_params=_cparams(("parallel",)),
        name="residual_ln2",
    )(x, y, g, b)


def _topk_axis0(v, k):
    r, n = v.shape
    iota = lax.broadcasted_iota(jnp.int32, (r, n), 0).astype(F32)
    slot = lax.broadcasted_iota(jnp.int32, (k, n), 0)

    def body(it, c):
        v, vals, idxs = c
        mx = jnp.max(v, axis=0, keepdims=True)
        idx = jnp.min(jnp.where(v == mx, iota, float(r)), axis=0, keepdims=True)
        v = jnp.where(iota == idx, -jnp.inf, v)
        return v, jnp.where(slot == it, mx, vals), jnp.where(slot == it, idx, idxs)

    _, vals, idxs = lax.fori_loop(0, k, body, (v, jnp.zeros((k, n), F32), jnp.zeros((k, n), F32)))
    return vals, idxs


def _pick_rows(table, pos, k):
    out = jnp.zeros(pos.shape, F32)
    for a in range(k):
        out = out + jnp.where(pos == float(a), table[a:a + 1], 0.0)
    return out


def _route_body(q_ref, keys_ref, ei_ref, ej_ref, g_ref):
    k = PEER_TOPK
    scores = _dot_nt(keys_ref[...], q_ref[...])
    v0, i0 = _topk_axis0(scores[:PEER_NKEYS], k)
    v1, i1 = _topk_axis0(scores[PEER_NKEYS:], k)
    counts = [k // (a + 1) for a in range(k)]
    starts = np.cumsum([0] + counts[:-1])
    pad = (-sum(counts)) % 8
    comb = jnp.concatenate([v0[a:a + 1] + v1[:counts[a]] for a in range(k)]
                           + [jnp.full((pad, v0.shape[1]), -jnp.inf, F32)], axis=0)
    sf, pos = _topk_axis0(comb, k)
    pa = jnp.zeros(pos.shape, F32)
    pb = pos
    for a in range(1, k):
        later = pos >= float(starts[a])
        pa = pa + jnp.where(later, 1.0, 0.0)
        pb = pb - jnp.where(later, float(counts[a - 1]), 0.0)
    ei_ref[...] = _pick_rows(i0, pa, k)
    ej_ref[...] = _pick_rows(i1, pb, k)
    e = jnp.exp(sf - jnp.max(sf, axis=0, keepdims=True))
    g_ref[...] = e / jnp.sum(e, axis=0, keepdims=True)


def _peer_route(q, keys):
    n = q.shape[0]
    tn = 512
    out = jax.ShapeDtypeStruct((PEER_HEADS, PEER_TOPK, n), F32)
    ospec = pl.BlockSpec((None, PEER_TOPK, tn), lambda i, h: (h, 0, i))
    return pl.pallas_call(
        _route_body,
        grid=(n // tn, PEER_HEADS),
        in_specs=[pl.BlockSpec((tn, PEER_DKEY), lambda i, h: (i, h)),
                  pl.BlockSpec((None, 2 * PEER_NKEYS, PEER_DKEY), lambda i, h: (h, 0, 0))],
        out_specs=[ospec, ospec, ospec],
        out_shape=[out, out, out],
        compiler_params=_cparams(("parallel", "arbitrary")),
        name="peer_route",
    )(q, keys)


def _gate_body(ei_ref, ej_ref, g_ref, o_ref):
    tb = ei_ref.shape[0]
    nk = PEER_NKEYS
    iota = lax.broadcasted_iota(jnp.int32, (tb, nk, ei_ref.shape[2]), 1).astype(F32)
    rows = jnp.where(iota == ei_ref[...], 1.0, 0.0).astype(BF16)
    cols = jnp.where(iota == ej_ref[...], g_ref[...], 0.0).astype(BF16)
    gm = lax.dot_general(rows, cols, (((2,), (2,)), ((0,), (0,))), preferred_element_type=F32)
    o_ref[...] = jnp.swapaxes(gm, 0, 1).astype(o_ref.dtype)


def _gate_matrix(ei, ej, g):
    n, _, slots = ei.shape
    tb = 128
    spec = pl.BlockSpec((tb, 1, slots), lambda i: (i, 0, 0))
    return pl.pallas_call(
        _gate_body,
        grid=(n // tb,),
        in_specs=[spec, spec, spec],
        out_specs=pl.BlockSpec((PEER_NKEYS, tb, PEER_NKEYS), lambda i: (0, i, 0)),
        out_shape=jax.ShapeDtypeStruct((PEER_NKEYS, n, PEER_NKEYS), BF16),
        compiler_params=_cparams(("parallel",)),
        name="peer_gate_matrix",
    )(ei, ej, g)


def _expert_body(x_ref, u_ref, g_ref, v_ref, o_ref):
    @pl.when(pl.program_id(1) == 0)
    def _():
        o_ref[...] = jnp.zeros(o_ref.shape, F32)

    hid = _dot_nt(x_ref[...], u_ref[...].astype(BF16))
    gate = jnp.concatenate([g_ref[i] for i in range(g_ref.shape[0])], axis=1)
    act = (jax.nn.gelu(hid) * gate.astype(F32)).astype(BF16)
    o_ref[...] += jnp.dot(act, v_ref[...].astype(BF16), preferred_element_type=F32)


def _peer_experts(xb, u, gm, v):
    n = xb.shape[0]
    ne = u.shape[0]
    tn, te = 1024, 512
    return pl.pallas_call(
        _expert_body,
        grid=(n // tn, ne // te),
        in_specs=[pl.BlockSpec((tn, D_MODEL), lambda i, j: (i, 0)),
                  pl.BlockSpec((te, D_MODEL), lambda i, j: (j, 0)),
                  pl.BlockSpec((te // PEER_NKEYS, tn, PEER_NKEYS), lambda i, j: (j, i, 0)),
                  pl.BlockSpec((te, D_MODEL), lambda i, j: (j, 0))],
        out_specs=pl.BlockSpec((tn, D_MODEL), lambda i, j: (i, 0)),
        out_shape=jax.ShapeDtypeStruct((n, D_MODEL), F32),
        compiler_params=_cparams(("parallel", "arbitrary")),
        name="peer_experts",
    )(xb, u, gm, v)


def _token_mixer(x, mem, w_in, diff_lambda, diff_subln, cmp_pe, cmp_w1, cmp_w2, w_mem_kv, w_branch, rel_bias):
    b, t, _ = x.shape
    n = b * t
    L = ATT_TILE
    xb = x.reshape(n, D_MODEL).astype(BF16)

    w_a, w_mg, w_g = _w_prep(w_in.T)
    ha = _matmul(xb, w_a, BF16, 1024, 1664, "in_proj_a", w_transposed=True)
    hg = _matmul(xb, w_g, F32, 1024, W_G_COLS, "nsa_gate_logits", w_transposed=True)
    ha3 = ha.reshape(b, t, ha.shape[1])

    tab1d = rel_bias[_rel_bucket(jnp.arange(t))].T * LOG2E
    n_far = min(t // L, REL_MAX_DIST // L + 2)
    bias_a = _bias_tiles(tab1d[:A_HEADS], t, -1, n_far + 1, t, 1)
    bias_b = tab1d[A_HEADS:]
    bias_sel = _bias_tiles(bias_b, t, -SEL_NQ, n_far + SEL_NQ, t, B_HPG)
    n_win = min(t // L, WINDOW // L + 1)
    bias_win = _bias_tiles(bias_b, t, -1, n_win + 1, WINDOW, B_HPG)

    def values_t(v):
        heads, dv = v.shape[2:]
        return v.reshape(b, t // L, L, heads, dv).transpose(0, 3, 1, 4, 2)

    o_a = _diff_attention(ha3, values_t(ha3[:, :, 3072:4096].reshape(b, t, A_HEADS, A_DV)), bias_a, diff_lambda,
                          jnp.broadcast_to(diff_subln[:, None], (A_DV, LANES)), b, t)

    glog_t = hg[:, :48].reshape(b, t, 3, B_GROUPS, B_HPG).transpose(2, 0, 3, 4, 1)
    nr = t // CMP_STRIDE
    w1 = cmp_w1.reshape(2, CMP_LEN, B_DH, CMP_HIDDEN)
    z1 = jnp.zeros_like(w1[0])
    w1_bd = jnp.concatenate([jnp.concatenate([w1[0], z1], axis=2), jnp.concatenate([z1, w1[1]], axis=2)],
                            axis=1).astype(BF16)
    z2 = jnp.zeros_like(cmp_w2[0])
    w2_bd = jnp.concatenate([jnp.concatenate([cmp_w2[0], z2], axis=1), jnp.concatenate([z2, cmp_w2[1]], axis=1)],
                            axis=0).astype(BF16)
    pe_bd = jnp.broadcast_to(jnp.concatenate([cmp_pe[0], cmp_pe[1]], axis=1)[:, None, :],
                             (CMP_LEN, 8, 2 * B_DH)).astype(BF16)
    kvc = _compress(ha3[:, :, 5120:5632].astype(F32), w1_bd, w2_bd, pe_bd)
    n_c = (t - CMP_LEN) // CMP_STRIDE + 1
    n_sel = t // SEL_BLOCK
    cidx = np.arange(nr)[:, None] * CMP_STRIDE + np.arange(CMP_LEN)[None, :]
    overlap = (cidx[:, :, None] // SEL_BLOCK == np.arange(n_sel)[None, None, :]).astype(np.float32).mean(axis=1)
    overlap[n_c:] = 0.0
    overlap_t = np.zeros((LANES, nr), np.float32)
    overlap_t[:n_sel] = overlap.T
    o_c, sel_mask = _nsa_compressed(ha3, kvc, jnp.asarray(overlap_t, BF16), glog_t[0], b, t, n_c,
                                    min(SEL_TOPN, n_sel))
    v_sel = ha3[:, :, 5632:6144].reshape(b, t, B_GROUPS, 2, B_DH)[:, :, :, 1]
    v_win = ha3[:, :, 6144:6656].reshape(b, t, B_GROUPS, 2, B_DH)[:, :, :, 1]
    assert n_sel <= B_DH
    hot = np.zeros((t, LANES), np.float32)
    hot[np.arange(t), B_DH + np.arange(t) // SEL_BLOCK] = 1.0
    o_cs = _nsa_branch(ha3, values_t(v_sel), bias_sel, glog_t[1], o_c, sel_mask, jnp.asarray(hot, BF16), b, t,
                       "sel", SEL_NQ, F32)
    o_b = _nsa_branch(ha3, values_t(v_win), bias_win, glog_t[2], o_cs, None, None, b, t, "win", 1, BF16)

    n_mem = mem.shape[1]
    memkv = _matmul(mem.reshape(b * n_mem, D_MODEL).astype(BF16), w_mem_kv.astype(BF16), BF16,
                    b * n_mem, 1024, "mem_kv_proj")
    o_m = _memory_attention(ha3, memkv, b, t, n_mem)

    flat = lambda o: o.reshape(n, MIX_W)
    return _merge(xb, w_mg, flat(o_a), flat(o_b), flat(o_m), w_branch.astype(BF16))


def _peer(x1, x1b, peer_wq, peer_keys, peer_u, peer_v):
    n = x1.shape[0]
    q = _matmul(x1b, peer_wq.astype(BF16), BF16, 1024, 1024, "peer_query")
    zk = jnp.zeros((PEER_HEADS, PEER_NKEYS, PEER_DKEY // 2), F32)
    keys_bd = jnp.concatenate([jnp.concatenate([peer_keys[:, 0], zk], axis=2),
                               jnp.concatenate([zk, peer_keys[:, 1]], axis=2)], axis=1).astype(BF16)
    ei, ej, gate = _peer_route(q, keys_bd)
    slots = lambda a: a.reshape(PEER_HEADS * PEER_TOPK, n).T.reshape(n, 1, PEER_HEADS * PEER_TOPK)
    gm = _gate_matrix(slots(ei), slots(ej), slots(gate))
    return _peer_experts(x1b, peer_u, gm, peer_v)


def kernel(x, mem, w_in, diff_lambda, diff_subln, cmp_pe, cmp_w1, cmp_w2, w_mem_kv, w_branch, w_out, ln1_g, ln1_b,
           peer_wq, peer_keys, peer_u, peer_v, ln2_g, ln2_b, rel_bias):
    b, t, _ = x.shape
    n = b * t
    for l in range(DEPTH):
        mixed = _token_mixer(x, mem, w_in[l], diff_lambda[l], diff_subln[l], cmp_pe[l], cmp_w1[l], cmp_w2[l],
                             w_mem_kv[l], w_branch[l], rel_bias)
        x1, x1b = _out_proj_ln(mixed, w_out[l].astype(BF16), x.reshape(n, D_MODEL),
                               ln1_g[l].reshape(1, D_MODEL), ln1_b[l].reshape(1, D_MODEL))
        y = _peer(x1, x1b, peer_wq[l], peer_keys[l], peer_u[l], peer_v[l])
        x = _residual_ln(x1, y, ln2_g[l].reshape(1, D_MODEL), ln2_b[l].reshape(1, D_MODEL)).reshape(b, t, D_MODEL)
    return x
```

```python
import functools
import math

import numpy as np
import jax
import jax.numpy as jnp
from jax import lax
from jax.experimental import pallas as pl
from jax.experimental.pallas import tpu as pltpu

F32 = jnp.float32
BF16 = jnp.bfloat16

D_MODEL = 2048
A_HEADS, A_DQK, A_DV = 8, 64, 128
B_HEADS, B_GROUPS, B_HPG, B_DH = 16, 4, 4, 64
CMP_LEN, CMP_STRIDE, CMP_HIDDEN = 32, 16, 256
SEL_BLOCK, SEL_TOPN, WINDOW = 64, 16, 512
M_HEADS, M_DH = 4, 256
REL_BUCKETS, REL_MAX_DIST = 32, 1024
PEER_HEADS, PEER_NKEYS, PEER_DKEY, PEER_TOPK = 8, 128, 256, 16
MIX_W = 1024
LN_EPS = 1e-5
FORCE = 1e9
DEPTH = 1
ALPHA = (2 * DEPTH) ** 0.25
LAMBDA_INIT = 0.8 - 0.6 * math.exp(-0.3 * 0)
LOG2E = math.log2(math.e)

NEG = -1e30
ATT_TILE = 256
LANES = 128
ONES_ROWS = 16
N_MERGE = 3
SEL_NQ = 1
VMEM_LIMIT = 56 * 1024 * 1024


def _cparams(sem):
    return pltpu.CompilerParams(dimension_semantics=sem, vmem_limit_bytes=VMEM_LIMIT)


def _dot_nt(a, b):
    return lax.dot_general(a, b, (((1,), (1,)), ((), ())), preferred_element_type=F32)


def _mm_body(x_ref, w_ref, o_ref):
    o_ref[...] = jnp.dot(x_ref[...], w_ref[...], preferred_element_type=F32).astype(o_ref.dtype)


def _mm_nt_body(x_ref, w_ref, o_ref):
    o_ref[...] = _dot_nt(x_ref[...], w_ref[...]).astype(o_ref.dtype)


def _matmul(x, w, out_dtype, tm, tn, name, w_transposed=False):
    m, k = x.shape
    n = w.shape[0] if w_transposed else w.shape[1]
    w_spec = pl.BlockSpec((tn, k), lambda i, j: (j, 0)) if w_transposed else pl.BlockSpec((k, tn), lambda i, j: (0, j))
    return pl.pallas_call(
        _mm_nt_body if w_transposed else _mm_body,
        grid=(m // tm, n // tn),
        in_specs=[pl.BlockSpec((tm, k), lambda i, j: (i, 0)), w_spec],
        out_specs=pl.BlockSpec((tm, tn), lambda i, j: (i, j)),
        out_shape=jax.ShapeDtypeStruct((m, n), out_dtype),
        compiler_params=_cparams(("parallel", "arbitrary")),
        name=name,
    )(x, w)


def _in_proj_body(x_ref, w_ref, wg_ref, o_ref, og_ref, xb_ref):
    xb = x_ref[...].astype(BF16)
    o_ref[...] = _dot_nt(xb, w_ref[...]).astype(o_ref.dtype)

    @pl.when(pl.program_id(1) == 0)
    def _():
        xb_ref[...] = xb
        og_ref[...] = _dot_nt(xb, wg_ref[...])


def _in_proj(x, w_a, w_g):
    n, k = x.shape
    tm, tn = 512, 1664
    return pl.pallas_call(
        _in_proj_body,
        grid=(n // tm, W_A_COLS // tn),
        in_specs=[pl.BlockSpec((tm, k), lambda i, j: (i, 0)),
                  pl.BlockSpec((tn, k), lambda i, j: (j, 0)),
                  pl.BlockSpec((W_G_COLS, k), lambda i, j: (0, 0))],
        out_specs=[pl.BlockSpec((tm, tn), lambda i, j: (i, j)),
                   pl.BlockSpec((tm, W_G_COLS), lambda i, j: (i, 0)),
                   pl.BlockSpec((tm, k), lambda i, j: (i, 0))],
        out_shape=[jax.ShapeDtypeStruct((n, W_A_COLS), BF16), jax.ShapeDtypeStruct((n, W_G_COLS), F32),
                   jax.ShapeDtypeStruct((n, k), BF16)],
        compiler_params=_cparams(("parallel", "arbitrary")),
        name="in_proj",
    )(x, w_a, w_g)


W_A_COLS = 6656
W_G_COLS = 128


def _w_prep_body(w_ref, wa_ref, wm_ref, wg_ref):
    def put(dst_ref, d0, s0, n, c=1.0):
        val = w_ref[s0:s0 + n, :]
        dst_ref[d0:d0 + n, :] = (val * c if c != 1.0 else val).astype(dst_ref.dtype)

    put(wa_ref, 0, 5680, 1024, M_DH ** -0.5 * LOG2E)
    put(wa_ref, 1024, 0, 1024, A_DQK ** -0.5 * LOG2E)
    put(wa_ref, 2048, 1024, 2048)
    put(wa_ref, 4096, 3072, 1024, B_DH ** -0.5 * LOG2E)
    for br in range(3):
        for g in range(B_GROUPS):
            src = 4096 + br * 2 * B_GROUPS * B_DH + g * B_DH
            dst = 5120 + (br * B_GROUPS + g) * 2 * B_DH
            put(wa_ref, dst, src, B_DH)
            put(wa_ref, dst + B_DH, src + B_GROUPS * B_DH, B_DH)
    put(wm_ref, 0, 6704, N_MERGE * D_MODEL)
    put(wg_ref, 0, 5632, 48)
    wg_ref[48:, :] = jnp.zeros((W_G_COLS - 48, wg_ref.shape[1]), wg_ref.dtype)


def _w_prep(w_in_t):
    c, k = w_in_t.shape
    tk = 256
    col = lambda i: (0, i)
    return pl.pallas_call(
        _w_prep_body,
        grid=(k // tk,),
        in_specs=[pl.BlockSpec((c, tk), col)],
        out_specs=[pl.BlockSpec((W_A_COLS, tk), col), pl.BlockSpec((N_MERGE * D_MODEL, tk), col),
                   pl.BlockSpec((W_G_COLS, tk), col)],
        out_shape=[jax.ShapeDtypeStruct((W_A_COLS, k), BF16), jax.ShapeDtypeStruct((N_MERGE * D_MODEL, k), BF16),
                   jax.ShapeDtypeStruct((W_G_COLS, k), BF16)],
        compiler_params=_cparams(("parallel",)),
        name="w_in_regroup",
    )(w_in_t)


def _rel_bucket(dist):
    n = jnp.maximum(dist, 0)
    max_exact = REL_BUCKETS // 2
    nf = jnp.maximum(n, 1).astype(jnp.float32)
    large = max_exact + (jnp.log(nf / max_exact) / math.log(REL_MAX_DIST / max_exact)
                         * (REL_BUCKETS - max_exact)).astype(jnp.int32)
    large = jnp.minimum(large, REL_BUCKETS - 1)
    return jnp.where(n < max_exact, n, large)


def _bias_tiles(tab1d, t, first, n_tiles, max_dist, hpr):
    L = ATT_TILE
    m = np.arange(2 * L)
    off = np.where(m <= L, m, m - 2 * L)
    d = (first + np.arange(n_tiles))[:, None] * L + off[None, :]
    ok = (d >= 0) & (d < max_dist)
    h = tab1d.shape[0]
    rp = jnp.where(ok[:, None], tab1d.T[np.clip(d, 0, t - 1)].transpose(0, 2, 1), NEG)
    return pl.pallas_call(
        functools.partial(_toeplitz_body, hpr=hpr),
        grid=(n_tiles,),
        in_specs=[pl.BlockSpec((None, h, 1, 2 * L), lambda c: (c, 0, 0, 0))],
        out_specs=pl.BlockSpec((None, h // hpr, L, hpr * L), lambda c: (c, 0, 0, 0)),
        out_shape=jax.ShapeDtypeStruct((n_tiles, h // hpr, L, hpr * L), F32),
        compiler_params=_cparams(("parallel",)),
        name="bias_tiles",
    )(rp.reshape(n_tiles, h, 1, 2 * L))


def _toeplitz_body(rp_ref, o_ref, *, hpr):
    L = o_ref.shape[1]
    for hd in range(rp_ref.shape[0]):
        rows = jnp.broadcast_to(rp_ref[hd], (L, 2 * L))
        tile = pltpu.roll(rows, 0, 1, stride=1, stride_axis=0)[:, :L]
        o_ref[hd // hpr, :, (hd % hpr) * L:(hd % hpr + 1) * L] = tile


def _flash_loop(lo, hi, scores, values, m_ref, acc_ref, s_ref, mx_ref):
    def ahead(kt, slot):
        s = scores(kt)
        s_ref[slot] = s
        mx_ref[slot] = jnp.broadcast_to(jnp.max(s, axis=0, keepdims=True), mx_ref.shape[1:])

    def finish(kt, slot):
        m_prev = m_ref[...]
        m_new = jnp.maximum(m_prev, mx_ref[slot])
        alpha = jnp.exp2(m_prev - m_new)
        p = jnp.exp2(s_ref[slot] - m_new[0:1])
        acc_ref[...] = alpha[0:1] * acc_ref[...] + jnp.dot(values(kt), p.astype(BF16), preferred_element_type=F32)
        m_ref[...] = m_new

    def body(i, carry):
        kt = lo + 2 * i
        ahead(kt + 1, 1)
        finish(kt, 0)
        ahead(kt + 2, 0)
        finish(kt + 1, 1)
        return carry

    ahead(lo, 0)
    lax.fori_loop(0, (hi - lo + 1) // 2, body, 0)


def _diff_body(q_ref, k_ref, vt_ref, bias_ref, lam_ref, g_ref, o_ref, m_sc, acc_sc, s_sc, mx_sc, *, n_bt):
    L = ATT_TILE
    tq = 2 * L
    last = vt_ref.shape[0] - 1
    qi = pl.program_id(2)
    m_sc[...] = jnp.full(m_sc.shape, NEG, F32)
    acc_sc[...] = jnp.zeros(acc_sc.shape, F32)
    qt = q_ref[...].astype(F32).T.astype(BF16)
    zero = jnp.zeros((A_DQK, tq), BF16)
    q_cat = jnp.concatenate([jnp.concatenate([qt[:A_DQK], zero], axis=0),
                             jnp.concatenate([zero, qt[A_DQK:]], axis=0)], axis=1)

    def scores(kt):
        off = pl.multiple_of(jnp.minimum(kt, last) * L, L)
        k = k_ref[pl.ds(off, L), :]
        d0 = 2 * qi - kt
        bias = jnp.concatenate([bias_ref[jnp.clip(d0 + 1, 0, n_bt - 1)],
                                bias_ref[jnp.clip(d0 + 2, 0, n_bt - 1)]], axis=1)
        s = jnp.dot(k, q_cat, preferred_element_type=F32)
        return jnp.concatenate([s[:, :tq] + bias, s[:, tq:] + bias], axis=1)

    ones = jnp.ones((ONES_ROWS, L), BF16)

    def values(kt):
        return jnp.concatenate([vt_ref[jnp.minimum(kt, last)], ones], axis=0)

    _flash_loop(0, 2 * qi + 2, scores, values, m_sc, acc_sc, s_sc, mx_sc)

    lp = lam_ref[...]
    lam = (jnp.exp(jnp.sum(lp[0:1] * lp[1:2], axis=1, keepdims=True))
           - jnp.exp(jnp.sum(lp[2:3] * lp[3:4], axis=1, keepdims=True)) + LAMBDA_INIT)
    acc = acc_sc[...]
    o0 = acc[:A_DV, :tq] / jnp.maximum(acc[A_DV:A_DV + 1, :tq], 1e-30)
    o1 = acc[:A_DV, tq:] / jnp.maximum(acc[A_DV:A_DV + 1, tq:], 1e-30)
    o = o0 - lam * o1
    g = jnp.concatenate([g_ref[...]] * (tq // LANES), axis=1)
    o = o * lax.rsqrt(jnp.mean(o * o, axis=0, keepdims=True) + LN_EPS) * g
    o_ref[...] = (o * (1.0 - LAMBDA_INIT)).T.astype(o_ref.dtype)


def _diff_attention(ha, v_t, bias, lam_params, subln, b, t):
    L = ATT_TILE
    tq = 2 * L
    n_bt = bias.shape[0]
    return pl.pallas_call(
        functools.partial(_diff_body, n_bt=n_bt),
        grid=(b, A_HEADS, t // tq),
        in_specs=[
            pl.BlockSpec((None, tq, 128), lambda bi, h, qi: (bi, qi, 8 + h)),
            pl.BlockSpec((None, t, 128), lambda bi, h, qi: (bi, 0, 16 + h)),
            pl.BlockSpec((None, None, t // L, A_DV, L), lambda bi, h, qi: (bi, h, 0, 0, 0)),
            pl.BlockSpec((n_bt, None, L, L), lambda bi, h, qi: (0, h, 0, 0)),
            pl.BlockSpec((4, A_DQK), lambda bi, h, qi: (0, 0)),
            pl.BlockSpec((A_DV, LANES), lambda bi, h, qi: (0, 0)),
        ],
        out_specs=pl.BlockSpec((None, tq, 128), lambda bi, h, qi: (bi, qi, h)),
        out_shape=jax.ShapeDtypeStruct((b, t, MIX_W), BF16),
        scratch_shapes=[pltpu.VMEM((8, 2 * tq), F32), pltpu.VMEM((A_DV + ONES_ROWS, 2 * tq), F32),
                        pltpu.VMEM((2, L, 2 * tq), F32), pltpu.VMEM((2, 8, 2 * tq), F32)],
        compiler_params=_cparams(("parallel", "parallel", "arbitrary")),
        name="diff_attention",
    )(ha, ha, v_t, bias, lam_params, subln)


def _compress_body(kv_ref, w1_ref, w2_ref, pe_ref, o_ref):
    nr = o_ref.shape[0]
    first = jnp.zeros((nr, 2 * CMP_HIDDEN), F32)
    second = jnp.zeros((nr, 2 * CMP_HIDDEN), F32)
    for l in range(CMP_STRIDE):
        rows = kv_ref[pl.ds(l, nr, stride=CMP_STRIDE), :].astype(BF16)
        first = first + jnp.dot(rows, w1_ref[l], preferred_element_type=F32)
        second = second + jnp.dot(rows, w1_ref[CMP_STRIDE + l], preferred_element_type=F32)
    pw = jnp.zeros((8, 2 * CMP_HIDDEN), F32)
    for l in range(CMP_LEN):
        pw = pw + jnp.dot(pe_ref[l], w1_ref[l], preferred_element_type=F32)
    second = jnp.concatenate([second[1:], second[:1]], axis=0)
    hdn = jax.nn.gelu(first + second + pw[0:1])
    o_ref[...] = jnp.dot(hdn.astype(BF16), w2_ref[...], preferred_element_type=F32)


def _compress(kv, w1_bd, w2_bd, pe_bd):
    b, t, _ = kv.shape
    nr = t // CMP_STRIDE
    return pl.pallas_call(
        _compress_body,
        grid=(b, B_GROUPS),
        in_specs=[
            pl.BlockSpec((None, t, LANES), lambda bi, gi: (bi, 0, gi)),
            pl.BlockSpec(w1_bd.shape, lambda bi, gi: (0, 0, 0)),
            pl.BlockSpec(w2_bd.shape, lambda bi, gi: (0, 0)),
            pl.BlockSpec(pe_bd.shape, lambda bi, gi: (0, 0, 0)),
        ],
        out_specs=pl.BlockSpec((None, None, nr, 2 * B_DH), lambda bi, gi: (bi, gi, 0, 0)),
        out_shape=jax.ShapeDtypeStruct((b, B_GROUPS, nr, 2 * B_DH), F32),
        compiler_params=_cparams(("parallel", "parallel")),
        name="nsa_compress",
    )(kv, w1_bd, w2_bd, pe_bd)


def _topk_mask_axis0(v, k):
    r, n = v.shape
    iota = lax.broadcasted_iota(jnp.int32, (r, n), 0).astype(F32)

    def body(_, c):
        v, sel = c
        mx = jnp.max(v, axis=0, keepdims=True)
        idx = jnp.min(jnp.where(v == mx, iota, float(r)), axis=0, keepdims=True)
        hit = iota == idx
        return jnp.where(hit, -jnp.inf, v), jnp.where(hit, 1.0, sel)

    _, sel = lax.fori_loop(0, k, body, (v, jnp.zeros((r, n), F32)))
    return sel


def _cmp_body(q_ref, kvc_ref, ov_ref, gl_ref, o_ref, mask_ref, *, n_c, n_top):
    L = ATT_TILE
    qi = pl.program_id(2)
    ncp = kvc_ref.shape[0]
    n_sel = mask_ref.shape[0]
    qt = q_ref[...].astype(F32).T.astype(BF16)
    qh = jnp.concatenate([qt[h * B_DH:(h + 1) * B_DH] for h in range(B_HPG)], axis=1)
    kvc = kvc_ref[...]
    s = jnp.dot(kvc[:, :B_DH].astype(BF16), qh, preferred_element_type=F32)
    tcol = qi * L + lax.broadcasted_iota(jnp.int32, (1, L), 1)
    crow = lax.broadcasted_iota(jnp.int32, (ncp, 1), 0)
    seen = jnp.where(crow < n_c, crow * CMP_STRIDE + (CMP_LEN - 1), jnp.int32(2 ** 30)) <= tcol
    valid = jnp.concatenate([seen] * B_HPG, axis=1)
    s = jnp.where(valid, s, NEG)
    mx = jnp.max(s, axis=0, keepdims=True)
    e = jnp.where(valid, jnp.exp2(s - mx), 0.0)
    p = e / jnp.maximum(jnp.sum(e, axis=0, keepdims=True), 1e-30)
    o = jnp.dot(kvc.T[B_DH:].astype(BF16), p.astype(BF16), preferred_element_type=F32)
    gate = jax.nn.sigmoid(gl_ref[...])
    o = jnp.concatenate([o[:, h * L:(h + 1) * L] * gate[h:h + 1] for h in range(B_HPG)], axis=0)
    o_ref[...] = o.T

    psum = p[:, :L] + p[:, L:2 * L] + p[:, 2 * L:3 * L] + p[:, 3 * L:]
    ov_t = ov_ref[...]
    imp = jnp.zeros((ov_t.shape[0], L), F32)
    rem = psum
    for _ in range(3):
        part = rem.astype(BF16)
        imp = imp + jnp.dot(ov_t, part, preferred_element_type=F32)
        rem = rem - part.astype(F32)
    imp = imp[:n_sel]
    blk = lax.broadcasted_iota(jnp.int32, (n_sel, 1), 0)
    cur = jnp.right_shift(tcol, int(math.log2(SEL_BLOCK)))
    imp = jnp.where(blk * SEL_BLOCK > tcol, -FORCE, imp)
    imp = jnp.where(blk == 0, FORCE, imp)
    imp = jnp.where(blk == cur, FORCE, imp)
    imp = jnp.where(blk == cur - 1, FORCE, imp)
    mask_ref[...] = jnp.where(_topk_mask_axis0(imp, n_top) > 0.0, 0.0, NEG).astype(mask_ref.dtype)


def _nsa_compressed(ha, kvc, overlap_t, glog_t, b, t, n_c, n_top):
    L = ATT_TILE
    ncp = kvc.shape[2]
    n_sel = t // SEL_BLOCK
    return pl.pallas_call(
        functools.partial(_cmp_body, n_c=n_c, n_top=n_top),
        grid=(b, B_GROUPS, t // L),
        in_specs=[
            pl.BlockSpec((None, L, 256), lambda bi, g, qi: (bi, qi, 16 + g)),
            pl.BlockSpec((None, None, ncp, 2 * B_DH), lambda bi, g, qi: (bi, g, 0, 0)),
            pl.BlockSpec(overlap_t.shape, lambda bi, g, qi: (0, 0)),
            pl.BlockSpec((None, None, B_HPG, L), lambda bi, g, qi: (bi, g, 0, qi)),
        ],
        out_specs=[
            pl.BlockSpec((None, L, 256), lambda bi, g, qi: (bi, qi, g)),
            pl.BlockSpec((None, None, n_sel, L), lambda bi, g, qi: (bi, g, 0, qi)),
        ],
        out_shape=[jax.ShapeDtypeStruct((b, t, MIX_W), F32),
                   jax.ShapeDtypeStruct((b, B_GROUPS, n_sel, t), BF16)],
        compiler_params=_cparams(("parallel", "parallel", "arbitrary")),
        name="nsa_compressed_select",
    )(ha, kvc, overlap_t, glog_t)


def _nsa_body(*refs, mode, n_bt):
    if mode == "sel":
        q_ref, kv_ref, vt_ref, bias_ref, gl_ref, prev_ref, mask_ref, hot_ref, o_ref, m_sc, acc_sc, s_sc, mx_sc = refs
    else:
        q_ref, kv_ref, vt_ref, bias_ref, gl_ref, prev_ref, o_ref, m_sc, acc_sc, s_sc, mx_sc = refs
    L = ATT_TILE
    nq = q_ref.shape[0] // L
    qi = pl.program_id(2)
    m_sc[...] = jnp.full(m_sc.shape, NEG, F32)
    acc_sc[...] = jnp.zeros(acc_sc.shape, F32)
    qt = q_ref[...].astype(F32).T.astype(BF16)
    if mode == "sel":
        n_sel = mask_ref.shape[0]
        qm = mask_ref[...]
        if n_sel < B_DH:
            qm = jnp.concatenate([qm, jnp.zeros((B_DH - n_sel, nq * L), BF16)], axis=0)
    else:
        qm = jnp.zeros((B_DH, nq * L), BF16)
    q_aug_t = jnp.concatenate(
        [jnp.concatenate([qt[h * B_DH:(h + 1) * B_DH, s * L:(s + 1) * L], qm[:, s * L:(s + 1) * L]], axis=0)
         for s in range(nq) for h in range(B_HPG)], axis=1)

    last = vt_ref.shape[0] - 1
    ones = jnp.ones((ONES_ROWS, L), BF16)
    k_lanes = lax.broadcasted_iota(jnp.int32, (L, LANES), 1) < B_DH

    def scores(kt):
        off = pl.multiple_of(jnp.minimum(kt, last) * L, L)
        idx = nq * qi - kt + nq
        bias = [bias_ref[jnp.clip(idx + s, 0, n_bt - 1)] for s in range(nq)]
        bias = bias[0] if nq == 1 else jnp.concatenate(bias, axis=1)
        k_aug = kv_ref[pl.ds(off, L), :]
        if mode == "sel":
            k_aug = jnp.where(k_lanes, k_aug, hot_ref[pl.ds(off, L), :])
        return jnp.dot(k_aug, q_aug_t, preferred_element_type=F32) + bias

    def values(kt):
        return jnp.concatenate([vt_ref[jnp.minimum(kt, last)], ones], axis=0)

    lo = 0 if mode == "sel" else jnp.maximum(nq * qi - (n_bt - nq - 1), 0)
    _flash_loop(lo, nq * qi + nq, scores, values, m_sc, acc_sc, s_sc, mx_sc)
    acc = acc_sc[...]
    o = acc[:B_DH] / jnp.maximum(acc[B_DH:B_DH + 1], 1e-30)
    gate = jax.nn.sigmoid(gl_ref[...])
    heads = []
    for h in range(B_HPG):
        cols = [o[:, (s * B_HPG + h) * L:(s * B_HPG + h + 1) * L] for s in range(nq)]
        heads.append((cols[0] if nq == 1 else jnp.concatenate(cols, axis=1)) * gate[h:h + 1])
    o_ref[...] = (prev_ref[...] + jnp.concatenate(heads, axis=0).T).astype(o_ref.dtype)


def _nsa_branch(ha, v_t, bias, glog_t, prev, mask_t, hot, b, t, mode, nq, out_dtype):
    L = ATT_TILE
    tq = nq * L
    n_bt = bias.shape[0]
    branch = 1 if mode == "sel" else 2
    in_specs = [
        pl.BlockSpec((None, tq, 256), lambda bi, g, qi: (bi, qi, 16 + g)),
        pl.BlockSpec((None, t, LANES), lambda bi, g, qi: (bi, 0, 40 + 4 * branch + g)),
        pl.BlockSpec((None, None, t // L, B_DH, L), lambda bi, g, qi: (bi, g, 0, 0, 0)),
        pl.BlockSpec((n_bt, None, L, B_HPG * L), lambda bi, g, qi: (0, g, 0, 0)),
        pl.BlockSpec((None, None, B_HPG, tq), lambda bi, g, qi: (bi, g, 0, qi)),
        pl.BlockSpec((None, tq, 256), lambda bi, g, qi: (bi, qi, g)),
    ]
    args = [ha, ha, v_t, bias, glog_t, prev]
    if mode == "sel":
        n_sel = mask_t.shape[2]
        in_specs.append(pl.BlockSpec((None, None, n_sel, tq), lambda bi, g, qi: (bi, g, 0, qi)))
        in_specs.append(pl.BlockSpec((t, LANES), lambda bi, g, qi: (0, 0)))
        args += [mask_t, hot]
    return pl.pallas_call(
        functools.partial(_nsa_body, mode=mode, n_bt=n_bt),
        grid=(b, B_GROUPS, t // tq),
        in_specs=in_specs,
        out_specs=pl.BlockSpec((None, tq, 256), lambda bi, g, qi: (bi, qi, g)),
        out_shape=jax.ShapeDtypeStruct((b, t, MIX_W), out_dtype),
        scratch_shapes=[pltpu.VMEM((8, B_HPG * tq), F32), pltpu.VMEM((B_DH + ONES_ROWS, B_HPG * tq), F32),
                        pltpu.VMEM((2, L, B_HPG * tq), F32), pltpu.VMEM((2, 8, B_HPG * tq), F32)],
        compiler_params=_cparams(("parallel", "parallel", "arbitrary")),
        name="nsa_" + mode,
    )(*args)


def _mem_body(q_ref, k_ref, v_ref, o_ref):
    q = q_ref[...]
    outs = []
    for h in range(M_HEADS):
        sl = slice(h * M_DH, (h + 1) * M_DH)
        s = _dot_nt(q[:, sl], k_ref[:, sl])
        e = jnp.exp2(s - jnp.max(s, axis=1, keepdims=True))
        p = e / jnp.sum(e, axis=1, keepdims=True)
        outs.append(jnp.dot(p.astype(BF16), v_ref[:, sl], preferred_element_type=F32))
    o_ref[...] = jnp.concatenate(outs, axis=1).astype(o_ref.dtype)


def _memory_attention(ha, memkv, b, t, n_mem):
    tq = 512
    w = M_HEADS * M_DH
    return pl.pallas_call(
        _mem_body,
        grid=(b, t // tq),
        in_specs=[
            pl.BlockSpec((None, tq, w), lambda bi, qi: (bi, qi, 0)),
            pl.BlockSpec((n_mem, w), lambda bi, qi: (bi, 0)),
            pl.BlockSpec((n_mem, w), lambda bi, qi: (bi, 1)),
        ],
        out_specs=pl.BlockSpec((None, tq, w), lambda bi, qi: (bi, qi, 0)),
        out_shape=jax.ShapeDtypeStruct((b, t, w), BF16),
        compiler_params=_cparams(("parallel", "arbitrary")),
        name="memory_attention",
    )(ha, memkv, memkv)


def _merge_body(x_ref, wg_ref, oa_ref, ob_ref, om_ref, wb_ref, o_ref, acc_sc):
    n = pl.program_id(1)

    @pl.when(n == 0)
    def _():
        acc_sc[...] = jnp.zeros(acc_sc.shape, F32)

    gate = jax.nn.sigmoid(_dot_nt(x_ref[...], wg_ref[...]))
    branch = jnp.where(n == 0, oa_ref[...], jnp.where(n == 1, ob_ref[...], om_ref[...]))
    acc = acc_sc[...] + gate * jnp.dot(branch, wb_ref[...], preferred_element_type=F32)
    acc_sc[...] = acc
    o_ref[...] = acc.astype(o_ref.dtype)


def _merge(xb, w_gate_t, o_a, o_b, o_m, w_branch):
    n = xb.shape[0]
    tm = 512
    row = lambda i, j: (i, 0)
    return pl.pallas_call(
        _merge_body,
        grid=(n // tm, N_MERGE),
        in_specs=[pl.BlockSpec((tm, D_MODEL), row),
                  pl.BlockSpec((D_MODEL, D_MODEL), lambda i, j: (j, 0))]
                 + [pl.BlockSpec((tm, MIX_W), row)] * 3
                 + [pl.BlockSpec((None, MIX_W, D_MODEL), lambda i, j: (j, 0, 0))],
        out_specs=pl.BlockSpec((tm, D_MODEL), row),
        out_shape=jax.ShapeDtypeStruct((n, D_MODEL), BF16),
        scratch_shapes=[pltpu.VMEM((tm, D_MODEL), F32)],
        compiler_params=_cparams(("parallel", "arbitrary")),
        name="branch_merge",
    )(xb, w_gate_t, o_a, o_b, o_m, w_branch)


def _layer_norm(z, g, b):
    mu = jnp.mean(z, axis=1, keepdims=True)
    zc = z - mu
    var = jnp.mean(zc * zc, axis=1, keepdims=True)
    return zc * lax.rsqrt(var + LN_EPS) * g + b


def _out_ln_body(y_ref, w_ref, x_ref, g_ref, b_ref, o_ref, ob_ref):
    y = jnp.dot(y_ref[...], w_ref[...], preferred_element_type=F32)
    o = _layer_norm(ALPHA * x_ref[...] + y, g_ref[...], b_ref[...])
    o_ref[...] = o
    ob_ref[...] = o.astype(BF16)


def _out_proj_ln(mixed, w_out, x, g, b):
    n = x.shape[0]
    tm = 512
    row = lambda i: (i, 0)
    const = lambda i: (0, 0)
    return pl.pallas_call(
        _out_ln_body,
        grid=(n // tm,),
        in_specs=[pl.BlockSpec((tm, D_MODEL), row), pl.BlockSpec((D_MODEL, D_MODEL), const),
                  pl.BlockSpec((tm, D_MODEL), row), pl.BlockSpec((1, D_MODEL), const),
                  pl.BlockSpec((1, D_MODEL), const)],
        out_specs=[pl.BlockSpec((tm, D_MODEL), row), pl.BlockSpec((tm, D_MODEL), row)],
        out_shape=[jax.ShapeDtypeStruct((n, D_MODEL), F32), jax.ShapeDtypeStruct((n, D_MODEL), BF16)],
        compiler_params=_cparams(("parallel",)),
        name="out_proj_ln1",
    )(mixed, w_out, x, g, b)


def _res_ln_body(x_ref, y_ref, g_ref, b_ref, o_ref):
    o_ref[...] = _layer_norm(ALPHA * x_ref[...] + y_ref[...], g_ref[...], b_ref[...])


def _residual_ln(x, y, g, b):
    n = x.shape[0]
    tm = 512
    row = lambda i: (i, 0)
    const = lambda i: (0, 0)
    return pl.pallas_call(
        _res_ln_body,
        grid=(n // tm,),
        in_specs=[pl.BlockSpec((tm, D_MODEL), row), pl.BlockSpec((tm, D_MODEL), row),
                  pl.BlockSpec((1, D_MODEL), const), pl.BlockSpec((1, D_MODEL), const)],
        out_specs=pl.BlockSpec((tm, D_MODEL), row),
        out_shape=jax.ShapeDtypeStruct((n, D_MODEL), F32),
        compiler_params=_cparams(("parallel",)),
        name="residual_ln2",
    )(x, y, g, b)


def _topk_axis0(v, k):
    r, n = v.shape
    iota = lax.broadcasted_iota(jnp.int32, (r, n), 0).astype(F32)
    slot = lax.broadcasted_iota(jnp.int32, (k, n), 0)

    def body(it, c):
        v, vals, idxs = c
        mx = jnp.max(v, axis=0, keepdims=True)
        idx = jnp.min(jnp.where(v == mx, iota, float(r)), axis=0, keepdims=True)
        v = jnp.where(iota == idx, -jnp.inf, v)
        return v, jnp.where(slot == it, mx, vals), jnp.where(slot == it, idx, idxs)

    _, vals, idxs = lax.fori_loop(0, k, body, (v, jnp.zeros((k, n), F32), jnp.zeros((k, n), F32)))
    return vals, idxs


def _pick_rows(table, pos, k):
    out = jnp.zeros(pos.shape, F32)
    for a in range(k):
        out = out + jnp.where(pos == float(a), table[a:a + 1], 0.0)
    return out


def _route_body(q_ref, keys_ref, ei_ref, ej_ref, g_ref):
    k = PEER_TOPK
    scores = _dot_nt(keys_ref[...], q_ref[...])
    v0, i0 = _topk_axis0(scores[:PEER_NKEYS], k)
    v1, i1 = _topk_axis0(scores[PEER_NKEYS:], k)
    counts = [k // (a + 1) for a in range(k)]
    starts = np.cumsum([0] + counts[:-1])
    pad = (-sum(counts)) % 8
    comb = jnp.concatenate([v0[a:a + 1] + v1[:counts[a]] for a in range(k)]
                           + [jnp.full((pad, v0.shape[1]), -jnp.inf, F32)], axis=0)
    sf, pos = _topk_axis0(comb, k)
    pa = jnp.zeros(pos.shape, F32)
    pb = pos
    for a in range(1, k):
        later = pos >= float(starts[a])
        pa = pa + jnp.where(later, 1.0, 0.0)
        pb = pb - jnp.where(later, float(counts[a - 1]), 0.0)
    ei_ref[...] = _pick_rows(i0, pa, k)
    ej_ref[...] = _pick_rows(i1, pb, k)
    e = jnp.exp(sf - jnp.max(sf, axis=0, keepdims=True))
    g_ref[...] = e / jnp.sum(e, axis=0, keepdims=True)


def _peer_route(q, keys):
    n = q.shape[0]
    tn = 512
    out = jax.ShapeDtypeStruct((PEER_HEADS, PEER_TOPK, n), F32)
    ospec = pl.BlockSpec((None, PEER_TOPK, tn), lambda i, h: (h, 0, i))
    return pl.pallas_call(
        _route_body,
        grid=(n // tn, PEER_HEADS),
        in_specs=[pl.BlockSpec((tn, PEER_DKEY), lambda i, h: (i, h)),
                  pl.BlockSpec((None, 2 * PEER_NKEYS, PEER_DKEY), lambda i, h: (h, 0, 0))],
        out_specs=[ospec, ospec, ospec],
        out_shape=[out, out, out],
        compiler_params=_cparams(("parallel", "arbitrary")),
        name="peer_route",
    )(q, keys)


def _gate_body(ei_ref, ej_ref, g_ref, o_ref):
    tb = ei_ref.shape[0]
    nk = PEER_NKEYS
    iota = lax.broadcasted_iota(jnp.int32, (tb, nk, ei_ref.shape[2]), 1).astype(F32)
    rows = jnp.where(iota == ei_ref[...], 1.0, 0.0).astype(BF16)
    cols = jnp.where(iota == ej_ref[...], g_ref[...], 0.0).astype(BF16)
    gm = lax.dot_general(rows, cols, (((2,), (2,)), ((0,), (0,))), preferred_element_type=F32)
    o_ref[...] = jnp.swapaxes(gm, 0, 1).astype(o_ref.dtype)


def _gate_matrix(ei, ej, g):
    n, _, slots = ei.shape
    tb = 128
    spec = pl.BlockSpec((tb, 1, slots), lambda i: (i, 0, 0))
    return pl.pallas_call(
        _gate_body,
        grid=(n // tb,),
        in_specs=[spec, spec, spec],
        out_specs=pl.BlockSpec((PEER_NKEYS, tb, PEER_NKEYS), lambda i: (0, i, 0)),
        out_shape=jax.ShapeDtypeStruct((PEER_NKEYS, n, PEER_NKEYS), BF16),
        compiler_params=_cparams(("parallel",)),
        name="peer_gate_matrix",
    )(ei, ej, g)


def _expert_body(x_ref, u_ref, g_ref, v_ref, o_ref):
    @pl.when(pl.program_id(1) == 0)
    def _():
        o_ref[...] = jnp.zeros(o_ref.shape, F32)

    hid = _dot_nt(x_ref[...], u_ref[...].astype(BF16))
    gate = jnp.concatenate([g_ref[i] for i in range(g_ref.shape[0])], axis=1)
    act = (jax.nn.gelu(hid) * gate.astype(F32)).astype(BF16)
    o_ref[...] += jnp.dot(act, v_ref[...].astype(BF16), preferred_element_type=F32)


def _peer_experts(xb, u, gm, v):
    n = xb.shape[0]
    ne = u.shape[0]
    tn, te = 1024, 512
    return pl.pallas_call(
        _expert_body,
        grid=(n // tn, ne // te),
        in_specs=[pl.BlockSpec((tn, D_MODEL), lambda i, j: (i, 0)),
                  pl.BlockSpec((te, D_MODEL), lambda i, j: (j, 0)),
                  pl.BlockSpec((te // PEER_NKEYS, tn, PEER_NKEYS), lambda i, j: (j, i, 0)),
                  pl.BlockSpec((te, D_MODEL), lambda i, j: (j, 0))],
        out_specs=pl.BlockSpec((tn, D_MODEL), lambda i, j: (i, 0)),
        out_shape=jax.ShapeDtypeStruct((n, D_MODEL), F32),
        compiler_params=_cparams(("parallel", "arbitrary")),
        name="peer_experts",
    )(xb, u, gm, v)


def _token_mixer(x, mem, w_in, diff_lambda, diff_subln, cmp_pe, cmp_w1, cmp_w2, w_mem_kv, w_branch, rel_bias):
    b, t, _ = x.shape
    n = b * t
    L = ATT_TILE

    w_a, w_mg, w_g = _w_prep(w_in.T)
    ha, hg, xb = _in_proj(x.reshape(n, D_MODEL), w_a, w_g)
    ha3 = ha.reshape(b, t, ha.shape[1])

    tab1d = rel_bias[_rel_bucket(jnp.arange(t))].T * LOG2E
    n_far = min(t // L, REL_MAX_DIST // L + 2)
    bias_a = _bias_tiles(tab1d[:A_HEADS], t, -1, n_far + 1, t, 1)
    bias_b = tab1d[A_HEADS:]
    bias_sel = _bias_tiles(bias_b, t, -SEL_NQ, n_far + SEL_NQ, t, B_HPG)
    n_win = min(t // L, WINDOW // L + 1)
    bias_win = _bias_tiles(bias_b, t, -1, n_win + 1, WINDOW, B_HPG)

    def values_t(v):
        heads, dv = v.shape[2:]
        return v.reshape(b, t // L, L, heads, dv).transpose(0, 3, 1, 4, 2)

    o_a = _diff_attention(ha3, values_t(ha3[:, :, 3072:4096].reshape(b, t, A_HEADS, A_DV)), bias_a, diff_lambda,
                          jnp.broadcast_to(diff_subln[:, None], (A_DV, LANES)), b, t)

    glog_t = hg[:, :48].reshape(b, t, 3, B_GROUPS, B_HPG).transpose(2, 0, 3, 4, 1)
    nr = t // CMP_STRIDE
    w1 = cmp_w1.reshape(2, CMP_LEN, B_DH, CMP_HIDDEN)
    z1 = jnp.zeros_like(w1[0])
    w1_bd = jnp.concatenate([jnp.concatenate([w1[0], z1], axis=2), jnp.concatenate([z1, w1[1]], axis=2)],
                            axis=1).astype(BF16)
    z2 = jnp.zeros_like(cmp_w2[0])
    w2_bd = jnp.concatenate([jnp.concatenate([cmp_w2[0], z2], axis=1), jnp.concatenate([z2, cmp_w2[1]], axis=1)],
                            axis=0).astype(BF16)
    pe_bd = jnp.broadcast_to(jnp.concatenate([cmp_pe[0], cmp_pe[1]], axis=1)[:, None, :],
                             (CMP_LEN, 8, 2 * B_DH)).astype(BF16)
    kvc = _compress(ha3[:, :, 5120:5632].astype(F32), w1_bd, w2_bd, pe_bd)
    n_c = (t - CMP_LEN) // CMP_STRIDE + 1
    n_sel = t // SEL_BLOCK
    cidx = np.arange(nr)[:, None] * CMP_STRIDE + np.arange(CMP_LEN)[None, :]
    overlap = (cidx[:, :, None] // SEL_BLOCK == np.arange(n_sel)[None, None, :]).astype(np.float32).mean(axis=1)
    overlap[n_c:] = 0.0
    overlap_t = np.zeros((LANES, nr), np.float32)
    overlap_t[:n_sel] = overlap.T
    o_c, sel_mask = _nsa_compressed(ha3, kvc, jnp.asarray(overlap_t, BF16), glog_t[0], b, t, n_c,
                                    min(SEL_TOPN, n_sel))
    v_sel = ha3[:, :, 5632:6144].reshape(b, t, B_GROUPS, 2, B_DH)[:, :, :, 1]
    v_win = ha3[:, :, 6144:6656].reshape(b, t, B_GROUPS, 2, B_DH)[:, :, :, 1]
    assert n_sel <= B_DH
    hot = np.zeros((t, LANES), np.float32)
    hot[np.arange(t), B_DH + np.arange(t) // SEL_BLOCK] = 1.0
    o_cs = _nsa_branch(ha3, values_t(v_sel), bias_sel, glog_t[1], o_c, sel_mask, jnp.asarray(hot, BF16), b, t,
                       "sel", SEL_NQ, F32)
    o_b = _nsa_branch(ha3, values_t(v_win), bias_win, glog_t[2], o_cs, None, None, b, t, "win", 1, BF16)

    n_mem = mem.shape[1]
    memkv = _matmul(mem.reshape(b * n_mem, D_MODEL).astype(BF16), w_mem_kv.astype(BF16), BF16,
                    b * n_mem, 1024, "mem_kv_proj")
    o_m = _memory_attention(ha3, memkv, b, t, n_mem)

    flat = lambda o: o.reshape(n, MIX_W)
    return _merge(xb, w_mg, flat(o_a), flat(o_b), flat(o_m), w_branch.astype(BF16))


def _peer(x1, x1b, peer_wq, peer_keys, peer_u, peer_v):
    n = x1.shape[0]
    q = _matmul(x1b, peer_wq.astype(BF16), BF16, 1024, 1024, "peer_query")
    zk = jnp.zeros((PEER_HEADS, PEER_NKEYS, PEER_DKEY // 2), F32)
    keys_bd = jnp.concatenate([jnp.concatenate([peer_keys[:, 0], zk], axis=2),
                               jnp.concatenate([zk, peer_keys[:, 1]], axis=2)], axis=1).astype(BF16)
    ei, ej, gate = _peer_route(q, keys_bd)
    slots = lambda a: a.reshape(PEER_HEADS * PEER_TOPK, n).T.reshape(n, 1, PEER_HEADS * PEER_TOPK)
    gm = _gate_matrix(slots(ei), slots(ej), slots(gate))
    return _peer_experts(x1b, peer_u, gm, peer_v)


def kernel(x, mem, w_in, diff_lambda, diff_subln, cmp_pe, cmp_w1, cmp_w2, w_mem_kv, w_branch, w_out, ln1_g, ln1_b,
           peer_wq, peer_keys, peer_u, peer_v, ln2_g, ln2_b, rel_bias):
    b, t, _ = x.shape
    n = b * t
    for l in range(DEPTH):
        mixed = _token_mixer(x, mem, w_in[l], diff_lambda[l], diff_subln[l], cmp_pe[l], cmp_w1[l], cmp_w2[l],
                             w_mem_kv[l], w_branch[l], rel_bias)
        x1, x1b = _out_proj_ln(mixed, w_out[l].astype(BF16), x.reshape(n, D_MODEL),
                               ln1_g[l].reshape(1, D_MODEL), ln1_b[l].reshape(1, D_MODEL))
        y = _peer(x1, x1b, peer_wq[l], peer_keys[l], peer_u[l], peer_v[l])
        x = _residual_ln(x1, y, ln2_g[l].reshape(1, D_MODEL), ln2_b[l].reshape(1, D_MODEL)).reshape(b, t, D_MODEL)
    return x
```

```python
import functools
import math

import numpy as np
import jax
import jax.numpy as jnp
from jax import lax
from jax.experimental import pallas as pl
from jax.experimental.pallas import tpu as pltpu

F32 = jnp.float32
BF16 = jnp.bfloat16

D_MODEL = 2048
A_HEADS, A_DQK, A_DV = 8, 64, 128
B_HEADS, B_GROUPS, B_HPG, B_DH = 16, 4, 4, 64
CMP_LEN, CMP_STRIDE, CMP_HIDDEN = 32, 16, 256
SEL_BLOCK, SEL_TOPN, WINDOW = 64, 16, 512
M_HEADS, M_DH = 4, 256
REL_BUCKETS, REL_MAX_DIST = 32, 1024
PEER_HEADS, PEER_NKEYS, PEER_DKEY, PEER_TOPK = 8, 128, 256, 16
MIX_W = 1024
LN_EPS = 1e-5
FORCE = 1e9
DEPTH = 1
ALPHA = (2 * DEPTH) ** 0.25
LAMBDA_INIT = 0.8 - 0.6 * math.exp(-0.3 * 0)
LOG2E = math.log2(math.e)

NEG = -1e30
ATT_TILE = 256
LANES = 128
ONES_ROWS = 16
N_MERGE = 3
SEL_NQ = 1
VMEM_LIMIT = 56 * 1024 * 1024


def _cparams(sem):
    return pltpu.CompilerParams(dimension_semantics=sem, vmem_limit_bytes=VMEM_LIMIT)


def _dot_nt(a, b):
    return lax.dot_general(a, b, (((1,), (1,)), ((), ())), preferred_element_type=F32)


def _mm_body(x_ref, w_ref, o_ref):
    o_ref[...] = jnp.dot(x_ref[...], w_ref[...], preferred_element_type=F32).astype(o_ref.dtype)


def _mm_nt_body(x_ref, w_ref, o_ref):
    o_ref[...] = _dot_nt(x_ref[...], w_ref[...]).astype(o_ref.dtype)


def _matmul(x, w, out_dtype, tm, tn, name, w_transposed=False):
    m, k = x.shape
    n = w.shape[0] if w_transposed else w.shape[1]
    w_spec = pl.BlockSpec((tn, k), lambda i, j: (j, 0)) if w_transposed else pl.BlockSpec((k, tn), lambda i, j: (0, j))
    return pl.pallas_call(
        _mm_nt_body if w_transposed else _mm_body,
        grid=(m // tm, n // tn),
        in_specs=[pl.BlockSpec((tm, k), lambda i, j: (i, 0)), w_spec],
        out_specs=pl.BlockSpec((tm, tn), lambda i, j: (i, j)),
        out_shape=jax.ShapeDtypeStruct((m, n), out_dtype),
        compiler_params=_cparams(("parallel", "arbitrary")),
        name=name,
    )(x, w)


def _in_proj_body(x_ref, w_ref, wg_ref, o_ref, og_ref, xb_ref):
    xb = x_ref[...].astype(BF16)
    o_ref[...] = _dot_nt(xb, w_ref[...]).astype(o_ref.dtype)

    @pl.when(pl.program_id(1) == 0)
    def _():
        xb_ref[...] = xb
        og_ref[...] = _dot_nt(xb, wg_ref[...])


def _in_proj(x, w_a, w_g):
    n, k = x.shape
    tm, tn = 1024, 1664
    once = dict(pipeline_mode=pl.Buffered(1))
    return pl.pallas_call(
        _in_proj_body,
        grid=(n // tm, W_A_COLS // tn),
        in_specs=[pl.BlockSpec((tm, k), lambda i, j: (i, 0), **once),
                  pl.BlockSpec((tn, k), lambda i, j: (j, 0)),
                  pl.BlockSpec((W_G_COLS, k), lambda i, j: (0, 0), **once)],
        out_specs=[pl.BlockSpec((tm, tn), lambda i, j: (i, j)),
                   pl.BlockSpec((tm, W_G_COLS), lambda i, j: (i, 0), **once),
                   pl.BlockSpec((tm, k), lambda i, j: (i, 0), **once)],
        out_shape=[jax.ShapeDtypeStruct((n, W_A_COLS), BF16), jax.ShapeDtypeStruct((n, W_G_COLS), F32),
                   jax.ShapeDtypeStruct((n, k), BF16)],
        compiler_params=_cparams(("parallel", "arbitrary")),
        name="in_proj",
    )(x, w_a, w_g)


W_A_COLS = 6656
W_G_COLS = 128


def _w_prep_body(w_ref, wa_ref, wm_ref, wg_ref):
    def put(dst_ref, d0, s0, n, c=1.0):
        val = w_ref[s0:s0 + n, :]
        dst_ref[d0:d0 + n, :] = (val * c if c != 1.0 else val).astype(dst_ref.dtype)

    put(wa_ref, 0, 5680, 1024, M_DH ** -0.5 * LOG2E)
    put(wa_ref, 1024, 0, 1024, A_DQK ** -0.5 * LOG2E)
    put(wa_ref, 2048, 1024, 2048)
    put(wa_ref, 4096, 3072, 1024, B_DH ** -0.5 * LOG2E)
    for br in range(3):
        for g in range(B_GROUPS):
            src = 4096 + br * 2 * B_GROUPS * B_DH + g * B_DH
            dst = 5120 + (br * B_GROUPS + g) * 2 * B_DH
            put(wa_ref, dst, src, B_DH)
            put(wa_ref, dst + B_DH, src + B_GROUPS * B_DH, B_DH)
    put(wm_ref, 0, 6704, N_MERGE * D_MODEL)
    put(wg_ref, 0, 5632, 48)
    wg_ref[48:, :] = jnp.zeros((W_G_COLS - 48, wg_ref.shape[1]), wg_ref.dtype)


def _w_prep(w_in_t):
    c, k = w_in_t.shape
    tk = 256
    col = lambda i: (0, i)
    return pl.pallas_call(
        _w_prep_body,
        grid=(k // tk,),
        in_specs=[pl.BlockSpec((c, tk), col)],
        out_specs=[pl.BlockSpec((W_A_COLS, tk), col), pl.BlockSpec((N_MERGE * D_MODEL, tk), col),
                   pl.BlockSpec((W_G_COLS, tk), col)],
        out_shape=[jax.ShapeDtypeStruct((W_A_COLS, k), BF16), jax.ShapeDtypeStruct((N_MERGE * D_MODEL, k), BF16),
                   jax.ShapeDtypeStruct((W_G_COLS, k), BF16)],
        compiler_params=_cparams(("parallel",)),
        name="w_in_regroup",
    )(w_in_t)


def _rel_bucket(dist):
    n = jnp.maximum(dist, 0)
    max_exact = REL_BUCKETS // 2
    nf = jnp.maximum(n, 1).astype(jnp.float32)
    large = max_exact + (jnp.log(nf / max_exact) / math.log(REL_MAX_DIST / max_exact)
                         * (REL_BUCKETS - max_exact)).astype(jnp.int32)
    large = jnp.minimum(large, REL_BUCKETS - 1)
    return jnp.where(n < max_exact, n, large)


def _bias_tiles(tab1d, t, first, n_tiles, max_dist, hpr):
    L = ATT_TILE
    m = np.arange(2 * L)
    off = np.where(m <= L, m, m - 2 * L)
    d = (first + np.arange(n_tiles))[:, None] * L + off[None, :]
    ok = (d >= 0) & (d < max_dist)
    h = tab1d.shape[0]
    rp = jnp.where(ok[:, None], tab1d.T[np.clip(d, 0, t - 1)].transpose(0, 2, 1), NEG)
    return pl.pallas_call(
        functools.partial(_toeplitz_body, hpr=hpr),
        grid=(n_tiles,),
        in_specs=[pl.BlockSpec((None, h, 1, 2 * L), lambda c: (c, 0, 0, 0))],
        out_specs=pl.BlockSpec((None, h // hpr, L, hpr * L), lambda c: (c, 0, 0, 0)),
        out_shape=jax.ShapeDtypeStruct((n_tiles, h // hpr, L, hpr * L), F32),
        compiler_params=_cparams(("parallel",)),
        name="bias_tiles",
    )(rp.reshape(n_tiles, h, 1, 2 * L))


def _toeplitz_body(rp_ref, o_ref, *, hpr):
    L = o_ref.shape[1]
    for hd in range(rp_ref.shape[0]):
        rows = jnp.broadcast_to(rp_ref[hd], (L, 2 * L))
        tile = pltpu.roll(rows, 0, 1, stride=1, stride_axis=0)[:, :L]
        o_ref[hd // hpr, :, (hd % hpr) * L:(hd % hpr + 1) * L] = tile


def _flash_loop(lo, hi, scores, values, m_ref, acc_ref, s_ref, mx_ref):
    def ahead(kt, slot):
        s = scores(kt)
        s_ref[slot] = s
        mx_ref[slot] = jnp.broadcast_to(jnp.max(s, axis=0, keepdims=True), mx_ref.shape[1:])

    def finish(kt, slot):
        m_prev = m_ref[...]
        m_new = jnp.maximum(m_prev, mx_ref[slot])
        alpha = jnp.exp2(m_prev - m_new)
        p = jnp.exp2(s_ref[slot] - m_new[0:1])
        acc_ref[...] = alpha[0:1] * acc_ref[...] + jnp.dot(values(kt), p.astype(BF16), preferred_element_type=F32)
        m_ref[...] = m_new

    def body(i, carry):
        kt = lo + 2 * i
        ahead(kt + 1, 1)
        finish(kt, 0)
        ahead(kt + 2, 0)
        finish(kt + 1, 1)
        return carry

    ahead(lo, 0)
    lax.fori_loop(0, (hi - lo + 1) // 2, body, 0)


def _diff_body(q_ref, k_ref, vt_ref, bias_ref, lam_ref, g_ref, o_ref, m_sc, acc_sc, s_sc, mx_sc, *, n_bt):
    L = ATT_TILE
    tq = 2 * L
    last = vt_ref.shape[0] - 1
    qi = pl.program_id(2)
    m_sc[...] = jnp.full(m_sc.shape, NEG, F32)
    acc_sc[...] = jnp.zeros(acc_sc.shape, F32)
    qt = q_ref[...].astype(F32).T.astype(BF16)
    zero = jnp.zeros((A_DQK, tq), BF16)
    q_cat = jnp.concatenate([jnp.concatenate([qt[:A_DQK], zero], axis=0),
                             jnp.concatenate([zero, qt[A_DQK:]], axis=0)], axis=1)

    def scores(kt):
        off = pl.multiple_of(jnp.minimum(kt, last) * L, L)
        k = k_ref[pl.ds(off, L), :]
        d0 = 2 * qi - kt
        bias = jnp.concatenate([bias_ref[jnp.clip(d0 + 1, 0, n_bt - 1)],
                                bias_ref[jnp.clip(d0 + 2, 0, n_bt - 1)]], axis=1)
        s = jnp.dot(k, q_cat, preferred_element_type=F32)
        return jnp.concatenate([s[:, :tq] + bias, s[:, tq:] + bias], axis=1)

    ones = jnp.ones((ONES_ROWS, L), BF16)

    def values(kt):
        return jnp.concatenate([vt_ref[jnp.minimum(kt, last)], ones], axis=0)

    _flash_loop(0, 2 * qi + 2, scores, values, m_sc, acc_sc, s_sc, mx_sc)

    lp = lam_ref[...]
    lam = (jnp.exp(jnp.sum(lp[0:1] * lp[1:2], axis=1, keepdims=True))
           - jnp.exp(jnp.sum(lp[2:3] * lp[3:4], axis=1, keepdims=True)) + LAMBDA_INIT)
    acc = acc_sc[...]
    o0 = acc[:A_DV, :tq] / jnp.maximum(acc[A_DV:A_DV + 1, :tq], 1e-30)
    o1 = acc[:A_DV, tq:] / jnp.maximum(acc[A_DV:A_DV + 1, tq:], 1e-30)
    o = o0 - lam * o1
    g = jnp.concatenate([g_ref[...]] * (tq // LANES), axis=1)
    o = o * lax.rsqrt(jnp.mean(o * o, axis=0, keepdims=True) + LN_EPS) * g
    o_ref[...] = (o * (1.0 - LAMBDA_INIT)).T.astype(o_ref.dtype)


def _diff_attention(ha, v_t, bias, lam_params, subln, b, t):
    L = ATT_TILE
    tq = 2 * L
    n_bt = bias.shape[0]
    return pl.pallas_call(
        functools.partial(_diff_body, n_bt=n_bt),
        grid=(b, A_HEADS, t // tq),
        in_specs=[
            pl.BlockSpec((None, tq, 128), lambda bi, h, qi: (bi, qi, 8 + h)),
            pl.BlockSpec((None, t, 128), lambda bi, h, qi: (bi, 0, 16 + h)),
            pl.BlockSpec((None, None, t // L, A_DV, L), lambda bi, h, qi: (bi, h, 0, 0, 0)),
            pl.BlockSpec((n_bt, None, L, L), lambda bi, h, qi: (0, h, 0, 0)),
            pl.BlockSpec((4, A_DQK), lambda bi, h, qi: (0, 0)),
            pl.BlockSpec((A_DV, LANES), lambda bi, h, qi: (0, 0)),
        ],
        out_specs=pl.BlockSpec((None, tq, 128), lambda bi, h, qi: (bi, qi, h)),
        out_shape=jax.ShapeDtypeStruct((b, t, MIX_W), BF16),
        scratch_shapes=[pltpu.VMEM((8, 2 * tq), F32), pltpu.VMEM((A_DV + ONES_ROWS, 2 * tq), F32),
                        pltpu.VMEM((2, L, 2 * tq), F32), pltpu.VMEM((2, 8, 2 * tq), F32)],
        compiler_params=_cparams(("parallel", "parallel", "arbitrary")),
        name="diff_attention",
    )(ha, ha, v_t, bias, lam_params, subln)


def _compress_body(kv_ref, w1_ref, w2_ref, pe_ref, o_ref):
    nr = o_ref.shape[0]
    first = jnp.zeros((nr, 2 * CMP_HIDDEN), F32)
    second = jnp.zeros((nr, 2 * CMP_HIDDEN), F32)
    for l in range(CMP_STRIDE):
        rows = kv_ref[pl.ds(l, nr, stride=CMP_STRIDE), :].astype(BF16)
        first = first + jnp.dot(rows, w1_ref[l], preferred_element_type=F32)
        second = second + jnp.dot(rows, w1_ref[CMP_STRIDE + l], preferred_element_type=F32)
    pw = jnp.zeros((8, 2 * CMP_HIDDEN), F32)
    for l in range(CMP_LEN):
        pw = pw + jnp.dot(pe_ref[l], w1_ref[l], preferred_element_type=F32)
    second = jnp.concatenate([second[1:], second[:1]], axis=0)
    hdn = jax.nn.gelu(first + second + pw[0:1])
    o_ref[...] = jnp.dot(hdn.astype(BF16), w2_ref[...], preferred_element_type=F32)


def _compress(kv, w1_bd, w2_bd, pe_bd):
    b, t, _ = kv.shape
    nr = t // CMP_STRIDE
    return pl.pallas_call(
        _compress_body,
        grid=(b, B_GROUPS),
        in_specs=[
            pl.BlockSpec((None, t, LANES), lambda bi, gi: (bi, 0, gi)),
            pl.BlockSpec(w1_bd.shape, lambda bi, gi: (0, 0, 0)),
            pl.BlockSpec(w2_bd.shape, lambda bi, gi: (0, 0)),
            pl.BlockSpec(pe_bd.shape, lambda bi, gi: (0, 0, 0)),
        ],
        out_specs=pl.BlockSpec((None, None, nr, 2 * B_DH), lambda bi, gi: (bi, gi, 0, 0)),
        out_shape=jax.ShapeDtypeStruct((b, B_GROUPS, nr, 2 * B_DH), F32),
        compiler_params=_cparams(("parallel", "parallel")),
        name="nsa_compress",
    )(kv, w1_bd, w2_bd, pe_bd)


def _topk_mask_axis0(v, k):
    r, n = v.shape
    iota = lax.broadcasted_iota(jnp.int32, (r, n), 0).astype(F32)

    def body(_, c):
        v, sel = c
        mx = jnp.max(v, axis=0, keepdims=True)
        idx = jnp.min(jnp.where(v == mx, iota, float(r)), axis=0, keepdims=True)
        hit = iota == idx
        return jnp.where(hit, -jnp.inf, v), jnp.where(hit, 1.0, sel)

    _, sel = lax.fori_loop(0, k, body, (v, jnp.zeros((r, n), F32)))
    return sel


def _cmp_body(q_ref, kvc_ref, ov_ref, gl_ref, o_ref, mask_ref, *, n_c, n_top):
    L = ATT_TILE
    qi = pl.program_id(2)
    ncp = kvc_ref.shape[0]
    n_sel = mask_ref.shape[0]
    qt = q_ref[...].astype(F32).T.astype(BF16)
    qh = jnp.concatenate([qt[h * B_DH:(h + 1) * B_DH] for h in range(B_HPG)], axis=1)
    kvc = kvc_ref[...]
    s = jnp.dot(kvc[:, :B_DH].astype(BF16), qh, preferred_element_type=F32)
    tcol = qi * L + lax.broadcasted_iota(jnp.int32, (1, L), 1)
    crow = lax.broadcasted_iota(jnp.int32, (ncp, 1), 0)
    seen = jnp.where(crow < n_c, crow * CMP_STRIDE + (CMP_LEN - 1), jnp.int32(2 ** 30)) <= tcol
    valid = jnp.concatenate([seen] * B_HPG, axis=1)
    s = jnp.where(valid, s, NEG)
    mx = jnp.max(s, axis=0, keepdims=True)
    e = jnp.where(valid, jnp.exp2(s - mx), 0.0)
    p = e / jnp.maximum(jnp.sum(e, axis=0, keepdims=True), 1e-30)
    o = jnp.dot(kvc.T[B_DH:].astype(BF16), p.astype(BF16), preferred_element_type=F32)
    gate = jax.nn.sigmoid(gl_ref[...])
    o = jnp.concatenate([o[:, h * L:(h + 1) * L] * gate[h:h + 1] for h in range(B_HPG)], axis=0)
    o_ref[...] = o.T

    psum = p[:, :L] + p[:, L:2 * L] + p[:, 2 * L:3 * L] + p[:, 3 * L:]
    ov_t = ov_ref[...]
    imp = jnp.zeros((ov_t.shape[0], L), F32)
    rem = psum
    for _ in range(3):
        part = rem.astype(BF16)
        imp = imp + jnp.dot(ov_t, part, preferred_element_type=F32)
        rem = rem - part.astype(F32)
    imp = imp[:n_sel]
    blk = lax.broadcasted_iota(jnp.int32, (n_sel, 1), 0)
    cur = jnp.right_shift(tcol, int(math.log2(SEL_BLOCK)))
    imp = jnp.where(blk * SEL_BLOCK > tcol, -FORCE, imp)
    imp = jnp.where(blk == 0, FORCE, imp)
    imp = jnp.where(blk == cur, FORCE, imp)
    imp = jnp.where(blk == cur - 1, FORCE, imp)
    mask_ref[...] = jnp.where(_topk_mask_axis0(imp, n_top) > 0.0, 0.0, NEG).astype(mask_ref.dtype)


def _nsa_compressed(ha, kvc, overlap_t, glog_t, b, t, n_c, n_top):
    L = ATT_TILE
    ncp = kvc.shape[2]
    n_sel = t // SEL_BLOCK
    return pl.pallas_call(
        functools.partial(_cmp_body, n_c=n_c, n_top=n_top),
        grid=(b, B_GROUPS, t // L),
        in_specs=[
            pl.BlockSpec((None, L, 256), lambda bi, g, qi: (bi, qi, 16 + g)),
            pl.BlockSpec((None, None, ncp, 2 * B_DH), lambda bi, g, qi: (bi, g, 0, 0)),
            pl.BlockSpec(overlap_t.shape, lambda bi, g, qi: (0, 0)),
            pl.BlockSpec((None, None, B_HPG, L), lambda bi, g, qi: (bi, g, 0, qi)),
        ],
        out_specs=[
            pl.BlockSpec((None, L, 256), lambda bi, g, qi: (bi, qi, g)),
            pl.BlockSpec((None, None, n_sel, L), lambda bi, g, qi: (bi, g, 0, qi)),
        ],
        out_shape=[jax.ShapeDtypeStruct((b, t, MIX_W), F32),
                   jax.ShapeDtypeStruct((b, B_GROUPS, n_sel, t), BF16)],
        compiler_params=_cparams(("parallel", "parallel", "arbitrary")),
        name="nsa_compressed_select",
    )(ha, kvc, overlap_t, glog_t)


def _nsa_body(*refs, mode, n_bt):
    if mode == "sel":
        q_ref, kv_ref, vt_ref, bias_ref, gl_ref, prev_ref, mask_ref, hot_ref, o_ref, m_sc, acc_sc, s_sc, mx_sc = refs
    else:
        q_ref, kv_ref, vt_ref, bias_ref, gl_ref, prev_ref, o_ref, m_sc, acc_sc, s_sc, mx_sc = refs
    L = ATT_TILE
    nq = q_ref.shape[0] // L
    qi = pl.program_id(2)
    m_sc[...] = jnp.full(m_sc.shape, NEG, F32)
    acc_sc[...] = jnp.zeros(acc_sc.shape, F32)
    qt = q_ref[...].astype(F32).T.astype(BF16)
    if mode == "sel":
        n_sel = mask_ref.shape[0]
        qm = mask_ref[...]
        if n_sel < B_DH:
            qm = jnp.concatenate([qm, jnp.zeros((B_DH - n_sel, nq * L), BF16)], axis=0)
    else:
        qm = jnp.zeros((B_DH, nq * L), BF16)
    q_aug_t = jnp.concatenate(
        [jnp.concatenate([qt[h * B_DH:(h + 1) * B_DH, s * L:(s + 1) * L], qm[:, s * L:(s + 1) * L]], axis=0)
         for s in range(nq) for h in range(B_HPG)], axis=1)

    last = vt_ref.shape[0] - 1
    ones = jnp.ones((ONES_ROWS, L), BF16)
    k_lanes = lax.broadcasted_iota(jnp.int32, (L, LANES), 1) < B_DH

    def scores(kt):
        off = pl.multiple_of(jnp.minimum(kt, last) * L, L)
        idx = nq * qi - kt + nq
        bias = [bias_ref[jnp.clip(idx + s, 0, n_bt - 1)] for s in range(nq)]
        bias = bias[0] if nq == 1 else jnp.concatenate(bias, axis=1)
        k_aug = kv_ref[pl.ds(off, L), :]
        if mode == "sel":
            k_aug = jnp.where(k_lanes, k_aug, hot_ref[pl.ds(off, L), :])
        return jnp.dot(k_aug, q_aug_t, preferred_element_type=F32) + bias

    def values(kt):
        return jnp.concatenate([vt_ref[jnp.minimum(kt, last)], ones], axis=0)

    lo = 0 if mode == "sel" else jnp.maximum(nq * qi - (n_bt - nq - 1), 0)
    _flash_loop(lo, nq * qi + nq, scores, values, m_sc, acc_sc, s_sc, mx_sc)
    acc = acc_sc[...]
    o = acc[:B_DH] / jnp.maximum(acc[B_DH:B_DH + 1], 1e-30)
    gate = jax.nn.sigmoid(gl_ref[...])
    heads = []
    for h in range(B_HPG):
        cols = [o[:, (s * B_HPG + h) * L:(s * B_HPG + h + 1) * L] for s in range(nq)]
        heads.append((cols[0] if nq == 1 else jnp.concatenate(cols, axis=1)) * gate[h:h + 1])
    o_ref[...] = (prev_ref[...] + jnp.concatenate(heads, axis=0).T).astype(o_ref.dtype)


def _nsa_branch(ha, v_t, bias, glog_t, prev, mask_t, hot, b, t, mode, nq, out_dtype):
    L = ATT_TILE
    tq = nq * L
    n_bt = bias.shape[0]
    branch = 1 if mode == "sel" else 2
    in_specs = [
        pl.BlockSpec((None, tq, 256), lambda bi, g, qi: (bi, qi, 16 + g)),
        pl.BlockSpec((None, t, LANES), lambda bi, g, qi: (bi, 0, 40 + 4 * branch + g)),
        pl.BlockSpec((None, None, t // L, B_DH, L), lambda bi, g, qi: (bi, g, 0, 0, 0)),
        pl.BlockSpec((n_bt, None, L, B_HPG * L), lambda bi, g, qi: (0, g, 0, 0)),
        pl.BlockSpec((None, None, B_HPG, tq), lambda bi, g, qi: (bi, g, 0, qi)),
        pl.BlockSpec((None, tq, 256), lambda bi, g, qi: (bi, qi, g)),
    ]
    args = [ha, ha, v_t, bias, glog_t, prev]
    if mode == "sel":
        n_sel = mask_t.shape[2]
        in_specs.append(pl.BlockSpec((None, None, n_sel, tq), lambda bi, g, qi: (bi, g, 0, qi)))
        in_specs.append(pl.BlockSpec((t, LANES), lambda bi, g, qi: (0, 0)))
        args += [mask_t, hot]
    return pl.pallas_call(
        functools.partial(_nsa_body, mode=mode, n_bt=n_bt),
        grid=(b, B_GROUPS, t // tq),
        in_specs=in_specs,
        out_specs=pl.BlockSpec((None, tq, 256), lambda bi, g, qi: (bi, qi, g)),
        out_shape=jax.ShapeDtypeStruct((b, t, MIX_W), out_dtype),
        scratch_shapes=[pltpu.VMEM((8, B_HPG * tq), F32), pltpu.VMEM((B_DH + ONES_ROWS, B_HPG * tq), F32),
                        pltpu.VMEM((2, L, B_HPG * tq), F32), pltpu.VMEM((2, 8, B_HPG * tq), F32)],
        compiler_params=_cparams(("parallel", "parallel", "arbitrary")),
        name="nsa_" + mode,
    )(*args)


def _mem_body(q_ref, k_ref, v_ref, o_ref):
    q = q_ref[...]
    outs = []
    for h in range(M_HEADS):
        sl = slice(h * M_DH, (h + 1) * M_DH)
        s = _dot_nt(q[:, sl], k_ref[:, sl])
        e = jnp.exp2(s - jnp.max(s, axis=1, keepdims=True))
        p = e / jnp.sum(e, axis=1, keepdims=True)
        outs.append(jnp.dot(p.astype(BF16), v_ref[:, sl], preferred_element_type=F32))
    o_ref[...] = jnp.concatenate(outs, axis=1).astype(o_ref.dtype)


def _memory_attention(ha, memkv, b, t, n_mem):
    tq = 512
    w = M_HEADS * M_DH
    return pl.pallas_call(
        _mem_body,
        grid=(b, t // tq),
        in_specs=[
            pl.BlockSpec((None, tq, w), lambda bi, qi: (bi, qi, 0)),
            pl.BlockSpec((n_mem, w), lambda bi, qi: (bi, 0)),
            pl.BlockSpec((n_mem, w), lambda bi, qi: (bi, 1)),
        ],
        out_specs=pl.BlockSpec((None, tq, w), lambda bi, qi: (bi, qi, 0)),
        out_shape=jax.ShapeDtypeStruct((b, t, w), BF16),
        compiler_params=_cparams(("parallel", "arbitrary")),
        name="memory_attention",
    )(ha, memkv, memkv)


def _merge_body(x_ref, wg_ref, oa_ref, ob_ref, om_ref, wb_ref, o_ref, acc_sc):
    n = pl.program_id(1)

    @pl.when(n == 0)
    def _():
        acc_sc[...] = jnp.zeros(acc_sc.shape, F32)

    gate = jax.nn.sigmoid(_dot_nt(x_ref[...], wg_ref[...]))
    branch = jnp.where(n == 0, oa_ref[...], jnp.where(n == 1, ob_ref[...], om_ref[...]))
    acc = acc_sc[...] + gate * jnp.dot(branch, wb_ref[...], preferred_element_type=F32)
    acc_sc[...] = acc
    o_ref[...] = acc.astype(o_ref.dtype)


def _merge(xb, w_gate_t, o_a, o_b, o_m, w_branch):
    n = xb.shape[0]
    tm = 512
    row = lambda i, j: (i, 0)
    return pl.pallas_call(
        _merge_body,
        grid=(n // tm, N_MERGE),
        in_specs=[pl.BlockSpec((tm, D_MODEL), row),
                  pl.BlockSpec((D_MODEL, D_MODEL), lambda i, j: (j, 0))]
                 + [pl.BlockSpec((tm, MIX_W), row)] * 3
                 + [pl.BlockSpec((None, MIX_W, D_MODEL), lambda i, j: (j, 0, 0))],
        out_specs=pl.BlockSpec((tm, D_MODEL), row),
        out_shape=jax.ShapeDtypeStruct((n, D_MODEL), BF16),
        scratch_shapes=[pltpu.VMEM((tm, D_MODEL), F32)],
        compiler_params=_cparams(("parallel", "arbitrary")),
        name="branch_merge",
    )(xb, w_gate_t, o_a, o_b, o_m, w_branch)


def _layer_norm(z, g, b):
    mu = jnp.mean(z, axis=1, keepdims=True)
    zc = z - mu
    var = jnp.mean(zc * zc, axis=1, keepdims=True)
    return zc * lax.rsqrt(var + LN_EPS) * g + b


def _out_ln_body(y_ref, w_ref, x_ref, g_ref, b_ref, o_ref, ob_ref):
    y = jnp.dot(y_ref[...], w_ref[...], preferred_element_type=F32)
    o = _layer_norm(ALPHA * x_ref[...] + y, g_ref[...], b_ref[...])
    o_ref[...] = o
    ob_ref[...] = o.astype(BF16)


def _out_proj_ln(mixed, w_out, x, g, b):
    n = x.shape[0]
    tm = 512
    row = lambda i: (i, 0)
    const = lambda i: (0, 0)
    return pl.pallas_call(
        _out_ln_body,
        grid=(n // tm,),
        in_specs=[pl.BlockSpec((tm, D_MODEL), row), pl.BlockSpec((D_MODEL, D_MODEL), const),
                  pl.BlockSpec((tm, D_MODEL), row), pl.BlockSpec((1, D_MODEL), const),
                  pl.BlockSpec((1, D_MODEL), const)],
        out_specs=[pl.BlockSpec((tm, D_MODEL), row), pl.BlockSpec((tm, D_MODEL), row)],
        out_shape=[jax.ShapeDtypeStruct((n, D_MODEL), F32), jax.ShapeDtypeStruct((n, D_MODEL), BF16)],
        compiler_params=_cparams(("parallel",)),
        name="out_proj_ln1",
    )(mixed, w_out, x, g, b)


def _res_ln_body(x_ref, y_ref, g_ref, b_ref, o_ref):
    o_ref[...] = _layer_norm(ALPHA * x_ref[...] + y_ref[...], g_ref[...], b_ref[...])


def _residual_ln(x, y, g, b):
    n = x.shape[0]
    tm = 512
    row = lambda i: (i, 0)
    const = lambda i: (0, 0)
    return pl.pallas_call(
        _res_ln_body,
        grid=(n // tm,),
        in_specs=[pl.BlockSpec((tm, D_MODEL), row), pl.BlockSpec((tm, D_MODEL), row),
                  pl.BlockSpec((1, D_MODEL), const), pl.BlockSpec((1, D_MODEL), const)],
        out_specs=pl.BlockSpec((tm, D_MODEL), row),
        out_shape=jax.ShapeDtypeStruct((n, D_MODEL), F32),
        compiler_params=_cparams(("parallel",)),
        name="residual_ln2",
    )(x, y, g, b)


def _topk_axis0(v, k):
    r, n = v.shape
    iota = lax.broadcasted_iota(jnp.int32, (r, n), 0).astype(F32)
    slot = lax.broadcasted_iota(jnp.int32, (k, n), 0)

    def body(it, c):
        v, vals, idxs = c
        mx = jnp.max(v, axis=0, keepdims=True)
        idx = jnp.min(jnp.where(v == mx, iota, float(r)), axis=0, keepdims=True)
        v = jnp.where(iota == idx, -jnp.inf, v)
        return v, jnp.where(slot == it, mx, vals), jnp.where(slot == it, idx, idxs)

    _, vals, idxs = lax.fori_loop(0, k, body, (v, jnp.zeros((k, n), F32), jnp.zeros((k, n), F32)))
    return vals, idxs


def _pick_rows(table, pos, k):
    out = jnp.zeros(pos.shape, F32)
    for a in range(k):
        out = out + jnp.where(pos == float(a), table[a:a + 1], 0.0)
    return out


def _route_body(q_ref, keys_ref, ei_ref, ej_ref, g_ref):
    k = PEER_TOPK
    scores = _dot_nt(keys_ref[...], q_ref[...])
    v0, i0 = _topk_axis0(scores[:PEER_NKEYS], k)
    v1, i1 = _topk_axis0(scores[PEER_NKEYS:], k)
    counts = [k // (a + 1) for a in range(k)]
    starts = np.cumsum([0] + counts[:-1])
    pad = (-sum(counts)) % 8
    comb = jnp.concatenate([v0[a:a + 1] + v1[:counts[a]] for a in range(k)]
                           + [jnp.full((pad, v0.shape[1]), -jnp.inf, F32)], axis=0)
    sf, pos = _topk_axis0(comb, k)
    pa = jnp.zeros(pos.shape, F32)
    pb = pos
    for a in range(1, k):
        later = pos >= float(starts[a])
        pa = pa + jnp.where(later, 1.0, 0.0)
        pb = pb - jnp.where(later, float(counts[a - 1]), 0.0)
    ei_ref[...] = _pick_rows(i0, pa, k)
    ej_ref[...] = _pick_rows(i1, pb, k)
    e = jnp.exp(sf - jnp.max(sf, axis=0, keepdims=True))
    g_ref[...] = e / jnp.sum(e, axis=0, keepdims=True)


def _peer_route(q, keys):
    n = q.shape[0]
    tn = 512
    out = jax.ShapeDtypeStruct((PEER_HEADS, PEER_TOPK, n), F32)
    ospec = pl.BlockSpec((None, PEER_TOPK, tn), lambda i, h: (h, 0, i))
    return pl.pallas_call(
        _route_body,
        grid=(n // tn, PEER_HEADS),
        in_specs=[pl.BlockSpec((tn, PEER_DKEY), lambda i, h: (i, h)),
                  pl.BlockSpec((None, 2 * PEER_NKEYS, PEER_DKEY), lambda i, h: (h, 0, 0))],
        out_specs=[ospec, ospec, ospec],
        out_shape=[out, out, out],
        compiler_params=_cparams(("parallel", "arbitrary")),
        name="peer_route",
    )(q, keys)


def _gate_body(ei_ref, ej_ref, g_ref, o_ref):
    tb = ei_ref.shape[0]
    nk = PEER_NKEYS
    iota = lax.broadcasted_iota(jnp.int32, (tb, nk, ei_ref.shape[2]), 1).astype(F32)
    rows = jnp.where(iota == ei_ref[...], 1.0, 0.0).astype(BF16)
    cols = jnp.where(iota == ej_ref[...], g_ref[...], 0.0).astype(BF16)
    gm = lax.dot_general(rows, cols, (((2,), (2,)), ((0,), (0,))), preferred_element_type=F32)
    o_ref[...] = jnp.swapaxes(gm, 0, 1).astype(o_ref.dtype)


def _gate_matrix(ei, ej, g):
    n, _, slots = ei.shape
    tb = 128
    spec = pl.BlockSpec((tb, 1, slots), lambda i: (i, 0, 0))
    return pl.pallas_call(
        _gate_body,
        grid=(n // tb,),
        in_specs=[spec, spec, spec],
        out_specs=pl.BlockSpec((PEER_NKEYS, tb, PEER_NKEYS), lambda i: (0, i, 0)),
        out_shape=jax.ShapeDtypeStruct((PEER_NKEYS, n, PEER_NKEYS), BF16),
        compiler_params=_cparams(("parallel",)),
        name="peer_gate_matrix",
    )(ei, ej, g)


def _expert_body(x_ref, u_ref, g_ref, v_ref, o_ref):
    @pl.when(pl.program_id(1) == 0)
    def _():
        o_ref[...] = jnp.zeros(o_ref.shape, F32)

    hid = _dot_nt(x_ref[...], u_ref[...].astype(BF16))
    gate = jnp.concatenate([g_ref[i] for i in range(g_ref.shape[0])], axis=1)
    act = (jax.nn.gelu(hid) * gate.astype(F32)).astype(BF16)
    o_ref[...] += jnp.dot(act, v_ref[...].astype(BF16), preferred_element_type=F32)


def _peer_experts(xb, u, gm, v):
    n = xb.shape[0]
    ne = u.shape[0]
    tn, te = 1024, 1024
    once = dict(pipeline_mode=pl.Buffered(1))
    return pl.pallas_call(
        _expert_body,
        grid=(n // tn, ne // te),
        in_specs=[pl.BlockSpec((tn, D_MODEL), lambda i, j: (i, 0), **once),
                  pl.BlockSpec((te, D_MODEL), lambda i, j: (j, 0)),
                  pl.BlockSpec((te // PEER_NKEYS, tn, PEER_NKEYS), lambda i, j: (j, i, 0)),
                  pl.BlockSpec((te, D_MODEL), lambda i, j: (j, 0))],
        out_specs=pl.BlockSpec((tn, D_MODEL), lambda i, j: (i, 0), **once),
        out_shape=jax.ShapeDtypeStruct((n, D_MODEL), F32),
        compiler_params=_cparams(("parallel", "arbitrary")),
        name="peer_experts",
    )(xb, u, gm, v)


def _token_mixer(x, mem, w_in, diff_lambda, diff_subln, cmp_pe, cmp_w1, cmp_w2, w_mem_kv, w_branch, rel_bias):
    b, t, _ = x.shape
    n = b * t
    L = ATT_TILE

    w_a, w_mg, w_g = _w_prep(w_in.T)
    ha, hg, xb = _in_proj(x.reshape(n, D_MODEL), w_a, w_g)
    ha3 = ha.reshape(b, t, ha.shape[1])

    tab1d = rel_bias[_rel_bucket(jnp.arange(t))].T * LOG2E
    n_far = min(t // L, REL_MAX_DIST // L + 2)
    bias_a = _bias_tiles(tab1d[:A_HEADS], t, -1, n_far + 1, t, 1)
    bias_b = tab1d[A_HEADS:]
    bias_sel = _bias_tiles(bias_b, t, -SEL_NQ, n_far + SEL_NQ, t, B_HPG)
    n_win = min(t // L, WINDOW // L + 1)
    bias_win = _bias_tiles(bias_b, t, -1, n_win + 1, WINDOW, B_HPG)

    def values_t(v):
        heads, dv = v.shape[2:]
        return v.reshape(b, t // L, L, heads, dv).transpose(0, 3, 1, 4, 2)

    o_a = _diff_attention(ha3, values_t(ha3[:, :, 3072:4096].reshape(b, t, A_HEADS, A_DV)), bias_a, diff_lambda,
                          jnp.broadcast_to(diff_subln[:, None], (A_DV, LANES)), b, t)

    glog_t = hg[:, :48].reshape(b, t, 3, B_GROUPS, B_HPG).transpose(2, 0, 3, 4, 1)
    nr = t // CMP_STRIDE
    w1 = cmp_w1.reshape(2, CMP_LEN, B_DH, CMP_HIDDEN)
    z1 = jnp.zeros_like(w1[0])
    w1_bd = jnp.concatenate([jnp.concatenate([w1[0], z1], axis=2), jnp.concatenate([z1, w1[1]], axis=2)],
                            axis=1).astype(BF16)
    z2 = jnp.zeros_like(cmp_w2[0])
    w2_bd = jnp.concatenate([jnp.concatenate([cmp_w2[0], z2], axis=1), jnp.concatenate([z2, cmp_w2[1]], axis=1)],
                            axis=0).astype(BF16)
    pe_bd = jnp.broadcast_to(jnp.concatenate([cmp_pe[0], cmp_pe[1]], axis=1)[:, None, :],
                             (CMP_LEN, 8, 2 * B_DH)).astype(BF16)
    kvc = _compress(ha3[:, :, 5120:5632].astype(F32), w1_bd, w2_bd, pe_bd)
    n_c = (t - CMP_LEN) // CMP_STRIDE + 1
    n_sel = t // SEL_BLOCK
    cidx = np.arange(nr)[:, None] * CMP_STRIDE + np.arange(CMP_LEN)[None, :]
    overlap = (cidx[:, :, None] // SEL_BLOCK == np.arange(n_sel)[None, None, :]).astype(np.float32).mean(axis=1)
    overlap[n_c:] = 0.0
    overlap_t = np.zeros((LANES, nr), np.float32)
    overlap_t[:n_sel] = overlap.T
    o_c, sel_mask = _nsa_compressed(ha3, kvc, jnp.asarray(overlap_t, BF16), glog_t[0], b, t, n_c,
                                    min(SEL_TOPN, n_sel))
    v_sel = ha3[:, :, 5632:6144].reshape(b, t, B_GROUPS, 2, B_DH)[:, :, :, 1]
    v_win = ha3[:, :, 6144:6656].reshape(b, t, B_GROUPS, 2, B_DH)[:, :, :, 1]
    assert n_sel <= B_DH
    hot = np.zeros((t, LANES), np.float32)
    hot[np.arange(t), B_DH + np.arange(t) // SEL_BLOCK] = 1.0
    o_cs = _nsa_branch(ha3, values_t(v_sel), bias_sel, glog_t[1], o_c, sel_mask, jnp.asarray(hot, BF16), b, t,
                       "sel", SEL_NQ, F32)
    o_b = _nsa_branch(ha3, values_t(v_win), bias_win, glog_t[2], o_cs, None, None, b, t, "win", 1, BF16)

    n_mem = mem.shape[1]
    memkv = _matmul(mem.reshape(b * n_mem, D_MODEL).astype(BF16), w_mem_kv.astype(BF16), BF16,
                    b * n_mem, 1024, "mem_kv_proj")
    o_m = _memory_attention(ha3, memkv, b, t, n_mem)

    flat = lambda o: o.reshape(n, MIX_W)
    return _merge(xb, w_mg, flat(o_a), flat(o_b), flat(o_m), w_branch.astype(BF16))


def _peer(x1, x1b, peer_wq, peer_keys, peer_u, peer_v):
    n = x1.shape[0]
    q = _matmul(x1b, peer_wq.astype(BF16), BF16, 1024, 1024, "peer_query")
    zk = jnp.zeros((PEER_HEADS, PEER_NKEYS, PEER_DKEY // 2), F32)
    keys_bd = jnp.concatenate([jnp.concatenate([peer_keys[:, 0], zk], axis=2),
                               jnp.concatenate([zk, peer_keys[:, 1]], axis=2)], axis=1).astype(BF16)
    ei, ej, gate = _peer_route(q, keys_bd)
    slots = lambda a: a.reshape(PEER_HEADS * PEER_TOPK, n).T.reshape(n, 1, PEER_HEADS * PEER_TOPK)
    gm = _gate_matrix(slots(ei), slots(ej), slots(gate))
    return _peer_experts(x1b, peer_u, gm, peer_v)


def kernel(x, mem, w_in, diff_lambda, diff_subln, cmp_pe, cmp_w1, cmp_w2, w_mem_kv, w_branch, w_out, ln1_g, ln1_b,
           peer_wq, peer_keys, peer_u, peer_v, ln2_g, ln2_b, rel_bias):
    b, t, _ = x.shape
    n = b * t
    for l in range(DEPTH):
        mixed = _token_mixer(x, mem, w_in[l], diff_lambda[l], diff_subln[l], cmp_pe[l], cmp_w1[l], cmp_w2[l],
                             w_mem_kv[l], w_branch[l], rel_bias)
        x1, x1b = _out_proj_ln(mixed, w_out[l].astype(BF16), x.reshape(n, D_MODEL),
                               ln1_g[l].reshape(1, D_MODEL), ln1_b[l].reshape(1, D_MODEL))
        y = _peer(x1, x1b, peer_wq[l], peer_keys[l], peer_u[l], peer_v[l])
        x = _residual_ln(x1, y, ln2_g[l].reshape(1, D_MODEL), ln2_b[l].reshape(1, D_MODEL)).reshape(b, t, D_MODEL)
    return x
```

```python
import functools
import math

import numpy as np
import jax
import jax.numpy as jnp
from jax import lax
from jax.experimental import pallas as pl
from jax.experimental.pallas import tpu as pltpu

F32 = jnp.float32
BF16 = jnp.bfloat16

D_MODEL = 2048
A_HEADS, A_DQK, A_DV = 8, 64, 128
B_HEADS, B_GROUPS, B_HPG, B_DH = 16, 4, 4, 64
CMP_LEN, CMP_STRIDE, CMP_HIDDEN = 32, 16, 256
SEL_BLOCK, SEL_TOPN, WINDOW = 64, 16, 512
M_HEADS, M_DH = 4, 256
REL_BUCKETS, REL_MAX_DIST = 32, 1024
PEER_HEADS, PEER_NKEYS, PEER_DKEY, PEER_TOPK = 8, 128, 256, 16
MIX_W = 1024
LN_EPS = 1e-5
FORCE = 1e9
DEPTH = 1
ALPHA = (2 * DEPTH) ** 0.25
LAMBDA_INIT = 0.8 - 0.6 * math.exp(-0.3 * 0)
LOG2E = math.log2(math.e)

NEG = -1e30
ATT_TILE = 256
LANES = 128
ONES_ROWS = 16
N_MERGE = 3
SEL_NQ = 1
VMEM_LIMIT = 56 * 1024 * 1024


def _cparams(sem):
    return pltpu.CompilerParams(dimension_semantics=sem, vmem_limit_bytes=VMEM_LIMIT)


def _dot_nt(a, b):
    return lax.dot_general(a, b, (((1,), (1,)), ((), ())), preferred_element_type=F32)


def _mm_body(x_ref, w_ref, o_ref):
    o_ref[...] = jnp.dot(x_ref[...], w_ref[...], preferred_element_type=F32).astype(o_ref.dtype)


def _mm_nt_body(x_ref, w_ref, o_ref):
    o_ref[...] = _dot_nt(x_ref[...], w_ref[...]).astype(o_ref.dtype)


def _matmul(x, w, out_dtype, tm, tn, name, w_transposed=False):
    m, k = x.shape
    n = w.shape[0] if w_transposed else w.shape[1]
    w_spec = pl.BlockSpec((tn, k), lambda i, j: (j, 0)) if w_transposed else pl.BlockSpec((k, tn), lambda i, j: (0, j))
    return pl.pallas_call(
        _mm_nt_body if w_transposed else _mm_body,
        grid=(m // tm, n // tn),
        in_specs=[pl.BlockSpec((tm, k), lambda i, j: (i, 0)), w_spec],
        out_specs=pl.BlockSpec((tm, tn), lambda i, j: (i, j)),
        out_shape=jax.ShapeDtypeStruct((m, n), out_dtype),
        compiler_params=_cparams(("parallel", "arbitrary")),
        name=name,
    )(x, w)


def _in_proj_body(x_ref, w_ref, wg_ref, o_ref, og_ref, xb_ref):
    xb = x_ref[...].astype(BF16)
    o_ref[...] = _dot_nt(xb, w_ref[...]).astype(o_ref.dtype)

    @pl.when(pl.program_id(1) == 0)
    def _():
        xb_ref[...] = xb
        og_ref[...] = _dot_nt(xb, wg_ref[...])


def _in_proj(x, w_a, w_g):
    n, k = x.shape
    tm, tn = 512, 1664
    return pl.pallas_call(
        _in_proj_body,
        grid=(n // tm, W_A_COLS // tn),
        in_specs=[pl.BlockSpec((tm, k), lambda i, j: (i, 0)),
                  pl.BlockSpec((tn, k), lambda i, j: (j, 0)),
                  pl.BlockSpec((W_G_COLS, k), lambda i, j: (0, 0))],
        out_specs=[pl.BlockSpec((tm, tn), lambda i, j: (i, j)),
                   pl.BlockSpec((tm, W_G_COLS), lambda i, j: (i, 0)),
                   pl.BlockSpec((tm, k), lambda i, j: (i, 0))],
        out_shape=[jax.ShapeDtypeStruct((n, W_A_COLS), BF16), jax.ShapeDtypeStruct((n, W_G_COLS), F32),
                   jax.ShapeDtypeStruct((n, k), BF16)],
        compiler_params=_cparams(("parallel", "arbitrary")),
        name="in_proj",
    )(x, w_a, w_g)


W_A_COLS = 6656
W_G_COLS = 128


def _w_prep_body(w_ref, wa_ref, wm_ref, wg_ref):
    def put(dst_ref, d0, s0, n, c=1.0):
        val = w_ref[s0:s0 + n, :]
        dst_ref[d0:d0 + n, :] = (val * c if c != 1.0 else val).astype(dst_ref.dtype)

    put(wa_ref, 0, 5680, 1024, M_DH ** -0.5 * LOG2E)
    put(wa_ref, 1024, 0, 1024, A_DQK ** -0.5 * LOG2E)
    put(wa_ref, 2048, 1024, 2048)
    put(wa_ref, 4096, 3072, 1024, B_DH ** -0.5 * LOG2E)
    for br in range(3):
        for g in range(B_GROUPS):
            src = 4096 + br * 2 * B_GROUPS * B_DH + g * B_DH
            dst = 5120 + (br * B_GROUPS + g) * 2 * B_DH
            put(wa_ref, dst, src, B_DH)
            put(wa_ref, dst + B_DH, src + B_GROUPS * B_DH, B_DH)
    put(wm_ref, 0, 6704, N_MERGE * D_MODEL)
    put(wg_ref, 0, 5632, 48)
    wg_ref[48:, :] = jnp.zeros((W_G_COLS - 48, wg_ref.shape[1]), wg_ref.dtype)


def _w_prep(w_in_t):
    c, k = w_in_t.shape
    tk = 256
    col = lambda i: (0, i)
    return pl.pallas_call(
        _w_prep_body,
        grid=(k // tk,),
        in_specs=[pl.BlockSpec((c, tk), col)],
        out_specs=[pl.BlockSpec((W_A_COLS, tk), col), pl.BlockSpec((N_MERGE * D_MODEL, tk), col),
                   pl.BlockSpec((W_G_COLS, tk), col)],
        out_shape=[jax.ShapeDtypeStruct((W_A_COLS, k), BF16), jax.ShapeDtypeStruct((N_MERGE * D_MODEL, k), BF16),
                   jax.ShapeDtypeStruct((W_G_COLS, k), BF16)],
        compiler_params=_cparams(("parallel",)),
        name="w_in_regroup",
    )(w_in_t)


def _rel_bucket(dist):
    n = jnp.maximum(dist, 0)
    max_exact = REL_BUCKETS // 2
    nf = jnp.maximum(n, 1).astype(jnp.float32)
    large = max_exact + (jnp.log(nf / max_exact) / math.log(REL_MAX_DIST / max_exact)
                         * (REL_BUCKETS - max_exact)).astype(jnp.int32)
    large = jnp.minimum(large, REL_BUCKETS - 1)
    return jnp.where(n < max_exact, n, large)


def _bias_tiles(tab1d, t, first, n_tiles, max_dist, hpr):
    L = ATT_TILE
    m = np.arange(2 * L)
    off = np.where(m <= L, m, m - 2 * L)
    d = (first + np.arange(n_tiles))[:, None] * L + off[None, :]
    ok = (d >= 0) & (d < max_dist)
    h = tab1d.shape[0]
    rp = jnp.where(ok[:, None], tab1d.T[np.clip(d, 0, t - 1)].transpose(0, 2, 1), NEG)
    return pl.pallas_call(
        functools.partial(_toeplitz_body, hpr=hpr),
        grid=(n_tiles,),
        in_specs=[pl.BlockSpec((None, h, 1, 2 * L), lambda c: (c, 0, 0, 0))],
        out_specs=pl.BlockSpec((None, h // hpr, L, hpr * L), lambda c: (c, 0, 0, 0)),
        out_shape=jax.ShapeDtypeStruct((n_tiles, h // hpr, L, hpr * L), F32),
        compiler_params=_cparams(("parallel",)),
        name="bias_tiles",
    )(rp.reshape(n_tiles, h, 1, 2 * L))


def _toeplitz_body(rp_ref, o_ref, *, hpr):
    L = o_ref.shape[1]
    for hd in range(rp_ref.shape[0]):
        rows = jnp.broadcast_to(rp_ref[hd], (L, 2 * L))
        tile = pltpu.roll(rows, 0, 1, stride=1, stride_axis=0)[:, :L]
        o_ref[hd // hpr, :, (hd % hpr) * L:(hd % hpr + 1) * L] = tile


def _flash_loop(lo, hi, scores, values, m_ref, acc_ref, s_ref, mx_ref):
    def ahead(kt, slot):
        s = scores(kt)
        s_ref[slot] = s
        mx_ref[slot] = jnp.broadcast_to(jnp.max(s, axis=0, keepdims=True), mx_ref.shape[1:])

    def finish(kt, slot):
        m_prev = m_ref[...]
        m_new = jnp.maximum(m_prev, mx_ref[slot])
        alpha = jnp.exp2(m_prev - m_new)
        p = jnp.exp2(s_ref[slot] - m_new[0:1])
        acc_ref[...] = alpha[0:1] * acc_ref[...] + jnp.dot(values(kt), p.astype(BF16), preferred_element_type=F32)
        m_ref[...] = m_new

    if lo is None:
        tiles = hi
        ahead(tiles[0], 0)
        for j, kt in enumerate(tiles):
            if j + 1 < len(tiles):
                ahead(tiles[j + 1], (j + 1) % 2)
            finish(kt, j % 2)
        return

    def body(i, carry):
        kt = lo + 2 * i
        ahead(kt + 1, 1)
        finish(kt, 0)
        ahead(kt + 2, 0)
        finish(kt + 1, 1)
        return carry

    ahead(lo, 0)
    lax.fori_loop(0, (hi - lo + 1) // 2, body, 0)


def _diff_body(q_ref, k_ref, vt_ref, bias_ref, lam_ref, g_ref, o_ref, m_sc, acc_sc, s_sc, mx_sc, *, n_bt):
    L = ATT_TILE
    tq = 2 * L
    last = vt_ref.shape[0] - 1
    qi = pl.program_id(2)
    m_sc[...] = jnp.full(m_sc.shape, NEG, F32)
    acc_sc[...] = jnp.zeros(acc_sc.shape, F32)
    qt = q_ref[...].astype(F32).T.astype(BF16)
    zero = jnp.zeros((A_DQK, tq), BF16)
    q_cat = jnp.concatenate([jnp.concatenate([qt[:A_DQK], zero], axis=0),
                             jnp.concatenate([zero, qt[A_DQK:]], axis=0)], axis=1)

    def scores(kt):
        off = pl.multiple_of(jnp.minimum(kt, last) * L, L)
        k = k_ref[pl.ds(off, L), :]
        d0 = 2 * qi - kt
        bias = jnp.concatenate([bias_ref[jnp.clip(d0 + 1, 0, n_bt - 1)],
                                bias_ref[jnp.clip(d0 + 2, 0, n_bt - 1)]], axis=1)
        s = jnp.dot(k, q_cat, preferred_element_type=F32)
        return jnp.concatenate([s[:, :tq] + bias, s[:, tq:] + bias], axis=1)

    ones = jnp.ones((ONES_ROWS, L), BF16)

    def values(kt):
        return jnp.concatenate([vt_ref[jnp.minimum(kt, last)], ones], axis=0)

    _flash_loop(0, 2 * qi + 2, scores, values, m_sc, acc_sc, s_sc, mx_sc)

    lp = lam_ref[...]
    lam = (jnp.exp(jnp.sum(lp[0:1] * lp[1:2], axis=1, keepdims=True))
           - jnp.exp(jnp.sum(lp[2:3] * lp[3:4], axis=1, keepdims=True)) + LAMBDA_INIT)
    acc = acc_sc[...]
    o0 = acc[:A_DV, :tq] / jnp.maximum(acc[A_DV:A_DV + 1, :tq], 1e-30)
    o1 = acc[:A_DV, tq:] / jnp.maximum(acc[A_DV:A_DV + 1, tq:], 1e-30)
    o = o0 - lam * o1
    g = jnp.concatenate([g_ref[...]] * (tq // LANES), axis=1)
    o = o * lax.rsqrt(jnp.mean(o * o, axis=0, keepdims=True) + LN_EPS) * g
    o_ref[...] = (o * (1.0 - LAMBDA_INIT)).T.astype(o_ref.dtype)


def _diff_attention(ha, v_t, bias, lam_params, subln, b, t):
    L = ATT_TILE
    tq = 2 * L
    n_bt = bias.shape[0]
    return pl.pallas_call(
        functools.partial(_diff_body, n_bt=n_bt),
        grid=(b, A_HEADS, t // tq),
        in_specs=[
            pl.BlockSpec((None, tq, 128), lambda bi, h, qi: (bi, qi, 8 + h)),
            pl.BlockSpec((None, t, 128), lambda bi, h, qi: (bi, 0, 16 + h)),
            pl.BlockSpec((None, None, t // L, A_DV, L), lambda bi, h, qi: (bi, h, 0, 0, 0)),
            pl.BlockSpec((n_bt, None, L, L), lambda bi, h, qi: (0, h, 0, 0)),
            pl.BlockSpec((4, A_DQK), lambda bi, h, qi: (0, 0)),
            pl.BlockSpec((A_DV, LANES), lambda bi, h, qi: (0, 0)),
        ],
        out_specs=pl.BlockSpec((None, tq, 128), lambda bi, h, qi: (bi, qi, h)),
        out_shape=jax.ShapeDtypeStruct((b, t, MIX_W), BF16),
        scratch_shapes=[pltpu.VMEM((8, 2 * tq), F32), pltpu.VMEM((A_DV + ONES_ROWS, 2 * tq), F32),
                        pltpu.VMEM((2, L, 2 * tq), F32), pltpu.VMEM((2, 8, 2 * tq), F32)],
        compiler_params=_cparams(("parallel", "parallel", "arbitrary")),
        name="diff_attention",
    )(ha, ha, v_t, bias, lam_params, subln)


def _compress_body(kv_ref, w1_ref, w2_ref, pe_ref, o_ref):
    nr = o_ref.shape[0]
    first = jnp.zeros((nr, 2 * CMP_HIDDEN), F32)
    second = jnp.zeros((nr, 2 * CMP_HIDDEN), F32)
    for l in range(CMP_STRIDE):
        rows = kv_ref[pl.ds(l, nr, stride=CMP_STRIDE), :].astype(BF16)
        first = first + jnp.dot(rows, w1_ref[l], preferred_element_type=F32)
        second = second + jnp.dot(rows, w1_ref[CMP_STRIDE + l], preferred_element_type=F32)
    pw = jnp.zeros((8, 2 * CMP_HIDDEN), F32)
    for l in range(CMP_LEN):
        pw = pw + jnp.dot(pe_ref[l], w1_ref[l], preferred_element_type=F32)
    second = jnp.concatenate([second[1:], second[:1]], axis=0)
    hdn = jax.nn.gelu(first + second + pw[0:1])
    o_ref[...] = jnp.dot(hdn.astype(BF16), w2_ref[...], preferred_element_type=F32)


def _compress(kv, w1_bd, w2_bd, pe_bd):
    b, t, _ = kv.shape
    nr = t // CMP_STRIDE
    return pl.pallas_call(
        _compress_body,
        grid=(b, B_GROUPS),
        in_specs=[
            pl.BlockSpec((None, t, LANES), lambda bi, gi: (bi, 0, gi)),
            pl.BlockSpec(w1_bd.shape, lambda bi, gi: (0, 0, 0)),
            pl.BlockSpec(w2_bd.shape, lambda bi, gi: (0, 0)),
            pl.BlockSpec(pe_bd.shape, lambda bi, gi: (0, 0, 0)),
        ],
        out_specs=pl.BlockSpec((None, None, nr, 2 * B_DH), lambda bi, gi: (bi, gi, 0, 0)),
        out_shape=jax.ShapeDtypeStruct((b, B_GROUPS, nr, 2 * B_DH), F32),
        compiler_params=_cparams(("parallel", "parallel")),
        name="nsa_compress",
    )(kv, w1_bd, w2_bd, pe_bd)


def _topk_mask_axis0(v, k):
    r, n = v.shape
    iota = lax.broadcasted_iota(jnp.int32, (r, n), 0).astype(F32)

    def body(_, c):
        v, sel = c
        mx = jnp.max(v, axis=0, keepdims=True)
        idx = jnp.min(jnp.where(v == mx, iota, float(r)), axis=0, keepdims=True)
        hit = iota == idx
        return jnp.where(hit, -jnp.inf, v), jnp.where(hit, 1.0, sel)

    _, sel = lax.fori_loop(0, k, body, (v, jnp.zeros((r, n), F32)))
    return sel


def _cmp_body(q_ref, kvc_ref, ov_ref, gl_ref, o_ref, mask_ref, *, n_c, n_top):
    L = ATT_TILE
    qi = pl.program_id(2)
    ncp = kvc_ref.shape[0]
    n_sel = mask_ref.shape[0]
    qt = q_ref[...].astype(F32).T.astype(BF16)
    qh = jnp.concatenate([qt[h * B_DH:(h + 1) * B_DH] for h in range(B_HPG)], axis=1)
    kvc = kvc_ref[...]
    s = jnp.dot(kvc[:, :B_DH].astype(BF16), qh, preferred_element_type=F32)
    tcol = qi * L + lax.broadcasted_iota(jnp.int32, (1, L), 1)
    crow = lax.broadcasted_iota(jnp.int32, (ncp, 1), 0)
    seen = jnp.where(crow < n_c, crow * CMP_STRIDE + (CMP_LEN - 1), jnp.int32(2 ** 30)) <= tcol
    valid = jnp.concatenate([seen] * B_HPG, axis=1)
    s = jnp.where(valid, s, NEG)
    mx = jnp.max(s, axis=0, keepdims=True)
    e = jnp.where(valid, jnp.exp2(s - mx), 0.0)
    p = e / jnp.maximum(jnp.sum(e, axis=0, keepdims=True), 1e-30)
    o = jnp.dot(kvc.T[B_DH:].astype(BF16), p.astype(BF16), preferred_element_type=F32)
    gate = jax.nn.sigmoid(gl_ref[...])
    o = jnp.concatenate([o[:, h * L:(h + 1) * L] * gate[h:h + 1] for h in range(B_HPG)], axis=0)
    o_ref[...] = o.T

    psum = p[:, :L] + p[:, L:2 * L] + p[:, 2 * L:3 * L] + p[:, 3 * L:]
    ov_t = ov_ref[...]
    imp = jnp.zeros((ov_t.shape[0], L), F32)
    rem = psum
    for _ in range(3):
        part = rem.astype(BF16)
        imp = imp + jnp.dot(ov_t, part, preferred_element_type=F32)
        rem = rem - part.astype(F32)
    imp = imp[:n_sel]
    blk = lax.broadcasted_iota(jnp.int32, (n_sel, 1), 0)
    cur = jnp.right_shift(tcol, int(math.log2(SEL_BLOCK)))
    imp = jnp.where(blk * SEL_BLOCK > tcol, -FORCE, imp)
    imp = jnp.where(blk == 0, FORCE, imp)
    imp = jnp.where(blk == cur, FORCE, imp)
    imp = jnp.where(blk == cur - 1, FORCE, imp)
    mask_ref[...] = jnp.where(_topk_mask_axis0(imp, n_top) > 0.0, 0.0, NEG).astype(mask_ref.dtype)


def _nsa_compressed(ha, kvc, overlap_t, glog_t, b, t, n_c, n_top):
    L = ATT_TILE
    ncp = kvc.shape[2]
    n_sel = t // SEL_BLOCK
    return pl.pallas_call(
        functools.partial(_cmp_body, n_c=n_c, n_top=n_top),
        grid=(b, B_GROUPS, t // L),
        in_specs=[
            pl.BlockSpec((None, L, 256), lambda bi, g, qi: (bi, qi, 16 + g)),
            pl.BlockSpec((None, None, ncp, 2 * B_DH), lambda bi, g, qi: (bi, g, 0, 0)),
            pl.BlockSpec(overlap_t.shape, lambda bi, g, qi: (0, 0)),
            pl.BlockSpec((None, None, B_HPG, L), lambda bi, g, qi: (bi, g, 0, qi)),
        ],
        out_specs=[
            pl.BlockSpec((None, L, 256), lambda bi, g, qi: (bi, qi, g)),
            pl.BlockSpec((None, None, n_sel, L), lambda bi, g, qi: (bi, g, 0, qi)),
        ],
        out_shape=[jax.ShapeDtypeStruct((b, t, MIX_W), F32),
                   jax.ShapeDtypeStruct((b, B_GROUPS, n_sel, t), BF16)],
        compiler_params=_cparams(("parallel", "parallel", "arbitrary")),
        name="nsa_compressed_select",
    )(ha, kvc, overlap_t, glog_t)


def _nsa_body(*refs, mode, n_bt):
    if mode == "sel":
        q_ref, kv_ref, vt_ref, bias_ref, gl_ref, prev_ref, mask_ref, hot_ref, o_ref, m_sc, acc_sc, s_sc, mx_sc = refs
    else:
        q_ref, kv_ref, vt_ref, bias_ref, gl_ref, prev_ref, o_ref, m_sc, acc_sc, s_sc, mx_sc = refs
    L = ATT_TILE
    nq = q_ref.shape[0] // L
    qi = pl.program_id(2)
    m_sc[...] = jnp.full(m_sc.shape, NEG, F32)
    acc_sc[...] = jnp.zeros(acc_sc.shape, F32)
    qt = q_ref[...].astype(F32).T.astype(BF16)
    if mode == "sel":
        n_sel = mask_ref.shape[0]
        qm = mask_ref[...]
        if n_sel < B_DH:
            qm = jnp.concatenate([qm, jnp.zeros((B_DH - n_sel, nq * L), BF16)], axis=0)
    else:
        qm = jnp.zeros((B_DH, nq * L), BF16)
    q_aug_t = jnp.concatenate(
        [jnp.concatenate([qt[h * B_DH:(h + 1) * B_DH, s * L:(s + 1) * L], qm[:, s * L:(s + 1) * L]], axis=0)
         for s in range(nq) for h in range(B_HPG)], axis=1)

    last = vt_ref.shape[0] - 1
    ones = jnp.ones((ONES_ROWS, L), BF16)
    k_lanes = lax.broadcasted_iota(jnp.int32, (L, LANES), 1) < B_DH

    def scores(kt):
        off = pl.multiple_of(jnp.clip(kt, 0, last) * L, L)
        idx = jnp.where(kt < 0, 0, nq * qi - kt + nq)
        bias = [bias_ref[jnp.clip(idx + s, 0, n_bt - 1)] for s in range(nq)]
        bias = bias[0] if nq == 1 else jnp.concatenate(bias, axis=1)
        k_aug = kv_ref[pl.ds(off, L), :]
        if mode == "sel":
            k_aug = jnp.where(k_lanes, k_aug, hot_ref[pl.ds(off, L), :])
        return jnp.dot(k_aug, q_aug_t, preferred_element_type=F32) + bias

    def values(kt):
        return jnp.concatenate([vt_ref[jnp.clip(kt, 0, last)], ones], axis=0)

    if mode == "sel":
        _flash_loop(0, nq * qi + nq, scores, values, m_sc, acc_sc, s_sc, mx_sc)
    else:
        n_tiles = n_bt - nq + (nq - 1)
        tiles = [nq * qi + (nq - 1) - j for j in range(n_tiles)]
        _flash_loop(None, tiles, scores, values, m_sc, acc_sc, s_sc, mx_sc)
    acc = acc_sc[...]
    o = acc[:B_DH] / jnp.maximum(acc[B_DH:B_DH + 1], 1e-30)
    gate = jax.nn.sigmoid(gl_ref[...])
    heads = []
    for h in range(B_HPG):
        cols = [o[:, (s * B_HPG + h) * L:(s * B_HPG + h + 1) * L] for s in range(nq)]
        heads.append((cols[0] if nq == 1 else jnp.concatenate(cols, axis=1)) * gate[h:h + 1])
    o_ref[...] = (prev_ref[...] + jnp.concatenate(heads, axis=0).T).astype(o_ref.dtype)


def _nsa_branch(ha, v_t, bias, glog_t, prev, mask_t, hot, b, t, mode, nq, out_dtype):
    L = ATT_TILE
    tq = nq * L
    n_bt = bias.shape[0]
    branch = 1 if mode == "sel" else 2
    in_specs = [
        pl.BlockSpec((None, tq, 256), lambda bi, g, qi: (bi, qi, 16 + g)),
        pl.BlockSpec((None, t, LANES), lambda bi, g, qi: (bi, 0, 40 + 4 * branch + g)),
        pl.BlockSpec((None, None, t // L, B_DH, L), lambda bi, g, qi: (bi, g, 0, 0, 0)),
        pl.BlockSpec((n_bt, None, L, B_HPG * L), lambda bi, g, qi: (0, g, 0, 0)),
        pl.BlockSpec((None, None, B_HPG, tq), lambda bi, g, qi: (bi, g, 0, qi)),
        pl.BlockSpec((None, tq, 256), lambda bi, g, qi: (bi, qi, g)),
    ]
    args = [ha, ha, v_t, bias, glog_t, prev]
    if mode == "sel":
        n_sel = mask_t.shape[2]
        in_specs.append(pl.BlockSpec((None, None, n_sel, tq), lambda bi, g, qi: (bi, g, 0, qi)))
        in_specs.append(pl.BlockSpec((t, LANES), lambda bi, g, qi: (0, 0)))
        args += [mask_t, hot]
    return pl.pallas_call(
        functools.partial(_nsa_body, mode=mode, n_bt=n_bt),
        grid=(b, B_GROUPS, t // tq),
        in_specs=in_specs,
        out_specs=pl.BlockSpec((None, tq, 256), lambda bi, g, qi: (bi, qi, g)),
        out_shape=jax.ShapeDtypeStruct((b, t, MIX_W), out_dtype),
        scratch_shapes=[pltpu.VMEM((8, B_HPG * tq), F32), pltpu.VMEM((B_DH + ONES_ROWS, B_HPG * tq), F32),
                        pltpu.VMEM((2, L, B_HPG * tq), F32), pltpu.VMEM((2, 8, B_HPG * tq), F32)],
        compiler_params=_cparams(("parallel", "parallel", "arbitrary")),
        name="nsa_" + mode,
    )(*args)


def _mem_body(q_ref, k_ref, v_ref, o_ref):
    q = q_ref[...]
    outs = []
    for h in range(M_HEADS):
        sl = slice(h * M_DH, (h + 1) * M_DH)
        s = _dot_nt(q[:, sl], k_ref[:, sl])
        e = jnp.exp2(s - jnp.max(s, axis=1, keepdims=True))
        p = e / jnp.sum(e, axis=1, keepdims=True)
        outs.append(jnp.dot(p.astype(BF16), v_ref[:, sl], preferred_element_type=F32))
    o_ref[...] = jnp.concatenate(outs, axis=1).astype(o_ref.dtype)


def _memory_attention(ha, memkv, b, t, n_mem):
    tq = 512
    w = M_HEADS * M_DH
    return pl.pallas_call(
        _mem_body,
        grid=(b, t // tq),
        in_specs=[
            pl.BlockSpec((None, tq, w), lambda bi, qi: (bi, qi, 0)),
            pl.BlockSpec((n_mem, w), lambda bi, qi: (bi, 0)),
            pl.BlockSpec((n_mem, w), lambda bi, qi: (bi, 1)),
        ],
        out_specs=pl.BlockSpec((None, tq, w), lambda bi, qi: (bi, qi, 0)),
        out_shape=jax.ShapeDtypeStruct((b, t, w), BF16),
        compiler_params=_cparams(("parallel", "arbitrary")),
        name="memory_attention",
    )(ha, memkv, memkv)


def _merge_body(x_ref, wg_ref, oa_ref, ob_ref, om_ref, wb_ref, o_ref, acc_sc):
    n = pl.program_id(1)

    @pl.when(n == 0)
    def _():
        acc_sc[...] = jnp.zeros(acc_sc.shape, F32)

    gate = jax.nn.sigmoid(_dot_nt(x_ref[...], wg_ref[...]))
    branch = jnp.where(n == 0, oa_ref[...], jnp.where(n == 1, ob_ref[...], om_ref[...]))
    acc = acc_sc[...] + gate * jnp.dot(branch, wb_ref[...], preferred_element_type=F32)
    acc_sc[...] = acc
    o_ref[...] = acc.astype(o_ref.dtype)


def _merge(xb, w_gate_t, o_a, o_b, o_m, w_branch):
    n = xb.shape[0]
    tm = 512
    row = lambda i, j: (i, 0)
    return pl.pallas_call(
        _merge_body,
        grid=(n // tm, N_MERGE),
        in_specs=[pl.BlockSpec((tm, D_MODEL), row),
                  pl.BlockSpec((D_MODEL, D_MODEL), lambda i, j: (j, 0))]
                 + [pl.BlockSpec((tm, MIX_W), row)] * 3
                 + [pl.BlockSpec((None, MIX_W, D_MODEL), lambda i, j: (j, 0, 0))],
        out_specs=pl.BlockSpec((tm, D_MODEL), row),
        out_shape=jax.ShapeDtypeStruct((n, D_MODEL), BF16),
        scratch_shapes=[pltpu.VMEM((tm, D_MODEL), F32)],
        compiler_params=_cparams(("parallel", "arbitrary")),
        name="branch_merge",
    )(xb, w_gate_t, o_a, o_b, o_m, w_branch)


def _layer_norm(z, g, b):
    mu = jnp.mean(z, axis=1, keepdims=True)
    zc = z - mu
    var = jnp.mean(zc * zc, axis=1, keepdims=True)
    return zc * lax.rsqrt(var + LN_EPS) * g + b


def _out_ln_body(y_ref, w_ref, x_ref, g_ref, b_ref, o_ref, ob_ref):
    y = jnp.dot(y_ref[...], w_ref[...], preferred_element_type=F32)
    o = _layer_norm(ALPHA * x_ref[...] + y, g_ref[...], b_ref[...])
    o_ref[...] = o
    ob_ref[...] = o.astype(BF16)


def _out_proj_ln(mixed, w_out, x, g, b):
    n = x.shape[0]
    tm = 512
    row = lambda i: (i, 0)
    const = lambda i: (0, 0)
    return pl.pallas_call(
        _out_ln_body,
        grid=(n // tm,),
        in_specs=[pl.BlockSpec((tm, D_MODEL), row), pl.BlockSpec((D_MODEL, D_MODEL), const),
                  pl.BlockSpec((tm, D_MODEL), row), pl.BlockSpec((1, D_MODEL), const),
                  pl.BlockSpec((1, D_MODEL), const)],
        out_specs=[pl.BlockSpec((tm, D_MODEL), row), pl.BlockSpec((tm, D_MODEL), row)],
        out_shape=[jax.ShapeDtypeStruct((n, D_MODEL), F32), jax.ShapeDtypeStruct((n, D_MODEL), BF16)],
        compiler_params=_cparams(("parallel",)),
        name="out_proj_ln1",
    )(mixed, w_out, x, g, b)


def _res_ln_body(x_ref, y_ref, g_ref, b_ref, o_ref):
    o_ref[...] = _layer_norm(ALPHA * x_ref[...] + y_ref[...], g_ref[...], b_ref[...])


def _residual_ln(x, y, g, b):
    n = x.shape[0]
    tm = 512
    row = lambda i: (i, 0)
    const = lambda i: (0, 0)
    return pl.pallas_call(
        _res_ln_body,
        grid=(n // tm,),
        in_specs=[pl.BlockSpec((tm, D_MODEL), row), pl.BlockSpec((tm, D_MODEL), row),
                  pl.BlockSpec((1, D_MODEL), const), pl.BlockSpec((1, D_MODEL), const)],
        out_specs=pl.BlockSpec((tm, D_MODEL), row),
        out_shape=jax.ShapeDtypeStruct((n, D_MODEL), F32),
        compiler_params=_cparams(("parallel",)),
        name="residual_ln2",
    )(x, y, g, b)


def _topk_axis0(v, k):
    r, n = v.shape
    iota = lax.broadcasted_iota(jnp.int32, (r, n), 0).astype(F32)
    slot = lax.broadcasted_iota(jnp.int32, (k, n), 0)

    def body(it, c):
        v, vals, idxs = c
        mx = jnp.max(v, axis=0, keepdims=True)
        idx = jnp.min(jnp.where(v == mx, iota, float(r)), axis=0, keepdims=True)
        v = jnp.where(iota == idx, -jnp.inf, v)
        return v, jnp.where(slot == it, mx, vals), jnp.where(slot == it, idx, idxs)

    _, vals, idxs = lax.fori_loop(0, k, body, (v, jnp.zeros((k, n), F32), jnp.zeros((k, n), F32)))
    return vals, idxs


def _pick_rows(table, pos, k):
    out = jnp.zeros(pos.shape, F32)
    for a in range(k):
        out = out + jnp.where(pos == float(a), table[a:a + 1], 0.0)
    return out


def _route_body(q_ref, keys_ref, ei_ref, ej_ref, g_ref):
    k = PEER_TOPK
    scores = _dot_nt(keys_ref[...], q_ref[...])
    v0, i0 = _topk_axis0(scores[:PEER_NKEYS], k)
    v1, i1 = _topk_axis0(scores[PEER_NKEYS:], k)
    counts = [k // (a + 1) for a in range(k)]
    starts = np.cumsum([0] + counts[:-1])
    pad = (-sum(counts)) % 8
    comb = jnp.concatenate([v0[a:a + 1] + v1[:counts[a]] for a in range(k)]
                           + [jnp.full((pad, v0.shape[1]), -jnp.inf, F32)], axis=0)
    sf, pos = _topk_axis0(comb, k)
    pa = jnp.zeros(pos.shape, F32)
    pb = pos
    for a in range(1, k):
        later = pos >= float(starts[a])
        pa = pa + jnp.where(later, 1.0, 0.0)
        pb = pb - jnp.where(later, float(counts[a - 1]), 0.0)
    ei_ref[...] = _pick_rows(i0, pa, k)
    ej_ref[...] = _pick_rows(i1, pb, k)
    e = jnp.exp(sf - jnp.max(sf, axis=0, keepdims=True))
    g_ref[...] = e / jnp.sum(e, axis=0, keepdims=True)


def _peer_route(q, keys):
    n = q.shape[0]
    tn = 512
    out = jax.ShapeDtypeStruct((PEER_HEADS, PEER_TOPK, n), F32)
    ospec = pl.BlockSpec((None, PEER_TOPK, tn), lambda i, h: (h, 0, i))
    return pl.pallas_call(
        _route_body,
        grid=(n // tn, PEER_HEADS),
        in_specs=[pl.BlockSpec((tn, PEER_DKEY), lambda i, h: (i, h)),
                  pl.BlockSpec((None, 2 * PEER_NKEYS, PEER_DKEY), lambda i, h: (h, 0, 0))],
        out_specs=[ospec, ospec, ospec],
        out_shape=[out, out, out],
        compiler_params=_cparams(("parallel", "arbitrary")),
        name="peer_route",
    )(q, keys)


def _gate_body(ei_ref, ej_ref, g_ref, o_ref):
    tb = ei_ref.shape[0]
    nk = PEER_NKEYS
    iota = lax.broadcasted_iota(jnp.int32, (tb, nk, ei_ref.shape[2]), 1).astype(F32)
    rows = jnp.where(iota == ei_ref[...], 1.0, 0.0).astype(BF16)
    cols = jnp.where(iota == ej_ref[...], g_ref[...], 0.0).astype(BF16)
    gm = lax.dot_general(rows, cols, (((2,), (2,)), ((0,), (0,))), preferred_element_type=F32)
    o_ref[...] = jnp.swapaxes(gm, 0, 1).astype(o_ref.dtype)


def _gate_matrix(ei, ej, g):
    n, _, slots = ei.shape
    tb = 128
    spec = pl.BlockSpec((tb, 1, slots), lambda i: (i, 0, 0))
    return pl.pallas_call(
        _gate_body,
        grid=(n // tb,),
        in_specs=[spec, spec, spec],
        out_specs=pl.BlockSpec((PEER_NKEYS, tb, PEER_NKEYS), lambda i: (0, i, 0)),
        out_shape=jax.ShapeDtypeStruct((PEER_NKEYS, n, PEER_NKEYS), BF16),
        compiler_params=_cparams(("parallel",)),
        name="peer_gate_matrix",
    )(ei, ej, g)


def _expert_body(x_ref, u_ref, g_ref, v_ref, o_ref):
    @pl.when(pl.program_id(1) == 0)
    def _():
        o_ref[...] = jnp.zeros(o_ref.shape, F32)

    hid = _dot_nt(x_ref[...], u_ref[...].astype(BF16))
    gate = jnp.concatenate([g_ref[i] for i in range(g_ref.shape[0])], axis=1)
    act = (jax.nn.gelu(hid) * gate.astype(F32)).astype(BF16)
    o_ref[...] += jnp.dot(act, v_ref[...].astype(BF16), preferred_element_type=F32)


def _peer_experts(xb, u, gm, v):
    n = xb.shape[0]
    ne = u.shape[0]
    tn, te = 1024, 1024
    once = dict(pipeline_mode=pl.Buffered(1))
    return pl.pallas_call(
        _expert_body,
        grid=(n // tn, ne // te),
        in_specs=[pl.BlockSpec((tn, D_MODEL), lambda i, j: (i, 0), **once),
                  pl.BlockSpec((te, D_MODEL), lambda i, j: (j, 0)),
                  pl.BlockSpec((te // PEER_NKEYS, tn, PEER_NKEYS), lambda i, j: (j, i, 0)),
                  pl.BlockSpec((te, D_MODEL), lambda i, j: (j, 0))],
        out_specs=pl.BlockSpec((tn, D_MODEL), lambda i, j: (i, 0), **once),
        out_shape=jax.ShapeDtypeStruct((n, D_MODEL), F32),
        compiler_params=_cparams(("parallel", "arbitrary")),
        name="peer_experts",
    )(xb, u, gm, v)


def _token_mixer(x, mem, w_in, diff_lambda, diff_subln, cmp_pe, cmp_w1, cmp_w2, w_mem_kv, w_branch, rel_bias):
    b, t, _ = x.shape
    n = b * t
    L = ATT_TILE

    w_a, w_mg, w_g = _w_prep(w_in.T)
    ha, hg, xb = _in_proj(x.reshape(n, D_MODEL), w_a, w_g)
    ha3 = ha.reshape(b, t, ha.shape[1])

    tab1d = rel_bias[_rel_bucket(jnp.arange(t))].T * LOG2E
    n_far = min(t // L, REL_MAX_DIST // L + 2)
    bias_a = _bias_tiles(tab1d[:A_HEADS], t, -1, n_far + 1, t, 1)
    bias_b = tab1d[A_HEADS:]
    bias_sel = _bias_tiles(bias_b, t, -SEL_NQ, n_far + SEL_NQ, t, B_HPG)
    n_win = min(t // L, WINDOW // L + 1)
    bias_win = _bias_tiles(bias_b, t, -1, n_win + 1, WINDOW, B_HPG)

    def values_t(v):
        heads, dv = v.shape[2:]
        return v.reshape(b, t // L, L, heads, dv).transpose(0, 3, 1, 4, 2)

    o_a = _diff_attention(ha3, values_t(ha3[:, :, 3072:4096].reshape(b, t, A_HEADS, A_DV)), bias_a, diff_lambda,
                          jnp.broadcast_to(diff_subln[:, None], (A_DV, LANES)), b, t)

    glog_t = hg[:, :48].reshape(b, t, 3, B_GROUPS, B_HPG).transpose(2, 0, 3, 4, 1)
    nr = t // CMP_STRIDE
    w1 = cmp_w1.reshape(2, CMP_LEN, B_DH, CMP_HIDDEN)
    z1 = jnp.zeros_like(w1[0])
    w1_bd = jnp.concatenate([jnp.concatenate([w1[0], z1], axis=2), jnp.concatenate([z1, w1[1]], axis=2)],
                            axis=1).astype(BF16)
    z2 = jnp.zeros_like(cmp_w2[0])
    w2_bd = jnp.concatenate([jnp.concatenate([cmp_w2[0], z2], axis=1), jnp.concatenate([z2, cmp_w2[1]], axis=1)],
                            axis=0).astype(BF16)
    pe_bd = jnp.broadcast_to(jnp.concatenate([cmp_pe[0], cmp_pe[1]], axis=1)[:, None, :],
                             (CMP_LEN, 8, 2 * B_DH)).astype(BF16)
    kvc = _compress(ha3[:, :, 5120:5632].astype(F32), w1_bd, w2_bd, pe_bd)
    n_c = (t - CMP_LEN) // CMP_STRIDE + 1
    n_sel = t // SEL_BLOCK
    cidx = np.arange(nr)[:, None] * CMP_STRIDE + np.arange(CMP_LEN)[None, :]
    overlap = (cidx[:, :, None] // SEL_BLOCK == np.arange(n_sel)[None, None, :]).astype(np.float32).mean(axis=1)
    overlap[n_c:] = 0.0
    overlap_t = np.zeros((LANES, nr), np.float32)
    overlap_t[:n_sel] = overlap.T
    o_c, sel_mask = _nsa_compressed(ha3, kvc, jnp.asarray(overlap_t, BF16), glog_t[0], b, t, n_c,
                                    min(SEL_TOPN, n_sel))
    v_sel = ha3[:, :, 5632:6144].reshape(b, t, B_GROUPS, 2, B_DH)[:, :, :, 1]
    v_win = ha3[:, :, 6144:6656].reshape(b, t, B_GROUPS, 2, B_DH)[:, :, :, 1]
    assert n_sel <= B_DH
    hot = np.zeros((t, LANES), np.float32)
    hot[np.arange(t), B_DH + np.arange(t) // SEL_BLOCK] = 1.0
    o_cs = _nsa_branch(ha3, values_t(v_sel), bias_sel, glog_t[1], o_c, sel_mask, jnp.asarray(hot, BF16), b, t,
                       "sel", SEL_NQ, F32)
    o_b = _nsa_branch(ha3, values_t(v_win), bias_win, glog_t[2], o_cs, None, None, b, t, "win", 1, BF16)

    n_mem = mem.shape[1]
    memkv = _matmul(mem.reshape(b * n_mem, D_MODEL).astype(BF16), w_mem_kv.astype(BF16), BF16,
                    b * n_mem, 1024, "mem_kv_proj")
    o_m = _memory_attention(ha3, memkv, b, t, n_mem)

    flat = lambda o: o.reshape(n, MIX_W)
    return _merge(xb, w_mg, flat(o_a), flat(o_b), flat(o_m), w_branch.astype(BF16))


def _peer(x1, x1b, peer_wq, peer_keys, peer_u, peer_v):
    n = x1.shape[0]
    q = _matmul(x1b, peer_wq.astype(BF16), BF16, 1024, 1024, "peer_query")
    zk = jnp.zeros((PEER_HEADS, PEER_NKEYS, PEER_DKEY // 2), F32)
    keys_bd = jnp.concatenate([jnp.concatenate([peer_keys[:, 0], zk], axis=2),
                               jnp.concatenate([zk, peer_keys[:, 1]], axis=2)], axis=1).astype(BF16)
    ei, ej, gate = _peer_route(q, keys_bd)
    slots = lambda a: a.reshape(PEER_HEADS * PEER_TOPK, n).T.reshape(n, 1, PEER_HEADS * PEER_TOPK)
    gm = _gate_matrix(slots(ei), slots(ej), slots(gate))
    return _peer_experts(x1b, peer_u, gm, peer_v)


def kernel(x, mem, w_in, diff_lambda, diff_subln, cmp_pe, cmp_w1, cmp_w2, w_mem_kv, w_branch, w_out, ln1_g, ln1_b,
           peer_wq, peer_keys, peer_u, peer_v, ln2_g, ln2_b, rel_bias):
    b, t, _ = x.shape
    n = b * t
    for l in range(DEPTH):
        mixed = _token_mixer(x, mem, w_in[l], diff_lambda[l], diff_subln[l], cmp_pe[l], cmp_w1[l], cmp_w2[l],
                             w_mem_kv[l], w_branch[l], rel_bias)
        x1, x1b = _out_proj_ln(mixed, w_out[l].astype(BF16), x.reshape(n, D_MODEL),
                               ln1_g[l].reshape(1, D_MODEL), ln1_b[l].reshape(1, D_MODEL))
        y = _peer(x1, x1b, peer_wq[l], peer_keys[l], peer_u[l], peer_v[l])
        x = _residual_ln(x1, y, ln2_g[l].reshape(1, D_MODEL), ln2_b[l].reshape(1, D_MODEL)).reshape(b, t, D_MODEL)
    return x
```

```python
import functools
import math

import numpy as np
import jax
import jax.numpy as jnp
from jax import lax
from jax.experimental import pallas as pl
from jax.experimental.pallas import tpu as pltpu

F32 = jnp.float32
BF16 = jnp.bfloat16

D_MODEL = 2048
A_HEADS, A_DQK, A_DV = 8, 64, 128
B_HEADS, B_GROUPS, B_HPG, B_DH = 16, 4, 4, 64
CMP_LEN, CMP_STRIDE, CMP_HIDDEN = 32, 16, 256
SEL_BLOCK, SEL_TOPN, WINDOW = 64, 16, 512
M_HEADS, M_DH = 4, 256
REL_BUCKETS, REL_MAX_DIST = 32, 1024
PEER_HEADS, PEER_NKEYS, PEER_DKEY, PEER_TOPK = 8, 128, 256, 16
MIX_W = 1024
LN_EPS = 1e-5
FORCE = 1e9
DEPTH = 1
ALPHA = (2 * DEPTH) ** 0.25
LAMBDA_INIT = 0.8 - 0.6 * math.exp(-0.3 * 0)
LOG2E = math.log2(math.e)

NEG = -1e30
ATT_TILE = 256
LANES = 128
ONES_ROWS = 16
N_MERGE = 3
SEL_NQ = 1
VMEM_LIMIT = 56 * 1024 * 1024


def _cparams(sem):
    return pltpu.CompilerParams(dimension_semantics=sem, vmem_limit_bytes=VMEM_LIMIT)


def _dot_nt(a, b):
    return lax.dot_general(a, b, (((1,), (1,)), ((), ())), preferred_element_type=F32)


def _mm_body(x_ref, w_ref, o_ref):
    o_ref[...] = jnp.dot(x_ref[...], w_ref[...], preferred_element_type=F32).astype(o_ref.dtype)


def _matmul(x, w, out_dtype, tm, tn, name):
    m, k = x.shape
    n = w.shape[1]
    return pl.pallas_call(
        _mm_body,
        grid=(m // tm, n // tn),
        in_specs=[pl.BlockSpec((tm, k), lambda i, j: (i, 0)), pl.BlockSpec((k, tn), lambda i, j: (0, j))],
        out_specs=pl.BlockSpec((tm, tn), lambda i, j: (i, j)),
        out_shape=jax.ShapeDtypeStruct((m, n), out_dtype),
        compiler_params=_cparams(("parallel", "arbitrary")),
        name=name,
    )(x, w)


def _in_proj_body(x_ref, w_ref, wg_ref, o_ref, og_ref, xb_ref):
    xb = x_ref[...].astype(BF16)
    o_ref[...] = _dot_nt(xb, w_ref[...]).astype(o_ref.dtype)

    @pl.when(pl.program_id(1) == 0)
    def _():
        xb_ref[...] = xb
        og_ref[...] = _dot_nt(xb, wg_ref[...])


def _in_proj(x, w_a, w_g):
    n, k = x.shape
    tm, tn = 512, 1664
    return pl.pallas_call(
        _in_proj_body,
        grid=(n // tm, W_A_COLS // tn),
        in_specs=[pl.BlockSpec((tm, k), lambda i, j: (i, 0)),
                  pl.BlockSpec((tn, k), lambda i, j: (j, 0)),
                  pl.BlockSpec((W_G_COLS, k), lambda i, j: (0, 0))],
        out_specs=[pl.BlockSpec((tm, tn), lambda i, j: (i, j)),
                   pl.BlockSpec((tm, W_G_COLS), lambda i, j: (i, 0)),
                   pl.BlockSpec((tm, k), lambda i, j: (i, 0))],
        out_shape=[jax.ShapeDtypeStruct((n, W_A_COLS), BF16), jax.ShapeDtypeStruct((n, W_G_COLS), F32),
                   jax.ShapeDtypeStruct((n, k), BF16)],
        compiler_params=_cparams(("parallel", "arbitrary")),
        name="in_proj",
    )(x, w_a, w_g)


W_A_COLS = 6656
W_G_COLS = 128


def _w_prep_body(w_ref, wa_ref, wm_ref, wg_ref):
    def put(dst_ref, d0, s0, n, c=1.0):
        val = w_ref[s0:s0 + n, :]
        dst_ref[d0:d0 + n, :] = (val * c if c != 1.0 else val).astype(dst_ref.dtype)

    put(wa_ref, 0, 5680, 1024, M_DH ** -0.5 * LOG2E)
    put(wa_ref, 1024, 0, 1024, A_DQK ** -0.5 * LOG2E)
    put(wa_ref, 2048, 1024, 2048)
    put(wa_ref, 4096, 3072, 1024, B_DH ** -0.5 * LOG2E)
    for br in range(3):
        for g in range(B_GROUPS):
            src = 4096 + br * 2 * B_GROUPS * B_DH + g * B_DH
            dst = 5120 + (br * B_GROUPS + g) * 2 * B_DH
            put(wa_ref, dst, src, B_DH)
            put(wa_ref, dst + B_DH, src + B_GROUPS * B_DH, B_DH)
    put(wm_ref, 0, 6704, N_MERGE * D_MODEL)
    put(wg_ref, 0, 5632, 48)
    wg_ref[48:, :] = jnp.zeros((W_G_COLS - 48, wg_ref.shape[1]), wg_ref.dtype)


def _w_prep(w_in_t):
    c, k = w_in_t.shape
    tk = 256
    col = lambda i: (0, i)
    return pl.pallas_call(
        _w_prep_body,
        grid=(k // tk,),
        in_specs=[pl.BlockSpec((c, tk), col)],
        out_specs=[pl.BlockSpec((W_A_COLS, tk), col), pl.BlockSpec((N_MERGE * D_MODEL, tk), col),
                   pl.BlockSpec((W_G_COLS, tk), col)],
        out_shape=[jax.ShapeDtypeStruct((W_A_COLS, k), BF16), jax.ShapeDtypeStruct((N_MERGE * D_MODEL, k), BF16),
                   jax.ShapeDtypeStruct((W_G_COLS, k), BF16)],
        compiler_params=_cparams(("parallel",)),
        name="w_in_regroup",
    )(w_in_t)


def _rel_bucket(dist):
    n = jnp.maximum(dist, 0)
    max_exact = REL_BUCKETS // 2
    nf = jnp.maximum(n, 1).astype(jnp.float32)
    large = max_exact + (jnp.log(nf / max_exact) / math.log(REL_MAX_DIST / max_exact)
                         * (REL_BUCKETS - max_exact)).astype(jnp.int32)
    large = jnp.minimum(large, REL_BUCKETS - 1)
    return jnp.where(n < max_exact, n, large)


def _bias_tiles(tab1d, t, first, n_tiles, max_dist, hpr):
    L = ATT_TILE
    m = np.arange(2 * L)
    off = np.where(m <= L, m, m - 2 * L)
    d = (first + np.arange(n_tiles))[:, None] * L + off[None, :]
    ok = (d >= 0) & (d < max_dist)
    h = tab1d.shape[0]
    rp = jnp.where(ok[:, None], tab1d.T[np.clip(d, 0, t - 1)].transpose(0, 2, 1), NEG)
    return pl.pallas_call(
        functools.partial(_toeplitz_body, hpr=hpr),
        grid=(n_tiles,),
        in_specs=[pl.BlockSpec((None, h, 1, 2 * L), lambda c: (c, 0, 0, 0))],
        out_specs=pl.BlockSpec((None, h // hpr, L, hpr * L), lambda c: (c, 0, 0, 0)),
        out_shape=jax.ShapeDtypeStruct((n_tiles, h // hpr, L, hpr * L), F32),
        compiler_params=_cparams(("parallel",)),
        name="bias_tiles",
    )(rp.reshape(n_tiles, h, 1, 2 * L))


def _toeplitz_body(rp_ref, o_ref, *, hpr):
    L = o_ref.shape[1]
    for hd in range(rp_ref.shape[0]):
        rows = jnp.broadcast_to(rp_ref[hd], (L, 2 * L))
        tile = pltpu.roll(rows, 0, 1, stride=1, stride_axis=0)[:, :L]
        o_ref[hd // hpr, :, (hd % hpr) * L:(hd % hpr + 1) * L] = tile


def _flash_loop(lo, hi, scores, values, m_ref, acc_ref, s_ref, mx_ref):
    def ahead(kt, slot):
        s = scores(kt)
        s_ref[slot] = s
        mx_ref[slot] = jnp.broadcast_to(jnp.max(s, axis=0, keepdims=True), mx_ref.shape[1:])

    def finish(kt, slot):
        m_prev = m_ref[...]
        m_new = jnp.maximum(m_prev, mx_ref[slot])
        alpha = jnp.exp2(m_prev - m_new)
        p = jnp.exp2(s_ref[slot] - m_new[0:1])
        acc_ref[...] = alpha[0:1] * acc_ref[...] + jnp.dot(values(kt), p.astype(BF16), preferred_element_type=F32)
        m_ref[...] = m_new

    if lo is None:
        tiles = hi
        ahead(tiles[0], 0)
        for j, kt in enumerate(tiles):
            if j + 1 < len(tiles):
                ahead(tiles[j + 1], (j + 1) % 2)
            finish(kt, j % 2)
        return

    def body(i, carry):
        kt = lo + 2 * i
        ahead(kt + 1, 1)
        finish(kt, 0)
        ahead(kt + 2, 0)
        finish(kt + 1, 1)
        return carry

    ahead(lo, 0)
    lax.fori_loop(0, (hi - lo + 1) // 2, body, 0)


def _diff_body(q_ref, k_ref, vt_ref, bias_ref, lam_ref, g_ref, o_ref, m_sc, acc_sc, s_sc, mx_sc, *, n_bt):
    L = ATT_TILE
    tq = 2 * L
    last = vt_ref.shape[0] - 1
    qi = pl.program_id(2)
    m_sc[...] = jnp.full(m_sc.shape, NEG, F32)
    acc_sc[...] = jnp.zeros(acc_sc.shape, F32)
    qt = q_ref[...].astype(F32).T.astype(BF16)
    zero = jnp.zeros((A_DQK, tq), BF16)
    q_cat = jnp.concatenate([jnp.concatenate([qt[:A_DQK], zero], axis=0),
                             jnp.concatenate([zero, qt[A_DQK:]], axis=0)], axis=1)

    def scores(kt):
        off = pl.multiple_of(jnp.minimum(kt, last) * L, L)
        k = k_ref[pl.ds(off, L), :]
        d0 = 2 * qi - kt
        bias = jnp.concatenate([bias_ref[jnp.clip(d0 + 1, 0, n_bt - 1)],
                                bias_ref[jnp.clip(d0 + 2, 0, n_bt - 1)]], axis=1)
        s = jnp.dot(k, q_cat, preferred_element_type=F32)
        return jnp.concatenate([s[:, :tq] + bias, s[:, tq:] + bias], axis=1)

    ones = jnp.ones((ONES_ROWS, L), BF16)

    def values(kt):
        return jnp.concatenate([vt_ref[jnp.minimum(kt, last)], ones], axis=0)

    _flash_loop(0, 2 * qi + 2, scores, values, m_sc, acc_sc, s_sc, mx_sc)

    lp = lam_ref[...]
    lam = (jnp.exp(jnp.sum(lp[0:1] * lp[1:2], axis=1, keepdims=True))
           - jnp.exp(jnp.sum(lp[2:3] * lp[3:4], axis=1, keepdims=True)) + LAMBDA_INIT)
    acc = acc_sc[...]
    o0 = acc[:A_DV, :tq] / jnp.maximum(acc[A_DV:A_DV + 1, :tq], 1e-30)
    o1 = acc[:A_DV, tq:] / jnp.maximum(acc[A_DV:A_DV + 1, tq:], 1e-30)
    o = o0 - lam * o1
    g = jnp.concatenate([g_ref[...]] * (tq // LANES), axis=1)
    o = o * lax.rsqrt(jnp.mean(o * o, axis=0, keepdims=True) + LN_EPS) * g
    o_ref[...] = (o * (1.0 - LAMBDA_INIT)).T.astype(o_ref.dtype)


def _diff_attention(ha, v_t, bias, lam_params, subln, b, t):
    L = ATT_TILE
    tq = 2 * L
    n_bt = bias.shape[0]
    return pl.pallas_call(
        functools.partial(_diff_body, n_bt=n_bt),
        grid=(b, A_HEADS, t // tq),
        in_specs=[
            pl.BlockSpec((None, tq, 128), lambda bi, h, qi: (bi, qi, 8 + h)),
            pl.BlockSpec((None, t, 128), lambda bi, h, qi: (bi, 0, 16 + h)),
            pl.BlockSpec((None, None, t // L, A_DV, L), lambda bi, h, qi: (bi, h, 0, 0, 0)),
            pl.BlockSpec((n_bt, None, L, L), lambda bi, h, qi: (0, h, 0, 0)),
            pl.BlockSpec((4, A_DQK), lambda bi, h, qi: (0, 0)),
            pl.BlockSpec((A_DV, LANES), lambda bi, h, qi: (0, 0)),
        ],
        out_specs=pl.BlockSpec((None, tq, 128), lambda bi, h, qi: (bi, qi, h)),
        out_shape=jax.ShapeDtypeStruct((b, t, MIX_W), BF16),
        scratch_shapes=[pltpu.VMEM((8, 2 * tq), F32), pltpu.VMEM((A_DV + ONES_ROWS, 2 * tq), F32),
                        pltpu.VMEM((2, L, 2 * tq), F32), pltpu.VMEM((2, 8, 2 * tq), F32)],
        compiler_params=_cparams(("parallel", "parallel", "arbitrary")),
        name="diff_attention",
    )(ha, ha, v_t, bias, lam_params, subln)


def _compress_body(kv_ref, w1_ref, w2_ref, pe_ref, o_ref):
    nr = o_ref.shape[0]
    first = jnp.zeros((nr, 2 * CMP_HIDDEN), F32)
    second = jnp.zeros((nr, 2 * CMP_HIDDEN), F32)
    for l in range(CMP_STRIDE):
        rows = kv_ref[pl.ds(l, nr, stride=CMP_STRIDE), :].astype(BF16)
        first = first + jnp.dot(rows, w1_ref[l], preferred_element_type=F32)
        second = second + jnp.dot(rows, w1_ref[CMP_STRIDE + l], preferred_element_type=F32)
    pw = jnp.zeros((8, 2 * CMP_HIDDEN), F32)
    for l in range(CMP_LEN):
        pw = pw + jnp.dot(pe_ref[l], w1_ref[l], preferred_element_type=F32)
    second = jnp.concatenate([second[1:], second[:1]], axis=0)
    hdn = jax.nn.gelu(first + second + pw[0:1])
    o_ref[...] = jnp.dot(hdn.astype(BF16), w2_ref[...], preferred_element_type=F32)


def _compress(kv, w1_bd, w2_bd, pe_bd):
    b, t, _ = kv.shape
    nr = t // CMP_STRIDE
    return pl.pallas_call(
        _compress_body,
        grid=(b, B_GROUPS),
        in_specs=[
            pl.BlockSpec((None, t, LANES), lambda bi, gi: (bi, 0, gi)),
            pl.BlockSpec(w1_bd.shape, lambda bi, gi: (0, 0, 0)),
            pl.BlockSpec(w2_bd.shape, lambda bi, gi: (0, 0)),
            pl.BlockSpec(pe_bd.shape, lambda bi, gi: (0, 0, 0)),
        ],
        out_specs=pl.BlockSpec((None, None, nr, 2 * B_DH), lambda bi, gi: (bi, gi, 0, 0)),
        out_shape=jax.ShapeDtypeStruct((b, B_GROUPS, nr, 2 * B_DH), F32),
        compiler_params=_cparams(("parallel", "parallel")),
        name="nsa_compress",
    )(kv, w1_bd, w2_bd, pe_bd)


def _topk_mask_axis0(v, k):
    r, n = v.shape
    iota = lax.broadcasted_iota(jnp.int32, (r, n), 0).astype(F32)

    def body(_, c):
        v, sel = c
        mx = jnp.max(v, axis=0, keepdims=True)
        idx = jnp.min(jnp.where(v == mx, iota, float(r)), axis=0, keepdims=True)
        hit = iota == idx
        return jnp.where(hit, -jnp.inf, v), jnp.where(hit, 1.0, sel)

    _, sel = lax.fori_loop(0, k, body, (v, jnp.zeros((r, n), F32)))
    return sel


def _cmp_body(q_ref, kvc_ref, ov_ref, gl_ref, o_ref, mask_ref, *, n_c, n_top):
    L = ATT_TILE
    qi = pl.program_id(2)
    ncp = kvc_ref.shape[0]
    n_sel = mask_ref.shape[0]
    qt = q_ref[...].astype(F32).T.astype(BF16)
    qh = jnp.concatenate([qt[h * B_DH:(h + 1) * B_DH] for h in range(B_HPG)], axis=1)
    kvc = kvc_ref[...]
    s = jnp.dot(kvc[:, :B_DH].astype(BF16), qh, preferred_element_type=F32)
    tcol = qi * L + lax.broadcasted_iota(jnp.int32, (1, L), 1)
    crow = lax.broadcasted_iota(jnp.int32, (ncp, 1), 0)
    seen = jnp.where(crow < n_c, crow * CMP_STRIDE + (CMP_LEN - 1), jnp.int32(2 ** 30)) <= tcol
    valid = jnp.concatenate([seen] * B_HPG, axis=1)
    s = jnp.where(valid, s, NEG)
    mx = jnp.max(s, axis=0, keepdims=True)
    e = jnp.where(valid, jnp.exp2(s - mx), 0.0)
    p = e / jnp.maximum(jnp.sum(e, axis=0, keepdims=True), 1e-30)
    o = jnp.dot(kvc.T[B_DH:].astype(BF16), p.astype(BF16), preferred_element_type=F32)
    gate = jax.nn.sigmoid(gl_ref[...])
    o = jnp.concatenate([o[:, h * L:(h + 1) * L] * gate[h:h + 1] for h in range(B_HPG)], axis=0)
    o_ref[...] = o.T

    psum = p[:, :L] + p[:, L:2 * L] + p[:, 2 * L:3 * L] + p[:, 3 * L:]
    ov_t = ov_ref[...]
    imp = jnp.zeros((ov_t.shape[0], L), F32)
    rem = psum
    for _ in range(3):
        part = rem.astype(BF16)
        imp = imp + jnp.dot(ov_t, part, preferred_element_type=F32)
        rem = rem - part.astype(F32)
    imp = imp[:n_sel]
    blk = lax.broadcasted_iota(jnp.int32, (n_sel, 1), 0)
    cur = jnp.right_shift(tcol, int(math.log2(SEL_BLOCK)))
    imp = jnp.where(blk * SEL_BLOCK > tcol, -FORCE, imp)
    imp = jnp.where(blk == 0, FORCE, imp)
    imp = jnp.where(blk == cur, FORCE, imp)
    imp = jnp.where(blk == cur - 1, FORCE, imp)
    mask_ref[...] = jnp.where(_topk_mask_axis0(imp, n_top) > 0.0, 0.0, NEG).astype(mask_ref.dtype)


def _nsa_compressed(ha, kvc, overlap_t, glog_t, b, t, n_c, n_top):
    L = ATT_TILE
    ncp = kvc.shape[2]
    n_sel = t // SEL_BLOCK
    return pl.pallas_call(
        functools.partial(_cmp_body, n_c=n_c, n_top=n_top),
        grid=(b, B_GROUPS, t // L),
        in_specs=[
            pl.BlockSpec((None, L, 256), lambda bi, g, qi: (bi, qi, 16 + g)),
            pl.BlockSpec((None, None, ncp, 2 * B_DH), lambda bi, g, qi: (bi, g, 0, 0)),
            pl.BlockSpec(overlap_t.shape, lambda bi, g, qi: (0, 0)),
            pl.BlockSpec((None, None, B_HPG, L), lambda bi, g, qi: (bi, g, 0, qi)),
        ],
        out_specs=[
            pl.BlockSpec((None, L, 256), lambda bi, g, qi: (bi, qi, g)),
            pl.BlockSpec((None, None, n_sel, L), lambda bi, g, qi: (bi, g, 0, qi)),
        ],
        out_shape=[jax.ShapeDtypeStruct((b, t, MIX_W), F32),
                   jax.ShapeDtypeStruct((b, B_GROUPS, n_sel, t), BF16)],
        compiler_params=_cparams(("parallel", "parallel", "arbitrary")),
        name="nsa_compressed_select",
    )(ha, kvc, overlap_t, glog_t)


def _nsa_body(*refs, mode, n_bt):
    if mode == "sel":
        q_ref, kv_ref, vt_ref, bias_ref, gl_ref, prev_ref, mask_ref, hot_ref, o_ref, m_sc, acc_sc, s_sc, mx_sc = refs
    else:
        q_ref, kv_ref, vt_ref, bias_ref, gl_ref, prev_ref, o_ref, m_sc, acc_sc, s_sc, mx_sc = refs
    L = ATT_TILE
    nq = q_ref.shape[0] // L
    qi = pl.program_id(2)
    m_sc[...] = jnp.full(m_sc.shape, NEG, F32)
    acc_sc[...] = jnp.zeros(acc_sc.shape, F32)
    qt = q_ref[...].astype(F32).T.astype(BF16)
    if mode == "sel":
        n_sel = mask_ref.shape[0]
        qm = mask_ref[...]
        if n_sel < B_DH:
            qm = jnp.concatenate([qm, jnp.zeros((B_DH - n_sel, nq * L), BF16)], axis=0)
    else:
        qm = jnp.zeros((B_DH, nq * L), BF16)
    q_aug_t = jnp.concatenate(
        [jnp.concatenate([qt[h * B_DH:(h + 1) * B_DH, s * L:(s + 1) * L], qm[:, s * L:(s + 1) * L]], axis=0)
         for s in range(nq) for h in range(B_HPG)], axis=1)

    last = vt_ref.shape[0] - 1
    ones = jnp.ones((ONES_ROWS, L), BF16)
    k_lanes = lax.broadcasted_iota(jnp.int32, (L, LANES), 1) < B_DH

    def scores(kt):
        off = pl.multiple_of(jnp.clip(kt, 0, last) * L, L)
        idx = jnp.where(kt < 0, 0, nq * qi - kt + nq)
        bias = [bias_ref[jnp.clip(idx + s, 0, n_bt - 1)] for s in range(nq)]
        bias = bias[0] if nq == 1 else jnp.concatenate(bias, axis=1)
        k_aug = kv_ref[pl.ds(off, L), :]
        if mode == "sel":
            k_aug = jnp.where(k_lanes, k_aug, hot_ref[pl.ds(off, L), :])
        return jnp.dot(k_aug, q_aug_t, preferred_element_type=F32) + bias

    def values(kt):
        return jnp.concatenate([vt_ref[jnp.clip(kt, 0, last)], ones], axis=0)

    if mode == "sel":
        _flash_loop(0, nq * qi + nq, scores, values, m_sc, acc_sc, s_sc, mx_sc)
    else:
        n_tiles = n_bt - nq + (nq - 1)
        tiles = [nq * qi + (nq - 1) - j for j in range(n_tiles)]
        _flash_loop(None, tiles, scores, values, m_sc, acc_sc, s_sc, mx_sc)
    acc = acc_sc[...]
    o = acc[:B_DH] / jnp.maximum(acc[B_DH:B_DH + 1], 1e-30)
    gate = jax.nn.sigmoid(gl_ref[...])
    heads = []
    for h in range(B_HPG):
        cols = [o[:, (s * B_HPG + h) * L:(s * B_HPG + h + 1) * L] for s in range(nq)]
        heads.append((cols[0] if nq == 1 else jnp.concatenate(cols, axis=1)) * gate[h:h + 1])
    o_ref[...] = (prev_ref[...] + jnp.concatenate(heads, axis=0).T).astype(o_ref.dtype)


def _nsa_branch(ha, v_t, bias, glog_t, prev, mask_t, hot, b, t, mode, nq, out_dtype):
    L = ATT_TILE
    tq = nq * L
    n_bt = bias.shape[0]
    branch = 1 if mode == "sel" else 2
    in_specs = [
        pl.BlockSpec((None, tq, 256), lambda bi, g, qi: (bi, qi, 16 + g)),
        pl.BlockSpec((None, t, LANES), lambda bi, g, qi: (bi, 0, 40 + 4 * branch + g)),
        pl.BlockSpec((None, None, t // L, B_DH, L), lambda bi, g, qi: (bi, g, 0, 0, 0)),
        pl.BlockSpec((n_bt, None, L, B_HPG * L), lambda bi, g, qi: (0, g, 0, 0)),
        pl.BlockSpec((None, None, B_HPG, tq), lambda bi, g, qi: (bi, g, 0, qi)),
        pl.BlockSpec((None, tq, 256), lambda bi, g, qi: (bi, qi, g)),
    ]
    args = [ha, ha, v_t, bias, glog_t, prev]
    if mode == "sel":
        n_sel = mask_t.shape[2]
        in_specs.append(pl.BlockSpec((None, None, n_sel, tq), lambda bi, g, qi: (bi, g, 0, qi)))
        in_specs.append(pl.BlockSpec((t, LANES), lambda bi, g, qi: (0, 0)))
        args += [mask_t, hot]
    return pl.pallas_call(
        functools.partial(_nsa_body, mode=mode, n_bt=n_bt),
        grid=(b, B_GROUPS, t // tq),
        in_specs=in_specs,
        out_specs=pl.BlockSpec((None, tq, 256), lambda bi, g, qi: (bi, qi, g)),
        out_shape=jax.ShapeDtypeStruct((b, t, MIX_W), out_dtype),
        scratch_shapes=[pltpu.VMEM((8, B_HPG * tq), F32), pltpu.VMEM((B_DH + ONES_ROWS, B_HPG * tq), F32),
                        pltpu.VMEM((2, L, B_HPG * tq), F32), pltpu.VMEM((2, 8, B_HPG * tq), F32)],
        compiler_params=_cparams(("parallel", "parallel", "arbitrary")),
        name="nsa_" + mode,
    )(*args)


def _mem_body(q_ref, k_ref, v_ref, o_ref):
    q = q_ref[...]
    outs = []
    for h in range(M_HEADS):
        sl = slice(h * M_DH, (h + 1) * M_DH)
        s = _dot_nt(q[:, sl], k_ref[:, sl])
        e = jnp.exp2(s - jnp.max(s, axis=1, keepdims=True))
        p = e / jnp.sum(e, axis=1, keepdims=True)
        outs.append(jnp.dot(p.astype(BF16), v_ref[:, sl], preferred_element_type=F32))
    o_ref[...] = jnp.concatenate(outs, axis=1).astype(o_ref.dtype)


def _memory_attention(ha, memkv, b, t, n_mem):
    tq = 512
    w = M_HEADS * M_DH
    return pl.pallas_call(
        _mem_body,
        grid=(b, t // tq),
        in_specs=[
            pl.BlockSpec((None, tq, w), lambda bi, qi: (bi, qi, 0)),
            pl.BlockSpec((n_mem, w), lambda bi, qi: (bi, 0)),
            pl.BlockSpec((n_mem, w), lambda bi, qi: (bi, 1)),
        ],
        out_specs=pl.BlockSpec((None, tq, w), lambda bi, qi: (bi, qi, 0)),
        out_shape=jax.ShapeDtypeStruct((b, t, w), BF16),
        compiler_params=_cparams(("parallel", "arbitrary")),
        name="memory_attention",
    )(ha, memkv, memkv)


def _merge_body(x_ref, wg_ref, oa_ref, ob_ref, om_ref, wb_ref, o_ref, acc_sc):
    n = pl.program_id(1)

    @pl.when(n == 0)
    def _():
        acc_sc[...] = jnp.zeros(acc_sc.shape, F32)

    gate = jax.nn.sigmoid(_dot_nt(x_ref[...], wg_ref[...]))
    branch = jnp.where(n == 0, oa_ref[...], jnp.where(n == 1, ob_ref[...], om_ref[...]))
    acc = acc_sc[...] + gate * jnp.dot(branch, wb_ref[...], preferred_element_type=F32)
    acc_sc[...] = acc
    o_ref[...] = acc.astype(o_ref.dtype)


def _merge(xb, w_gate_t, o_a, o_b, o_m, w_branch):
    n = xb.shape[0]
    tm = 512
    row = lambda i, j: (i, 0)
    return pl.pallas_call(
        _merge_body,
        grid=(n // tm, N_MERGE),
        in_specs=[pl.BlockSpec((tm, D_MODEL), row),
                  pl.BlockSpec((D_MODEL, D_MODEL), lambda i, j: (j, 0))]
                 + [pl.BlockSpec((tm, MIX_W), row)] * 3
                 + [pl.BlockSpec((None, MIX_W, D_MODEL), lambda i, j: (j, 0, 0))],
        out_specs=pl.BlockSpec((tm, D_MODEL), row),
        out_shape=jax.ShapeDtypeStruct((n, D_MODEL), BF16),
        scratch_shapes=[pltpu.VMEM((tm, D_MODEL), F32)],
        compiler_params=_cparams(("parallel", "arbitrary")),
        name="branch_merge",
    )(xb, w_gate_t, o_a, o_b, o_m, w_branch)


def _layer_norm(z, g, b):
    mu = jnp.mean(z, axis=1, keepdims=True)
    zc = z - mu
    var = jnp.mean(zc * zc, axis=1, keepdims=True)
    return zc * lax.rsqrt(var + LN_EPS) * g + b


def _out_ln_body(y_ref, w_ref, x_ref, g_ref, b_ref, o_ref, ob_ref):
    y = jnp.dot(y_ref[...], w_ref[...], preferred_element_type=F32)
    o = _layer_norm(ALPHA * x_ref[...] + y, g_ref[...], b_ref[...])
    o_ref[...] = o
    ob_ref[...] = o.astype(BF16)


def _out_proj_ln(mixed, w_out, x, g, b):
    n = x.shape[0]
    tm = 512
    row = lambda i: (i, 0)
    const = lambda i: (0, 0)
    return pl.pallas_call(
        _out_ln_body,
        grid=(n // tm,),
        in_specs=[pl.BlockSpec((tm, D_MODEL), row), pl.BlockSpec((D_MODEL, D_MODEL), const),
                  pl.BlockSpec((tm, D_MODEL), row), pl.BlockSpec((1, D_MODEL), const),
                  pl.BlockSpec((1, D_MODEL), const)],
        out_specs=[pl.BlockSpec((tm, D_MODEL), row), pl.BlockSpec((tm, D_MODEL), row)],
        out_shape=[jax.ShapeDtypeStruct((n, D_MODEL), F32), jax.ShapeDtypeStruct((n, D_MODEL), BF16)],
        compiler_params=_cparams(("parallel",)),
        name="out_proj_ln1",
    )(mixed, w_out, x, g, b)


def _res_ln_body(x_ref, y_ref, g_ref, b_ref, o_ref):
    o_ref[...] = _layer_norm(ALPHA * x_ref[...] + y_ref[...], g_ref[...], b_ref[...])


def _residual_ln(x, y, g, b):
    n = x.shape[0]
    tm = 512
    row = lambda i: (i, 0)
    const = lambda i: (0, 0)
    return pl.pallas_call(
        _res_ln_body,
        grid=(n // tm,),
        in_specs=[pl.BlockSpec((tm, D_MODEL), row), pl.BlockSpec((tm, D_MODEL), row),
                  pl.BlockSpec((1, D_MODEL), const), pl.BlockSpec((1, D_MODEL), const)],
        out_specs=pl.BlockSpec((tm, D_MODEL), row),
        out_shape=jax.ShapeDtypeStruct((n, D_MODEL), F32),
        compiler_params=_cparams(("parallel",)),
        name="residual_ln2",
    )(x, y, g, b)


def _topk_axis0(v, k):
    r, n = v.shape
    iota = lax.broadcasted_iota(jnp.int32, (r, n), 0).astype(F32)
    slot = lax.broadcasted_iota(jnp.int32, (k, n), 0)

    def body(it, c):
        v, vals, idxs = c
        mx = jnp.max(v, axis=0, keepdims=True)
        idx = jnp.min(jnp.where(v == mx, iota, float(r)), axis=0, keepdims=True)
        v = jnp.where(iota == idx, -jnp.inf, v)
        return v, jnp.where(slot == it, mx, vals), jnp.where(slot == it, idx, idxs)

    _, vals, idxs = lax.fori_loop(0, k, body, (v, jnp.zeros((k, n), F32), jnp.zeros((k, n), F32)))
    return vals, idxs


def _pick_rows(table, pos, k):
    out = jnp.zeros(pos.shape, F32)
    for a in range(k):
        out = out + jnp.where(pos == float(a), table[a:a + 1], 0.0)
    return out


def _route_body(q_ref, keys_ref, ei_ref, ej_ref, g_ref):
    k = PEER_TOPK
    scores = _dot_nt(keys_ref[...], q_ref[...])
    v0, i0 = _topk_axis0(scores[:PEER_NKEYS], k)
    v1, i1 = _topk_axis0(scores[PEER_NKEYS:], k)
    counts = [k // (a + 1) for a in range(k)]
    starts = np.cumsum([0] + counts[:-1])
    pad = (-sum(counts)) % 8
    comb = jnp.concatenate([v0[a:a + 1] + v1[:counts[a]] for a in range(k)]
                           + [jnp.full((pad, v0.shape[1]), -jnp.inf, F32)], axis=0)
    sf, pos = _topk_axis0(comb, k)
    pa = jnp.zeros(pos.shape, F32)
    pb = pos
    for a in range(1, k):
        later = pos >= float(starts[a])
        pa = pa + jnp.where(later, 1.0, 0.0)
        pb = pb - jnp.where(later, float(counts[a - 1]), 0.0)
    ei_ref[...] = _pick_rows(i0, pa, k)
    ej_ref[...] = _pick_rows(i1, pb, k)
    e = jnp.exp(sf - jnp.max(sf, axis=0, keepdims=True))
    g_ref[...] = e / jnp.sum(e, axis=0, keepdims=True)


def _peer_route(q, keys):
    n = q.shape[0]
    tn = 512
    out = jax.ShapeDtypeStruct((PEER_HEADS, PEER_TOPK, n), F32)
    ospec = pl.BlockSpec((None, PEER_TOPK, tn), lambda i, h: (h, 0, i))
    return pl.pallas_call(
        _route_body,
        grid=(n // tn, PEER_HEADS),
        in_specs=[pl.BlockSpec((tn, PEER_DKEY), lambda i, h: (i, h)),
                  pl.BlockSpec((None, 2 * PEER_NKEYS, PEER_DKEY), lambda i, h: (h, 0, 0))],
        out_specs=[ospec, ospec, ospec],
        out_shape=[out, out, out],
        compiler_params=_cparams(("parallel", "arbitrary")),
        name="peer_route",
    )(q, keys)


def _gate_body(ei_ref, ej_ref, g_ref, o_ref):
    tb = ei_ref.shape[0]
    nk = PEER_NKEYS
    iota = lax.broadcasted_iota(jnp.int32, (tb, nk, ei_ref.shape[2]), 1).astype(F32)
    rows = jnp.where(iota == ei_ref[...], 1.0, 0.0).astype(BF16)
    cols = jnp.where(iota == ej_ref[...], g_ref[...], 0.0).astype(BF16)
    gm = lax.dot_general(rows, cols, (((2,), (2,)), ((0,), (0,))), preferred_element_type=F32)
    o_ref[...] = jnp.swapaxes(gm, 0, 1).astype(o_ref.dtype)


def _gate_matrix(ei, ej, g):
    n, _, slots = ei.shape
    tb = 128
    spec = pl.BlockSpec((tb, 1, slots), lambda i: (i, 0, 0))
    return pl.pallas_call(
        _gate_body,
        grid=(n // tb,),
        in_specs=[spec, spec, spec],
        out_specs=pl.BlockSpec((PEER_NKEYS, tb, PEER_NKEYS), lambda i: (0, i, 0)),
        out_shape=jax.ShapeDtypeStruct((PEER_NKEYS, n, PEER_NKEYS), BF16),
        compiler_params=_cparams(("parallel",)),
        name="peer_gate_matrix",
    )(ei, ej, g)


def _expert_body(x_ref, u_ref, g_ref, v_ref, o_ref):
    @pl.when(pl.program_id(1) == 0)
    def _():
        o_ref[...] = jnp.zeros(o_ref.shape, F32)

    hid = _dot_nt(x_ref[...], u_ref[...].astype(BF16))
    gate = jnp.concatenate([g_ref[i] for i in range(g_ref.shape[0])], axis=1)
    act = (jax.nn.gelu(hid) * gate.astype(F32)).astype(BF16)
    o_ref[...] += jnp.dot(act, v_ref[...].astype(BF16), preferred_element_type=F32)


def _peer_experts(xb, u, gm, v):
    n = xb.shape[0]
    ne = u.shape[0]
    tn, te = 1024, 1024
    once = dict(pipeline_mode=pl.Buffered(1))
    return pl.pallas_call(
        _expert_body,
        grid=(n // tn, ne // te),
        in_specs=[pl.BlockSpec((tn, D_MODEL), lambda i, j: (i, 0), **once),
                  pl.BlockSpec((te, D_MODEL), lambda i, j: (j, 0)),
                  pl.BlockSpec((te // PEER_NKEYS, tn, PEER_NKEYS), lambda i, j: (j, i, 0)),
                  pl.BlockSpec((te, D_MODEL), lambda i, j: (j, 0))],
        out_specs=pl.BlockSpec((tn, D_MODEL), lambda i, j: (i, 0), **once),
        out_shape=jax.ShapeDtypeStruct((n, D_MODEL), F32),
        compiler_params=_cparams(("parallel", "arbitrary")),
        name="peer_experts",
    )(xb, u, gm, v)


def _token_mixer(x, mem, w_in, diff_lambda, diff_subln, cmp_pe, cmp_w1, cmp_w2, w_mem_kv, w_branch, rel_bias):
    b, t, _ = x.shape
    n = b * t
    L = ATT_TILE

    w_a, w_mg, w_g = _w_prep(w_in.T)
    ha, hg, xb = _in_proj(x.reshape(n, D_MODEL), w_a, w_g)
    ha3 = ha.reshape(b, t, ha.shape[1])

    tab1d = rel_bias[_rel_bucket(jnp.arange(t))].T * LOG2E
    n_far = min(t // L, REL_MAX_DIST // L + 2)
    bias_a = _bias_tiles(tab1d[:A_HEADS], t, -1, n_far + 1, t, 1)
    bias_b = tab1d[A_HEADS:]
    bias_sel = _bias_tiles(bias_b, t, -SEL_NQ, n_far + SEL_NQ, t, B_HPG)
    n_win = min(t // L, WINDOW // L + 1)
    bias_win = _bias_tiles(bias_b, t, -1, n_win + 1, WINDOW, B_HPG)

    def values_t(v):
        heads, dv = v.shape[2:]
        return v.reshape(b, t // L, L, heads, dv).transpose(0, 3, 1, 4, 2)

    o_a = _diff_attention(ha3, values_t(ha3[:, :, 3072:4096].reshape(b, t, A_HEADS, A_DV)), bias_a, diff_lambda,
                          jnp.broadcast_to(diff_subln[:, None], (A_DV, LANES)), b, t)

    glog_t = hg[:, :48].reshape(b, t, 3, B_GROUPS, B_HPG).transpose(2, 0, 3, 4, 1)
    nr = t // CMP_STRIDE
    w1 = cmp_w1.reshape(2, CMP_LEN, B_DH, CMP_HIDDEN)
    z1 = jnp.zeros_like(w1[0])
    w1_bd = jnp.concatenate([jnp.concatenate([w1[0], z1], axis=2), jnp.concatenate([z1, w1[1]], axis=2)],
                            axis=1).astype(BF16)
    z2 = jnp.zeros_like(cmp_w2[0])
    w2_bd = jnp.concatenate([jnp.concatenate([cmp_w2[0], z2], axis=1), jnp.concatenate([z2, cmp_w2[1]], axis=1)],
                            axis=0).astype(BF16)
    pe_bd = jnp.broadcast_to(jnp.concatenate([cmp_pe[0], cmp_pe[1]], axis=1)[:, None, :],
                             (CMP_LEN, 8, 2 * B_DH)).astype(BF16)
    kvc = _compress(ha3[:, :, 5120:5632].astype(F32), w1_bd, w2_bd, pe_bd)
    n_c = (t - CMP_LEN) // CMP_STRIDE + 1
    n_sel = t // SEL_BLOCK
    cidx = np.arange(nr)[:, None] * CMP_STRIDE + np.arange(CMP_LEN)[None, :]
    overlap = (cidx[:, :, None] // SEL_BLOCK == np.arange(n_sel)[None, None, :]).astype(np.float32).mean(axis=1)
    overlap[n_c:] = 0.0
    overlap_t = np.zeros((LANES, nr), np.float32)
    overlap_t[:n_sel] = overlap.T
    o_c, sel_mask = _nsa_compressed(ha3, kvc, jnp.asarray(overlap_t, BF16), glog_t[0], b, t, n_c,
                                    min(SEL_TOPN, n_sel))
    v_sel = ha3[:, :, 5632:6144].reshape(b, t, B_GROUPS, 2, B_DH)[:, :, :, 1]
    v_win = ha3[:, :, 6144:6656].reshape(b, t, B_GROUPS, 2, B_DH)[:, :, :, 1]
    assert n_sel <= B_DH
    hot = np.zeros((t, LANES), np.float32)
    hot[np.arange(t), B_DH + np.arange(t) // SEL_BLOCK] = 1.0
    o_cs = _nsa_branch(ha3, values_t(v_sel), bias_sel, glog_t[1], o_c, sel_mask, jnp.asarray(hot, BF16), b, t,
                       "sel", SEL_NQ, F32)
    o_b = _nsa_branch(ha3, values_t(v_win), bias_win, glog_t[2], o_cs, None, None, b, t, "win", 1, BF16)

    n_mem = mem.shape[1]
    memkv = _matmul(mem.reshape(b * n_mem, D_MODEL).astype(BF16), w_mem_kv.astype(BF16), BF16,
                    b * n_mem, 1024, "mem_kv_proj")
    o_m = _memory_attention(ha3, memkv, b, t, n_mem)

    flat = lambda o: o.reshape(n, MIX_W)
    return _merge(xb, w_mg, flat(o_a), flat(o_b), flat(o_m), w_branch.astype(BF16))


def _peer(x1, x1b, peer_wq, peer_keys, peer_u, peer_v):
    n = x1.shape[0]
    q = _matmul(x1b, peer_wq.astype(BF16), BF16, 1024, 1024, "peer_query")
    zk = jnp.zeros((PEER_HEADS, PEER_NKEYS, PEER_DKEY // 2), F32)
    keys_bd = jnp.concatenate([jnp.concatenate([peer_keys[:, 0], zk], axis=2),
                               jnp.concatenate([zk, peer_keys[:, 1]], axis=2)], axis=1).astype(BF16)
    ei, ej, gate = _peer_route(q, keys_bd)
    slots = lambda a: a.reshape(PEER_HEADS * PEER_TOPK, n).T.reshape(n, 1, PEER_HEADS * PEER_TOPK)
    gm = _gate_matrix(slots(ei), slots(ej), slots(gate))
    return _peer_experts(x1b, peer_u, gm, peer_v)


def kernel(x, mem, w_in, diff_lambda, diff_subln, cmp_pe, cmp_w1, cmp_w2, w_mem_kv, w_branch, w_out, ln1_g, ln1_b,
           peer_wq, peer_keys, peer_u, peer_v, ln2_g, ln2_b, rel_bias):
    b, t, _ = x.shape
    n = b * t
    for l in range(DEPTH):
        mixed = _token_mixer(x, mem, w_in[l], diff_lambda[l], diff_subln[l], cmp_pe[l], cmp_w1[l], cmp_w2[l],
                             w_mem_kv[l], w_branch[l], rel_bias)
        x1, x1b = _out_proj_ln(mixed, w_out[l].astype(BF16), x.reshape(n, D_MODEL),
                               ln1_g[l].reshape(1, D_MODEL), ln1_b[l].reshape(1, D_MODEL))
        y = _peer(x1, x1b, peer_wq[l], peer_keys[l], peer_u[l], peer_v[l])
        x = _residual_ln(x1, y, ln2_g[l].reshape(1, D_MODEL), ln2_b[l].reshape(1, D_MODEL)).reshape(b, t, D_MODEL)
    return x
```

```python
import functools
import math

import numpy as np
import jax
import jax.numpy as jnp
from jax import lax
from jax.experimental import pallas as pl
from jax.experimental.pallas import tpu as pltpu

F32 = jnp.float32
BF16 = jnp.bfloat16

D_MODEL = 2048
A_HEADS, A_DQK, A_DV = 8, 64, 128
B_HEADS, B_GROUPS, B_HPG, B_DH = 16, 4, 4, 64
CMP_LEN, CMP_STRIDE, CMP_HIDDEN = 32, 16, 256
SEL_BLOCK, SEL_TOPN, WINDOW = 64, 16, 512
M_HEADS, M_DH = 4, 256
REL_BUCKETS, REL_MAX_DIST = 32, 1024
PEER_HEADS, PEER_NKEYS, PEER_DKEY, PEER_TOPK = 8, 128, 256, 16
MIX_W = 1024
LN_EPS = 1e-5
FORCE = 1e9
DEPTH = 1
ALPHA = (2 * DEPTH) ** 0.25
LAMBDA_INIT = 0.8 - 0.6 * math.exp(-0.3 * 0)
LOG2E = math.log2(math.e)

NEG = -1e30
ATT_TILE = 256
LANES = 128
ONES_ROWS = 16
N_MERGE = 3
VMEM_LIMIT = 56 * 1024 * 1024


def _cparams(sem):
    return pltpu.CompilerParams(dimension_semantics=sem, vmem_limit_bytes=VMEM_LIMIT)


def _dot_nt(a, b):
    return lax.dot_general(a, b, (((1,), (1,)), ((), ())), preferred_element_type=F32)


def _mm_body(x_ref, w_ref, o_ref):
    o_ref[...] = jnp.dot(x_ref[...], w_ref[...], preferred_element_type=F32).astype(o_ref.dtype)


def _matmul(x, w, out_dtype, tm, tn, name):
    m, k = x.shape
    n = w.shape[1]
    return pl.pallas_call(
        _mm_body,
        grid=(m // tm, n // tn),
        in_specs=[pl.BlockSpec((tm, k), lambda i, j: (i, 0)), pl.BlockSpec((k, tn), lambda i, j: (0, j))],
        out_specs=pl.BlockSpec((tm, tn), lambda i, j: (i, j)),
        out_shape=jax.ShapeDtypeStruct((m, n), out_dtype),
        compiler_params=_cparams(("parallel", "arbitrary")),
        name=name,
    )(x, w)


def _in_proj_body(x_ref, w_ref, wg_ref, o_ref, og_ref, xb_ref):
    xb = x_ref[...].astype(BF16)
    o_ref[...] = _dot_nt(xb, w_ref[...]).astype(o_ref.dtype)

    @pl.when(pl.program_id(1) == 0)
    def _():
        xb_ref[...] = xb
        og_ref[...] = _dot_nt(xb, wg_ref[...])


def _in_proj(x, w_a, w_g):
    n, k = x.shape
    tm, tn = 512, 1664
    return pl.pallas_call(
        _in_proj_body,
        grid=(n // tm, W_A_COLS // tn),
        in_specs=[pl.BlockSpec((tm, k), lambda i, j: (i, 0)),
                  pl.BlockSpec((tn, k), lambda i, j: (j, 0)),
                  pl.BlockSpec((W_G_COLS, k), lambda i, j: (0, 0))],
        out_specs=[pl.BlockSpec((tm, tn), lambda i, j: (i, j)),
                   pl.BlockSpec((tm, W_G_COLS), lambda i, j: (i, 0)),
                   pl.BlockSpec((tm, k), lambda i, j: (i, 0))],
        out_shape=[jax.ShapeDtypeStruct((n, W_A_COLS), BF16), jax.ShapeDtypeStruct((n, W_G_COLS), F32),
                   jax.ShapeDtypeStruct((n, k), BF16)],
        compiler_params=_cparams(("parallel", "arbitrary")),
        name="in_proj",
    )(x, w_a, w_g)


W_A_COLS = 6656
W_G_COLS = 128


def _w_prep_body(w_ref, wa_ref, wm_ref, wg_ref):
    def put(dst_ref, d0, s0, n, c=1.0):
        val = w_ref[s0:s0 + n, :]
        dst_ref[d0:d0 + n, :] = (val * c if c != 1.0 else val).astype(dst_ref.dtype)

    put(wa_ref, 0, 5680, 1024, M_DH ** -0.5 * LOG2E)
    put(wa_ref, 1024, 0, 1024, A_DQK ** -0.5 * LOG2E)
    put(wa_ref, 2048, 1024, 2048)
    put(wa_ref, 4096, 3072, 1024, B_DH ** -0.5 * LOG2E)
    for br in range(3):
        for g in range(B_GROUPS):
            src = 4096 + br * 2 * B_GROUPS * B_DH + g * B_DH
            dst = 5120 + (br * B_GROUPS + g) * 2 * B_DH
            put(wa_ref, dst, src, B_DH)
            put(wa_ref, dst + B_DH, src + B_GROUPS * B_DH, B_DH)
    put(wm_ref, 0, 6704, N_MERGE * D_MODEL)
    put(wg_ref, 0, 5632, 48)
    wg_ref[48:, :] = jnp.zeros((W_G_COLS - 48, wg_ref.shape[1]), wg_ref.dtype)


def _w_prep(w_in_t):
    c, k = w_in_t.shape
    tk = 256
    col = lambda i: (0, i)
    return pl.pallas_call(
        _w_prep_body,
        grid=(k // tk,),
        in_specs=[pl.BlockSpec((c, tk), col)],
        out_specs=[pl.BlockSpec((W_A_COLS, tk), col), pl.BlockSpec((N_MERGE * D_MODEL, tk), col),
                   pl.BlockSpec((W_G_COLS, tk), col)],
        out_shape=[jax.ShapeDtypeStruct((W_A_COLS, k), BF16), jax.ShapeDtypeStruct((N_MERGE * D_MODEL, k), BF16),
                   jax.ShapeDtypeStruct((W_G_COLS, k), BF16)],
        compiler_params=_cparams(("parallel",)),
        name="w_in_regroup",
    )(w_in_t)


def _rel_bucket(dist):
    n = jnp.maximum(dist, 0)
    max_exact = REL_BUCKETS // 2
    nf = jnp.maximum(n, 1).astype(jnp.float32)
    large = max_exact + (jnp.log(nf / max_exact) / math.log(REL_MAX_DIST / max_exact)
                         * (REL_BUCKETS - max_exact)).astype(jnp.int32)
    large = jnp.minimum(large, REL_BUCKETS - 1)
    return jnp.where(n < max_exact, n, large)


def _bias_tiles(tab1d, t, first, n_tiles, max_dist, hpr):
    L = ATT_TILE
    m = np.arange(2 * L)
    off = np.where(m <= L, m, m - 2 * L)
    d = (first + np.arange(n_tiles))[:, None] * L + off[None, :]
    ok = (d >= 0) & (d < max_dist)
    h = tab1d.shape[0]
    rp = jnp.where(ok[:, None], tab1d.T[np.clip(d, 0, t - 1)].transpose(0, 2, 1), NEG)
    return pl.pallas_call(
        functools.partial(_toeplitz_body, hpr=hpr),
        grid=(n_tiles,),
        in_specs=[pl.BlockSpec((None, h, 1, 2 * L), lambda c: (c, 0, 0, 0))],
        out_specs=pl.BlockSpec((None, h // hpr, L, hpr * L), lambda c: (c, 0, 0, 0)),
        out_shape=jax.ShapeDtypeStruct((n_tiles, h // hpr, L, hpr * L), F32),
        compiler_params=_cparams(("parallel",)),
        name="bias_tiles",
    )(rp.reshape(n_tiles, h, 1, 2 * L))


def _toeplitz_body(rp_ref, o_ref, *, hpr):
    L = o_ref.shape[1]
    for hd in range(rp_ref.shape[0]):
        rows = jnp.broadcast_to(rp_ref[hd], (L, 2 * L))
        tile = pltpu.roll(rows, 0, 1, stride=1, stride_axis=0)[:, :L]
        o_ref[hd // hpr, :, (hd % hpr) * L:(hd % hpr + 1) * L] = tile


def _flash_loop(lo, hi, scores, values, m_ref, acc_ref, s_ref, mx_ref):
    def ahead(kt, slot):
        s = scores(kt)
        s_ref[slot] = s
        mx_ref[slot] = jnp.broadcast_to(jnp.max(s, axis=0, keepdims=True), mx_ref.shape[1:])

    def finish(kt, slot):
        m_prev = m_ref[...]
        m_new = jnp.maximum(m_prev, mx_ref[slot])
        alpha = jnp.exp2(m_prev - m_new)
        p = jnp.exp2(s_ref[slot] - m_new[0:1])
        acc_ref[...] = alpha[0:1] * acc_ref[...] + jnp.dot(values(kt), p.astype(BF16), preferred_element_type=F32)
        m_ref[...] = m_new

    if lo is None:
        tiles = hi
        ahead(tiles[0], 0)
        for j, kt in enumerate(tiles):
            if j + 1 < len(tiles):
                ahead(tiles[j + 1], (j + 1) % 2)
            finish(kt, j % 2)
        return

    def body(i, carry):
        kt = lo + 2 * i
        ahead(kt + 1, 1)
        finish(kt, 0)
        ahead(kt + 2, 0)
        finish(kt + 1, 1)
        return carry

    ahead(lo, 0)
    lax.fori_loop(0, (hi - lo + 1) // 2, body, 0)


def _diff_body(q_ref, k_ref, vt_ref, bias_ref, lam_ref, g_ref, o_ref, m_sc, acc_sc, s_sc, mx_sc, *, n_bt):
    L = ATT_TILE
    tq = 2 * L
    last = vt_ref.shape[0] - 1
    qi = pl.program_id(2)
    m_sc[...] = jnp.full(m_sc.shape, NEG, F32)
    acc_sc[...] = jnp.zeros(acc_sc.shape, F32)
    qt = q_ref[...].astype(F32).T.astype(BF16)
    zero = jnp.zeros((A_DQK, tq), BF16)
    q_cat = jnp.concatenate([jnp.concatenate([qt[:A_DQK], zero], axis=0),
                             jnp.concatenate([zero, qt[A_DQK:]], axis=0)], axis=1)

    def scores(kt):
        off = pl.multiple_of(jnp.minimum(kt, last) * L, L)
        k = k_ref[pl.ds(off, L), :]
        d0 = 2 * qi - kt
        bias = jnp.concatenate([bias_ref[jnp.clip(d0 + 1, 0, n_bt - 1)],
                                bias_ref[jnp.clip(d0 + 2, 0, n_bt - 1)]], axis=1)
        s = jnp.dot(k, q_cat, preferred_element_type=F32)
        return jnp.concatenate([s[:, :tq] + bias, s[:, tq:] + bias], axis=1)

    ones = jnp.ones((ONES_ROWS, L), BF16)

    def values(kt):
        return jnp.concatenate([vt_ref[jnp.minimum(kt, last)], ones], axis=0)

    _flash_loop(0, 2 * qi + 2, scores, values, m_sc, acc_sc, s_sc, mx_sc)

    lp = lam_ref[...]
    lam = (jnp.exp(jnp.sum(lp[0:1] * lp[1:2], axis=1, keepdims=True))
           - jnp.exp(jnp.sum(lp[2:3] * lp[3:4], axis=1, keepdims=True)) + LAMBDA_INIT)
    acc = acc_sc[...]
    o0 = acc[:A_DV, :tq] / jnp.maximum(acc[A_DV:A_DV + 1, :tq], 1e-30)
    o1 = acc[:A_DV, tq:] / jnp.maximum(acc[A_DV:A_DV + 1, tq:], 1e-30)
    o = o0 - lam * o1
    g = jnp.concatenate([g_ref[...]] * (tq // LANES), axis=1)
    o = o * lax.rsqrt(jnp.mean(o * o, axis=0, keepdims=True) + LN_EPS) * g
    o_ref[...] = (o * (1.0 - LAMBDA_INIT)).T.astype(o_ref.dtype)


def _diff_attention(ha, v_t, bias, lam_params, subln, b, t):
    L = ATT_TILE
    tq = 2 * L
    n_bt = bias.shape[0]
    return pl.pallas_call(
        functools.partial(_diff_body, n_bt=n_bt),
        grid=(b, A_HEADS, t // tq),
        in_specs=[
            pl.BlockSpec((None, tq, 128), lambda bi, h, qi: (bi, qi, 8 + h)),
            pl.BlockSpec((None, t, 128), lambda bi, h, qi: (bi, 0, 16 + h)),
            pl.BlockSpec((None, None, t // L, A_DV, L), lambda bi, h, qi: (bi, h, 0, 0, 0)),
            pl.BlockSpec((n_bt, None, L, L), lambda bi, h, qi: (0, h, 0, 0)),
            pl.BlockSpec((4, A_DQK), lambda bi, h, qi: (0, 0)),
            pl.BlockSpec((A_DV, LANES), lambda bi, h, qi: (0, 0)),
        ],
        out_specs=pl.BlockSpec((None, tq, 128), lambda bi, h, qi: (bi, qi, h)),
        out_shape=jax.ShapeDtypeStruct((b, t, MIX_W), BF16),
        scratch_shapes=[pltpu.VMEM((8, 2 * tq), F32), pltpu.VMEM((A_DV + ONES_ROWS, 2 * tq), F32),
                        pltpu.VMEM((2, L, 2 * tq), F32), pltpu.VMEM((2, 8, 2 * tq), F32)],
        compiler_params=_cparams(("parallel", "parallel", "arbitrary")),
        name="diff_attention",
    )(ha, ha, v_t, bias, lam_params, subln)


def _compress_body(kv_ref, w1_ref, w2_ref, pe_ref, o_ref):
    nr = o_ref.shape[0]
    first = jnp.zeros((nr, 2 * CMP_HIDDEN), F32)
    second = jnp.zeros((nr, 2 * CMP_HIDDEN), F32)
    for l in range(CMP_STRIDE):
        rows = kv_ref[pl.ds(l, nr, stride=CMP_STRIDE), :].astype(BF16)
        first = first + jnp.dot(rows, w1_ref[l], preferred_element_type=F32)
        second = second + jnp.dot(rows, w1_ref[CMP_STRIDE + l], preferred_element_type=F32)
    pw = jnp.zeros((8, 2 * CMP_HIDDEN), F32)
    for l in range(CMP_LEN):
        pw = pw + jnp.dot(pe_ref[l], w1_ref[l], preferred_element_type=F32)
    second = jnp.concatenate([second[1:], second[:1]], axis=0)
    hdn = jax.nn.gelu(first + second + pw[0:1])
    o_ref[...] = jnp.dot(hdn.astype(BF16), w2_ref[...], preferred_element_type=F32)


def _compress(kv, w1_bd, w2_bd, pe_bd):
    b, t, _ = kv.shape
    nr = t // CMP_STRIDE
    return pl.pallas_call(
        _compress_body,
        grid=(b, B_GROUPS),
        in_specs=[
            pl.BlockSpec((None, t, LANES), lambda bi, gi: (bi, 0, gi)),
            pl.BlockSpec(w1_bd.shape, lambda bi, gi: (0, 0, 0)),
            pl.BlockSpec(w2_bd.shape, lambda bi, gi: (0, 0)),
            pl.BlockSpec(pe_bd.shape, lambda bi, gi: (0, 0, 0)),
        ],
        out_specs=pl.BlockSpec((None, None, nr, 2 * B_DH), lambda bi, gi: (bi, gi, 0, 0)),
        out_shape=jax.ShapeDtypeStruct((b, B_GROUPS, nr, 2 * B_DH), F32),
        compiler_params=_cparams(("parallel", "parallel")),
        name="nsa_compress",
    )(kv, w1_bd, w2_bd, pe_bd)


def _topk_mask_axis0(v, k):
    r, n = v.shape
    iota = lax.broadcasted_iota(jnp.int32, (r, n), 0).astype(F32)

    def body(_, c):
        v, sel = c
        mx = jnp.max(v, axis=0, keepdims=True)
        idx = jnp.min(jnp.where(v == mx, iota, float(r)), axis=0, keepdims=True)
        hit = iota == idx
        return jnp.where(hit, -jnp.inf, v), jnp.where(hit, 1.0, sel)

    _, sel = lax.fori_loop(0, k, body, (v, jnp.zeros((r, n), F32)))
    return sel


def _compressed_branch(qt, kvc, ov_t, qi, n_c, n_top, n_sel):
    L = ATT_TILE
    ncp = kvc.shape[0]
    qh = jnp.concatenate([qt[h * B_DH:(h + 1) * B_DH] for h in range(B_HPG)], axis=1)
    s = jnp.dot(kvc[:, :B_DH].astype(BF16), qh, preferred_element_type=F32)
    tcol = qi * L + lax.broadcasted_iota(jnp.int32, (1, L), 1)
    crow = lax.broadcasted_iota(jnp.int32, (ncp, 1), 0)
    seen = jnp.where(crow < n_c, crow * CMP_STRIDE + (CMP_LEN - 1), jnp.int32(2 ** 30)) <= tcol
    valid = jnp.concatenate([seen] * B_HPG, axis=1)
    s = jnp.where(valid, s, NEG)
    mx = jnp.max(s, axis=0, keepdims=True)
    e = jnp.where(valid, jnp.exp2(s - mx), 0.0)
    p = e / jnp.maximum(jnp.sum(e, axis=0, keepdims=True), 1e-30)
    o = jnp.dot(kvc.T[B_DH:].astype(BF16), p.astype(BF16), preferred_element_type=F32)

    psum = p[:, :L] + p[:, L:2 * L] + p[:, 2 * L:3 * L] + p[:, 3 * L:]
    imp = jnp.zeros((ov_t.shape[0], L), F32)
    rem = psum
    for _ in range(3):
        part = rem.astype(BF16)
        imp = imp + jnp.dot(ov_t, part, preferred_element_type=F32)
        rem = rem - part.astype(F32)
    imp = imp[:n_sel]
    blk = lax.broadcasted_iota(jnp.int32, (n_sel, 1), 0)
    cur = jnp.right_shift(tcol, int(math.log2(SEL_BLOCK)))
    imp = jnp.where(blk * SEL_BLOCK > tcol, -FORCE, imp)
    imp = jnp.where(blk == 0, FORCE, imp)
    imp = jnp.where(blk == cur, FORCE, imp)
    imp = jnp.where(blk == cur - 1, FORCE, imp)
    return o, jnp.where(_topk_mask_axis0(imp, n_top) > 0.0, 0.0, NEG)


def _nsa_body(q_ref, kvc_ref, ov_ref, gl_ref, kvs_ref, vts_ref, bs_ref, hot_ref, kvw_ref, vtw_ref, bw_ref, o_ref,
              m_sc, acc_sc, s_sc, mx_sc, *, n_c, n_top, n_sel):
    L = ATT_TILE
    qi = pl.program_id(2)
    last = vts_ref.shape[0] - 1
    qt = q_ref[...].astype(F32).T.astype(BF16)
    gates = jax.nn.sigmoid(gl_ref[...])
    ones = jnp.ones((ONES_ROWS, L), BF16)
    k_lanes = lax.broadcasted_iota(jnp.int32, (L, LANES), 1) < B_DH

    def gated(o, branch):
        return jnp.concatenate([o[:, h * L:(h + 1) * L] * gates[branch, h:h + 1] for h in range(B_HPG)], axis=0)

    def attend(qm, kv_ref, vt_ref, bias_ref, use_hot, tiles):
        m_sc[...] = jnp.full(m_sc.shape, NEG, F32)
        acc_sc[...] = jnp.zeros(acc_sc.shape, F32)
        q_aug_t = jnp.concatenate([jnp.concatenate([qt[h * B_DH:(h + 1) * B_DH], qm], axis=0)
                                   for h in range(B_HPG)], axis=1)
        n_bt = bias_ref.shape[0]

        def scores(kt):
            off = pl.multiple_of(jnp.clip(kt, 0, last) * L, L)
            idx = jnp.where(kt < 0, 0, jnp.clip(qi - kt + 1, 0, n_bt - 1))
            k_aug = kv_ref[pl.ds(off, L), :]
            if use_hot:
                k_aug = jnp.where(k_lanes, k_aug, hot_ref[pl.ds(off, L), :])
            return jnp.dot(k_aug, q_aug_t, preferred_element_type=F32) + bias_ref[idx]

        def values(kt):
            return jnp.concatenate([vt_ref[jnp.clip(kt, 0, last)], ones], axis=0)

        if tiles is None:
            _flash_loop(0, qi + 1, scores, values, m_sc, acc_sc, s_sc, mx_sc)
        else:
            _flash_loop(None, tiles, scores, values, m_sc, acc_sc, s_sc, mx_sc)
        acc = acc_sc[...]
        return acc[:B_DH] / jnp.maximum(acc[B_DH:B_DH + 1], 1e-30)

    o_cmp, sel = _compressed_branch(qt, kvc_ref[...], ov_ref[...], qi, n_c, n_top, n_sel)
    total = gated(o_cmp, 0)
    qm = sel.astype(BF16)
    if n_sel < B_DH:
        qm = jnp.concatenate([qm, jnp.zeros((B_DH - n_sel, L), BF16)], axis=0)
    total = total + gated(attend(qm, kvs_ref, vts_ref, bs_ref, True, None), 1)
    window_tiles = [qi - j for j in range(bw_ref.shape[0] - 1)]
    total = total + gated(attend(jnp.zeros((B_DH, L), BF16), kvw_ref, vtw_ref, bw_ref, False, window_tiles), 2)
    o_ref[...] = total.T.astype(o_ref.dtype)


def _nsa_attention(ha, kvc, overlap_t, glog_t, vt_sel, bias_sel, hot, vt_win, bias_win, b, t, n_c, n_top):
    L = ATT_TILE
    ncp = kvc.shape[2]
    n_sel = t // SEL_BLOCK
    kv_spec = lambda br: pl.BlockSpec((None, t, LANES), lambda bi, g, qi: (bi, 0, 40 + 4 * br + g))
    vt_spec = pl.BlockSpec((None, None, t // L, B_DH, L), lambda bi, g, qi: (bi, g, 0, 0, 0))
    bias_spec = lambda bias: pl.BlockSpec((bias.shape[0], None, L, B_HPG * L), lambda bi, g, qi: (0, g, 0, 0))
    return pl.pallas_call(
        functools.partial(_nsa_body, n_c=n_c, n_top=n_top, n_sel=n_sel),
        grid=(b, B_GROUPS, t // L),
        in_specs=[
            pl.BlockSpec((None, L, 256), lambda bi, g, qi: (bi, qi, 16 + g)),
            pl.BlockSpec((None, None, ncp, 2 * B_DH), lambda bi, g, qi: (bi, g, 0, 0)),
            pl.BlockSpec(overlap_t.shape, lambda bi, g, qi: (0, 0)),
            pl.BlockSpec((3, None, None, B_HPG, L), lambda bi, g, qi: (0, bi, g, 0, qi)),
            kv_spec(1), vt_spec, bias_spec(bias_sel),
            pl.BlockSpec((t, LANES), lambda bi, g, qi: (0, 0)),
            kv_spec(2), vt_spec, bias_spec(bias_win),
        ],
        out_specs=pl.BlockSpec((None, L, 256), lambda bi, g, qi: (bi, qi, g)),
        out_shape=jax.ShapeDtypeStruct((b, t, MIX_W), BF16),
        scratch_shapes=[pltpu.VMEM((8, B_HPG * L), F32), pltpu.VMEM((B_DH + ONES_ROWS, B_HPG * L), F32),
                        pltpu.VMEM((2, L, B_HPG * L), F32), pltpu.VMEM((2, 8, B_HPG * L), F32)],
        compiler_params=_cparams(("parallel", "parallel", "arbitrary")),
        name="nsa_attention",
    )(ha, kvc, overlap_t, glog_t, ha, vt_sel, bias_sel, hot, ha, vt_win, bias_win)


def _mem_body(q_ref, k_ref, v_ref, o_ref):
    q = q_ref[...]
    outs = []
    for h in range(M_HEADS):
        sl = slice(h * M_DH, (h + 1) * M_DH)
        s = _dot_nt(q[:, sl], k_ref[:, sl])
        e = jnp.exp2(s - jnp.max(s, axis=1, keepdims=True))
        p = e / jnp.sum(e, axis=1, keepdims=True)
        outs.append(jnp.dot(p.astype(BF16), v_ref[:, sl], preferred_element_type=F32))
    o_ref[...] = jnp.concatenate(outs, axis=1).astype(o_ref.dtype)


def _memory_attention(ha, memkv, b, t, n_mem):
    tq = 512
    w = M_HEADS * M_DH
    return pl.pallas_call(
        _mem_body,
        grid=(b, t // tq),
        in_specs=[
            pl.BlockSpec((None, tq, w), lambda bi, qi: (bi, qi, 0)),
            pl.BlockSpec((n_mem, w), lambda bi, qi: (bi, 0)),
            pl.BlockSpec((n_mem, w), lambda bi, qi: (bi, 1)),
        ],
        out_specs=pl.BlockSpec((None, tq, w), lambda bi, qi: (bi, qi, 0)),
        out_shape=jax.ShapeDtypeStruct((b, t, w), BF16),
        compiler_params=_cparams(("parallel", "arbitrary")),
        name="memory_attention",
    )(ha, memkv, memkv)


def _merge_body(x_ref, wg_ref, oa_ref, ob_ref, om_ref, wb_ref, o_ref, acc_sc):
    n = pl.program_id(1)

    @pl.when(n == 0)
    def _():
        acc_sc[...] = jnp.zeros(acc_sc.shape, F32)

    gate = jax.nn.sigmoid(_dot_nt(x_ref[...], wg_ref[...]))
    branch = jnp.where(n == 0, oa_ref[...], jnp.where(n == 1, ob_ref[...], om_ref[...]))
    acc = acc_sc[...] + gate * jnp.dot(branch, wb_ref[...], preferred_element_type=F32)
    acc_sc[...] = acc
    o_ref[...] = acc.astype(o_ref.dtype)


def _merge(xb, w_gate_t, o_a, o_b, o_m, w_branch):
    n = xb.shape[0]
    tm = 512
    row = lambda i, j: (i, 0)
    return pl.pallas_call(
        _merge_body,
        grid=(n // tm, N_MERGE),
        in_specs=[pl.BlockSpec((tm, D_MODEL), row),
                  pl.BlockSpec((D_MODEL, D_MODEL), lambda i, j: (j, 0))]
                 + [pl.BlockSpec((tm, MIX_W), row)] * 3
                 + [pl.BlockSpec((None, MIX_W, D_MODEL), lambda i, j: (j, 0, 0))],
        out_specs=pl.BlockSpec((tm, D_MODEL), row),
        out_shape=jax.ShapeDtypeStruct((n, D_MODEL), BF16),
        scratch_shapes=[pltpu.VMEM((tm, D_MODEL), F32)],
        compiler_params=_cparams(("parallel", "arbitrary")),
        name="branch_merge",
    )(xb, w_gate_t, o_a, o_b, o_m, w_branch)


def _layer_norm(z, g, b):
    mu = jnp.mean(z, axis=1, keepdims=True)
    zc = z - mu
    var = jnp.mean(zc * zc, axis=1, keepdims=True)
    return zc * lax.rsqrt(var + LN_EPS) * g + b


def _out_ln_body(y_ref, w_ref, x_ref, g_ref, b_ref, o_ref, ob_ref):
    y = jnp.dot(y_ref[...], w_ref[...], preferred_element_type=F32)
    o = _layer_norm(ALPHA * x_ref[...] + y, g_ref[...], b_ref[...])
    o_ref[...] = o
    ob_ref[...] = o.astype(BF16)


def _out_proj_ln(mixed, w_out, x, g, b):
    n = x.shape[0]
    tm = 512
    row = lambda i: (i, 0)
    const = lambda i: (0, 0)
    return pl.pallas_call(
        _out_ln_body,
        grid=(n // tm,),
        in_specs=[pl.BlockSpec((tm, D_MODEL), row), pl.BlockSpec((D_MODEL, D_MODEL), const),
                  pl.BlockSpec((tm, D_MODEL), row), pl.BlockSpec((1, D_MODEL), const),
                  pl.BlockSpec((1, D_MODEL), const)],
        out_specs=[pl.BlockSpec((tm, D_MODEL), row), pl.BlockSpec((tm, D_MODEL), row)],
        out_shape=[jax.ShapeDtypeStruct((n, D_MODEL), F32), jax.ShapeDtypeStruct((n, D_MODEL), BF16)],
        compiler_params=_cparams(("parallel",)),
        name="out_proj_ln1",
    )(mixed, w_out, x, g, b)


def _res_ln_body(x_ref, y_ref, g_ref, b_ref, o_ref):
    o_ref[...] = _layer_norm(ALPHA * x_ref[...] + y_ref[...], g_ref[...], b_ref[...])


def _residual_ln(x, y, g, b):
    n = x.shape[0]
    tm = 512
    row = lambda i: (i, 0)
    const = lambda i: (0, 0)
    return pl.pallas_call(
        _res_ln_body,
        grid=(n // tm,),
        in_specs=[pl.BlockSpec((tm, D_MODEL), row), pl.BlockSpec((tm, D_MODEL), row),
                  pl.BlockSpec((1, D_MODEL), const), pl.BlockSpec((1, D_MODEL), const)],
        out_specs=pl.BlockSpec((tm, D_MODEL), row),
        out_shape=jax.ShapeDtypeStruct((n, D_MODEL), F32),
        compiler_params=_cparams(("parallel",)),
        name="residual_ln2",
    )(x, y, g, b)


def _topk_axis0(v, k):
    r, n = v.shape
    iota = lax.broadcasted_iota(jnp.int32, (r, n), 0).astype(F32)
    slot = lax.broadcasted_iota(jnp.int32, (k, n), 0)

    def body(it, c):
        v, vals, idxs = c
        mx = jnp.max(v, axis=0, keepdims=True)
        idx = jnp.min(jnp.where(v == mx, iota, float(r)), axis=0, keepdims=True)
        v = jnp.where(iota == idx, -jnp.inf, v)
        return v, jnp.where(slot == it, mx, vals), jnp.where(slot == it, idx, idxs)

    _, vals, idxs = lax.fori_loop(0, k, body, (v, jnp.zeros((k, n), F32), jnp.zeros((k, n), F32)))
    return vals, idxs


def _pick_rows(table, pos, k):
    out = jnp.zeros(pos.shape, F32)
    for a in range(k):
        out = out + jnp.where(pos == float(a), table[a:a + 1], 0.0)
    return out


def _route_body(q_ref, keys_ref, ei_ref, ej_ref, g_ref):
    k = PEER_TOPK
    scores = _dot_nt(keys_ref[...], q_ref[...])
    v0, i0 = _topk_axis0(scores[:PEER_NKEYS], k)
    v1, i1 = _topk_axis0(scores[PEER_NKEYS:], k)
    counts = [k // (a + 1) for a in range(k)]
    starts = np.cumsum([0] + counts[:-1])
    pad = (-sum(counts)) % 8
    comb = jnp.concatenate([v0[a:a + 1] + v1[:counts[a]] for a in range(k)]
                           + [jnp.full((pad, v0.shape[1]), -jnp.inf, F32)], axis=0)
    sf, pos = _topk_axis0(comb, k)
    pa = jnp.zeros(pos.shape, F32)
    pb = pos
    for a in range(1, k):
        later = pos >= float(starts[a])
        pa = pa + jnp.where(later, 1.0, 0.0)
        pb = pb - jnp.where(later, float(counts[a - 1]), 0.0)
    ei_ref[...] = _pick_rows(i0, pa, k)
    ej_ref[...] = _pick_rows(i1, pb, k)
    e = jnp.exp(sf - jnp.max(sf, axis=0, keepdims=True))
    g_ref[...] = e / jnp.sum(e, axis=0, keepdims=True)


def _peer_route(q, keys):
    n = q.shape[0]
    tn = 512
    out = jax.ShapeDtypeStruct((PEER_HEADS, PEER_TOPK, n), F32)
    ospec = pl.BlockSpec((None, PEER_TOPK, tn), lambda i, h: (h, 0, i))
    return pl.pallas_call(
        _route_body,
        grid=(n // tn, PEER_HEADS),
        in_specs=[pl.BlockSpec((tn, PEER_DKEY), lambda i, h: (i, h)),
                  pl.BlockSpec((None, 2 * PEER_NKEYS, PEER_DKEY), lambda i, h: (h, 0, 0))],
        out_specs=[ospec, ospec, ospec],
        out_shape=[out, out, out],
        compiler_params=_cparams(("parallel", "arbitrary")),
        name="peer_route",
    )(q, keys)


def _gate_body(ei_ref, ej_ref, g_ref, o_ref):
    tb = ei_ref.shape[0]
    nk = PEER_NKEYS
    iota = lax.broadcasted_iota(jnp.int32, (tb, nk, ei_ref.shape[2]), 1).astype(F32)
    rows = jnp.where(iota == ei_ref[...], 1.0, 0.0).astype(BF16)
    cols = jnp.where(iota == ej_ref[...], g_ref[...], 0.0).astype(BF16)
    gm = lax.dot_general(rows, cols, (((2,), (2,)), ((0,), (0,))), preferred_element_type=F32)
    o_ref[...] = jnp.swapaxes(gm, 0, 1).astype(o_ref.dtype)


def _gate_matrix(ei, ej, g):
    n, _, slots = ei.shape
    tb = 128
    spec = pl.BlockSpec((tb, 1, slots), lambda i: (i, 0, 0))
    return pl.pallas_call(
        _gate_body,
        grid=(n // tb,),
        in_specs=[spec, spec, spec],
        out_specs=pl.BlockSpec((PEER_NKEYS, tb, PEER_NKEYS), lambda i: (0, i, 0)),
        out_shape=jax.ShapeDtypeStruct((PEER_NKEYS, n, PEER_NKEYS), BF16),
        compiler_params=_cparams(("parallel",)),
        name="peer_gate_matrix",
    )(ei, ej, g)


def _expert_body(x_ref, u_ref, g_ref, v_ref, o_ref):
    @pl.when(pl.program_id(1) == 0)
    def _():
        o_ref[...] = jnp.zeros(o_ref.shape, F32)

    hid = _dot_nt(x_ref[...], u_ref[...].astype(BF16))
    gate = jnp.concatenate([g_ref[i] for i in range(g_ref.shape[0])], axis=1)
    act = (jax.nn.gelu(hid) * gate.astype(F32)).astype(BF16)
    o_ref[...] += jnp.dot(act, v_ref[...].astype(BF16), preferred_element_type=F32)


def _peer_experts(xb, u, gm, v):
    n = xb.shape[0]
    ne = u.shape[0]
    tn, te = 1024, 1024
    once = dict(pipeline_mode=pl.Buffered(1))
    return pl.pallas_call(
        _expert_body,
        grid=(n // tn, ne // te),
        in_specs=[pl.BlockSpec((tn, D_MODEL), lambda i, j: (i, 0), **once),
                  pl.BlockSpec((te, D_MODEL), lambda i, j: (j, 0)),
                  pl.BlockSpec((te // PEER_NKEYS, tn, PEER_NKEYS), lambda i, j: (j, i, 0)),
                  pl.BlockSpec((te, D_MODEL), lambda i, j: (j, 0))],
        out_specs=pl.BlockSpec((tn, D_MODEL), lambda i, j: (i, 0), **once),
        out_shape=jax.ShapeDtypeStruct((n, D_MODEL), F32),
        compiler_params=_cparams(("parallel", "arbitrary")),
        name="peer_experts",
    )(xb, u, gm, v)


def _token_mixer(x, mem, w_in, diff_lambda, diff_subln, cmp_pe, cmp_w1, cmp_w2, w_mem_kv, w_branch, rel_bias):
    b, t, _ = x.shape
    n = b * t
    L = ATT_TILE

    w_a, w_mg, w_g = _w_prep(w_in.T)
    ha, hg, xb = _in_proj(x.reshape(n, D_MODEL), w_a, w_g)
    ha3 = ha.reshape(b, t, ha.shape[1])

    tab1d = rel_bias[_rel_bucket(jnp.arange(t))].T * LOG2E
    n_far = min(t // L, REL_MAX_DIST // L + 2)
    bias_a = _bias_tiles(tab1d[:A_HEADS], t, -1, n_far + 1, t, 1)
    bias_b = tab1d[A_HEADS:]
    bias_sel = _bias_tiles(bias_b, t, -1, n_far + 1, t, B_HPG)
    n_win = min(t // L, WINDOW // L + 1)
    bias_win = _bias_tiles(bias_b, t, -1, n_win + 1, WINDOW, B_HPG)

    def values_t(v):
        heads, dv = v.shape[2:]
        return v.reshape(b, t // L, L, heads, dv).transpose(0, 3, 1, 4, 2)

    o_a = _diff_attention(ha3, values_t(ha3[:, :, 3072:4096].reshape(b, t, A_HEADS, A_DV)), bias_a, diff_lambda,
                          jnp.broadcast_to(diff_subln[:, None], (A_DV, LANES)), b, t)

    glog_t = hg[:, :48].reshape(b, t, 3, B_GROUPS, B_HPG).transpose(2, 0, 3, 4, 1)
    nr = t // CMP_STRIDE
    w1 = cmp_w1.reshape(2, CMP_LEN, B_DH, CMP_HIDDEN)
    z1 = jnp.zeros_like(w1[0])
    w1_bd = jnp.concatenate([jnp.concatenate([w1[0], z1], axis=2), jnp.concatenate([z1, w1[1]], axis=2)],
                            axis=1).astype(BF16)
    z2 = jnp.zeros_like(cmp_w2[0])
    w2_bd = jnp.concatenate([jnp.concatenate([cmp_w2[0], z2], axis=1), jnp.concatenate([z2, cmp_w2[1]], axis=1)],
                            axis=0).astype(BF16)
    pe_bd = jnp.broadcast_to(jnp.concatenate([cmp_pe[0], cmp_pe[1]], axis=1)[:, None, :],
                             (CMP_LEN, 8, 2 * B_DH)).astype(BF16)
    kvc = _compress(ha3[:, :, 5120:5632].astype(F32), w1_bd, w2_bd, pe_bd)
    n_c = (t - CMP_LEN) // CMP_STRIDE + 1
    n_sel = t // SEL_BLOCK
    cidx = np.arange(nr)[:, None] * CMP_STRIDE + np.arange(CMP_LEN)[None, :]
    overlap = (cidx[:, :, None] // SEL_BLOCK == np.arange(n_sel)[None, None, :]).astype(np.float32).mean(axis=1)
    overlap[n_c:] = 0.0
    overlap_t = np.zeros((LANES, nr), np.float32)
    overlap_t[:n_sel] = overlap.T
    v_sel = ha3[:, :, 5632:6144].reshape(b, t, B_GROUPS, 2, B_DH)[:, :, :, 1]
    v_win = ha3[:, :, 6144:6656].reshape(b, t, B_GROUPS, 2, B_DH)[:, :, :, 1]
    assert n_sel <= B_DH
    hot = np.zeros((t, LANES), np.float32)
    hot[np.arange(t), B_DH + np.arange(t) // SEL_BLOCK] = 1.0
    o_b = _nsa_attention(ha3, kvc, jnp.asarray(overlap_t, BF16), glog_t, values_t(v_sel), bias_sel,
                         jnp.asarray(hot, BF16), values_t(v_win), bias_win, b, t, n_c, min(SEL_TOPN, n_sel))

    n_mem = mem.shape[1]
    memkv = _matmul(mem.reshape(b * n_mem, D_MODEL).astype(BF16), w_mem_kv.astype(BF16), BF16,
                    b * n_mem, 1024, "mem_kv_proj")
    o_m = _memory_attention(ha3, memkv, b, t, n_mem)

    flat = lambda o: o.reshape(n, MIX_W)
    return _merge(xb, w_mg, flat(o_a), flat(o_b), flat(o_m), w_branch.astype(BF16))


def _peer(x1, x1b, peer_wq, peer_keys, peer_u, peer_v):
    n = x1.shape[0]
    q = _matmul(x1b, peer_wq.astype(BF16), BF16, 1024, 1024, "peer_query")
    zk = jnp.zeros((PEER_HEADS, PEER_NKEYS, PEER_DKEY // 2), F32)
    keys_bd = jnp.concatenate([jnp.concatenate([peer_keys[:, 0], zk], axis=2),
                               jnp.concatenate([zk, peer_keys[:, 1]], axis=2)], axis=1).astype(BF16)
    ei, ej, gate = _peer_route(q, keys_bd)
    slots = lambda a: a.reshape(PEER_HEADS * PEER_TOPK, n).T.reshape(n, 1, PEER_HEADS * PEER_TOPK)
    gm = _gate_matrix(slots(ei), slots(ej), slots(gate))
    return _peer_experts(x1b, peer_u, gm, peer_v)


def kernel(x, mem, w_in, diff_lambda, diff_subln, cmp_pe, cmp_w1, cmp_w2, w_mem_kv, w_branch, w_out, ln1_g, ln1_b,
           peer_wq, peer_keys, peer_u, peer_v, ln2_g, ln2_b, rel_bias):
    b, t, _ = x.shape
    n = b * t
    for l in range(DEPTH):
        mixed = _token_mixer(x, mem, w_in[l], diff_lambda[l], diff_subln[l], cmp_pe[l], cmp_w1[l], cmp_w2[l],
                             w_mem_kv[l], w_branch[l], rel_bias)
        x1, x1b = _out_proj_ln(mixed, w_out[l].astype(BF16), x.reshape(n, D_MODEL),
                               ln1_g[l].reshape(1, D_MODEL), ln1_b[l].reshape(1, D_MODEL))
        y = _peer(x1, x1b, peer_wq[l], peer_keys[l], peer_u[l], peer_v[l])
        x = _residual_ln(x1, y, ln2_g[l].reshape(1, D_MODEL), ln2_b[l].reshape(1, D_MODEL)).reshape(b, t, D_MODEL)
    return x
```

```python
import functools
import math

import numpy as np
import jax
import jax.numpy as jnp
from jax import lax
from jax.experimental import pallas as pl
from jax.experimental.pallas import tpu as pltpu

F32 = jnp.float32
BF16 = jnp.bfloat16

D_MODEL = 2048
A_HEADS, A_DQK, A_DV = 8, 64, 128
B_HEADS, B_GROUPS, B_HPG, B_DH = 16, 4, 4, 64
CMP_LEN, CMP_STRIDE, CMP_HIDDEN = 32, 16, 256
SEL_BLOCK, SEL_TOPN, WINDOW = 64, 16, 512
M_HEADS, M_DH = 4, 256
REL_BUCKETS, REL_MAX_DIST = 32, 1024
PEER_HEADS, PEER_NKEYS, PEER_DKEY, PEER_TOPK = 8, 128, 256, 16
MIX_W = 1024
LN_EPS = 1e-5
FORCE = 1e9
DEPTH = 1
ALPHA = (2 * DEPTH) ** 0.25
LAMBDA_INIT = 0.8 - 0.6 * math.exp(-0.3 * 0)
LOG2E = math.log2(math.e)

NEG = -1e30
ATT_TILE = 256
LANES = 128
ONES_ROWS = 16
N_MERGE = 3
VMEM_LIMIT = 56 * 1024 * 1024


def _cparams(sem):
    return pltpu.CompilerParams(dimension_semantics=sem, vmem_limit_bytes=VMEM_LIMIT)


def _dot_nt(a, b):
    return lax.dot_general(a, b, (((1,), (1,)), ((), ())), preferred_element_type=F32)


def _mm_body(x_ref, w_ref, o_ref):
    o_ref[...] = jnp.dot(x_ref[...], w_ref[...], preferred_element_type=F32).astype(o_ref.dtype)


def _matmul(x, w, out_dtype, tm, tn, name):
    m, k = x.shape
    n = w.shape[1]
    return pl.pallas_call(
        _mm_body,
        grid=(m // tm, n // tn),
        in_specs=[pl.BlockSpec((tm, k), lambda i, j: (i, 0)), pl.BlockSpec((k, tn), lambda i, j: (0, j))],
        out_specs=pl.BlockSpec((tm, tn), lambda i, j: (i, j)),
        out_shape=jax.ShapeDtypeStruct((m, n), out_dtype),
        compiler_params=_cparams(("parallel", "arbitrary")),
        name=name,
    )(x, w)


def _in_proj_body(x_ref, w_ref, wg_ref, o_ref, og_ref, xb_ref):
    xb = x_ref[...].astype(BF16)
    o_ref[...] = _dot_nt(xb, w_ref[...]).astype(o_ref.dtype)

    @pl.when(pl.program_id(1) == 0)
    def _():
        xb_ref[...] = xb
        og_ref[...] = _dot_nt(xb, wg_ref[...])


def _in_proj(x, w_a, w_g):
    n, k = x.shape
    tm, tn = 512, 1664
    return pl.pallas_call(
        _in_proj_body,
        grid=(n // tm, W_A_COLS // tn),
        in_specs=[pl.BlockSpec((tm, k), lambda i, j: (i, 0)),
                  pl.BlockSpec((tn, k), lambda i, j: (j, 0)),
                  pl.BlockSpec((W_G_COLS, k), lambda i, j: (0, 0))],
        out_specs=[pl.BlockSpec((tm, tn), lambda i, j: (i, j)),
                   pl.BlockSpec((tm, W_G_COLS), lambda i, j: (i, 0)),
                   pl.BlockSpec((tm, k), lambda i, j: (i, 0))],
        out_shape=[jax.ShapeDtypeStruct((n, W_A_COLS), BF16), jax.ShapeDtypeStruct((n, W_G_COLS), F32),
                   jax.ShapeDtypeStruct((n, k), BF16)],
        compiler_params=_cparams(("parallel", "arbitrary")),
        name="in_proj",
    )(x, w_a, w_g)


W_A_COLS = 6656
W_G_COLS = 128


def _w_prep_body(w_ref, wa_ref, wm_ref, wg_ref):
    def put(dst_ref, d0, s0, n, c=1.0):
        val = w_ref[s0:s0 + n, :]
        dst_ref[d0:d0 + n, :] = (val * c if c != 1.0 else val).astype(dst_ref.dtype)

    put(wa_ref, 0, 5680, 1024, M_DH ** -0.5 * LOG2E)
    put(wa_ref, 1024, 0, 1024, A_DQK ** -0.5 * LOG2E)
    put(wa_ref, 2048, 1024, 2048)
    put(wa_ref, 4096, 3072, 1024, B_DH ** -0.5 * LOG2E)
    for br in range(3):
        for g in range(B_GROUPS):
            src = 4096 + br * 2 * B_GROUPS * B_DH + g * B_DH
            dst = 5120 + (br * B_GROUPS + g) * 2 * B_DH
            put(wa_ref, dst, src, B_DH)
            put(wa_ref, dst + B_DH, src + B_GROUPS * B_DH, B_DH)
    put(wm_ref, 0, 6704, N_MERGE * D_MODEL)
    put(wg_ref, 0, 5632, 48)
    wg_ref[48:, :] = jnp.zeros((W_G_COLS - 48, wg_ref.shape[1]), wg_ref.dtype)


def _w_prep(w_in_t):
    c, k = w_in_t.shape
    tk = 256
    col = lambda i: (0, i)
    return pl.pallas_call(
        _w_prep_body,
        grid=(k // tk,),
        in_specs=[pl.BlockSpec((c, tk), col)],
        out_specs=[pl.BlockSpec((W_A_COLS, tk), col), pl.BlockSpec((N_MERGE * D_MODEL, tk), col),
                   pl.BlockSpec((W_G_COLS, tk), col)],
        out_shape=[jax.ShapeDtypeStruct((W_A_COLS, k), BF16), jax.ShapeDtypeStruct((N_MERGE * D_MODEL, k), BF16),
                   jax.ShapeDtypeStruct((W_G_COLS, k), BF16)],
        compiler_params=_cparams(("parallel",)),
        name="w_in_regroup",
    )(w_in_t)


def _rel_bucket(dist):
    n = jnp.maximum(dist, 0)
    max_exact = REL_BUCKETS // 2
    nf = jnp.maximum(n, 1).astype(jnp.float32)
    large = max_exact + (jnp.log(nf / max_exact) / math.log(REL_MAX_DIST / max_exact)
                         * (REL_BUCKETS - max_exact)).astype(jnp.int32)
    large = jnp.minimum(large, REL_BUCKETS - 1)
    return jnp.where(n < max_exact, n, large)


def _bias_tiles(tab1d, t, first, n_tiles, max_dist, hpr):
    L = ATT_TILE
    m = np.arange(2 * L)
    off = np.where(m <= L, m, m - 2 * L)
    d = (first + np.arange(n_tiles))[:, None] * L + off[None, :]
    ok = (d >= 0) & (d < max_dist)
    h = tab1d.shape[0]
    rp = jnp.where(ok[:, None], tab1d.T[np.clip(d, 0, t - 1)].transpose(0, 2, 1), NEG)
    return pl.pallas_call(
        functools.partial(_toeplitz_body, hpr=hpr),
        grid=(n_tiles,),
        in_specs=[pl.BlockSpec((None, h, 1, 2 * L), lambda c: (c, 0, 0, 0))],
        out_specs=pl.BlockSpec((None, h // hpr, L, hpr * L), lambda c: (c, 0, 0, 0)),
        out_shape=jax.ShapeDtypeStruct((n_tiles, h // hpr, L, hpr * L), F32),
        compiler_params=_cparams(("parallel",)),
        name="bias_tiles",
    )(rp.reshape(n_tiles, h, 1, 2 * L))


def _toeplitz_body(rp_ref, o_ref, *, hpr):
    L = o_ref.shape[1]
    for hd in range(rp_ref.shape[0]):
        rows = jnp.broadcast_to(rp_ref[hd], (L, 2 * L))
        tile = pltpu.roll(rows, 0, 1, stride=1, stride_axis=0)[:, :L]
        o_ref[hd // hpr, :, (hd % hpr) * L:(hd % hpr + 1) * L] = tile


def _flash_loop(lo, hi, scores, values, m_ref, acc_ref, s_ref, mx_ref):
    def ahead(kt, slot):
        s = scores(kt)
        s_ref[slot] = s
        mx_ref[slot] = jnp.broadcast_to(jnp.max(s, axis=0, keepdims=True), mx_ref.shape[1:])

    def finish(kt, slot):
        m_prev = m_ref[...]
        m_new = jnp.maximum(m_prev, mx_ref[slot])
        alpha = jnp.exp2(m_prev - m_new)
        p = jnp.exp2(s_ref[slot] - m_new[0:1])
        acc_ref[...] = alpha[0:1] * acc_ref[...] + jnp.dot(values(kt), p.astype(BF16), preferred_element_type=F32)
        m_ref[...] = m_new

    if lo is None:
        tiles = hi
        ahead(tiles[0], 0)
        for j, kt in enumerate(tiles):
            if j + 1 < len(tiles):
                ahead(tiles[j + 1], (j + 1) % 2)
            finish(kt, j % 2)
        return

    def body(i, carry):
        kt = lo + 2 * i
        ahead(kt + 1, 1)
        finish(kt, 0)
        ahead(kt + 2, 0)
        finish(kt + 1, 1)
        return carry

    ahead(lo, 0)
    lax.fori_loop(0, (hi - lo + 1) // 2, body, 0)


def _diff_body(q_ref, k_ref, vt_ref, bias_ref, lam_ref, g_ref, o_ref, m_sc, acc_sc, s_sc, mx_sc, *, n_bt):
    L = ATT_TILE
    tq = 2 * L
    last = vt_ref.shape[0] - 1
    qi = pl.program_id(2)
    m_sc[...] = jnp.full(m_sc.shape, NEG, F32)
    acc_sc[...] = jnp.zeros(acc_sc.shape, F32)
    qt = q_ref[...].astype(F32).T.astype(BF16)
    zero = jnp.zeros((A_DQK, tq), BF16)
    q_cat = jnp.concatenate([jnp.concatenate([qt[:A_DQK], zero], axis=0),
                             jnp.concatenate([zero, qt[A_DQK:]], axis=0)], axis=1)

    def scores(kt):
        off = pl.multiple_of(jnp.minimum(kt, last) * L, L)
        k = k_ref[pl.ds(off, L), :]
        d0 = 2 * qi - kt
        bias = jnp.concatenate([bias_ref[jnp.clip(d0 + 1, 0, n_bt - 1)],
                                bias_ref[jnp.clip(d0 + 2, 0, n_bt - 1)]], axis=1)
        s = jnp.dot(k, q_cat, preferred_element_type=F32)
        return jnp.concatenate([s[:, :tq] + bias, s[:, tq:] + bias], axis=1)

    ones = jnp.ones((ONES_ROWS, L), BF16)

    def values(kt):
        return jnp.concatenate([vt_ref[jnp.minimum(kt, last)], ones], axis=0)

    _flash_loop(0, 2 * qi + 2, scores, values, m_sc, acc_sc, s_sc, mx_sc)

    lp = lam_ref[...]
    lam = (jnp.exp(jnp.sum(lp[0:1] * lp[1:2], axis=1, keepdims=True))
           - jnp.exp(jnp.sum(lp[2:3] * lp[3:4], axis=1, keepdims=True)) + LAMBDA_INIT)
    acc = acc_sc[...]
    o0 = acc[:A_DV, :tq] / jnp.maximum(acc[A_DV:A_DV + 1, :tq], 1e-30)
    o1 = acc[:A_DV, tq:] / jnp.maximum(acc[A_DV:A_DV + 1, tq:], 1e-30)
    o = o0 - lam * o1
    g = jnp.concatenate([g_ref[...]] * (tq // LANES), axis=1)
    o = o * lax.rsqrt(jnp.mean(o * o, axis=0, keepdims=True) + LN_EPS) * g
    o_ref[...] = (o * (1.0 - LAMBDA_INIT)).T.astype(o_ref.dtype)


def _diff_attention(ha, v_t, bias, lam_params, subln, b, t):
    L = ATT_TILE
    tq = 2 * L
    n_bt = bias.shape[0]
    return pl.pallas_call(
        functools.partial(_diff_body, n_bt=n_bt),
        grid=(b, A_HEADS, t // tq),
        in_specs=[
            pl.BlockSpec((None, tq, 128), lambda bi, h, qi: (bi, qi, 8 + h)),
            pl.BlockSpec((None, t, 128), lambda bi, h, qi: (bi, 0, 16 + h)),
            pl.BlockSpec((None, None, t // L, A_DV, L), lambda bi, h, qi: (bi, h, 0, 0, 0)),
            pl.BlockSpec((n_bt, None, L, L), lambda bi, h, qi: (0, h, 0, 0)),
            pl.BlockSpec((4, A_DQK), lambda bi, h, qi: (0, 0)),
            pl.BlockSpec((A_DV, LANES), lambda bi, h, qi: (0, 0)),
        ],
        out_specs=pl.BlockSpec((None, tq, 128), lambda bi, h, qi: (bi, qi, h)),
        out_shape=jax.ShapeDtypeStruct((b, t, MIX_W), BF16),
        scratch_shapes=[pltpu.VMEM((8, 2 * tq), F32), pltpu.VMEM((A_DV + ONES_ROWS, 2 * tq), F32),
                        pltpu.VMEM((2, L, 2 * tq), F32), pltpu.VMEM((2, 8, 2 * tq), F32)],
        compiler_params=_cparams(("parallel", "parallel", "arbitrary")),
        name="diff_attention",
    )(ha, ha, v_t, bias, lam_params, subln)


def _compress_body(kv_ref, w1_ref, w2_ref, pe_ref, o_ref):
    nr = o_ref.shape[0]
    first = jnp.zeros((nr, 2 * CMP_HIDDEN), F32)
    second = jnp.zeros((nr, 2 * CMP_HIDDEN), F32)
    for l in range(CMP_STRIDE):
        rows = kv_ref[pl.ds(l, nr, stride=CMP_STRIDE), :].astype(BF16)
        first = first + jnp.dot(rows, w1_ref[l], preferred_element_type=F32)
        second = second + jnp.dot(rows, w1_ref[CMP_STRIDE + l], preferred_element_type=F32)
    pw = jnp.zeros((8, 2 * CMP_HIDDEN), F32)
    for l in range(CMP_LEN):
        pw = pw + jnp.dot(pe_ref[l], w1_ref[l], preferred_element_type=F32)
    second = jnp.concatenate([second[1:], second[:1]], axis=0)
    hdn = jax.nn.gelu(first + second + pw[0:1])
    o_ref[...] = jnp.dot(hdn.astype(BF16), w2_ref[...], preferred_element_type=F32)


def _compress(kv, w1_bd, w2_bd, pe_bd):
    b, t, _ = kv.shape
    nr = t // CMP_STRIDE
    return pl.pallas_call(
        _compress_body,
        grid=(b, B_GROUPS),
        in_specs=[
            pl.BlockSpec((None, t, LANES), lambda bi, gi: (bi, 0, gi)),
            pl.BlockSpec(w1_bd.shape, lambda bi, gi: (0, 0, 0)),
            pl.BlockSpec(w2_bd.shape, lambda bi, gi: (0, 0)),
            pl.BlockSpec(pe_bd.shape, lambda bi, gi: (0, 0, 0)),
        ],
        out_specs=pl.BlockSpec((None, None, nr, 2 * B_DH), lambda bi, gi: (bi, gi, 0, 0)),
        out_shape=jax.ShapeDtypeStruct((b, B_GROUPS, nr, 2 * B_DH), F32),
        compiler_params=_cparams(("parallel", "parallel")),
        name="nsa_compress",
    )(kv, w1_bd, w2_bd, pe_bd)


def _topk_mask_axis0(v, k):
    r, n = v.shape
    iota = lax.broadcasted_iota(jnp.int32, (r, n), 0).astype(F32)

    def body(_, c):
        v, sel = c
        mx = jnp.max(v, axis=0, keepdims=True)
        idx = jnp.min(jnp.where(v == mx, iota, float(r)), axis=0, keepdims=True)
        hit = iota == idx
        return jnp.where(hit, -jnp.inf, v), jnp.where(hit, 1.0, sel)

    _, sel = lax.fori_loop(0, k, body, (v, jnp.zeros((r, n), F32)))
    return sel


def _compressed_branch(qt, kvc, ov_t, qi, n_c, n_top, n_sel):
    L = ATT_TILE
    ncp = kvc.shape[0]
    qh = jnp.concatenate([qt[h * B_DH:(h + 1) * B_DH] for h in range(B_HPG)], axis=1)
    s = jnp.dot(kvc[:, :B_DH].astype(BF16), qh, preferred_element_type=F32)
    tcol = qi * L + lax.broadcasted_iota(jnp.int32, (1, L), 1)
    crow = lax.broadcasted_iota(jnp.int32, (ncp, 1), 0)
    seen = jnp.where(crow < n_c, crow * CMP_STRIDE + (CMP_LEN - 1), jnp.int32(2 ** 30)) <= tcol
    valid = jnp.concatenate([seen] * B_HPG, axis=1)
    s = jnp.where(valid, s, NEG)
    mx = jnp.max(s, axis=0, keepdims=True)
    e = jnp.where(valid, jnp.exp2(s - mx), 0.0)
    p = e / jnp.maximum(jnp.sum(e, axis=0, keepdims=True), 1e-30)
    o = jnp.dot(kvc.T[B_DH:].astype(BF16), p.astype(BF16), preferred_element_type=F32)

    psum = p[:, :L] + p[:, L:2 * L] + p[:, 2 * L:3 * L] + p[:, 3 * L:]
    imp = jnp.zeros((ov_t.shape[0], L), F32)
    rem = psum
    for _ in range(3):
        part = rem.astype(BF16)
        imp = imp + jnp.dot(ov_t, part, preferred_element_type=F32)
        rem = rem - part.astype(F32)
    imp = imp[:n_sel]
    blk = lax.broadcasted_iota(jnp.int32, (n_sel, 1), 0)
    cur = jnp.right_shift(tcol, int(math.log2(SEL_BLOCK)))
    imp = jnp.where(blk * SEL_BLOCK > tcol, -FORCE, imp)
    imp = jnp.where(blk == 0, FORCE, imp)
    imp = jnp.where(blk == cur, FORCE, imp)
    imp = jnp.where(blk == cur - 1, FORCE, imp)
    return o, jnp.where(_topk_mask_axis0(imp, n_top) > 0.0, 0.0, NEG)


def _nsa_body(q_ref, kvc_ref, ov_ref, gl_ref, kvs_ref, vts_ref, bs_ref, hot_ref, kvw_ref, vtw_ref, bw_ref, o_ref,
              m_sc, acc_sc, s_sc, mx_sc, *, n_c, n_top, n_sel):
    L = ATT_TILE
    qi = pl.program_id(2)
    last = vts_ref.shape[0] - 1
    qt = q_ref[...].astype(F32).T.astype(BF16)
    gates = jax.nn.sigmoid(gl_ref[...])
    ones = jnp.ones((ONES_ROWS, L), BF16)
    k_lanes = lax.broadcasted_iota(jnp.int32, (L, LANES), 1) < B_DH

    def gated(o, branch):
        return jnp.concatenate([o[:, h * L:(h + 1) * L] * gates[branch, h:h + 1] for h in range(B_HPG)], axis=0)

    def attend(qm, kv_ref, vt_ref, bias_ref, use_hot, tiles):
        m_sc[...] = jnp.full(m_sc.shape, NEG, F32)
        acc_sc[...] = jnp.zeros(acc_sc.shape, F32)
        q_aug_t = jnp.concatenate([jnp.concatenate([qt[h * B_DH:(h + 1) * B_DH], qm], axis=0)
                                   for h in range(B_HPG)], axis=1)
        n_bt = bias_ref.shape[0]

        def scores(kt):
            off = pl.multiple_of(jnp.clip(kt, 0, last) * L, L)
            idx = jnp.where(kt < 0, 0, jnp.clip(qi - kt + 1, 0, n_bt - 1))
            k_aug = kv_ref[pl.ds(off, L), :]
            if use_hot:
                k_aug = jnp.where(k_lanes, k_aug, hot_ref[pl.ds(off, L), :])
            return jnp.dot(k_aug, q_aug_t, preferred_element_type=F32) + bias_ref[idx]

        def values(kt):
            return jnp.concatenate([vt_ref[jnp.clip(kt, 0, last)], ones], axis=0)

        if tiles is None:
            _flash_loop(0, qi + 1, scores, values, m_sc, acc_sc, s_sc, mx_sc)
        else:
            _flash_loop(None, tiles, scores, values, m_sc, acc_sc, s_sc, mx_sc)
        acc = acc_sc[...]
        return acc[:B_DH] / jnp.maximum(acc[B_DH:B_DH + 1], 1e-30)

    o_cmp, sel = _compressed_branch(qt, kvc_ref[...], ov_ref[...], qi, n_c, n_top, n_sel)
    total = gated(o_cmp, 0)
    qm = sel.astype(BF16)
    if n_sel < B_DH:
        qm = jnp.concatenate([qm, jnp.zeros((B_DH - n_sel, L), BF16)], axis=0)
    total = total + gated(attend(qm, kvs_ref, vts_ref, bs_ref, True, None), 1)
    window_tiles = [qi - j for j in range(bw_ref.shape[0] - 1)]
    total = total + gated(attend(jnp.zeros((B_DH, L), BF16), kvw_ref, vtw_ref, bw_ref, False, window_tiles), 2)
    o_ref[...] = total.T.astype(o_ref.dtype)


def _nsa_attention(ha, kvc, overlap_t, glog_t, vt_sel, bias_sel, hot, vt_win, bias_win, b, t, n_c, n_top):
    L = ATT_TILE
    ncp = kvc.shape[2]
    n_sel = t // SEL_BLOCK
    kv_spec = lambda br: pl.BlockSpec((None, t, LANES), lambda bi, g, qi: (bi, 0, 40 + 4 * br + g))
    vt_spec = pl.BlockSpec((None, None, t // L, B_DH, L), lambda bi, g, qi: (bi, g, 0, 0, 0))
    bias_spec = lambda bias: pl.BlockSpec((bias.shape[0], None, L, B_HPG * L), lambda bi, g, qi: (0, g, 0, 0))
    return pl.pallas_call(
        functools.partial(_nsa_body, n_c=n_c, n_top=n_top, n_sel=n_sel),
        grid=(b, B_GROUPS, t // L),
        in_specs=[
            pl.BlockSpec((None, L, 256), lambda bi, g, qi: (bi, qi, 16 + g)),
            pl.BlockSpec((None, None, ncp, 2 * B_DH), lambda bi, g, qi: (bi, g, 0, 0)),
            pl.BlockSpec(overlap_t.shape, lambda bi, g, qi: (0, 0)),
            pl.BlockSpec((3, None, None, B_HPG, L), lambda bi, g, qi: (0, bi, g, 0, qi)),
            kv_spec(1), vt_spec, bias_spec(bias_sel),
            pl.BlockSpec((t, LANES), lambda bi, g, qi: (0, 0)),
            kv_spec(2), vt_spec, bias_spec(bias_win),
        ],
        out_specs=pl.BlockSpec((None, L, 256), lambda bi, g, qi: (bi, qi, g)),
        out_shape=jax.ShapeDtypeStruct((b, t, MIX_W), BF16),
        scratch_shapes=[pltpu.VMEM((8, B_HPG * L), F32), pltpu.VMEM((B_DH + ONES_ROWS, B_HPG * L), F32),
                        pltpu.VMEM((2, L, B_HPG * L), F32), pltpu.VMEM((2, 8, B_HPG * L), F32)],
        compiler_params=_cparams(("parallel", "parallel", "arbitrary")),
        name="nsa_attention",
    )(ha, kvc, overlap_t, glog_t, ha, vt_sel, bias_sel, hot, ha, vt_win, bias_win)


def _mem_body(q_ref, k_ref, v_ref, o_ref):
    q = q_ref[...]
    outs = []
    for h in range(M_HEADS):
        sl = slice(h * M_DH, (h + 1) * M_DH)
        s = _dot_nt(q[:, sl], k_ref[:, sl])
        e = jnp.exp2(s - jnp.max(s, axis=1, keepdims=True))
        p = e / jnp.sum(e, axis=1, keepdims=True)
        outs.append(jnp.dot(p.astype(BF16), v_ref[:, sl], preferred_element_type=F32))
    o_ref[...] = jnp.concatenate(outs, axis=1).astype(o_ref.dtype)


def _memory_attention(ha, memkv, b, t, n_mem):
    tq = 512
    w = M_HEADS * M_DH
    return pl.pallas_call(
        _mem_body,
        grid=(b, t // tq),
        in_specs=[
            pl.BlockSpec((None, tq, w), lambda bi, qi: (bi, qi, 0)),
            pl.BlockSpec((n_mem, w), lambda bi, qi: (bi, 0)),
            pl.BlockSpec((n_mem, w), lambda bi, qi: (bi, 1)),
        ],
        out_specs=pl.BlockSpec((None, tq, w), lambda bi, qi: (bi, qi, 0)),
        out_shape=jax.ShapeDtypeStruct((b, t, w), BF16),
        compiler_params=_cparams(("parallel", "arbitrary")),
        name="memory_attention",
    )(ha, memkv, memkv)


def _merge_body(x_ref, wg_ref, oa_ref, ob_ref, om_ref, wb_ref, o_ref, acc_sc):
    n = pl.program_id(1)

    @pl.when(n == 0)
    def _():
        acc_sc[...] = jnp.zeros(acc_sc.shape, F32)

    gate = jax.nn.sigmoid(_dot_nt(x_ref[...], wg_ref[...]))
    branch = jnp.where(n == 0, oa_ref[...], jnp.where(n == 1, ob_ref[...], om_ref[...]))
    acc = acc_sc[...] + gate * jnp.dot(branch, wb_ref[...], preferred_element_type=F32)
    acc_sc[...] = acc
    o_ref[...] = acc.astype(o_ref.dtype)


def _merge(xb, w_gate_t, o_a, o_b, o_m, w_branch):
    n = xb.shape[0]
    tm = 512
    row = lambda i, j: (i, 0)
    return pl.pallas_call(
        _merge_body,
        grid=(n // tm, N_MERGE),
        in_specs=[pl.BlockSpec((tm, D_MODEL), row),
                  pl.BlockSpec((D_MODEL, D_MODEL), lambda i, j: (j, 0))]
                 + [pl.BlockSpec((tm, MIX_W), row)] * 3
                 + [pl.BlockSpec((None, MIX_W, D_MODEL), lambda i, j: (j, 0, 0))],
        out_specs=pl.BlockSpec((tm, D_MODEL), row),
        out_shape=jax.ShapeDtypeStruct((n, D_MODEL), BF16),
        scratch_shapes=[pltpu.VMEM((tm, D_MODEL), F32)],
        compiler_params=_cparams(("parallel", "arbitrary")),
        name="branch_merge",
    )(xb, w_gate_t, o_a, o_b, o_m, w_branch)


def _layer_norm(z, g, b):
    mu = jnp.mean(z, axis=1, keepdims=True)
    zc = z - mu
    var = jnp.mean(zc * zc, axis=1, keepdims=True)
    return zc * lax.rsqrt(var + LN_EPS) * g + b


def _out_ln_body(y_ref, w_ref, x_ref, g_ref, b_ref, wq_ref, o_ref, ob_ref, q_ref):
    y = jnp.dot(y_ref[...], w_ref[...], preferred_element_type=F32)
    o = _layer_norm(ALPHA * x_ref[...] + y, g_ref[...], b_ref[...])
    o_ref[...] = o
    ob = o.astype(BF16)
    ob_ref[...] = ob
    q_ref[...] = jnp.dot(ob, wq_ref[...], preferred_element_type=F32).astype(q_ref.dtype)


def _out_proj_ln(mixed, w_out, x, g, b, w_q):
    n = x.shape[0]
    nq = w_q.shape[1]
    tm = 512
    row = lambda i: (i, 0)
    const = lambda i: (0, 0)
    once = dict(pipeline_mode=pl.Buffered(1))
    return pl.pallas_call(
        _out_ln_body,
        grid=(n // tm,),
        in_specs=[pl.BlockSpec((tm, D_MODEL), row), pl.BlockSpec((D_MODEL, D_MODEL), const, **once),
                  pl.BlockSpec((tm, D_MODEL), row), pl.BlockSpec((1, D_MODEL), const),
                  pl.BlockSpec((1, D_MODEL), const), pl.BlockSpec((D_MODEL, nq), const, **once)],
        out_specs=[pl.BlockSpec((tm, D_MODEL), row), pl.BlockSpec((tm, D_MODEL), row), pl.BlockSpec((tm, nq), row)],
        out_shape=[jax.ShapeDtypeStruct((n, D_MODEL), F32), jax.ShapeDtypeStruct((n, D_MODEL), BF16),
                   jax.ShapeDtypeStruct((n, nq), BF16)],
        compiler_params=_cparams(("parallel",)),
        name="out_proj_ln1_query",
    )(mixed, w_out, x, g, b, w_q)


def _res_ln_body(x_ref, y_ref, g_ref, b_ref, o_ref):
    o_ref[...] = _layer_norm(ALPHA * x_ref[...] + y_ref[...], g_ref[...], b_ref[...])


def _residual_ln(x, y, g, b):
    n = x.shape[0]
    tm = 512
    row = lambda i: (i, 0)
    const = lambda i: (0, 0)
    return pl.pallas_call(
        _res_ln_body,
        grid=(n // tm,),
        in_specs=[pl.BlockSpec((tm, D_MODEL), row), pl.BlockSpec((tm, D_MODEL), row),
                  pl.BlockSpec((1, D_MODEL), const), pl.BlockSpec((1, D_MODEL), const)],
        out_specs=pl.BlockSpec((tm, D_MODEL), row),
        out_shape=jax.ShapeDtypeStruct((n, D_MODEL), F32),
        compiler_params=_cparams(("parallel",)),
        name="residual_ln2",
    )(x, y, g, b)


def _topk_axis0(v, k):
    r, n = v.shape
    iota = lax.broadcasted_iota(jnp.int32, (r, n), 0).astype(F32)
    slot = lax.broadcasted_iota(jnp.int32, (k, n), 0)

    def body(it, c):
        v, vals, idxs = c
        mx = jnp.max(v, axis=0, keepdims=True)
        idx = jnp.min(jnp.where(v == mx, iota, float(r)), axis=0, keepdims=True)
        v = jnp.where(iota == idx, -jnp.inf, v)
        return v, jnp.where(slot == it, mx, vals), jnp.where(slot == it, idx, idxs)

    _, vals, idxs = lax.fori_loop(0, k, body, (v, jnp.zeros((k, n), F32), jnp.zeros((k, n), F32)))
    return vals, idxs


def _pick_rows(table, pos, k):
    out = jnp.zeros(pos.shape, F32)
    for a in range(k):
        out = out + jnp.where(pos == float(a), table[a:a + 1], 0.0)
    return out


def _route_body(q_ref, keys_ref, ei_ref, ej_ref, g_ref):
    k = PEER_TOPK
    scores = _dot_nt(keys_ref[...], q_ref[...])
    v0, i0 = _topk_axis0(scores[:PEER_NKEYS], k)
    v1, i1 = _topk_axis0(scores[PEER_NKEYS:], k)
    counts = [k // (a + 1) for a in range(k)]
    starts = np.cumsum([0] + counts[:-1])
    pad = (-sum(counts)) % 8
    comb = jnp.concatenate([v0[a:a + 1] + v1[:counts[a]] for a in range(k)]
                           + [jnp.full((pad, v0.shape[1]), -jnp.inf, F32)], axis=0)
    sf, pos = _topk_axis0(comb, k)
    pa = jnp.zeros(pos.shape, F32)
    pb = pos
    for a in range(1, k):
        later = pos >= float(starts[a])
        pa = pa + jnp.where(later, 1.0, 0.0)
        pb = pb - jnp.where(later, float(counts[a - 1]), 0.0)
    ei_ref[...] = _pick_rows(i0, pa, k)
    ej_ref[...] = _pick_rows(i1, pb, k)
    e = jnp.exp(sf - jnp.max(sf, axis=0, keepdims=True))
    g_ref[...] = e / jnp.sum(e, axis=0, keepdims=True)


def _peer_route(q, keys):
    n = q.shape[0]
    tn = 512
    out = jax.ShapeDtypeStruct((PEER_HEADS, PEER_TOPK, n), F32)
    ospec = pl.BlockSpec((None, PEER_TOPK, tn), lambda i, h: (h, 0, i))
    return pl.pallas_call(
        _route_body,
        grid=(n // tn, PEER_HEADS),
        in_specs=[pl.BlockSpec((tn, PEER_DKEY), lambda i, h: (i, h)),
                  pl.BlockSpec((None, 2 * PEER_NKEYS, PEER_DKEY), lambda i, h: (h, 0, 0))],
        out_specs=[ospec, ospec, ospec],
        out_shape=[out, out, out],
        compiler_params=_cparams(("parallel", "arbitrary")),
        name="peer_route",
    )(q, keys)


def _gate_body(ei_ref, ej_ref, g_ref, o_ref):
    tb = ei_ref.shape[0]
    nk = PEER_NKEYS
    iota = lax.broadcasted_iota(jnp.int32, (tb, nk, ei_ref.shape[2]), 1).astype(F32)
    rows = jnp.where(iota == ei_ref[...], 1.0, 0.0).astype(BF16)
    cols = jnp.where(iota == ej_ref[...], g_ref[...], 0.0).astype(BF16)
    gm = lax.dot_general(rows, cols, (((2,), (2,)), ((0,), (0,))), preferred_element_type=F32)
    o_ref[...] = jnp.swapaxes(gm, 0, 1).astype(o_ref.dtype)


def _gate_matrix(ei, ej, g):
    n, _, slots = ei.shape
    tb = 128
    spec = pl.BlockSpec((tb, 1, slots), lambda i: (i, 0, 0))
    return pl.pallas_call(
        _gate_body,
        grid=(n // tb,),
        in_specs=[spec, spec, spec],
        out_specs=pl.BlockSpec((PEER_NKEYS, tb, PEER_NKEYS), lambda i: (0, i, 0)),
        out_shape=jax.ShapeDtypeStruct((PEER_NKEYS, n, PEER_NKEYS), BF16),
        compiler_params=_cparams(("parallel",)),
        name="peer_gate_matrix",
    )(ei, ej, g)


def _expert_body(x_ref, u_ref, g_ref, v_ref, o_ref):
    @pl.when(pl.program_id(1) == 0)
    def _():
        o_ref[...] = jnp.zeros(o_ref.shape, F32)

    hid = _dot_nt(x_ref[...], u_ref[...].astype(BF16))
    gate = jnp.concatenate([g_ref[i] for i in range(g_ref.shape[0])], axis=1)
    act = (jax.nn.gelu(hid) * gate.astype(F32)).astype(BF16)
    o_ref[...] += jnp.dot(act, v_ref[...].astype(BF16), preferred_element_type=F32)


def _peer_experts(xb, u, gm, v):
    n = xb.shape[0]
    ne = u.shape[0]
    tn, te = 1024, 1024
    once = dict(pipeline_mode=pl.Buffered(1))
    return pl.pallas_call(
        _expert_body,
        grid=(n // tn, ne // te),
        in_specs=[pl.BlockSpec((tn, D_MODEL), lambda i, j: (i, 0), **once),
                  pl.BlockSpec((te, D_MODEL), lambda i, j: (j, 0)),
                  pl.BlockSpec((te // PEER_NKEYS, tn, PEER_NKEYS), lambda i, j: (j, i, 0)),
                  pl.BlockSpec((te, D_MODEL), lambda i, j: (j, 0))],
        out_specs=pl.BlockSpec((tn, D_MODEL), lambda i, j: (i, 0), **once),
        out_shape=jax.ShapeDtypeStruct((n, D_MODEL), F32),
        compiler_params=_cparams(("parallel", "arbitrary")),
        name="peer_experts",
    )(xb, u, gm, v)


def _token_mixer(x, mem, w_in, diff_lambda, diff_subln, cmp_pe, cmp_w1, cmp_w2, w_mem_kv, w_branch, rel_bias):
    b, t, _ = x.shape
    n = b * t
    L = ATT_TILE

    w_a, w_mg, w_g = _w_prep(w_in.T)
    ha, hg, xb = _in_proj(x.reshape(n, D_MODEL), w_a, w_g)
    ha3 = ha.reshape(b, t, ha.shape[1])

    tab1d = rel_bias[_rel_bucket(jnp.arange(t))].T * LOG2E
    n_far = min(t // L, REL_MAX_DIST // L + 2)
    bias_a = _bias_tiles(tab1d[:A_HEADS], t, -1, n_far + 1, t, 1)
    bias_b = tab1d[A_HEADS:]
    bias_sel = _bias_tiles(bias_b, t, -1, n_far + 1, t, B_HPG)
    n_win = min(t // L, WINDOW // L + 1)
    bias_win = _bias_tiles(bias_b, t, -1, n_win + 1, WINDOW, B_HPG)

    def values_t(v):
        heads, dv = v.shape[2:]
        return v.reshape(b, t // L, L, heads, dv).transpose(0, 3, 1, 4, 2)

    o_a = _diff_attention(ha3, values_t(ha3[:, :, 3072:4096].reshape(b, t, A_HEADS, A_DV)), bias_a, diff_lambda,
                          jnp.broadcast_to(diff_subln[:, None], (A_DV, LANES)), b, t)

    glog_t = hg[:, :48].reshape(b, t, 3, B_GROUPS, B_HPG).transpose(2, 0, 3, 4, 1)
    nr = t // CMP_STRIDE
    w1 = cmp_w1.reshape(2, CMP_LEN, B_DH, CMP_HIDDEN)
    z1 = jnp.zeros_like(w1[0])
    w1_bd = jnp.concatenate([jnp.concatenate([w1[0], z1], axis=2), jnp.concatenate([z1, w1[1]], axis=2)],
                            axis=1).astype(BF16)
    z2 = jnp.zeros_like(cmp_w2[0])
    w2_bd = jnp.concatenate([jnp.concatenate([cmp_w2[0], z2], axis=1), jnp.concatenate([z2, cmp_w2[1]], axis=1)],
                            axis=0).astype(BF16)
    pe_bd = jnp.broadcast_to(jnp.concatenate([cmp_pe[0], cmp_pe[1]], axis=1)[:, None, :],
                             (CMP_LEN, 8, 2 * B_DH)).astype(BF16)
    kvc = _compress(ha3[:, :, 5120:5632].astype(F32), w1_bd, w2_bd, pe_bd)
    n_c = (t - CMP_LEN) // CMP_STRIDE + 1
    n_sel = t // SEL_BLOCK
    cidx = np.arange(nr)[:, None] * CMP_STRIDE + np.arange(CMP_LEN)[None, :]
    overlap = (cidx[:, :, None] // SEL_BLOCK == np.arange(n_sel)[None, None, :]).astype(np.float32).mean(axis=1)
    overlap[n_c:] = 0.0
    overlap_t = np.zeros((LANES, nr), np.float32)
    overlap_t[:n_sel] = overlap.T
    v_sel = ha3[:, :, 5632:6144].reshape(b, t, B_GROUPS, 2, B_DH)[:, :, :, 1]
    v_win = ha3[:, :, 6144:6656].reshape(b, t, B_GROUPS, 2, B_DH)[:, :, :, 1]
    assert n_sel <= B_DH
    hot = np.zeros((t, LANES), np.float32)
    hot[np.arange(t), B_DH + np.arange(t) // SEL_BLOCK] = 1.0
    o_b = _nsa_attention(ha3, kvc, jnp.asarray(overlap_t, BF16), glog_t, values_t(v_sel), bias_sel,
                         jnp.asarray(hot, BF16), values_t(v_win), bias_win, b, t, n_c, min(SEL_TOPN, n_sel))

    n_mem = mem.shape[1]
    memkv = _matmul(mem.reshape(b * n_mem, D_MODEL).astype(BF16), w_mem_kv.astype(BF16), BF16,
                    b * n_mem, 1024, "mem_kv_proj")
    o_m = _memory_attention(ha3, memkv, b, t, n_mem)

    flat = lambda o: o.reshape(n, MIX_W)
    return _merge(xb, w_mg, flat(o_a), flat(o_b), flat(o_m), w_branch.astype(BF16))


def _peer(x1b, q, peer_keys, peer_u, peer_v):
    n = x1b.shape[0]
    zk = jnp.zeros((PEER_HEADS, PEER_NKEYS, PEER_DKEY // 2), F32)
    keys_bd = jnp.concatenate([jnp.concatenate([peer_keys[:, 0], zk], axis=2),
                               jnp.concatenate([zk, peer_keys[:, 1]], axis=2)], axis=1).astype(BF16)
    ei, ej, gate = _peer_route(q, keys_bd)
    slots = lambda a: a.reshape(PEER_HEADS * PEER_TOPK, n).T.reshape(n, 1, PEER_HEADS * PEER_TOPK)
    gm = _gate_matrix(slots(ei), slots(ej), slots(gate))
    return _peer_experts(x1b, peer_u, gm, peer_v)


def kernel(x, mem, w_in, diff_lambda, diff_subln, cmp_pe, cmp_w1, cmp_w2, w_mem_kv, w_branch, w_out, ln1_g, ln1_b,
           peer_wq, peer_keys, peer_u, peer_v, ln2_g, ln2_b, rel_bias):
    b, t, _ = x.shape
    n = b * t
    for l in range(DEPTH):
        mixed = _token_mixer(x, mem, w_in[l], diff_lambda[l], diff_subln[l], cmp_pe[l], cmp_w1[l], cmp_w2[l],
                             w_mem_kv[l], w_branch[l], rel_bias)
        x1, x1b, q = _out_proj_ln(mixed, w_out[l].astype(BF16), x.reshape(n, D_MODEL), ln1_g[l].reshape(1, D_MODEL),
                                  ln1_b[l].reshape(1, D_MODEL), peer_wq[l].astype(BF16))
        y = _peer(x1b, q, peer_keys[l], peer_u[l], peer_v[l])
        x = _residual_ln(x1, y, ln2_g[l].reshape(1, D_MODEL), ln2_b[l].reshape(1, D_MODEL)).reshape(b, t, D_MODEL)
    return x
```

```python
import functools
import math

import numpy as np
import jax
import jax.numpy as jnp
from jax import lax
from jax.experimental import pallas as pl
from jax.experimental.pallas import tpu as pltpu

F32 = jnp.float32
BF16 = jnp.bfloat16

D_MODEL = 2048
A_HEADS, A_DQK, A_DV = 8, 64, 128
B_HEADS, B_GROUPS, B_HPG, B_DH = 16, 4, 4, 64
CMP_LEN, CMP_STRIDE, CMP_HIDDEN = 32, 16, 256
SEL_BLOCK, SEL_TOPN, WINDOW = 64, 16, 512
M_HEADS, M_DH = 4, 256
REL_BUCKETS, REL_MAX_DIST = 32, 1024
PEER_HEADS, PEER_NKEYS, PEER_DKEY, PEER_TOPK = 8, 128, 256, 16
MIX_W = 1024
LN_EPS = 1e-5
FORCE = 1e9
DEPTH = 1
ALPHA = (2 * DEPTH) ** 0.25
LAMBDA_INIT = 0.8 - 0.6 * math.exp(-0.3 * 0)
LOG2E = math.log2(math.e)

NEG = -1e30
ATT_TILE = 256
LANES = 128
ONES_ROWS = 16
N_MERGE = 3
VMEM_LIMIT = 56 * 1024 * 1024


def _cparams(sem):
    return pltpu.CompilerParams(dimension_semantics=sem, vmem_limit_bytes=VMEM_LIMIT)


def _dot_nt(a, b):
    return lax.dot_general(a, b, (((1,), (1,)), ((), ())), preferred_element_type=F32)


def _mm_body(x_ref, w_ref, o_ref):
    o_ref[...] = jnp.dot(x_ref[...], w_ref[...], preferred_element_type=F32).astype(o_ref.dtype)


def _matmul(x, w, out_dtype, tm, tn, name):
    m, k = x.shape
    n = w.shape[1]
    return pl.pallas_call(
        _mm_body,
        grid=(m // tm, n // tn),
        in_specs=[pl.BlockSpec((tm, k), lambda i, j: (i, 0)), pl.BlockSpec((k, tn), lambda i, j: (0, j))],
        out_specs=pl.BlockSpec((tm, tn), lambda i, j: (i, j)),
        out_shape=jax.ShapeDtypeStruct((m, n), out_dtype),
        compiler_params=_cparams(("parallel", "arbitrary")),
        name=name,
    )(x, w)


def _in_proj_body(x_ref, w_ref, wg_ref, o_ref, og_ref, xb_ref):
    xb = x_ref[...].astype(BF16)
    o_ref[...] = _dot_nt(xb, w_ref[...]).astype(o_ref.dtype)

    @pl.when(pl.program_id(1) == 0)
    def _():
        xb_ref[...] = xb
        og_ref[...] = _dot_nt(xb, wg_ref[...])


def _in_proj(x, w_a, w_g):
    n, k = x.shape
    tm, tn = 512, 1664
    return pl.pallas_call(
        _in_proj_body,
        grid=(n // tm, W_A_COLS // tn),
        in_specs=[pl.BlockSpec((tm, k), lambda i, j: (i, 0)),
                  pl.BlockSpec((tn, k), lambda i, j: (j, 0)),
                  pl.BlockSpec((W_G_COLS, k), lambda i, j: (0, 0))],
        out_specs=[pl.BlockSpec((tm, tn), lambda i, j: (i, j)),
                   pl.BlockSpec((tm, W_G_COLS), lambda i, j: (i, 0)),
                   pl.BlockSpec((tm, k), lambda i, j: (i, 0))],
        out_shape=[jax.ShapeDtypeStruct((n, W_A_COLS), BF16), jax.ShapeDtypeStruct((n, W_G_COLS), F32),
                   jax.ShapeDtypeStruct((n, k), BF16)],
        compiler_params=_cparams(("parallel", "arbitrary")),
        name="in_proj",
    )(x, w_a, w_g)


W_A_COLS = 6656
W_G_COLS = 128


def _w_prep_body(w_ref, wa_ref, wm_ref, wg_ref):
    def put(dst_ref, d0, s0, n, c=1.0):
        val = w_ref[s0:s0 + n, :]
        dst_ref[d0:d0 + n, :] = (val * c if c != 1.0 else val).astype(dst_ref.dtype)

    put(wa_ref, 0, 5680, 1024, M_DH ** -0.5 * LOG2E)
    put(wa_ref, 1024, 0, 1024, A_DQK ** -0.5 * LOG2E)
    put(wa_ref, 2048, 1024, 2048)
    put(wa_ref, 4096, 3072, 1024, B_DH ** -0.5 * LOG2E)
    for br in range(3):
        for g in range(B_GROUPS):
            src = 4096 + br * 2 * B_GROUPS * B_DH + g * B_DH
            dst = 5120 + (br * B_GROUPS + g) * 2 * B_DH
            put(wa_ref, dst, src, B_DH)
            put(wa_ref, dst + B_DH, src + B_GROUPS * B_DH, B_DH)
    put(wm_ref, 0, 6704, N_MERGE * D_MODEL)
    put(wg_ref, 0, 5632, 48)
    wg_ref[48:, :] = jnp.zeros((W_G_COLS - 48, wg_ref.shape[1]), wg_ref.dtype)


def _w_prep(w_in_t):
    c, k = w_in_t.shape
    tk = 256
    col = lambda i: (0, i)
    return pl.pallas_call(
        _w_prep_body,
        grid=(k // tk,),
        in_specs=[pl.BlockSpec((c, tk), col)],
        out_specs=[pl.BlockSpec((W_A_COLS, tk), col), pl.BlockSpec((N_MERGE * D_MODEL, tk), col),
                   pl.BlockSpec((W_G_COLS, tk), col)],
        out_shape=[jax.ShapeDtypeStruct((W_A_COLS, k), BF16), jax.ShapeDtypeStruct((N_MERGE * D_MODEL, k), BF16),
                   jax.ShapeDtypeStruct((W_G_COLS, k), BF16)],
        compiler_params=_cparams(("parallel",)),
        name="w_in_regroup",
    )(w_in_t)


def _rel_bucket(dist):
    n = jnp.maximum(dist, 0)
    max_exact = REL_BUCKETS // 2
    nf = jnp.maximum(n, 1).astype(jnp.float32)
    large = max_exact + (jnp.log(nf / max_exact) / math.log(REL_MAX_DIST / max_exact)
                         * (REL_BUCKETS - max_exact)).astype(jnp.int32)
    large = jnp.minimum(large, REL_BUCKETS - 1)
    return jnp.where(n < max_exact, n, large)


def _bias_tiles(tab1d, t, first, n_tiles, max_dist, hpr):
    L = ATT_TILE
    m = np.arange(2 * L)
    off = np.where(m <= L, m, m - 2 * L)
    d = (first + np.arange(n_tiles))[:, None] * L + off[None, :]
    ok = (d >= 0) & (d < max_dist)
    h = tab1d.shape[0]
    rp = jnp.where(ok[:, None], tab1d.T[np.clip(d, 0, t - 1)].transpose(0, 2, 1), NEG)
    return pl.pallas_call(
        functools.partial(_toeplitz_body, hpr=hpr),
        grid=(n_tiles,),
        in_specs=[pl.BlockSpec((None, h, 1, 2 * L), lambda c: (c, 0, 0, 0))],
        out_specs=pl.BlockSpec((None, h // hpr, L, hpr * L), lambda c: (c, 0, 0, 0)),
        out_shape=jax.ShapeDtypeStruct((n_tiles, h // hpr, L, hpr * L), F32),
        compiler_params=_cparams(("parallel",)),
        name="bias_tiles",
    )(rp.reshape(n_tiles, h, 1, 2 * L))


def _toeplitz_body(rp_ref, o_ref, *, hpr):
    L = o_ref.shape[1]
    for hd in range(rp_ref.shape[0]):
        rows = jnp.broadcast_to(rp_ref[hd], (L, 2 * L))
        tile = pltpu.roll(rows, 0, 1, stride=1, stride_axis=0)[:, :L]
        o_ref[hd // hpr, :, (hd % hpr) * L:(hd % hpr + 1) * L] = tile


def _flash_loop(lo, hi, scores, values, m_ref, acc_ref, s_ref, mx_ref):
    def ahead(kt, slot):
        s = scores(kt)
        s_ref[slot] = s
        mx_ref[slot] = jnp.broadcast_to(jnp.max(s, axis=0, keepdims=True), mx_ref.shape[1:])

    def finish(kt, slot):
        m_prev = m_ref[...]
        m_new = jnp.maximum(m_prev, mx_ref[slot])
        alpha = jnp.exp2(m_prev - m_new)
        p = jnp.exp2(s_ref[slot] - m_new[0:1])
        acc_ref[...] = alpha[0:1] * acc_ref[...] + jnp.dot(values(kt), p.astype(BF16), preferred_element_type=F32)
        m_ref[...] = m_new

    if lo is None:
        tiles = hi
        ahead(tiles[0], 0)
        for j, kt in enumerate(tiles):
            if j + 1 < len(tiles):
                ahead(tiles[j + 1], (j + 1) % 2)
            finish(kt, j % 2)
        return

    def body(i, carry):
        kt = lo + 2 * i
        ahead(kt + 1, 1)
        finish(kt, 0)
        ahead(kt + 2, 0)
        finish(kt + 1, 1)
        return carry

    ahead(lo, 0)
    lax.fori_loop(0, (hi - lo + 1) // 2, body, 0)


def _diff_body(q_ref, k_ref, vt_ref, bias_ref, lam_ref, g_ref, o_ref, m_sc, acc_sc, s_sc, mx_sc, *, n_bt):
    L = ATT_TILE
    tq = 2 * L
    last = vt_ref.shape[0] - 1
    qi = pl.program_id(2)
    m_sc[...] = jnp.full(m_sc.shape, NEG, F32)
    acc_sc[...] = jnp.zeros(acc_sc.shape, F32)
    qt = q_ref[...].astype(F32).T.astype(BF16)
    zero = jnp.zeros((A_DQK, tq), BF16)
    q_cat = jnp.concatenate([jnp.concatenate([qt[:A_DQK], zero], axis=0),
                             jnp.concatenate([zero, qt[A_DQK:]], axis=0)], axis=1)

    def scores(kt):
        off = pl.multiple_of(jnp.minimum(kt, last) * L, L)
        k = k_ref[pl.ds(off, L), :]
        d0 = 2 * qi - kt
        bias = jnp.concatenate([bias_ref[jnp.clip(d0 + 1, 0, n_bt - 1)],
                                bias_ref[jnp.clip(d0 + 2, 0, n_bt - 1)]], axis=1)
        s = jnp.dot(k, q_cat, preferred_element_type=F32)
        return jnp.concatenate([s[:, :tq] + bias, s[:, tq:] + bias], axis=1)

    ones = jnp.ones((ONES_ROWS, L), BF16)

    def values(kt):
        return jnp.concatenate([vt_ref[jnp.minimum(kt, last)], ones], axis=0)

    _flash_loop(0, 2 * qi + 2, scores, values, m_sc, acc_sc, s_sc, mx_sc)

    lp = lam_ref[...]
    lam = (jnp.exp(jnp.sum(lp[0:1] * lp[1:2], axis=1, keepdims=True))
           - jnp.exp(jnp.sum(lp[2:3] * lp[3:4], axis=1, keepdims=True)) + LAMBDA_INIT)
    acc = acc_sc[...]
    o0 = acc[:A_DV, :tq] / jnp.maximum(acc[A_DV:A_DV + 1, :tq], 1e-30)
    o1 = acc[:A_DV, tq:] / jnp.maximum(acc[A_DV:A_DV + 1, tq:], 1e-30)
    o = o0 - lam * o1
    g = jnp.concatenate([g_ref[...]] * (tq // LANES), axis=1)
    o = o * lax.rsqrt(jnp.mean(o * o, axis=0, keepdims=True) + LN_EPS) * g
    o_ref[...] = (o * (1.0 - LAMBDA_INIT)).T.astype(o_ref.dtype)


def _diff_attention(ha, v_t, bias, lam_params, subln, b, t):
    L = ATT_TILE
    tq = 2 * L
    n_bt = bias.shape[0]
    return pl.pallas_call(
        functools.partial(_diff_body, n_bt=n_bt),
        grid=(b, A_HEADS, t // tq),
        in_specs=[
            pl.BlockSpec((None, tq, 128), lambda bi, h, qi: (bi, qi, 8 + h)),
            pl.BlockSpec((None, t, 128), lambda bi, h, qi: (bi, 0, 16 + h)),
            pl.BlockSpec((None, None, t // L, A_DV, L), lambda bi, h, qi: (bi, h, 0, 0, 0)),
            pl.BlockSpec((n_bt, None, L, L), lambda bi, h, qi: (0, h, 0, 0)),
            pl.BlockSpec((4, A_DQK), lambda bi, h, qi: (0, 0)),
            pl.BlockSpec((A_DV, LANES), lambda bi, h, qi: (0, 0)),
        ],
        out_specs=pl.BlockSpec((None, tq, 128), lambda bi, h, qi: (bi, qi, h)),
        out_shape=jax.ShapeDtypeStruct((b, t, MIX_W), BF16),
        scratch_shapes=[pltpu.VMEM((8, 2 * tq), F32), pltpu.VMEM((A_DV + ONES_ROWS, 2 * tq), F32),
                        pltpu.VMEM((2, L, 2 * tq), F32), pltpu.VMEM((2, 8, 2 * tq), F32)],
        compiler_params=_cparams(("parallel", "parallel", "arbitrary")),
        name="diff_attention",
    )(ha, ha, v_t, bias, lam_params, subln)


def _compress_body(kv_ref, w1_ref, w2_ref, pe_ref, o_ref):
    nr = o_ref.shape[0]
    first = jnp.zeros((nr, 2 * CMP_HIDDEN), F32)
    second = jnp.zeros((nr, 2 * CMP_HIDDEN), F32)
    for l in range(CMP_STRIDE):
        rows = kv_ref[pl.ds(l, nr, stride=CMP_STRIDE), :].astype(BF16)
        first = first + jnp.dot(rows, w1_ref[l], preferred_element_type=F32)
        second = second + jnp.dot(rows, w1_ref[CMP_STRIDE + l], preferred_element_type=F32)
    pw = jnp.zeros((8, 2 * CMP_HIDDEN), F32)
    for l in range(CMP_LEN):
        pw = pw + jnp.dot(pe_ref[l], w1_ref[l], preferred_element_type=F32)
    second = jnp.concatenate([second[1:], second[:1]], axis=0)
    hdn = jax.nn.gelu(first + second + pw[0:1])
    o_ref[...] = jnp.dot(hdn.astype(BF16), w2_ref[...], preferred_element_type=F32)


def _compress(kv, w1_bd, w2_bd, pe_bd):
    b, t, _ = kv.shape
    nr = t // CMP_STRIDE
    return pl.pallas_call(
        _compress_body,
        grid=(b, B_GROUPS),
        in_specs=[
            pl.BlockSpec((None, t, LANES), lambda bi, gi: (bi, 0, gi)),
            pl.BlockSpec(w1_bd.shape, lambda bi, gi: (0, 0, 0)),
            pl.BlockSpec(w2_bd.shape, lambda bi, gi: (0, 0)),
            pl.BlockSpec(pe_bd.shape, lambda bi, gi: (0, 0, 0)),
        ],
        out_specs=pl.BlockSpec((None, None, nr, 2 * B_DH), lambda bi, gi: (bi, gi, 0, 0)),
        out_shape=jax.ShapeDtypeStruct((b, B_GROUPS, nr, 2 * B_DH), F32),
        compiler_params=_cparams(("parallel", "parallel")),
        name="nsa_compress",
    )(kv, w1_bd, w2_bd, pe_bd)


def _topk_mask_axis0(v, k):
    r, n = v.shape
    iota = lax.broadcasted_iota(jnp.int32, (r, n), 0).astype(F32)
    sel = jnp.zeros((r, n), F32)
    for _ in range(k):
        mx = jnp.max(v, axis=0, keepdims=True)
        idx = jnp.min(jnp.where(v == mx, iota, float(r)), axis=0, keepdims=True)
        hit = iota == idx
        v = jnp.where(hit, -jnp.inf, v)
        sel = jnp.where(hit, 1.0, sel)
    return sel


def _compressed_branch(qt, kvc, ov_t, qi, n_c, n_top, n_sel):
    L = ATT_TILE
    ncp = kvc.shape[0]
    qh = jnp.concatenate([qt[h * B_DH:(h + 1) * B_DH] for h in range(B_HPG)], axis=1)
    s = jnp.dot(kvc[:, :B_DH].astype(BF16), qh, preferred_element_type=F32)
    tcol = qi * L + lax.broadcasted_iota(jnp.int32, (1, L), 1)
    crow = lax.broadcasted_iota(jnp.int32, (ncp, 1), 0)
    seen = jnp.where(crow < n_c, crow * CMP_STRIDE + (CMP_LEN - 1), jnp.int32(2 ** 30)) <= tcol
    valid = jnp.concatenate([seen] * B_HPG, axis=1)
    s = jnp.where(valid, s, NEG)
    mx = jnp.max(s, axis=0, keepdims=True)
    e = jnp.where(valid, jnp.exp2(s - mx), 0.0)
    p = e / jnp.maximum(jnp.sum(e, axis=0, keepdims=True), 1e-30)
    o = jnp.dot(kvc.T[B_DH:].astype(BF16), p.astype(BF16), preferred_element_type=F32)

    psum = p[:, :L] + p[:, L:2 * L] + p[:, 2 * L:3 * L] + p[:, 3 * L:]
    imp = jnp.zeros((ov_t.shape[0], L), F32)
    rem = psum
    for _ in range(3):
        part = rem.astype(BF16)
        imp = imp + jnp.dot(ov_t, part, preferred_element_type=F32)
        rem = rem - part.astype(F32)
    imp = imp[:n_sel]
    blk = lax.broadcasted_iota(jnp.int32, (n_sel, 1), 0)
    cur = jnp.right_shift(tcol, int(math.log2(SEL_BLOCK)))
    imp = jnp.where(blk * SEL_BLOCK > tcol, -FORCE, imp)
    imp = jnp.where(blk == 0, FORCE, imp)
    imp = jnp.where(blk == cur, FORCE, imp)
    imp = jnp.where(blk == cur - 1, FORCE, imp)
    return o, jnp.where(_topk_mask_axis0(imp, n_top) > 0.0, 0.0, NEG)


def _nsa_body(q_ref, kvc_ref, ov_ref, gl_ref, kvs_ref, vts_ref, bs_ref, hot_ref, kvw_ref, vtw_ref, bw_ref, o_ref,
              m_sc, acc_sc, s_sc, mx_sc, *, n_c, n_top, n_sel):
    L = ATT_TILE
    qi = pl.program_id(2)
    last = vts_ref.shape[0] - 1
    qt = q_ref[...].astype(F32).T.astype(BF16)
    gates = jax.nn.sigmoid(gl_ref[...])
    ones = jnp.ones((ONES_ROWS, L), BF16)
    k_lanes = lax.broadcasted_iota(jnp.int32, (L, LANES), 1) < B_DH

    def gated(o, branch):
        return jnp.concatenate([o[:, h * L:(h + 1) * L] * gates[branch, h:h + 1] for h in range(B_HPG)], axis=0)

    def attend(qm, kv_ref, vt_ref, bias_ref, use_hot, tiles):
        m_sc[...] = jnp.full(m_sc.shape, NEG, F32)
        acc_sc[...] = jnp.zeros(acc_sc.shape, F32)
        q_aug_t = jnp.concatenate([jnp.concatenate([qt[h * B_DH:(h + 1) * B_DH], qm], axis=0)
                                   for h in range(B_HPG)], axis=1)
        n_bt = bias_ref.shape[0]

        def scores(kt):
            off = pl.multiple_of(jnp.clip(kt, 0, last) * L, L)
            idx = jnp.where(kt < 0, 0, jnp.clip(qi - kt + 1, 0, n_bt - 1))
            k_aug = kv_ref[pl.ds(off, L), :]
            if use_hot:
                k_aug = jnp.where(k_lanes, k_aug, hot_ref[pl.ds(off, L), :])
            return jnp.dot(k_aug, q_aug_t, preferred_element_type=F32) + bias_ref[idx]

        def values(kt):
            return jnp.concatenate([vt_ref[jnp.clip(kt, 0, last)], ones], axis=0)

        if tiles is None:
            _flash_loop(0, qi + 1, scores, values, m_sc, acc_sc, s_sc, mx_sc)
        else:
            _flash_loop(None, tiles, scores, values, m_sc, acc_sc, s_sc, mx_sc)
        acc = acc_sc[...]
        return acc[:B_DH] / jnp.maximum(acc[B_DH:B_DH + 1], 1e-30)

    o_cmp, sel = _compressed_branch(qt, kvc_ref[...], ov_ref[...], qi, n_c, n_top, n_sel)
    window_tiles = [qi - j for j in range(bw_ref.shape[0] - 1)]
    o_win = attend(jnp.zeros((B_DH, L), BF16), kvw_ref, vtw_ref, bw_ref, False, window_tiles)
    qm = sel.astype(BF16)
    if n_sel < B_DH:
        qm = jnp.concatenate([qm, jnp.zeros((B_DH - n_sel, L), BF16)], axis=0)
    o_sel = attend(qm, kvs_ref, vts_ref, bs_ref, True, None)
    total = gated(o_cmp, 0) + gated(o_sel, 1) + gated(o_win, 2)
    o_ref[...] = total.T.astype(o_ref.dtype)


def _nsa_attention(ha, kvc, overlap_t, glog_t, vt_sel, bias_sel, hot, vt_win, bias_win, b, t, n_c, n_top):
    L = ATT_TILE
    ncp = kvc.shape[2]
    n_sel = t // SEL_BLOCK
    kv_spec = lambda br: pl.BlockSpec((None, t, LANES), lambda bi, g, qi: (bi, 0, 40 + 4 * br + g))
    vt_spec = pl.BlockSpec((None, None, t // L, B_DH, L), lambda bi, g, qi: (bi, g, 0, 0, 0))
    bias_spec = lambda bias: pl.BlockSpec((bias.shape[0], None, L, B_HPG * L), lambda bi, g, qi: (0, g, 0, 0))
    return pl.pallas_call(
        functools.partial(_nsa_body, n_c=n_c, n_top=n_top, n_sel=n_sel),
        grid=(b, B_GROUPS, t // L),
        in_specs=[
            pl.BlockSpec((None, L, 256), lambda bi, g, qi: (bi, qi, 16 + g)),
            pl.BlockSpec((None, None, ncp, 2 * B_DH), lambda bi, g, qi: (bi, g, 0, 0)),
            pl.BlockSpec(overlap_t.shape, lambda bi, g, qi: (0, 0)),
            pl.BlockSpec((3, None, None, B_HPG, L), lambda bi, g, qi: (0, bi, g, 0, qi)),
            kv_spec(1), vt_spec, bias_spec(bias_sel),
            pl.BlockSpec((t, LANES), lambda bi, g, qi: (0, 0)),
            kv_spec(2), vt_spec, bias_spec(bias_win),
        ],
        out_specs=pl.BlockSpec((None, L, 256), lambda bi, g, qi: (bi, qi, g)),
        out_shape=jax.ShapeDtypeStruct((b, t, MIX_W), BF16),
        scratch_shapes=[pltpu.VMEM((8, B_HPG * L), F32), pltpu.VMEM((B_DH + ONES_ROWS, B_HPG * L), F32),
                        pltpu.VMEM((2, L, B_HPG * L), F32), pltpu.VMEM((2, 8, B_HPG * L), F32)],
        compiler_params=_cparams(("parallel", "parallel", "arbitrary")),
        name="nsa_attention",
    )(ha, kvc, overlap_t, glog_t, ha, vt_sel, bias_sel, hot, ha, vt_win, bias_win)


def _mem_body(q_ref, k_ref, v_ref, o_ref):
    q = q_ref[...]
    outs = []
    for h in range(M_HEADS):
        sl = slice(h * M_DH, (h + 1) * M_DH)
        s = _dot_nt(q[:, sl], k_ref[:, sl])
        e = jnp.exp2(s - jnp.max(s, axis=1, keepdims=True))
        p = e / jnp.sum(e, axis=1, keepdims=True)
        outs.append(jnp.dot(p.astype(BF16), v_ref[:, sl], preferred_element_type=F32))
    o_ref[...] = jnp.concatenate(outs, axis=1).astype(o_ref.dtype)


def _memory_attention(ha, memkv, b, t, n_mem):
    tq = 512
    w = M_HEADS * M_DH
    return pl.pallas_call(
        _mem_body,
        grid=(b, t // tq),
        in_specs=[
            pl.BlockSpec((None, tq, w), lambda bi, qi: (bi, qi, 0)),
            pl.BlockSpec((n_mem, w), lambda bi, qi: (bi, 0)),
            pl.BlockSpec((n_mem, w), lambda bi, qi: (bi, 1)),
        ],
        out_specs=pl.BlockSpec((None, tq, w), lambda bi, qi: (bi, qi, 0)),
        out_shape=jax.ShapeDtypeStruct((b, t, w), BF16),
        compiler_params=_cparams(("parallel", "arbitrary")),
        name="memory_attention",
    )(ha, memkv, memkv)


def _merge_body(x_ref, wg_ref, oa_ref, ob_ref, om_ref, wb_ref, o_ref, acc_sc):
    n = pl.program_id(1)

    @pl.when(n == 0)
    def _():
        acc_sc[...] = jnp.zeros(acc_sc.shape, F32)

    gate = jax.nn.sigmoid(_dot_nt(x_ref[...], wg_ref[...]))
    branch = jnp.where(n == 0, oa_ref[...], jnp.where(n == 1, ob_ref[...], om_ref[...]))
    acc = acc_sc[...] + gate * jnp.dot(branch, wb_ref[...], preferred_element_type=F32)
    acc_sc[...] = acc
    o_ref[...] = acc.astype(o_ref.dtype)


def _merge(xb, w_gate_t, o_a, o_b, o_m, w_branch):
    n = xb.shape[0]
    tm = 512
    row = lambda i, j: (i, 0)
    return pl.pallas_call(
        _merge_body,
        grid=(n // tm, N_MERGE),
        in_specs=[pl.BlockSpec((tm, D_MODEL), row),
                  pl.BlockSpec((D_MODEL, D_MODEL), lambda i, j: (j, 0))]
                 + [pl.BlockSpec((tm, MIX_W), row)] * 3
                 + [pl.BlockSpec((None, MIX_W, D_MODEL), lambda i, j: (j, 0, 0))],
        out_specs=pl.BlockSpec((tm, D_MODEL), row),
        out_shape=jax.ShapeDtypeStruct((n, D_MODEL), BF16),
        scratch_shapes=[pltpu.VMEM((tm, D_MODEL), F32)],
        compiler_params=_cparams(("parallel", "arbitrary")),
        name="branch_merge",
    )(xb, w_gate_t, o_a, o_b, o_m, w_branch)


def _layer_norm(z, g, b):
    mu = jnp.mean(z, axis=1, keepdims=True)
    zc = z - mu
    var = jnp.mean(zc * zc, axis=1, keepdims=True)
    return zc * lax.rsqrt(var + LN_EPS) * g + b


def _out_ln_body(y_ref, w_ref, x_ref, g_ref, b_ref, wq_ref, o_ref, ob_ref, q_ref):
    y = jnp.dot(y_ref[...], w_ref[...], preferred_element_type=F32)
    o = _layer_norm(ALPHA * x_ref[...] + y, g_ref[...], b_ref[...])
    o_ref[...] = o
    ob = o.astype(BF16)
    ob_ref[...] = ob
    q_ref[...] = jnp.dot(ob, wq_ref[...], preferred_element_type=F32).astype(q_ref.dtype)


def _out_proj_ln(mixed, w_out, x, g, b, w_q):
    n = x.shape[0]
    nq = w_q.shape[1]
    tm = 512
    row = lambda i: (i, 0)
    const = lambda i: (0, 0)
    once = dict(pipeline_mode=pl.Buffered(1))
    return pl.pallas_call(
        _out_ln_body,
        grid=(n // tm,),
        in_specs=[pl.BlockSpec((tm, D_MODEL), row), pl.BlockSpec((D_MODEL, D_MODEL), const, **once),
                  pl.BlockSpec((tm, D_MODEL), row), pl.BlockSpec((1, D_MODEL), const),
                  pl.BlockSpec((1, D_MODEL), const), pl.BlockSpec((D_MODEL, nq), const, **once)],
        out_specs=[pl.BlockSpec((tm, D_MODEL), row), pl.BlockSpec((tm, D_MODEL), row), pl.BlockSpec((tm, nq), row)],
        out_shape=[jax.ShapeDtypeStruct((n, D_MODEL), F32), jax.ShapeDtypeStruct((n, D_MODEL), BF16),
                   jax.ShapeDtypeStruct((n, nq), BF16)],
        compiler_params=_cparams(("parallel",)),
        name="out_proj_ln1_query",
    )(mixed, w_out, x, g, b, w_q)


def _res_ln_body(x_ref, y_ref, g_ref, b_ref, o_ref):
    o_ref[...] = _layer_norm(ALPHA * x_ref[...] + y_ref[...], g_ref[...], b_ref[...])


def _residual_ln(x, y, g, b):
    n = x.shape[0]
    tm = 512
    row = lambda i: (i, 0)
    const = lambda i: (0, 0)
    return pl.pallas_call(
        _res_ln_body,
        grid=(n // tm,),
        in_specs=[pl.BlockSpec((tm, D_MODEL), row), pl.BlockSpec((tm, D_MODEL), row),
                  pl.BlockSpec((1, D_MODEL), const), pl.BlockSpec((1, D_MODEL), const)],
        out_specs=pl.BlockSpec((tm, D_MODEL), row),
        out_shape=jax.ShapeDtypeStruct((n, D_MODEL), F32),
        compiler_params=_cparams(("parallel",)),
        name="residual_ln2",
    )(x, y, g, b)


def _topk_axis0(v, k):
    r, n = v.shape
    iota = lax.broadcasted_iota(jnp.int32, (r, n), 0).astype(F32)
    vals, idxs = [], []
    for _ in range(k):
        mx = jnp.max(v, axis=0, keepdims=True)
        idx = jnp.min(jnp.where(v == mx, iota, float(r)), axis=0, keepdims=True)
        v = jnp.where(iota == idx, -jnp.inf, v)
        vals.append(mx)
        idxs.append(idx)
    return jnp.concatenate(vals, axis=0), jnp.concatenate(idxs, axis=0)


def _pick_rows(table, pos, k):
    out = jnp.zeros(pos.shape, F32)
    for a in range(k):
        out = out + jnp.where(pos == float(a), table[a:a + 1], 0.0)
    return out


def _route_body(q_ref, keys_ref, ei_ref, ej_ref, g_ref):
    k = PEER_TOPK
    scores = _dot_nt(keys_ref[...], q_ref[...])
    v0, i0 = _topk_axis0(scores[:PEER_NKEYS], k)
    v1, i1 = _topk_axis0(scores[PEER_NKEYS:], k)
    counts = [k // (a + 1) for a in range(k)]
    starts = np.cumsum([0] + counts[:-1])
    pad = (-sum(counts)) % 8
    comb = jnp.concatenate([v0[a:a + 1] + v1[:counts[a]] for a in range(k)]
                           + [jnp.full((pad, v0.shape[1]), -jnp.inf, F32)], axis=0)
    sf, pos = _topk_axis0(comb, k)
    pa = jnp.zeros(pos.shape, F32)
    pb = pos
    for a in range(1, k):
        later = pos >= float(starts[a])
        pa = pa + jnp.where(later, 1.0, 0.0)
        pb = pb - jnp.where(later, float(counts[a - 1]), 0.0)
    ei_ref[...] = _pick_rows(i0, pa, k)
    ej_ref[...] = _pick_rows(i1, pb, k)
    e = jnp.exp(sf - jnp.max(sf, axis=0, keepdims=True))
    g_ref[...] = e / jnp.sum(e, axis=0, keepdims=True)


def _peer_route(q, keys):
    n = q.shape[0]
    tn = 512
    out = jax.ShapeDtypeStruct((PEER_HEADS, PEER_TOPK, n), F32)
    ospec = pl.BlockSpec((None, PEER_TOPK, tn), lambda i, h: (h, 0, i))
    return pl.pallas_call(
        _route_body,
        grid=(n // tn, PEER_HEADS),
        in_specs=[pl.BlockSpec((tn, PEER_DKEY), lambda i, h: (i, h)),
                  pl.BlockSpec((None, 2 * PEER_NKEYS, PEER_DKEY), lambda i, h: (h, 0, 0))],
        out_specs=[ospec, ospec, ospec],
        out_shape=[out, out, out],
        compiler_params=_cparams(("parallel", "arbitrary")),
        name="peer_route",
    )(q, keys)


def _gate_body(ei_ref, ej_ref, g_ref, o_ref):
    tb = ei_ref.shape[0]
    nk = PEER_NKEYS
    iota = lax.broadcasted_iota(jnp.int32, (tb, nk, ei_ref.shape[2]), 1).astype(F32)
    rows = jnp.where(iota == ei_ref[...], 1.0, 0.0).astype(BF16)
    cols = jnp.where(iota == ej_ref[...], g_ref[...], 0.0).astype(BF16)
    gm = lax.dot_general(rows, cols, (((2,), (2,)), ((0,), (0,))), preferred_element_type=F32)
    o_ref[...] = jnp.swapaxes(gm, 0, 1).astype(o_ref.dtype)


def _gate_matrix(ei, ej, g):
    n, _, slots = ei.shape
    tb = 128
    spec = pl.BlockSpec((tb, 1, slots), lambda i: (i, 0, 0))
    return pl.pallas_call(
        _gate_body,
        grid=(n // tb,),
        in_specs=[spec, spec, spec],
        out_specs=pl.BlockSpec((PEER_NKEYS, tb, PEER_NKEYS), lambda i: (0, i, 0)),
        out_shape=jax.ShapeDtypeStruct((PEER_NKEYS, n, PEER_NKEYS), BF16),
        compiler_params=_cparams(("parallel",)),
        name="peer_gate_matrix",
    )(ei, ej, g)


def _expert_body(x_ref, u_ref, g_ref, v_ref, o_ref):
    @pl.when(pl.program_id(1) == 0)
    def _():
        o_ref[...] = jnp.zeros(o_ref.shape, F32)

    hid = _dot_nt(x_ref[...], u_ref[...].astype(BF16))
    gate = jnp.concatenate([g_ref[i] for i in range(g_ref.shape[0])], axis=1)
    act = (jax.nn.gelu(hid) * gate.astype(F32)).astype(BF16)
    o_ref[...] += jnp.dot(act, v_ref[...].astype(BF16), preferred_element_type=F32)


def _peer_experts(xb, u, gm, v):
    n = xb.shape[0]
    ne = u.shape[0]
    tn, te = 1024, 1024
    once = dict(pipeline_mode=pl.Buffered(1))
    return pl.pallas_call(
        _expert_body,
        grid=(n // tn, ne // te),
        in_specs=[pl.BlockSpec((tn, D_MODEL), lambda i, j: (i, 0), **once),
                  pl.BlockSpec((te, D_MODEL), lambda i, j: (j, 0)),
                  pl.BlockSpec((te // PEER_NKEYS, tn, PEER_NKEYS), lambda i, j: (j, i, 0)),
                  pl.BlockSpec((te, D_MODEL), lambda i, j: (j, 0))],
        out_specs=pl.BlockSpec((tn, D_MODEL), lambda i, j: (i, 0), **once),
        out_shape=jax.ShapeDtypeStruct((n, D_MODEL), F32),
        compiler_params=_cparams(("parallel", "arbitrary")),
        name="peer_experts",
    )(xb, u, gm, v)


def _token_mixer(x, mem, w_in, diff_lambda, diff_subln, cmp_pe, cmp_w1, cmp_w2, w_mem_kv, w_branch, rel_bias):
    b, t, _ = x.shape
    n = b * t
    L = ATT_TILE

    w_a, w_mg, w_g = _w_prep(w_in.T)
    ha, hg, xb = _in_proj(x.reshape(n, D_MODEL), w_a, w_g)
    ha3 = ha.reshape(b, t, ha.shape[1])

    tab1d = rel_bias[_rel_bucket(jnp.arange(t))].T * LOG2E
    n_far = min(t // L, REL_MAX_DIST // L + 2)
    bias_a = _bias_tiles(tab1d[:A_HEADS], t, -1, n_far + 1, t, 1)
    bias_b = tab1d[A_HEADS:]
    bias_sel = _bias_tiles(bias_b, t, -1, n_far + 1, t, B_HPG)
    n_win = min(t // L, WINDOW // L + 1)
    bias_win = _bias_tiles(bias_b, t, -1, n_win + 1, WINDOW, B_HPG)

    def values_t(v):
        heads, dv = v.shape[2:]
        return v.reshape(b, t // L, L, heads, dv).transpose(0, 3, 1, 4, 2)

    o_a = _diff_attention(ha3, values_t(ha3[:, :, 3072:4096].reshape(b, t, A_HEADS, A_DV)), bias_a, diff_lambda,
                          jnp.broadcast_to(diff_subln[:, None], (A_DV, LANES)), b, t)

    glog_t = hg[:, :48].reshape(b, t, 3, B_GROUPS, B_HPG).transpose(2, 0, 3, 4, 1)
    nr = t // CMP_STRIDE
    w1 = cmp_w1.reshape(2, CMP_LEN, B_DH, CMP_HIDDEN)
    z1 = jnp.zeros_like(w1[0])
    w1_bd = jnp.concatenate([jnp.concatenate([w1[0], z1], axis=2), jnp.concatenate([z1, w1[1]], axis=2)],
                            axis=1).astype(BF16)
    z2 = jnp.zeros_like(cmp_w2[0])
    w2_bd = jnp.concatenate([jnp.concatenate([cmp_w2[0], z2], axis=1), jnp.concatenate([z2, cmp_w2[1]], axis=1)],
                            axis=0).astype(BF16)
    pe_bd = jnp.broadcast_to(jnp.concatenate([cmp_pe[0], cmp_pe[1]], axis=1)[:, None, :],
                             (CMP_LEN, 8, 2 * B_DH)).astype(BF16)
    kvc = _compress(ha3[:, :, 5120:5632].astype(F32), w1_bd, w2_bd, pe_bd)
    n_c = (t - CMP_LEN) // CMP_STRIDE + 1
    n_sel = t // SEL_BLOCK
    cidx = np.arange(nr)[:, None] * CMP_STRIDE + np.arange(CMP_LEN)[None, :]
    overlap = (cidx[:, :, None] // SEL_BLOCK == np.arange(n_sel)[None, None, :]).astype(np.float32).mean(axis=1)
    overlap[n_c:] = 0.0
    overlap_t = np.zeros((LANES, nr), np.float32)
    overlap_t[:n_sel] = overlap.T
    v_sel = ha3[:, :, 5632:6144].reshape(b, t, B_GROUPS, 2, B_DH)[:, :, :, 1]
    v_win = ha3[:, :, 6144:6656].reshape(b, t, B_GROUPS, 2, B_DH)[:, :, :, 1]
    assert n_sel <= B_DH
    hot = np.zeros((t, LANES), np.float32)
    hot[np.arange(t), B_DH + np.arange(t) // SEL_BLOCK] = 1.0
    o_b = _nsa_attention(ha3, kvc, jnp.asarray(overlap_t, BF16), glog_t, values_t(v_sel), bias_sel,
                         jnp.asarray(hot, BF16), values_t(v_win), bias_win, b, t, n_c, min(SEL_TOPN, n_sel))

    n_mem = mem.shape[1]
    memkv = _matmul(mem.reshape(b * n_mem, D_MODEL).astype(BF16), w_mem_kv.astype(BF16), BF16,
                    b * n_mem, 1024, "mem_kv_proj")
    o_m = _memory_attention(ha3, memkv, b, t, n_mem)

    flat = lambda o: o.reshape(n, MIX_W)
    return _merge(xb, w_mg, flat(o_a), flat(o_b), flat(o_m), w_branch.astype(BF16))


def _peer(x1b, q, peer_keys, peer_u, peer_v):
    n = x1b.shape[0]
    zk = jnp.zeros((PEER_HEADS, PEER_NKEYS, PEER_DKEY // 2), F32)
    keys_bd = jnp.concatenate([jnp.concatenate([peer_keys[:, 0], zk], axis=2),
                               jnp.concatenate([zk, peer_keys[:, 1]], axis=2)], axis=1).astype(BF16)
    ei, ej, gate = _peer_route(q, keys_bd)
    slots = lambda a: a.reshape(PEER_HEADS * PEER_TOPK, n).T.reshape(n, 1, PEER_HEADS * PEER_TOPK)
    gm = _gate_matrix(slots(ei), slots(ej), slots(gate))
    return _peer_experts(x1b, peer_u, gm, peer_v)


def kernel(x, mem, w_in, diff_lambda, diff_subln, cmp_pe, cmp_w1, cmp_w2, w_mem_kv, w_branch, w_out, ln1_g, ln1_b,
           peer_wq, peer_keys, peer_u, peer_v, ln2_g, ln2_b, rel_bias):
    b, t, _ = x.shape
    n = b * t
    for l in range(DEPTH):
        mixed = _token_mixer(x, mem, w_in[l], diff_lambda[l], diff_subln[l], cmp_pe[l], cmp_w1[l], cmp_w2[l],
                             w_mem_kv[l], w_branch[l], rel_bias)
        x1, x1b, q = _out_proj_ln(mixed, w_out[l].astype(BF16), x.reshape(n, D_MODEL), ln1_g[l].reshape(1, D_MODEL),
                                  ln1_b[l].reshape(1, D_MODEL), peer_wq[l].astype(BF16))
        y = _peer(x1b, q, peer_keys[l], peer_u[l], peer_v[l])
        x = _residual_ln(x1, y, ln2_g[l].reshape(1, D_MODEL), ln2_b[l].reshape(1, D_MODEL)).reshape(b, t, D_MODEL)
    return x
```

```python
import functools
import math

import numpy as np
import jax
import jax.numpy as jnp
from jax import lax
from jax.experimental import pallas as pl
from jax.experimental.pallas import tpu as pltpu

F32 = jnp.float32
BF16 = jnp.bfloat16

D_MODEL = 2048
A_HEADS, A_DQK, A_DV = 8, 64, 128
B_HEADS, B_GROUPS, B_HPG, B_DH = 16, 4, 4, 64
CMP_LEN, CMP_STRIDE, CMP_HIDDEN = 32, 16, 256
SEL_BLOCK, SEL_TOPN, WINDOW = 64, 16, 512
M_HEADS, M_DH = 4, 256
REL_BUCKETS, REL_MAX_DIST = 32, 1024
PEER_HEADS, PEER_NKEYS, PEER_DKEY, PEER_TOPK = 8, 128, 256, 16
MIX_W = 1024
LN_EPS = 1e-5
FORCE = 1e9
DEPTH = 1
ALPHA = (2 * DEPTH) ** 0.25
LAMBDA_INIT = 0.8 - 0.6 * math.exp(-0.3 * 0)
LOG2E = math.log2(math.e)

NEG = -1e30
ATT_TILE = 256
LANES = 128
ONES_ROWS = 16
N_MERGE = 3
VMEM_LIMIT = 56 * 1024 * 1024


def _cparams(sem):
    return pltpu.CompilerParams(dimension_semantics=sem, vmem_limit_bytes=VMEM_LIMIT)


def _dot_nt(a, b):
    return lax.dot_general(a, b, (((1,), (1,)), ((), ())), preferred_element_type=F32)


def _mm_body(x_ref, w_ref, o_ref):
    o_ref[...] = jnp.dot(x_ref[...], w_ref[...], preferred_element_type=F32).astype(o_ref.dtype)


def _matmul(x, w, out_dtype, tm, tn, name):
    m, k = x.shape
    n = w.shape[1]
    return pl.pallas_call(
        _mm_body,
        grid=(m // tm, n // tn),
        in_specs=[pl.BlockSpec((tm, k), lambda i, j: (i, 0)), pl.BlockSpec((k, tn), lambda i, j: (0, j))],
        out_specs=pl.BlockSpec((tm, tn), lambda i, j: (i, j)),
        out_shape=jax.ShapeDtypeStruct((m, n), out_dtype),
        compiler_params=_cparams(("parallel", "arbitrary")),
        name=name,
    )(x, w)


def _in_proj_body(x_ref, w_ref, wg_ref, o_ref, og_ref, xb_ref):
    xb = x_ref[...].astype(BF16)
    o_ref[...] = _dot_nt(xb, w_ref[...]).astype(o_ref.dtype)

    @pl.when(pl.program_id(1) == 0)
    def _():
        xb_ref[...] = xb
        og_ref[...] = _dot_nt(xb, wg_ref[...])


def _in_proj(x, w_a, w_g):
    n, k = x.shape
    tm, tn = 512, 1664
    return pl.pallas_call(
        _in_proj_body,
        grid=(n // tm, W_A_COLS // tn),
        in_specs=[pl.BlockSpec((tm, k), lambda i, j: (i, 0)),
                  pl.BlockSpec((tn, k), lambda i, j: (j, 0)),
                  pl.BlockSpec((W_G_COLS, k), lambda i, j: (0, 0))],
        out_specs=[pl.BlockSpec((tm, tn), lambda i, j: (i, j)),
                   pl.BlockSpec((tm, W_G_COLS), lambda i, j: (i, 0)),
                   pl.BlockSpec((tm, k), lambda i, j: (i, 0))],
        out_shape=[jax.ShapeDtypeStruct((n, W_A_COLS), BF16), jax.ShapeDtypeStruct((n, W_G_COLS), F32),
                   jax.ShapeDtypeStruct((n, k), BF16)],
        compiler_params=_cparams(("parallel", "arbitrary")),
        name="in_proj",
    )(x, w_a, w_g)


W_A_COLS = 6656
W_G_COLS = 128


def _w_prep_body(w_ref, wa_ref, wm_ref, wg_ref):
    def put(dst_ref, d0, s0, n, c=1.0):
        val = w_ref[s0:s0 + n, :]
        dst_ref[d0:d0 + n, :] = (val * c if c != 1.0 else val).astype(dst_ref.dtype)

    put(wa_ref, 0, 5680, 1024, M_DH ** -0.5 * LOG2E)
    put(wa_ref, 1024, 0, 1024, A_DQK ** -0.5 * LOG2E)
    put(wa_ref, 2048, 1024, 2048)
    put(wa_ref, 4096, 3072, 1024, B_DH ** -0.5 * LOG2E)
    for br in range(3):
        for g in range(B_GROUPS):
            src = 4096 + br * 2 * B_GROUPS * B_DH + g * B_DH
            dst = 5120 + (br * B_GROUPS + g) * 2 * B_DH
            put(wa_ref, dst, src, B_DH)
            put(wa_ref, dst + B_DH, src + B_GROUPS * B_DH, B_DH)
    put(wm_ref, 0, 6704, N_MERGE * D_MODEL)
    put(wg_ref, 0, 5632, 48)
    wg_ref[48:, :] = jnp.zeros((W_G_COLS - 48, wg_ref.shape[1]), wg_ref.dtype)


def _w_prep(w_in_t):
    c, k = w_in_t.shape
    tk = 256
    col = lambda i: (0, i)
    return pl.pallas_call(
        _w_prep_body,
        grid=(k // tk,),
        in_specs=[pl.BlockSpec((c, tk), col)],
        out_specs=[pl.BlockSpec((W_A_COLS, tk), col), pl.BlockSpec((N_MERGE * D_MODEL, tk), col),
                   pl.BlockSpec((W_G_COLS, tk), col)],
        out_shape=[jax.ShapeDtypeStruct((W_A_COLS, k), BF16), jax.ShapeDtypeStruct((N_MERGE * D_MODEL, k), BF16),
                   jax.ShapeDtypeStruct((W_G_COLS, k), BF16)],
        compiler_params=_cparams(("parallel",)),
        name="w_in_regroup",
    )(w_in_t)


def _rel_bucket(dist):
    n = jnp.maximum(dist, 0)
    max_exact = REL_BUCKETS // 2
    nf = jnp.maximum(n, 1).astype(jnp.float32)
    large = max_exact + (jnp.log(nf / max_exact) / math.log(REL_MAX_DIST / max_exact)
                         * (REL_BUCKETS - max_exact)).astype(jnp.int32)
    large = jnp.minimum(large, REL_BUCKETS - 1)
    return jnp.where(n < max_exact, n, large)


def _bias_tiles(tab1d, t, first, n_tiles, max_dist, hpr):
    L = ATT_TILE
    m = np.arange(2 * L)
    off = np.where(m <= L, m, m - 2 * L)
    d = (first + np.arange(n_tiles))[:, None] * L + off[None, :]
    ok = (d >= 0) & (d < max_dist)
    h = tab1d.shape[0]
    rp = jnp.where(ok[:, None], tab1d.T[np.clip(d, 0, t - 1)].transpose(0, 2, 1), NEG)
    return pl.pallas_call(
        functools.partial(_toeplitz_body, hpr=hpr),
        grid=(n_tiles,),
        in_specs=[pl.BlockSpec((None, h, 1, 2 * L), lambda c: (c, 0, 0, 0))],
        out_specs=pl.BlockSpec((None, h // hpr, L, hpr * L), lambda c: (c, 0, 0, 0)),
        out_shape=jax.ShapeDtypeStruct((n_tiles, h // hpr, L, hpr * L), F32),
        compiler_params=_cparams(("parallel",)),
        name="bias_tiles",
    )(rp.reshape(n_tiles, h, 1, 2 * L))


def _toeplitz_body(rp_ref, o_ref, *, hpr):
    L = o_ref.shape[1]
    for hd in range(rp_ref.shape[0]):
        rows = jnp.broadcast_to(rp_ref[hd], (L, 2 * L))
        tile = pltpu.roll(rows, 0, 1, stride=1, stride_axis=0)[:, :L]
        o_ref[hd // hpr, :, (hd % hpr) * L:(hd % hpr + 1) * L] = tile


def _flash_loop(lo, hi, scores, values, m_ref, acc_ref, s_ref, mx_ref):
    def ahead(kt, slot):
        s = scores(kt)
        s_ref[slot] = s
        mx_ref[slot] = jnp.broadcast_to(jnp.max(s, axis=0, keepdims=True), mx_ref.shape[1:])

    def finish(kt, slot):
        m_prev = m_ref[...]
        m_new = jnp.maximum(m_prev, mx_ref[slot])
        alpha = jnp.exp2(m_prev - m_new)
        p = jnp.exp2(s_ref[slot] - m_new[0:1])
        acc_ref[...] = alpha[0:1] * acc_ref[...] + jnp.dot(values(kt), p.astype(BF16), preferred_element_type=F32)
        m_ref[...] = m_new

    if lo is None:
        tiles = hi
        ahead(tiles[0], 0)
        for j, kt in enumerate(tiles):
            if j + 1 < len(tiles):
                ahead(tiles[j + 1], (j + 1) % 2)
            finish(kt, j % 2)
        return

    def body(i, carry):
        kt = lo + 2 * i
        ahead(kt + 1, 1)
        finish(kt, 0)
        ahead(kt + 2, 0)
        finish(kt + 1, 1)
        return carry

    ahead(lo, 0)
    lax.fori_loop(0, (hi - lo + 1) // 2, body, 0)


def _diff_body(q_ref, k_ref, vt_ref, bias_ref, lam_ref, g_ref, o_ref, m_sc, acc_sc, s_sc, mx_sc, *, n_bt):
    L = ATT_TILE
    tq = 2 * L
    last = vt_ref.shape[0] - 1
    qi = pl.program_id(2)
    m_sc[...] = jnp.full(m_sc.shape, NEG, F32)
    acc_sc[...] = jnp.zeros(acc_sc.shape, F32)
    qt = q_ref[...].astype(F32).T.astype(BF16)
    zero = jnp.zeros((A_DQK, tq), BF16)
    q_cat = jnp.concatenate([jnp.concatenate([qt[:A_DQK], zero], axis=0),
                             jnp.concatenate([zero, qt[A_DQK:]], axis=0)], axis=1)

    def scores(kt):
        off = pl.multiple_of(jnp.minimum(kt, last) * L, L)
        k = k_ref[pl.ds(off, L), :]
        d0 = 2 * qi - kt
        bias = jnp.concatenate([bias_ref[jnp.clip(d0 + 1, 0, n_bt - 1)],
                                bias_ref[jnp.clip(d0 + 2, 0, n_bt - 1)]], axis=1)
        s = jnp.dot(k, q_cat, preferred_element_type=F32)
        return jnp.concatenate([s[:, :tq] + bias, s[:, tq:] + bias], axis=1)

    ones = jnp.ones((ONES_ROWS, L), BF16)

    def values(kt):
        return jnp.concatenate([vt_ref[jnp.minimum(kt, last)], ones], axis=0)

    def unrolled(n_tiles):
        def run():
            _flash_loop(None, list(range(n_tiles)), scores, values, m_sc, acc_sc, s_sc, mx_sc)
            return 0
        return run

    lax.switch(qi, [unrolled(2 * i + 2) for i in range((last + 1) * L // tq)])

    lp = lam_ref[...]
    lam = (jnp.exp(jnp.sum(lp[0:1] * lp[1:2], axis=1, keepdims=True))
           - jnp.exp(jnp.sum(lp[2:3] * lp[3:4], axis=1, keepdims=True)) + LAMBDA_INIT)
    acc = acc_sc[...]
    o0 = acc[:A_DV, :tq] / jnp.maximum(acc[A_DV:A_DV + 1, :tq], 1e-30)
    o1 = acc[:A_DV, tq:] / jnp.maximum(acc[A_DV:A_DV + 1, tq:], 1e-30)
    o = o0 - lam * o1
    g = jnp.concatenate([g_ref[...]] * (tq // LANES), axis=1)
    o = o * lax.rsqrt(jnp.mean(o * o, axis=0, keepdims=True) + LN_EPS) * g
    o_ref[...] = (o * (1.0 - LAMBDA_INIT)).T.astype(o_ref.dtype)


def _diff_attention(ha, v_t, bias, lam_params, subln, b, t):
    L = ATT_TILE
    tq = 2 * L
    n_bt = bias.shape[0]
    return pl.pallas_call(
        functools.partial(_diff_body, n_bt=n_bt),
        grid=(b, A_HEADS, t // tq),
        in_specs=[
            pl.BlockSpec((None, tq, 128), lambda bi, h, qi: (bi, qi, 8 + h)),
            pl.BlockSpec((None, t, 128), lambda bi, h, qi: (bi, 0, 16 + h)),
            pl.BlockSpec((None, None, t // L, A_DV, L), lambda bi, h, qi: (bi, h, 0, 0, 0)),
            pl.BlockSpec((n_bt, None, L, L), lambda bi, h, qi: (0, h, 0, 0)),
            pl.BlockSpec((4, A_DQK), lambda bi, h, qi: (0, 0)),
            pl.BlockSpec((A_DV, LANES), lambda bi, h, qi: (0, 0)),
        ],
        out_specs=pl.BlockSpec((None, tq, 128), lambda bi, h, qi: (bi, qi, h)),
        out_shape=jax.ShapeDtypeStruct((b, t, MIX_W), BF16),
        scratch_shapes=[pltpu.VMEM((8, 2 * tq), F32), pltpu.VMEM((A_DV + ONES_ROWS, 2 * tq), F32),
                        pltpu.VMEM((2, L, 2 * tq), F32), pltpu.VMEM((2, 8, 2 * tq), F32)],
        compiler_params=_cparams(("parallel", "parallel", "arbitrary")),
        name="diff_attention",
    )(ha, ha, v_t, bias, lam_params, subln)


def _compress_body(kv_ref, w1_ref, w2_ref, pe_ref, o_ref):
    nr = o_ref.shape[0]
    first = jnp.zeros((nr, 2 * CMP_HIDDEN), F32)
    second = jnp.zeros((nr, 2 * CMP_HIDDEN), F32)
    for l in range(CMP_STRIDE):
        rows = kv_ref[pl.ds(l, nr, stride=CMP_STRIDE), :].astype(BF16)
        first = first + jnp.dot(rows, w1_ref[l], preferred_element_type=F32)
        second = second + jnp.dot(rows, w1_ref[CMP_STRIDE + l], preferred_element_type=F32)
    pw = jnp.zeros((8, 2 * CMP_HIDDEN), F32)
    for l in range(CMP_LEN):
        pw = pw + jnp.dot(pe_ref[l], w1_ref[l], preferred_element_type=F32)
    second = jnp.concatenate([second[1:], second[:1]], axis=0)
    hdn = jax.nn.gelu(first + second + pw[0:1])
    o_ref[...] = jnp.dot(hdn.astype(BF16), w2_ref[...], preferred_element_type=F32)


def _compress(kv, w1_bd, w2_bd, pe_bd):
    b, t, _ = kv.shape
    nr = t // CMP_STRIDE
    return pl.pallas_call(
        _compress_body,
        grid=(b, B_GROUPS),
        in_specs=[
            pl.BlockSpec((None, t, LANES), lambda bi, gi: (bi, 0, gi)),
            pl.BlockSpec(w1_bd.shape, lambda bi, gi: (0, 0, 0)),
            pl.BlockSpec(w2_bd.shape, lambda bi, gi: (0, 0)),
            pl.BlockSpec(pe_bd.shape, lambda bi, gi: (0, 0, 0)),
        ],
        out_specs=pl.BlockSpec((None, None, nr, 2 * B_DH), lambda bi, gi: (bi, gi, 0, 0)),
        out_shape=jax.ShapeDtypeStruct((b, B_GROUPS, nr, 2 * B_DH), F32),
        compiler_params=_cparams(("parallel", "parallel")),
        name="nsa_compress",
    )(kv, w1_bd, w2_bd, pe_bd)


def _topk_mask_axis0(v, k):
    r, n = v.shape
    iota = lax.broadcasted_iota(jnp.int32, (r, n), 0).astype(F32)
    sel = jnp.zeros((r, n), F32)
    for _ in range(k):
        mx = jnp.max(v, axis=0, keepdims=True)
        idx = jnp.min(jnp.where(v == mx, iota, float(r)), axis=0, keepdims=True)
        hit = iota == idx
        v = jnp.where(hit, -jnp.inf, v)
        sel = jnp.where(hit, 1.0, sel)
    return sel


def _compressed_branch(qt, kvc, ov_t, qi, n_c, n_top, n_sel):
    L = ATT_TILE
    ncp = kvc.shape[0]
    qh = jnp.concatenate([qt[h * B_DH:(h + 1) * B_DH] for h in range(B_HPG)], axis=1)
    s = jnp.dot(kvc[:, :B_DH].astype(BF16), qh, preferred_element_type=F32)
    tcol = qi * L + lax.broadcasted_iota(jnp.int32, (1, L), 1)
    crow = lax.broadcasted_iota(jnp.int32, (ncp, 1), 0)
    seen = jnp.where(crow < n_c, crow * CMP_STRIDE + (CMP_LEN - 1), jnp.int32(2 ** 30)) <= tcol
    valid = jnp.concatenate([seen] * B_HPG, axis=1)
    s = jnp.where(valid, s, NEG)
    mx = jnp.max(s, axis=0, keepdims=True)
    e = jnp.where(valid, jnp.exp2(s - mx), 0.0)
    p = e / jnp.maximum(jnp.sum(e, axis=0, keepdims=True), 1e-30)
    o = jnp.dot(kvc.T[B_DH:].astype(BF16), p.astype(BF16), preferred_element_type=F32)

    psum = p[:, :L] + p[:, L:2 * L] + p[:, 2 * L:3 * L] + p[:, 3 * L:]
    imp = jnp.zeros((ov_t.shape[0], L), F32)
    rem = psum
    for _ in range(3):
        part = rem.astype(BF16)
        imp = imp + jnp.dot(ov_t, part, preferred_element_type=F32)
        rem = rem - part.astype(F32)
    imp = imp[:n_sel]
    blk = lax.broadcasted_iota(jnp.int32, (n_sel, 1), 0)
    cur = jnp.right_shift(tcol, int(math.log2(SEL_BLOCK)))
    imp = jnp.where(blk * SEL_BLOCK > tcol, -FORCE, imp)
    imp = jnp.where(blk == 0, FORCE, imp)
    imp = jnp.where(blk == cur, FORCE, imp)
    imp = jnp.where(blk == cur - 1, FORCE, imp)
    return o, jnp.where(_topk_mask_axis0(imp, n_top) > 0.0, 0.0, NEG)


def _nsa_body(q_ref, kvc_ref, ov_ref, gl_ref, kvs_ref, vts_ref, bs_ref, hot_ref, kvw_ref, vtw_ref, bw_ref, o_ref,
              m_sc, acc_sc, s_sc, mx_sc, *, n_c, n_top, n_sel):
    L = ATT_TILE
    qi = pl.program_id(2)
    last = vts_ref.shape[0] - 1
    qt = q_ref[...].astype(F32).T.astype(BF16)
    gates = jax.nn.sigmoid(gl_ref[...])
    ones = jnp.ones((ONES_ROWS, L), BF16)
    k_lanes = lax.broadcasted_iota(jnp.int32, (L, LANES), 1) < B_DH

    def gated(o, branch):
        return jnp.concatenate([o[:, h * L:(h + 1) * L] * gates[branch, h:h + 1] for h in range(B_HPG)], axis=0)

    def attend(qm, kv_ref, vt_ref, bias_ref, use_hot, tiles):
        m_sc[...] = jnp.full(m_sc.shape, NEG, F32)
        acc_sc[...] = jnp.zeros(acc_sc.shape, F32)
        q_aug_t = jnp.concatenate([jnp.concatenate([qt[h * B_DH:(h + 1) * B_DH], qm], axis=0)
                                   for h in range(B_HPG)], axis=1)
        n_bt = bias_ref.shape[0]

        def scores(kt):
            off = pl.multiple_of(jnp.clip(kt, 0, last) * L, L)
            idx = jnp.where(kt < 0, 0, jnp.clip(qi - kt + 1, 0, n_bt - 1))
            k_aug = kv_ref[pl.ds(off, L), :]
            if use_hot:
                k_aug = jnp.where(k_lanes, k_aug, hot_ref[pl.ds(off, L), :])
            return jnp.dot(k_aug, q_aug_t, preferred_element_type=F32) + bias_ref[idx]

        def values(kt):
            return jnp.concatenate([vt_ref[jnp.clip(kt, 0, last)], ones], axis=0)

        if tiles is None:
            _flash_loop(0, qi + 1, scores, values, m_sc, acc_sc, s_sc, mx_sc)
        else:
            _flash_loop(None, tiles, scores, values, m_sc, acc_sc, s_sc, mx_sc)
        acc = acc_sc[...]
        return acc[:B_DH] / jnp.maximum(acc[B_DH:B_DH + 1], 1e-30)

    o_cmp, sel = _compressed_branch(qt, kvc_ref[...], ov_ref[...], qi, n_c, n_top, n_sel)
    window_tiles = [qi - j for j in range(bw_ref.shape[0] - 1)]
    o_win = attend(jnp.zeros((B_DH, L), BF16), kvw_ref, vtw_ref, bw_ref, False, window_tiles)
    qm = sel.astype(BF16)
    if n_sel < B_DH:
        qm = jnp.concatenate([qm, jnp.zeros((B_DH - n_sel, L), BF16)], axis=0)
    o_sel = attend(qm, kvs_ref, vts_ref, bs_ref, True, None)
    total = gated(o_cmp, 0) + gated(o_sel, 1) + gated(o_win, 2)
    o_ref[...] = total.T.astype(o_ref.dtype)


def _nsa_attention(ha, kvc, overlap_t, glog_t, vt_sel, bias_sel, hot, vt_win, bias_win, b, t, n_c, n_top):
    L = ATT_TILE
    ncp = kvc.shape[2]
    n_sel = t // SEL_BLOCK
    kv_spec = lambda br: pl.BlockSpec((None, t, LANES), lambda bi, g, qi: (bi, 0, 40 + 4 * br + g))
    vt_spec = pl.BlockSpec((None, None, t // L, B_DH, L), lambda bi, g, qi: (bi, g, 0, 0, 0))
    bias_spec = lambda bias: pl.BlockSpec((bias.shape[0], None, L, B_HPG * L), lambda bi, g, qi: (0, g, 0, 0))
    return pl.pallas_call(
        functools.partial(_nsa_body, n_c=n_c, n_top=n_top, n_sel=n_sel),
        grid=(b, B_GROUPS, t // L),
        in_specs=[
            pl.BlockSpec((None, L, 256), lambda bi, g, qi: (bi, qi, 16 + g)),
            pl.BlockSpec((None, None, ncp, 2 * B_DH), lambda bi, g, qi: (bi, g, 0, 0)),
            pl.BlockSpec(overlap_t.shape, lambda bi, g, qi: (0, 0)),
            pl.BlockSpec((3, None, None, B_HPG, L), lambda bi, g, qi: (0, bi, g, 0, qi)),
            kv_spec(1), vt_spec, bias_spec(bias_sel),
            pl.BlockSpec((t, LANES), lambda bi, g, qi: (0, 0)),
            kv_spec(2), vt_spec, bias_spec(bias_win),
        ],
        out_specs=pl.BlockSpec((None, L, 256), lambda bi, g, qi: (bi, qi, g)),
        out_shape=jax.ShapeDtypeStruct((b, t, MIX_W), BF16),
        scratch_shapes=[pltpu.VMEM((8, B_HPG * L), F32), pltpu.VMEM((B_DH + ONES_ROWS, B_HPG * L), F32),
                        pltpu.VMEM((2, L, B_HPG * L), F32), pltpu.VMEM((2, 8, B_HPG * L), F32)],
        compiler_params=_cparams(("parallel", "parallel", "arbitrary")),
        name="nsa_attention",
    )(ha, kvc, overlap_t, glog_t, ha, vt_sel, bias_sel, hot, ha, vt_win, bias_win)


def _mem_body(q_ref, k_ref, v_ref, o_ref):
    q = q_ref[...]
    outs = []
    for h in range(M_HEADS):
        sl = slice(h * M_DH, (h + 1) * M_DH)
        s = _dot_nt(q[:, sl], k_ref[:, sl])
        e = jnp.exp2(s - jnp.max(s, axis=1, keepdims=True))
        p = e / jnp.sum(e, axis=1, keepdims=True)
        outs.append(jnp.dot(p.astype(BF16), v_ref[:, sl], preferred_element_type=F32))
    o_ref[...] = jnp.concatenate(outs, axis=1).astype(o_ref.dtype)


def _memory_attention(ha, memkv, b, t, n_mem):
    tq = 512
    w = M_HEADS * M_DH
    return pl.pallas_call(
        _mem_body,
        grid=(b, t // tq),
        in_specs=[
            pl.BlockSpec((None, tq, w), lambda bi, qi: (bi, qi, 0)),
            pl.BlockSpec((n_mem, w), lambda bi, qi: (bi, 0)),
            pl.BlockSpec((n_mem, w), lambda bi, qi: (bi, 1)),
        ],
        out_specs=pl.BlockSpec((None, tq, w), lambda bi, qi: (bi, qi, 0)),
        out_shape=jax.ShapeDtypeStruct((b, t, w), BF16),
        compiler_params=_cparams(("parallel", "arbitrary")),
        name="memory_attention",
    )(ha, memkv, memkv)


def _merge_body(x_ref, wg_ref, oa_ref, ob_ref, om_ref, wb_ref, o_ref, acc_sc):
    n = pl.program_id(1)

    @pl.when(n == 0)
    def _():
        acc_sc[...] = jnp.zeros(acc_sc.shape, F32)

    gate = jax.nn.sigmoid(_dot_nt(x_ref[...], wg_ref[...]))
    branch = jnp.where(n == 0, oa_ref[...], jnp.where(n == 1, ob_ref[...], om_ref[...]))
    acc = acc_sc[...] + gate * jnp.dot(branch, wb_ref[...], preferred_element_type=F32)
    acc_sc[...] = acc
    o_ref[...] = acc.astype(o_ref.dtype)


def _merge(xb, w_gate_t, o_a, o_b, o_m, w_branch):
    n = xb.shape[0]
    tm = 512
    row = lambda i, j: (i, 0)
    return pl.pallas_call(
        _merge_body,
        grid=(n // tm, N_MERGE),
        in_specs=[pl.BlockSpec((tm, D_MODEL), row),
                  pl.BlockSpec((D_MODEL, D_MODEL), lambda i, j: (j, 0))]
                 + [pl.BlockSpec((tm, MIX_W), row)] * 3
                 + [pl.BlockSpec((None, MIX_W, D_MODEL), lambda i, j: (j, 0, 0))],
        out_specs=pl.BlockSpec((tm, D_MODEL), row),
        out_shape=jax.ShapeDtypeStruct((n, D_MODEL), BF16),
        scratch_shapes=[pltpu.VMEM((tm, D_MODEL), F32)],
        compiler_params=_cparams(("parallel", "arbitrary")),
        name="branch_merge",
    )(xb, w_gate_t, o_a, o_b, o_m, w_branch)


def _layer_norm(z, g, b):
    mu = jnp.mean(z, axis=1, keepdims=True)
    zc = z - mu
    var = jnp.mean(zc * zc, axis=1, keepdims=True)
    return zc * lax.rsqrt(var + LN_EPS) * g + b


def _out_ln_body(y_ref, w_ref, x_ref, g_ref, b_ref, wq_ref, o_ref, ob_ref, q_ref):
    y = jnp.dot(y_ref[...], w_ref[...], preferred_element_type=F32)
    o = _layer_norm(ALPHA * x_ref[...] + y, g_ref[...], b_ref[...])
    o_ref[...] = o
    ob = o.astype(BF16)
    ob_ref[...] = ob
    q_ref[...] = jnp.dot(ob, wq_ref[...], preferred_element_type=F32).astype(q_ref.dtype)


def _out_proj_ln(mixed, w_out, x, g, b, w_q):
    n = x.shape[0]
    nq = w_q.shape[1]
    tm = 512
    row = lambda i: (i, 0)
    const = lambda i: (0, 0)
    once = dict(pipeline_mode=pl.Buffered(1))
    return pl.pallas_call(
        _out_ln_body,
        grid=(n // tm,),
        in_specs=[pl.BlockSpec((tm, D_MODEL), row), pl.BlockSpec((D_MODEL, D_MODEL), const, **once),
                  pl.BlockSpec((tm, D_MODEL), row), pl.BlockSpec((1, D_MODEL), const),
                  pl.BlockSpec((1, D_MODEL), const), pl.BlockSpec((D_MODEL, nq), const, **once)],
        out_specs=[pl.BlockSpec((tm, D_MODEL), row), pl.BlockSpec((tm, D_MODEL), row), pl.BlockSpec((tm, nq), row)],
        out_shape=[jax.ShapeDtypeStruct((n, D_MODEL), F32), jax.ShapeDtypeStruct((n, D_MODEL), BF16),
                   jax.ShapeDtypeStruct((n, nq), BF16)],
        compiler_params=_cparams(("parallel",)),
        name="out_proj_ln1_query",
    )(mixed, w_out, x, g, b, w_q)


def _res_ln_body(x_ref, y_ref, g_ref, b_ref, o_ref):
    o_ref[...] = _layer_norm(ALPHA * x_ref[...] + y_ref[...], g_ref[...], b_ref[...])


def _residual_ln(x, y, g, b):
    n = x.shape[0]
    tm = 512
    row = lambda i: (i, 0)
    const = lambda i: (0, 0)
    return pl.pallas_call(
        _res_ln_body,
        grid=(n // tm,),
        in_specs=[pl.BlockSpec((tm, D_MODEL), row), pl.BlockSpec((tm, D_MODEL), row),
                  pl.BlockSpec((1, D_MODEL), const), pl.BlockSpec((1, D_MODEL), const)],
        out_specs=pl.BlockSpec((tm, D_MODEL), row),
        out_shape=jax.ShapeDtypeStruct((n, D_MODEL), F32),
        compiler_params=_cparams(("parallel",)),
        name="residual_ln2",
    )(x, y, g, b)


def _topk_axis0(v, k):
    r, n = v.shape
    iota = lax.broadcasted_iota(jnp.int32, (r, n), 0).astype(F32)
    vals, idxs = [], []
    for _ in range(k):
        mx = jnp.max(v, axis=0, keepdims=True)
        idx = jnp.min(jnp.where(v == mx, iota, float(r)), axis=0, keepdims=True)
        v = jnp.where(iota == idx, -jnp.inf, v)
        vals.append(mx)
        idxs.append(idx)
    return jnp.concatenate(vals, axis=0), jnp.concatenate(idxs, axis=0)


def _pick_rows(table, pos, k):
    out = jnp.zeros(pos.shape, F32)
    for a in range(k):
        out = out + jnp.where(pos == float(a), table[a:a + 1], 0.0)
    return out


def _route_body(q_ref, keys_ref, ei_ref, ej_ref, g_ref):
    k = PEER_TOPK
    scores = _dot_nt(keys_ref[...], q_ref[...])
    v0, i0 = _topk_axis0(scores[:PEER_NKEYS], k)
    v1, i1 = _topk_axis0(scores[PEER_NKEYS:], k)
    counts = [k // (a + 1) for a in range(k)]
    starts = np.cumsum([0] + counts[:-1])
    pad = (-sum(counts)) % 8
    comb = jnp.concatenate([v0[a:a + 1] + v1[:counts[a]] for a in range(k)]
                           + [jnp.full((pad, v0.shape[1]), -jnp.inf, F32)], axis=0)
    sf, pos = _topk_axis0(comb, k)
    pa = jnp.zeros(pos.shape, F32)
    pb = pos
    for a in range(1, k):
        later = pos >= float(starts[a])
        pa = pa + jnp.where(later, 1.0, 0.0)
        pb = pb - jnp.where(later, float(counts[a - 1]), 0.0)
    ei_ref[...] = _pick_rows(i0, pa, k)
    ej_ref[...] = _pick_rows(i1, pb, k)
    e = jnp.exp(sf - jnp.max(sf, axis=0, keepdims=True))
    g_ref[...] = e / jnp.sum(e, axis=0, keepdims=True)


def _peer_route(q, keys):
    n = q.shape[0]
    tn = 512
    out = jax.ShapeDtypeStruct((PEER_HEADS, PEER_TOPK, n), F32)
    ospec = pl.BlockSpec((None, PEER_TOPK, tn), lambda i, h: (h, 0, i))
    return pl.pallas_call(
        _route_body,
        grid=(n // tn, PEER_HEADS),
        in_specs=[pl.BlockSpec((tn, PEER_DKEY), lambda i, h: (i, h)),
                  pl.BlockSpec((None, 2 * PEER_NKEYS, PEER_DKEY), lambda i, h: (h, 0, 0))],
        out_specs=[ospec, ospec, ospec],
        out_shape=[out, out, out],
        compiler_params=_cparams(("parallel", "arbitrary")),
        name="peer_route",
    )(q, keys)


def _gate_body(ei_ref, ej_ref, g_ref, o_ref):
    tb = ei_ref.shape[0]
    nk = PEER_NKEYS
    iota = lax.broadcasted_iota(jnp.int32, (tb, nk, ei_ref.shape[2]), 1).astype(F32)
    rows = jnp.where(iota == ei_ref[...], 1.0, 0.0).astype(BF16)
    cols = jnp.where(iota == ej_ref[...], g_ref[...], 0.0).astype(BF16)
    gm = lax.dot_general(rows, cols, (((2,), (2,)), ((0,), (0,))), preferred_element_type=F32)
    o_ref[...] = jnp.swapaxes(gm, 0, 1).astype(o_ref.dtype)


def _gate_matrix(ei, ej, g):
    n, _, slots = ei.shape
    tb = 128
    spec = pl.BlockSpec((tb, 1, slots), lambda i: (i, 0, 0))
    return pl.pallas_call(
        _gate_body,
        grid=(n // tb,),
        in_specs=[spec, spec, spec],
        out_specs=pl.BlockSpec((PEER_NKEYS, tb, PEER_NKEYS), lambda i: (0, i, 0)),
        out_shape=jax.ShapeDtypeStruct((PEER_NKEYS, n, PEER_NKEYS), BF16),
        compiler_params=_cparams(("parallel",)),
        name="peer_gate_matrix",
    )(ei, ej, g)


def _expert_body(x_ref, u_ref, g_ref, v_ref, o_ref):
    @pl.when(pl.program_id(1) == 0)
    def _():
        o_ref[...] = jnp.zeros(o_ref.shape, F32)

    hid = _dot_nt(x_ref[...], u_ref[...].astype(BF16))
    gate = jnp.concatenate([g_ref[i] for i in range(g_ref.shape[0])], axis=1)
    act = (jax.nn.gelu(hid) * gate.astype(F32)).astype(BF16)
    o_ref[...] += jnp.dot(act, v_ref[...].astype(BF16), preferred_element_type=F32)


def _peer_experts(xb, u, gm, v):
    n = xb.shape[0]
    ne = u.shape[0]
    tn, te = 1024, 1024
    once = dict(pipeline_mode=pl.Buffered(1))
    return pl.pallas_call(
        _expert_body,
        grid=(n // tn, ne // te),
        in_specs=[pl.BlockSpec((tn, D_MODEL), lambda i, j: (i, 0), **once),
                  pl.BlockSpec((te, D_MODEL), lambda i, j: (j, 0)),
                  pl.BlockSpec((te // PEER_NKEYS, tn, PEER_NKEYS), lambda i, j: (j, i, 0)),
                  pl.BlockSpec((te, D_MODEL), lambda i, j: (j, 0))],
        out_specs=pl.BlockSpec((tn, D_MODEL), lambda i, j: (i, 0), **once),
        out_shape=jax.ShapeDtypeStruct((n, D_MODEL), F32),
        compiler_params=_cparams(("parallel", "arbitrary")),
        name="peer_experts",
    )(xb, u, gm, v)


def _token_mixer(x, mem, w_in, diff_lambda, diff_subln, cmp_pe, cmp_w1, cmp_w2, w_mem_kv, w_branch, rel_bias):
    b, t, _ = x.shape
    n = b * t
    L = ATT_TILE

    w_a, w_mg, w_g = _w_prep(w_in.T)
    ha, hg, xb = _in_proj(x.reshape(n, D_MODEL), w_a, w_g)
    ha3 = ha.reshape(b, t, ha.shape[1])

    tab1d = rel_bias[_rel_bucket(jnp.arange(t))].T * LOG2E
    n_far = min(t // L, REL_MAX_DIST // L + 2)
    bias_a = _bias_tiles(tab1d[:A_HEADS], t, -1, n_far + 1, t, 1)
    bias_b = tab1d[A_HEADS:]
    bias_sel = _bias_tiles(bias_b, t, -1, n_far + 1, t, B_HPG)
    n_win = min(t // L, WINDOW // L + 1)
    bias_win = _bias_tiles(bias_b, t, -1, n_win + 1, WINDOW, B_HPG)

    def values_t(v):
        heads, dv = v.shape[2:]
        return v.reshape(b, t // L, L, heads, dv).transpose(0, 3, 1, 4, 2)

    o_a = _diff_attention(ha3, values_t(ha3[:, :, 3072:4096].reshape(b, t, A_HEADS, A_DV)), bias_a, diff_lambda,
                          jnp.broadcast_to(diff_subln[:, None], (A_DV, LANES)), b, t)

    glog_t = hg[:, :48].reshape(b, t, 3, B_GROUPS, B_HPG).transpose(2, 0, 3, 4, 1)
    nr = t // CMP_STRIDE
    w1 = cmp_w1.reshape(2, CMP_LEN, B_DH, CMP_HIDDEN)
    z1 = jnp.zeros_like(w1[0])
    w1_bd = jnp.concatenate([jnp.concatenate([w1[0], z1], axis=2), jnp.concatenate([z1, w1[1]], axis=2)],
                            axis=1).astype(BF16)
    z2 = jnp.zeros_like(cmp_w2[0])
    w2_bd = jnp.concatenate([jnp.concatenate([cmp_w2[0], z2], axis=1), jnp.concatenate([z2, cmp_w2[1]], axis=1)],
                            axis=0).astype(BF16)
    pe_bd = jnp.broadcast_to(jnp.concatenate([cmp_pe[0], cmp_pe[1]], axis=1)[:, None, :],
                             (CMP_LEN, 8, 2 * B_DH)).astype(BF16)
    kvc = _compress(ha3[:, :, 5120:5632].astype(F32), w1_bd, w2_bd, pe_bd)
    n_c = (t - CMP_LEN) // CMP_STRIDE + 1
    n_sel = t // SEL_BLOCK
    cidx = np.arange(nr)[:, None] * CMP_STRIDE + np.arange(CMP_LEN)[None, :]
    overlap = (cidx[:, :, None] // SEL_BLOCK == np.arange(n_sel)[None, None, :]).astype(np.float32).mean(axis=1)
    overlap[n_c:] = 0.0
    overlap_t = np.zeros((LANES, nr), np.float32)
    overlap_t[:n_sel] = overlap.T
    v_sel = ha3[:, :, 5632:6144].reshape(b, t, B_GROUPS, 2, B_DH)[:, :, :, 1]
    v_win = ha3[:, :, 6144:6656].reshape(b, t, B_GROUPS, 2, B_DH)[:, :, :, 1]
    assert n_sel <= B_DH
    hot = np.zeros((t, LANES), np.float32)
    hot[np.arange(t), B_DH + np.arange(t) // SEL_BLOCK] = 1.0
    o_b = _nsa_attention(ha3, kvc, jnp.asarray(overlap_t, BF16), glog_t, values_t(v_sel), bias_sel,
                         jnp.asarray(hot, BF16), values_t(v_win), bias_win, b, t, n_c, min(SEL_TOPN, n_sel))

    n_mem = mem.shape[1]
    memkv = _matmul(mem.reshape(b * n_mem, D_MODEL).astype(BF16), w_mem_kv.astype(BF16), BF16,
                    b * n_mem, 1024, "mem_kv_proj")
    o_m = _memory_attention(ha3, memkv, b, t, n_mem)

    flat = lambda o: o.reshape(n, MIX_W)
    return _merge(xb, w_mg, flat(o_a), flat(o_b), flat(o_m), w_branch.astype(BF16))


def _peer(x1b, q, peer_keys, peer_u, peer_v):
    n = x1b.shape[0]
    zk = jnp.zeros((PEER_HEADS, PEER_NKEYS, PEER_DKEY // 2), F32)
    keys_bd = jnp.concatenate([jnp.concatenate([peer_keys[:, 0], zk], axis=2),
                               jnp.concatenate([zk, peer_keys[:, 1]], axis=2)], axis=1).astype(BF16)
    ei, ej, gate = _peer_route(q, keys_bd)
    slots = lambda a: a.reshape(PEER_HEADS * PEER_TOPK, n).T.reshape(n, 1, PEER_HEADS * PEER_TOPK)
    gm = _gate_matrix(slots(ei), slots(ej), slots(gate))
    return _peer_experts(x1b, peer_u, gm, peer_v)


def kernel(x, mem, w_in, diff_lambda, diff_subln, cmp_pe, cmp_w1, cmp_w2, w_mem_kv, w_branch, w_out, ln1_g, ln1_b,
           peer_wq, peer_keys, peer_u, peer_v, ln2_g, ln2_b, rel_bias):
    b, t, _ = x.shape
    n = b * t
    for l in range(DEPTH):
        mixed = _token_mixer(x, mem, w_in[l], diff_lambda[l], diff_subln[l], cmp_pe[l], cmp_w1[l], cmp_w2[l],
                             w_mem_kv[l], w_branch[l], rel_bias)
        x1, x1b, q = _out_proj_ln(mixed, w_out[l].astype(BF16), x.reshape(n, D_MODEL), ln1_g[l].reshape(1, D_MODEL),
                                  ln1_b[l].reshape(1, D_MODEL), peer_wq[l].astype(BF16))
        y = _peer(x1b, q, peer_keys[l], peer_u[l], peer_v[l])
        x = _residual_ln(x1, y, ln2_g[l].reshape(1, D_MODEL), ln2_b[l].reshape(1, D_MODEL)).reshape(b, t, D_MODEL)
    return x
```

```python
import functools
import math

import numpy as np
import jax
import jax.numpy as jnp
from jax import lax
from jax.experimental import pallas as pl
from jax.experimental.pallas import tpu as pltpu

F32 = jnp.float32
BF16 = jnp.bfloat16

D_MODEL = 2048
A_HEADS, A_DQK, A_DV = 8, 64, 128
B_HEADS, B_GROUPS, B_HPG, B_DH = 16, 4, 4, 64
CMP_LEN, CMP_STRIDE, CMP_HIDDEN = 32, 16, 256
SEL_BLOCK, SEL_TOPN, WINDOW = 64, 16, 512
M_HEADS, M_DH = 4, 256
REL_BUCKETS, REL_MAX_DIST = 32, 1024
PEER_HEADS, PEER_NKEYS, PEER_DKEY, PEER_TOPK = 8, 128, 256, 16
MIX_W = 1024
LN_EPS = 1e-5
FORCE = 1e9
DEPTH = 1
ALPHA = (2 * DEPTH) ** 0.25
LAMBDA_INIT = 0.8 - 0.6 * math.exp(-0.3 * 0)
LOG2E = math.log2(math.e)

NEG = -1e30
ATT_TILE = 256
LANES = 128
ONES_ROWS = 16
N_MERGE = 3
VMEM_LIMIT = 56 * 1024 * 1024


def _cparams(sem):
    return pltpu.CompilerParams(dimension_semantics=sem, vmem_limit_bytes=VMEM_LIMIT)


def _dot_nt(a, b):
    return lax.dot_general(a, b, (((1,), (1,)), ((), ())), preferred_element_type=F32)


def _mm_body(x_ref, w_ref, o_ref):
    o_ref[...] = jnp.dot(x_ref[...], w_ref[...], preferred_element_type=F32).astype(o_ref.dtype)


def _matmul(x, w, out_dtype, tm, tn, name):
    m, k = x.shape
    n = w.shape[1]
    return pl.pallas_call(
        _mm_body,
        grid=(m // tm, n // tn),
        in_specs=[pl.BlockSpec((tm, k), lambda i, j: (i, 0)), pl.BlockSpec((k, tn), lambda i, j: (0, j))],
        out_specs=pl.BlockSpec((tm, tn), lambda i, j: (i, j)),
        out_shape=jax.ShapeDtypeStruct((m, n), out_dtype),
        compiler_params=_cparams(("parallel", "arbitrary")),
        name=name,
    )(x, w)


def _in_proj_body(x_ref, w_ref, wg_ref, o_ref, og_ref, xb_ref):
    xb = x_ref[...].astype(BF16)
    o_ref[...] = _dot_nt(xb, w_ref[...]).astype(o_ref.dtype)

    @pl.when(pl.program_id(1) == 0)
    def _():
        xb_ref[...] = xb
        og_ref[...] = _dot_nt(xb, wg_ref[...])


def _in_proj(x, w_a, w_g):
    n, k = x.shape
    tm, tn = 512, 1664
    return pl.pallas_call(
        _in_proj_body,
        grid=(n // tm, W_A_COLS // tn),
        in_specs=[pl.BlockSpec((tm, k), lambda i, j: (i, 0)),
                  pl.BlockSpec((tn, k), lambda i, j: (j, 0)),
                  pl.BlockSpec((W_G_COLS, k), lambda i, j: (0, 0))],
        out_specs=[pl.BlockSpec((tm, tn), lambda i, j: (i, j)),
                   pl.BlockSpec((tm, W_G_COLS), lambda i, j: (i, 0)),
                   pl.BlockSpec((tm, k), lambda i, j: (i, 0))],
        out_shape=[jax.ShapeDtypeStruct((n, W_A_COLS), BF16), jax.ShapeDtypeStruct((n, W_G_COLS), F32),
                   jax.ShapeDtypeStruct((n, k), BF16)],
        compiler_params=_cparams(("parallel", "arbitrary")),
        name="in_proj",
    )(x, w_a, w_g)


W_A_COLS = 6656
W_G_COLS = 128


def _w_prep_body(w_ref, wa_ref, wm_ref, wg_ref):
    def put(dst_ref, d0, s0, n, c=1.0):
        val = w_ref[s0:s0 + n, :]
        dst_ref[d0:d0 + n, :] = (val * c if c != 1.0 else val).astype(dst_ref.dtype)

    put(wa_ref, 0, 5680, 1024, M_DH ** -0.5 * LOG2E)
    put(wa_ref, 1024, 0, 1024, A_DQK ** -0.5 * LOG2E)
    put(wa_ref, 2048, 1024, 2048)
    put(wa_ref, 4096, 3072, 1024, B_DH ** -0.5 * LOG2E)
    for br in range(3):
        for g in range(B_GROUPS):
            src = 4096 + br * 2 * B_GROUPS * B_DH + g * B_DH
            dst = 5120 + (br * B_GROUPS + g) * 2 * B_DH
            put(wa_ref, dst, src, B_DH)
            put(wa_ref, dst + B_DH, src + B_GROUPS * B_DH, B_DH)
    put(wm_ref, 0, 6704, N_MERGE * D_MODEL)
    put(wg_ref, 0, 5632, 48)
    wg_ref[48:, :] = jnp.zeros((W_G_COLS - 48, wg_ref.shape[1]), wg_ref.dtype)


def _w_prep(w_in_t):
    c, k = w_in_t.shape
    tk = 256
    col = lambda i: (0, i)
    return pl.pallas_call(
        _w_prep_body,
        grid=(k // tk,),
        in_specs=[pl.BlockSpec((c, tk), col)],
        out_specs=[pl.BlockSpec((W_A_COLS, tk), col), pl.BlockSpec((N_MERGE * D_MODEL, tk), col),
                   pl.BlockSpec((W_G_COLS, tk), col)],
        out_shape=[jax.ShapeDtypeStruct((W_A_COLS, k), BF16), jax.ShapeDtypeStruct((N_MERGE * D_MODEL, k), BF16),
                   jax.ShapeDtypeStruct((W_G_COLS, k), BF16)],
        compiler_params=_cparams(("parallel",)),
        name="w_in_regroup",
    )(w_in_t)


def _rel_bucket(dist):
    n = jnp.maximum(dist, 0)
    max_exact = REL_BUCKETS // 2
    nf = jnp.maximum(n, 1).astype(jnp.float32)
    large = max_exact + (jnp.log(nf / max_exact) / math.log(REL_MAX_DIST / max_exact)
                         * (REL_BUCKETS - max_exact)).astype(jnp.int32)
    large = jnp.minimum(large, REL_BUCKETS - 1)
    return jnp.where(n < max_exact, n, large)


def _bias_tiles(tab1d, t, first, n_tiles, max_dist, hpr):
    L = ATT_TILE
    m = np.arange(2 * L)
    off = np.where(m <= L, m, m - 2 * L)
    d = (first + np.arange(n_tiles))[:, None] * L + off[None, :]
    ok = (d >= 0) & (d < max_dist)
    h = tab1d.shape[0]
    rp = jnp.where(ok[:, None], tab1d.T[np.clip(d, 0, t - 1)].transpose(0, 2, 1), NEG)
    return pl.pallas_call(
        functools.partial(_toeplitz_body, hpr=hpr),
        grid=(n_tiles,),
        in_specs=[pl.BlockSpec((None, h, 1, 2 * L), lambda c: (c, 0, 0, 0))],
        out_specs=pl.BlockSpec((None, h // hpr, L, hpr * L), lambda c: (c, 0, 0, 0)),
        out_shape=jax.ShapeDtypeStruct((n_tiles, h // hpr, L, hpr * L), F32),
        compiler_params=_cparams(("parallel",)),
        name="bias_tiles",
    )(rp.reshape(n_tiles, h, 1, 2 * L))


def _toeplitz_body(rp_ref, o_ref, *, hpr):
    L = o_ref.shape[1]
    for hd in range(rp_ref.shape[0]):
        rows = jnp.broadcast_to(rp_ref[hd], (L, 2 * L))
        tile = pltpu.roll(rows, 0, 1, stride=1, stride_axis=0)[:, :L]
        o_ref[hd // hpr, :, (hd % hpr) * L:(hd % hpr + 1) * L] = tile


def _flash_loop(lo, hi, scores, values, m_ref, acc_ref, s_ref, mx_ref):
    def ahead(kt, slot):
        s = scores(kt)
        s_ref[slot] = s
        mx_ref[slot] = jnp.broadcast_to(jnp.max(s, axis=0, keepdims=True), mx_ref.shape[1:])

    def finish(kt, slot):
        m_prev = m_ref[...]
        m_new = jnp.maximum(m_prev, mx_ref[slot])
        alpha = jnp.exp2(m_prev - m_new)
        p = jnp.exp2(s_ref[slot] - m_new[0:1])
        acc_ref[...] = alpha[0:1] * acc_ref[...] + jnp.dot(values(kt), p.astype(BF16), preferred_element_type=F32)
        m_ref[...] = m_new

    if lo is None:
        tiles = hi
        ahead(tiles[0], 0)
        for j, kt in enumerate(tiles):
            if j + 1 < len(tiles):
                ahead(tiles[j + 1], (j + 1) % 2)
            finish(kt, j % 2)
        return

    def body(i, carry):
        kt = lo + 2 * i
        ahead(kt + 1, 1)
        finish(kt, 0)
        ahead(kt + 2, 0)
        finish(kt + 1, 1)
        return carry

    ahead(lo, 0)
    lax.fori_loop(0, (hi - lo + 1) // 2, body, 0)


def _diff_body(q_ref, k_ref, vt_ref, bias_ref, lam_ref, g_ref, o_ref, m_sc, acc_sc, s_sc, mx_sc, *, n_bt):
    L = ATT_TILE
    tq = 2 * L
    last = vt_ref.shape[0] - 1
    qi = pl.program_id(2)
    m_sc[...] = jnp.full(m_sc.shape, NEG, F32)
    acc_sc[...] = jnp.zeros(acc_sc.shape, F32)
    qt = q_ref[...].astype(F32).T.astype(BF16)
    zero = jnp.zeros((A_DQK, tq), BF16)
    q_cat = jnp.concatenate([jnp.concatenate([qt[:A_DQK], zero], axis=0),
                             jnp.concatenate([zero, qt[A_DQK:]], axis=0)], axis=1)

    def scores(kt):
        off = pl.multiple_of(jnp.minimum(kt, last) * L, L)
        k = k_ref[pl.ds(off, L), :]
        d0 = 2 * qi - kt
        bias = jnp.concatenate([bias_ref[jnp.clip(d0 + 1, 0, n_bt - 1)],
                                bias_ref[jnp.clip(d0 + 2, 0, n_bt - 1)]], axis=1)
        s = jnp.dot(k, q_cat, preferred_element_type=F32)
        return jnp.concatenate([s[:, :tq] + bias, s[:, tq:] + bias], axis=1)

    ones = jnp.ones((ONES_ROWS, L), BF16)

    def values(kt):
        return jnp.concatenate([vt_ref[jnp.minimum(kt, last)], ones], axis=0)

    def unrolled(n_tiles):
        def run():
            _flash_loop(None, list(range(n_tiles)), scores, values, m_sc, acc_sc, s_sc, mx_sc)
            return 0
        return run

    lax.switch(qi, [unrolled(2 * i + 2) for i in range((last + 1) * L // tq)])

    lp = lam_ref[...]
    lam = (jnp.exp(jnp.sum(lp[0:1] * lp[1:2], axis=1, keepdims=True))
           - jnp.exp(jnp.sum(lp[2:3] * lp[3:4], axis=1, keepdims=True)) + LAMBDA_INIT)
    acc = acc_sc[...]
    o0 = acc[:A_DV, :tq] / jnp.maximum(acc[A_DV:A_DV + 1, :tq], 1e-30)
    o1 = acc[:A_DV, tq:] / jnp.maximum(acc[A_DV:A_DV + 1, tq:], 1e-30)
    o = o0 - lam * o1
    g = jnp.concatenate([g_ref[...]] * (tq // LANES), axis=1)
    o = o * lax.rsqrt(jnp.mean(o * o, axis=0, keepdims=True) + LN_EPS) * g
    o_ref[...] = (o * (1.0 - LAMBDA_INIT)).T.astype(o_ref.dtype)


def _diff_attention(ha, v_t, bias, lam_params, subln, b, t):
    L = ATT_TILE
    tq = 2 * L
    n_bt = bias.shape[0]
    return pl.pallas_call(
        functools.partial(_diff_body, n_bt=n_bt),
        grid=(b, A_HEADS, t // tq),
        in_specs=[
            pl.BlockSpec((None, tq, 128), lambda bi, h, qi: (bi, qi, 8 + h)),
            pl.BlockSpec((None, t, 128), lambda bi, h, qi: (bi, 0, 16 + h)),
            pl.BlockSpec((None, None, t // L, A_DV, L), lambda bi, h, qi: (bi, h, 0, 0, 0)),
            pl.BlockSpec((n_bt, None, L, L), lambda bi, h, qi: (0, h, 0, 0)),
            pl.BlockSpec((4, A_DQK), lambda bi, h, qi: (0, 0)),
            pl.BlockSpec((A_DV, LANES), lambda bi, h, qi: (0, 0)),
        ],
        out_specs=pl.BlockSpec((None, tq, 128), lambda bi, h, qi: (bi, qi, h)),
        out_shape=jax.ShapeDtypeStruct((b, t, MIX_W), BF16),
        scratch_shapes=[pltpu.VMEM((8, 2 * tq), F32), pltpu.VMEM((A_DV + ONES_ROWS, 2 * tq), F32),
                        pltpu.VMEM((2, L, 2 * tq), F32), pltpu.VMEM((2, 8, 2 * tq), F32)],
        compiler_params=_cparams(("parallel", "parallel", "arbitrary")),
        name="diff_attention",
    )(ha, ha, v_t, bias, lam_params, subln)


def _compress_body(kv_ref, w1_ref, w2_ref, pe_ref, o_ref):
    nr = o_ref.shape[0]
    first = jnp.zeros((nr, 2 * CMP_HIDDEN), F32)
    second = jnp.zeros((nr, 2 * CMP_HIDDEN), F32)
    for l in range(CMP_STRIDE):
        rows = kv_ref[pl.ds(l, nr, stride=CMP_STRIDE), :].astype(BF16)
        first = first + jnp.dot(rows, w1_ref[l], preferred_element_type=F32)
        second = second + jnp.dot(rows, w1_ref[CMP_STRIDE + l], preferred_element_type=F32)
    pw = jnp.zeros((8, 2 * CMP_HIDDEN), F32)
    for l in range(CMP_LEN):
        pw = pw + jnp.dot(pe_ref[l], w1_ref[l], preferred_element_type=F32)
    second = jnp.concatenate([second[1:], second[:1]], axis=0)
    hdn = jax.nn.gelu(first + second + pw[0:1])
    o_ref[...] = jnp.dot(hdn.astype(BF16), w2_ref[...], preferred_element_type=F32)


def _compress(kv, w1_bd, w2_bd, pe_bd):
    b, t, _ = kv.shape
    nr = t // CMP_STRIDE
    return pl.pallas_call(
        _compress_body,
        grid=(b, B_GROUPS),
        in_specs=[
            pl.BlockSpec((None, t, LANES), lambda bi, gi: (bi, 0, gi)),
            pl.BlockSpec(w1_bd.shape, lambda bi, gi: (0, 0, 0)),
            pl.BlockSpec(w2_bd.shape, lambda bi, gi: (0, 0)),
            pl.BlockSpec(pe_bd.shape, lambda bi, gi: (0, 0, 0)),
        ],
        out_specs=pl.BlockSpec((None, None, nr, 2 * B_DH), lambda bi, gi: (bi, gi, 0, 0)),
        out_shape=jax.ShapeDtypeStruct((b, B_GROUPS, nr, 2 * B_DH), F32),
        compiler_params=_cparams(("parallel", "parallel")),
        name="nsa_compress",
    )(kv, w1_bd, w2_bd, pe_bd)


def _topk_mask_axis0(v, k):
    r, n = v.shape
    iota = lax.broadcasted_iota(jnp.int32, (r, n), 0).astype(F32)
    sel = jnp.zeros((r, n), F32)
    for _ in range(k):
        mx = jnp.max(v, axis=0, keepdims=True)
        idx = jnp.min(jnp.where(v == mx, iota, float(r)), axis=0, keepdims=True)
        hit = iota == idx
        v = jnp.where(hit, -jnp.inf, v)
        sel = jnp.where(hit, 1.0, sel)
    return sel


def _compressed_branch(qt, kvc, ov_t, qi, n_c, n_top, n_sel):
    L = ATT_TILE
    ncp = kvc.shape[0]
    qh = jnp.concatenate([qt[h * B_DH:(h + 1) * B_DH] for h in range(B_HPG)], axis=1)
    s = jnp.dot(kvc[:, :B_DH].astype(BF16), qh, preferred_element_type=F32)
    tcol = qi * L + lax.broadcasted_iota(jnp.int32, (1, L), 1)
    crow = lax.broadcasted_iota(jnp.int32, (ncp, 1), 0)
    seen = jnp.where(crow < n_c, crow * CMP_STRIDE + (CMP_LEN - 1), jnp.int32(2 ** 30)) <= tcol
    valid = jnp.concatenate([seen] * B_HPG, axis=1)
    s = jnp.where(valid, s, NEG)
    mx = jnp.max(s, axis=0, keepdims=True)
    e = jnp.where(valid, jnp.exp2(s - mx), 0.0)
    p = e / jnp.maximum(jnp.sum(e, axis=0, keepdims=True), 1e-30)
    o = jnp.dot(kvc.T[B_DH:].astype(BF16), p.astype(BF16), preferred_element_type=F32)

    psum = p[:, :L] + p[:, L:2 * L] + p[:, 2 * L:3 * L] + p[:, 3 * L:]
    imp = jnp.zeros((ov_t.shape[0], L), F32)
    rem = psum
    for _ in range(3):
        part = rem.astype(BF16)
        imp = imp + jnp.dot(ov_t, part, preferred_element_type=F32)
        rem = rem - part.astype(F32)
    imp = imp[:n_sel]
    blk = lax.broadcasted_iota(jnp.int32, (n_sel, 1), 0)
    cur = jnp.right_shift(tcol, int(math.log2(SEL_BLOCK)))
    imp = jnp.where(blk * SEL_BLOCK > tcol, -FORCE, imp)
    imp = jnp.where(blk == 0, FORCE, imp)
    imp = jnp.where(blk == cur, FORCE, imp)
    imp = jnp.where(blk == cur - 1, FORCE, imp)
    return o, jnp.where(_topk_mask_axis0(imp, n_top) > 0.0, 0.0, NEG)


def _nsa_body(q_ref, kvc_ref, ov_ref, gl_ref, kvs_ref, vts_ref, bs_ref, hot_ref, kvw_ref, vtw_ref, bw_ref, o_ref,
              m_sc, acc_sc, s_sc, mx_sc, *, n_c, n_top, n_sel):
    L = ATT_TILE
    qi = pl.program_id(2)
    last = vts_ref.shape[0] - 1
    qt = q_ref[...].astype(F32).T.astype(BF16)
    gates = jax.nn.sigmoid(gl_ref[...])
    ones = jnp.ones((ONES_ROWS, L), BF16)
    k_lanes = lax.broadcasted_iota(jnp.int32, (L, LANES), 1) < B_DH

    def gated(o, branch):
        return jnp.concatenate([o[:, h * L:(h + 1) * L] * gates[branch, h:h + 1] for h in range(B_HPG)], axis=0)

    def attend(qm, kv_ref, vt_ref, bias_ref, use_hot, tiles):
        m_sc[...] = jnp.full(m_sc.shape, NEG, F32)
        acc_sc[...] = jnp.zeros(acc_sc.shape, F32)
        q_aug_t = jnp.concatenate([jnp.concatenate([qt[h * B_DH:(h + 1) * B_DH], qm], axis=0)
                                   for h in range(B_HPG)], axis=1)
        n_bt = bias_ref.shape[0]

        def scores(kt):
            off = pl.multiple_of(jnp.clip(kt, 0, last) * L, L)
            idx = jnp.where(kt < 0, 0, jnp.clip(qi - kt + 1, 0, n_bt - 1))
            k_aug = kv_ref[pl.ds(off, L), :]
            if use_hot:
                k_aug = jnp.where(k_lanes, k_aug, hot_ref[pl.ds(off, L), :])
            return jnp.dot(k_aug, q_aug_t, preferred_element_type=F32) + bias_ref[idx]

        def values(kt):
            return jnp.concatenate([vt_ref[jnp.clip(kt, 0, last)], ones], axis=0)

        if tiles is None:
            def unrolled(n_tiles):
                def run():
                    _flash_loop(None, list(range(n_tiles)), scores, values, m_sc, acc_sc, s_sc, mx_sc)
                    return 0
                return run

            lax.switch(qi, [unrolled(i + 1) for i in range(last + 1)])
        else:
            _flash_loop(None, tiles, scores, values, m_sc, acc_sc, s_sc, mx_sc)
        acc = acc_sc[...]
        return acc[:B_DH] / jnp.maximum(acc[B_DH:B_DH + 1], 1e-30)

    o_cmp, sel = _compressed_branch(qt, kvc_ref[...], ov_ref[...], qi, n_c, n_top, n_sel)
    window_tiles = [qi - j for j in range(bw_ref.shape[0] - 1)]
    o_win = attend(jnp.zeros((B_DH, L), BF16), kvw_ref, vtw_ref, bw_ref, False, window_tiles)
    qm = sel.astype(BF16)
    if n_sel < B_DH:
        qm = jnp.concatenate([qm, jnp.zeros((B_DH - n_sel, L), BF16)], axis=0)
    o_sel = attend(qm, kvs_ref, vts_ref, bs_ref, True, None)
    total = gated(o_cmp, 0) + gated(o_sel, 1) + gated(o_win, 2)
    o_ref[...] = total.T.astype(o_ref.dtype)


def _nsa_attention(ha, kvc, overlap_t, glog_t, vt_sel, bias_sel, hot, vt_win, bias_win, b, t, n_c, n_top):
    L = ATT_TILE
    ncp = kvc.shape[2]
    n_sel = t // SEL_BLOCK
    kv_spec = lambda br: pl.BlockSpec((None, t, LANES), lambda bi, g, qi: (bi, 0, 40 + 4 * br + g))
    vt_spec = pl.BlockSpec((None, None, t // L, B_DH, L), lambda bi, g, qi: (bi, g, 0, 0, 0))
    bias_spec = lambda bias: pl.BlockSpec((bias.shape[0], None, L, B_HPG * L), lambda bi, g, qi: (0, g, 0, 0))
    return pl.pallas_call(
        functools.partial(_nsa_body, n_c=n_c, n_top=n_top, n_sel=n_sel),
        grid=(b, B_GROUPS, t // L),
        in_specs=[
            pl.BlockSpec((None, L, 256), lambda bi, g, qi: (bi, qi, 16 + g)),
            pl.BlockSpec((None, None, ncp, 2 * B_DH), lambda bi, g, qi: (bi, g, 0, 0)),
            pl.BlockSpec(overlap_t.shape, lambda bi, g, qi: (0, 0)),
            pl.BlockSpec((3, None, None, B_HPG, L), lambda bi, g, qi: (0, bi, g, 0, qi)),
            kv_spec(1), vt_spec, bias_spec(bias_sel),
            pl.BlockSpec((t, LANES), lambda bi, g, qi: (0, 0)),
            kv_spec(2), vt_spec, bias_spec(bias_win),
        ],
        out_specs=pl.BlockSpec((None, L, 256), lambda bi, g, qi: (bi, qi, g)),
        out_shape=jax.ShapeDtypeStruct((b, t, MIX_W), BF16),
        scratch_shapes=[pltpu.VMEM((8, B_HPG * L), F32), pltpu.VMEM((B_DH + ONES_ROWS, B_HPG * L), F32),
                        pltpu.VMEM((2, L, B_HPG * L), F32), pltpu.VMEM((2, 8, B_HPG * L), F32)],
        compiler_params=_cparams(("parallel", "parallel", "arbitrary")),
        name="nsa_attention",
    )(ha, kvc, overlap_t, glog_t, ha, vt_sel, bias_sel, hot, ha, vt_win, bias_win)


def _mem_body(q_ref, k_ref, v_ref, o_ref):
    q = q_ref[...]
    outs = []
    for h in range(M_HEADS):
        sl = slice(h * M_DH, (h + 1) * M_DH)
        s = _dot_nt(q[:, sl], k_ref[:, sl])
        e = jnp.exp2(s - jnp.max(s, axis=1, keepdims=True))
        p = e / jnp.sum(e, axis=1, keepdims=True)
        outs.append(jnp.dot(p.astype(BF16), v_ref[:, sl], preferred_element_type=F32))
    o_ref[...] = jnp.concatenate(outs, axis=1).astype(o_ref.dtype)


def _memory_attention(ha, memkv, b, t, n_mem):
    tq = 512
    w = M_HEADS * M_DH
    return pl.pallas_call(
        _mem_body,
        grid=(b, t // tq),
        in_specs=[
            pl.BlockSpec((None, tq, w), lambda bi, qi: (bi, qi, 0)),
            pl.BlockSpec((n_mem, w), lambda bi, qi: (bi, 0)),
            pl.BlockSpec((n_mem, w), lambda bi, qi: (bi, 1)),
        ],
        out_specs=pl.BlockSpec((None, tq, w), lambda bi, qi: (bi, qi, 0)),
        out_shape=jax.ShapeDtypeStruct((b, t, w), BF16),
        compiler_params=_cparams(("parallel", "arbitrary")),
        name="memory_attention",
    )(ha, memkv, memkv)


def _merge_body(x_ref, wg_ref, oa_ref, ob_ref, om_ref, wb_ref, o_ref, acc_sc):
    n = pl.program_id(1)

    @pl.when(n == 0)
    def _():
        acc_sc[...] = jnp.zeros(acc_sc.shape, F32)

    gate = jax.nn.sigmoid(_dot_nt(x_ref[...], wg_ref[...]))
    branch = jnp.where(n == 0, oa_ref[...], jnp.where(n == 1, ob_ref[...], om_ref[...]))
    acc = acc_sc[...] + gate * jnp.dot(branch, wb_ref[...], preferred_element_type=F32)
    acc_sc[...] = acc
    o_ref[...] = acc.astype(o_ref.dtype)


def _merge(xb, w_gate_t, o_a, o_b, o_m, w_branch):
    n = xb.shape[0]
    tm = 512
    row = lambda i, j: (i, 0)
    return pl.pallas_call(
        _merge_body,
        grid=(n // tm, N_MERGE),
        in_specs=[pl.BlockSpec((tm, D_MODEL), row),
                  pl.BlockSpec((D_MODEL, D_MODEL), lambda i, j: (j, 0))]
                 + [pl.BlockSpec((tm, MIX_W), row)] * 3
                 + [pl.BlockSpec((None, MIX_W, D_MODEL), lambda i, j: (j, 0, 0))],
        out_specs=pl.BlockSpec((tm, D_MODEL), row),
        out_shape=jax.ShapeDtypeStruct((n, D_MODEL), BF16),
        scratch_shapes=[pltpu.VMEM((tm, D_MODEL), F32)],
        compiler_params=_cparams(("parallel", "arbitrary")),
        name="branch_merge",
    )(xb, w_gate_t, o_a, o_b, o_m, w_branch)


def _layer_norm(z, g, b):
    mu = jnp.mean(z, axis=1, keepdims=True)
    zc = z - mu
    var = jnp.mean(zc * zc, axis=1, keepdims=True)
    return zc * lax.rsqrt(var + LN_EPS) * g + b


def _out_ln_body(y_ref, w_ref, x_ref, g_ref, b_ref, wq_ref, o_ref, ob_ref, q_ref):
    y = jnp.dot(y_ref[...], w_ref[...], preferred_element_type=F32)
    o = _layer_norm(ALPHA * x_ref[...] + y, g_ref[...], b_ref[...])
    o_ref[...] = o
    ob = o.astype(BF16)
    ob_ref[...] = ob
    q_ref[...] = jnp.dot(ob, wq_ref[...], preferred_element_type=F32).astype(q_ref.dtype)


def _out_proj_ln(mixed, w_out, x, g, b, w_q):
    n = x.shape[0]
    nq = w_q.shape[1]
    tm = 512
    row = lambda i: (i, 0)
    const = lambda i: (0, 0)
    once = dict(pipeline_mode=pl.Buffered(1))
    return pl.pallas_call(
        _out_ln_body,
        grid=(n // tm,),
        in_specs=[pl.BlockSpec((tm, D_MODEL), row), pl.BlockSpec((D_MODEL, D_MODEL), const, **once),
                  pl.BlockSpec((tm, D_MODEL), row), pl.BlockSpec((1, D_MODEL), const),
                  pl.BlockSpec((1, D_MODEL), const), pl.BlockSpec((D_MODEL, nq), const, **once)],
        out_specs=[pl.BlockSpec((tm, D_MODEL), row), pl.BlockSpec((tm, D_MODEL), row), pl.BlockSpec((tm, nq), row)],
        out_shape=[jax.ShapeDtypeStruct((n, D_MODEL), F32), jax.ShapeDtypeStruct((n, D_MODEL), BF16),
                   jax.ShapeDtypeStruct((n, nq), BF16)],
        compiler_params=_cparams(("parallel",)),
        name="out_proj_ln1_query",
    )(mixed, w_out, x, g, b, w_q)


def _res_ln_body(x_ref, y_ref, g_ref, b_ref, o_ref):
    o_ref[...] = _layer_norm(ALPHA * x_ref[...] + y_ref[...], g_ref[...], b_ref[...])


def _residual_ln(x, y, g, b):
    n = x.shape[0]
    tm = 512
    row = lambda i: (i, 0)
    const = lambda i: (0, 0)
    return pl.pallas_call(
        _res_ln_body,
        grid=(n // tm,),
        in_specs=[pl.BlockSpec((tm, D_MODEL), row), pl.BlockSpec((tm, D_MODEL), row),
                  pl.BlockSpec((1, D_MODEL), const), pl.BlockSpec((1, D_MODEL), const)],
        out_specs=pl.BlockSpec((tm, D_MODEL), row),
        out_shape=jax.ShapeDtypeStruct((n, D_MODEL), F32),
        compiler_params=_cparams(("parallel",)),
        name="residual_ln2",
    )(x, y, g, b)


def _topk_axis0(v, k):
    r, n = v.shape
    iota = lax.broadcasted_iota(jnp.int32, (r, n), 0).astype(F32)
    vals, idxs = [], []
    for _ in range(k):
        mx = jnp.max(v, axis=0, keepdims=True)
        idx = jnp.min(jnp.where(v == mx, iota, float(r)), axis=0, keepdims=True)
        v = jnp.where(iota == idx, -jnp.inf, v)
        vals.append(mx)
        idxs.append(idx)
    return jnp.concatenate(vals, axis=0), jnp.concatenate(idxs, axis=0)


def _pick_rows(table, pos, k):
    out = jnp.zeros(pos.shape, F32)
    for a in range(k):
        out = out + jnp.where(pos == float(a), table[a:a + 1], 0.0)
    return out


def _route_body(q_ref, keys_ref, ei_ref, ej_ref, g_ref):
    k = PEER_TOPK
    scores = _dot_nt(keys_ref[...], q_ref[...])
    v0, i0 = _topk_axis0(scores[:PEER_NKEYS], k)
    v1, i1 = _topk_axis0(scores[PEER_NKEYS:], k)
    counts = [k // (a + 1) for a in range(k)]
    starts = np.cumsum([0] + counts[:-1])
    pad = (-sum(counts)) % 8
    comb = jnp.concatenate([v0[a:a + 1] + v1[:counts[a]] for a in range(k)]
                           + [jnp.full((pad, v0.shape[1]), -jnp.inf, F32)], axis=0)
    sf, pos = _topk_axis0(comb, k)
    pa = jnp.zeros(pos.shape, F32)
    pb = pos
    for a in range(1, k):
        later = pos >= float(starts[a])
        pa = pa + jnp.where(later, 1.0, 0.0)
        pb = pb - jnp.where(later, float(counts[a - 1]), 0.0)
    ei_ref[...] = _pick_rows(i0, pa, k)
    ej_ref[...] = _pick_rows(i1, pb, k)
    e = jnp.exp(sf - jnp.max(sf, axis=0, keepdims=True))
    g_ref[...] = e / jnp.sum(e, axis=0, keepdims=True)


def _peer_route(q, keys):
    n = q.shape[0]
    tn = 512
    out = jax.ShapeDtypeStruct((PEER_HEADS, PEER_TOPK, n), F32)
    ospec = pl.BlockSpec((None, PEER_TOPK, tn), lambda i, h: (h, 0, i))
    return pl.pallas_call(
        _route_body,
        grid=(n // tn, PEER_HEADS),
        in_specs=[pl.BlockSpec((tn, PEER_DKEY), lambda i, h: (i, h)),
                  pl.BlockSpec((None, 2 * PEER_NKEYS, PEER_DKEY), lambda i, h: (h, 0, 0))],
        out_specs=[ospec, ospec, ospec],
        out_shape=[out, out, out],
        compiler_params=_cparams(("parallel", "arbitrary")),
        name="peer_route",
    )(q, keys)


def _gate_body(ei_ref, ej_ref, g_ref, o_ref):
    tb = ei_ref.shape[0]
    nk = PEER_NKEYS
    iota = lax.broadcasted_iota(jnp.int32, (tb, nk, ei_ref.shape[2]), 1).astype(F32)
    rows = jnp.where(iota == ei_ref[...], 1.0, 0.0).astype(BF16)
    cols = jnp.where(iota == ej_ref[...], g_ref[...], 0.0).astype(BF16)
    gm = lax.dot_general(rows, cols, (((2,), (2,)), ((0,), (0,))), preferred_element_type=F32)
    o_ref[...] = jnp.swapaxes(gm, 0, 1).astype(o_ref.dtype)


def _gate_matrix(ei, ej, g):
    n, _, slots = ei.shape
    tb = 128
    spec = pl.BlockSpec((tb, 1, slots), lambda i: (i, 0, 0))
    return pl.pallas_call(
        _gate_body,
        grid=(n // tb,),
        in_specs=[spec, spec, spec],
        out_specs=pl.BlockSpec((PEER_NKEYS, tb, PEER_NKEYS), lambda i: (0, i, 0)),
        out_shape=jax.ShapeDtypeStruct((PEER_NKEYS, n, PEER_NKEYS), BF16),
        compiler_params=_cparams(("parallel",)),
        name="peer_gate_matrix",
    )(ei, ej, g)


def _expert_body(x_ref, u_ref, g_ref, v_ref, o_ref):
    @pl.when(pl.program_id(1) == 0)
    def _():
        o_ref[...] = jnp.zeros(o_ref.shape, F32)

    hid = _dot_nt(x_ref[...], u_ref[...].astype(BF16))
    gate = jnp.concatenate([g_ref[i] for i in range(g_ref.shape[0])], axis=1)
    act = (jax.nn.gelu(hid) * gate.astype(F32)).astype(BF16)
    o_ref[...] += jnp.dot(act, v_ref[...].astype(BF16), preferred_element_type=F32)


def _peer_experts(xb, u, gm, v):
    n = xb.shape[0]
    ne = u.shape[0]
    tn, te = 1024, 1024
    once = dict(pipeline_mode=pl.Buffered(1))
    return pl.pallas_call(
        _expert_body,
        grid=(n // tn, ne // te),
        in_specs=[pl.BlockSpec((tn, D_MODEL), lambda i, j: (i, 0), **once),
                  pl.BlockSpec((te, D_MODEL), lambda i, j: (j, 0)),
                  pl.BlockSpec((te // PEER_NKEYS, tn, PEER_NKEYS), lambda i, j: (j, i, 0)),
                  pl.BlockSpec((te, D_MODEL), lambda i, j: (j, 0))],
        out_specs=pl.BlockSpec((tn, D_MODEL), lambda i, j: (i, 0), **once),
        out_shape=jax.ShapeDtypeStruct((n, D_MODEL), F32),
        compiler_params=_cparams(("parallel", "arbitrary")),
        name="peer_experts",
    )(xb, u, gm, v)


def _token_mixer(x, mem, w_in, diff_lambda, diff_subln, cmp_pe, cmp_w1, cmp_w2, w_mem_kv, w_branch, rel_bias):
    b, t, _ = x.shape
    n = b * t
    L = ATT_TILE

    w_a, w_mg, w_g = _w_prep(w_in.T)
    ha, hg, xb = _in_proj(x.reshape(n, D_MODEL), w_a, w_g)
    ha3 = ha.reshape(b, t, ha.shape[1])

    tab1d = rel_bias[_rel_bucket(jnp.arange(t))].T * LOG2E
    n_far = min(t // L, REL_MAX_DIST // L + 2)
    bias_a = _bias_tiles(tab1d[:A_HEADS], t, -1, n_far + 1, t, 1)
    bias_b = tab1d[A_HEADS:]
    bias_sel = _bias_tiles(bias_b, t, -1, n_far + 1, t, B_HPG)
    n_win = min(t // L, WINDOW // L + 1)
    bias_win = _bias_tiles(bias_b, t, -1, n_win + 1, WINDOW, B_HPG)

    def values_t(v):
        heads, dv = v.shape[2:]
        return v.reshape(b, t // L, L, heads, dv).transpose(0, 3, 1, 4, 2)

    o_a = _diff_attention(ha3, values_t(ha3[:, :, 3072:4096].reshape(b, t, A_HEADS, A_DV)), bias_a, diff_lambda,
                          jnp.broadcast_to(diff_subln[:, None], (A_DV, LANES)), b, t)

    glog_t = hg[:, :48].reshape(b, t, 3, B_GROUPS, B_HPG).transpose(2, 0, 3, 4, 1)
    nr = t // CMP_STRIDE
    w1 = cmp_w1.reshape(2, CMP_LEN, B_DH, CMP_HIDDEN)
    z1 = jnp.zeros_like(w1[0])
    w1_bd = jnp.concatenate([jnp.concatenate([w1[0], z1], axis=2), jnp.concatenate([z1, w1[1]], axis=2)],
                            axis=1).astype(BF16)
    z2 = jnp.zeros_like(cmp_w2[0])
    w2_bd = jnp.concatenate([jnp.concatenate([cmp_w2[0], z2], axis=1), jnp.concatenate([z2, cmp_w2[1]], axis=1)],
                            axis=0).astype(BF16)
    pe_bd = jnp.broadcast_to(jnp.concatenate([cmp_pe[0], cmp_pe[1]], axis=1)[:, None, :],
                             (CMP_LEN, 8, 2 * B_DH)).astype(BF16)
    kvc = _compress(ha3[:, :, 5120:5632].astype(F32), w1_bd, w2_bd, pe_bd)
    n_c = (t - CMP_LEN) // CMP_STRIDE + 1
    n_sel = t // SEL_BLOCK
    cidx = np.arange(nr)[:, None] * CMP_STRIDE + np.arange(CMP_LEN)[None, :]
    overlap = (cidx[:, :, None] // SEL_BLOCK == np.arange(n_sel)[None, None, :]).astype(np.float32).mean(axis=1)
    overlap[n_c:] = 0.0
    overlap_t = np.zeros((LANES, nr), np.float32)
    overlap_t[:n_sel] = overlap.T
    v_sel = ha3[:, :, 5632:6144].reshape(b, t, B_GROUPS, 2, B_DH)[:, :, :, 1]
    v_win = ha3[:, :, 6144:6656].reshape(b, t, B_GROUPS, 2, B_DH)[:, :, :, 1]
    assert n_sel <= B_DH
    hot = np.zeros((t, LANES), np.float32)
    hot[np.arange(t), B_DH + np.arange(t) // SEL_BLOCK] = 1.0
    o_b = _nsa_attention(ha3, kvc, jnp.asarray(overlap_t, BF16), glog_t, values_t(v_sel), bias_sel,
                         jnp.asarray(hot, BF16), values_t(v_win), bias_win, b, t, n_c, min(SEL_TOPN, n_sel))

    n_mem = mem.shape[1]
    memkv = _matmul(mem.reshape(b * n_mem, D_MODEL).astype(BF16), w_mem_kv.astype(BF16), BF16,
                    b * n_mem, 1024, "mem_kv_proj")
    o_m = _memory_attention(ha3, memkv, b, t, n_mem)

    flat = lambda o: o.reshape(n, MIX_W)
    return _merge(xb, w_mg, flat(o_a), flat(o_b), flat(o_m), w_branch.astype(BF16))


def _peer(x1b, q, peer_keys, peer_u, peer_v):
    n = x1b.shape[0]
    zk = jnp.zeros((PEER_HEADS, PEER_NKEYS, PEER_DKEY // 2), F32)
    keys_bd = jnp.concatenate([jnp.concatenate([peer_keys[:, 0], zk], axis=2),
                               jnp.concatenate([zk, peer_keys[:, 1]], axis=2)], axis=1).astype(BF16)
    ei, ej, gate = _peer_route(q, keys_bd)
    slots = lambda a: a.reshape(PEER_HEADS * PEER_TOPK, n).T.reshape(n, 1, PEER_HEADS * PEER_TOPK)
    gm = _gate_matrix(slots(ei), slots(ej), slots(gate))
    return _peer_experts(x1b, peer_u, gm, peer_v)


def kernel(x, mem, w_in, diff_lambda, diff_subln, cmp_pe, cmp_w1, cmp_w2, w_mem_kv, w_branch, w_out, ln1_g, ln1_b,
           peer_wq, peer_keys, peer_u, peer_v, ln2_g, ln2_b, rel_bias):
    b, t, _ = x.shape
    n = b * t
    for l in range(DEPTH):
        mixed = _token_mixer(x, mem, w_in[l], diff_lambda[l], diff_subln[l], cmp_pe[l], cmp_w1[l], cmp_w2[l],
                             w_mem_kv[l], w_branch[l], rel_bias)
        x1, x1b, q = _out_proj_ln(mixed, w_out[l].astype(BF16), x.reshape(n, D_MODEL), ln1_g[l].reshape(1, D_MODEL),
                                  ln1_b[l].reshape(1, D_MODEL), peer_wq[l].astype(BF16))
        y = _peer(x1b, q, peer_keys[l], peer_u[l], peer_v[l])
        x = _residual_ln(x1, y, ln2_g[l].reshape(1, D_MODEL), ln2_b[l].reshape(1, D_MODEL)).reshape(b, t, D_MODEL)
    return x
```

```python
import functools
import math

import numpy as np
import jax
import jax.numpy as jnp
from jax import lax
from jax.experimental import pallas as pl
from jax.experimental.pallas import tpu as pltpu

F32 = jnp.float32
BF16 = jnp.bfloat16

D_MODEL = 2048
A_HEADS, A_DQK, A_DV = 8, 64, 128
B_HEADS, B_GROUPS, B_HPG, B_DH = 16, 4, 4, 64
CMP_LEN, CMP_STRIDE, CMP_HIDDEN = 32, 16, 256
SEL_BLOCK, SEL_TOPN, WINDOW = 64, 16, 512
M_HEADS, M_DH = 4, 256
REL_BUCKETS, REL_MAX_DIST = 32, 1024
PEER_HEADS, PEER_NKEYS, PEER_DKEY, PEER_TOPK = 8, 128, 256, 16
MIX_W = 1024
LN_EPS = 1e-5
FORCE = 1e9
DEPTH = 1
ALPHA = (2 * DEPTH) ** 0.25
LAMBDA_INIT = 0.8 - 0.6 * math.exp(-0.3 * 0)
LOG2E = math.log2(math.e)

NEG = -1e30
ATT_TILE = 256
LANES = 128
ONES_ROWS = 16
N_MERGE = 3
FLASH_UNROLL = 4
VMEM_LIMIT = 56 * 1024 * 1024


def _cparams(sem):
    return pltpu.CompilerParams(dimension_semantics=sem, vmem_limit_bytes=VMEM_LIMIT)


def _dot_nt(a, b):
    return lax.dot_general(a, b, (((1,), (1,)), ((), ())), preferred_element_type=F32)


def _mm_body(x_ref, w_ref, o_ref):
    o_ref[...] = jnp.dot(x_ref[...], w_ref[...], preferred_element_type=F32).astype(o_ref.dtype)


def _matmul(x, w, out_dtype, tm, tn, name):
    m, k = x.shape
    n = w.shape[1]
    return pl.pallas_call(
        _mm_body,
        grid=(m // tm, n // tn),
        in_specs=[pl.BlockSpec((tm, k), lambda i, j: (i, 0)), pl.BlockSpec((k, tn), lambda i, j: (0, j))],
        out_specs=pl.BlockSpec((tm, tn), lambda i, j: (i, j)),
        out_shape=jax.ShapeDtypeStruct((m, n), out_dtype),
        compiler_params=_cparams(("parallel", "arbitrary")),
        name=name,
    )(x, w)


def _in_proj_body(x_ref, w_ref, wg_ref, o_ref, og_ref, xb_ref):
    xb = x_ref[...].astype(BF16)
    o_ref[...] = _dot_nt(xb, w_ref[...]).astype(o_ref.dtype)

    @pl.when(pl.program_id(1) == 0)
    def _():
        xb_ref[...] = xb
        og_ref[...] = _dot_nt(xb, wg_ref[...])


def _in_proj(x, w_a, w_g):
    n, k = x.shape
    tm, tn = 512, 1664
    return pl.pallas_call(
        _in_proj_body,
        grid=(n // tm, W_A_COLS // tn),
        in_specs=[pl.BlockSpec((tm, k), lambda i, j: (i, 0)),
                  pl.BlockSpec((tn, k), lambda i, j: (j, 0)),
                  pl.BlockSpec((W_G_COLS, k), lambda i, j: (0, 0))],
        out_specs=[pl.BlockSpec((tm, tn), lambda i, j: (i, j)),
                   pl.BlockSpec((tm, W_G_COLS), lambda i, j: (i, 0)),
                   pl.BlockSpec((tm, k), lambda i, j: (i, 0))],
        out_shape=[jax.ShapeDtypeStruct((n, W_A_COLS), BF16), jax.ShapeDtypeStruct((n, W_G_COLS), F32),
                   jax.ShapeDtypeStruct((n, k), BF16)],
        compiler_params=_cparams(("parallel", "arbitrary")),
        name="in_proj",
    )(x, w_a, w_g)


W_A_COLS = 6656
W_G_COLS = 128


def _w_prep_body(w_ref, wa_ref, wm_ref, wg_ref):
    def put(dst_ref, d0, s0, n, c=1.0):
        val = w_ref[s0:s0 + n, :]
        dst_ref[d0:d0 + n, :] = (val * c if c != 1.0 else val).astype(dst_ref.dtype)

    put(wa_ref, 0, 5680, 1024, M_DH ** -0.5 * LOG2E)
    put(wa_ref, 1024, 0, 1024, A_DQK ** -0.5 * LOG2E)
    put(wa_ref, 2048, 1024, 2048)
    put(wa_ref, 4096, 3072, 1024, B_DH ** -0.5 * LOG2E)
    for br in range(3):
        for g in range(B_GROUPS):
            src = 4096 + br * 2 * B_GROUPS * B_DH + g * B_DH
            dst = 5120 + (br * B_GROUPS + g) * 2 * B_DH
            put(wa_ref, dst, src, B_DH)
            put(wa_ref, dst + B_DH, src + B_GROUPS * B_DH, B_DH)
    put(wm_ref, 0, 6704, N_MERGE * D_MODEL)
    put(wg_ref, 0, 5632, 48)
    wg_ref[48:, :] = jnp.zeros((W_G_COLS - 48, wg_ref.shape[1]), wg_ref.dtype)


def _w_prep(w_in_t):
    c, k = w_in_t.shape
    tk = 256
    col = lambda i: (0, i)
    return pl.pallas_call(
        _w_prep_body,
        grid=(k // tk,),
        in_specs=[pl.BlockSpec((c, tk), col)],
        out_specs=[pl.BlockSpec((W_A_COLS, tk), col), pl.BlockSpec((N_MERGE * D_MODEL, tk), col),
                   pl.BlockSpec((W_G_COLS, tk), col)],
        out_shape=[jax.ShapeDtypeStruct((W_A_COLS, k), BF16), jax.ShapeDtypeStruct((N_MERGE * D_MODEL, k), BF16),
                   jax.ShapeDtypeStruct((W_G_COLS, k), BF16)],
        compiler_params=_cparams(("parallel",)),
        name="w_in_regroup",
    )(w_in_t)


def _rel_bucket(dist):
    n = jnp.maximum(dist, 0)
    max_exact = REL_BUCKETS // 2
    nf = jnp.maximum(n, 1).astype(jnp.float32)
    large = max_exact + (jnp.log(nf / max_exact) / math.log(REL_MAX_DIST / max_exact)
                         * (REL_BUCKETS - max_exact)).astype(jnp.int32)
    large = jnp.minimum(large, REL_BUCKETS - 1)
    return jnp.where(n < max_exact, n, large)


def _bias_tiles(tab1d, t, first, n_tiles, max_dist, hpr):
    L = ATT_TILE
    m = np.arange(2 * L)
    off = np.where(m <= L, m, m - 2 * L)
    d = (first + np.arange(n_tiles))[:, None] * L + off[None, :]
    ok = (d >= 0) & (d < max_dist)
    h = tab1d.shape[0]
    rp = jnp.where(ok[:, None], tab1d.T[np.clip(d, 0, t - 1)].transpose(0, 2, 1), NEG)
    return pl.pallas_call(
        functools.partial(_toeplitz_body, hpr=hpr),
        grid=(n_tiles,),
        in_specs=[pl.BlockSpec((None, h, 1, 2 * L), lambda c: (c, 0, 0, 0))],
        out_specs=pl.BlockSpec((None, h // hpr, L, hpr * L), lambda c: (c, 0, 0, 0)),
        out_shape=jax.ShapeDtypeStruct((n_tiles, h // hpr, L, hpr * L), F32),
        compiler_params=_cparams(("parallel",)),
        name="bias_tiles",
    )(rp.reshape(n_tiles, h, 1, 2 * L))


def _toeplitz_body(rp_ref, o_ref, *, hpr):
    L = o_ref.shape[1]
    for hd in range(rp_ref.shape[0]):
        rows = jnp.broadcast_to(rp_ref[hd], (L, 2 * L))
        tile = pltpu.roll(rows, 0, 1, stride=1, stride_axis=0)[:, :L]
        o_ref[hd // hpr, :, (hd % hpr) * L:(hd % hpr + 1) * L] = tile


def _flash_loop(lo, hi, scores, values, m_ref, acc_ref, s_ref, mx_ref):
    def ahead(kt, slot):
        s = scores(kt)
        s_ref[slot] = s
        mx_ref[slot] = jnp.broadcast_to(jnp.max(s, axis=0, keepdims=True), mx_ref.shape[1:])

    def finish(kt, slot):
        m_prev = m_ref[...]
        m_new = jnp.maximum(m_prev, mx_ref[slot])
        alpha = jnp.exp2(m_prev - m_new)
        p = jnp.exp2(s_ref[slot] - m_new[0:1])
        acc_ref[...] = alpha[0:1] * acc_ref[...] + jnp.dot(values(kt), p.astype(BF16), preferred_element_type=F32)
        m_ref[...] = m_new

    if lo is None:
        tiles = hi
        ahead(tiles[0], 0)
        for j, kt in enumerate(tiles):
            if j + 1 < len(tiles):
                ahead(tiles[j + 1], (j + 1) % 2)
            finish(kt, j % 2)
        return

    def body(i, carry):
        kt = lo + FLASH_UNROLL * i
        for j in range(FLASH_UNROLL):
            ahead(kt + j + 1, (j + 1) % 2)
            finish(kt + j, j % 2)
        return carry

    def tail(r):
        def run():
            for j in range(r):
                if j + 1 < r:
                    ahead(base + j + 1, (j + 1) % 2)
                finish(base + j, j % 2)
            return 0
        return run

    ahead(lo, 0)
    n_full = (hi - lo) // FLASH_UNROLL
    lax.fori_loop(0, n_full, body, 0)
    base = lo + n_full * FLASH_UNROLL
    lax.switch(hi - base, [tail(r) for r in range(FLASH_UNROLL)])


def _diff_body(q_ref, k_ref, vt_ref, bias_ref, lam_ref, g_ref, o_ref, m_sc, acc_sc, s_sc, mx_sc, *, n_bt):
    L = ATT_TILE
    tq = 2 * L
    last = vt_ref.shape[0] - 1
    qi = pl.program_id(2)
    m_sc[...] = jnp.full(m_sc.shape, NEG, F32)
    acc_sc[...] = jnp.zeros(acc_sc.shape, F32)
    qt = q_ref[...].astype(F32).T.astype(BF16)
    zero = jnp.zeros((A_DQK, tq), BF16)
    q_cat = jnp.concatenate([jnp.concatenate([qt[:A_DQK], zero], axis=0),
                             jnp.concatenate([zero, qt[A_DQK:]], axis=0)], axis=1)

    def scores(kt):
        off = pl.multiple_of(jnp.minimum(kt, last) * L, L)
        k = k_ref[pl.ds(off, L), :]
        d0 = 2 * qi - kt
        bias = jnp.concatenate([bias_ref[jnp.clip(d0 + 1, 0, n_bt - 1)],
                                bias_ref[jnp.clip(d0 + 2, 0, n_bt - 1)]], axis=1)
        s = jnp.dot(k, q_cat, preferred_element_type=F32)
        return jnp.concatenate([s[:, :tq] + bias, s[:, tq:] + bias], axis=1)

    ones = jnp.ones((ONES_ROWS, L), BF16)

    def values(kt):
        return jnp.concatenate([vt_ref[jnp.minimum(kt, last)], ones], axis=0)

    def unrolled(n_tiles):
        def run():
            _flash_loop(None, list(range(n_tiles)), scores, values, m_sc, acc_sc, s_sc, mx_sc)
            return 0
        return run

    lax.switch(qi, [unrolled(2 * i + 2) for i in range((last + 1) * L // tq)])

    lp = lam_ref[...]
    lam = (jnp.exp(jnp.sum(lp[0:1] * lp[1:2], axis=1, keepdims=True))
           - jnp.exp(jnp.sum(lp[2:3] * lp[3:4], axis=1, keepdims=True)) + LAMBDA_INIT)
    acc = acc_sc[...]
    o0 = acc[:A_DV, :tq] / jnp.maximum(acc[A_DV:A_DV + 1, :tq], 1e-30)
    o1 = acc[:A_DV, tq:] / jnp.maximum(acc[A_DV:A_DV + 1, tq:], 1e-30)
    o = o0 - lam * o1
    g = jnp.concatenate([g_ref[...]] * (tq // LANES), axis=1)
    o = o * lax.rsqrt(jnp.mean(o * o, axis=0, keepdims=True) + LN_EPS) * g
    o_ref[...] = (o * (1.0 - LAMBDA_INIT)).T.astype(o_ref.dtype)


def _diff_attention(ha, v_t, bias, lam_params, subln, b, t):
    L = ATT_TILE
    tq = 2 * L
    n_bt = bias.shape[0]
    return pl.pallas_call(
        functools.partial(_diff_body, n_bt=n_bt),
        grid=(b, A_HEADS, t // tq),
        in_specs=[
            pl.BlockSpec((None, tq, 128), lambda bi, h, qi: (bi, qi, 8 + h)),
            pl.BlockSpec((None, t, 128), lambda bi, h, qi: (bi, 0, 16 + h)),
            pl.BlockSpec((None, None, t // L, A_DV, L), lambda bi, h, qi: (bi, h, 0, 0, 0)),
            pl.BlockSpec((n_bt, None, L, L), lambda bi, h, qi: (0, h, 0, 0)),
            pl.BlockSpec((4, A_DQK), lambda bi, h, qi: (0, 0)),
            pl.BlockSpec((A_DV, LANES), lambda bi, h, qi: (0, 0)),
        ],
        out_specs=pl.BlockSpec((None, tq, 128), lambda bi, h, qi: (bi, qi, h)),
        out_shape=jax.ShapeDtypeStruct((b, t, MIX_W), BF16),
        scratch_shapes=[pltpu.VMEM((8, 2 * tq), F32), pltpu.VMEM((A_DV + ONES_ROWS, 2 * tq), F32),
                        pltpu.VMEM((2, L, 2 * tq), F32), pltpu.VMEM((2, 8, 2 * tq), F32)],
        compiler_params=_cparams(("parallel", "parallel", "arbitrary")),
        name="diff_attention",
    )(ha, ha, v_t, bias, lam_params, subln)


def _compress_body(kv_ref, w1_ref, w2_ref, pe_ref, o_ref):
    nr = o_ref.shape[0]
    first = jnp.zeros((nr, 2 * CMP_HIDDEN), F32)
    second = jnp.zeros((nr, 2 * CMP_HIDDEN), F32)
    for l in range(CMP_STRIDE):
        rows = kv_ref[pl.ds(l, nr, stride=CMP_STRIDE), :].astype(BF16)
        first = first + jnp.dot(rows, w1_ref[l], preferred_element_type=F32)
        second = second + jnp.dot(rows, w1_ref[CMP_STRIDE + l], preferred_element_type=F32)
    pw = jnp.zeros((8, 2 * CMP_HIDDEN), F32)
    for l in range(CMP_LEN):
        pw = pw + jnp.dot(pe_ref[l], w1_ref[l], preferred_element_type=F32)
    second = jnp.concatenate([second[1:], second[:1]], axis=0)
    hdn = jax.nn.gelu(first + second + pw[0:1])
    o_ref[...] = jnp.dot(hdn.astype(BF16), w2_ref[...], preferred_element_type=F32)


def _compress(kv, w1_bd, w2_bd, pe_bd):
    b, t, _ = kv.shape
    nr = t // CMP_STRIDE
    return pl.pallas_call(
        _compress_body,
        grid=(b, B_GROUPS),
        in_specs=[
            pl.BlockSpec((None, t, LANES), lambda bi, gi: (bi, 0, gi)),
            pl.BlockSpec(w1_bd.shape, lambda bi, gi: (0, 0, 0)),
            pl.BlockSpec(w2_bd.shape, lambda bi, gi: (0, 0)),
            pl.BlockSpec(pe_bd.shape, lambda bi, gi: (0, 0, 0)),
        ],
        out_specs=pl.BlockSpec((None, None, nr, 2 * B_DH), lambda bi, gi: (bi, gi, 0, 0)),
        out_shape=jax.ShapeDtypeStruct((b, B_GROUPS, nr, 2 * B_DH), F32),
        compiler_params=_cparams(("parallel", "parallel")),
        name="nsa_compress",
    )(kv, w1_bd, w2_bd, pe_bd)


def _topk_mask_axis0(v, k):
    r, n = v.shape
    iota = lax.broadcasted_iota(jnp.int32, (r, n), 0).astype(F32)
    sel = jnp.zeros((r, n), F32)
    for _ in range(k):
        mx = jnp.max(v, axis=0, keepdims=True)
        idx = jnp.min(jnp.where(v == mx, iota, float(r)), axis=0, keepdims=True)
        hit = iota == idx
        v = jnp.where(hit, -jnp.inf, v)
        sel = jnp.where(hit, 1.0, sel)
    return sel


def _compressed_branch(qt, kvc, ov_t, qi, n_c, n_top, n_sel):
    L = ATT_TILE
    ncp = kvc.shape[0]
    qh = jnp.concatenate([qt[h * B_DH:(h + 1) * B_DH] for h in range(B_HPG)], axis=1)
    s = jnp.dot(kvc[:, :B_DH].astype(BF16), qh, preferred_element_type=F32)
    tcol = qi * L + lax.broadcasted_iota(jnp.int32, (1, L), 1)
    crow = lax.broadcasted_iota(jnp.int32, (ncp, 1), 0)
    seen = jnp.where(crow < n_c, crow * CMP_STRIDE + (CMP_LEN - 1), jnp.int32(2 ** 30)) <= tcol
    valid = jnp.concatenate([seen] * B_HPG, axis=1)
    s = jnp.where(valid, s, NEG)
    mx = jnp.max(s, axis=0, keepdims=True)
    e = jnp.where(valid, jnp.exp2(s - mx), 0.0)
    p = e / jnp.maximum(jnp.sum(e, axis=0, keepdims=True), 1e-30)
    o = jnp.dot(kvc.T[B_DH:].astype(BF16), p.astype(BF16), preferred_element_type=F32)

    psum = p[:, :L] + p[:, L:2 * L] + p[:, 2 * L:3 * L] + p[:, 3 * L:]
    imp = jnp.zeros((ov_t.shape[0], L), F32)
    rem = psum
    for _ in range(3):
        part = rem.astype(BF16)
        imp = imp + jnp.dot(ov_t, part, preferred_element_type=F32)
        rem = rem - part.astype(F32)
    imp = imp[:n_sel]
    blk = lax.broadcasted_iota(jnp.int32, (n_sel, 1), 0)
    cur = jnp.right_shift(tcol, int(math.log2(SEL_BLOCK)))
    imp = jnp.where(blk * SEL_BLOCK > tcol, -FORCE, imp)
    imp = jnp.where(blk == 0, FORCE, imp)
    imp = jnp.where(blk == cur, FORCE, imp)
    imp = jnp.where(blk == cur - 1, FORCE, imp)
    return o, jnp.where(_topk_mask_axis0(imp, n_top) > 0.0, 0.0, NEG)


def _nsa_body(q_ref, kvc_ref, ov_ref, gl_ref, kvs_ref, vts_ref, bs_ref, hot_ref, kvw_ref, vtw_ref, bw_ref, o_ref,
              m_sc, acc_sc, s_sc, mx_sc, *, n_c, n_top, n_sel):
    L = ATT_TILE
    qi = pl.program_id(2)
    last = vts_ref.shape[0] - 1
    qt = q_ref[...].astype(F32).T.astype(BF16)
    gates = jax.nn.sigmoid(gl_ref[...])
    ones = jnp.ones((ONES_ROWS, L), BF16)
    k_lanes = lax.broadcasted_iota(jnp.int32, (L, LANES), 1) < B_DH

    def gated(o, branch):
        return jnp.concatenate([o[:, h * L:(h + 1) * L] * gates[branch, h:h + 1] for h in range(B_HPG)], axis=0)

    def attend(qm, kv_ref, vt_ref, bias_ref, use_hot, tiles):
        m_sc[...] = jnp.full(m_sc.shape, NEG, F32)
        acc_sc[...] = jnp.zeros(acc_sc.shape, F32)
        q_aug_t = jnp.concatenate([jnp.concatenate([qt[h * B_DH:(h + 1) * B_DH], qm], axis=0)
                                   for h in range(B_HPG)], axis=1)
        n_bt = bias_ref.shape[0]

        def scores(kt):
            off = pl.multiple_of(jnp.clip(kt, 0, last) * L, L)
            idx = jnp.where(kt < 0, 0, jnp.clip(qi - kt + 1, 0, n_bt - 1))
            k_aug = kv_ref[pl.ds(off, L), :]
            if use_hot:
                k_aug = jnp.where(k_lanes, k_aug, hot_ref[pl.ds(off, L), :])
            return jnp.dot(k_aug, q_aug_t, preferred_element_type=F32) + bias_ref[idx]

        def values(kt):
            return jnp.concatenate([vt_ref[jnp.clip(kt, 0, last)], ones], axis=0)

        if tiles is None:
            _flash_loop(0, qi + 1, scores, values, m_sc, acc_sc, s_sc, mx_sc)
        else:
            _flash_loop(None, tiles, scores, values, m_sc, acc_sc, s_sc, mx_sc)
        acc = acc_sc[...]
        return acc[:B_DH] / jnp.maximum(acc[B_DH:B_DH + 1], 1e-30)

    o_cmp, sel = _compressed_branch(qt, kvc_ref[...], ov_ref[...], qi, n_c, n_top, n_sel)
    window_tiles = [qi - j for j in range(bw_ref.shape[0] - 1)]
    o_win = attend(jnp.zeros((B_DH, L), BF16), kvw_ref, vtw_ref, bw_ref, False, window_tiles)
    qm = sel.astype(BF16)
    if n_sel < B_DH:
        qm = jnp.concatenate([qm, jnp.zeros((B_DH - n_sel, L), BF16)], axis=0)
    o_sel = attend(qm, kvs_ref, vts_ref, bs_ref, True, None)
    total = gated(o_cmp, 0) + gated(o_sel, 1) + gated(o_win, 2)
    o_ref[...] = total.T.astype(o_ref.dtype)


def _nsa_attention(ha, kvc, overlap_t, glog_t, vt_sel, bias_sel, hot, vt_win, bias_win, b, t, n_c, n_top):
    L = ATT_TILE
    ncp = kvc.shape[2]
    n_sel = t // SEL_BLOCK
    kv_spec = lambda br: pl.BlockSpec((None, t, LANES), lambda bi, g, qi: (bi, 0, 40 + 4 * br + g))
    vt_spec = pl.BlockSpec((None, None, t // L, B_DH, L), lambda bi, g, qi: (bi, g, 0, 0, 0))
    bias_spec = lambda bias: pl.BlockSpec((bias.shape[0], None, L, B_HPG * L), lambda bi, g, qi: (0, g, 0, 0))
    return pl.pallas_call(
        functools.partial(_nsa_body, n_c=n_c, n_top=n_top, n_sel=n_sel),
        grid=(b, B_GROUPS, t // L),
        in_specs=[
            pl.BlockSpec((None, L, 256), lambda bi, g, qi: (bi, qi, 16 + g)),
            pl.BlockSpec((None, None, ncp, 2 * B_DH), lambda bi, g, qi: (bi, g, 0, 0)),
            pl.BlockSpec(overlap_t.shape, lambda bi, g, qi: (0, 0)),
            pl.BlockSpec((3, None, None, B_HPG, L), lambda bi, g, qi: (0, bi, g, 0, qi)),
            kv_spec(1), vt_spec, bias_spec(bias_sel),
            pl.BlockSpec((t, LANES), lambda bi, g, qi: (0, 0)),
            kv_spec(2), vt_spec, bias_spec(bias_win),
        ],
        out_specs=pl.BlockSpec((None, L, 256), lambda bi, g, qi: (bi, qi, g)),
        out_shape=jax.ShapeDtypeStruct((b, t, MIX_W), BF16),
        scratch_shapes=[pltpu.VMEM((8, B_HPG * L), F32), pltpu.VMEM((B_DH + ONES_ROWS, B_HPG * L), F32),
                        pltpu.VMEM((2, L, B_HPG * L), F32), pltpu.VMEM((2, 8, B_HPG * L), F32)],
        compiler_params=_cparams(("parallel", "parallel", "arbitrary")),
        name="nsa_attention",
    )(ha, kvc, overlap_t, glog_t, ha, vt_sel, bias_sel, hot, ha, vt_win, bias_win)


def _mem_body(q_ref, k_ref, v_ref, o_ref):
    q = q_ref[...]
    outs = []
    for h in range(M_HEADS):
        sl = slice(h * M_DH, (h + 1) * M_DH)
        s = _dot_nt(q[:, sl], k_ref[:, sl])
        e = jnp.exp2(s - jnp.max(s, axis=1, keepdims=True))
        p = e / jnp.sum(e, axis=1, keepdims=True)
        outs.append(jnp.dot(p.astype(BF16), v_ref[:, sl], preferred_element_type=F32))
    o_ref[...] = jnp.concatenate(outs, axis=1).astype(o_ref.dtype)


def _memory_attention(ha, memkv, b, t, n_mem):
    tq = 512
    w = M_HEADS * M_DH
    return pl.pallas_call(
        _mem_body,
        grid=(b, t // tq),
        in_specs=[
            pl.BlockSpec((None, tq, w), lambda bi, qi: (bi, qi, 0)),
            pl.BlockSpec((n_mem, w), lambda bi, qi: (bi, 0)),
            pl.BlockSpec((n_mem, w), lambda bi, qi: (bi, 1)),
        ],
        out_specs=pl.BlockSpec((None, tq, w), lambda bi, qi: (bi, qi, 0)),
        out_shape=jax.ShapeDtypeStruct((b, t, w), BF16),
        compiler_params=_cparams(("parallel", "arbitrary")),
        name="memory_attention",
    )(ha, memkv, memkv)


def _merge_body(x_ref, wg_ref, oa_ref, ob_ref, om_ref, wb_ref, o_ref, acc_sc):
    n = pl.program_id(1)

    @pl.when(n == 0)
    def _():
        acc_sc[...] = jnp.zeros(acc_sc.shape, F32)

    gate = jax.nn.sigmoid(_dot_nt(x_ref[...], wg_ref[...]))
    branch = jnp.where(n == 0, oa_ref[...], jnp.where(n == 1, ob_ref[...], om_ref[...]))
    acc = acc_sc[...] + gate * jnp.dot(branch, wb_ref[...], preferred_element_type=F32)
    acc_sc[...] = acc
    o_ref[...] = acc.astype(o_ref.dtype)


def _merge(xb, w_gate_t, o_a, o_b, o_m, w_branch):
    n = xb.shape[0]
    tm = 512
    row = lambda i, j: (i, 0)
    return pl.pallas_call(
        _merge_body,
        grid=(n // tm, N_MERGE),
        in_specs=[pl.BlockSpec((tm, D_MODEL), row),
                  pl.BlockSpec((D_MODEL, D_MODEL), lambda i, j: (j, 0))]
                 + [pl.BlockSpec((tm, MIX_W), row)] * 3
                 + [pl.BlockSpec((None, MIX_W, D_MODEL), lambda i, j: (j, 0, 0))],
        out_specs=pl.BlockSpec((tm, D_MODEL), row),
        out_shape=jax.ShapeDtypeStruct((n, D_MODEL), BF16),
        scratch_shapes=[pltpu.VMEM((tm, D_MODEL), F32)],
        compiler_params=_cparams(("parallel", "arbitrary")),
        name="branch_merge",
    )(xb, w_gate_t, o_a, o_b, o_m, w_branch)


def _layer_norm(z, g, b):
    mu = jnp.mean(z, axis=1, keepdims=True)
    zc = z - mu
    var = jnp.mean(zc * zc, axis=1, keepdims=True)
    return zc * lax.rsqrt(var + LN_EPS) * g + b


def _out_ln_body(y_ref, w_ref, x_ref, g_ref, b_ref, wq_ref, o_ref, ob_ref, q_ref):
    y = jnp.dot(y_ref[...], w_ref[...], preferred_element_type=F32)
    o = _layer_norm(ALPHA * x_ref[...] + y, g_ref[...], b_ref[...])
    o_ref[...] = o
    ob = o.astype(BF16)
    ob_ref[...] = ob
    q_ref[...] = jnp.dot(ob, wq_ref[...], preferred_element_type=F32).astype(q_ref.dtype)


def _out_proj_ln(mixed, w_out, x, g, b, w_q):
    n = x.shape[0]
    nq = w_q.shape[1]
    tm = 512
    row = lambda i: (i, 0)
    const = lambda i: (0, 0)
    once = dict(pipeline_mode=pl.Buffered(1))
    return pl.pallas_call(
        _out_ln_body,
        grid=(n // tm,),
        in_specs=[pl.BlockSpec((tm, D_MODEL), row), pl.BlockSpec((D_MODEL, D_MODEL), const, **once),
                  pl.BlockSpec((tm, D_MODEL), row), pl.BlockSpec((1, D_MODEL), const),
                  pl.BlockSpec((1, D_MODEL), const), pl.BlockSpec((D_MODEL, nq), const, **once)],
        out_specs=[pl.BlockSpec((tm, D_MODEL), row), pl.BlockSpec((tm, D_MODEL), row), pl.BlockSpec((tm, nq), row)],
        out_shape=[jax.ShapeDtypeStruct((n, D_MODEL), F32), jax.ShapeDtypeStruct((n, D_MODEL), BF16),
                   jax.ShapeDtypeStruct((n, nq), BF16)],
        compiler_params=_cparams(("parallel",)),
        name="out_proj_ln1_query",
    )(mixed, w_out, x, g, b, w_q)


def _res_ln_body(x_ref, y_ref, g_ref, b_ref, o_ref):
    o_ref[...] = _layer_norm(ALPHA * x_ref[...] + y_ref[...], g_ref[...], b_ref[...])


def _residual_ln(x, y, g, b):
    n = x.shape[0]
    tm = 512
    row = lambda i: (i, 0)
    const = lambda i: (0, 0)
    return pl.pallas_call(
        _res_ln_body,
        grid=(n // tm,),
        in_specs=[pl.BlockSpec((tm, D_MODEL), row), pl.BlockSpec((tm, D_MODEL), row),
                  pl.BlockSpec((1, D_MODEL), const), pl.BlockSpec((1, D_MODEL), const)],
        out_specs=pl.BlockSpec((tm, D_MODEL), row),
        out_shape=jax.ShapeDtypeStruct((n, D_MODEL), F32),
        compiler_params=_cparams(("parallel",)),
        name="residual_ln2",
    )(x, y, g, b)


def _topk_axis0(v, k):
    r, n = v.shape
    iota = lax.broadcasted_iota(jnp.int32, (r, n), 0).astype(F32)
    vals, idxs = [], []
    for _ in range(k):
        mx = jnp.max(v, axis=0, keepdims=True)
        idx = jnp.min(jnp.where(v == mx, iota, float(r)), axis=0, keepdims=True)
        v = jnp.where(iota == idx, -jnp.inf, v)
        vals.append(mx)
        idxs.append(idx)
    return jnp.concatenate(vals, axis=0), jnp.concatenate(idxs, axis=0)


def _pick_rows(table, pos, k):
    out = jnp.zeros(pos.shape, F32)
    for a in range(k):
        out = out + jnp.where(pos == float(a), table[a:a + 1], 0.0)
    return out


def _route_body(q_ref, keys_ref, ei_ref, ej_ref, g_ref):
    k = PEER_TOPK
    scores = _dot_nt(keys_ref[...], q_ref[...])
    v0, i0 = _topk_axis0(scores[:PEER_NKEYS], k)
    v1, i1 = _topk_axis0(scores[PEER_NKEYS:], k)
    counts = [k // (a + 1) for a in range(k)]
    starts = np.cumsum([0] + counts[:-1])
    pad = (-sum(counts)) % 8
    comb = jnp.concatenate([v0[a:a + 1] + v1[:counts[a]] for a in range(k)]
                           + [jnp.full((pad, v0.shape[1]), -jnp.inf, F32)], axis=0)
    sf, pos = _topk_axis0(comb, k)
    pa = jnp.zeros(pos.shape, F32)
    pb = pos
    for a in range(1, k):
        later = pos >= float(starts[a])
        pa = pa + jnp.where(later, 1.0, 0.0)
        pb = pb - jnp.where(later, float(counts[a - 1]), 0.0)
    ei_ref[...] = _pick_rows(i0, pa, k)
    ej_ref[...] = _pick_rows(i1, pb, k)
    e = jnp.exp(sf - jnp.max(sf, axis=0, keepdims=True))
    g_ref[...] = e / jnp.sum(e, axis=0, keepdims=True)


def _peer_route(q, keys):
    n = q.shape[0]
    tn = 512
    out = jax.ShapeDtypeStruct((PEER_HEADS, PEER_TOPK, n), F32)
    ospec = pl.BlockSpec((None, PEER_TOPK, tn), lambda i, h: (h, 0, i))
    return pl.pallas_call(
        _route_body,
        grid=(n // tn, PEER_HEADS),
        in_specs=[pl.BlockSpec((tn, PEER_DKEY), lambda i, h: (i, h)),
                  pl.BlockSpec((None, 2 * PEER_NKEYS, PEER_DKEY), lambda i, h: (h, 0, 0))],
        out_specs=[ospec, ospec, ospec],
        out_shape=[out, out, out],
        compiler_params=_cparams(("parallel", "arbitrary")),
        name="peer_route",
    )(q, keys)


def _gate_body(ei_ref, ej_ref, g_ref, o_ref):
    tb = ei_ref.shape[0]
    nk = PEER_NKEYS
    iota = lax.broadcasted_iota(jnp.int32, (tb, nk, ei_ref.shape[2]), 1).astype(F32)
    rows = jnp.where(iota == ei_ref[...], 1.0, 0.0).astype(BF16)
    cols = jnp.where(iota == ej_ref[...], g_ref[...], 0.0).astype(BF16)
    gm = lax.dot_general(rows, cols, (((2,), (2,)), ((0,), (0,))), preferred_element_type=F32)
    o_ref[...] = jnp.swapaxes(gm, 0, 1).astype(o_ref.dtype)


def _gate_matrix(ei, ej, g):
    n, _, slots = ei.shape
    tb = 128
    spec = pl.BlockSpec((tb, 1, slots), lambda i: (i, 0, 0))
    return pl.pallas_call(
        _gate_body,
        grid=(n // tb,),
        in_specs=[spec, spec, spec],
        out_specs=pl.BlockSpec((PEER_NKEYS, tb, PEER_NKEYS), lambda i: (0, i, 0)),
        out_shape=jax.ShapeDtypeStruct((PEER_NKEYS, n, PEER_NKEYS), BF16),
        compiler_params=_cparams(("parallel",)),
        name="peer_gate_matrix",
    )(ei, ej, g)


def _expert_body(x_ref, u_ref, g_ref, v_ref, o_ref):
    @pl.when(pl.program_id(1) == 0)
    def _():
        o_ref[...] = jnp.zeros(o_ref.shape, F32)

    hid = _dot_nt(x_ref[...], u_ref[...].astype(BF16))
    gate = jnp.concatenate([g_ref[i] for i in range(g_ref.shape[0])], axis=1)
    act = (jax.nn.gelu(hid) * gate.astype(F32)).astype(BF16)
    o_ref[...] += jnp.dot(act, v_ref[...].astype(BF16), preferred_element_type=F32)


def _peer_experts(xb, u, gm, v):
    n = xb.shape[0]
    ne = u.shape[0]
    tn, te = 1024, 1024
    once = dict(pipeline_mode=pl.Buffered(1))
    return pl.pallas_call(
        _expert_body,
        grid=(n // tn, ne // te),
        in_specs=[pl.BlockSpec((tn, D_MODEL), lambda i, j: (i, 0), **once),
                  pl.BlockSpec((te, D_MODEL), lambda i, j: (j, 0)),
                  pl.BlockSpec((te // PEER_NKEYS, tn, PEER_NKEYS), lambda i, j: (j, i, 0)),
                  pl.BlockSpec((te, D_MODEL), lambda i, j: (j, 0))],
        out_specs=pl.BlockSpec((tn, D_MODEL), lambda i, j: (i, 0), **once),
        out_shape=jax.ShapeDtypeStruct((n, D_MODEL), F32),
        compiler_params=_cparams(("parallel", "arbitrary")),
        name="peer_experts",
    )(xb, u, gm, v)


def _token_mixer(x, mem, w_in, diff_lambda, diff_subln, cmp_pe, cmp_w1, cmp_w2, w_mem_kv, w_branch, rel_bias):
    b, t, _ = x.shape
    n = b * t
    L = ATT_TILE

    w_a, w_mg, w_g = _w_prep(w_in.T)
    ha, hg, xb = _in_proj(x.reshape(n, D_MODEL), w_a, w_g)
    ha3 = ha.reshape(b, t, ha.shape[1])

    tab1d = rel_bias[_rel_bucket(jnp.arange(t))].T * LOG2E
    n_far = min(t // L, REL_MAX_DIST // L + 2)
    bias_a = _bias_tiles(tab1d[:A_HEADS], t, -1, n_far + 1, t, 1)
    bias_b = tab1d[A_HEADS:]
    bias_sel = _bias_tiles(bias_b, t, -1, n_far + 1, t, B_HPG)
    n_win = min(t // L, WINDOW // L + 1)
    bias_win = _bias_tiles(bias_b, t, -1, n_win + 1, WINDOW, B_HPG)

    def values_t(v):
        heads, dv = v.shape[2:]
        return v.reshape(b, t // L, L, heads, dv).transpose(0, 3, 1, 4, 2)

    o_a = _diff_attention(ha3, values_t(ha3[:, :, 3072:4096].reshape(b, t, A_HEADS, A_DV)), bias_a, diff_lambda,
                          jnp.broadcast_to(diff_subln[:, None], (A_DV, LANES)), b, t)

    glog_t = hg[:, :48].reshape(b, t, 3, B_GROUPS, B_HPG).transpose(2, 0, 3, 4, 1)
    nr = t // CMP_STRIDE
    w1 = cmp_w1.reshape(2, CMP_LEN, B_DH, CMP_HIDDEN)
    z1 = jnp.zeros_like(w1[0])
    w1_bd = jnp.concatenate([jnp.concatenate([w1[0], z1], axis=2), jnp.concatenate([z1, w1[1]], axis=2)],
                            axis=1).astype(BF16)
    z2 = jnp.zeros_like(cmp_w2[0])
    w2_bd = jnp.concatenate([jnp.concatenate([cmp_w2[0], z2], axis=1), jnp.concatenate([z2, cmp_w2[1]], axis=1)],
                            axis=0).astype(BF16)
    pe_bd = jnp.broadcast_to(jnp.concatenate([cmp_pe[0], cmp_pe[1]], axis=1)[:, None, :],
                             (CMP_LEN, 8, 2 * B_DH)).astype(BF16)
    kvc = _compress(ha3[:, :, 5120:5632].astype(F32), w1_bd, w2_bd, pe_bd)
    n_c = (t - CMP_LEN) // CMP_STRIDE + 1
    n_sel = t // SEL_BLOCK
    cidx = np.arange(nr)[:, None] * CMP_STRIDE + np.arange(CMP_LEN)[None, :]
    overlap = (cidx[:, :, None] // SEL_BLOCK == np.arange(n_sel)[None, None, :]).astype(np.float32).mean(axis=1)
    overlap[n_c:] = 0.0
    overlap_t = np.zeros((LANES, nr), np.float32)
    overlap_t[:n_sel] = overlap.T
    v_sel = ha3[:, :, 5632:6144].reshape(b, t, B_GROUPS, 2, B_DH)[:, :, :, 1]
    v_win = ha3[:, :, 6144:6656].reshape(b, t, B_GROUPS, 2, B_DH)[:, :, :, 1]
    assert n_sel <= B_DH
    hot = np.zeros((t, LANES), np.float32)
    hot[np.arange(t), B_DH + np.arange(t) // SEL_BLOCK] = 1.0
    o_b = _nsa_attention(ha3, kvc, jnp.asarray(overlap_t, BF16), glog_t, values_t(v_sel), bias_sel,
                         jnp.asarray(hot, BF16), values_t(v_win), bias_win, b, t, n_c, min(SEL_TOPN, n_sel))

    n_mem = mem.shape[1]
    memkv = _matmul(mem.reshape(b * n_mem, D_MODEL).astype(BF16), w_mem_kv.astype(BF16), BF16,
                    b * n_mem, 1024, "mem_kv_proj")
    o_m = _memory_attention(ha3, memkv, b, t, n_mem)

    flat = lambda o: o.reshape(n, MIX_W)
    return _merge(xb, w_mg, flat(o_a), flat(o_b), flat(o_m), w_branch.astype(BF16))


def _peer(x1b, q, peer_keys, peer_u, peer_v):
    n = x1b.shape[0]
    zk = jnp.zeros((PEER_HEADS, PEER_NKEYS, PEER_DKEY // 2), F32)
    keys_bd = jnp.concatenate([jnp.concatenate([peer_keys[:, 0], zk], axis=2),
                               jnp.concatenate([zk, peer_keys[:, 1]], axis=2)], axis=1).astype(BF16)
    ei, ej, gate = _peer_route(q, keys_bd)
    slots = lambda a: a.reshape(PEER_HEADS * PEER_TOPK, n).T.reshape(n, 1, PEER_HEADS * PEER_TOPK)
    gm = _gate_matrix(slots(ei), slots(ej), slots(gate))
    return _peer_experts(x1b, peer_u, gm, peer_v)


def kernel(x, mem, w_in, diff_lambda, diff_subln, cmp_pe, cmp_w1, cmp_w2, w_mem_kv, w_branch, w_out, ln1_g, ln1_b,
           peer_wq, peer_keys, peer_u, peer_v, ln2_g, ln2_b, rel_bias):
    b, t, _ = x.shape
    n = b * t
    for l in range(DEPTH):
        mixed = _token_mixer(x, mem, w_in[l], diff_lambda[l], diff_subln[l], cmp_pe[l], cmp_w1[l], cmp_w2[l],
                             w_mem_kv[l], w_branch[l], rel_bias)
        x1, x1b, q = _out_proj_ln(mixed, w_out[l].astype(BF16), x.reshape(n, D_MODEL), ln1_g[l].reshape(1, D_MODEL),
                                  ln1_b[l].reshape(1, D_MODEL), peer_wq[l].astype(BF16))
        y = _peer(x1b, q, peer_keys[l], peer_u[l], peer_v[l])
        x = _residual_ln(x1, y, ln2_g[l].reshape(1, D_MODEL), ln2_b[l].reshape(1, D_MODEL)).reshape(b, t, D_MODEL)
    return x
```

```python
import functools
import math

import numpy as np
import jax
import jax.numpy as jnp
from jax import lax
from jax.experimental import pallas as pl
from jax.experimental.pallas import tpu as pltpu

F32 = jnp.float32
BF16 = jnp.bfloat16

D_MODEL = 2048
A_HEADS, A_DQK, A_DV = 8, 64, 128
B_HEADS, B_GROUPS, B_HPG, B_DH = 16, 4, 4, 64
CMP_LEN, CMP_STRIDE, CMP_HIDDEN = 32, 16, 256
SEL_BLOCK, SEL_TOPN, WINDOW = 64, 16, 512
M_HEADS, M_DH = 4, 256
REL_BUCKETS, REL_MAX_DIST = 32, 1024
PEER_HEADS, PEER_NKEYS, PEER_DKEY, PEER_TOPK = 8, 128, 256, 16
MIX_W = 1024
LN_EPS = 1e-5
FORCE = 1e9
DEPTH = 1
ALPHA = (2 * DEPTH) ** 0.25
LAMBDA_INIT = 0.8 - 0.6 * math.exp(-0.3 * 0)
LOG2E = math.log2(math.e)

NEG = -1e30
ATT_TILE = 256
LANES = 128
ONES_ROWS = 16
N_MERGE = 3
FLASH_UNROLL = 8
VMEM_LIMIT = 56 * 1024 * 1024


def _cparams(sem):
    return pltpu.CompilerParams(dimension_semantics=sem, vmem_limit_bytes=VMEM_LIMIT)


def _dot_nt(a, b):
    return lax.dot_general(a, b, (((1,), (1,)), ((), ())), preferred_element_type=F32)


def _mm_body(x_ref, w_ref, o_ref):
    o_ref[...] = jnp.dot(x_ref[...], w_ref[...], preferred_element_type=F32).astype(o_ref.dtype)


def _matmul(x, w, out_dtype, tm, tn, name):
    m, k = x.shape
    n = w.shape[1]
    return pl.pallas_call(
        _mm_body,
        grid=(m // tm, n // tn),
        in_specs=[pl.BlockSpec((tm, k), lambda i, j: (i, 0)), pl.BlockSpec((k, tn), lambda i, j: (0, j))],
        out_specs=pl.BlockSpec((tm, tn), lambda i, j: (i, j)),
        out_shape=jax.ShapeDtypeStruct((m, n), out_dtype),
        compiler_params=_cparams(("parallel", "arbitrary")),
        name=name,
    )(x, w)


def _in_proj_body(x_ref, w_ref, wg_ref, o_ref, og_ref, xb_ref):
    xb = x_ref[...].astype(BF16)
    o_ref[...] = _dot_nt(xb, w_ref[...]).astype(o_ref.dtype)

    @pl.when(pl.program_id(1) == 0)
    def _():
        xb_ref[...] = xb
        og_ref[...] = _dot_nt(xb, wg_ref[...])


def _in_proj(x, w_a, w_g):
    n, k = x.shape
    tm, tn = 512, 1664
    return pl.pallas_call(
        _in_proj_body,
        grid=(n // tm, W_A_COLS // tn),
        in_specs=[pl.BlockSpec((tm, k), lambda i, j: (i, 0)),
                  pl.BlockSpec((tn, k), lambda i, j: (j, 0)),
                  pl.BlockSpec((W_G_COLS, k), lambda i, j: (0, 0))],
        out_specs=[pl.BlockSpec((tm, tn), lambda i, j: (i, j)),
                   pl.BlockSpec((tm, W_G_COLS), lambda i, j: (i, 0)),
                   pl.BlockSpec((tm, k), lambda i, j: (i, 0))],
        out_shape=[jax.ShapeDtypeStruct((n, W_A_COLS), BF16), jax.ShapeDtypeStruct((n, W_G_COLS), F32),
                   jax.ShapeDtypeStruct((n, k), BF16)],
        compiler_params=_cparams(("parallel", "arbitrary")),
        name="in_proj",
    )(x, w_a, w_g)


W_A_COLS = 6656
W_G_COLS = 128


def _w_prep_body(w_ref, wa_ref, wm_ref, wg_ref):
    def put(dst_ref, d0, s0, n, c=1.0):
        val = w_ref[s0:s0 + n, :]
        dst_ref[d0:d0 + n, :] = (val * c if c != 1.0 else val).astype(dst_ref.dtype)

    put(wa_ref, 0, 5680, 1024, M_DH ** -0.5 * LOG2E)
    put(wa_ref, 1024, 0, 1024, A_DQK ** -0.5 * LOG2E)
    put(wa_ref, 2048, 1024, 2048)
    put(wa_ref, 4096, 3072, 1024, B_DH ** -0.5 * LOG2E)
    for br in range(3):
        for g in range(B_GROUPS):
            src = 4096 + br * 2 * B_GROUPS * B_DH + g * B_DH
            dst = 5120 + (br * B_GROUPS + g) * 2 * B_DH
            put(wa_ref, dst, src, B_DH)
            put(wa_ref, dst + B_DH, src + B_GROUPS * B_DH, B_DH)
    put(wm_ref, 0, 6704, N_MERGE * D_MODEL)
    put(wg_ref, 0, 5632, 48)
    wg_ref[48:, :] = jnp.zeros((W_G_COLS - 48, wg_ref.shape[1]), wg_ref.dtype)


def _w_prep(w_in_t):
    c, k = w_in_t.shape
    tk = 256
    col = lambda i: (0, i)
    return pl.pallas_call(
        _w_prep_body,
        grid=(k // tk,),
        in_specs=[pl.BlockSpec((c, tk), col)],
        out_specs=[pl.BlockSpec((W_A_COLS, tk), col), pl.BlockSpec((N_MERGE * D_MODEL, tk), col),
                   pl.BlockSpec((W_G_COLS, tk), col)],
        out_shape=[jax.ShapeDtypeStruct((W_A_COLS, k), BF16), jax.ShapeDtypeStruct((N_MERGE * D_MODEL, k), BF16),
                   jax.ShapeDtypeStruct((W_G_COLS, k), BF16)],
        compiler_params=_cparams(("parallel",)),
        name="w_in_regroup",
    )(w_in_t)


def _rel_bucket(dist):
    n = jnp.maximum(dist, 0)
    max_exact = REL_BUCKETS // 2
    nf = jnp.maximum(n, 1).astype(jnp.float32)
    large = max_exact + (jnp.log(nf / max_exact) / math.log(REL_MAX_DIST / max_exact)
                         * (REL_BUCKETS - max_exact)).astype(jnp.int32)
    large = jnp.minimum(large, REL_BUCKETS - 1)
    return jnp.where(n < max_exact, n, large)


def _bias_tiles(tab1d, t, first, n_tiles, max_dist, hpr):
    L = ATT_TILE
    m = np.arange(2 * L)
    off = np.where(m <= L, m, m - 2 * L)
    d = (first + np.arange(n_tiles))[:, None] * L + off[None, :]
    ok = (d >= 0) & (d < max_dist)
    h = tab1d.shape[0]
    rp = jnp.where(ok[:, None], tab1d.T[np.clip(d, 0, t - 1)].transpose(0, 2, 1), NEG)
    return pl.pallas_call(
        functools.partial(_toeplitz_body, hpr=hpr),
        grid=(n_tiles,),
        in_specs=[pl.BlockSpec((None, h, 1, 2 * L), lambda c: (c, 0, 0, 0))],
        out_specs=pl.BlockSpec((None, h // hpr, L, hpr * L), lambda c: (c, 0, 0, 0)),
        out_shape=jax.ShapeDtypeStruct((n_tiles, h // hpr, L, hpr * L), F32),
        compiler_params=_cparams(("parallel",)),
        name="bias_tiles",
    )(rp.reshape(n_tiles, h, 1, 2 * L))


def _toeplitz_body(rp_ref, o_ref, *, hpr):
    L = o_ref.shape[1]
    for hd in range(rp_ref.shape[0]):
        rows = jnp.broadcast_to(rp_ref[hd], (L, 2 * L))
        tile = pltpu.roll(rows, 0, 1, stride=1, stride_axis=0)[:, :L]
        o_ref[hd // hpr, :, (hd % hpr) * L:(hd % hpr + 1) * L] = tile


def _flash_loop(lo, hi, scores, values, m_ref, acc_ref, s_ref, mx_ref):
    def ahead(kt, slot):
        s = scores(kt)
        s_ref[slot] = s
        mx_ref[slot] = jnp.broadcast_to(jnp.max(s, axis=0, keepdims=True), mx_ref.shape[1:])

    def finish(kt, slot):
        m_prev = m_ref[...]
        m_new = jnp.maximum(m_prev, mx_ref[slot])
        alpha = jnp.exp2(m_prev - m_new)
        p = jnp.exp2(s_ref[slot] - m_new[0:1])
        acc_ref[...] = alpha[0:1] * acc_ref[...] + jnp.dot(values(kt), p.astype(BF16), preferred_element_type=F32)
        m_ref[...] = m_new

    if lo is None:
        tiles = hi
        ahead(tiles[0], 0)
        for j, kt in enumerate(tiles):
            if j + 1 < len(tiles):
                ahead(tiles[j + 1], (j + 1) % 2)
            finish(kt, j % 2)
        return

    def body(i, carry):
        kt = lo + FLASH_UNROLL * i
        for j in range(FLASH_UNROLL):
            ahead(kt + j + 1, (j + 1) % 2)
            finish(kt + j, j % 2)
        return carry

    def tail(r):
        def run():
            for j in range(r):
                if j + 1 < r:
                    ahead(base + j + 1, (j + 1) % 2)
                finish(base + j, j % 2)
            return 0
        return run

    ahead(lo, 0)
    n_full = (hi - lo) // FLASH_UNROLL
    lax.fori_loop(0, n_full, body, 0)
    base = lo + n_full * FLASH_UNROLL
    lax.switch(hi - base, [tail(r) for r in range(FLASH_UNROLL)])


def _diff_body(q_ref, k_ref, vt_ref, bias_ref, lam_ref, g_ref, o_ref, m_sc, acc_sc, s_sc, mx_sc, *, n_bt):
    L = ATT_TILE
    tq = 2 * L
    last = vt_ref.shape[0] - 1
    qi = pl.program_id(2)
    m_sc[...] = jnp.full(m_sc.shape, NEG, F32)
    acc_sc[...] = jnp.zeros(acc_sc.shape, F32)
    qt = q_ref[...].astype(F32).T.astype(BF16)
    zero = jnp.zeros((A_DQK, tq), BF16)
    q_cat = jnp.concatenate([jnp.concatenate([qt[:A_DQK], zero], axis=0),
                             jnp.concatenate([zero, qt[A_DQK:]], axis=0)], axis=1)

    def scores(kt):
        off = pl.multiple_of(jnp.minimum(kt, last) * L, L)
        k = k_ref[pl.ds(off, L), :]
        d0 = 2 * qi - kt
        bias = jnp.concatenate([bias_ref[jnp.clip(d0 + 1, 0, n_bt - 1)],
                                bias_ref[jnp.clip(d0 + 2, 0, n_bt - 1)]], axis=1)
        s = jnp.dot(k, q_cat, preferred_element_type=F32)
        return jnp.concatenate([s[:, :tq] + bias, s[:, tq:] + bias], axis=1)

    ones = jnp.ones((ONES_ROWS, L), BF16)

    def values(kt):
        return jnp.concatenate([vt_ref[jnp.minimum(kt, last)], ones], axis=0)

    def unrolled(n_tiles):
        def run():
            _flash_loop(None, list(range(n_tiles)), scores, values, m_sc, acc_sc, s_sc, mx_sc)
            return 0
        return run

    lax.switch(qi, [unrolled(2 * i + 2) for i in range((last + 1) * L // tq)])

    lp = lam_ref[...]
    lam = (jnp.exp(jnp.sum(lp[0:1] * lp[1:2], axis=1, keepdims=True))
           - jnp.exp(jnp.sum(lp[2:3] * lp[3:4], axis=1, keepdims=True)) + LAMBDA_INIT)
    acc = acc_sc[...]
    o0 = acc[:A_DV, :tq] / jnp.maximum(acc[A_DV:A_DV + 1, :tq], 1e-30)
    o1 = acc[:A_DV, tq:] / jnp.maximum(acc[A_DV:A_DV + 1, tq:], 1e-30)
    o = o0 - lam * o1
    g = jnp.concatenate([g_ref[...]] * (tq // LANES), axis=1)
    o = o * lax.rsqrt(jnp.mean(o * o, axis=0, keepdims=True) + LN_EPS) * g
    o_ref[...] = (o * (1.0 - LAMBDA_INIT)).T.astype(o_ref.dtype)


def _diff_attention(ha, v_t, bias, lam_params, subln, b, t):
    L = ATT_TILE
    tq = 2 * L
    n_bt = bias.shape[0]
    return pl.pallas_call(
        functools.partial(_diff_body, n_bt=n_bt),
        grid=(b, A_HEADS, t // tq),
        in_specs=[
            pl.BlockSpec((None, tq, 128), lambda bi, h, qi: (bi, qi, 8 + h)),
            pl.BlockSpec((None, t, 128), lambda bi, h, qi: (bi, 0, 16 + h)),
            pl.BlockSpec((None, None, t // L, A_DV, L), lambda bi, h, qi: (bi, h, 0, 0, 0)),
            pl.BlockSpec((n_bt, None, L, L), lambda bi, h, qi: (0, h, 0, 0)),
            pl.BlockSpec((4, A_DQK), lambda bi, h, qi: (0, 0)),
            pl.BlockSpec((A_DV, LANES), lambda bi, h, qi: (0, 0)),
        ],
        out_specs=pl.BlockSpec((None, tq, 128), lambda bi, h, qi: (bi, qi, h)),
        out_shape=jax.ShapeDtypeStruct((b, t, MIX_W), BF16),
        scratch_shapes=[pltpu.VMEM((8, 2 * tq), F32), pltpu.VMEM((A_DV + ONES_ROWS, 2 * tq), F32),
                        pltpu.VMEM((2, L, 2 * tq), F32), pltpu.VMEM((2, 8, 2 * tq), F32)],
        compiler_params=_cparams(("parallel", "parallel", "arbitrary")),
        name="diff_attention",
    )(ha, ha, v_t, bias, lam_params, subln)


def _compress_body(kv_ref, w1_ref, w2_ref, pe_ref, o_ref):
    nr = o_ref.shape[0]
    first = jnp.zeros((nr, 2 * CMP_HIDDEN), F32)
    second = jnp.zeros((nr, 2 * CMP_HIDDEN), F32)
    for l in range(CMP_STRIDE):
        rows = kv_ref[pl.ds(l, nr, stride=CMP_STRIDE), :].astype(BF16)
        first = first + jnp.dot(rows, w1_ref[l], preferred_element_type=F32)
        second = second + jnp.dot(rows, w1_ref[CMP_STRIDE + l], preferred_element_type=F32)
    pw = jnp.zeros((8, 2 * CMP_HIDDEN), F32)
    for l in range(CMP_LEN):
        pw = pw + jnp.dot(pe_ref[l], w1_ref[l], preferred_element_type=F32)
    second = jnp.concatenate([second[1:], second[:1]], axis=0)
    hdn = jax.nn.gelu(first + second + pw[0:1])
    o_ref[...] = jnp.dot(hdn.astype(BF16), w2_ref[...], preferred_element_type=F32)


def _compress(kv, w1_bd, w2_bd, pe_bd):
    b, t, _ = kv.shape
    nr = t // CMP_STRIDE
    return pl.pallas_call(
        _compress_body,
        grid=(b, B_GROUPS),
        in_specs=[
            pl.BlockSpec((None, t, LANES), lambda bi, gi: (bi, 0, gi)),
            pl.BlockSpec(w1_bd.shape, lambda bi, gi: (0, 0, 0)),
            pl.BlockSpec(w2_bd.shape, lambda bi, gi: (0, 0)),
            pl.BlockSpec(pe_bd.shape, lambda bi, gi: (0, 0, 0)),
        ],
        out_specs=pl.BlockSpec((None, None, nr, 2 * B_DH), lambda bi, gi: (bi, gi, 0, 0)),
        out_shape=jax.ShapeDtypeStruct((b, B_GROUPS, nr, 2 * B_DH), F32),
        compiler_params=_cparams(("parallel", "parallel")),
        name="nsa_compress",
    )(kv, w1_bd, w2_bd, pe_bd)


def _topk_mask_axis0(v, k):
    r, n = v.shape
    iota = lax.broadcasted_iota(jnp.int32, (r, n), 0).astype(F32)
    sel = jnp.zeros((r, n), F32)
    for _ in range(k):
        mx = jnp.max(v, axis=0, keepdims=True)
        idx = jnp.min(jnp.where(v == mx, iota, float(r)), axis=0, keepdims=True)
        hit = iota == idx
        v = jnp.where(hit, -jnp.inf, v)
        sel = jnp.where(hit, 1.0, sel)
    return sel


def _compressed_branch(qt, kvc, ov_t, qi, n_c, n_top, n_sel):
    L = ATT_TILE
    ncp = kvc.shape[0]
    qh = jnp.concatenate([qt[h * B_DH:(h + 1) * B_DH] for h in range(B_HPG)], axis=1)
    s = jnp.dot(kvc[:, :B_DH].astype(BF16), qh, preferred_element_type=F32)
    tcol = qi * L + lax.broadcasted_iota(jnp.int32, (1, L), 1)
    crow = lax.broadcasted_iota(jnp.int32, (ncp, 1), 0)
    seen = jnp.where(crow < n_c, crow * CMP_STRIDE + (CMP_LEN - 1), jnp.int32(2 ** 30)) <= tcol
    valid = jnp.concatenate([seen] * B_HPG, axis=1)
    s = jnp.where(valid, s, NEG)
    mx = jnp.max(s, axis=0, keepdims=True)
    e = jnp.where(valid, jnp.exp2(s - mx), 0.0)
    p = e / jnp.maximum(jnp.sum(e, axis=0, keepdims=True), 1e-30)
    o = jnp.dot(kvc.T[B_DH:].astype(BF16), p.astype(BF16), preferred_element_type=F32)

    psum = p[:, :L] + p[:, L:2 * L] + p[:, 2 * L:3 * L] + p[:, 3 * L:]
    imp = jnp.zeros((ov_t.shape[0], L), F32)
    rem = psum
    for _ in range(3):
        part = rem.astype(BF16)
        imp = imp + jnp.dot(ov_t, part, preferred_element_type=F32)
        rem = rem - part.astype(F32)
    imp = imp[:n_sel]
    blk = lax.broadcasted_iota(jnp.int32, (n_sel, 1), 0)
    cur = jnp.right_shift(tcol, int(math.log2(SEL_BLOCK)))
    imp = jnp.where(blk * SEL_BLOCK > tcol, -FORCE, imp)
    imp = jnp.where(blk == 0, FORCE, imp)
    imp = jnp.where(blk == cur, FORCE, imp)
    imp = jnp.where(blk == cur - 1, FORCE, imp)
    return o, jnp.where(_topk_mask_axis0(imp, n_top) > 0.0, 0.0, NEG)


def _nsa_body(q_ref, kvc_ref, ov_ref, gl_ref, kvs_ref, vts_ref, bs_ref, hot_ref, kvw_ref, vtw_ref, bw_ref, o_ref,
              m_sc, acc_sc, s_sc, mx_sc, *, n_c, n_top, n_sel):
    L = ATT_TILE
    qi = pl.program_id(2)
    last = vts_ref.shape[0] - 1
    qt = q_ref[...].astype(F32).T.astype(BF16)
    gates = jax.nn.sigmoid(gl_ref[...])
    ones = jnp.ones((ONES_ROWS, L), BF16)
    k_lanes = lax.broadcasted_iota(jnp.int32, (L, LANES), 1) < B_DH

    def gated(o, branch):
        return jnp.concatenate([o[:, h * L:(h + 1) * L] * gates[branch, h:h + 1] for h in range(B_HPG)], axis=0)

    def attend(qm, kv_ref, vt_ref, bias_ref, use_hot, tiles):
        m_sc[...] = jnp.full(m_sc.shape, NEG, F32)
        acc_sc[...] = jnp.zeros(acc_sc.shape, F32)
        q_aug_t = jnp.concatenate([jnp.concatenate([qt[h * B_DH:(h + 1) * B_DH], qm], axis=0)
                                   for h in range(B_HPG)], axis=1)
        n_bt = bias_ref.shape[0]

        def scores(kt):
            off = pl.multiple_of(jnp.clip(kt, 0, last) * L, L)
            idx = jnp.where(kt < 0, 0, jnp.clip(qi - kt + 1, 0, n_bt - 1))
            k_aug = kv_ref[pl.ds(off, L), :]
            if use_hot:
                k_aug = jnp.where(k_lanes, k_aug, hot_ref[pl.ds(off, L), :])
            return jnp.dot(k_aug, q_aug_t, preferred_element_type=F32) + bias_ref[idx]

        def values(kt):
            return jnp.concatenate([vt_ref[jnp.clip(kt, 0, last)], ones], axis=0)

        if tiles is None:
            _flash_loop(0, qi + 1, scores, values, m_sc, acc_sc, s_sc, mx_sc)
        else:
            _flash_loop(None, tiles, scores, values, m_sc, acc_sc, s_sc, mx_sc)
        acc = acc_sc[...]
        return acc[:B_DH] / jnp.maximum(acc[B_DH:B_DH + 1], 1e-30)

    o_cmp, sel = _compressed_branch(qt, kvc_ref[...], ov_ref[...], qi, n_c, n_top, n_sel)
    window_tiles = [qi - j for j in range(bw_ref.shape[0] - 1)]
    o_win = attend(jnp.zeros((B_DH, L), BF16), kvw_ref, vtw_ref, bw_ref, False, window_tiles)
    qm = sel.astype(BF16)
    if n_sel < B_DH:
        qm = jnp.concatenate([qm, jnp.zeros((B_DH - n_sel, L), BF16)], axis=0)
    o_sel = attend(qm, kvs_ref, vts_ref, bs_ref, True, None)
    total = gated(o_cmp, 0) + gated(o_sel, 1) + gated(o_win, 2)
    o_ref[...] = total.T.astype(o_ref.dtype)


def _nsa_attention(ha, kvc, overlap_t, glog_t, vt_sel, bias_sel, hot, vt_win, bias_win, b, t, n_c, n_top):
    L = ATT_TILE
    ncp = kvc.shape[2]
    n_sel = t // SEL_BLOCK
    kv_spec = lambda br: pl.BlockSpec((None, t, LANES), lambda bi, g, qi: (bi, 0, 40 + 4 * br + g))
    vt_spec = pl.BlockSpec((None, None, t // L, B_DH, L), lambda bi, g, qi: (bi, g, 0, 0, 0))
    bias_spec = lambda bias: pl.BlockSpec((bias.shape[0], None, L, B_HPG * L), lambda bi, g, qi: (0, g, 0, 0))
    return pl.pallas_call(
        functools.partial(_nsa_body, n_c=n_c, n_top=n_top, n_sel=n_sel),
        grid=(b, B_GROUPS, t // L),
        in_specs=[
            pl.BlockSpec((None, L, 256), lambda bi, g, qi: (bi, qi, 16 + g)),
            pl.BlockSpec((None, None, ncp, 2 * B_DH), lambda bi, g, qi: (bi, g, 0, 0)),
            pl.BlockSpec(overlap_t.shape, lambda bi, g, qi: (0, 0)),
            pl.BlockSpec((3, None, None, B_HPG, L), lambda bi, g, qi: (0, bi, g, 0, qi)),
            kv_spec(1), vt_spec, bias_spec(bias_sel),
            pl.BlockSpec((t, LANES), lambda bi, g, qi: (0, 0)),
            kv_spec(2), vt_spec, bias_spec(bias_win),
        ],
        out_specs=pl.BlockSpec((None, L, 256), lambda bi, g, qi: (bi, qi, g)),
        out_shape=jax.ShapeDtypeStruct((b, t, MIX_W), BF16),
        scratch_shapes=[pltpu.VMEM((8, B_HPG * L), F32), pltpu.VMEM((B_DH + ONES_ROWS, B_HPG * L), F32),
                        pltpu.VMEM((2, L, B_HPG * L), F32), pltpu.VMEM((2, 8, B_HPG * L), F32)],
        compiler_params=_cparams(("parallel", "parallel", "arbitrary")),
        name="nsa_attention",
    )(ha, kvc, overlap_t, glog_t, ha, vt_sel, bias_sel, hot, ha, vt_win, bias_win)


def _mem_body(q_ref, k_ref, v_ref, o_ref):
    q = q_ref[...]
    outs = []
    for h in range(M_HEADS):
        sl = slice(h * M_DH, (h + 1) * M_DH)
        s = _dot_nt(q[:, sl], k_ref[:, sl])
        e = jnp.exp2(s - jnp.max(s, axis=1, keepdims=True))
        p = e / jnp.sum(e, axis=1, keepdims=True)
        outs.append(jnp.dot(p.astype(BF16), v_ref[:, sl], preferred_element_type=F32))
    o_ref[...] = jnp.concatenate(outs, axis=1).astype(o_ref.dtype)


def _memory_attention(ha, memkv, b, t, n_mem):
    tq = 512
    w = M_HEADS * M_DH
    return pl.pallas_call(
        _mem_body,
        grid=(b, t // tq),
        in_specs=[
            pl.BlockSpec((None, tq, w), lambda bi, qi: (bi, qi, 0)),
            pl.BlockSpec((n_mem, w), lambda bi, qi: (bi, 0)),
            pl.BlockSpec((n_mem, w), lambda bi, qi: (bi, 1)),
        ],
        out_specs=pl.BlockSpec((None, tq, w), lambda bi, qi: (bi, qi, 0)),
        out_shape=jax.ShapeDtypeStruct((b, t, w), BF16),
        compiler_params=_cparams(("parallel", "arbitrary")),
        name="memory_attention",
    )(ha, memkv, memkv)


def _merge_body(x_ref, wg_ref, oa_ref, ob_ref, om_ref, wb_ref, o_ref, acc_sc):
    n = pl.program_id(1)

    @pl.when(n == 0)
    def _():
        acc_sc[...] = jnp.zeros(acc_sc.shape, F32)

    gate = jax.nn.sigmoid(_dot_nt(x_ref[...], wg_ref[...]))
    branch = jnp.where(n == 0, oa_ref[...], jnp.where(n == 1, ob_ref[...], om_ref[...]))
    acc = acc_sc[...] + gate * jnp.dot(branch, wb_ref[...], preferred_element_type=F32)
    acc_sc[...] = acc
    o_ref[...] = acc.astype(o_ref.dtype)


def _merge(xb, w_gate_t, o_a, o_b, o_m, w_branch):
    n = xb.shape[0]
    tm = 512
    row = lambda i, j: (i, 0)
    return pl.pallas_call(
        _merge_body,
        grid=(n // tm, N_MERGE),
        in_specs=[pl.BlockSpec((tm, D_MODEL), row),
                  pl.BlockSpec((D_MODEL, D_MODEL), lambda i, j: (j, 0))]
                 + [pl.BlockSpec((tm, MIX_W), row)] * 3
                 + [pl.BlockSpec((None, MIX_W, D_MODEL), lambda i, j: (j, 0, 0))],
        out_specs=pl.BlockSpec((tm, D_MODEL), row),
        out_shape=jax.ShapeDtypeStruct((n, D_MODEL), BF16),
        scratch_shapes=[pltpu.VMEM((tm, D_MODEL), F32)],
        compiler_params=_cparams(("parallel", "arbitrary")),
        name="branch_merge",
    )(xb, w_gate_t, o_a, o_b, o_m, w_branch)


def _layer_norm(z, g, b):
    mu = jnp.mean(z, axis=1, keepdims=True)
    zc = z - mu
    var = jnp.mean(zc * zc, axis=1, keepdims=True)
    return zc * lax.rsqrt(var + LN_EPS) * g + b


def _out_ln_body(y_ref, w_ref, x_ref, g_ref, b_ref, wq_ref, o_ref, ob_ref, q_ref):
    y = jnp.dot(y_ref[...], w_ref[...], preferred_element_type=F32)
    o = _layer_norm(ALPHA * x_ref[...] + y, g_ref[...], b_ref[...])
    o_ref[...] = o
    ob = o.astype(BF16)
    ob_ref[...] = ob
    q_ref[...] = jnp.dot(ob, wq_ref[...], preferred_element_type=F32).astype(q_ref.dtype)


def _out_proj_ln(mixed, w_out, x, g, b, w_q):
    n = x.shape[0]
    nq = w_q.shape[1]
    tm = 512
    row = lambda i: (i, 0)
    const = lambda i: (0, 0)
    once = dict(pipeline_mode=pl.Buffered(1))
    return pl.pallas_call(
        _out_ln_body,
        grid=(n // tm,),
        in_specs=[pl.BlockSpec((tm, D_MODEL), row), pl.BlockSpec((D_MODEL, D_MODEL), const, **once),
                  pl.BlockSpec((tm, D_MODEL), row), pl.BlockSpec((1, D_MODEL), const),
                  pl.BlockSpec((1, D_MODEL), const), pl.BlockSpec((D_MODEL, nq), const, **once)],
        out_specs=[pl.BlockSpec((tm, D_MODEL), row), pl.BlockSpec((tm, D_MODEL), row), pl.BlockSpec((tm, nq), row)],
        out_shape=[jax.ShapeDtypeStruct((n, D_MODEL), F32), jax.ShapeDtypeStruct((n, D_MODEL), BF16),
                   jax.ShapeDtypeStruct((n, nq), BF16)],
        compiler_params=_cparams(("parallel",)),
        name="out_proj_ln1_query",
    )(mixed, w_out, x, g, b, w_q)


def _res_ln_body(x_ref, y_ref, g_ref, b_ref, o_ref):
    o_ref[...] = _layer_norm(ALPHA * x_ref[...] + y_ref[...], g_ref[...], b_ref[...])


def _residual_ln(x, y, g, b):
    n = x.shape[0]
    tm = 512
    row = lambda i: (i, 0)
    const = lambda i: (0, 0)
    return pl.pallas_call(
        _res_ln_body,
        grid=(n // tm,),
        in_specs=[pl.BlockSpec((tm, D_MODEL), row), pl.BlockSpec((tm, D_MODEL), row),
                  pl.BlockSpec((1, D_MODEL), const), pl.BlockSpec((1, D_MODEL), const)],
        out_specs=pl.BlockSpec((tm, D_MODEL), row),
        out_shape=jax.ShapeDtypeStruct((n, D_MODEL), F32),
        compiler_params=_cparams(("parallel",)),
        name="residual_ln2",
    )(x, y, g, b)


def _topk_axis0(v, k):
    r, n = v.shape
    iota = lax.broadcasted_iota(jnp.int32, (r, n), 0).astype(F32)
    vals, idxs = [], []
    for _ in range(k):
        mx = jnp.max(v, axis=0, keepdims=True)
        idx = jnp.min(jnp.where(v == mx, iota, float(r)), axis=0, keepdims=True)
        v = jnp.where(iota == idx, -jnp.inf, v)
        vals.append(mx)
        idxs.append(idx)
    return jnp.concatenate(vals, axis=0), jnp.concatenate(idxs, axis=0)


def _pick_rows(table, pos, k):
    out = jnp.zeros(pos.shape, F32)
    for a in range(k):
        out = out + jnp.where(pos == float(a), table[a:a + 1], 0.0)
    return out


def _route_body(q_ref, keys_ref, ei_ref, ej_ref, g_ref):
    k = PEER_TOPK
    scores = _dot_nt(keys_ref[...], q_ref[...])
    v0, i0 = _topk_axis0(scores[:PEER_NKEYS], k)
    v1, i1 = _topk_axis0(scores[PEER_NKEYS:], k)
    counts = [k // (a + 1) for a in range(k)]
    starts = np.cumsum([0] + counts[:-1])
    pad = (-sum(counts)) % 8
    comb = jnp.concatenate([v0[a:a + 1] + v1[:counts[a]] for a in range(k)]
                           + [jnp.full((pad, v0.shape[1]), -jnp.inf, F32)], axis=0)
    sf, pos = _topk_axis0(comb, k)
    pa = jnp.zeros(pos.shape, F32)
    pb = pos
    for a in range(1, k):
        later = pos >= float(starts[a])
        pa = pa + jnp.where(later, 1.0, 0.0)
        pb = pb - jnp.where(later, float(counts[a - 1]), 0.0)
    ei_ref[...] = _pick_rows(i0, pa, k)
    ej_ref[...] = _pick_rows(i1, pb, k)
    e = jnp.exp(sf - jnp.max(sf, axis=0, keepdims=True))
    g_ref[...] = e / jnp.sum(e, axis=0, keepdims=True)


def _peer_route(q, keys):
    n = q.shape[0]
    tn = 512
    out = jax.ShapeDtypeStruct((PEER_HEADS, PEER_TOPK, n), F32)
    ospec = pl.BlockSpec((None, PEER_TOPK, tn), lambda i, h: (h, 0, i))
    return pl.pallas_call(
        _route_body,
        grid=(n // tn, PEER_HEADS),
        in_specs=[pl.BlockSpec((tn, PEER_DKEY), lambda i, h: (i, h)),
                  pl.BlockSpec((None, 2 * PEER_NKEYS, PEER_DKEY), lambda i, h: (h, 0, 0))],
        out_specs=[ospec, ospec, ospec],
        out_shape=[out, out, out],
        compiler_params=_cparams(("parallel", "arbitrary")),
        name="peer_route",
    )(q, keys)


def _gate_body(ei_ref, ej_ref, g_ref, o_ref):
    tb = ei_ref.shape[0]
    nk = PEER_NKEYS
    iota = lax.broadcasted_iota(jnp.int32, (tb, nk, ei_ref.shape[2]), 1).astype(F32)
    rows = jnp.where(iota == ei_ref[...], 1.0, 0.0).astype(BF16)
    cols = jnp.where(iota == ej_ref[...], g_ref[...], 0.0).astype(BF16)
    gm = lax.dot_general(rows, cols, (((2,), (2,)), ((0,), (0,))), preferred_element_type=F32)
    o_ref[...] = jnp.swapaxes(gm, 0, 1).astype(o_ref.dtype)


def _gate_matrix(ei, ej, g):
    n, _, slots = ei.shape
    tb = 128
    spec = pl.BlockSpec((tb, 1, slots), lambda i: (i, 0, 0))
    return pl.pallas_call(
        _gate_body,
        grid=(n // tb,),
        in_specs=[spec, spec, spec],
        out_specs=pl.BlockSpec((PEER_NKEYS, tb, PEER_NKEYS), lambda i: (0, i, 0)),
        out_shape=jax.ShapeDtypeStruct((PEER_NKEYS, n, PEER_NKEYS), BF16),
        compiler_params=_cparams(("parallel",)),
        name="peer_gate_matrix",
    )(ei, ej, g)


def _expert_body(x_ref, u_ref, g_ref, v_ref, o_ref):
    @pl.when(pl.program_id(1) == 0)
    def _():
        o_ref[...] = jnp.zeros(o_ref.shape, F32)

    hid = _dot_nt(x_ref[...], u_ref[...].astype(BF16))
    gate = jnp.concatenate([g_ref[i] for i in range(g_ref.shape[0])], axis=1)
    act = (jax.nn.gelu(hid) * gate.astype(F32)).astype(BF16)
    o_ref[...] += jnp.dot(act, v_ref[...].astype(BF16), preferred_element_type=F32)


def _peer_experts(xb, u, gm, v):
    n = xb.shape[0]
    ne = u.shape[0]
    tn, te = 1024, 1024
    once = dict(pipeline_mode=pl.Buffered(1))
    return pl.pallas_call(
        _expert_body,
        grid=(n // tn, ne // te),
        in_specs=[pl.BlockSpec((tn, D_MODEL), lambda i, j: (i, 0), **once),
                  pl.BlockSpec((te, D_MODEL), lambda i, j: (j, 0)),
                  pl.BlockSpec((te // PEER_NKEYS, tn, PEER_NKEYS), lambda i, j: (j, i, 0)),
                  pl.BlockSpec((te, D_MODEL), lambda i, j: (j, 0))],
        out_specs=pl.BlockSpec((tn, D_MODEL), lambda i, j: (i, 0), **once),
        out_shape=jax.ShapeDtypeStruct((n, D_MODEL), F32),
        compiler_params=_cparams(("parallel", "arbitrary")),
        name="peer_experts",
    )(xb, u, gm, v)


def _token_mixer(x, mem, w_in, diff_lambda, diff_subln, cmp_pe, cmp_w1, cmp_w2, w_mem_kv, w_branch, rel_bias):
    b, t, _ = x.shape
    n = b * t
    L = ATT_TILE

    w_a, w_mg, w_g = _w_prep(w_in.T)
    ha, hg, xb = _in_proj(x.reshape(n, D_MODEL), w_a, w_g)
    ha3 = ha.reshape(b, t, ha.shape[1])

    tab1d = rel_bias[_rel_bucket(jnp.arange(t))].T * LOG2E
    n_far = min(t // L, REL_MAX_DIST // L + 2)
    bias_a = _bias_tiles(tab1d[:A_HEADS], t, -1, n_far + 1, t, 1)
    bias_b = tab1d[A_HEADS:]
    bias_sel = _bias_tiles(bias_b, t, -1, n_far + 1, t, B_HPG)
    n_win = min(t // L, WINDOW // L + 1)
    bias_win = _bias_tiles(bias_b, t, -1, n_win + 1, WINDOW, B_HPG)

    def values_t(v):
        heads, dv = v.shape[2:]
        return v.reshape(b, t // L, L, heads, dv).transpose(0, 3, 1, 4, 2)

    o_a = _diff_attention(ha3, values_t(ha3[:, :, 3072:4096].reshape(b, t, A_HEADS, A_DV)), bias_a, diff_lambda,
                          jnp.broadcast_to(diff_subln[:, None], (A_DV, LANES)), b, t)

    glog_t = hg[:, :48].reshape(b, t, 3, B_GROUPS, B_HPG).transpose(2, 0, 3, 4, 1)
    nr = t // CMP_STRIDE
    w1 = cmp_w1.reshape(2, CMP_LEN, B_DH, CMP_HIDDEN)
    z1 = jnp.zeros_like(w1[0])
    w1_bd = jnp.concatenate([jnp.concatenate([w1[0], z1], axis=2), jnp.concatenate([z1, w1[1]], axis=2)],
                            axis=1).astype(BF16)
    z2 = jnp.zeros_like(cmp_w2[0])
    w2_bd = jnp.concatenate([jnp.concatenate([cmp_w2[0], z2], axis=1), jnp.concatenate([z2, cmp_w2[1]], axis=1)],
                            axis=0).astype(BF16)
    pe_bd = jnp.broadcast_to(jnp.concatenate([cmp_pe[0], cmp_pe[1]], axis=1)[:, None, :],
                             (CMP_LEN, 8, 2 * B_DH)).astype(BF16)
    kvc = _compress(ha3[:, :, 5120:5632].astype(F32), w1_bd, w2_bd, pe_bd)
    n_c = (t - CMP_LEN) // CMP_STRIDE + 1
    n_sel = t // SEL_BLOCK
    cidx = np.arange(nr)[:, None] * CMP_STRIDE + np.arange(CMP_LEN)[None, :]
    overlap = (cidx[:, :, None] // SEL_BLOCK == np.arange(n_sel)[None, None, :]).astype(np.float32).mean(axis=1)
    overlap[n_c:] = 0.0
    overlap_t = np.zeros((LANES, nr), np.float32)
    overlap_t[:n_sel] = overlap.T
    v_sel = ha3[:, :, 5632:6144].reshape(b, t, B_GROUPS, 2, B_DH)[:, :, :, 1]
    v_win = ha3[:, :, 6144:6656].reshape(b, t, B_GROUPS, 2, B_DH)[:, :, :, 1]
    assert n_sel <= B_DH
    hot = np.zeros((t, LANES), np.float32)
    hot[np.arange(t), B_DH + np.arange(t) // SEL_BLOCK] = 1.0
    o_b = _nsa_attention(ha3, kvc, jnp.asarray(overlap_t, BF16), glog_t, values_t(v_sel), bias_sel,
                         jnp.asarray(hot, BF16), values_t(v_win), bias_win, b, t, n_c, min(SEL_TOPN, n_sel))

    n_mem = mem.shape[1]
    memkv = _matmul(mem.reshape(b * n_mem, D_MODEL).astype(BF16), w_mem_kv.astype(BF16), BF16,
                    b * n_mem, 1024, "mem_kv_proj")
    o_m = _memory_attention(ha3, memkv, b, t, n_mem)

    flat = lambda o: o.reshape(n, MIX_W)
    return _merge(xb, w_mg, flat(o_a), flat(o_b), flat(o_m), w_branch.astype(BF16))


def _peer(x1b, q, peer_keys, peer_u, peer_v):
    n = x1b.shape[0]
    zk = jnp.zeros((PEER_HEADS, PEER_NKEYS, PEER_DKEY // 2), F32)
    keys_bd = jnp.concatenate([jnp.concatenate([peer_keys[:, 0], zk], axis=2),
                               jnp.concatenate([zk, peer_keys[:, 1]], axis=2)], axis=1).astype(BF16)
    ei, ej, gate = _peer_route(q, keys_bd)
    slots = lambda a: a.reshape(PEER_HEADS * PEER_TOPK, n).T.reshape(n, 1, PEER_HEADS * PEER_TOPK)
    gm = _gate_matrix(slots(ei), slots(ej), slots(gate))
    return _peer_experts(x1b, peer_u, gm, peer_v)


def kernel(x, mem, w_in, diff_lambda, diff_subln, cmp_pe, cmp_w1, cmp_w2, w_mem_kv, w_branch, w_out, ln1_g, ln1_b,
           peer_wq, peer_keys, peer_u, peer_v, ln2_g, ln2_b, rel_bias):
    b, t, _ = x.shape
    n = b * t
    for l in range(DEPTH):
        mixed = _token_mixer(x, mem, w_in[l], diff_lambda[l], diff_subln[l], cmp_pe[l], cmp_w1[l], cmp_w2[l],
                             w_mem_kv[l], w_branch[l], rel_bias)
        x1, x1b, q = _out_proj_ln(mixed, w_out[l].astype(BF16), x.reshape(n, D_MODEL), ln1_g[l].reshape(1, D_MODEL),
                                  ln1_b[l].reshape(1, D_MODEL), peer_wq[l].astype(BF16))
        y = _peer(x1b, q, peer_keys[l], peer_u[l], peer_v[l])
        x = _residual_ln(x1, y, ln2_g[l].reshape(1, D_MODEL), ln2_b[l].reshape(1, D_MODEL)).reshape(b, t, D_MODEL)
    return x
```

```python
import functools
import math

import numpy as np
import jax
import jax.numpy as jnp
from jax import lax
from jax.experimental import pallas as pl
from jax.experimental.pallas import tpu as pltpu

F32 = jnp.float32
BF16 = jnp.bfloat16

D_MODEL = 2048
A_HEADS, A_DQK, A_DV = 8, 64, 128
B_HEADS, B_GROUPS, B_HPG, B_DH = 16, 4, 4, 64
CMP_LEN, CMP_STRIDE, CMP_HIDDEN = 32, 16, 256
SEL_BLOCK, SEL_TOPN, WINDOW = 64, 16, 512
M_HEADS, M_DH = 4, 256
REL_BUCKETS, REL_MAX_DIST = 32, 1024
PEER_HEADS, PEER_NKEYS, PEER_DKEY, PEER_TOPK = 8, 128, 256, 16
MIX_W = 1024
LN_EPS = 1e-5
FORCE = 1e9
DEPTH = 1
ALPHA = (2 * DEPTH) ** 0.25
LAMBDA_INIT = 0.8 - 0.6 * math.exp(-0.3 * 0)
LOG2E = math.log2(math.e)

NEG = -1e30
ATT_TILE = 256
LANES = 128
ONES_ROWS = 16
N_MERGE = 3
FLASH_UNROLL = 8
VMEM_LIMIT = 56 * 1024 * 1024


def _cparams(sem):
    return pltpu.CompilerParams(dimension_semantics=sem, vmem_limit_bytes=VMEM_LIMIT)


def _dot_nt(a, b):
    return lax.dot_general(a, b, (((1,), (1,)), ((), ())), preferred_element_type=F32)


def _mm_body(x_ref, w_ref, o_ref):
    o_ref[...] = jnp.dot(x_ref[...], w_ref[...], preferred_element_type=F32).astype(o_ref.dtype)


def _matmul(x, w, out_dtype, tm, tn, name):
    m, k = x.shape
    n = w.shape[1]
    return pl.pallas_call(
        _mm_body,
        grid=(m // tm, n // tn),
        in_specs=[pl.BlockSpec((tm, k), lambda i, j: (i, 0)), pl.BlockSpec((k, tn), lambda i, j: (0, j))],
        out_specs=pl.BlockSpec((tm, tn), lambda i, j: (i, j)),
        out_shape=jax.ShapeDtypeStruct((m, n), out_dtype),
        compiler_params=_cparams(("parallel", "arbitrary")),
        name=name,
    )(x, w)


def _in_proj_body(x_ref, w_ref, wg_ref, o_ref, og_ref, xb_ref):
    xb = x_ref[...].astype(BF16)
    o_ref[...] = _dot_nt(xb, w_ref[...]).astype(o_ref.dtype)

    @pl.when(pl.program_id(1) == 0)
    def _():
        xb_ref[...] = xb
        og_ref[...] = _dot_nt(xb, wg_ref[...])


def _in_proj(x, w_a, w_g):
    n, k = x.shape
    tm, tn = 512, 1664
    return pl.pallas_call(
        _in_proj_body,
        grid=(n // tm, W_A_COLS // tn),
        in_specs=[pl.BlockSpec((tm, k), lambda i, j: (i, 0)),
                  pl.BlockSpec((tn, k), lambda i, j: (j, 0)),
                  pl.BlockSpec((W_G_COLS, k), lambda i, j: (0, 0))],
        out_specs=[pl.BlockSpec((tm, tn), lambda i, j: (i, j)),
                   pl.BlockSpec((tm, W_G_COLS), lambda i, j: (i, 0)),
                   pl.BlockSpec((tm, k), lambda i, j: (i, 0))],
        out_shape=[jax.ShapeDtypeStruct((n, W_A_COLS), BF16), jax.ShapeDtypeStruct((n, W_G_COLS), F32),
                   jax.ShapeDtypeStruct((n, k), BF16)],
        compiler_params=_cparams(("parallel", "arbitrary")),
        name="in_proj",
    )(x, w_a, w_g)


W_A_COLS = 6656
W_G_COLS = 128


def _w_prep_body(w_ref, wa_ref, wm_ref, wg_ref):
    def put(dst_ref, d0, s0, n, c=1.0):
        val = w_ref[s0:s0 + n, :]
        dst_ref[d0:d0 + n, :] = (val * c if c != 1.0 else val).astype(dst_ref.dtype)

    put(wa_ref, 0, 5680, 1024, M_DH ** -0.5 * LOG2E)
    put(wa_ref, 1024, 0, 1024, A_DQK ** -0.5 * LOG2E)
    put(wa_ref, 2048, 1024, 2048)
    put(wa_ref, 4096, 3072, 1024, B_DH ** -0.5 * LOG2E)
    for br in range(3):
        for g in range(B_GROUPS):
            src = 4096 + br * 2 * B_GROUPS * B_DH + g * B_DH
            dst = 5120 + (br * B_GROUPS + g) * 2 * B_DH
            put(wa_ref, dst, src, B_DH)
            put(wa_ref, dst + B_DH, src + B_GROUPS * B_DH, B_DH)
    put(wm_ref, 0, 6704, N_MERGE * D_MODEL)
    put(wg_ref, 0, 5632, 48)
    wg_ref[48:, :] = jnp.zeros((W_G_COLS - 48, wg_ref.shape[1]), wg_ref.dtype)


def _w_prep(w_in_t):
    c, k = w_in_t.shape
    tk = 256
    col = lambda i: (0, i)
    return pl.pallas_call(
        _w_prep_body,
        grid=(k // tk,),
        in_specs=[pl.BlockSpec((c, tk), col)],
        out_specs=[pl.BlockSpec((W_A_COLS, tk), col), pl.BlockSpec((N_MERGE * D_MODEL, tk), col),
                   pl.BlockSpec((W_G_COLS, tk), col)],
        out_shape=[jax.ShapeDtypeStruct((W_A_COLS, k), BF16), jax.ShapeDtypeStruct((N_MERGE * D_MODEL, k), BF16),
                   jax.ShapeDtypeStruct((W_G_COLS, k), BF16)],
        compiler_params=_cparams(("parallel",)),
        name="w_in_regroup",
    )(w_in_t)


def _rel_bucket(dist):
    n = jnp.maximum(dist, 0)
    max_exact = REL_BUCKETS // 2
    nf = jnp.maximum(n, 1).astype(jnp.float32)
    large = max_exact + (jnp.log(nf / max_exact) / math.log(REL_MAX_DIST / max_exact)
                         * (REL_BUCKETS - max_exact)).astype(jnp.int32)
    large = jnp.minimum(large, REL_BUCKETS - 1)
    return jnp.where(n < max_exact, n, large)


def _bias_tiles(tab1d, t, first, n_tiles, max_dist, hpr):
    L = ATT_TILE
    m = np.arange(2 * L)
    off = np.where(m <= L, m, m - 2 * L)
    d = (first + np.arange(n_tiles))[:, None] * L + off[None, :]
    ok = (d >= 0) & (d < max_dist)
    h = tab1d.shape[0]
    rp = jnp.where(ok[:, None], tab1d.T[np.clip(d, 0, t - 1)].transpose(0, 2, 1), NEG)
    return pl.pallas_call(
        functools.partial(_toeplitz_body, hpr=hpr),
        grid=(n_tiles,),
        in_specs=[pl.BlockSpec((None, h, 1, 2 * L), lambda c: (c, 0, 0, 0))],
        out_specs=pl.BlockSpec((None, h // hpr, L, hpr * L), lambda c: (c, 0, 0, 0)),
        out_shape=jax.ShapeDtypeStruct((n_tiles, h // hpr, L, hpr * L), F32),
        compiler_params=_cparams(("parallel",)),
        name="bias_tiles",
    )(rp.reshape(n_tiles, h, 1, 2 * L))


def _toeplitz_body(rp_ref, o_ref, *, hpr):
    L = o_ref.shape[1]
    for hd in range(rp_ref.shape[0]):
        rows = jnp.broadcast_to(rp_ref[hd], (L, 2 * L))
        tile = pltpu.roll(rows, 0, 1, stride=1, stride_axis=0)[:, :L]
        o_ref[hd // hpr, :, (hd % hpr) * L:(hd % hpr + 1) * L] = tile


def _flash_loop(lo, hi, scores, values, m_ref, acc_ref, s_ref, mx_ref):
    def ahead(kt, slot):
        s = scores(kt)
        s_ref[slot] = s
        mx_ref[slot] = jnp.broadcast_to(jnp.max(s, axis=0, keepdims=True), mx_ref.shape[1:])

    def finish(kt, slot):
        m_prev = m_ref[...]
        m_new = jnp.maximum(m_prev, mx_ref[slot])
        alpha = jnp.exp2(m_prev - m_new)
        p = jnp.exp2(s_ref[slot] - m_new[0:1])
        acc_ref[...] = alpha[0:1] * acc_ref[...] + jnp.dot(values(kt), p.astype(BF16), preferred_element_type=F32)
        m_ref[...] = m_new

    if lo is None:
        tiles = hi
        ahead(tiles[0], 0)
        for j, kt in enumerate(tiles):
            if j + 1 < len(tiles):
                ahead(tiles[j + 1], (j + 1) % 2)
            finish(kt, j % 2)
        return

    def body(i, carry):
        kt = lo + FLASH_UNROLL * i
        for j in range(FLASH_UNROLL):
            ahead(kt + j + 1, (j + 1) % 2)
            finish(kt + j, j % 2)
        return carry

    def tail(r):
        def run():
            for j in range(r):
                if j + 1 < r:
                    ahead(base + j + 1, (j + 1) % 2)
                finish(base + j, j % 2)
            return 0
        return run

    ahead(lo, 0)
    n_full = (hi - lo) // FLASH_UNROLL
    lax.fori_loop(0, n_full, body, 0)
    base = lo + n_full * FLASH_UNROLL
    lax.switch(hi - base, [tail(r) for r in range(FLASH_UNROLL)])


def _diff_body(q_ref, k_ref, vt_ref, bias_ref, lam_ref, g_ref, o_ref, m_sc, acc_sc, s_sc, mx_sc, *, n_bt):
    L = ATT_TILE
    tq = 2 * L
    last = vt_ref.shape[0] - 1
    qi = pl.program_id(2)
    m_sc[...] = jnp.full(m_sc.shape, NEG, F32)
    acc_sc[...] = jnp.zeros(acc_sc.shape, F32)
    qt = q_ref[...].astype(F32).T.astype(BF16)
    zero = jnp.zeros((A_DQK, tq), BF16)
    q_cat = jnp.concatenate([jnp.concatenate([qt[:A_DQK], zero], axis=0),
                             jnp.concatenate([zero, qt[A_DQK:]], axis=0)], axis=1)

    def scores(kt):
        off = pl.multiple_of(jnp.minimum(kt, last) * L, L)
        k = k_ref[pl.ds(off, L), :]
        d0 = 2 * qi - kt
        bias = jnp.concatenate([bias_ref[jnp.clip(d0 + 1, 0, n_bt - 1)],
                                bias_ref[jnp.clip(d0 + 2, 0, n_bt - 1)]], axis=1)
        s = jnp.dot(k, q_cat, preferred_element_type=F32)
        return jnp.concatenate([s[:, :tq] + bias, s[:, tq:] + bias], axis=1)

    ones = jnp.ones((ONES_ROWS, L), BF16)

    def values(kt):
        return jnp.concatenate([vt_ref[jnp.minimum(kt, last)], ones], axis=0)

    def unrolled(n_tiles):
        def run():
            _flash_loop(None, list(range(n_tiles)), scores, values, m_sc, acc_sc, s_sc, mx_sc)
            return 0
        return run

    lax.switch(qi, [unrolled(2 * i + 2) for i in range((last + 1) * L // tq)])

    lp = lam_ref[...]
    lam = (jnp.exp(jnp.sum(lp[0:1] * lp[1:2], axis=1, keepdims=True))
           - jnp.exp(jnp.sum(lp[2:3] * lp[3:4], axis=1, keepdims=True)) + LAMBDA_INIT)
    acc = acc_sc[...]
    o0 = acc[:A_DV, :tq] / jnp.maximum(acc[A_DV:A_DV + 1, :tq], 1e-30)
    o1 = acc[:A_DV, tq:] / jnp.maximum(acc[A_DV:A_DV + 1, tq:], 1e-30)
    o = o0 - lam * o1
    g = jnp.concatenate([g_ref[...]] * (tq // LANES), axis=1)
    o = o * lax.rsqrt(jnp.mean(o * o, axis=0, keepdims=True) + LN_EPS) * g
    o_ref[...] = (o * (1.0 - LAMBDA_INIT)).T.astype(o_ref.dtype)


def _diff_attention(ha, v_t, bias, lam_params, subln, b, t):
    L = ATT_TILE
    tq = 2 * L
    n_bt = bias.shape[0]
    return pl.pallas_call(
        functools.partial(_diff_body, n_bt=n_bt),
        grid=(b, A_HEADS, t // tq),
        in_specs=[
            pl.BlockSpec((None, tq, 128), lambda bi, h, qi: (bi, qi, 8 + h)),
            pl.BlockSpec((None, t, 128), lambda bi, h, qi: (bi, 0, 16 + h)),
            pl.BlockSpec((None, None, t // L, A_DV, L), lambda bi, h, qi: (bi, h, 0, 0, 0)),
            pl.BlockSpec((n_bt, None, L, L), lambda bi, h, qi: (0, h, 0, 0)),
            pl.BlockSpec((4, A_DQK), lambda bi, h, qi: (0, 0)),
            pl.BlockSpec((A_DV, LANES), lambda bi, h, qi: (0, 0)),
        ],
        out_specs=pl.BlockSpec((None, tq, 128), lambda bi, h, qi: (bi, qi, h)),
        out_shape=jax.ShapeDtypeStruct((b, t, MIX_W), BF16),
        scratch_shapes=[pltpu.VMEM((8, 2 * tq), F32), pltpu.VMEM((A_DV + ONES_ROWS, 2 * tq), F32),
                        pltpu.VMEM((2, L, 2 * tq), F32), pltpu.VMEM((2, 8, 2 * tq), F32)],
        compiler_params=_cparams(("parallel", "parallel", "arbitrary")),
        name="diff_attention",
    )(ha, ha, v_t, bias, lam_params, subln)


def _compress_body(kv_ref, w1_ref, w2_ref, pe_ref, o_ref):
    nr = o_ref.shape[0]
    first = jnp.zeros((nr, 2 * CMP_HIDDEN), F32)
    second = jnp.zeros((nr, 2 * CMP_HIDDEN), F32)
    for l in range(CMP_STRIDE):
        rows = kv_ref[pl.ds(l, nr, stride=CMP_STRIDE), :].astype(BF16)
        first = first + jnp.dot(rows, w1_ref[l], preferred_element_type=F32)
        second = second + jnp.dot(rows, w1_ref[CMP_STRIDE + l], preferred_element_type=F32)
    pw = jnp.zeros((8, 2 * CMP_HIDDEN), F32)
    for l in range(CMP_LEN):
        pw = pw + jnp.dot(pe_ref[l], w1_ref[l], preferred_element_type=F32)
    second = jnp.concatenate([second[1:], second[:1]], axis=0)
    hdn = jax.nn.gelu(first + second + pw[0:1])
    o_ref[...] = jnp.dot(hdn.astype(BF16), w2_ref[...], preferred_element_type=F32)


def _compress(kv, w1_bd, w2_bd, pe_bd):
    b, t, _ = kv.shape
    nr = t // CMP_STRIDE
    return pl.pallas_call(
        _compress_body,
        grid=(b, B_GROUPS),
        in_specs=[
            pl.BlockSpec((None, t, LANES), lambda bi, gi: (bi, 0, gi)),
            pl.BlockSpec(w1_bd.shape, lambda bi, gi: (0, 0, 0)),
            pl.BlockSpec(w2_bd.shape, lambda bi, gi: (0, 0)),
            pl.BlockSpec(pe_bd.shape, lambda bi, gi: (0, 0, 0)),
        ],
        out_specs=pl.BlockSpec((None, None, nr, 2 * B_DH), lambda bi, gi: (bi, gi, 0, 0)),
        out_shape=jax.ShapeDtypeStruct((b, B_GROUPS, nr, 2 * B_DH), F32),
        compiler_params=_cparams(("parallel", "parallel")),
        name="nsa_compress",
    )(kv, w1_bd, w2_bd, pe_bd)


def _topk_mask_axis0(v, k):
    r, n = v.shape
    iota = lax.broadcasted_iota(jnp.int32, (r, n), 0).astype(F32)
    sel = jnp.zeros((r, n), F32)
    for _ in range(k):
        mx = jnp.max(v, axis=0, keepdims=True)
        idx = jnp.min(jnp.where(v == mx, iota, float(r)), axis=0, keepdims=True)
        hit = iota == idx
        v = jnp.where(hit, -jnp.inf, v)
        sel = jnp.where(hit, 1.0, sel)
    return sel


def _compressed_branch(qt, kvc, ov_t, qi, n_c, n_top, n_sel):
    L = ATT_TILE
    ncp = kvc.shape[0]
    qh = jnp.concatenate([qt[h * B_DH:(h + 1) * B_DH] for h in range(B_HPG)], axis=1)
    s = jnp.dot(kvc[:, :B_DH].astype(BF16), qh, preferred_element_type=F32)
    tcol = qi * L + lax.broadcasted_iota(jnp.int32, (1, L), 1)
    crow = lax.broadcasted_iota(jnp.int32, (ncp, 1), 0)
    seen = jnp.where(crow < n_c, crow * CMP_STRIDE + (CMP_LEN - 1), jnp.int32(2 ** 30)) <= tcol
    valid = jnp.concatenate([seen] * B_HPG, axis=1)
    s = jnp.where(valid, s, NEG)
    mx = jnp.max(s, axis=0, keepdims=True)
    e = jnp.where(valid, jnp.exp2(s - mx), 0.0)
    p = e / jnp.maximum(jnp.sum(e, axis=0, keepdims=True), 1e-30)
    o = jnp.dot(kvc.T[B_DH:].astype(BF16), p.astype(BF16), preferred_element_type=F32)

    psum = p[:, :L] + p[:, L:2 * L] + p[:, 2 * L:3 * L] + p[:, 3 * L:]
    imp = jnp.zeros((ov_t.shape[0], L), F32)
    rem = psum
    for _ in range(3):
        part = rem.astype(BF16)
        imp = imp + jnp.dot(ov_t, part, preferred_element_type=F32)
        rem = rem - part.astype(F32)
    imp = imp[:n_sel]
    blk = lax.broadcasted_iota(jnp.int32, (n_sel, 1), 0)
    cur = jnp.right_shift(tcol, int(math.log2(SEL_BLOCK)))
    imp = jnp.where(blk * SEL_BLOCK > tcol, -FORCE, imp)
    imp = jnp.where(blk == 0, FORCE, imp)
    imp = jnp.where(blk == cur, FORCE, imp)
    imp = jnp.where(blk == cur - 1, FORCE, imp)
    return o, jnp.where(_topk_mask_axis0(imp, n_top) > 0.0, 0.0, NEG)


def _nsa_body(q_ref, kvc_ref, ov_ref, gl_ref, kvs_ref, vts_ref, bs_ref, hot_ref, kvw_ref, vtw_ref, bw_ref, o_ref,
              m_sc, acc_sc, s_sc, mx_sc, *, n_c, n_top, n_sel):
    L = ATT_TILE
    qi = pl.program_id(2)
    last = vts_ref.shape[0] - 1
    qt = q_ref[...].astype(F32).T.astype(BF16)
    gates = jax.nn.sigmoid(gl_ref[...])
    ones = jnp.ones((ONES_ROWS, L), BF16)
    k_lanes = lax.broadcasted_iota(jnp.int32, (L, LANES), 1) < B_DH

    def gated(o, branch):
        return jnp.concatenate([o[:, h * L:(h + 1) * L] * gates[branch, h:h + 1] for h in range(B_HPG)], axis=0)

    def attend(qm, kv_ref, vt_ref, bias_ref, use_hot, tiles):
        m_sc[...] = jnp.full(m_sc.shape, NEG, F32)
        acc_sc[...] = jnp.zeros(acc_sc.shape, F32)
        q_aug_t = jnp.concatenate([jnp.concatenate([qt[h * B_DH:(h + 1) * B_DH], qm], axis=0)
                                   for h in range(B_HPG)], axis=1)
        n_bt = bias_ref.shape[0]

        def scores(kt):
            off = pl.multiple_of(jnp.clip(kt, 0, last) * L, L)
            idx = jnp.where(kt < 0, 0, jnp.clip(qi - kt + 1, 0, n_bt - 1))
            k_aug = kv_ref[pl.ds(off, L), :]
            if use_hot:
                k_aug = jnp.where(k_lanes, k_aug, hot_ref[pl.ds(off, L), :])
            return jnp.dot(k_aug, q_aug_t, preferred_element_type=F32) + bias_ref[idx]

        def values(kt):
            return jnp.concatenate([vt_ref[jnp.clip(kt, 0, last)], ones], axis=0)

        if tiles is None:
            _flash_loop(0, qi + 1, scores, values, m_sc, acc_sc, s_sc, mx_sc)
        else:
            _flash_loop(None, tiles, scores, values, m_sc, acc_sc, s_sc, mx_sc)
        acc = acc_sc[...]
        return acc[:B_DH] / jnp.maximum(acc[B_DH:B_DH + 1], 1e-30)

    o_cmp, sel = _compressed_branch(qt, kvc_ref[...], ov_ref[...], qi, n_c, n_top, n_sel)
    window_tiles = [qi - j for j in range(bw_ref.shape[0] - 1)]
    o_win = attend(jnp.zeros((B_DH, L), BF16), kvw_ref, vtw_ref, bw_ref, False, window_tiles)
    qm = sel.astype(BF16)
    if n_sel < B_DH:
        qm = jnp.concatenate([qm, jnp.zeros((B_DH - n_sel, L), BF16)], axis=0)
    o_sel = attend(qm, kvs_ref, vts_ref, bs_ref, True, None)
    total = gated(o_cmp, 0) + gated(o_sel, 1) + gated(o_win, 2)
    o_ref[...] = total.T.astype(o_ref.dtype)


def _nsa_attention(ha, kvc, overlap_t, glog_t, vt_sel, bias_sel, hot, vt_win, bias_win, b, t, n_c, n_top):
    L = ATT_TILE
    ncp = kvc.shape[2]
    n_sel = t // SEL_BLOCK
    kv_spec = lambda br: pl.BlockSpec((None, t, LANES), lambda bi, g, qi: (bi, 0, 40 + 4 * br + g))
    vt_spec = pl.BlockSpec((None, None, t // L, B_DH, L), lambda bi, g, qi: (bi, g, 0, 0, 0))
    bias_spec = lambda bias: pl.BlockSpec((bias.shape[0], None, L, B_HPG * L), lambda bi, g, qi: (0, g, 0, 0))
    return pl.pallas_call(
        functools.partial(_nsa_body, n_c=n_c, n_top=n_top, n_sel=n_sel),
        grid=(b, B_GROUPS, t // L),
        in_specs=[
            pl.BlockSpec((None, L, 256), lambda bi, g, qi: (bi, qi, 16 + g)),
            pl.BlockSpec((None, None, ncp, 2 * B_DH), lambda bi, g, qi: (bi, g, 0, 0)),
            pl.BlockSpec(overlap_t.shape, lambda bi, g, qi: (0, 0)),
            pl.BlockSpec((3, None, None, B_HPG, L), lambda bi, g, qi: (0, bi, g, 0, qi)),
            kv_spec(1), vt_spec, bias_spec(bias_sel),
            pl.BlockSpec((t, LANES), lambda bi, g, qi: (0, 0)),
            kv_spec(2), vt_spec, bias_spec(bias_win),
        ],
        out_specs=pl.BlockSpec((None, L, 256), lambda bi, g, qi: (bi, qi, g)),
        out_shape=jax.ShapeDtypeStruct((b, t, MIX_W), BF16),
        scratch_shapes=[pltpu.VMEM((8, B_HPG * L), F32), pltpu.VMEM((B_DH + ONES_ROWS, B_HPG * L), F32),
                        pltpu.VMEM((2, L, B_HPG * L), F32), pltpu.VMEM((2, 8, B_HPG * L), F32)],
        compiler_params=_cparams(("parallel", "parallel", "arbitrary")),
        name="nsa_attention",
    )(ha, kvc, overlap_t, glog_t, ha, vt_sel, bias_sel, hot, ha, vt_win, bias_win)


def _mem_body(q_ref, k_ref, v_ref, o_ref):
    q = q_ref[...]
    outs = []
    for h in range(M_HEADS):
        sl = slice(h * M_DH, (h + 1) * M_DH)
        s = _dot_nt(q[:, sl], k_ref[:, sl])
        e = jnp.exp2(s - jnp.max(s, axis=1, keepdims=True))
        p = e / jnp.sum(e, axis=1, keepdims=True)
        outs.append(jnp.dot(p.astype(BF16), v_ref[:, sl], preferred_element_type=F32))
    o_ref[...] = jnp.concatenate(outs, axis=1).astype(o_ref.dtype)


def _memory_attention(ha, memkv, b, t, n_mem):
    tq = 512
    w = M_HEADS * M_DH
    return pl.pallas_call(
        _mem_body,
        grid=(b, t // tq),
        in_specs=[
            pl.BlockSpec((None, tq, w), lambda bi, qi: (bi, qi, 0)),
            pl.BlockSpec((n_mem, w), lambda bi, qi: (bi, 0)),
            pl.BlockSpec((n_mem, w), lambda bi, qi: (bi, 1)),
        ],
        out_specs=pl.BlockSpec((None, tq, w), lambda bi, qi: (bi, qi, 0)),
        out_shape=jax.ShapeDtypeStruct((b, t, w), BF16),
        compiler_params=_cparams(("parallel", "arbitrary")),
        name="memory_attention",
    )(ha, memkv, memkv)


def _merge_body(x_ref, wg_ref, oa_ref, ob_ref, om_ref, wb_ref, o_ref, acc_sc):
    n = pl.program_id(1)

    @pl.when(n == 0)
    def _():
        acc_sc[...] = jnp.zeros(acc_sc.shape, F32)

    gate = jax.nn.sigmoid(_dot_nt(x_ref[...], wg_ref[...]))
    branch = jnp.where(n == 0, oa_ref[...], jnp.where(n == 1, ob_ref[...], om_ref[...]))
    acc = acc_sc[...] + gate * jnp.dot(branch, wb_ref[...], preferred_element_type=F32)
    acc_sc[...] = acc
    o_ref[...] = acc.astype(o_ref.dtype)


def _merge(xb, w_gate_t, o_a, o_b, o_m, w_branch):
    n = xb.shape[0]
    tm = 512
    row = lambda i, j: (i, 0)
    return pl.pallas_call(
        _merge_body,
        grid=(n // tm, N_MERGE),
        in_specs=[pl.BlockSpec((tm, D_MODEL), row),
                  pl.BlockSpec((D_MODEL, D_MODEL), lambda i, j: (j, 0))]
                 + [pl.BlockSpec((tm, MIX_W), row)] * 3
                 + [pl.BlockSpec((None, MIX_W, D_MODEL), lambda i, j: (j, 0, 0))],
        out_specs=pl.BlockSpec((tm, D_MODEL), row),
        out_shape=jax.ShapeDtypeStruct((n, D_MODEL), BF16),
        scratch_shapes=[pltpu.VMEM((tm, D_MODEL), F32)],
        compiler_params=_cparams(("parallel", "arbitrary")),
        name="branch_merge",
    )(xb, w_gate_t, o_a, o_b, o_m, w_branch)


def _layer_norm(z, g, b):
    mu = jnp.mean(z, axis=1, keepdims=True)
    zc = z - mu
    var = jnp.mean(zc * zc, axis=1, keepdims=True)
    return zc * lax.rsqrt(var + LN_EPS) * g + b


def _out_ln_body(y_ref, w_ref, x_ref, g_ref, b_ref, wq_ref, o_ref, ob_ref, q_ref):
    y = jnp.dot(y_ref[...], w_ref[...], preferred_element_type=F32)
    o = _layer_norm(ALPHA * x_ref[...] + y, g_ref[...], b_ref[...])
    o_ref[...] = o
    ob = o.astype(BF16)
    ob_ref[...] = ob
    q_ref[...] = jnp.dot(ob, wq_ref[...], preferred_element_type=F32).astype(q_ref.dtype)


def _out_proj_ln(mixed, w_out, x, g, b, w_q):
    n = x.shape[0]
    nq = w_q.shape[1]
    tm = 512
    row = lambda i: (i, 0)
    const = lambda i: (0, 0)
    once = dict(pipeline_mode=pl.Buffered(1))
    return pl.pallas_call(
        _out_ln_body,
        grid=(n // tm,),
        in_specs=[pl.BlockSpec((tm, D_MODEL), row), pl.BlockSpec((D_MODEL, D_MODEL), const, **once),
                  pl.BlockSpec((tm, D_MODEL), row), pl.BlockSpec((1, D_MODEL), const),
                  pl.BlockSpec((1, D_MODEL), const), pl.BlockSpec((D_MODEL, nq), const, **once)],
        out_specs=[pl.BlockSpec((tm, D_MODEL), row), pl.BlockSpec((tm, D_MODEL), row), pl.BlockSpec((tm, nq), row)],
        out_shape=[jax.ShapeDtypeStruct((n, D_MODEL), F32), jax.ShapeDtypeStruct((n, D_MODEL), BF16),
                   jax.ShapeDtypeStruct((n, nq), BF16)],
        compiler_params=_cparams(("parallel",)),
        name="out_proj_ln1_query",
    )(mixed, w_out, x, g, b, w_q)


def _res_ln_body(x_ref, y_ref, g_ref, b_ref, o_ref):
    o_ref[...] = _layer_norm(ALPHA * x_ref[...] + y_ref[...], g_ref[...], b_ref[...])


def _residual_ln(x, y, g, b):
    n = x.shape[0]
    tm = 512
    row = lambda i: (i, 0)
    const = lambda i: (0, 0)
    return pl.pallas_call(
        _res_ln_body,
        grid=(n // tm,),
        in_specs=[pl.BlockSpec((tm, D_MODEL), row), pl.BlockSpec((tm, D_MODEL), row),
                  pl.BlockSpec((1, D_MODEL), const), pl.BlockSpec((1, D_MODEL), const)],
        out_specs=pl.BlockSpec((tm, D_MODEL), row),
        out_shape=jax.ShapeDtypeStruct((n, D_MODEL), F32),
        compiler_params=_cparams(("parallel",)),
        name="residual_ln2",
    )(x, y, g, b)


def _topk_axis0(v, k):
    r, n = v.shape
    iota = lax.broadcasted_iota(jnp.int32, (r, n), 0).astype(F32)
    vals, idxs = [], []
    for _ in range(k):
        mx = jnp.max(v, axis=0, keepdims=True)
        idx = jnp.min(jnp.where(v == mx, iota, float(r)), axis=0, keepdims=True)
        v = jnp.where(iota == idx, -jnp.inf, v)
        vals.append(mx)
        idxs.append(idx)
    return jnp.concatenate(vals, axis=0), jnp.concatenate(idxs, axis=0)


def _pick_rows(table, pos, k):
    out = jnp.zeros(pos.shape, F32)
    for a in range(k):
        out = out + jnp.where(pos == float(a), table[a:a + 1], 0.0)
    return out


def _route_body(q_ref, keys_ref, ei_ref, ej_ref, g_ref):
    k = PEER_TOPK
    scores = _dot_nt(keys_ref[...], q_ref[...])
    v0, i0 = _topk_axis0(scores[:PEER_NKEYS], k)
    v1, i1 = _topk_axis0(scores[PEER_NKEYS:], k)
    counts = [k // (a + 1) for a in range(k)]
    starts = np.cumsum([0] + counts[:-1])
    pad = (-sum(counts)) % 8
    comb = jnp.concatenate([v0[a:a + 1] + v1[:counts[a]] for a in range(k)]
                           + [jnp.full((pad, v0.shape[1]), -jnp.inf, F32)], axis=0)
    sf, pos = _topk_axis0(comb, k)
    pa = jnp.zeros(pos.shape, F32)
    pb = pos
    for a in range(1, k):
        later = pos >= float(starts[a])
        pa = pa + jnp.where(later, 1.0, 0.0)
        pb = pb - jnp.where(later, float(counts[a - 1]), 0.0)
    ei_ref[...] = _pick_rows(i0, pa, k)
    ej_ref[...] = _pick_rows(i1, pb, k)
    e = jnp.exp(sf - jnp.max(sf, axis=0, keepdims=True))
    g_ref[...] = e / jnp.sum(e, axis=0, keepdims=True)


def _peer_route(q, keys):
    n = q.shape[0]
    tn = 1024
    out = jax.ShapeDtypeStruct((PEER_HEADS, PEER_TOPK, n), F32)
    ospec = pl.BlockSpec((None, PEER_TOPK, tn), lambda i, h: (h, 0, i))
    return pl.pallas_call(
        _route_body,
        grid=(n // tn, PEER_HEADS),
        in_specs=[pl.BlockSpec((tn, PEER_DKEY), lambda i, h: (i, h)),
                  pl.BlockSpec((None, 2 * PEER_NKEYS, PEER_DKEY), lambda i, h: (h, 0, 0))],
        out_specs=[ospec, ospec, ospec],
        out_shape=[out, out, out],
        compiler_params=_cparams(("parallel", "arbitrary")),
        name="peer_route",
    )(q, keys)


def _gate_body(ei_ref, ej_ref, g_ref, o_ref):
    tb = ei_ref.shape[0]
    nk = PEER_NKEYS
    iota = lax.broadcasted_iota(jnp.int32, (tb, nk, ei_ref.shape[2]), 1).astype(F32)
    rows = jnp.where(iota == ei_ref[...], 1.0, 0.0).astype(BF16)
    cols = jnp.where(iota == ej_ref[...], g_ref[...], 0.0).astype(BF16)
    gm = lax.dot_general(rows, cols, (((2,), (2,)), ((0,), (0,))), preferred_element_type=F32)
    o_ref[...] = jnp.swapaxes(gm, 0, 1).astype(o_ref.dtype)


def _gate_matrix(ei, ej, g):
    n, _, slots = ei.shape
    tb = 128
    spec = pl.BlockSpec((tb, 1, slots), lambda i: (i, 0, 0))
    return pl.pallas_call(
        _gate_body,
        grid=(n // tb,),
        in_specs=[spec, spec, spec],
        out_specs=pl.BlockSpec((PEER_NKEYS, tb, PEER_NKEYS), lambda i: (0, i, 0)),
        out_shape=jax.ShapeDtypeStruct((PEER_NKEYS, n, PEER_NKEYS), BF16),
        compiler_params=_cparams(("parallel",)),
        name="peer_gate_matrix",
    )(ei, ej, g)


def _expert_body(x_ref, u_ref, g_ref, v_ref, o_ref):
    @pl.when(pl.program_id(1) == 0)
    def _():
        o_ref[...] = jnp.zeros(o_ref.shape, F32)

    hid = _dot_nt(x_ref[...], u_ref[...].astype(BF16))
    gate = jnp.concatenate([g_ref[i] for i in range(g_ref.shape[0])], axis=1)
    act = (jax.nn.gelu(hid) * gate.astype(F32)).astype(BF16)
    o_ref[...] += jnp.dot(act, v_ref[...].astype(BF16), preferred_element_type=F32)


def _peer_experts(xb, u, gm, v):
    n = xb.shape[0]
    ne = u.shape[0]
    tn, te = 1024, 1024
    once = dict(pipeline_mode=pl.Buffered(1))
    return pl.pallas_call(
        _expert_body,
        grid=(n // tn, ne // te),
        in_specs=[pl.BlockSpec((tn, D_MODEL), lambda i, j: (i, 0), **once),
                  pl.BlockSpec((te, D_MODEL), lambda i, j: (j, 0)),
                  pl.BlockSpec((te // PEER_NKEYS, tn, PEER_NKEYS), lambda i, j: (j, i, 0)),
                  pl.BlockSpec((te, D_MODEL), lambda i, j: (j, 0))],
        out_specs=pl.BlockSpec((tn, D_MODEL), lambda i, j: (i, 0), **once),
        out_shape=jax.ShapeDtypeStruct((n, D_MODEL), F32),
        compiler_params=_cparams(("parallel", "arbitrary")),
        name="peer_experts",
    )(xb, u, gm, v)


def _token_mixer(x, mem, w_in, diff_lambda, diff_subln, cmp_pe, cmp_w1, cmp_w2, w_mem_kv, w_branch, rel_bias):
    b, t, _ = x.shape
    n = b * t
    L = ATT_TILE

    w_a, w_mg, w_g = _w_prep(w_in.T)
    ha, hg, xb = _in_proj(x.reshape(n, D_MODEL), w_a, w_g)
    ha3 = ha.reshape(b, t, ha.shape[1])

    tab1d = rel_bias[_rel_bucket(jnp.arange(t))].T * LOG2E
    n_far = min(t // L, REL_MAX_DIST // L + 2)
    bias_a = _bias_tiles(tab1d[:A_HEADS], t, -1, n_far + 1, t, 1)
    bias_b = tab1d[A_HEADS:]
    bias_sel = _bias_tiles(bias_b, t, -1, n_far + 1, t, B_HPG)
    n_win = min(t // L, WINDOW // L + 1)
    bias_win = _bias_tiles(bias_b, t, -1, n_win + 1, WINDOW, B_HPG)

    def values_t(v):
        heads, dv = v.shape[2:]
        return v.reshape(b, t // L, L, heads, dv).transpose(0, 3, 1, 4, 2)

    o_a = _diff_attention(ha3, values_t(ha3[:, :, 3072:4096].reshape(b, t, A_HEADS, A_DV)), bias_a, diff_lambda,
                          jnp.broadcast_to(diff_subln[:, None], (A_DV, LANES)), b, t)

    glog_t = hg[:, :48].reshape(b, t, 3, B_GROUPS, B_HPG).transpose(2, 0, 3, 4, 1)
    nr = t // CMP_STRIDE
    w1 = cmp_w1.reshape(2, CMP_LEN, B_DH, CMP_HIDDEN)
    z1 = jnp.zeros_like(w1[0])
    w1_bd = jnp.concatenate([jnp.concatenate([w1[0], z1], axis=2), jnp.concatenate([z1, w1[1]], axis=2)],
                            axis=1).astype(BF16)
    z2 = jnp.zeros_like(cmp_w2[0])
    w2_bd = jnp.concatenate([jnp.concatenate([cmp_w2[0], z2], axis=1), jnp.concatenate([z2, cmp_w2[1]], axis=1)],
                            axis=0).astype(BF16)
    pe_bd = jnp.broadcast_to(jnp.concatenate([cmp_pe[0], cmp_pe[1]], axis=1)[:, None, :],
                             (CMP_LEN, 8, 2 * B_DH)).astype(BF16)
    kvc = _compress(ha3[:, :, 5120:5632].astype(F32), w1_bd, w2_bd, pe_bd)
    n_c = (t - CMP_LEN) // CMP_STRIDE + 1
    n_sel = t // SEL_BLOCK
    cidx = np.arange(nr)[:, None] * CMP_STRIDE + np.arange(CMP_LEN)[None, :]
    overlap = (cidx[:, :, None] // SEL_BLOCK == np.arange(n_sel)[None, None, :]).astype(np.float32).mean(axis=1)
    overlap[n_c:] = 0.0
    overlap_t = np.zeros((LANES, nr), np.float32)
    overlap_t[:n_sel] = overlap.T
    v_sel = ha3[:, :, 5632:6144].reshape(b, t, B_GROUPS, 2, B_DH)[:, :, :, 1]
    v_win = ha3[:, :, 6144:6656].reshape(b, t, B_GROUPS, 2, B_DH)[:, :, :, 1]
    assert n_sel <= B_DH
    hot = np.zeros((t, LANES), np.float32)
    hot[np.arange(t), B_DH + np.arange(t) // SEL_BLOCK] = 1.0
    o_b = _nsa_attention(ha3, kvc, jnp.asarray(overlap_t, BF16), glog_t, values_t(v_sel), bias_sel,
                         jnp.asarray(hot, BF16), values_t(v_win), bias_win, b, t, n_c, min(SEL_TOPN, n_sel))

    n_mem = mem.shape[1]
    memkv = _matmul(mem.reshape(b * n_mem, D_MODEL).astype(BF16), w_mem_kv.astype(BF16), BF16,
                    b * n_mem, 1024, "mem_kv_proj")
    o_m = _memory_attention(ha3, memkv, b, t, n_mem)

    flat = lambda o: o.reshape(n, MIX_W)
    return _merge(xb, w_mg, flat(o_a), flat(o_b), flat(o_m), w_branch.astype(BF16))


def _peer(x1b, q, peer_keys, peer_u, peer_v):
    n = x1b.shape[0]
    zk = jnp.zeros((PEER_HEADS, PEER_NKEYS, PEER_DKEY // 2), F32)
    keys_bd = jnp.concatenate([jnp.concatenate([peer_keys[:, 0], zk], axis=2),
                               jnp.concatenate([zk, peer_keys[:, 1]], axis=2)], axis=1).astype(BF16)
    ei, ej, gate = _peer_route(q, keys_bd)
    slots = lambda a: a.reshape(PEER_HEADS * PEER_TOPK, n).T.reshape(n, 1, PEER_HEADS * PEER_TOPK)
    gm = _gate_matrix(slots(ei), slots(ej), slots(gate))
    return _peer_experts(x1b, peer_u, gm, peer_v)


def kernel(x, mem, w_in, diff_lambda, diff_subln, cmp_pe, cmp_w1, cmp_w2, w_mem_kv, w_branch, w_out, ln1_g, ln1_b,
           peer_wq, peer_keys, peer_u, peer_v, ln2_g, ln2_b, rel_bias):
    b, t, _ = x.shape
    n = b * t
    for l in range(DEPTH):
        mixed = _token_mixer(x, mem, w_in[l], diff_lambda[l], diff_subln[l], cmp_pe[l], cmp_w1[l], cmp_w2[l],
                             w_mem_kv[l], w_branch[l], rel_bias)
        x1, x1b, q = _out_proj_ln(mixed, w_out[l].astype(BF16), x.reshape(n, D_MODEL), ln1_g[l].reshape(1, D_MODEL),
                                  ln1_b[l].reshape(1, D_MODEL), peer_wq[l].astype(BF16))
        y = _peer(x1b, q, peer_keys[l], peer_u[l], peer_v[l])
        x = _residual_ln(x1, y, ln2_g[l].reshape(1, D_MODEL), ln2_b[l].reshape(1, D_MODEL)).reshape(b, t, D_MODEL)
    return x
```
